```python
import math
import jax, jax.numpy as jnp
from jax import lax
import numpy as np

D_MODEL = 2048
BATCH = 8
SEQ = 4096
DEPTH = 1

D_SSM = D_MODEL // 2
SSM_GROUP = 16
N_SSM_GROUPS = D_SSM // SSM_GROUP
SSM_STATE = 64
D_GMLP = D_MODEL - D_SSM
GMLP_HEAD = 128
N_GMLP_HEADS = D_GMLP // GMLP_HEAD
CHUNK = 128
D_IN = D_SSM + 2 * D_GMLP
D_FF = 5632
D_PLE = 256
EPS = 1e-6
DT_MIN = 1e-3
DT_MAX = 1e-1

kernel_name = "hybrid_s5_gmlp_macaron_ple"


def rms_norm(x, g):
    xf = x.astype(jnp.float32)
    y = xf * lax.rsqrt(jnp.mean(xf * xf, axis=-1, keepdims=True) + EPS) * g.astype(jnp.float32)
    return y.astype(x.dtype)


def layer_norm(x, g):
    xf = x.astype(jnp.float32)
    xc = xf - jnp.mean(xf, axis=-1, keepdims=True)
    y = xc * lax.rsqrt(jnp.mean(xc * xc, axis=-1, keepdims=True) + EPS) * g.astype(jnp.float32)
    return y.astype(x.dtype)


def swiglu(x, w_gate, w_up, w_down):
    return (jax.nn.silu(x @ w_gate) * (x @ w_up)) @ w_down


def _complex_affine_combine(e1, e2):
    a1r, a1i, b1r, b1i = e1
    a2r, a2i, b2r, b2i = e2
    return (a2r * a1r - a2i * a1i,
            a2r * a1i + a2i * a1r,
            a2r * b1r - a2i * b1i + b2r,
            a2r * b1i + a2i * b1r + b2i)


def s5_mixer(u, log_dt, a_re, a_im, b_re, b_im, c_re, c_im, d, w_glu):
    bsz, seqlen, _ = u.shape
    f32 = jnp.float32
    uf = u.astype(f32).reshape(bsz, seqlen, N_SSM_GROUPS, SSM_GROUP)
    dt = jnp.exp(log_dt.astype(f32))[:, None]
    lr = jnp.minimum(a_re.astype(f32), -1e-4)
    li = a_im.astype(f32)
    mag = jnp.exp(lr * dt)
    ang = li * dt
    abar_r = mag * jnp.cos(ang)
    abar_i = mag * jnp.sin(ang)
    den = lr * lr + li * li
    xr = abar_r - 1.0
    xi = abar_i
    zr = (xr * lr + xi * li) / den
    zi = (xi * lr - xr * li) / den
    br = b_re.astype(f32)
    bi = b_im.astype(f32)
    bbar_r = zr[..., None] * br - zi[..., None] * bi
    bbar_i = zr[..., None] * bi + zi[..., None] * br
    drive_r = jnp.einsum("blgp,gnp->lbgn", uf, bbar_r)
    drive_i = jnp.einsum("blgp,gnp->lbgn", uf, bbar_i)
    ar = jnp.broadcast_to(abar_r[None, None], (seqlen, 1, N_SSM_GROUPS, SSM_STATE))
    ai = jnp.broadcast_to(abar_i[None, None], (seqlen, 1, N_SSM_GROUPS, SSM_STATE))
    _, _, sr, si = lax.associative_scan(_complex_affine_combine, (ar, ai, drive_r, drive_i), axis=0)
    y = (jnp.einsum("lbgn,gpn->blgp", sr, c_re.astype(f32))
         - jnp.einsum("lbgn,gpn->blgp", si, c_im.astype(f32)))
    y = y + d.astype(f32).reshape(N_SSM_GROUPS, SSM_GROUP) * uf
    y = jax.nn.gelu(y.reshape(bsz, seqlen, D_SSM))
    y = y * jax.nn.sigmoid(y @ w_glu.astype(f32))
    return y.astype(u.dtype)


def gmlp_mixer(z_u, z_v, norm_v, w_s, b_s):
    bsz, seqlen, _ = z_u.shape
    n_chunks = seqlen // CHUNK
    u = jax.nn.gelu(z_u)
    v = layer_norm(jax.nn.gelu(z_v), norm_v)
    causal = jnp.tril(jnp.ones((CHUNK, CHUNK), dtype=bool))
    w = jnp.where(causal[None], w_s, jnp.zeros_like(w_s))
    vc = v.reshape(bsz, n_chunks, CHUNK, N_GMLP_HEADS, GMLP_HEAD)
    s = jnp.einsum("hts,bcshp->bcthp", w, vc) + b_s.T[None, None, :, :, None]
    out = u.reshape(bsz, n_chunks, CHUNK, N_GMLP_HEADS, GMLP_HEAD) * s
    return out.reshape(bsz, seqlen, D_GMLP)


def _fwd_setup_inputs(seed: int = 0) -> dict:
    key = jax.random.key(seed)
    ks = jax.random.split(key, 32)
    f32 = jnp.float32

    def nrm(k, shape, std):
        return (jax.random.normal(k, shape, f32) * std).astype(f32)

    def gain(k, shape):
        return 1.0 + nrm(k, shape, 0.02)

    L = DEPTH
    x = nrm(ks[0], (BATCH, SEQ, D_MODEL), 1.0)
    p = nrm(ks[1], (L, BATCH, SEQ, D_PLE), 1.0)
    n_idx = jnp.arange(SSM_STATE, dtype=f32)
    return {
        "x": x,
        "p": p,
        "norm_ffn1": gain(ks[2], (L, D_MODEL)),
        "w1_gate": nrm(ks[3], (L, D_MODEL, D_FF), D_MODEL ** -0.5),
        "w1_up": nrm(ks[4], (L, D_MODEL, D_FF), D_MODEL ** -0.5),
        "w1_down": nrm(ks[5], (L, D_FF, D_MODEL), D_FF ** -0.5),
        "norm_mix": gain(ks[6], (L, D_MODEL)),
        "w_in": nrm(ks[7], (L, D_MODEL, D_IN), D_MODEL ** -0.5),
        "ssm_log_dt": jax.random.uniform(ks[8], (L, N_SSM_GROUPS), f32, math.log(DT_MIN), math.log(DT_MAX)),
        "ssm_a_re": -0.5 + nrm(ks[9], (L, N_SSM_GROUPS, SSM_STATE), 0.01),
        "ssm_a_im": math.pi * n_idx[None, None, :] + nrm(ks[10], (L, N_SSM_GROUPS, SSM_STATE), 0.01),
        "ssm_b_re": nrm(ks[11], (L, N_SSM_GROUPS, SSM_STATE, SSM_GROUP), (2 * SSM_GROUP) ** -0.5),
        "ssm_b_im": nrm(ks[12], (L, N_SSM_GROUPS, SSM_STATE, SSM_GROUP), (2 * SSM_GROUP) ** -0.5),
        "ssm_c_re": nrm(ks[13], (L, N_SSM_GROUPS, SSM_GROUP, SSM_STATE), 0.5 ** 0.5),
        "ssm_c_im": nrm(ks[14], (L, N_SSM_GROUPS, SSM_GROUP, SSM_STATE), 0.5 ** 0.5),
        "ssm_d": nrm(ks[15], (L, D_SSM), 1.0),
        "ssm_w_glu": nrm(ks[16], (L, D_SSM, D_SSM), D_SSM ** -0.5),
        "gmlp_norm_v": gain(ks[17], (L, D_GMLP)),
        "gmlp_w_s": nrm(ks[18], (L, N_GMLP_HEADS, CHUNK, CHUNK), CHUNK ** -0.5),
        "gmlp_b_s": 1.0 + nrm(ks[19], (L, N_GMLP_HEADS, CHUNK), 0.01),
        "norm_ssm_out": gain(ks[20], (L, D_SSM)),
        "norm_gmlp_out": gain(ks[21], (L, D_GMLP)),
        "w_out": nrm(ks[22], (L, D_MODEL, D_MODEL), D_MODEL ** -0.5),
        "norm_ffn2": gain(ks[23], (L, D_MODEL)),
        "w2_gate": nrm(ks[24], (L, D_MODEL, D_FF), D_MODEL ** -0.5),
        "w2_up": nrm(ks[25], (L, D_MODEL, D_FF), D_MODEL ** -0.5),
        "w2_down": nrm(ks[26], (L, D_FF, D_MODEL), D_FF ** -0.5),
        "norm_ple": gain(ks[27], (L, D_MODEL)),
        "w_ple_gate": nrm(ks[28], (L, D_MODEL, D_MODEL), D_MODEL ** -0.5),
        "w_ple_proj": nrm(ks[29], (L, D_PLE, D_MODEL), D_PLE ** -0.5),
        "norm_final": gain(ks[30], (D_MODEL,)),
    }


def _fwd_reference(x, p, norm_ffn1, w1_gate, w1_up, w1_down, norm_mix, w_in,
              ssm_log_dt, ssm_a_re, ssm_a_im, ssm_b_re, ssm_b_im, ssm_c_re, ssm_c_im,
              ssm_d, ssm_w_glu, gmlp_norm_v, gmlp_w_s, gmlp_b_s,
              norm_ssm_out, norm_gmlp_out, w_out, norm_ffn2, w2_gate, w2_up, w2_down,
              norm_ple, w_ple_gate, w_ple_proj, norm_final):
    h = x
    for i in range(DEPTH):
        h = h + 0.5 * swiglu(rms_norm(h, norm_ffn1[i]), w1_gate[i], w1_up[i], w1_down[i])
        z = rms_norm(h, norm_mix[i]) @ w_in[i]
        z_ssm = z[..., :D_SSM]
        z_u = z[..., D_SSM:D_SSM + D_GMLP]
        z_v = z[..., D_SSM + D_GMLP:]
        y_ssm = s5_mixer(z_ssm, ssm_log_dt[i], ssm_a_re[i], ssm_a_im[i], ssm_b_re[i], ssm_b_im[i],
                         ssm_c_re[i], ssm_c_im[i], ssm_d[i], ssm_w_glu[i])
        y_gmlp = gmlp_mixer(z_u, z_v, gmlp_norm_v[i], gmlp_w_s[i], gmlp_b_s[i])
        y = jnp.concatenate([rms_norm(y_ssm, norm_ssm_out[i]), rms_norm(y_gmlp, norm_gmlp_out[i])], axis=-1)
        h = h + y @ w_out[i]
        h = h + 0.5 * swiglu(rms_norm(h, norm_ffn2[i]), w2_gate[i], w2_up[i], w2_down[i])
        gate = jax.nn.sigmoid(rms_norm(h, norm_ple[i]) @ w_ple_gate[i])
        h = h + gate * (p[i] @ w_ple_proj[i])
    return rms_norm(h, norm_final)


import jax as _jax
import jax.numpy as _jnp

TWIN_FORMAT = 'train_step'
FWD_PARAMS = ['x', 'p', 'norm_ffn1', 'w1_gate', 'w1_up', 'w1_down', 'norm_mix', 'w_in', 'ssm_log_dt', 'ssm_a_re', 'ssm_a_im', 'ssm_b_re', 'ssm_b_im', 'ssm_c_re', 'ssm_c_im', 'ssm_d', 'ssm_w_glu', 'gmlp_norm_v', 'gmlp_w_s', 'gmlp_b_s', 'norm_ssm_out', 'norm_gmlp_out', 'w_out', 'norm_ffn2', 'w2_gate', 'w2_up', 'w2_down', 'norm_ple', 'w_ple_gate', 'w_ple_proj', 'norm_final']
TWIN_WEIGHTS = ['norm_ffn1', 'w1_gate', 'w1_up', 'w1_down', 'norm_mix', 'w_in', 'ssm_log_dt', 'ssm_a_re', 'ssm_a_im', 'ssm_b_re', 'ssm_b_im', 'ssm_c_re', 'ssm_c_im', 'ssm_d', 'ssm_w_glu', 'gmlp_norm_v', 'gmlp_w_s', 'gmlp_b_s', 'norm_ssm_out', 'norm_gmlp_out', 'w_out', 'norm_ffn2', 'w2_gate', 'w2_up', 'w2_down', 'norm_ple', 'w_ple_gate', 'w_ple_proj', 'norm_final']
TWIN_DIFF_INPUT = 'x'
TWIN_INPUTS = ['x', 'p', 'norm_ffn1', 'w1_gate', 'w1_up', 'w1_down', 'norm_mix', 'w_in', 'ssm_log_dt', 'ssm_a_re', 'ssm_a_im', 'ssm_b_re', 'ssm_b_im', 'ssm_c_re', 'ssm_c_im', 'ssm_d', 'ssm_w_glu', 'gmlp_norm_v', 'gmlp_w_s', 'gmlp_b_s', 'norm_ssm_out', 'norm_gmlp_out', 'w_out', 'norm_ffn2', 'w2_gate', 'w2_up', 'w2_down', 'norm_ple', 'w_ple_gate', 'w_ple_proj', 'norm_final', 'loss_target', 'm_norm_ffn1', 'm_w1_gate', 'm_w1_up', 'm_w1_down', 'm_norm_mix', 'm_w_in', 'm_ssm_log_dt', 'm_ssm_a_re', 'm_ssm_a_im', 'm_ssm_b_re', 'm_ssm_b_im', 'm_ssm_c_re', 'm_ssm_c_im', 'm_ssm_d', 'm_ssm_w_glu', 'm_gmlp_norm_v', 'm_gmlp_w_s', 'm_gmlp_b_s', 'm_norm_ssm_out', 'm_norm_gmlp_out', 'm_w_out', 'm_norm_ffn2', 'm_w2_gate', 'm_w2_up', 'm_w2_down', 'm_norm_ple', 'm_w_ple_gate', 'm_w_ple_proj', 'm_norm_final', 'v_norm_ffn1', 'v_w1_gate', 'v_w1_up', 'v_w1_down', 'v_norm_mix', 'v_w_in', 'v_ssm_log_dt', 'v_ssm_a_re', 'v_ssm_a_im', 'v_ssm_b_re', 'v_ssm_b_im', 'v_ssm_c_re', 'v_ssm_c_im', 'v_ssm_d', 'v_ssm_w_glu', 'v_gmlp_norm_v', 'v_gmlp_w_s', 'v_gmlp_b_s', 'v_norm_ssm_out', 'v_norm_gmlp_out', 'v_w_out', 'v_norm_ffn2', 'v_w2_gate', 'v_w2_up', 'v_w2_down', 'v_norm_ple', 'v_w_ple_gate', 'v_w_ple_proj', 'v_norm_final']
TWIN_OUTPUTS = ['loss', 'grad_x', 'grad_norm_ffn1', 'grad_w1_gate', 'grad_w1_up', 'grad_w1_down', 'grad_norm_mix', 'grad_w_in', 'grad_ssm_log_dt', 'grad_ssm_a_re', 'grad_ssm_a_im', 'grad_ssm_b_re', 'grad_ssm_b_im', 'grad_ssm_c_re', 'grad_ssm_c_im', 'grad_ssm_d', 'grad_ssm_w_glu', 'grad_gmlp_norm_v', 'grad_gmlp_w_s', 'grad_gmlp_b_s', 'grad_norm_ssm_out', 'grad_norm_gmlp_out', 'grad_w_out', 'grad_norm_ffn2', 'grad_w2_gate', 'grad_w2_up', 'grad_w2_down', 'grad_norm_ple', 'grad_w_ple_gate', 'grad_w_ple_proj', 'grad_norm_final', 'delta_norm_ffn1', 'delta_w1_gate', 'delta_w1_up', 'delta_w1_down', 'delta_norm_mix', 'delta_w_in', 'delta_ssm_log_dt', 'delta_ssm_a_re', 'delta_ssm_a_im', 'delta_ssm_b_re', 'delta_ssm_b_im', 'delta_ssm_c_re', 'delta_ssm_c_im', 'delta_ssm_d', 'delta_ssm_w_glu', 'delta_gmlp_norm_v', 'delta_gmlp_w_s', 'delta_gmlp_b_s', 'delta_norm_ssm_out', 'delta_norm_gmlp_out', 'delta_w_out', 'delta_norm_ffn2', 'delta_w2_gate', 'delta_w2_up', 'delta_w2_down', 'delta_norm_ple', 'delta_w_ple_gate', 'delta_w_ple_proj', 'delta_norm_final', 'new_m_norm_ffn1', 'new_m_w1_gate', 'new_m_w1_up', 'new_m_w1_down', 'new_m_norm_mix', 'new_m_w_in', 'new_m_ssm_log_dt', 'new_m_ssm_a_re', 'new_m_ssm_a_im', 'new_m_ssm_b_re', 'new_m_ssm_b_im', 'new_m_ssm_c_re', 'new_m_ssm_c_im', 'new_m_ssm_d', 'new_m_ssm_w_glu', 'new_m_gmlp_norm_v', 'new_m_gmlp_w_s', 'new_m_gmlp_b_s', 'new_m_norm_ssm_out', 'new_m_norm_gmlp_out', 'new_m_w_out', 'new_m_norm_ffn2', 'new_m_w2_gate', 'new_m_w2_up', 'new_m_w2_down', 'new_m_norm_ple', 'new_m_w_ple_gate', 'new_m_w_ple_proj', 'new_m_norm_final', 'new_v_norm_ffn1', 'new_v_w1_gate', 'new_v_w1_up', 'new_v_w1_down', 'new_v_norm_mix', 'new_v_w_in', 'new_v_ssm_log_dt', 'new_v_ssm_a_re', 'new_v_ssm_a_im', 'new_v_ssm_b_re', 'new_v_ssm_b_im', 'new_v_ssm_c_re', 'new_v_ssm_c_im', 'new_v_ssm_d', 'new_v_ssm_w_glu', 'new_v_gmlp_norm_v', 'new_v_gmlp_w_s', 'new_v_gmlp_b_s', 'new_v_norm_ssm_out', 'new_v_norm_gmlp_out', 'new_v_w_out', 'new_v_norm_ffn2', 'new_v_w2_gate', 'new_v_w2_up', 'new_v_w2_down', 'new_v_norm_ple', 'new_v_w_ple_gate', 'new_v_w_ple_proj', 'new_v_norm_final']
TWIN_LEAF_KINDS = {'loss': 'loss', 'grad_x': 'grad_x', 'grad_norm_ffn1': 'grad_w', 'grad_w1_gate': 'grad_w', 'grad_w1_up': 'grad_w', 'grad_w1_down': 'grad_w', 'grad_norm_mix': 'grad_w', 'grad_w_in': 'grad_w', 'grad_ssm_log_dt': 'grad_w', 'grad_ssm_a_re': 'grad_w', 'grad_ssm_a_im': 'grad_w', 'grad_ssm_b_re': 'grad_w', 'grad_ssm_b_im': 'grad_w', 'grad_ssm_c_re': 'grad_w', 'grad_ssm_c_im': 'grad_w', 'grad_ssm_d': 'grad_w', 'grad_ssm_w_glu': 'grad_w', 'grad_gmlp_norm_v': 'grad_w', 'grad_gmlp_w_s': 'grad_w', 'grad_gmlp_b_s': 'grad_w', 'grad_norm_ssm_out': 'grad_w', 'grad_norm_gmlp_out': 'grad_w', 'grad_w_out': 'grad_w', 'grad_norm_ffn2': 'grad_w', 'grad_w2_gate': 'grad_w', 'grad_w2_up': 'grad_w', 'grad_w2_down': 'grad_w', 'grad_norm_ple': 'grad_w', 'grad_w_ple_gate': 'grad_w', 'grad_w_ple_proj': 'grad_w', 'grad_norm_final': 'grad_w', 'delta_norm_ffn1': 'delta_w', 'delta_w1_gate': 'delta_w', 'delta_w1_up': 'delta_w', 'delta_w1_down': 'delta_w', 'delta_norm_mix': 'delta_w', 'delta_w_in': 'delta_w', 'delta_ssm_log_dt': 'delta_w', 'delta_ssm_a_re': 'delta_w', 'delta_ssm_a_im': 'delta_w', 'delta_ssm_b_re': 'delta_w', 'delta_ssm_b_im': 'delta_w', 'delta_ssm_c_re': 'delta_w', 'delta_ssm_c_im': 'delta_w', 'delta_ssm_d': 'delta_w', 'delta_ssm_w_glu': 'delta_w', 'delta_gmlp_norm_v': 'delta_w', 'delta_gmlp_w_s': 'delta_w', 'delta_gmlp_b_s': 'delta_w', 'delta_norm_ssm_out': 'delta_w', 'delta_norm_gmlp_out': 'delta_w', 'delta_w_out': 'delta_w', 'delta_norm_ffn2': 'delta_w', 'delta_w2_gate': 'delta_w', 'delta_w2_up': 'delta_w', 'delta_w2_down': 'delta_w', 'delta_norm_ple': 'delta_w', 'delta_w_ple_gate': 'delta_w', 'delta_w_ple_proj': 'delta_w', 'delta_norm_final': 'delta_w', 'new_m_norm_ffn1': 'new_m', 'new_m_w1_gate': 'new_m', 'new_m_w1_up': 'new_m', 'new_m_w1_down': 'new_m', 'new_m_norm_mix': 'new_m', 'new_m_w_in': 'new_m', 'new_m_ssm_log_dt': 'new_m', 'new_m_ssm_a_re': 'new_m', 'new_m_ssm_a_im': 'new_m', 'new_m_ssm_b_re': 'new_m', 'new_m_ssm_b_im': 'new_m', 'new_m_ssm_c_re': 'new_m', 'new_m_ssm_c_im': 'new_m', 'new_m_ssm_d': 'new_m', 'new_m_ssm_w_glu': 'new_m', 'new_m_gmlp_norm_v': 'new_m', 'new_m_gmlp_w_s': 'new_m', 'new_m_gmlp_b_s': 'new_m', 'new_m_norm_ssm_out': 'new_m', 'new_m_norm_gmlp_out': 'new_m', 'new_m_w_out': 'new_m', 'new_m_norm_ffn2': 'new_m', 'new_m_w2_gate': 'new_m', 'new_m_w2_up': 'new_m', 'new_m_w2_down': 'new_m', 'new_m_norm_ple': 'new_m', 'new_m_w_ple_gate': 'new_m', 'new_m_w_ple_proj': 'new_m', 'new_m_norm_final': 'new_m', 'new_v_norm_ffn1': 'new_v', 'new_v_w1_gate': 'new_v', 'new_v_w1_up': 'new_v', 'new_v_w1_down': 'new_v', 'new_v_norm_mix': 'new_v', 'new_v_w_in': 'new_v', 'new_v_ssm_log_dt': 'new_v', 'new_v_ssm_a_re': 'new_v', 'new_v_ssm_a_im': 'new_v', 'new_v_ssm_b_re': 'new_v', 'new_v_ssm_b_im': 'new_v', 'new_v_ssm_c_re': 'new_v', 'new_v_ssm_c_im': 'new_v', 'new_v_ssm_d': 'new_v', 'new_v_ssm_w_glu': 'new_v', 'new_v_gmlp_norm_v': 'new_v', 'new_v_gmlp_w_s': 'new_v', 'new_v_gmlp_b_s': 'new_v', 'new_v_norm_ssm_out': 'new_v', 'new_v_norm_gmlp_out': 'new_v', 'new_v_w_out': 'new_v', 'new_v_norm_ffn2': 'new_v', 'new_v_w2_gate': 'new_v', 'new_v_w2_up': 'new_v', 'new_v_w2_down': 'new_v', 'new_v_norm_ple': 'new_v', 'new_v_w_ple_gate': 'new_v', 'new_v_w_ple_proj': 'new_v', 'new_v_norm_final': 'new_v'}


def _forward(args):
    return _fwd_reference(*[args[k] for k in FWD_PARAMS])


def _output_shape():
    def fwd():
        inp = _fwd_setup_inputs(0)
        return _fwd_reference(*[inp[k] for k in FWD_PARAMS])
    out = _jax.eval_shape(fwd)
    return out.shape, out.dtype

N_MICROBATCH = 1
ADAM_LR = 0.001
ADAM_B1 = 0.9
ADAM_B2 = 0.999
ADAM_EPS = 1e-08
ADAM_WD = 0.01
ADAM_STEP = 10
PER_EXAMPLE_BATCH_AXIS = {'x': 0, 'p': 1, 'loss_target': 0}
SHARED_INPUTS = []
_WEIGHT_DTYPES = {'norm_ffn1': _jnp.float32, 'w1_gate': _jnp.float32, 'w1_up': _jnp.float32, 'w1_down': _jnp.float32, 'norm_mix': _jnp.float32, 'w_in': _jnp.float32, 'ssm_log_dt': _jnp.float32, 'ssm_a_re': _jnp.float32, 'ssm_a_im': _jnp.float32, 'ssm_b_re': _jnp.float32, 'ssm_b_im': _jnp.float32, 'ssm_c_re': _jnp.float32, 'ssm_c_im': _jnp.float32, 'ssm_d': _jnp.float32, 'ssm_w_glu': _jnp.float32, 'gmlp_norm_v': _jnp.float32, 'gmlp_w_s': _jnp.float32, 'gmlp_b_s': _jnp.float32, 'norm_ssm_out': _jnp.float32, 'norm_gmlp_out': _jnp.float32, 'w_out': _jnp.float32, 'norm_ffn2': _jnp.float32, 'w2_gate': _jnp.float32, 'w2_up': _jnp.float32, 'w2_down': _jnp.float32, 'norm_ple': _jnp.float32, 'w_ple_gate': _jnp.float32, 'w_ple_proj': _jnp.float32, 'norm_final': _jnp.float32}
MOMENT_SCALE = {'norm_ffn1': 4.200988e-02, 'w1_gate': 1.649892e-02, 'w1_up': 1.598160e-02, 'w1_down': 2.652780e-02, 'norm_mix': 7.037091e-02, 'w_in': 5.621243e-02, 'ssm_log_dt': 1.102690e+01, 'ssm_a_re': 2.355159e-02, 'ssm_a_im': 2.100839e-02, 'ssm_b_re': 1.490667e-02, 'ssm_b_im': 1.526739e-02, 'ssm_c_re': 3.780078e-03, 'ssm_c_im': 3.712400e-03, 'ssm_d': 6.601826e-02, 'ssm_w_glu': 2.049125e-02, 'gmlp_norm_v': 3.484040e-02, 'gmlp_w_s': 3.331347e-02, 'gmlp_b_s': 4.973516e-02, 'norm_ssm_out': 6.917026e-02, 'norm_gmlp_out': 6.643795e-02, 'w_out': 6.815781e-02, 'norm_ffn2': 2.474500e-02, 'w2_gate': 1.061558e-02, 'w2_up': 1.043064e-02, 'w2_down': 1.718318e-02, 'norm_ple': 1.327616e-02, 'w_ple_gate': 1.296770e-02, 'w_ple_proj': 3.077414e-02, 'norm_final': 1.607230e+01}


def _to_microbatches(a, axis):
    t = _jnp.moveaxis(a, axis, 0)
    t = t.reshape((N_MICROBATCH, t.shape[0] // N_MICROBATCH) + t.shape[1:])
    return _jnp.moveaxis(t, 1, axis + 1)


def setup_inputs(seed: int = 0) -> dict:
    inp = _fwd_setup_inputs(seed)
    key = _jax.random.fold_in(_jax.random.key(seed), 7919)
    shape, _ = _output_shape()
    out = dict(inp)
    out["loss_target"] = _jax.random.normal(_jax.random.fold_in(key, 0), shape, _jnp.float32)
    for i, name in enumerate(TWIN_WEIGHTS):
        w = inp[name].astype(_jnp.float32)
        if MOMENT_SCALE is None:
            s = _jnp.sqrt(_jnp.mean(_jnp.square(w)) + 1e-30)
        else:
            s = MOMENT_SCALE[name]
        km, kv = _jax.random.split(_jax.random.fold_in(key, i + 1))
        out[name] = w
        out["m_" + name] = s * _jax.random.normal(km, w.shape, _jnp.float32)
        out["v_" + name] = (s * s) * _jax.random.uniform(kv, w.shape, _jnp.float32, 0.5, 1.5)
    if N_MICROBATCH > 1:
        for name, axis in PER_EXAMPLE_BATCH_AXIS.items():
            out[name] = _to_microbatches(out[name], axis)
    return {'x': out['x'], 'p': out['p'], 'norm_ffn1': out['norm_ffn1'], 'w1_gate': out['w1_gate'], 'w1_up': out['w1_up'], 'w1_down': out['w1_down'], 'norm_mix': out['norm_mix'], 'w_in': out['w_in'], 'ssm_log_dt': out['ssm_log_dt'], 'ssm_a_re': out['ssm_a_re'], 'ssm_a_im': out['ssm_a_im'], 'ssm_b_re': out['ssm_b_re'], 'ssm_b_im': out['ssm_b_im'], 'ssm_c_re': out['ssm_c_re'], 'ssm_c_im': out['ssm_c_im'], 'ssm_d': out['ssm_d'], 'ssm_w_glu': out['ssm_w_glu'], 'gmlp_norm_v': out['gmlp_norm_v'], 'gmlp_w_s': out['gmlp_w_s'], 'gmlp_b_s': out['gmlp_b_s'], 'norm_ssm_out': out['norm_ssm_out'], 'norm_gmlp_out': out['norm_gmlp_out'], 'w_out': out['w_out'], 'norm_ffn2': out['norm_ffn2'], 'w2_gate': out['w2_gate'], 'w2_up': out['w2_up'], 'w2_down': out['w2_down'], 'norm_ple': out['norm_ple'], 'w_ple_gate': out['w_ple_gate'], 'w_ple_proj': out['w_ple_proj'], 'norm_final': out['norm_final'], 'loss_target': out['loss_target'], 'm_norm_ffn1': out['m_norm_ffn1'], 'm_w1_gate': out['m_w1_gate'], 'm_w1_up': out['m_w1_up'], 'm_w1_down': out['m_w1_down'], 'm_norm_mix': out['m_norm_mix'], 'm_w_in': out['m_w_in'], 'm_ssm_log_dt': out['m_ssm_log_dt'], 'm_ssm_a_re': out['m_ssm_a_re'], 'm_ssm_a_im': out['m_ssm_a_im'], 'm_ssm_b_re': out['m_ssm_b_re'], 'm_ssm_b_im': out['m_ssm_b_im'], 'm_ssm_c_re': out['m_ssm_c_re'], 'm_ssm_c_im': out['m_ssm_c_im'], 'm_ssm_d': out['m_ssm_d'], 'm_ssm_w_glu': out['m_ssm_w_glu'], 'm_gmlp_norm_v': out['m_gmlp_norm_v'], 'm_gmlp_w_s': out['m_gmlp_w_s'], 'm_gmlp_b_s': out['m_gmlp_b_s'], 'm_norm_ssm_out': out['m_norm_ssm_out'], 'm_norm_gmlp_out': out['m_norm_gmlp_out'], 'm_w_out': out['m_w_out'], 'm_norm_ffn2': out['m_norm_ffn2'], 'm_w2_gate': out['m_w2_gate'], 'm_w2_up': out['m_w2_up'], 'm_w2_down': out['m_w2_down'], 'm_norm_ple': out['m_norm_ple'], 'm_w_ple_gate': out['m_w_ple_gate'], 'm_w_ple_proj': out['m_w_ple_proj'], 'm_norm_final': out['m_norm_final'], 'v_norm_ffn1': out['v_norm_ffn1'], 'v_w1_gate': out['v_w1_gate'], 'v_w1_up': out['v_w1_up'], 'v_w1_down': out['v_w1_down'], 'v_norm_mix': out['v_norm_mix'], 'v_w_in': out['v_w_in'], 'v_ssm_log_dt': out['v_ssm_log_dt'], 'v_ssm_a_re': out['v_ssm_a_re'], 'v_ssm_a_im': out['v_ssm_a_im'], 'v_ssm_b_re': out['v_ssm_b_re'], 'v_ssm_b_im': out['v_ssm_b_im'], 'v_ssm_c_re': out['v_ssm_c_re'], 'v_ssm_c_im': out['v_ssm_c_im'], 'v_ssm_d': out['v_ssm_d'], 'v_ssm_w_glu': out['v_ssm_w_glu'], 'v_gmlp_norm_v': out['v_gmlp_norm_v'], 'v_gmlp_w_s': out['v_gmlp_w_s'], 'v_gmlp_b_s': out['v_gmlp_b_s'], 'v_norm_ssm_out': out['v_norm_ssm_out'], 'v_norm_gmlp_out': out['v_norm_gmlp_out'], 'v_w_out': out['v_w_out'], 'v_norm_ffn2': out['v_norm_ffn2'], 'v_w2_gate': out['v_w2_gate'], 'v_w2_up': out['v_w2_up'], 'v_w2_down': out['v_w2_down'], 'v_norm_ple': out['v_norm_ple'], 'v_w_ple_gate': out['v_w_ple_gate'], 'v_w_ple_proj': out['v_w_ple_proj'], 'v_norm_final': out['v_norm_final']}


def _loss(weights, diff, rest, loss_target):
    with _jax.named_scope("forward"):
        args = {**rest, TWIN_DIFF_INPUT: diff, **{k: w.astype(_WEIGHT_DTYPES[k]) for k, w in weights.items()}}
        y = _forward(args)
    with _jax.named_scope("loss_head"):
        err = _jnp.square(y.astype(_jnp.float32) - loss_target)
        return 0.5 * _jnp.sum(_jnp.mean(err, axis=-1)) if err.ndim else 0.5 * err


def _adamw(w, g, m, v):
    m = ADAM_B1 * m + (1.0 - ADAM_B1) * g
    v = ADAM_B2 * v + (1.0 - ADAM_B2) * _jnp.square(g)
    m_hat = m / (1.0 - ADAM_B1 ** ADAM_STEP)
    v_hat = v / (1.0 - ADAM_B2 ** ADAM_STEP)
    delta = -ADAM_LR * (m_hat / (_jnp.sqrt(v_hat) + ADAM_EPS) + ADAM_WD * w)
    return delta, m, v


def reference(x, p, norm_ffn1, w1_gate, w1_up, w1_down, norm_mix, w_in, ssm_log_dt, ssm_a_re, ssm_a_im, ssm_b_re, ssm_b_im, ssm_c_re, ssm_c_im, ssm_d, ssm_w_glu, gmlp_norm_v, gmlp_w_s, gmlp_b_s, norm_ssm_out, norm_gmlp_out, w_out, norm_ffn2, w2_gate, w2_up, w2_down, norm_ple, w_ple_gate, w_ple_proj, norm_final, loss_target, m_norm_ffn1, m_w1_gate, m_w1_up, m_w1_down, m_norm_mix, m_w_in, m_ssm_log_dt, m_ssm_a_re, m_ssm_a_im, m_ssm_b_re, m_ssm_b_im, m_ssm_c_re, m_ssm_c_im, m_ssm_d, m_ssm_w_glu, m_gmlp_norm_v, m_gmlp_w_s, m_gmlp_b_s, m_norm_ssm_out, m_norm_gmlp_out, m_w_out, m_norm_ffn2, m_w2_gate, m_w2_up, m_w2_down, m_norm_ple, m_w_ple_gate, m_w_ple_proj, m_norm_final, v_norm_ffn1, v_w1_gate, v_w1_up, v_w1_down, v_norm_mix, v_w_in, v_ssm_log_dt, v_ssm_a_re, v_ssm_a_im, v_ssm_b_re, v_ssm_b_im, v_ssm_c_re, v_ssm_c_im, v_ssm_d, v_ssm_w_glu, v_gmlp_norm_v, v_gmlp_w_s, v_gmlp_b_s, v_norm_ssm_out, v_norm_gmlp_out, v_w_out, v_norm_ffn2, v_w2_gate, v_w2_up, v_w2_down, v_norm_ple, v_w_ple_gate, v_w_ple_proj, v_norm_final):
    given = dict(x=x, p=p, norm_ffn1=norm_ffn1, w1_gate=w1_gate, w1_up=w1_up, w1_down=w1_down, norm_mix=norm_mix, w_in=w_in, ssm_log_dt=ssm_log_dt, ssm_a_re=ssm_a_re, ssm_a_im=ssm_a_im, ssm_b_re=ssm_b_re, ssm_b_im=ssm_b_im, ssm_c_re=ssm_c_re, ssm_c_im=ssm_c_im, ssm_d=ssm_d, ssm_w_glu=ssm_w_glu, gmlp_norm_v=gmlp_norm_v, gmlp_w_s=gmlp_w_s, gmlp_b_s=gmlp_b_s, norm_ssm_out=norm_ssm_out, norm_gmlp_out=norm_gmlp_out, w_out=w_out, norm_ffn2=norm_ffn2, w2_gate=w2_gate, w2_up=w2_up, w2_down=w2_down, norm_ple=norm_ple, w_ple_gate=w_ple_gate, w_ple_proj=w_ple_proj, norm_final=norm_final, loss_target=loss_target, m_norm_ffn1=m_norm_ffn1, m_w1_gate=m_w1_gate, m_w1_up=m_w1_up, m_w1_down=m_w1_down, m_norm_mix=m_norm_mix, m_w_in=m_w_in, m_ssm_log_dt=m_ssm_log_dt, m_ssm_a_re=m_ssm_a_re, m_ssm_a_im=m_ssm_a_im, m_ssm_b_re=m_ssm_b_re, m_ssm_b_im=m_ssm_b_im, m_ssm_c_re=m_ssm_c_re, m_ssm_c_im=m_ssm_c_im, m_ssm_d=m_ssm_d, m_ssm_w_glu=m_ssm_w_glu, m_gmlp_norm_v=m_gmlp_norm_v, m_gmlp_w_s=m_gmlp_w_s, m_gmlp_b_s=m_gmlp_b_s, m_norm_ssm_out=m_norm_ssm_out, m_norm_gmlp_out=m_norm_gmlp_out, m_w_out=m_w_out, m_norm_ffn2=m_norm_ffn2, m_w2_gate=m_w2_gate, m_w2_up=m_w2_up, m_w2_down=m_w2_down, m_norm_ple=m_norm_ple, m_w_ple_gate=m_w_ple_gate, m_w_ple_proj=m_w_ple_proj, m_norm_final=m_norm_final, v_norm_ffn1=v_norm_ffn1, v_w1_gate=v_w1_gate, v_w1_up=v_w1_up, v_w1_down=v_w1_down, v_norm_mix=v_norm_mix, v_w_in=v_w_in, v_ssm_log_dt=v_ssm_log_dt, v_ssm_a_re=v_ssm_a_re, v_ssm_a_im=v_ssm_a_im, v_ssm_b_re=v_ssm_b_re, v_ssm_b_im=v_ssm_b_im, v_ssm_c_re=v_ssm_c_re, v_ssm_c_im=v_ssm_c_im, v_ssm_d=v_ssm_d, v_ssm_w_glu=v_ssm_w_glu, v_gmlp_norm_v=v_gmlp_norm_v, v_gmlp_w_s=v_gmlp_w_s, v_gmlp_b_s=v_gmlp_b_s, v_norm_ssm_out=v_norm_ssm_out, v_norm_gmlp_out=v_norm_gmlp_out, v_w_out=v_w_out, v_norm_ffn2=v_norm_ffn2, v_w2_gate=v_w2_gate, v_w2_up=v_w2_up, v_w2_down=v_w2_down, v_norm_ple=v_norm_ple, v_w_ple_gate=v_w_ple_gate, v_w_ple_proj=v_w_ple_proj, v_norm_final=v_norm_final)
    weights = {n: given[n] for n in TWIN_WEIGHTS}
    shared = {n: given[n] for n in SHARED_INPUTS}
    per_example = {n: given[n] for n in ['x', 'p']}
    grad_fn = _jax.value_and_grad(_loss, argnums=(0, 1))

    def one_microbatch(ex, loss_target):
        ex = dict(ex)
        diff = ex.pop(TWIN_DIFF_INPUT)
        return grad_fn(weights, diff, {**shared, **ex}, loss_target)

    if N_MICROBATCH == 1:
        loss, (grad_w, grad_x) = one_microbatch(per_example, given["loss_target"])
    else:
        def body(carry, xs):
            loss_sum, grad_sum = carry
            l_k, (gw_k, gx_k) = one_microbatch(xs[0], xs[1])
            with _jax.named_scope("update"):
                return (loss_sum + l_k, _jax.tree.map(_jnp.add, grad_sum, gw_k)), gx_k

        init = (_jnp.zeros((), _jnp.float32), _jax.tree.map(_jnp.zeros_like, weights))
        (loss, grad_w), grad_x = _jax.lax.scan(body, init, (per_example, given["loss_target"]))
    with _jax.named_scope("update"):
        delta_w, new_m, new_v = {}, {}, {}
        for n in TWIN_WEIGHTS:
            delta_w[n], new_m[n], new_v[n] = _adamw(weights[n], grad_w[n], given["m_" + n], given["v_" + n])
    return (loss, grad_x, *[grad_w[n] for n in TWIN_WEIGHTS], *[delta_w[n] for n in TWIN_WEIGHTS],
            *[new_m[n] for n in TWIN_WEIGHTS], *[new_v[n] for n in TWIN_WEIGHTS])
```

```python
import functools
import math

import jax
import jax.numpy as jnp
from jax import lax
from jax.experimental import pallas as pl
from jax.experimental.pallas import tpu as pltpu

F32 = jnp.float32
BF16 = jnp.bfloat16
EPS = 1e-6
SSM_GROUP = 16
SSM_STATE = 64
GROUPS_PER_BLOCK = 8
GMLP_HEAD = 128
CHUNK = 128
ADAM_LR = 0.001
ADAM_B1 = 0.9
ADAM_B2 = 0.999
ADAM_EPS = 1e-08
ADAM_WD = 0.01
ADAM_STEP = 10
N_CHIPS = 4
N_DEV = 8
LANES = 128
VMEM_LIMIT_BYTES = 56 * 1024 * 1024
MESH = pl.DeviceIdType.MESH
GELU_C = math.sqrt(2.0 / math.pi)
GELU_A = 0.044715

_DOT_DIMS = {
    "nn": (((1,), (0,)), ((), ())),
    "nt": (((1,), (1,)), ((), ())),
    "tn": (((0,), (0,)), ((), ())),
}


def _tile(dim, pref, align):
    if dim <= pref:
        return dim
    t = (pref // align) * align
    while t >= align:
        if dim % t == 0:
            return t
        t -= align
    return dim


def _params(semantics):
    return pltpu.CompilerParams(dimension_semantics=semantics, vmem_limit_bytes=VMEM_LIMIT_BYTES)


def _gelu(x):
    return 0.5 * x * (1.0 + jnp.tanh(GELU_C * (x + GELU_A * x * x * x)))


def _gelu_grad(x):
    t = jnp.tanh(GELU_C * (x + GELU_A * x * x * x))
    return 0.5 * (1.0 + t) + 0.5 * x * (1.0 - t * t) * GELU_C * (1.0 + 3.0 * GELU_A * x * x)


def _sigmoid(x):
    return 1.0 / (1.0 + jnp.exp(-x))


def _dot(a, b, mode):
    return lax.dot_general(a.astype(BF16), b.astype(BF16), _DOT_DIMS[mode], preferred_element_type=F32)


def _matmul(name, mode, a_list, b_list, products, out_dtypes, epilogue, extras=(), tm=512, tn=512, tk=2048):
    a0, b0 = a_list[0], b_list[0]
    if mode == "tn":
        k_dim, m_dim = a0.shape
    else:
        m_dim, k_dim = a0.shape
    n_dim = b0.shape[0] if mode == "nt" else b0.shape[1]
    tm = _tile(m_dim, tm, LANES)
    tn = _tile(n_dim, tn, LANES)
    tk = _tile(k_dim, tk, LANES)
    nk = k_dim // tk
    n_acc = 1 + max(p[2] for p in products)
    na, nb, ne, no = len(a_list), len(b_list), len(extras), len(out_dtypes)

    if mode == "tn":
        a_spec = pl.BlockSpec((tk, tm), lambda i, j, k: (k, i))
    else:
        a_spec = pl.BlockSpec((tm, tk), lambda i, j, k: (i, k))
    if mode == "nt":
        b_spec = pl.BlockSpec((tn, tk), lambda i, j, k: (j, k))
    else:
        b_spec = pl.BlockSpec((tk, tn), lambda i, j, k: (k, j))
    t_spec = pl.BlockSpec((tm, tn), lambda i, j, k: (i, j))

    def body(*refs):
        a_refs = refs[:na]
        b_refs = refs[na:na + nb]
        e_refs = refs[na + nb:na + nb + ne]
        o_refs = refs[na + nb + ne:na + nb + ne + no]
        acc_refs = refs[na + nb + ne + no:]

        sums = [None] * n_acc
        for ai, bi, ci in products:
            d = _dot(a_refs[ai][...], b_refs[bi][...], mode)
            sums[ci] = d if sums[ci] is None else sums[ci] + d

        def finish(accs):
            outs = epilogue(accs, [e[...] for e in e_refs])
            for o_ref, o in zip(o_refs, outs):
                o_ref[...] = o.astype(o_ref.dtype)

        if nk == 1:
            finish(sums)
        else:
            k = pl.program_id(2)

            @pl.when(k == 0)
            def _():
                for acc, s in zip(acc_refs, sums):
                    acc[...] = s

            @pl.when(k > 0)
            def _():
                for acc, s in zip(acc_refs, sums):
                    acc[...] += s

            @pl.when(k == nk - 1)
            def _():
                finish([acc[...] for acc in acc_refs])

    scratch = [pltpu.VMEM((tm, tn), F32) for _ in range(n_acc)] if nk > 1 else []
    outs = pl.pallas_call(
        body,
        name=name,
        grid=(m_dim // tm, n_dim // tn, nk),
        in_specs=[a_spec] * na + [b_spec] * nb + [t_spec] * ne,
        out_specs=[t_spec] * no,
        out_shape=[jax.ShapeDtypeStruct((m_dim, n_dim), dt) for dt in out_dtypes],
        scratch_shapes=scratch,
        compiler_params=_params(("parallel", "parallel", "arbitrary")),
    )(*a_list, *b_list, *extras)
    return outs


def _identity(accs, extras):
    return accs


def _mm_nn(name, a, b, out_dtype, res=None, alpha=1.0, **tiles):
    if res is None:
        return _matmul(name, "nn", [a], [b], [(0, 0, 0)], [out_dtype], _identity, **tiles)[0]

    def epilogue(accs, extras):
        return [extras[0] + alpha * accs[0]]

    return _matmul(name, "nn", [a], [b], [(0, 0, 0)], [out_dtype], epilogue, extras=(res,), **tiles)[0]


def _mm_nt_sum(name, a_list, b_list, out_dtype, **tiles):
    products = [(i, i, 0) for i in range(len(a_list))]
    return _matmul(name, "nt", a_list, b_list, products, [out_dtype], _identity, **tiles)[0]


def _mm_tn(name, a, b_list, out_dtype, **tiles):
    products = [(0, i, i) for i in range(len(b_list))]
    return _matmul(name, "tn", [a], b_list, products, [out_dtype] * len(b_list), _identity, **tiles)


def _rowwise(name, fn, row_ins, par_ins, row_outs, acc_outs=(), tr=512):
    first = row_ins[0][0] if isinstance(row_ins[0], tuple) else row_ins[0]
    t_dim = first.shape[0]
    tr = _tile(t_dim, tr, 16)
    arrays, specs = [], []
    for r in row_ins:
        if isinstance(r, tuple):
            arr, width, blk = r
            specs.append(pl.BlockSpec((tr, width), lambda i, blk=blk: (i, blk)))
        else:
            arr = r
            specs.append(pl.BlockSpec((tr, arr.shape[1]), lambda i: (i, 0)))
        arrays.append(arr)
    for p in par_ins:
        arrays.append(p)
        specs.append(pl.BlockSpec(p.shape, lambda i, nd=p.ndim: (0,) * nd))
    nr, npar, nro, nacc = len(row_ins), len(par_ins), len(row_outs), len(acc_outs)

    def body(*refs):
        rows = [r[...] for r in refs[:nr]]
        pars = [p[...] for p in refs[nr:nr + npar]]
        o_refs = refs[nr + npar:nr + npar + nro]
        acc_refs = refs[nr + npar + nro:]
        outs, accs = fn(rows, pars)
        for o_ref, o in zip(o_refs, outs):
            o_ref[...] = o.astype(o_ref.dtype)
        if nacc:
            @pl.when(pl.program_id(0) == 0)
            def _():
                for a_ref in acc_refs:
                    a_ref[...] = jnp.zeros_like(a_ref)

            for a_ref, a in zip(acc_refs, accs):
                a_ref[...] += a

    out_shape = [jax.ShapeDtypeStruct((t_dim, c), dt) for c, dt in row_outs]
    out_shape += [jax.ShapeDtypeStruct(s, F32) for s in acc_outs]
    out_specs = [pl.BlockSpec((tr, c), lambda i: (i, 0)) for c, _ in row_outs]
    out_specs += [pl.BlockSpec(s, lambda i: (0, 0)) for s in acc_outs]
    return pl.pallas_call(
        body,
        name=name,
        grid=(t_dim // tr,),
        in_specs=specs,
        out_specs=out_specs,
        out_shape=out_shape,
        compiler_params=_params(("arbitrary",)),
    )(*arrays)


def _rms_stats(x):
    r = lax.rsqrt(jnp.mean(x * x, axis=-1, keepdims=True) + EPS)
    return x * r, r


def _rms_backward(x, g, dy):
    xh, r = _rms_stats(x)
    a = dy * g
    dx = r * (a - xh * jnp.mean(a * xh, axis=-1, keepdims=True))
    return dx, jnp.sum(dy * xh, axis=0, keepdims=True)


def _rms_fwd(name, x, g):
    def fn(rows, pars):
        xh, _ = _rms_stats(rows[0])
        return [xh * pars[0]], []

    return _rowwise(name, fn, [x], [g], [(x.shape[1], BF16)])[0]


def _rms_bwd(name, x, g, dy, dres, scale):
    def fn(rows, pars):
        dx, dg = _rms_backward(rows[0], pars[0], rows[1])
        tot = rows[2] + dx
        return [tot, scale * tot], [dg]

    d = x.shape[1]
    return _rowwise(name, fn, [x, dy, dres], [g], [(d, F32), (d, BF16)], [(1, d)], tr=256)


def _cast_bf16(name, w):
    def fn(rows, pars):
        return [rows[0]], []

    return _rowwise(name, fn, [w], [], [(w.shape[1], BF16)], tr=256)[0]


def _ffn_fwd(tag, h, g, wg, wu, wd):
    n = _rms_fwd(tag + "_norm", h, g)

    def act(accs, extras):
        gate, up = accs
        return [gate * _sigmoid(gate) * up, gate, up]

    a, gate, up = _matmul(tag + "_gateup", "nn", [n], [wg, wu], [(0, 0, 0), (0, 1, 1)], [BF16] * 3, act,
                          tm=1024, tn=512, tk=2048)
    h_out = _mm_nn(tag + "_down", a, wd, F32, res=h, alpha=0.5, tm=512, tn=1024, tk=1408)
    return h_out, (n, a, gate, up)


def _ffn_bwd(tag, h, g, wg, wu, wd, saved, dh, dfb, next_scale):
    n, a, gate, up = saved

    def act_bwd(accs, extras):
        da = accs[0]
        gt, u = extras[0].astype(F32), extras[1].astype(F32)
        s = _sigmoid(gt)
        return [da * u * (s * (1.0 + gt * (1.0 - s))), da * (gt * s)]

    dgp, du = _matmul(tag + "_dact", "nt", [dfb], [wd], [(0, 0, 0)], [BF16, BF16], act_bwd, extras=(gate, up),
                      tm=1024, tn=512, tk=2048)
    (dwd,) = _mm_tn(tag + "_dwd", a, [dfb], BF16, tm=704, tn=1024, tk=1024)
    dwg, dwu = _mm_tn(tag + "_dwgu", n, [dgp, du], BF16, tm=1024, tn=512, tk=1024)
    dn = _mm_nt_sum(tag + "_dn", [dgp, du], [wg, wu], F32, tm=512, tn=1024, tk=1408)
    dh_in, dh_in_b, dg = _rms_bwd(tag + "_dnorm", h, g, dn, dh, next_scale)
    return dh_in, dh_in_b, dg, dwg, dwu, dwd


def _ssm_discretize(log_dt, a_re, a_im, b_re, b_im):
    dt = jnp.exp(log_dt)[:, None]
    lr = jnp.minimum(a_re, -1e-4)
    li = a_im
    mag = jnp.exp(lr * dt)
    ang = li * dt
    abar_r = mag * jnp.cos(ang)
    abar_i = mag * jnp.sin(ang)
    den = lr * lr + li * li
    xr = abar_r - 1.0
    xi = abar_i
    zr = (xr * lr + xi * li) / den
    zi = (xi * lr - xr * li) / den
    bbar_r = zr[..., None] * b_re - zi[..., None] * b_im
    bbar_i = zr[..., None] * b_im + zi[..., None] * b_re
    return abar_r, abar_i, bbar_r, bbar_i


def _blockdiag_in(b):
    g, n, p = b.shape
    nb = g // GROUPS_PER_BLOCK
    eye = jnp.eye(GROUPS_PER_BLOCK, dtype=b.dtype)
    b4 = b.reshape(nb, GROUPS_PER_BLOCK, n, p)
    return jnp.einsum("sgnp,gh->sgphn", b4, eye).reshape(nb, GROUPS_PER_BLOCK * p, GROUPS_PER_BLOCK * n)


def _blockdiag_in_grad(gw, n, p):
    nb = gw.shape[0]
    eye = jnp.eye(GROUPS_PER_BLOCK, dtype=gw.dtype)
    g5 = gw.reshape(nb, GROUPS_PER_BLOCK, p, GROUPS_PER_BLOCK, n)
    return jnp.einsum("sgphn,gh->sgnp", g5, eye).reshape(nb * GROUPS_PER_BLOCK, n, p)


def _blockdiag_out(c):
    g, p, n = c.shape
    nb = g // GROUPS_PER_BLOCK
    eye = jnp.eye(GROUPS_PER_BLOCK, dtype=c.dtype)
    c4 = c.reshape(nb, GROUPS_PER_BLOCK, p, n)
    return jnp.einsum("sgpn,gh->shngp", c4, eye).reshape(nb, GROUPS_PER_BLOCK * n, GROUPS_PER_BLOCK * p)


def _blockdiag_out_grad(gw, p, n):
    nb = gw.shape[0]
    eye = jnp.eye(GROUPS_PER_BLOCK, dtype=gw.dtype)
    g5 = gw.reshape(nb, GROUPS_PER_BLOCK, n, GROUPS_PER_BLOCK, p)
    return jnp.einsum("shngp,gh->sgpn", g5, eye).reshape(nb * GROUPS_PER_BLOCK, p, n)


def _ssm_scan_fwd(z, wb, wc, abar, d):
    t_dim = z.shape[0]
    nb, cb, sw2 = wb.shape
    sw = sw2 // 2
    tt = _tile(t_dim, 512, 8)
    nt = t_dim // tt

    def body(z_ref, wb_ref, wc_ref, a_ref, d_ref, s_ref, y_ref, yg_ref, drive_ref, st_ref):
        @pl.when(pl.program_id(1) == 0)
        def _():
            st_ref[...] = jnp.zeros_like(st_ref)

        u = z_ref[...]
        drive_ref[...] = _dot(u, wb_ref[...], "nn")
        ar = a_ref[:, :sw]
        ai = a_ref[:, sw:]

        def step(r, carry):
            sr, si = carry
            dd = drive_ref[pl.ds(r, 1), :]
            nr = ar * sr - ai * si + dd[:, :sw]
            ni = ar * si + ai * sr + dd[:, sw:]
            s_ref[pl.ds(r, 1), pl.ds(0, sw)] = nr
            s_ref[pl.ds(r, 1), pl.ds(sw, sw)] = ni
            return nr, ni

        sr, si = lax.fori_loop(0, tt, step, (st_ref[:, :sw], st_ref[:, sw:]), unroll=8)
        st_ref[:, pl.ds(0, sw)] = sr
        st_ref[:, pl.ds(sw, sw)] = si
        y = _dot(s_ref[...], wc_ref[...], "nn") + d_ref[...] * u
        y_ref[...] = y
        yg_ref[...] = _gelu(y).astype(BF16)

    return pl.pallas_call(
        body,
        name="ssm_scan_fwd",
        grid=(nb, nt),
        in_specs=[
            pl.BlockSpec((tt, cb), lambda s, t: (t, s)),
            pl.BlockSpec((None, cb, sw2), lambda s, t: (s, 0, 0)),
            pl.BlockSpec((None, sw2, cb), lambda s, t: (s, 0, 0)),
            pl.BlockSpec((None, 1, sw2), lambda s, t: (s, 0, 0)),
            pl.BlockSpec((1, cb), lambda s, t: (0, s)),
        ],
        out_specs=[
            pl.BlockSpec((tt, sw2), lambda s, t: (t, s)),
            pl.BlockSpec((tt, cb), lambda s, t: (t, s)),
            pl.BlockSpec((tt, cb), lambda s, t: (t, s)),
        ],
        out_shape=[
            jax.ShapeDtypeStruct((t_dim, nb * sw2), F32),
            jax.ShapeDtypeStruct((t_dim, nb * cb), F32),
            jax.ShapeDtypeStruct((t_dim, nb * cb), BF16),
        ],
        scratch_shapes=[pltpu.VMEM((tt, sw2), F32), pltpu.VMEM((1, sw2), F32)],
        compiler_params=_params(("parallel", "arbitrary")),
    )(z, wb, wc, abar, d)


def _ssm_scan_bwd(z, dy, states, wb, wc, abar_conj, d):
    t_dim = z.shape[0]
    nb, cb, sw2 = wb.shape
    sw = sw2 // 2
    tt = _tile(t_dim, 512, 8)
    nt = t_dim // tt
    edges = states.reshape(nt, tt, nb * sw2)[:, tt - 1, :]
    before = jnp.concatenate([jnp.zeros((1, nb * sw2), F32), edges[:-1]], axis=0).reshape(nt, 1, nb * sw2)

    def body(z_ref, dy_ref, s_ref, sp_ref, wb_ref, wc_ref, a_ref, d_ref,
             dz_ref, gwb_ref, gwc_ref, ga_ref, gd_ref, gin_ref, gs_ref, st_ref):
        @pl.when(pl.program_id(1) == 0)
        def _():
            st_ref[...] = jnp.zeros_like(st_ref)
            gwb_ref[...] = jnp.zeros_like(gwb_ref)
            gwc_ref[...] = jnp.zeros_like(gwc_ref)
            ga_ref[...] = jnp.zeros_like(ga_ref)
            gd_ref[...] = jnp.zeros_like(gd_ref)

        u = z_ref[...]
        dyv = dy_ref[...]
        gin_ref[...] = _dot(dyv, wc_ref[...], "nt")
        ar = a_ref[:, :sw]
        ai = a_ref[:, sw:]

        def step(k, carry):
            r = tt - 1 - k
            gr, gi = carry
            gg = gin_ref[pl.ds(r, 1), :]
            nr = ar * gr - ai * gi + gg[:, :sw]
            ni = ar * gi + ai * gr + gg[:, sw:]
            gs_ref[pl.ds(r, 1), pl.ds(0, sw)] = nr
            gs_ref[pl.ds(r, 1), pl.ds(sw, sw)] = ni
            return nr, ni

        gr, gi = lax.fori_loop(0, tt, step, (st_ref[:, :sw], st_ref[:, sw:]), unroll=8)
        st_ref[:, pl.ds(0, sw)] = gr
        st_ref[:, pl.ds(sw, sw)] = gi

        gs = gs_ref[...]
        s = s_ref[...]
        dz_ref[...] = (_dot(gs, wb_ref[...], "nt") + d_ref[...] * dyv).astype(BF16)
        gwb_ref[...] += _dot(u, gs, "tn")
        gwc_ref[...] += _dot(s, dyv, "tn")
        gd_ref[...] += jnp.sum(dyv * u, axis=0, keepdims=True)
        row = lax.broadcasted_iota(jnp.int32, s.shape, 0)
        s_prev = jnp.where(row == 0, sp_ref[...], pltpu.roll(s, 1, 0))
        g_r, g_i = gs[:, :sw], gs[:, sw:]
        p_r, p_i = s_prev[:, :sw], s_prev[:, sw:]
        ga_ref[:, pl.ds(0, sw)] += jnp.sum(g_r * p_r + g_i * p_i, axis=0, keepdims=True)
        ga_ref[:, pl.ds(sw, sw)] += jnp.sum(g_i * p_r - g_r * p_i, axis=0, keepdims=True)

    rev = lambda s, t: (nt - 1 - t, s)
    return pl.pallas_call(
        body,
        name="ssm_scan_bwd",
        grid=(nb, nt),
        in_specs=[
            pl.BlockSpec((tt, cb), rev),
            pl.BlockSpec((tt, cb), rev),
            pl.BlockSpec((tt, sw2), rev),
            pl.BlockSpec((None, 1, sw2), lambda s, t: (nt - 1 - t, 0, s)),
            pl.BlockSpec((None, cb, sw2), lambda s, t: (s, 0, 0)),
            pl.BlockSpec((None, sw2, cb), lambda s, t: (s, 0, 0)),
            pl.BlockSpec((None, 1, sw2), lambda s, t: (s, 0, 0)),
            pl.BlockSpec((1, cb), lambda s, t: (0, s)),
        ],
        out_specs=[
            pl.BlockSpec((tt, cb), rev),
            pl.BlockSpec((None, cb, sw2), lambda s, t: (s, 0, 0)),
            pl.BlockSpec((None, sw2, cb), lambda s, t: (s, 0, 0)),
            pl.BlockSpec((None, 1, sw2), lambda s, t: (s, 0, 0)),
            pl.BlockSpec((1, cb), lambda s, t: (0, s)),
        ],
        out_shape=[
            jax.ShapeDtypeStruct((t_dim, nb * cb), BF16),
            jax.ShapeDtypeStruct((nb, cb, sw2), F32),
            jax.ShapeDtypeStruct((nb, sw2, cb), F32),
            jax.ShapeDtypeStruct((nb, 1, sw2), F32),
            jax.ShapeDtypeStruct((1, nb * cb), F32),
        ],
        scratch_shapes=[pltpu.VMEM((tt, sw2), F32), pltpu.VMEM((tt, sw2), F32), pltpu.VMEM((1, sw2), F32)],
        compiler_params=_params(("parallel", "arbitrary")),
    )(z, dy, states, before, wb, wc, abar_conj, d)


def _gmlp_chunk(zu, zv, gv, wm_ref, bias, n_heads):
    ua = _gelu(zu)
    vg = _gelu(zv)
    xc = vg - jnp.mean(vg, axis=-1, keepdims=True)
    r = lax.rsqrt(jnp.mean(xc * xc, axis=-1, keepdims=True) + EPS)
    vh = xc * r
    vb = (vh * gv).astype(BF16)
    parts = []
    for h in range(n_heads):
        cols = slice(h * GMLP_HEAD, (h + 1) * GMLP_HEAD)
        parts.append(_dot(wm_ref[h], vb[:, cols], "nn"))
    s = jnp.concatenate(parts, axis=1) + bias
    return ua, vh, r, vb, s


def _gmlp_fwd(z, gv, wm, bias, ggo):
    t_dim = z.shape[0]
    dg = gv.shape[1]
    n_heads = dg // GMLP_HEAD
    tr = _tile(t_dim, 256, CHUNK)

    def body(zu_ref, zv_ref, gv_ref, wm_ref, b_ref, ggo_ref, o_ref):
        for ck in range(tr // CHUNK):
            rows = pl.ds(ck * CHUNK, CHUNK)
            ua, _, _, _, s = _gmlp_chunk(zu_ref[rows, :], zv_ref[rows, :], gv_ref[...], wm_ref, b_ref[...], n_heads)
            yh, _ = _rms_stats(ua * s)
            o_ref[rows, :] = (yh * ggo_ref[...]).astype(BF16)

    full = lambda a: pl.BlockSpec(a.shape, lambda i, nd=a.ndim: (0,) * nd)
    return pl.pallas_call(
        body,
        name="gmlp_fwd",
        grid=(t_dim // tr,),
        in_specs=[pl.BlockSpec((tr, dg), lambda i: (i, 1)), pl.BlockSpec((tr, dg), lambda i: (i, 2)),
                  full(gv), full(wm), full(bias), full(ggo)],
        out_specs=pl.BlockSpec((tr, dg), lambda i: (i, 0)),
        out_shape=jax.ShapeDtypeStruct((t_dim, dg), BF16),
        compiler_params=_params(("parallel",)),
    )(z, z, gv, wm, bias, ggo)


def _gmlp_bwd(z, dycat, gv, wm, bias, ggo):
    t_dim = z.shape[0]
    dg = gv.shape[1]
    n_heads = dg // GMLP_HEAD
    tr = _tile(t_dim, 256, CHUNK)

    def body(zu_ref, zv_ref, dy_ref, gv_ref, wm_ref, b_ref, ggo_ref,
             dzu_ref, dzv_ref, dggo_ref, dgv_ref, dwm_ref, dsum_ref):
        @pl.when(pl.program_id(0) == 0)
        def _():
            dggo_ref[...] = jnp.zeros_like(dggo_ref)
            dgv_ref[...] = jnp.zeros_like(dgv_ref)
            dwm_ref[...] = jnp.zeros_like(dwm_ref)
            dsum_ref[...] = jnp.zeros_like(dsum_ref)

        for ck in range(tr // CHUNK):
            rows = pl.ds(ck * CHUNK, CHUNK)
            zu = zu_ref[rows, :]
            zv = zv_ref[rows, :]
            gvv = gv_ref[...]
            ua, vh, r, vb, s = _gmlp_chunk(zu, zv, gvv, wm_ref, b_ref[...], n_heads)
            dy, dggo = _rms_backward(ua * s, ggo_ref[...], dy_ref[rows, :])
            dggo_ref[...] += dggo
            ds = dy * ua
            dsum_ref[...] += ds
            dsb = ds.astype(BF16)
            parts = []
            for h in range(n_heads):
                cols = slice(h * GMLP_HEAD, (h + 1) * GMLP_HEAD)
                dwm_ref[h] += _dot(dsb[:, cols], vb[:, cols], "nt")
                parts.append(_dot(wm_ref[h], dsb[:, cols], "tn"))
            dv = jnp.concatenate(parts, axis=1)
            dgv_ref[...] += jnp.sum(dv * vh, axis=0, keepdims=True)
            dvh = dv * gvv
            dvg = r * (dvh - jnp.mean(dvh, axis=-1, keepdims=True) - vh * jnp.mean(dvh * vh, axis=-1, keepdims=True))
            dzv_ref[rows, :] = (dvg * _gelu_grad(zv)).astype(BF16)
            dzu_ref[rows, :] = (dy * s * _gelu_grad(zu)).astype(BF16)

    full = lambda a: pl.BlockSpec(a.shape, lambda i, nd=a.ndim: (0,) * nd)
    return pl.pallas_call(
        body,
        name="gmlp_bwd",
        grid=(t_dim // tr,),
        in_specs=[pl.BlockSpec((tr, dg), lambda i: (i, 1)), pl.BlockSpec((tr, dg), lambda i: (i, 2)),
                  pl.BlockSpec((tr, dg), lambda i: (i, 1)), full(gv), full(wm), full(bias), full(ggo)],
        out_specs=[pl.BlockSpec((tr, dg), lambda i: (i, 0)), pl.BlockSpec((tr, dg), lambda i: (i, 0)),
                   pl.BlockSpec((1, dg), lambda i: (0, 0)), pl.BlockSpec((1, dg), lambda i: (0, 0)),
                   pl.BlockSpec(wm.shape, lambda i: (0, 0, 0)), pl.BlockSpec((CHUNK, dg), lambda i: (0, 0))],
        out_shape=[jax.ShapeDtypeStruct((t_dim, dg), BF16), jax.ShapeDtypeStruct((t_dim, dg), BF16),
                   jax.ShapeDtypeStruct((1, dg), F32), jax.ShapeDtypeStruct((1, dg), F32),
                   jax.ShapeDtypeStruct(wm.shape, F32), jax.ShapeDtypeStruct((CHUNK, dg), F32)],
        compiler_params=_params(("arbitrary",)),
    )(z, z, dycat, gv, wm, bias, ggo)


_ANY = pl.BlockSpec(memory_space=pl.ANY)


def _position():
    x, y, c = lax.axis_index("x"), lax.axis_index("y"), lax.axis_index("c")
    chips = [(1 - x, y), (x, 1 - y), (1 - x, 1 - y)]
    return x, y, c, chips


def _region(ref, kind, shard_shape, q, half):
    rs, cs = shard_shape
    r0, nr = (0, rs) if half is None else (half * (rs // 2), rs // 2)
    if kind == "row":
        return ref.at[pl.ds(q * rs + r0, nr), :]
    return ref.at[pl.ds(r0, nr), pl.ds(q * cs, cs)]


def _full_shape(kind, shard_shape):
    rs, cs = shard_shape
    return (N_CHIPS * rs, cs) if kind == "row" else (rs, N_CHIPS * cs)


def _allgather_weights(shards, kinds):
    nw = len(shards)
    shapes = [s.shape for s in shards]

    def body(*refs):
        sh = refs[:nw]
        full = refs[nw:2 * nw]
        send_sems, recv_sems, local_sems = refs[2 * nw:]
        x, y, c, chips = _position()
        q_me = 2 * x + y

        def remote(w, k, src, dst, to):
            return pltpu.make_async_remote_copy(src_ref=src, dst_ref=dst, send_sem=send_sems.at[w * 6 + k],
                                                recv_sem=recv_sems.at[w * 6 + k], device_id=to, device_id_type=MESH)

        local = []
        for w in range(nw):
            cp = pltpu.make_async_copy(sh[w], _region(full[w], kinds[w], shapes[w], q_me, None), local_sems.at[w])
            cp.start()
            local.append(cp)
        sent = []
        for w in range(nw):
            half_rows = shapes[w][0] // 2
            mine = sh[w].at[pl.ds(c * half_rows, half_rows), :]
            for j, (cx, cy) in enumerate(chips):
                cp = remote(w, j, mine, _region(full[w], kinds[w], shapes[w], q_me, c), (cx, cy, c))
                cp.start()
                sent.append(cp)
        for w in range(nw):
            for j, (cx, cy) in enumerate(chips):
                landed = _region(full[w], kinds[w], shapes[w], 2 * cx + cy, c)
                remote(w, j, landed, landed, (cx, cy, c)).wait_recv()
                cp = remote(w, 3 + j, landed, landed, (x, y, 1 - c))
                cp.start()
                sent.append(cp)
        for w in range(nw):
            for j, (cx, cy) in enumerate(chips):
                theirs = _region(full[w], kinds[w], shapes[w], 2 * cx + cy, 1 - c)
                remote(w, 3 + j, theirs, theirs, (x, y, 1 - c)).wait_recv()
        for cp in sent:
            cp.wait_send()
        for cp in local:
            cp.wait()

    return pl.pallas_call(
        body,
        name="allgather_weights",
        in_specs=[_ANY] * nw,
        out_specs=[_ANY] * nw,
        out_shape=[jax.ShapeDtypeStruct(_full_shape(k, s), BF16) for k, s in zip(kinds, shapes)],
        scratch_shapes=[pltpu.SemaphoreType.DMA((6 * nw,)), pltpu.SemaphoreType.DMA((6 * nw,)),
                        pltpu.SemaphoreType.DMA((nw,))],
    )(*shards)


def _reduce_pairs(grads, kinds, shapes):
    nw = len(grads)

    def body(*refs):
        g = refs[:nw]
        got = refs[nw:2 * nw]
        send_sems, recv_sems = refs[2 * nw:]
        x, y, c, _ = _position()
        sent = []
        for w in range(nw):
            for q in range(N_CHIPS):
                cp = pltpu.make_async_remote_copy(
                    src_ref=_region(g[w], kinds[w], shapes[w], q, 1 - c), dst_ref=got[w].at[q],
                    send_sem=send_sems.at[w * N_CHIPS + q], recv_sem=recv_sems.at[w * N_CHIPS + q],
                    device_id=(x, y, 1 - c), device_id_type=MESH)
                cp.start()
                sent.append(cp)
        for cp in sent:
            cp.wait()

    return pl.pallas_call(
        body,
        name="reduce_pairs",
        in_specs=[_ANY] * nw,
        out_specs=[_ANY] * nw,
        out_shape=[jax.ShapeDtypeStruct((N_CHIPS, s[0] // 2, s[1]), BF16) for s in shapes],
        scratch_shapes=[pltpu.SemaphoreType.DMA((N_CHIPS * nw,)), pltpu.SemaphoreType.DMA((N_CHIPS * nw,))],
    )(*grads)


def _pair_sum(name, grad, got, kind, shard_shape, c_arr):
    rs, cs = shard_shape
    hr = rs // 2
    tr = _tile(hr, 512, 16)
    nr = hr // tr

    def body(c_ref, g_ref, s_ref, o_ref):
        o_ref[...] = (g_ref[...].astype(F32) + s_ref[...].astype(F32)).astype(BF16)

    if kind == "row":
        g_spec = pl.BlockSpec((tr, cs), lambda q, i, c_ref: (q * (rs // tr) + c_ref[0] * nr + i, 0))
    else:
        g_spec = pl.BlockSpec((tr, cs), lambda q, i, c_ref: (c_ref[0] * nr + i, q))
    blk = pl.BlockSpec((None, tr, cs), lambda q, i, c_ref: (q, i, 0))
    return pl.pallas_call(
        body,
        name=name,
        grid_spec=pltpu.PrefetchScalarGridSpec(num_scalar_prefetch=1, grid=(N_CHIPS, nr), in_specs=[g_spec, blk],
                                               out_specs=blk),
        out_shape=jax.ShapeDtypeStruct((N_CHIPS, hr, cs), BF16),
        compiler_params=_params(("parallel", "parallel")),
    )(c_arr, grad, got)


def _scatter_to_owners(sums, shapes):
    nw = len(sums)

    def body(*refs):
        ps = refs[:nw]
        got = refs[nw:2 * nw]
        send_sems, recv_sems = refs[2 * nw:]
        x, y, c, chips = _position()
        sent = []
        for w in range(nw):
            for j, (cx, cy) in enumerate(chips):
                cp = pltpu.make_async_remote_copy(
                    src_ref=ps[w].at[2 * cx + cy], dst_ref=got[w].at[j],
                    send_sem=send_sems.at[w * 3 + j], recv_sem=recv_sems.at[w * 3 + j],
                    device_id=(cx, cy, c), device_id_type=MESH)
                cp.start()
                sent.append(cp)
        for cp in sent:
            cp.wait()

    return pl.pallas_call(
        body,
        name="scatter_to_owners",
        in_specs=[_ANY] * nw,
        out_specs=[_ANY] * nw,
        out_shape=[jax.ShapeDtypeStruct((3, s[0] // 2, s[1]), BF16) for s in shapes],
        scratch_shapes=[pltpu.SemaphoreType.DMA((3 * nw,)), pltpu.SemaphoreType.DMA((3 * nw,))],
    )(*sums)


def _owner_sum(name, sums, got, shard_shape, q_arr):
    rs, cs = shard_shape
    hr = rs // 2
    tr = _tile(hr, 512, 16)

    def body(q_ref, mine_ref, got_ref, o_ref):
        acc = mine_ref[...].astype(F32)
        for j in range(3):
            acc = acc + got_ref[j].astype(F32)
        o_ref[...] = acc

    return pl.pallas_call(
        body,
        name=name,
        grid_spec=pltpu.PrefetchScalarGridSpec(
            num_scalar_prefetch=1, grid=(hr // tr,),
            in_specs=[pl.BlockSpec((None, tr, cs), lambda i, q_ref: (q_ref[0], i, 0)),
                      pl.BlockSpec((3, tr, cs), lambda i, q_ref: (0, i, 0))],
            out_specs=pl.BlockSpec((tr, cs), lambda i, q_ref: (i, 0))),
        out_shape=jax.ShapeDtypeStruct((hr, cs), F32),
        compiler_params=_params(("parallel",)),
    )(q_arr, sums, got)


def _share_halves(halves, shapes):
    nw = len(halves)

    def body(*refs):
        hv = refs[:nw]
        out = refs[nw:2 * nw]
        send_sems, recv_sems, local_sems = refs[2 * nw:]
        x, y, c, _ = _position()
        pending = []
        for w in range(nw):
            hr = shapes[w][0] // 2
            mine = out[w].at[pl.ds(c * hr, hr), :]
            cp = pltpu.make_async_copy(hv[w], mine, local_sems.at[w])
            cp.start()
            rc = pltpu.make_async_remote_copy(src_ref=hv[w], dst_ref=mine, send_sem=send_sems.at[w],
                                              recv_sem=recv_sems.at[w], device_id=(x, y, 1 - c), device_id_type=MESH)
            rc.start()
            pending.append((cp, rc))
        for cp, rc in pending:
            rc.wait()
            cp.wait()

    return pl.pallas_call(
        body,
        name="share_halves",
        in_specs=[_ANY] * nw,
        out_specs=[_ANY] * nw,
        out_shape=[jax.ShapeDtypeStruct(s, F32) for s in shapes],
        scratch_shapes=[pltpu.SemaphoreType.DMA((nw,)), pltpu.SemaphoreType.DMA((nw,)), pltpu.SemaphoreType.DMA((nw,))],
    )(*halves)


def _allreduce_small(packed):
    rows = packed.shape[0]

    def body(x_ref, out_ref, gat_ref, send_sems, recv_sems, local_sem):
        x, y, c, chips = _position()
        me, sibling = (x, y, c), (x, y, 1 - c)

        def block(px, py, pc):
            return gat_ref.at[pl.ds((4 * px + 2 * py + pc) * rows, rows), :]

        def copy(k, blk, to, src=None):
            return pltpu.make_async_remote_copy(
                src_ref=block(*blk) if src is None else src, dst_ref=block(*blk), send_sem=send_sems.at[k],
                recv_sem=recv_sems.at[k], device_id=to, device_id_type=MESH)

        mine = pltpu.make_async_copy(x_ref, block(*me), local_sem)
        mine.start()
        first = [copy(0, me, sibling, src=x_ref)]
        first += [copy(1 + j, me, (*chip, c), src=x_ref) for j, chip in enumerate(chips)]
        for cp in first:
            cp.start()
        passed = [copy(4 + j, (*chip, c), sibling) for j, chip in enumerate(chips)]
        for j, chip in enumerate(chips):
            copy(1 + j, (*chip, c), me).wait_recv()
            passed[j].start()
        copy(0, sibling, me).wait_recv()
        for j, chip in enumerate(chips):
            copy(4 + j, (*chip, 1 - c), me).wait_recv()
        for cp in first + passed:
            cp.wait_send()
        mine.wait()
        acc = gat_ref[pl.ds(0, rows), :]
        for k in range(1, N_DEV):
            acc = acc + gat_ref[pl.ds(k * rows, rows), :]
        out_ref[...] = acc

    return pl.pallas_call(
        body,
        name="allreduce_small",
        in_specs=[pl.BlockSpec(memory_space=pltpu.VMEM)],
        out_specs=pl.BlockSpec(memory_space=pltpu.VMEM),
        out_shape=jax.ShapeDtypeStruct(packed.shape, F32),
        scratch_shapes=[pltpu.VMEM((N_DEV * rows, LANES), F32), pltpu.SemaphoreType.DMA((7,)),
                        pltpu.SemaphoreType.DMA((7,)), pltpu.SemaphoreType.DMA],
        compiler_params=pltpu.CompilerParams(vmem_limit_bytes=VMEM_LIMIT_BYTES),
    )(packed)


def _adamw(name, w, g, m, v):
    def fn(rows, pars):
        wv, gv, mv, vv = rows
        m_new = ADAM_B1 * mv + (1.0 - ADAM_B1) * gv
        v_new = ADAM_B2 * vv + (1.0 - ADAM_B2) * (gv * gv)
        m_hat = m_new / (1.0 - ADAM_B1 ** ADAM_STEP)
        v_hat = v_new / (1.0 - ADAM_B2 ** ADAM_STEP)
        delta = -ADAM_LR * (m_hat / (jnp.sqrt(v_hat) + ADAM_EPS) + ADAM_WD * wv)
        return [delta, m_new, v_new], []

    c = w.shape[1]
    return _rowwise(name, fn, [w, g, m, v], [], [(c, F32)] * 3, tr=256)


def _pack(arrays):
    rows = []
    for a in arrays:
        flat = a.reshape(-1).astype(F32)
        pad = (-flat.shape[0]) % LANES
        rows.append(jnp.pad(flat, (0, pad)).reshape(-1, LANES))
    stacked = jnp.concatenate(rows, axis=0)
    pad_rows = (-stacked.shape[0]) % 8
    return jnp.pad(stacked, ((0, pad_rows), (0, 0)))


def _unpack(packed, shapes):
    out, r = [], 0
    for s in shapes:
        n = math.prod(s)
        nr = -(-n // LANES)
        out.append(packed[r:r + nr].reshape(-1)[:n].reshape(s))
        r += nr
    return out


BIG = ["w1_gate", "w1_up", "w1_down", "w_in", "ssm_w_glu", "w_out", "w2_gate", "w2_up", "w2_down", "w_ple_gate",
       "w_ple_proj"]
KIND = {"w1_gate": "col", "w1_up": "col", "w1_down": "row", "w_in": "col", "ssm_w_glu": "row", "w_out": "row",
        "w2_gate": "col", "w2_up": "col", "w2_down": "row", "w_ple_gate": "row", "w_ple_proj": "col"}
SMALL = ["norm_ffn1", "norm_mix", "ssm_log_dt", "ssm_a_re", "ssm_a_im", "ssm_b_re", "ssm_b_im", "ssm_c_re", "ssm_c_im",
         "ssm_d", "gmlp_norm_v", "gmlp_w_s", "gmlp_b_s", "norm_ssm_out", "norm_gmlp_out", "norm_ffn2", "norm_ple",
         "norm_final"]
WEIGHTS = ["norm_ffn1", "w1_gate", "w1_up", "w1_down", "norm_mix", "w_in", "ssm_log_dt", "ssm_a_re", "ssm_a_im",
           "ssm_b_re", "ssm_b_im", "ssm_c_re", "ssm_c_im", "ssm_d", "ssm_w_glu", "gmlp_norm_v", "gmlp_w_s", "gmlp_b_s",
           "norm_ssm_out", "norm_gmlp_out", "w_out", "norm_ffn2", "w2_gate", "w2_up", "w2_down", "norm_ple",
           "w_ple_gate", "w_ple_proj", "norm_final"]


def _step(x, p, tgt, w, m, v):
    d_model = x.shape[1]
    d_ssm = w["ssm_d"].shape[1]
    n_groups = d_ssm // SSM_GROUP
    row = lambda a: a.reshape(1, -1)

    shards = [_cast_bf16("cast_" + n, w[n]) for n in BIG]
    kinds = [KIND[n] for n in BIG]
    shapes = [s.shape for s in shards]
    full = dict(zip(BIG, _allgather_weights(shards, kinds)))

    h1, ffn1 = _ffn_fwd("ffn1", x, w["norm_ffn1"], full["w1_gate"], full["w1_up"], full["w1_down"])
    nm = _rms_fwd("mix_norm", h1, w["norm_mix"])
    z = _mm_nn("in_proj", nm, full["w_in"], F32, tm=1024, tn=512, tk=2048)

    disc, disc_vjp = jax.vjp(_ssm_discretize, w["ssm_log_dt"][0], w["ssm_a_re"], w["ssm_a_im"], w["ssm_b_re"],
                             w["ssm_b_im"])
    abar_r, abar_i, bbar_r, bbar_i = disc
    nb = n_groups // GROUPS_PER_BLOCK
    wb = jnp.concatenate([_blockdiag_in(bbar_r), _blockdiag_in(bbar_i)], axis=-1).astype(BF16)
    wc = jnp.concatenate([_blockdiag_out(w["ssm_c_re"]), -_blockdiag_out(w["ssm_c_im"])], axis=1).astype(BF16)
    abar = jnp.concatenate([abar_r.reshape(nb, 1, -1), abar_i.reshape(nb, 1, -1)], axis=-1)
    abar_conj = jnp.concatenate([abar_r.reshape(nb, 1, -1), -abar_i.reshape(nb, 1, -1)], axis=-1)
    states, y_pre, yg = _ssm_scan_fwd(z, wb, wc, abar, w["ssm_d"])
    q = _mm_nn("glu_proj", yg, full["ssm_w_glu"], F32, tm=1024, tn=1024, tk=1024)

    def glu_norm(rows, pars):
        yv = _gelu(rows[0]) * _sigmoid(rows[1])
        yh, _ = _rms_stats(yv)
        return [yh * pars[0]], []

    yn_ssm = _rowwise("ssm_glu_norm", glu_norm, [y_pre, q], [w["norm_ssm_out"]], [(d_ssm, BF16)])[0]

    tril = jnp.tril(jnp.ones((CHUNK, CHUNK), dtype=bool))
    wm = jnp.where(tril[None], w["gmlp_w_s"], 0.0).astype(BF16)
    bias = jnp.repeat(w["gmlp_b_s"].T, GMLP_HEAD, axis=1)
    yn_gmlp = _gmlp_fwd(z, w["gmlp_norm_v"], wm, bias, w["norm_gmlp_out"])
    ycat = jnp.concatenate([yn_ssm, yn_gmlp], axis=1)
    h2 = _mm_nn("out_proj", ycat, full["w_out"], F32, res=h1, alpha=1.0, tm=512, tn=1024, tk=2048)

    h3, ffn2 = _ffn_fwd("ffn2", h2, w["norm_ffn2"], full["w2_gate"], full["w2_up"], full["w2_down"])
    npl = _rms_fwd("ple_norm", h3, w["norm_ple"])
    gq = _mm_nn("ple_gate", npl, full["w_ple_gate"], F32, tm=1024, tn=1024, tk=2048)
    pp = _mm_nn("ple_proj", p, full["w_ple_proj"], F32, tm=1024, tn=1024, tk=2048)

    def ple_combine(rows, pars):
        return [rows[0] + _sigmoid(rows[1]) * rows[2]], []

    h4 = _rowwise("ple_combine", ple_combine, [h3, gq, pp], [], [(d_model, F32)], tr=256)[0]

    def head(rows, pars):
        hv, tv = rows
        xh, _ = _rms_stats(hv)
        err = xh * pars[0] - tv
        dx, dg = _rms_backward(hv, pars[0], err * (1.0 / d_model))
        part = 0.5 * jnp.sum(err * err) * (1.0 / d_model)
        return [dx], [dg, jnp.full((1, LANES), part, F32)]

    dh4, g_norm_final, loss_part = _rowwise("loss_head", head, [h4, tgt], [row(w["norm_final"])], [(d_model, F32)],
                                            [(1, d_model), (1, LANES)], tr=256)

    def ple_bwd(rows, pars):
        dh, gqv, ppv = rows
        gate = _sigmoid(gqv)
        return [dh * ppv * gate * (1.0 - gate), dh * gate], []

    dgq, dpp = _rowwise("ple_dgate", ple_bwd, [dh4, gq, pp], [], [(d_model, BF16)] * 2, tr=256)
    (g_w_ple_proj,) = _mm_tn("ple_dwproj", p, [dpp], BF16, tm=256, tn=1024, tk=1024)
    (g_w_ple_gate,) = _mm_tn("ple_dwgate", npl, [dgq], BF16, tm=1024, tn=1024, tk=1024)
    dnpl = _mm_nt_sum("ple_dnorm_in", [dgq], [full["w_ple_gate"]], F32, tm=512, tn=1024, tk=2048)
    dh3, dh3_b, g_norm_ple = _rms_bwd("ple_dnorm", h3, w["norm_ple"], dnpl, dh4, 0.5)

    dh2, dh2_b, g_norm_ffn2, g_w2_gate, g_w2_up, g_w2_down = _ffn_bwd(
        "ffn2", h2, w["norm_ffn2"], full["w2_gate"], full["w2_up"], full["w2_down"], ffn2, dh3, dh3_b, 1.0)

    dycat = _mm_nt_sum("out_dproj", [dh2_b], [full["w_out"]], F32, tm=512, tn=1024, tk=2048)
    (g_w_out,) = _mm_tn("out_dw", ycat, [dh2_b], BF16, tm=1024, tn=1024, tk=1024)

    dzu, dzv, g_norm_gmlp_out, g_gmlp_norm_v, g_wm, g_s = _gmlp_bwd(z, dycat, w["gmlp_norm_v"], wm, bias,
                                                                  w["norm_gmlp_out"])
    g_gmlp_w_s = jnp.where(tril[None], g_wm, 0.0)
    g_gmlp_b_s = g_s.reshape(CHUNK, -1, GMLP_HEAD).sum(axis=-1).T

    def glu_bwd(rows, pars):
        dyn, ypre, qv = rows
        ygv = _gelu(ypre)
        sg = _sigmoid(qv)
        dy, dg = _rms_backward(ygv * sg, pars[0], dyn)
        return [dy * ygv * sg * (1.0 - sg), dy * sg], [dg]

    dq, dyg_part, g_norm_ssm_out = _rowwise("ssm_dglu", glu_bwd, [(dycat, d_ssm, 0), y_pre, q], [w["norm_ssm_out"]],
                                            [(d_ssm, BF16), (d_ssm, F32)], [(1, d_ssm)])
    dyg_proj = _mm_nt_sum("glu_dproj", [dq], [full["ssm_w_glu"]], F32, tm=1024, tn=1024, tk=1024)
    (g_ssm_w_glu,) = _mm_tn("glu_dw", yg, [dq], BF16, tm=1024, tn=1024, tk=1024)

    def gelu_bwd(rows, pars):
        return [(rows[0] + rows[1]) * _gelu_grad(rows[2])], []

    dy_pre = _rowwise("ssm_dgelu", gelu_bwd, [dyg_part, dyg_proj, y_pre], [], [(d_ssm, F32)])[0]
    dz_ssm, g_wb, g_wc, g_abar, g_ssm_d = _ssm_scan_bwd(z, dy_pre, states, wb, wc, abar_conj, w["ssm_d"])
    sw = g_abar.shape[-1] // 2
    g_bbar_r = _blockdiag_in_grad(g_wb[..., :sw], SSM_STATE, SSM_GROUP)
    g_bbar_i = _blockdiag_in_grad(g_wb[..., sw:], SSM_STATE, SSM_GROUP)
    g_ssm_c_re = _blockdiag_out_grad(g_wc[:, :sw, :], SSM_GROUP, SSM_STATE)
    g_ssm_c_im = -_blockdiag_out_grad(g_wc[:, sw:, :], SSM_GROUP, SSM_STATE)
    g_abar_r = g_abar[..., :sw].reshape(n_groups, SSM_STATE)
    g_abar_i = g_abar[..., sw:].reshape(n_groups, SSM_STATE)
    g_ssm_log_dt, g_ssm_a_re, g_ssm_a_im, g_ssm_b_re, g_ssm_b_im = disc_vjp((g_abar_r, g_abar_i, g_bbar_r, g_bbar_i))

    dz = jnp.concatenate([dz_ssm, dzu, dzv], axis=1)
    (g_w_in,) = _mm_tn("in_dw", nm, [dz], BF16, tm=1024, tn=768, tk=1024)
    dnm = _mm_nt_sum("in_dproj", [dz], [full["w_in"]], F32, tm=512, tn=1024, tk=1024)
    dh1, dh1_b, g_norm_mix = _rms_bwd("mix_dnorm", h1, w["norm_mix"], dnm, dh2, 0.5)
    dx, _, g_norm_ffn1, g_w1_gate, g_w1_up, g_w1_down = _ffn_bwd(
        "ffn1", x, w["norm_ffn1"], full["w1_gate"], full["w1_up"], full["w1_down"], ffn1, dh1, dh1_b, 1.0)

    local = {"w1_gate": g_w1_gate, "w1_up": g_w1_up, "w1_down": g_w1_down, "w_in": g_w_in, "ssm_w_glu": g_ssm_w_glu,
             "w_out": g_w_out, "w2_gate": g_w2_gate, "w2_up": g_w2_up, "w2_down": g_w2_down,
             "w_ple_gate": g_w_ple_gate, "w_ple_proj": g_w_ple_proj}
    xi, yi, ci = lax.axis_index("x"), lax.axis_index("y"), lax.axis_index("c")
    c_arr = jnp.reshape(ci, (1,)).astype(jnp.int32)
    q_arr = jnp.reshape(2 * xi + yi, (1,)).astype(jnp.int32)
    grads_full = [local[n] for n in BIG]
    from_sibling = _reduce_pairs(grads_full, kinds, shapes)
    sums = [_pair_sum("pair_sum_" + n, g, s, k, sh, c_arr)
            for n, g, s, k, sh in zip(BIG, grads_full, from_sibling, kinds, shapes)]
    from_chips = _scatter_to_owners(sums, shapes)
    halves = [_owner_sum("owner_sum_" + n, ps, got, sh, q_arr) for n, ps, got, sh in zip(BIG, sums, from_chips, shapes)]
    grad = dict(zip(BIG, _share_halves(halves, shapes)))

    small = {"norm_ffn1": g_norm_ffn1, "norm_mix": g_norm_mix, "ssm_log_dt": g_ssm_log_dt, "ssm_a_re": g_ssm_a_re,
             "ssm_a_im": g_ssm_a_im, "ssm_b_re": g_ssm_b_re, "ssm_b_im": g_ssm_b_im, "ssm_c_re": g_ssm_c_re,
             "ssm_c_im": g_ssm_c_im, "ssm_d": g_ssm_d, "gmlp_norm_v": g_gmlp_norm_v, "gmlp_w_s": g_gmlp_w_s,
             "gmlp_b_s": g_gmlp_b_s, "norm_ssm_out": g_norm_ssm_out, "norm_gmlp_out": g_norm_gmlp_out,
             "norm_ffn2": g_norm_ffn2, "norm_ple": g_norm_ple, "norm_final": g_norm_final}
    small_shapes = [w[n].shape for n in SMALL]
    reduced = _allreduce_small(_pack([small[n] for n in SMALL] + [loss_part[:, :1]]))
    small_grads = _unpack(reduced, small_shapes + [(1,)])
    loss = small_grads[-1].reshape(())
    for n, g in zip(SMALL, small_grads[:-1]):
        grad[n] = g

    delta, new_m, new_v = {}, {}, {}
    for n in BIG:
        delta[n], new_m[n], new_v[n] = _adamw("adamw_" + n, w[n], grad[n], m[n], v[n])
    g_packed = reduced[: _pack([small[n] for n in SMALL]).shape[0]]
    d_p, m_p, v_p = _adamw("adamw_small", _pack([w[n] for n in SMALL]), g_packed, _pack([m[n] for n in SMALL]),
                           _pack([v[n] for n in SMALL]))
    for name_list, packed in ((delta, d_p), (new_m, m_p), (new_v, v_p)):
        for n, a in zip(SMALL, _unpack(packed, small_shapes)):
            name_list[n] = a
    return loss, dx, grad, delta, new_m, new_v


def kernel(x, p, norm_ffn1, w1_gate, w1_up, w1_down, norm_mix, w_in, ssm_log_dt, ssm_a_re, ssm_a_im, ssm_b_re, ssm_b_im, ssm_c_re, ssm_c_im, ssm_d, ssm_w_glu, gmlp_norm_v, gmlp_w_s, gmlp_b_s, norm_ssm_out, norm_gmlp_out, w_out, norm_ffn2, w2_gate, w2_up, w2_down, norm_ple, w_ple_gate, w_ple_proj, norm_final, loss_target, m_norm_ffn1, m_w1_gate, m_w1_up, m_w1_down, m_norm_mix, m_w_in, m_ssm_log_dt, m_ssm_a_re, m_ssm_a_im, m_ssm_b_re, m_ssm_b_im, m_ssm_c_re, m_ssm_c_im, m_ssm_d, m_ssm_w_glu, m_gmlp_norm_v, m_gmlp_w_s, m_gmlp_b_s, m_norm_ssm_out, m_norm_gmlp_out, m_w_out, m_norm_ffn2, m_w2_gate, m_w2_up, m_w2_down, m_norm_ple, m_w_ple_gate, m_w_ple_proj, m_norm_final, v_norm_ffn1, v_w1_gate, v_w1_up, v_w1_down, v_norm_mix, v_w_in, v_ssm_log_dt, v_ssm_a_re, v_ssm_a_im, v_ssm_b_re, v_ssm_b_im, v_ssm_c_re, v_ssm_c_im, v_ssm_d, v_ssm_w_glu, v_gmlp_norm_v, v_gmlp_w_s, v_gmlp_b_s, v_norm_ssm_out, v_norm_gmlp_out, v_w_out, v_norm_ffn2, v_w2_gate, v_w2_up, v_w2_down, v_norm_ple, v_w_ple_gate, v_w_ple_proj, v_norm_final):
    given = dict(locals())
    shapes = {n: given[n].shape for n in WEIGHTS}

    def block(name):
        a = given[name]
        if a.ndim == 1:
            return a.reshape(1, -1)
        return a[0] if a.ndim >= 3 else a

    w = {n: block(n) for n in WEIGHTS}
    m = {n: block("m_" + n) for n in WEIGHTS}
    v = {n: block("v_" + n) for n in WEIGHTS}
    loss, dx, grad, delta, new_m, new_v = _step(x[0], p[0, 0], loss_target[0], w, m, v)
    outs = [loss, dx[None]]
    for tree in (grad, delta, new_m, new_v):
        outs += [tree[n].reshape(shapes[n]) for n in WEIGHTS]
    return tuple(outs)
```

```python
import functools
import math

import jax
import jax.numpy as jnp
from jax import lax
from jax.experimental import pallas as pl
from jax.experimental.pallas import tpu as pltpu

F32 = jnp.float32
BF16 = jnp.bfloat16
EPS = 1e-6
SSM_GROUP = 16
SSM_STATE = 64
GROUPS_PER_BLOCK = 8
GMLP_HEAD = 128
CHUNK = 128
ADAM_LR = 0.001
ADAM_B1 = 0.9
ADAM_B2 = 0.999
ADAM_EPS = 1e-08
ADAM_WD = 0.01
ADAM_STEP = 10
N_CHIPS = 4
N_DEV = 8
LANES = 128
VMEM_LIMIT_BYTES = 56 * 1024 * 1024
MESH = pl.DeviceIdType.MESH
GELU_C = math.sqrt(2.0 / math.pi)
GELU_A = 0.044715

_DOT_DIMS = {
    "nn": (((1,), (0,)), ((), ())),
    "nt": (((1,), (1,)), ((), ())),
    "tn": (((0,), (0,)), ((), ())),
}


def _tile(dim, pref, align):
    if dim <= pref:
        return dim
    t = (pref // align) * align
    while t >= align:
        if dim % t == 0:
            return t
        t -= align
    return dim


def _params(semantics):
    return pltpu.CompilerParams(dimension_semantics=semantics, vmem_limit_bytes=VMEM_LIMIT_BYTES)


def _gelu(x):
    return 0.5 * x * (1.0 + jnp.tanh(GELU_C * (x + GELU_A * x * x * x)))


def _gelu_grad(x):
    t = jnp.tanh(GELU_C * (x + GELU_A * x * x * x))
    return 0.5 * (1.0 + t) + 0.5 * x * (1.0 - t * t) * GELU_C * (1.0 + 3.0 * GELU_A * x * x)


def _sigmoid(x):
    return 1.0 / (1.0 + jnp.exp(-x))


def _dot(a, b, mode):
    return lax.dot_general(a.astype(BF16), b.astype(BF16), _DOT_DIMS[mode], preferred_element_type=F32)


def _matmul(name, mode, a_list, b_list, products, out_dtypes, epilogue, extras=(), tm=512, tn=512, tk=2048):
    a0, b0 = a_list[0], b_list[0]
    if mode == "tn":
        k_dim, m_dim = a0.shape
    else:
        m_dim, k_dim = a0.shape
    n_dim = b0.shape[0] if mode == "nt" else b0.shape[1]
    tm = _tile(m_dim, tm, LANES)
    tn = _tile(n_dim, tn, LANES)
    tk = _tile(k_dim, tk, LANES)
    nk = k_dim // tk
    n_acc = 1 + max(p[2] for p in products)
    na, nb, ne, no = len(a_list), len(b_list), len(extras), len(out_dtypes)

    if mode == "tn":
        a_spec = pl.BlockSpec((tk, tm), lambda i, j, k: (k, i))
    else:
        a_spec = pl.BlockSpec((tm, tk), lambda i, j, k: (i, k))
    if mode == "nt":
        b_spec = pl.BlockSpec((tn, tk), lambda i, j, k: (j, k))
    else:
        b_spec = pl.BlockSpec((tk, tn), lambda i, j, k: (k, j))
    t_spec = pl.BlockSpec((tm, tn), lambda i, j, k: (i, j))

    def body(*refs):
        a_refs = refs[:na]
        b_refs = refs[na:na + nb]
        e_refs = refs[na + nb:na + nb + ne]
        o_refs = refs[na + nb + ne:na + nb + ne + no]
        acc_refs = refs[na + nb + ne + no:]

        sums = [None] * n_acc
        for ai, bi, ci in products:
            d = _dot(a_refs[ai][...], b_refs[bi][...], mode)
            sums[ci] = d if sums[ci] is None else sums[ci] + d

        def finish(accs):
            outs = epilogue(accs, [e[...] for e in e_refs])
            for o_ref, o in zip(o_refs, outs):
                o_ref[...] = o.astype(o_ref.dtype)

        if nk == 1:
            finish(sums)
        else:
            k = pl.program_id(2)

            @pl.when(k == 0)
            def _():
                for acc, s in zip(acc_refs, sums):
                    acc[...] = s

            @pl.when(k > 0)
            def _():
                for acc, s in zip(acc_refs, sums):
                    acc[...] += s

            @pl.when(k == nk - 1)
            def _():
                finish([acc[...] for acc in acc_refs])

    scratch = [pltpu.VMEM((tm, tn), F32) for _ in range(n_acc)] if nk > 1 else []
    outs = pl.pallas_call(
        body,
        name=name,
        grid=(m_dim // tm, n_dim // tn, nk),
        in_specs=[a_spec] * na + [b_spec] * nb + [t_spec] * ne,
        out_specs=[t_spec] * no,
        out_shape=[jax.ShapeDtypeStruct((m_dim, n_dim), dt) for dt in out_dtypes],
        scratch_shapes=scratch,
        compiler_params=_params(("parallel", "parallel", "arbitrary")),
    )(*a_list, *b_list, *extras)
    return outs


def _identity(accs, extras):
    return accs


def _mm_nn(name, a, b, out_dtype, res=None, alpha=1.0, **tiles):
    if res is None:
        return _matmul(name, "nn", [a], [b], [(0, 0, 0)], [out_dtype], _identity, **tiles)[0]

    def epilogue(accs, extras):
        return [extras[0] + alpha * accs[0]]

    return _matmul(name, "nn", [a], [b], [(0, 0, 0)], [out_dtype], epilogue, extras=(res,), **tiles)[0]


def _mm_nt_sum(name, a_list, b_list, out_dtype, **tiles):
    products = [(i, i, 0) for i in range(len(a_list))]
    return _matmul(name, "nt", a_list, b_list, products, [out_dtype], _identity, **tiles)[0]


def _mm_tn(name, a, b_list, out_dtype, **tiles):
    products = [(0, i, i) for i in range(len(b_list))]
    return _matmul(name, "tn", [a], b_list, products, [out_dtype] * len(b_list), _identity, **tiles)


def _rowwise(name, fn, row_ins, par_ins, row_outs, acc_outs=(), tr=512):
    first = row_ins[0][0] if isinstance(row_ins[0], tuple) else row_ins[0]
    t_dim = first.shape[0]
    tr = _tile(t_dim, tr, 16)
    arrays, specs = [], []
    for r in row_ins:
        if isinstance(r, tuple):
            arr, width, blk = r
            specs.append(pl.BlockSpec((tr, width), lambda i, blk=blk: (i, blk)))
        else:
            arr = r
            specs.append(pl.BlockSpec((tr, arr.shape[1]), lambda i: (i, 0)))
        arrays.append(arr)
    for p in par_ins:
        arrays.append(p)
        specs.append(pl.BlockSpec(p.shape, lambda i, nd=p.ndim: (0,) * nd))
    nr, npar, nro, nacc = len(row_ins), len(par_ins), len(row_outs), len(acc_outs)

    def body(*refs):
        rows = [r[...] for r in refs[:nr]]
        pars = [p[...] for p in refs[nr:nr + npar]]
        o_refs = refs[nr + npar:nr + npar + nro]
        acc_refs = refs[nr + npar + nro:]
        outs, accs = fn(rows, pars)
        for o_ref, o in zip(o_refs, outs):
            o_ref[...] = o.astype(o_ref.dtype)
        if nacc:
            @pl.when(pl.program_id(0) == 0)
            def _():
                for a_ref in acc_refs:
                    a_ref[...] = jnp.zeros_like(a_ref)

            for a_ref, a in zip(acc_refs, accs):
                a_ref[...] += a

    out_shape = [jax.ShapeDtypeStruct((t_dim, c), dt) for c, dt in row_outs]
    out_shape += [jax.ShapeDtypeStruct(s, F32) for s in acc_outs]
    out_specs = [pl.BlockSpec((tr, c), lambda i: (i, 0)) for c, _ in row_outs]
    out_specs += [pl.BlockSpec(s, lambda i: (0, 0)) for s in acc_outs]
    return pl.pallas_call(
        body,
        name=name,
        grid=(t_dim // tr,),
        in_specs=specs,
        out_specs=out_specs,
        out_shape=out_shape,
        compiler_params=_params(("arbitrary",)),
    )(*arrays)


def _rms_stats(x):
    r = lax.rsqrt(jnp.mean(x * x, axis=-1, keepdims=True) + EPS)
    return x * r, r


def _rms_backward(x, g, dy):
    xh, r = _rms_stats(x)
    a = dy * g
    dx = r * (a - xh * jnp.mean(a * xh, axis=-1, keepdims=True))
    return dx, jnp.sum(dy * xh, axis=0, keepdims=True)


def _rms_fwd(name, x, g):
    def fn(rows, pars):
        xh, _ = _rms_stats(rows[0])
        return [xh * pars[0]], []

    return _rowwise(name, fn, [x], [g], [(x.shape[1], BF16)])[0]


def _rms_bwd(name, x, g, dy, dres, scale):
    def fn(rows, pars):
        dx, dg = _rms_backward(rows[0], pars[0], rows[1])
        tot = rows[2] + dx
        return [tot, scale * tot], [dg]

    d = x.shape[1]
    return _rowwise(name, fn, [x, dy, dres], [g], [(d, F32), (d, BF16)], [(1, d)], tr=256)


def _cast_into_gathered(name, w, kind, q_arr):
    rs, cs = w.shape
    tr = _tile(rs, 256, 16)
    nr = rs // tr

    def body(q_ref, w_ref, o_ref):
        o_ref[...] = w_ref[...].astype(BF16)

    if kind == "row":
        o_spec = pl.BlockSpec((tr, cs), lambda i, q_ref: (q_ref[0] * nr + i, 0))
    else:
        o_spec = pl.BlockSpec((tr, cs), lambda i, q_ref: (i, q_ref[0]))
    return pl.pallas_call(
        body,
        name=name,
        grid_spec=pltpu.PrefetchScalarGridSpec(num_scalar_prefetch=1, grid=(nr,),
                                               in_specs=[pl.BlockSpec((tr, cs), lambda i, q_ref: (i, 0))],
                                               out_specs=o_spec),
        out_shape=jax.ShapeDtypeStruct(_full_shape(kind, (rs, cs)), BF16),
        compiler_params=_params(("parallel",)),
    )(q_arr, w)


def _ffn_fwd(tag, h, g, wg, wu, wd):
    n = _rms_fwd(tag + "_norm", h, g)

    def act(accs, extras):
        gate, up = accs
        return [gate * _sigmoid(gate) * up, gate, up]

    a, gate, up = _matmul(tag + "_gateup", "nn", [n], [wg, wu], [(0, 0, 0), (0, 1, 1)], [BF16] * 3, act,
                          tm=1024, tn=512, tk=2048)
    h_out = _mm_nn(tag + "_down", a, wd, F32, res=h, alpha=0.5, tm=512, tn=1024, tk=1408)
    return h_out, (n, a, gate, up)


def _ffn_bwd(tag, h, g, wg, wu, wd, saved, dh, dfb, next_scale):
    n, a, gate, up = saved

    def act_bwd(accs, extras):
        da = accs[0]
        gt, u = extras[0].astype(F32), extras[1].astype(F32)
        s = _sigmoid(gt)
        return [da * u * (s * (1.0 + gt * (1.0 - s))), da * (gt * s)]

    dgp, du = _matmul(tag + "_dact", "nt", [dfb], [wd], [(0, 0, 0)], [BF16, BF16], act_bwd, extras=(gate, up),
                      tm=1024, tn=512, tk=2048)
    (dwd,) = _mm_tn(tag + "_dwd", a, [dfb], BF16, tm=704, tn=1024, tk=1024)
    dwg, dwu = _mm_tn(tag + "_dwgu", n, [dgp, du], BF16, tm=1024, tn=512, tk=1024)
    dn = _mm_nt_sum(tag + "_dn", [dgp, du], [wg, wu], F32, tm=512, tn=1024, tk=1408)
    dh_in, dh_in_b, dg = _rms_bwd(tag + "_dnorm", h, g, dn, dh, next_scale)
    return dh_in, dh_in_b, dg, dwg, dwu, dwd


def _ssm_discretize(log_dt, a_re, a_im, b_re, b_im):
    dt = jnp.exp(log_dt)[:, None]
    lr = jnp.minimum(a_re, -1e-4)
    li = a_im
    mag = jnp.exp(lr * dt)
    ang = li * dt
    abar_r = mag * jnp.cos(ang)
    abar_i = mag * jnp.sin(ang)
    den = lr * lr + li * li
    xr = abar_r - 1.0
    xi = abar_i
    zr = (xr * lr + xi * li) / den
    zi = (xi * lr - xr * li) / den
    bbar_r = zr[..., None] * b_re - zi[..., None] * b_im
    bbar_i = zr[..., None] * b_im + zi[..., None] * b_re
    return abar_r, abar_i, bbar_r, bbar_i


def _blockdiag_in(b):
    g, n, p = b.shape
    nb = g // GROUPS_PER_BLOCK
    eye = jnp.eye(GROUPS_PER_BLOCK, dtype=b.dtype)
    b4 = b.reshape(nb, GROUPS_PER_BLOCK, n, p)
    return jnp.einsum("sgnp,gh->sgphn", b4, eye).reshape(nb, GROUPS_PER_BLOCK * p, GROUPS_PER_BLOCK * n)


def _blockdiag_in_grad(gw, n, p):
    nb = gw.shape[0]
    eye = jnp.eye(GROUPS_PER_BLOCK, dtype=gw.dtype)
    g5 = gw.reshape(nb, GROUPS_PER_BLOCK, p, GROUPS_PER_BLOCK, n)
    return jnp.einsum("sgphn,gh->sgnp", g5, eye).reshape(nb * GROUPS_PER_BLOCK, n, p)


def _blockdiag_out(c):
    g, p, n = c.shape
    nb = g // GROUPS_PER_BLOCK
    eye = jnp.eye(GROUPS_PER_BLOCK, dtype=c.dtype)
    c4 = c.reshape(nb, GROUPS_PER_BLOCK, p, n)
    return jnp.einsum("sgpn,gh->shngp", c4, eye).reshape(nb, GROUPS_PER_BLOCK * n, GROUPS_PER_BLOCK * p)


def _blockdiag_out_grad(gw, p, n):
    nb = gw.shape[0]
    eye = jnp.eye(GROUPS_PER_BLOCK, dtype=gw.dtype)
    g5 = gw.reshape(nb, GROUPS_PER_BLOCK, n, GROUPS_PER_BLOCK, p)
    return jnp.einsum("shngp,gh->sgpn", g5, eye).reshape(nb * GROUPS_PER_BLOCK, p, n)


def _ssm_scan_fwd(z, wb, wc, abar, d):
    t_dim = z.shape[0]
    nb, cb, sw2 = wb.shape
    sw = sw2 // 2
    tt = _tile(t_dim, 512, 8)
    nt = t_dim // tt

    def body(z_ref, wb_ref, wc_ref, a_ref, d_ref, s_ref, y_ref, yg_ref, drive_ref, st_ref):
        @pl.when(pl.program_id(1) == 0)
        def _():
            st_ref[...] = jnp.zeros_like(st_ref)

        u = z_ref[...]
        drive_ref[...] = _dot(u, wb_ref[...], "nn")
        ar = a_ref[:, :sw]
        ai = a_ref[:, sw:]

        def step(r, carry):
            sr, si = carry
            dd = drive_ref[pl.ds(r, 1), :]
            nr = ar * sr - ai * si + dd[:, :sw]
            ni = ar * si + ai * sr + dd[:, sw:]
            s_ref[pl.ds(r, 1), pl.ds(0, sw)] = nr
            s_ref[pl.ds(r, 1), pl.ds(sw, sw)] = ni
            return nr, ni

        sr, si = lax.fori_loop(0, tt, step, (st_ref[:, :sw], st_ref[:, sw:]), unroll=8)
        st_ref[:, pl.ds(0, sw)] = sr
        st_ref[:, pl.ds(sw, sw)] = si
        y = _dot(s_ref[...], wc_ref[...], "nn") + d_ref[...] * u
        y_ref[...] = y
        yg_ref[...] = _gelu(y).astype(BF16)

    return pl.pallas_call(
        body,
        name="ssm_scan_fwd",
        grid=(nb, nt),
        in_specs=[
            pl.BlockSpec((tt, cb), lambda s, t: (t, s)),
            pl.BlockSpec((None, cb, sw2), lambda s, t: (s, 0, 0)),
            pl.BlockSpec((None, sw2, cb), lambda s, t: (s, 0, 0)),
            pl.BlockSpec((None, 1, sw2), lambda s, t: (s, 0, 0)),
            pl.BlockSpec((1, cb), lambda s, t: (0, s)),
        ],
        out_specs=[
            pl.BlockSpec((tt, sw2), lambda s, t: (t, s)),
            pl.BlockSpec((tt, cb), lambda s, t: (t, s)),
            pl.BlockSpec((tt, cb), lambda s, t: (t, s)),
        ],
        out_shape=[
            jax.ShapeDtypeStruct((t_dim, nb * sw2), F32),
            jax.ShapeDtypeStruct((t_dim, nb * cb), F32),
            jax.ShapeDtypeStruct((t_dim, nb * cb), BF16),
        ],
        scratch_shapes=[pltpu.VMEM((tt, sw2), F32), pltpu.VMEM((1, sw2), F32)],
        compiler_params=_params(("parallel", "arbitrary")),
    )(z, wb, wc, abar, d)


def _ssm_scan_bwd(z, dy, states, wb, wc, abar_conj, d):
    t_dim = z.shape[0]
    nb, cb, sw2 = wb.shape
    sw = sw2 // 2
    tt = _tile(t_dim, 512, 8)
    nt = t_dim // tt
    edges = states.reshape(nt, tt, nb * sw2)[:, tt - 1, :]
    before = jnp.concatenate([jnp.zeros((1, nb * sw2), F32), edges[:-1]], axis=0).reshape(nt, 1, nb * sw2)

    def body(z_ref, dy_ref, s_ref, sp_ref, wb_ref, wc_ref, a_ref, d_ref,
             dz_ref, gwb_ref, gwc_ref, ga_ref, gd_ref, gin_ref, gs_ref, st_ref):
        @pl.when(pl.program_id(1) == 0)
        def _():
            st_ref[...] = jnp.zeros_like(st_ref)
            gwb_ref[...] = jnp.zeros_like(gwb_ref)
            gwc_ref[...] = jnp.zeros_like(gwc_ref)
            ga_ref[...] = jnp.zeros_like(ga_ref)
            gd_ref[...] = jnp.zeros_like(gd_ref)

        u = z_ref[...]
        dyv = dy_ref[...]
        gin_ref[...] = _dot(dyv, wc_ref[...], "nt")
        ar = a_ref[:, :sw]
        ai = a_ref[:, sw:]

        def step(k, carry):
            r = tt - 1 - k
            gr, gi = carry
            gg = gin_ref[pl.ds(r, 1), :]
            nr = ar * gr - ai * gi + gg[:, :sw]
            ni = ar * gi + ai * gr + gg[:, sw:]
            gs_ref[pl.ds(r, 1), pl.ds(0, sw)] = nr
            gs_ref[pl.ds(r, 1), pl.ds(sw, sw)] = ni
            return nr, ni

        gr, gi = lax.fori_loop(0, tt, step, (st_ref[:, :sw], st_ref[:, sw:]), unroll=8)
        st_ref[:, pl.ds(0, sw)] = gr
        st_ref[:, pl.ds(sw, sw)] = gi

        gs = gs_ref[...]
        s = s_ref[...]
        dz_ref[...] = (_dot(gs, wb_ref[...], "nt") + d_ref[...] * dyv).astype(BF16)
        gwb_ref[...] += _dot(u, gs, "tn")
        gwc_ref[...] += _dot(s, dyv, "tn")
        gd_ref[...] += jnp.sum(dyv * u, axis=0, keepdims=True)
        row = lax.broadcasted_iota(jnp.int32, s.shape, 0)
        s_prev = jnp.where(row == 0, sp_ref[...], pltpu.roll(s, 1, 0))
        g_r, g_i = gs[:, :sw], gs[:, sw:]
        p_r, p_i = s_prev[:, :sw], s_prev[:, sw:]
        ga_ref[:, pl.ds(0, sw)] += jnp.sum(g_r * p_r + g_i * p_i, axis=0, keepdims=True)
        ga_ref[:, pl.ds(sw, sw)] += jnp.sum(g_i * p_r - g_r * p_i, axis=0, keepdims=True)

    rev = lambda s, t: (nt - 1 - t, s)
    return pl.pallas_call(
        body,
        name="ssm_scan_bwd",
        grid=(nb, nt),
        in_specs=[
            pl.BlockSpec((tt, cb), rev),
            pl.BlockSpec((tt, cb), rev),
            pl.BlockSpec((tt, sw2), rev),
            pl.BlockSpec((None, 1, sw2), lambda s, t: (nt - 1 - t, 0, s)),
            pl.BlockSpec((None, cb, sw2), lambda s, t: (s, 0, 0)),
            pl.BlockSpec((None, sw2, cb), lambda s, t: (s, 0, 0)),
            pl.BlockSpec((None, 1, sw2), lambda s, t: (s, 0, 0)),
            pl.BlockSpec((1, cb), lambda s, t: (0, s)),
        ],
        out_specs=[
            pl.BlockSpec((tt, cb), rev),
            pl.BlockSpec((None, cb, sw2), lambda s, t: (s, 0, 0)),
            pl.BlockSpec((None, sw2, cb), lambda s, t: (s, 0, 0)),
            pl.BlockSpec((None, 1, sw2), lambda s, t: (s, 0, 0)),
            pl.BlockSpec((1, cb), lambda s, t: (0, s)),
        ],
        out_shape=[
            jax.ShapeDtypeStruct((t_dim, nb * cb), BF16),
            jax.ShapeDtypeStruct((nb, cb, sw2), F32),
            jax.ShapeDtypeStruct((nb, sw2, cb), F32),
            jax.ShapeDtypeStruct((nb, 1, sw2), F32),
            jax.ShapeDtypeStruct((1, nb * cb), F32),
        ],
        scratch_shapes=[pltpu.VMEM((tt, sw2), F32), pltpu.VMEM((tt, sw2), F32), pltpu.VMEM((1, sw2), F32)],
        compiler_params=_params(("parallel", "arbitrary")),
    )(z, dy, states, before, wb, wc, abar_conj, d)


def _gmlp_chunk(zu, zv, gv, wm_ref, bias, n_heads):
    ua = _gelu(zu)
    vg = _gelu(zv)
    xc = vg - jnp.mean(vg, axis=-1, keepdims=True)
    r = lax.rsqrt(jnp.mean(xc * xc, axis=-1, keepdims=True) + EPS)
    vh = xc * r
    vb = (vh * gv).astype(BF16)
    parts = []
    for h in range(n_heads):
        cols = slice(h * GMLP_HEAD, (h + 1) * GMLP_HEAD)
        parts.append(_dot(wm_ref[h], vb[:, cols], "nn"))
    s = jnp.concatenate(parts, axis=1) + bias
    return ua, vh, r, vb, s


def _gmlp_fwd(z, gv, wm, bias, ggo):
    t_dim = z.shape[0]
    dg = gv.shape[1]
    n_heads = dg // GMLP_HEAD
    tr = _tile(t_dim, 256, CHUNK)

    def body(zu_ref, zv_ref, gv_ref, wm_ref, b_ref, ggo_ref, o_ref):
        for ck in range(tr // CHUNK):
            rows = pl.ds(ck * CHUNK, CHUNK)
            ua, _, _, _, s = _gmlp_chunk(zu_ref[rows, :], zv_ref[rows, :], gv_ref[...], wm_ref, b_ref[...], n_heads)
            yh, _ = _rms_stats(ua * s)
            o_ref[rows, :] = (yh * ggo_ref[...]).astype(BF16)

    full = lambda a: pl.BlockSpec(a.shape, lambda i, nd=a.ndim: (0,) * nd)
    return pl.pallas_call(
        body,
        name="gmlp_fwd",
        grid=(t_dim // tr,),
        in_specs=[pl.BlockSpec((tr, dg), lambda i: (i, 1)), pl.BlockSpec((tr, dg), lambda i: (i, 2)),
                  full(gv), full(wm), full(bias), full(ggo)],
        out_specs=pl.BlockSpec((tr, dg), lambda i: (i, 0)),
        out_shape=jax.ShapeDtypeStruct((t_dim, dg), BF16),
        compiler_params=_params(("parallel",)),
    )(z, z, gv, wm, bias, ggo)


def _gmlp_bwd(z, dycat, gv, wm, bias, ggo):
    t_dim = z.shape[0]
    dg = gv.shape[1]
    n_heads = dg // GMLP_HEAD
    tr = _tile(t_dim, 256, CHUNK)

    def body(zu_ref, zv_ref, dy_ref, gv_ref, wm_ref, b_ref, ggo_ref,
             dzu_ref, dzv_ref, dggo_ref, dgv_ref, dwm_ref, dsum_ref):
        @pl.when(pl.program_id(0) == 0)
        def _():
            dggo_ref[...] = jnp.zeros_like(dggo_ref)
            dgv_ref[...] = jnp.zeros_like(dgv_ref)
            dwm_ref[...] = jnp.zeros_like(dwm_ref)
            dsum_ref[...] = jnp.zeros_like(dsum_ref)

        for ck in range(tr // CHUNK):
            rows = pl.ds(ck * CHUNK, CHUNK)
            zu = zu_ref[rows, :]
            zv = zv_ref[rows, :]
            gvv = gv_ref[...]
            ua, vh, r, vb, s = _gmlp_chunk(zu, zv, gvv, wm_ref, b_ref[...], n_heads)
            dy, dggo = _rms_backward(ua * s, ggo_ref[...], dy_ref[rows, :])
            dggo_ref[...] += dggo
            ds = dy * ua
            dsum_ref[...] += ds
            dsb = ds.astype(BF16)
            parts = []
            for h in range(n_heads):
                cols = slice(h * GMLP_HEAD, (h + 1) * GMLP_HEAD)
                dwm_ref[h] += _dot(dsb[:, cols], vb[:, cols], "nt")
                parts.append(_dot(wm_ref[h], dsb[:, cols], "tn"))
            dv = jnp.concatenate(parts, axis=1)
            dgv_ref[...] += jnp.sum(dv * vh, axis=0, keepdims=True)
            dvh = dv * gvv
            dvg = r * (dvh - jnp.mean(dvh, axis=-1, keepdims=True) - vh * jnp.mean(dvh * vh, axis=-1, keepdims=True))
            dzv_ref[rows, :] = (dvg * _gelu_grad(zv)).astype(BF16)
            dzu_ref[rows, :] = (dy * s * _gelu_grad(zu)).astype(BF16)

    full = lambda a: pl.BlockSpec(a.shape, lambda i, nd=a.ndim: (0,) * nd)
    return pl.pallas_call(
        body,
        name="gmlp_bwd",
        grid=(t_dim // tr,),
        in_specs=[pl.BlockSpec((tr, dg), lambda i: (i, 1)), pl.BlockSpec((tr, dg), lambda i: (i, 2)),
                  pl.BlockSpec((tr, dg), lambda i: (i, 1)), full(gv), full(wm), full(bias), full(ggo)],
        out_specs=[pl.BlockSpec((tr, dg), lambda i: (i, 0)), pl.BlockSpec((tr, dg), lambda i: (i, 0)),
                   pl.BlockSpec((1, dg), lambda i: (0, 0)), pl.BlockSpec((1, dg), lambda i: (0, 0)),
                   pl.BlockSpec(wm.shape, lambda i: (0, 0, 0)), pl.BlockSpec((CHUNK, dg), lambda i: (0, 0))],
        out_shape=[jax.ShapeDtypeStruct((t_dim, dg), BF16), jax.ShapeDtypeStruct((t_dim, dg), BF16),
                   jax.ShapeDtypeStruct((1, dg), F32), jax.ShapeDtypeStruct((1, dg), F32),
                   jax.ShapeDtypeStruct(wm.shape, F32), jax.ShapeDtypeStruct((CHUNK, dg), F32)],
        compiler_params=_params(("arbitrary",)),
    )(z, z, dycat, gv, wm, bias, ggo)


_ANY = pl.BlockSpec(memory_space=pl.ANY)


def _position():
    x, y, c = lax.axis_index("x"), lax.axis_index("y"), lax.axis_index("c")
    chips = [(1 - x, y), (x, 1 - y), (1 - x, 1 - y)]
    return x, y, c, chips


def _region(ref, kind, shard_shape, q, half):
    rs, cs = shard_shape
    r0, nr = (0, rs) if half is None else (half * (rs // 2), rs // 2)
    if kind == "row":
        return ref.at[pl.ds(q * rs + r0, nr), :]
    return ref.at[pl.ds(r0, nr), pl.ds(q * cs, cs)]


def _full_shape(kind, shard_shape):
    rs, cs = shard_shape
    return (N_CHIPS * rs, cs) if kind == "row" else (rs, N_CHIPS * cs)


def _allgather_weights(gathered, kinds, shapes):
    nw = len(gathered)

    def body(*refs):
        full = refs[nw:2 * nw]
        send_sems, recv_sems = refs[2 * nw:]
        x, y, c, chips = _position()
        q_me = 2 * x + y

        def remote(w, k, region, to):
            return pltpu.make_async_remote_copy(src_ref=region, dst_ref=region, send_sem=send_sems.at[w * 6 + k],
                                                recv_sem=recv_sems.at[w * 6 + k], device_id=to, device_id_type=MESH)

        sent = []
        for w in range(nw):
            mine = _region(full[w], kinds[w], shapes[w], q_me, c)
            for j, (cx, cy) in enumerate(chips):
                cp = remote(w, j, mine, (cx, cy, c))
                cp.start()
                sent.append(cp)
        for w in range(nw):
            for j, (cx, cy) in enumerate(chips):
                landed = _region(full[w], kinds[w], shapes[w], 2 * cx + cy, c)
                remote(w, j, landed, (cx, cy, c)).wait_recv()
                cp = remote(w, 3 + j, landed, (x, y, 1 - c))
                cp.start()
                sent.append(cp)
        for w in range(nw):
            for j, (cx, cy) in enumerate(chips):
                theirs = _region(full[w], kinds[w], shapes[w], 2 * cx + cy, 1 - c)
                remote(w, 3 + j, theirs, (x, y, 1 - c)).wait_recv()
        for cp in sent:
            cp.wait_send()

    return pl.pallas_call(
        body,
        name="allgather_weights",
        in_specs=[_ANY] * nw,
        out_specs=[_ANY] * nw,
        out_shape=[jax.ShapeDtypeStruct(g.shape, g.dtype) for g in gathered],
        input_output_aliases={i: i for i in range(nw)},
        scratch_shapes=[pltpu.SemaphoreType.DMA((6 * nw,)), pltpu.SemaphoreType.DMA((6 * nw,))],
    )(*gathered)


def _reduce_pairs(grads, kinds, shapes):
    nw = len(grads)

    def body(*refs):
        g = refs[:nw]
        got = refs[nw:2 * nw]
        send_sems, recv_sems = refs[2 * nw:]
        x, y, c, _ = _position()
        sent = []
        for w in range(nw):
            for q in range(N_CHIPS):
                cp = pltpu.make_async_remote_copy(
                    src_ref=_region(g[w], kinds[w], shapes[w], q, 1 - c), dst_ref=got[w].at[q],
                    send_sem=send_sems.at[w * N_CHIPS + q], recv_sem=recv_sems.at[w * N_CHIPS + q],
                    device_id=(x, y, 1 - c), device_id_type=MESH)
                cp.start()
                sent.append(cp)
        for cp in sent:
            cp.wait()

    return pl.pallas_call(
        body,
        name="reduce_pairs",
        in_specs=[_ANY] * nw,
        out_specs=[_ANY] * nw,
        out_shape=[jax.ShapeDtypeStruct((N_CHIPS, s[0] // 2, s[1]), BF16) for s in shapes],
        scratch_shapes=[pltpu.SemaphoreType.DMA((N_CHIPS * nw,)), pltpu.SemaphoreType.DMA((N_CHIPS * nw,))],
    )(*grads)


def _pair_sum(name, grad, got, kind, shard_shape, c_arr):
    rs, cs = shard_shape
    hr = rs // 2
    tr = _tile(hr, 512, 16)
    nr = hr // tr

    def body(c_ref, g_ref, s_ref, o_ref):
        o_ref[...] = (g_ref[...].astype(F32) + s_ref[...].astype(F32)).astype(BF16)

    if kind == "row":
        g_spec = pl.BlockSpec((tr, cs), lambda q, i, c_ref: (q * (rs // tr) + c_ref[0] * nr + i, 0))
    else:
        g_spec = pl.BlockSpec((tr, cs), lambda q, i, c_ref: (c_ref[0] * nr + i, q))
    blk = pl.BlockSpec((None, tr, cs), lambda q, i, c_ref: (q, i, 0))
    return pl.pallas_call(
        body,
        name=name,
        grid_spec=pltpu.PrefetchScalarGridSpec(num_scalar_prefetch=1, grid=(N_CHIPS, nr), in_specs=[g_spec, blk],
                                               out_specs=blk),
        out_shape=jax.ShapeDtypeStruct((N_CHIPS, hr, cs), BF16),
        compiler_params=_params(("parallel", "parallel")),
    )(c_arr, grad, got)


def _scatter_to_owners(sums, shapes):
    nw = len(sums)

    def body(*refs):
        ps = refs[:nw]
        got = refs[nw:2 * nw]
        send_sems, recv_sems = refs[2 * nw:]
        x, y, c, chips = _position()
        sent = []
        for w in range(nw):
            for j, (cx, cy) in enumerate(chips):
                cp = pltpu.make_async_remote_copy(
                    src_ref=ps[w].at[2 * cx + cy], dst_ref=got[w].at[j],
                    send_sem=send_sems.at[w * 3 + j], recv_sem=recv_sems.at[w * 3 + j],
                    device_id=(cx, cy, c), device_id_type=MESH)
                cp.start()
                sent.append(cp)
        for cp in sent:
            cp.wait()

    return pl.pallas_call(
        body,
        name="scatter_to_owners",
        in_specs=[_ANY] * nw,
        out_specs=[_ANY] * nw,
        out_shape=[jax.ShapeDtypeStruct((3, s[0] // 2, s[1]), BF16) for s in shapes],
        scratch_shapes=[pltpu.SemaphoreType.DMA((3 * nw,)), pltpu.SemaphoreType.DMA((3 * nw,))],
    )(*sums)


def _owner_sum(name, sums, got, shard_shape, qc_arr):
    rs, cs = shard_shape
    hr = rs // 2
    tr = _tile(hr, 512, 16)
    nr = hr // tr

    def body(qc_ref, mine_ref, got_ref, o_ref):
        acc = mine_ref[...].astype(F32)
        for j in range(3):
            acc = acc + got_ref[j].astype(F32)
        o_ref[...] = acc

    return pl.pallas_call(
        body,
        name=name,
        grid_spec=pltpu.PrefetchScalarGridSpec(
            num_scalar_prefetch=1, grid=(nr,),
            in_specs=[pl.BlockSpec((None, tr, cs), lambda i, qc_ref: (qc_ref[0], i, 0)),
                      pl.BlockSpec((3, tr, cs), lambda i, qc_ref: (0, i, 0))],
            out_specs=pl.BlockSpec((tr, cs), lambda i, qc_ref: (qc_ref[1] * nr + i, 0))),
        out_shape=jax.ShapeDtypeStruct((rs, cs), F32),
        compiler_params=_params(("parallel",)),
    )(qc_arr, sums, got)


def _share_halves(grads, shapes):
    nw = len(grads)

    def body(*refs):
        out = refs[nw:2 * nw]
        send_sems, recv_sems = refs[2 * nw:]
        x, y, c, _ = _position()
        pending = []
        for w in range(nw):
            hr = shapes[w][0] // 2
            mine = out[w].at[pl.ds(c * hr, hr), :]
            theirs = out[w].at[pl.ds((1 - c) * hr, hr), :]
            send = pltpu.make_async_remote_copy(src_ref=mine, dst_ref=mine, send_sem=send_sems.at[w],
                                                recv_sem=recv_sems.at[w], device_id=(x, y, 1 - c), device_id_type=MESH)
            send.start()
            recv = pltpu.make_async_remote_copy(src_ref=theirs, dst_ref=theirs, send_sem=send_sems.at[w],
                                                recv_sem=recv_sems.at[w], device_id=(x, y, 1 - c), device_id_type=MESH)
            pending.append((send, recv))
        for send, recv in pending:
            send.wait_send()
            recv.wait_recv()

    return pl.pallas_call(
        body,
        name="share_halves",
        in_specs=[_ANY] * nw,
        out_specs=[_ANY] * nw,
        out_shape=[jax.ShapeDtypeStruct(s, F32) for s in shapes],
        input_output_aliases={i: i for i in range(nw)},
        scratch_shapes=[pltpu.SemaphoreType.DMA((nw,)), pltpu.SemaphoreType.DMA((nw,))],
    )(*grads)


def _allreduce_small(packed):
    rows = packed.shape[0]

    def body(x_ref, out_ref, gat_ref, send_sems, recv_sems, local_sem):
        x, y, c, chips = _position()
        me, sibling = (x, y, c), (x, y, 1 - c)

        def block(px, py, pc):
            return gat_ref.at[pl.ds((4 * px + 2 * py + pc) * rows, rows), :]

        def copy(k, blk, to, src=None):
            return pltpu.make_async_remote_copy(
                src_ref=block(*blk) if src is None else src, dst_ref=block(*blk), send_sem=send_sems.at[k],
                recv_sem=recv_sems.at[k], device_id=to, device_id_type=MESH)

        mine = pltpu.make_async_copy(x_ref, block(*me), local_sem)
        mine.start()
        first = [copy(0, me, sibling, src=x_ref)]
        first += [copy(1 + j, me, (*chip, c), src=x_ref) for j, chip in enumerate(chips)]
        for cp in first:
            cp.start()
        passed = [copy(4 + j, (*chip, c), sibling) for j, chip in enumerate(chips)]
        for j, chip in enumerate(chips):
            copy(1 + j, (*chip, c), me).wait_recv()
            passed[j].start()
        copy(0, sibling, me).wait_recv()
        for j, chip in enumerate(chips):
            copy(4 + j, (*chip, 1 - c), me).wait_recv()
        for cp in first + passed:
            cp.wait_send()
        mine.wait()
        acc = gat_ref[pl.ds(0, rows), :]
        for k in range(1, N_DEV):
            acc = acc + gat_ref[pl.ds(k * rows, rows), :]
        out_ref[...] = acc

    return pl.pallas_call(
        body,
        name="allreduce_small",
        in_specs=[pl.BlockSpec(memory_space=pltpu.VMEM)],
        out_specs=pl.BlockSpec(memory_space=pltpu.VMEM),
        out_shape=jax.ShapeDtypeStruct(packed.shape, F32),
        scratch_shapes=[pltpu.VMEM((N_DEV * rows, LANES), F32), pltpu.SemaphoreType.DMA((7,)),
                        pltpu.SemaphoreType.DMA((7,)), pltpu.SemaphoreType.DMA],
        compiler_params=pltpu.CompilerParams(vmem_limit_bytes=VMEM_LIMIT_BYTES),
    )(packed)


def _adamw(name, w, g, m, v):
    def fn(rows, pars):
        wv, gv, mv, vv = rows
        m_new = ADAM_B1 * mv + (1.0 - ADAM_B1) * gv
        v_new = ADAM_B2 * vv + (1.0 - ADAM_B2) * (gv * gv)
        m_hat = m_new / (1.0 - ADAM_B1 ** ADAM_STEP)
        v_hat = v_new / (1.0 - ADAM_B2 ** ADAM_STEP)
        delta = -ADAM_LR * (m_hat / (jnp.sqrt(v_hat) + ADAM_EPS) + ADAM_WD * wv)
        return [delta, m_new, v_new], []

    c = w.shape[1]
    return _rowwise(name, fn, [w, g, m, v], [], [(c, F32)] * 3, tr=256)


def _pack(arrays):
    rows = []
    for a in arrays:
        flat = a.reshape(-1).astype(F32)
        pad = (-flat.shape[0]) % LANES
        rows.append(jnp.pad(flat, (0, pad)).reshape(-1, LANES))
    stacked = jnp.concatenate(rows, axis=0)
    pad_rows = (-stacked.shape[0]) % 8
    return jnp.pad(stacked, ((0, pad_rows), (0, 0)))


def _unpack(packed, shapes):
    out, r = [], 0
    for s in shapes:
        n = math.prod(s)
        nr = -(-n // LANES)
        out.append(packed[r:r + nr].reshape(-1)[:n].reshape(s))
        r += nr
    return out


BIG = ["w1_gate", "w1_up", "w1_down", "w_in", "ssm_w_glu", "w_out", "w2_gate", "w2_up", "w2_down", "w_ple_gate",
       "w_ple_proj"]
KIND = {"w1_gate": "col", "w1_up": "col", "w1_down": "row", "w_in": "col", "ssm_w_glu": "row", "w_out": "row",
        "w2_gate": "col", "w2_up": "col", "w2_down": "row", "w_ple_gate": "row", "w_ple_proj": "col"}
SMALL = ["norm_ffn1", "norm_mix", "ssm_log_dt", "ssm_a_re", "ssm_a_im", "ssm_b_re", "ssm_b_im", "ssm_c_re", "ssm_c_im",
         "ssm_d", "gmlp_norm_v", "gmlp_w_s", "gmlp_b_s", "norm_ssm_out", "norm_gmlp_out", "norm_ffn2", "norm_ple",
         "norm_final"]
WEIGHTS = ["norm_ffn1", "w1_gate", "w1_up", "w1_down", "norm_mix", "w_in", "ssm_log_dt", "ssm_a_re", "ssm_a_im",
           "ssm_b_re", "ssm_b_im", "ssm_c_re", "ssm_c_im", "ssm_d", "ssm_w_glu", "gmlp_norm_v", "gmlp_w_s", "gmlp_b_s",
           "norm_ssm_out", "norm_gmlp_out", "w_out", "norm_ffn2", "w2_gate", "w2_up", "w2_down", "norm_ple",
           "w_ple_gate", "w_ple_proj", "norm_final"]


def _step(x, p, tgt, w, m, v):
    d_model = x.shape[1]
    d_ssm = w["ssm_d"].shape[1]
    n_groups = d_ssm // SSM_GROUP
    row = lambda a: a.reshape(1, -1)

    xi, yi, ci = lax.axis_index("x"), lax.axis_index("y"), lax.axis_index("c")
    c_arr = jnp.reshape(ci, (1,)).astype(jnp.int32)
    q_arr = jnp.reshape(2 * xi + yi, (1,)).astype(jnp.int32)
    qc_arr = jnp.stack([2 * xi + yi, ci]).astype(jnp.int32)
    kinds = [KIND[n] for n in BIG]
    shapes = [w[n].shape for n in BIG]
    placed = [_cast_into_gathered("cast_" + n, w[n], KIND[n], q_arr) for n in BIG]
    full = dict(zip(BIG, _allgather_weights(placed, kinds, shapes)))

    h1, ffn1 = _ffn_fwd("ffn1", x, w["norm_ffn1"], full["w1_gate"], full["w1_up"], full["w1_down"])
    nm = _rms_fwd("mix_norm", h1, w["norm_mix"])
    z = _mm_nn("in_proj", nm, full["w_in"], F32, tm=1024, tn=512, tk=2048)

    disc, disc_vjp = jax.vjp(_ssm_discretize, w["ssm_log_dt"][0], w["ssm_a_re"], w["ssm_a_im"], w["ssm_b_re"],
                             w["ssm_b_im"])
    abar_r, abar_i, bbar_r, bbar_i = disc
    nb = n_groups // GROUPS_PER_BLOCK
    wb = jnp.concatenate([_blockdiag_in(bbar_r), _blockdiag_in(bbar_i)], axis=-1).astype(BF16)
    wc = jnp.concatenate([_blockdiag_out(w["ssm_c_re"]), -_blockdiag_out(w["ssm_c_im"])], axis=1).astype(BF16)
    abar = jnp.concatenate([abar_r.reshape(nb, 1, -1), abar_i.reshape(nb, 1, -1)], axis=-1)
    abar_conj = jnp.concatenate([abar_r.reshape(nb, 1, -1), -abar_i.reshape(nb, 1, -1)], axis=-1)
    states, y_pre, yg = _ssm_scan_fwd(z, wb, wc, abar, w["ssm_d"])
    q = _mm_nn("glu_proj", yg, full["ssm_w_glu"], F32, tm=1024, tn=1024, tk=1024)

    def glu_norm(rows, pars):
        yv = _gelu(rows[0]) * _sigmoid(rows[1])
        yh, _ = _rms_stats(yv)
        return [yh * pars[0]], []

    yn_ssm = _rowwise("ssm_glu_norm", glu_norm, [y_pre, q], [w["norm_ssm_out"]], [(d_ssm, BF16)])[0]

    tril = jnp.tril(jnp.ones((CHUNK, CHUNK), dtype=bool))
    wm = jnp.where(tril[None], w["gmlp_w_s"], 0.0).astype(BF16)
    bias = jnp.repeat(w["gmlp_b_s"].T, GMLP_HEAD, axis=1)
    yn_gmlp = _gmlp_fwd(z, w["gmlp_norm_v"], wm, bias, w["norm_gmlp_out"])
    ycat = jnp.concatenate([yn_ssm, yn_gmlp], axis=1)
    h2 = _mm_nn("out_proj", ycat, full["w_out"], F32, res=h1, alpha=1.0, tm=512, tn=1024, tk=2048)

    h3, ffn2 = _ffn_fwd("ffn2", h2, w["norm_ffn2"], full["w2_gate"], full["w2_up"], full["w2_down"])
    npl = _rms_fwd("ple_norm", h3, w["norm_ple"])
    gq = _mm_nn("ple_gate", npl, full["w_ple_gate"], F32, tm=1024, tn=1024, tk=2048)
    pp = _mm_nn("ple_proj", p, full["w_ple_proj"], F32, tm=1024, tn=1024, tk=2048)

    def ple_combine(rows, pars):
        return [rows[0] + _sigmoid(rows[1]) * rows[2]], []

    h4 = _rowwise("ple_combine", ple_combine, [h3, gq, pp], [], [(d_model, F32)], tr=256)[0]

    def head(rows, pars):
        hv, tv = rows
        xh, _ = _rms_stats(hv)
        err = xh * pars[0] - tv
        dx, dg = _rms_backward(hv, pars[0], err * (1.0 / d_model))
        part = 0.5 * jnp.sum(err * err) * (1.0 / d_model)
        return [dx], [dg, jnp.full((1, LANES), part, F32)]

    dh4, g_norm_final, loss_part = _rowwise("loss_head", head, [h4, tgt], [row(w["norm_final"])], [(d_model, F32)],
                                            [(1, d_model), (1, LANES)], tr=256)

    def ple_bwd(rows, pars):
        dh, gqv, ppv = rows
        gate = _sigmoid(gqv)
        return [dh * ppv * gate * (1.0 - gate), dh * gate], []

    dgq, dpp = _rowwise("ple_dgate", ple_bwd, [dh4, gq, pp], [], [(d_model, BF16)] * 2, tr=256)
    (g_w_ple_proj,) = _mm_tn("ple_dwproj", p, [dpp], BF16, tm=256, tn=1024, tk=1024)
    (g_w_ple_gate,) = _mm_tn("ple_dwgate", npl, [dgq], BF16, tm=1024, tn=1024, tk=1024)
    dnpl = _mm_nt_sum("ple_dnorm_in", [dgq], [full["w_ple_gate"]], F32, tm=512, tn=1024, tk=2048)
    dh3, dh3_b, g_norm_ple = _rms_bwd("ple_dnorm", h3, w["norm_ple"], dnpl, dh4, 0.5)

    dh2, dh2_b, g_norm_ffn2, g_w2_gate, g_w2_up, g_w2_down = _ffn_bwd(
        "ffn2", h2, w["norm_ffn2"], full["w2_gate"], full["w2_up"], full["w2_down"], ffn2, dh3, dh3_b, 1.0)

    dycat = _mm_nt_sum("out_dproj", [dh2_b], [full["w_out"]], F32, tm=512, tn=1024, tk=2048)
    (g_w_out,) = _mm_tn("out_dw", ycat, [dh2_b], BF16, tm=1024, tn=1024, tk=1024)

    dzu, dzv, g_norm_gmlp_out, g_gmlp_norm_v, g_wm, g_s = _gmlp_bwd(z, dycat, w["gmlp_norm_v"], wm, bias,
                                                                  w["norm_gmlp_out"])
    g_gmlp_w_s = jnp.where(tril[None], g_wm, 0.0)
    g_gmlp_b_s = g_s.reshape(CHUNK, -1, GMLP_HEAD).sum(axis=-1).T

    def glu_bwd(rows, pars):
        dyn, ypre, qv = rows
        ygv = _gelu(ypre)
        sg = _sigmoid(qv)
        dy, dg = _rms_backward(ygv * sg, pars[0], dyn)
        return [dy * ygv * sg * (1.0 - sg), dy * sg], [dg]

    dq, dyg_part, g_norm_ssm_out = _rowwise("ssm_dglu", glu_bwd, [(dycat, d_ssm, 0), y_pre, q], [w["norm_ssm_out"]],
                                            [(d_ssm, BF16), (d_ssm, F32)], [(1, d_ssm)])
    dyg_proj = _mm_nt_sum("glu_dproj", [dq], [full["ssm_w_glu"]], F32, tm=1024, tn=1024, tk=1024)
    (g_ssm_w_glu,) = _mm_tn("glu_dw", yg, [dq], BF16, tm=1024, tn=1024, tk=1024)

    def gelu_bwd(rows, pars):
        return [(rows[0] + rows[1]) * _gelu_grad(rows[2])], []

    dy_pre = _rowwise("ssm_dgelu", gelu_bwd, [dyg_part, dyg_proj, y_pre], [], [(d_ssm, F32)])[0]
    dz_ssm, g_wb, g_wc, g_abar, g_ssm_d = _ssm_scan_bwd(z, dy_pre, states, wb, wc, abar_conj, w["ssm_d"])
    sw = g_abar.shape[-1] // 2
    g_bbar_r = _blockdiag_in_grad(g_wb[..., :sw], SSM_STATE, SSM_GROUP)
    g_bbar_i = _blockdiag_in_grad(g_wb[..., sw:], SSM_STATE, SSM_GROUP)
    g_ssm_c_re = _blockdiag_out_grad(g_wc[:, :sw, :], SSM_GROUP, SSM_STATE)
    g_ssm_c_im = -_blockdiag_out_grad(g_wc[:, sw:, :], SSM_GROUP, SSM_STATE)
    g_abar_r = g_abar[..., :sw].reshape(n_groups, SSM_STATE)
    g_abar_i = g_abar[..., sw:].reshape(n_groups, SSM_STATE)
    g_ssm_log_dt, g_ssm_a_re, g_ssm_a_im, g_ssm_b_re, g_ssm_b_im = disc_vjp((g_abar_r, g_abar_i, g_bbar_r, g_bbar_i))

    dz = jnp.concatenate([dz_ssm, dzu, dzv], axis=1)
    (g_w_in,) = _mm_tn("in_dw", nm, [dz], BF16, tm=1024, tn=768, tk=1024)
    dnm = _mm_nt_sum("in_dproj", [dz], [full["w_in"]], F32, tm=512, tn=1024, tk=1024)
    dh1, dh1_b, g_norm_mix = _rms_bwd("mix_dnorm", h1, w["norm_mix"], dnm, dh2, 0.5)
    dx, _, g_norm_ffn1, g_w1_gate, g_w1_up, g_w1_down = _ffn_bwd(
        "ffn1", x, w["norm_ffn1"], full["w1_gate"], full["w1_up"], full["w1_down"], ffn1, dh1, dh1_b, 1.0)

    local = {"w1_gate": g_w1_gate, "w1_up": g_w1_up, "w1_down": g_w1_down, "w_in": g_w_in, "ssm_w_glu": g_ssm_w_glu,
             "w_out": g_w_out, "w2_gate": g_w2_gate, "w2_up": g_w2_up, "w2_down": g_w2_down,
             "w_ple_gate": g_w_ple_gate, "w_ple_proj": g_w_ple_proj}
    grads_full = [local[n] for n in BIG]
    from_sibling = _reduce_pairs(grads_full, kinds, shapes)
    sums = [_pair_sum("pair_sum_" + n, g, s, k, sh, c_arr)
            for n, g, s, k, sh in zip(BIG, grads_full, from_sibling, kinds, shapes)]
    from_chips = _scatter_to_owners(sums, shapes)
    halves = [_owner_sum("owner_sum_" + n, ps, got, sh, qc_arr) for n, ps, got, sh in zip(BIG, sums, from_chips, shapes)]
    grad = dict(zip(BIG, _share_halves(halves, shapes)))

    small = {"norm_ffn1": g_norm_ffn1, "norm_mix": g_norm_mix, "ssm_log_dt": g_ssm_log_dt, "ssm_a_re": g_ssm_a_re,
             "ssm_a_im": g_ssm_a_im, "ssm_b_re": g_ssm_b_re, "ssm_b_im": g_ssm_b_im, "ssm_c_re": g_ssm_c_re,
             "ssm_c_im": g_ssm_c_im, "ssm_d": g_ssm_d, "gmlp_norm_v": g_gmlp_norm_v, "gmlp_w_s": g_gmlp_w_s,
             "gmlp_b_s": g_gmlp_b_s, "norm_ssm_out": g_norm_ssm_out, "norm_gmlp_out": g_norm_gmlp_out,
             "norm_ffn2": g_norm_ffn2, "norm_ple": g_norm_ple, "norm_final": g_norm_final}
    small_shapes = [w[n].shape for n in SMALL]
    reduced = _allreduce_small(_pack([small[n] for n in SMALL] + [loss_part[:, :1]]))
    small_grads = _unpack(reduced, small_shapes + [(1,)])
    loss = small_grads[-1].reshape(())
    for n, g in zip(SMALL, small_grads[:-1]):
        grad[n] = g

    delta, new_m, new_v = {}, {}, {}
    for n in BIG:
        delta[n], new_m[n], new_v[n] = _adamw("adamw_" + n, w[n], grad[n], m[n], v[n])
    g_packed = reduced[: _pack([small[n] for n in SMALL]).shape[0]]
    d_p, m_p, v_p = _adamw("adamw_small", _pack([w[n] for n in SMALL]), g_packed, _pack([m[n] for n in SMALL]),
                           _pack([v[n] for n in SMALL]))
    for name_list, packed in ((delta, d_p), (new_m, m_p), (new_v, v_p)):
        for n, a in zip(SMALL, _unpack(packed, small_shapes)):
            name_list[n] = a
    return loss, dx, grad, delta, new_m, new_v


def kernel(x, p, norm_ffn1, w1_gate, w1_up, w1_down, norm_mix, w_in, ssm_log_dt, ssm_a_re, ssm_a_im, ssm_b_re, ssm_b_im, ssm_c_re, ssm_c_im, ssm_d, ssm_w_glu, gmlp_norm_v, gmlp_w_s, gmlp_b_s, norm_ssm_out, norm_gmlp_out, w_out, norm_ffn2, w2_gate, w2_up, w2_down, norm_ple, w_ple_gate, w_ple_proj, norm_final, loss_target, m_norm_ffn1, m_w1_gate, m_w1_up, m_w1_down, m_norm_mix, m_w_in, m_ssm_log_dt, m_ssm_a_re, m_ssm_a_im, m_ssm_b_re, m_ssm_b_im, m_ssm_c_re, m_ssm_c_im, m_ssm_d, m_ssm_w_glu, m_gmlp_norm_v, m_gmlp_w_s, m_gmlp_b_s, m_norm_ssm_out, m_norm_gmlp_out, m_w_out, m_norm_ffn2, m_w2_gate, m_w2_up, m_w2_down, m_norm_ple, m_w_ple_gate, m_w_ple_proj, m_norm_final, v_norm_ffn1, v_w1_gate, v_w1_up, v_w1_down, v_norm_mix, v_w_in, v_ssm_log_dt, v_ssm_a_re, v_ssm_a_im, v_ssm_b_re, v_ssm_b_im, v_ssm_c_re, v_ssm_c_im, v_ssm_d, v_ssm_w_glu, v_gmlp_norm_v, v_gmlp_w_s, v_gmlp_b_s, v_norm_ssm_out, v_norm_gmlp_out, v_w_out, v_norm_ffn2, v_w2_gate, v_w2_up, v_w2_down, v_norm_ple, v_w_ple_gate, v_w_ple_proj, v_norm_final):
    given = dict(locals())
    shapes = {n: given[n].shape for n in WEIGHTS}

    def block(name):
        a = given[name]
        if a.ndim == 1:
            return a.reshape(1, -1)
        return a[0] if a.ndim >= 3 else a

    w = {n: block(n) for n in WEIGHTS}
    m = {n: block("m_" + n) for n in WEIGHTS}
    v = {n: block("v_" + n) for n in WEIGHTS}
    loss, dx, grad, delta, new_m, new_v = _step(x[0], p[0, 0], loss_target[0], w, m, v)
    outs = [loss, dx[None]]
    for tree in (grad, delta, new_m, new_v):
        outs += [tree[n].reshape(shapes[n]) for n in WEIGHTS]
    return tuple(outs)
```

```python
import functools
import math

import jax
import jax.numpy as jnp
from jax import lax
from jax.experimental import pallas as pl
from jax.experimental.pallas import tpu as pltpu

F32 = jnp.float32
BF16 = jnp.bfloat16
EPS = 1e-6
SSM_GROUP = 16
SSM_STATE = 64
GROUPS_PER_BLOCK = 8
GMLP_HEAD = 128
CHUNK = 128
ADAM_LR = 0.001
ADAM_B1 = 0.9
ADAM_B2 = 0.999
ADAM_EPS = 1e-08
ADAM_WD = 0.01
ADAM_STEP = 10
N_CHIPS = 4
N_DEV = 8
LANES = 128
VMEM_LIMIT_BYTES = 56 * 1024 * 1024
MESH = pl.DeviceIdType.MESH
GELU_C = math.sqrt(2.0 / math.pi)
GELU_A = 0.044715

_DOT_DIMS = {
    "nn": (((1,), (0,)), ((), ())),
    "nt": (((1,), (1,)), ((), ())),
    "tn": (((0,), (0,)), ((), ())),
}


def _tile(dim, pref, align):
    if dim <= pref:
        return dim
    t = (pref // align) * align
    while t >= align:
        if dim % t == 0:
            return t
        t -= align
    return dim


def _params(semantics):
    return pltpu.CompilerParams(dimension_semantics=semantics, vmem_limit_bytes=VMEM_LIMIT_BYTES)


def _gelu(x):
    return 0.5 * x * (1.0 + jnp.tanh(GELU_C * (x + GELU_A * x * x * x)))


def _gelu_grad(x):
    t = jnp.tanh(GELU_C * (x + GELU_A * x * x * x))
    return 0.5 * (1.0 + t) + 0.5 * x * (1.0 - t * t) * GELU_C * (1.0 + 3.0 * GELU_A * x * x)


def _sigmoid(x):
    return 1.0 / (1.0 + jnp.exp(-x))


def _dot(a, b, mode):
    return lax.dot_general(a.astype(BF16), b.astype(BF16), _DOT_DIMS[mode], preferred_element_type=F32)


class _Carry:
    def __init__(self, arrays, out_shapes, aliases, n_copies, copies):
        self.arrays = list(arrays)
        self.out_shapes = list(out_shapes)
        self.aliases = dict(aliases)
        self.n_copies = n_copies
        self.copies = copies

    def scratch(self):
        return [pltpu.SemaphoreType.DMA((self.n_copies,)), pltpu.SemaphoreType.DMA((self.n_copies,))]

    def split(self, refs):
        n_in, n_out = len(self.arrays), len(self.out_shapes)
        return refs[:n_in], refs[n_in:n_in + n_out], refs[n_in + n_out], refs[n_in + n_out + 1]

    def start(self, refs):
        for cp in self.copies(*self.split(refs)):
            cp.start()

    def wait(self, refs):
        for cp in self.copies(*self.split(refs)):
            cp.wait()


_ANY = pl.BlockSpec(memory_space=pl.ANY)


def _comm_call(name, carry):
    def body(*refs):
        carry.start(refs)
        carry.wait(refs)

    n_in = len(carry.arrays)
    return pl.pallas_call(
        body,
        name=name,
        in_specs=[_ANY] * n_in,
        out_specs=[_ANY] * len(carry.out_shapes),
        out_shape=carry.out_shapes,
        input_output_aliases=carry.aliases,
        scratch_shapes=carry.scratch(),
    )(*carry.arrays)


def _carried_call(body, carry, *, name, grid, in_specs, out_specs, out_shape, scratch_shapes, semantics, args):
    if carry is None:
        res = pl.pallas_call(body, name=name, grid=grid, in_specs=in_specs, out_specs=out_specs, out_shape=out_shape,
                             scratch_shapes=scratch_shapes, compiler_params=_params(semantics))(*args)
        return res, []
    n_in, n_out, n_scr = len(in_specs), len(out_specs), len(scratch_shapes)
    nci, nco = len(carry.arrays), len(carry.out_shapes)

    def wrapped(*refs):
        ins = refs[:n_in]
        outs = refs[n_in + nci:n_in + nci + n_out]
        scr = refs[n_in + nci + n_out + nco:n_in + nci + n_out + nco + n_scr]
        c_refs = (refs[n_in:n_in + nci] + refs[n_in + nci + n_out:n_in + nci + n_out + nco]
                  + refs[n_in + nci + n_out + nco + n_scr:])
        first = functools.reduce(jnp.logical_and, [pl.program_id(d) == 0 for d in range(len(grid))])
        last = functools.reduce(jnp.logical_and, [pl.program_id(d) == grid[d] - 1 for d in range(len(grid))])

        @pl.when(first)
        def _():
            carry.start(c_refs)

        body(*ins, *outs, *scr)

        @pl.when(last)
        def _():
            carry.wait(c_refs)

    res = pl.pallas_call(
        wrapped,
        name=name,
        grid=grid,
        in_specs=list(in_specs) + [_ANY] * nci,
        out_specs=list(out_specs) + [_ANY] * nco,
        out_shape=list(out_shape) + carry.out_shapes,
        input_output_aliases={n_in + i: n_out + o for i, o in carry.aliases.items()},
        scratch_shapes=list(scratch_shapes) + carry.scratch(),
        compiler_params=_params(("arbitrary",) * len(grid)),
    )(*args, *carry.arrays)
    return res[:n_out], res[n_out:]


def _matmul(name, mode, a_list, b_list, products, out_dtypes, epilogue, extras=(), tm=512, tn=512, tk=2048,
            carry=None):
    a0, b0 = a_list[0], b_list[0]
    if mode == "tn":
        k_dim, m_dim = a0.shape
    else:
        m_dim, k_dim = a0.shape
    n_dim = b0.shape[0] if mode == "nt" else b0.shape[1]
    tm = _tile(m_dim, tm, LANES)
    tn = _tile(n_dim, tn, LANES)
    tk = _tile(k_dim, tk, LANES)
    nk = k_dim // tk
    n_acc = 1 + max(p[2] for p in products)
    na, nb, ne, no = len(a_list), len(b_list), len(extras), len(out_dtypes)

    if mode == "tn":
        a_spec = pl.BlockSpec((tk, tm), lambda i, j, k: (k, i))
    else:
        a_spec = pl.BlockSpec((tm, tk), lambda i, j, k: (i, k))
    if mode == "nt":
        b_spec = pl.BlockSpec((tn, tk), lambda i, j, k: (j, k))
    else:
        b_spec = pl.BlockSpec((tk, tn), lambda i, j, k: (k, j))
    t_spec = pl.BlockSpec((tm, tn), lambda i, j, k: (i, j))

    def body(*refs):
        a_refs = refs[:na]
        b_refs = refs[na:na + nb]
        e_refs = refs[na + nb:na + nb + ne]
        o_refs = refs[na + nb + ne:na + nb + ne + no]
        acc_refs = refs[na + nb + ne + no:]

        sums = [None] * n_acc
        for ai, bi, ci in products:
            d = _dot(a_refs[ai][...], b_refs[bi][...], mode)
            sums[ci] = d if sums[ci] is None else sums[ci] + d

        def finish(accs):
            outs = epilogue(accs, [e[...] for e in e_refs])
            for o_ref, o in zip(o_refs, outs):
                o_ref[...] = o.astype(o_ref.dtype)

        if nk == 1:
            finish(sums)
        else:
            k = pl.program_id(2)

            @pl.when(k == 0)
            def _():
                for acc, s in zip(acc_refs, sums):
                    acc[...] = s

            @pl.when(k > 0)
            def _():
                for acc, s in zip(acc_refs, sums):
                    acc[...] += s

            @pl.when(k == nk - 1)
            def _():
                finish([acc[...] for acc in acc_refs])

    scratch = [pltpu.VMEM((tm, tn), F32) for _ in range(n_acc)] if nk > 1 else []
    outs, carried = _carried_call(
        body, carry,
        name=name,
        grid=(m_dim // tm, n_dim // tn, nk),
        in_specs=[a_spec] * na + [b_spec] * nb + [t_spec] * ne,
        out_specs=[t_spec] * no,
        out_shape=[jax.ShapeDtypeStruct((m_dim, n_dim), dt) for dt in out_dtypes],
        scratch_shapes=scratch,
        semantics=("parallel", "parallel", "arbitrary"),
        args=[*a_list, *b_list, *extras],
    )
    return (outs, carried) if carry else outs


def _identity(accs, extras):
    return accs


def _single(result, carry):
    return (result[0][0], result[1]) if carry else result[0]


def _mm_nn(name, a, b, out_dtype, res=None, alpha=1.0, carry=None, **tiles):
    if res is None:
        return _single(_matmul(name, "nn", [a], [b], [(0, 0, 0)], [out_dtype], _identity, carry=carry, **tiles), carry)

    def epilogue(accs, extras):
        return [extras[0] + alpha * accs[0]]

    return _single(_matmul(name, "nn", [a], [b], [(0, 0, 0)], [out_dtype], epilogue, extras=(res,), carry=carry,
                           **tiles), carry)


def _mm_nt_sum(name, a_list, b_list, out_dtype, carry=None, **tiles):
    products = [(i, i, 0) for i in range(len(a_list))]
    return _single(_matmul(name, "nt", a_list, b_list, products, [out_dtype], _identity, carry=carry, **tiles), carry)


def _mm_tn(name, a, b_list, out_dtype, carry=None, **tiles):
    products = [(0, i, i) for i in range(len(b_list))]
    return _matmul(name, "tn", [a], b_list, products, [out_dtype] * len(b_list), _identity, carry=carry, **tiles)


def _rowwise(name, fn, row_ins, par_ins, row_outs, acc_outs=(), tr=512):
    first = row_ins[0][0] if isinstance(row_ins[0], tuple) else row_ins[0]
    t_dim = first.shape[0]
    tr = _tile(t_dim, tr, 16)
    arrays, specs = [], []
    for r in row_ins:
        if isinstance(r, tuple):
            arr, width, blk = r
            specs.append(pl.BlockSpec((tr, width), lambda i, blk=blk: (i, blk)))
        else:
            arr = r
            specs.append(pl.BlockSpec((tr, arr.shape[1]), lambda i: (i, 0)))
        arrays.append(arr)
    for p in par_ins:
        arrays.append(p)
        specs.append(pl.BlockSpec(p.shape, lambda i, nd=p.ndim: (0,) * nd))
    nr, npar, nro, nacc = len(row_ins), len(par_ins), len(row_outs), len(acc_outs)

    def body(*refs):
        rows = [r[...] for r in refs[:nr]]
        pars = [p[...] for p in refs[nr:nr + npar]]
        o_refs = refs[nr + npar:nr + npar + nro]
        acc_refs = refs[nr + npar + nro:]
        outs, accs = fn(rows, pars)
        for o_ref, o in zip(o_refs, outs):
            o_ref[...] = o.astype(o_ref.dtype)
        if nacc:
            @pl.when(pl.program_id(0) == 0)
            def _():
                for a_ref in acc_refs:
                    a_ref[...] = jnp.zeros_like(a_ref)

            for a_ref, a in zip(acc_refs, accs):
                a_ref[...] += a

    out_shape = [jax.ShapeDtypeStruct((t_dim, c), dt) for c, dt in row_outs]
    out_shape += [jax.ShapeDtypeStruct(s, F32) for s in acc_outs]
    out_specs = [pl.BlockSpec((tr, c), lambda i: (i, 0)) for c, _ in row_outs]
    out_specs += [pl.BlockSpec(s, lambda i: (0, 0)) for s in acc_outs]
    return pl.pallas_call(
        body,
        name=name,
        grid=(t_dim // tr,),
        in_specs=specs,
        out_specs=out_specs,
        out_shape=out_shape,
        compiler_params=_params(("arbitrary",)),
    )(*arrays)


def _rms_stats(x):
    r = lax.rsqrt(jnp.mean(x * x, axis=-1, keepdims=True) + EPS)
    return x * r, r


def _rms_backward(x, g, dy):
    xh, r = _rms_stats(x)
    a = dy * g
    dx = r * (a - xh * jnp.mean(a * xh, axis=-1, keepdims=True))
    return dx, jnp.sum(dy * xh, axis=0, keepdims=True)


def _rms_fwd(name, x, g):
    def fn(rows, pars):
        xh, _ = _rms_stats(rows[0])
        return [xh * pars[0]], []

    return _rowwise(name, fn, [x], [g], [(x.shape[1], BF16)])[0]


def _rms_bwd(name, x, g, dy, dres, scale):
    def fn(rows, pars):
        dx, dg = _rms_backward(rows[0], pars[0], rows[1])
        tot = rows[2] + dx
        return [tot, scale * tot], [dg]

    d = x.shape[1]
    return _rowwise(name, fn, [x, dy, dres], [g], [(d, F32), (d, BF16)], [(1, d)], tr=256)


def _cast_into_gathered(name, w, kind, q_arr):
    rs, cs = w.shape
    tr = _tile(rs, 256, 16)
    nr = rs // tr

    def body(q_ref, w_ref, o_ref):
        o_ref[...] = w_ref[...].astype(BF16)

    if kind == "row":
        o_spec = pl.BlockSpec((tr, cs), lambda i, q_ref: (q_ref[0] * nr + i, 0))
    else:
        o_spec = pl.BlockSpec((tr, cs), lambda i, q_ref: (i, q_ref[0]))
    return pl.pallas_call(
        body,
        name=name,
        grid_spec=pltpu.PrefetchScalarGridSpec(num_scalar_prefetch=1, grid=(nr,),
                                               in_specs=[pl.BlockSpec((tr, cs), lambda i, q_ref: (i, 0))],
                                               out_specs=o_spec),
        out_shape=jax.ShapeDtypeStruct(_full_shape(kind, (rs, cs)), BF16),
        compiler_params=_params(("parallel",)),
    )(q_arr, w)


def _ffn_gateup(tag, n, wg, wu, carry):
    def act(accs, extras):
        gate, up = accs
        return [gate * _sigmoid(gate) * up, gate, up]

    return _matmul(tag + "_gateup", "nn", [n], [wg, wu], [(0, 0, 0), (0, 1, 1)], [BF16] * 3, act,
                   tm=1024, tn=512, tk=2048, carry=carry)


def _ffn_down(tag, a, wd, h, carry=None):
    return _mm_nn(tag + "_down", a, wd, F32, res=h, alpha=0.5, tm=512, tn=512, tk=5632, carry=carry)


def _ffn_bwd(tag, names, h, g, wg, wu, wd, saved, dh, dfb, next_scale, reducer):
    n, a, gate, up = saved

    def act_bwd(accs, extras):
        da = accs[0]
        gt, u = extras[0].astype(F32), extras[1].astype(F32)
        s = _sigmoid(gt)
        return [da * u * (s * (1.0 + gt * (1.0 - s))), da * (gt * s)]

    dgp, du = _matmul(tag + "_dact", "nt", [dfb], [wd], [(0, 0, 0)], [BF16, BF16], act_bwd, extras=(gate, up),
                      tm=1024, tn=512, tk=2048)
    (dwd,) = _mm_tn(tag + "_dwd", a, [dfb], BF16, tm=512, tn=2048, tk=2048)
    trip = reducer.begin(names[2:], [dwd])
    (dwg, dwu), got = _mm_tn(tag + "_dwgu", n, [dgp, du], BF16, tm=1024, tn=512, tk=2048, carry=trip.carry)
    reducer.end(trip, got)
    trip = reducer.begin(names[:2], [dwg, dwu])
    dn, got = _mm_nt_sum(tag + "_dn", [dgp, du], [wg, wu], F32, tm=1024, tn=1024, tk=1408, carry=trip.carry)
    reducer.end(trip, got)
    dh_in, dh_in_b, dg = _rms_bwd(tag + "_dnorm", h, g, dn, dh, next_scale)
    return dh_in, dh_in_b, dg


def _ssm_discretize(log_dt, a_re, a_im, b_re, b_im):
    dt = jnp.exp(log_dt)[:, None]
    lr = jnp.minimum(a_re, -1e-4)
    li = a_im
    mag = jnp.exp(lr * dt)
    ang = li * dt
    abar_r = mag * jnp.cos(ang)
    abar_i = mag * jnp.sin(ang)
    den = lr * lr + li * li
    xr = abar_r - 1.0
    xi = abar_i
    zr = (xr * lr + xi * li) / den
    zi = (xi * lr - xr * li) / den
    bbar_r = zr[..., None] * b_re - zi[..., None] * b_im
    bbar_i = zr[..., None] * b_im + zi[..., None] * b_re
    return abar_r, abar_i, bbar_r, bbar_i


def _blockdiag_in(b):
    g, n, p = b.shape
    nb = g // GROUPS_PER_BLOCK
    eye = jnp.eye(GROUPS_PER_BLOCK, dtype=b.dtype)
    b4 = b.reshape(nb, GROUPS_PER_BLOCK, n, p)
    return jnp.einsum("sgnp,gh->sgphn", b4, eye).reshape(nb, GROUPS_PER_BLOCK * p, GROUPS_PER_BLOCK * n)


def _blockdiag_in_grad(gw, n, p):
    nb = gw.shape[0]
    eye = jnp.eye(GROUPS_PER_BLOCK, dtype=gw.dtype)
    g5 = gw.reshape(nb, GROUPS_PER_BLOCK, p, GROUPS_PER_BLOCK, n)
    return jnp.einsum("sgphn,gh->sgnp", g5, eye).reshape(nb * GROUPS_PER_BLOCK, n, p)


def _blockdiag_out(c):
    g, p, n = c.shape
    nb = g // GROUPS_PER_BLOCK
    eye = jnp.eye(GROUPS_PER_BLOCK, dtype=c.dtype)
    c4 = c.reshape(nb, GROUPS_PER_BLOCK, p, n)
    return jnp.einsum("sgpn,gh->shngp", c4, eye).reshape(nb, GROUPS_PER_BLOCK * n, GROUPS_PER_BLOCK * p)


def _blockdiag_out_grad(gw, p, n):
    nb = gw.shape[0]
    eye = jnp.eye(GROUPS_PER_BLOCK, dtype=gw.dtype)
    g5 = gw.reshape(nb, GROUPS_PER_BLOCK, n, GROUPS_PER_BLOCK, p)
    return jnp.einsum("shngp,gh->sgpn", g5, eye).reshape(nb * GROUPS_PER_BLOCK, p, n)


def _ssm_scan_fwd(z, wb, wc, abar, d, carry=None):
    t_dim = z.shape[0]
    nb, cb, sw2 = wb.shape
    sw = sw2 // 2
    tt = _tile(t_dim, 512, 8)
    nt = t_dim // tt

    def body(z_ref, wb_ref, wc_ref, a_ref, d_ref, s_ref, y_ref, yg_ref, drive_ref, st_ref):
        @pl.when(pl.program_id(1) == 0)
        def _():
            st_ref[...] = jnp.zeros_like(st_ref)

        u = z_ref[...]
        drive_ref[...] = _dot(u, wb_ref[...], "nn")
        ar = a_ref[:, :sw]
        ai = a_ref[:, sw:]

        def step(r, carry):
            sr, si = carry
            dd = drive_ref[pl.ds(r, 1), :]
            nr = ar * sr - ai * si + dd[:, :sw]
            ni = ar * si + ai * sr + dd[:, sw:]
            s_ref[pl.ds(r, 1), pl.ds(0, sw)] = nr
            s_ref[pl.ds(r, 1), pl.ds(sw, sw)] = ni
            return nr, ni

        sr, si = lax.fori_loop(0, tt, step, (st_ref[:, :sw], st_ref[:, sw:]), unroll=8)
        st_ref[:, pl.ds(0, sw)] = sr
        st_ref[:, pl.ds(sw, sw)] = si
        y = _dot(s_ref[...], wc_ref[...], "nn") + d_ref[...] * u
        y_ref[...] = y
        yg_ref[...] = _gelu(y).astype(BF16)

    return _carried_call(
        body, carry,
        name="ssm_scan_fwd",
        grid=(nb, nt),
        in_specs=[
            pl.BlockSpec((tt, cb), lambda s, t: (t, s)),
            pl.BlockSpec((None, cb, sw2), lambda s, t: (s, 0, 0)),
            pl.BlockSpec((None, sw2, cb), lambda s, t: (s, 0, 0)),
            pl.BlockSpec((None, 1, sw2), lambda s, t: (s, 0, 0)),
            pl.BlockSpec((1, cb), lambda s, t: (0, s)),
        ],
        out_specs=[
            pl.BlockSpec((tt, sw2), lambda s, t: (t, s)),
            pl.BlockSpec((tt, cb), lambda s, t: (t, s)),
            pl.BlockSpec((tt, cb), lambda s, t: (t, s)),
        ],
        out_shape=[
            jax.ShapeDtypeStruct((t_dim, nb * sw2), F32),
            jax.ShapeDtypeStruct((t_dim, nb * cb), F32),
            jax.ShapeDtypeStruct((t_dim, nb * cb), BF16),
        ],
        scratch_shapes=[pltpu.VMEM((tt, sw2), F32), pltpu.VMEM((1, sw2), F32)],
        semantics=("parallel", "arbitrary"),
        args=[z, wb, wc, abar, d],
    )


def _ssm_scan_bwd(z, dy, states, wb, wc, abar_conj, d, carry=None):
    t_dim = z.shape[0]
    nb, cb, sw2 = wb.shape
    sw = sw2 // 2
    tt = _tile(t_dim, 512, 8)
    nt = t_dim // tt
    edges = states.reshape(nt, tt, nb * sw2)[:, tt - 1, :]
    before = jnp.concatenate([jnp.zeros((1, nb * sw2), F32), edges[:-1]], axis=0).reshape(nt, 1, nb * sw2)

    def body(z_ref, dy_ref, s_ref, sp_ref, wb_ref, wc_ref, a_ref, d_ref,
             dz_ref, gwb_ref, gwc_ref, ga_ref, gd_ref, gin_ref, gs_ref, st_ref):
        @pl.when(pl.program_id(1) == 0)
        def _():
            st_ref[...] = jnp.zeros_like(st_ref)
            gwb_ref[...] = jnp.zeros_like(gwb_ref)
            gwc_ref[...] = jnp.zeros_like(gwc_ref)
            ga_ref[...] = jnp.zeros_like(ga_ref)
            gd_ref[...] = jnp.zeros_like(gd_ref)

        u = z_ref[...]
        dyv = dy_ref[...]
        gin_ref[...] = _dot(dyv, wc_ref[...], "nt")
        ar = a_ref[:, :sw]
        ai = a_ref[:, sw:]

        def step(k, carry):
            r = tt - 1 - k
            gr, gi = carry
            gg = gin_ref[pl.ds(r, 1), :]
            nr = ar * gr - ai * gi + gg[:, :sw]
            ni = ar * gi + ai * gr + gg[:, sw:]
            gs_ref[pl.ds(r, 1), pl.ds(0, sw)] = nr
            gs_ref[pl.ds(r, 1), pl.ds(sw, sw)] = ni
            return nr, ni

        gr, gi = lax.fori_loop(0, tt, step, (st_ref[:, :sw], st_ref[:, sw:]), unroll=8)
        st_ref[:, pl.ds(0, sw)] = gr
        st_ref[:, pl.ds(sw, sw)] = gi

        gs = gs_ref[...]
        s = s_ref[...]
        dz_ref[...] = (_dot(gs, wb_ref[...], "nt") + d_ref[...] * dyv).astype(BF16)
        gwb_ref[...] += _dot(u, gs, "tn")
        gwc_ref[...] += _dot(s, dyv, "tn")
        gd_ref[...] += jnp.sum(dyv * u, axis=0, keepdims=True)
        row = lax.broadcasted_iota(jnp.int32, s.shape, 0)
        s_prev = jnp.where(row == 0, sp_ref[...], pltpu.roll(s, 1, 0))
        g_r, g_i = gs[:, :sw], gs[:, sw:]
        p_r, p_i = s_prev[:, :sw], s_prev[:, sw:]
        ga_ref[:, pl.ds(0, sw)] += jnp.sum(g_r * p_r + g_i * p_i, axis=0, keepdims=True)
        ga_ref[:, pl.ds(sw, sw)] += jnp.sum(g_i * p_r - g_r * p_i, axis=0, keepdims=True)

    rev = lambda s, t: (nt - 1 - t, s)
    return _carried_call(
        body, carry,
        name="ssm_scan_bwd",
        grid=(nb, nt),
        in_specs=[
            pl.BlockSpec((tt, cb), rev),
            pl.BlockSpec((tt, cb), rev),
            pl.BlockSpec((tt, sw2), rev),
            pl.BlockSpec((None, 1, sw2), lambda s, t: (nt - 1 - t, 0, s)),
            pl.BlockSpec((None, cb, sw2), lambda s, t: (s, 0, 0)),
            pl.BlockSpec((None, sw2, cb), lambda s, t: (s, 0, 0)),
            pl.BlockSpec((None, 1, sw2), lambda s, t: (s, 0, 0)),
            pl.BlockSpec((1, cb), lambda s, t: (0, s)),
        ],
        out_specs=[
            pl.BlockSpec((tt, cb), rev),
            pl.BlockSpec((None, cb, sw2), lambda s, t: (s, 0, 0)),
            pl.BlockSpec((None, sw2, cb), lambda s, t: (s, 0, 0)),
            pl.BlockSpec((None, 1, sw2), lambda s, t: (s, 0, 0)),
            pl.BlockSpec((1, cb), lambda s, t: (0, s)),
        ],
        out_shape=[
            jax.ShapeDtypeStruct((t_dim, nb * cb), BF16),
            jax.ShapeDtypeStruct((nb, cb, sw2), F32),
            jax.ShapeDtypeStruct((nb, sw2, cb), F32),
            jax.ShapeDtypeStruct((nb, 1, sw2), F32),
            jax.ShapeDtypeStruct((1, nb * cb), F32),
        ],
        scratch_shapes=[pltpu.VMEM((tt, sw2), F32), pltpu.VMEM((tt, sw2), F32), pltpu.VMEM((1, sw2), F32)],
        semantics=("parallel", "arbitrary"),
        args=[z, dy, states, before, wb, wc, abar_conj, d],
    )


def _gmlp_chunk(zu, zv, gv, wm_ref, bias, n_heads):
    ua = _gelu(zu)
    vg = _gelu(zv)
    xc = vg - jnp.mean(vg, axis=-1, keepdims=True)
    r = lax.rsqrt(jnp.mean(xc * xc, axis=-1, keepdims=True) + EPS)
    vh = xc * r
    vb = (vh * gv).astype(BF16)
    parts = []
    for h in range(n_heads):
        cols = slice(h * GMLP_HEAD, (h + 1) * GMLP_HEAD)
        parts.append(_dot(wm_ref[h], vb[:, cols], "nn"))
    s = jnp.concatenate(parts, axis=1) + bias
    return ua, vh, r, vb, s


def _gmlp_fwd(z, gv, wm, bias, ggo):
    t_dim = z.shape[0]
    dg = gv.shape[1]
    n_heads = dg // GMLP_HEAD
    tr = _tile(t_dim, 256, CHUNK)

    def body(zu_ref, zv_ref, gv_ref, wm_ref, b_ref, ggo_ref, o_ref):
        for ck in range(tr // CHUNK):
            rows = pl.ds(ck * CHUNK, CHUNK)
            ua, _, _, _, s = _gmlp_chunk(zu_ref[rows, :], zv_ref[rows, :], gv_ref[...], wm_ref, b_ref[...], n_heads)
            yh, _ = _rms_stats(ua * s)
            o_ref[rows, :] = (yh * ggo_ref[...]).astype(BF16)

    full = lambda a: pl.BlockSpec(a.shape, lambda i, nd=a.ndim: (0,) * nd)
    return pl.pallas_call(
        body,
        name="gmlp_fwd",
        grid=(t_dim // tr,),
        in_specs=[pl.BlockSpec((tr, dg), lambda i: (i, 1)), pl.BlockSpec((tr, dg), lambda i: (i, 2)),
                  full(gv), full(wm), full(bias), full(ggo)],
        out_specs=pl.BlockSpec((tr, dg), lambda i: (i, 0)),
        out_shape=jax.ShapeDtypeStruct((t_dim, dg), BF16),
        compiler_params=_params(("parallel",)),
    )(z, z, gv, wm, bias, ggo)


def _gmlp_bwd(z, dycat, gv, wm, bias, ggo):
    t_dim = z.shape[0]
    dg = gv.shape[1]
    n_heads = dg // GMLP_HEAD
    tr = _tile(t_dim, 256, CHUNK)

    def body(zu_ref, zv_ref, dy_ref, gv_ref, wm_ref, b_ref, ggo_ref,
             dzu_ref, dzv_ref, dggo_ref, dgv_ref, dwm_ref, dsum_ref):
        @pl.when(pl.program_id(0) == 0)
        def _():
            dggo_ref[...] = jnp.zeros_like(dggo_ref)
            dgv_ref[...] = jnp.zeros_like(dgv_ref)
            dwm_ref[...] = jnp.zeros_like(dwm_ref)
            dsum_ref[...] = jnp.zeros_like(dsum_ref)

        for ck in range(tr // CHUNK):
            rows = pl.ds(ck * CHUNK, CHUNK)
            zu = zu_ref[rows, :]
            zv = zv_ref[rows, :]
            gvv = gv_ref[...]
            ua, vh, r, vb, s = _gmlp_chunk(zu, zv, gvv, wm_ref, b_ref[...], n_heads)
            dy, dggo = _rms_backward(ua * s, ggo_ref[...], dy_ref[rows, :])
            dggo_ref[...] += dggo
            ds = dy * ua
            dsum_ref[...] += ds
            dsb = ds.astype(BF16)
            parts = []
            for h in range(n_heads):
                cols = slice(h * GMLP_HEAD, (h + 1) * GMLP_HEAD)
                dwm_ref[h] += _dot(dsb[:, cols], vb[:, cols], "nt")
                parts.append(_dot(wm_ref[h], dsb[:, cols], "tn"))
            dv = jnp.concatenate(parts, axis=1)
            dgv_ref[...] += jnp.sum(dv * vh, axis=0, keepdims=True)
            dvh = dv * gvv
            dvg = r * (dvh - jnp.mean(dvh, axis=-1, keepdims=True) - vh * jnp.mean(dvh * vh, axis=-1, keepdims=True))
            dzv_ref[rows, :] = (dvg * _gelu_grad(zv)).astype(BF16)
            dzu_ref[rows, :] = (dy * s * _gelu_grad(zu)).astype(BF16)

    full = lambda a: pl.BlockSpec(a.shape, lambda i, nd=a.ndim: (0,) * nd)
    return pl.pallas_call(
        body,
        name="gmlp_bwd",
        grid=(t_dim // tr,),
        in_specs=[pl.BlockSpec((tr, dg), lambda i: (i, 1)), pl.BlockSpec((tr, dg), lambda i: (i, 2)),
                  pl.BlockSpec((tr, dg), lambda i: (i, 1)), full(gv), full(wm), full(bias), full(ggo)],
        out_specs=[pl.BlockSpec((tr, dg), lambda i: (i, 0)), pl.BlockSpec((tr, dg), lambda i: (i, 0)),
                   pl.BlockSpec((1, dg), lambda i: (0, 0)), pl.BlockSpec((1, dg), lambda i: (0, 0)),
                   pl.BlockSpec(wm.shape, lambda i: (0, 0, 0)), pl.BlockSpec((CHUNK, dg), lambda i: (0, 0))],
        out_shape=[jax.ShapeDtypeStruct((t_dim, dg), BF16), jax.ShapeDtypeStruct((t_dim, dg), BF16),
                   jax.ShapeDtypeStruct((1, dg), F32), jax.ShapeDtypeStruct((1, dg), F32),
                   jax.ShapeDtypeStruct(wm.shape, F32), jax.ShapeDtypeStruct((CHUNK, dg), F32)],
        compiler_params=_params(("arbitrary",)),
    )(z, z, dycat, gv, wm, bias, ggo)


def _position():
    x, y, c = lax.axis_index("x"), lax.axis_index("y"), lax.axis_index("c")
    chips = [(1 - x, y), (x, 1 - y), (1 - x, 1 - y)]
    return x, y, c, chips


def _region(ref, kind, shard_shape, q, half):
    rs, cs = shard_shape
    r0, nr = (0, rs) if half is None else (half * (rs // 2), rs // 2)
    if kind == "row":
        return ref.at[pl.ds(q * rs + r0, nr), :]
    return ref.at[pl.ds(r0, nr), pl.ds(q * cs, cs)]


def _full_shape(kind, shard_shape):
    rs, cs = shard_shape
    return (N_CHIPS * rs, cs) if kind == "row" else (rs, N_CHIPS * cs)


def _remote(src, dst, send_sems, recv_sems, k, to):
    return pltpu.make_async_remote_copy(src_ref=src, dst_ref=dst, send_sem=send_sems.at[k], recv_sem=recv_sems.at[k],
                                        device_id=to, device_id_type=MESH)


def _same(arrays):
    return [jax.ShapeDtypeStruct(a.shape, a.dtype) for a in arrays]


def _gather_ici_carry(gathered, kinds, shapes):
    nw = len(gathered)

    def copies(ops, full, send_sems, recv_sems):
        x, y, c, chips = _position()
        out = []
        for w in range(nw):
            mine = _region(full[w], kinds[w], shapes[w], 2 * x + y, c)
            for j, (cx, cy) in enumerate(chips):
                out.append(_remote(mine, mine, send_sems, recv_sems, 3 * w + j, (cx, cy, c)))
        return out

    return _Carry(gathered, _same(gathered), {i: i for i in range(nw)}, 3 * nw, copies)


def _gather_d2d_carry(gathered, kinds, shapes):
    nw = len(gathered)

    def copies(ops, full, send_sems, recv_sems):
        x, y, c, chips = _position()
        out = []
        for w in range(nw):
            for j, (cx, cy) in enumerate(chips):
                landed = _region(full[w], kinds[w], shapes[w], 2 * cx + cy, c)
                out.append(_remote(landed, landed, send_sems, recv_sems, 3 * w + j, (x, y, 1 - c)))
        return out

    return _Carry(gathered, _same(gathered), {i: i for i in range(nw)}, 3 * nw, copies)


def _pairs_carry(grads, kinds, shapes):
    nw = len(grads)

    def copies(g, got, send_sems, recv_sems):
        x, y, c, _ = _position()
        out = []
        for w in range(nw):
            for q in range(N_CHIPS):
                out.append(_remote(_region(g[w], kinds[w], shapes[w], q, 1 - c), got[w].at[q], send_sems, recv_sems,
                                   N_CHIPS * w + q, (x, y, 1 - c)))
        return out

    outs = [jax.ShapeDtypeStruct((N_CHIPS, s[0] // 2, s[1]), BF16) for s in shapes]
    return _Carry(grads, outs, {}, N_CHIPS * nw, copies)


def _pair_sum(name, grad, got, kind, shard_shape, c_arr):
    rs, cs = shard_shape
    hr = rs // 2
    tr = _tile(hr, 512, 16)
    nr = hr // tr

    def body(c_ref, g_ref, s_ref, o_ref):
        o_ref[...] = (g_ref[...].astype(F32) + s_ref[...].astype(F32)).astype(BF16)

    if kind == "row":
        g_spec = pl.BlockSpec((tr, cs), lambda q, i, c_ref: (q * (rs // tr) + c_ref[0] * nr + i, 0))
    else:
        g_spec = pl.BlockSpec((tr, cs), lambda q, i, c_ref: (c_ref[0] * nr + i, q))
    blk = pl.BlockSpec((None, tr, cs), lambda q, i, c_ref: (q, i, 0))
    return pl.pallas_call(
        body,
        name=name,
        grid_spec=pltpu.PrefetchScalarGridSpec(num_scalar_prefetch=1, grid=(N_CHIPS, nr), in_specs=[g_spec, blk],
                                               out_specs=blk),
        out_shape=jax.ShapeDtypeStruct((N_CHIPS, hr, cs), BF16),
        compiler_params=_params(("parallel", "parallel")),
    )(c_arr, grad, got)


def _scatter_carry(sums, shapes):
    nw = len(sums)

    def copies(ps, got, send_sems, recv_sems):
        x, y, c, chips = _position()
        out = []
        for w in range(nw):
            for j, (cx, cy) in enumerate(chips):
                out.append(_remote(ps[w].at[2 * cx + cy], got[w].at[j], send_sems, recv_sems, 3 * w + j, (cx, cy, c)))
        return out

    outs = [jax.ShapeDtypeStruct((3, s[0] // 2, s[1]), BF16) for s in shapes]
    return _Carry(sums, outs, {}, 3 * nw, copies)


def _owner_sum(name, sums, got, shard_shape, qc_arr):
    rs, cs = shard_shape
    hr = rs // 2
    tr = _tile(hr, 512, 16)
    nr = hr // tr

    def body(qc_ref, mine_ref, got_ref, o_ref):
        acc = mine_ref[...].astype(F32)
        for j in range(3):
            acc = acc + got_ref[j].astype(F32)
        o_ref[...] = acc

    return pl.pallas_call(
        body,
        name=name,
        grid_spec=pltpu.PrefetchScalarGridSpec(
            num_scalar_prefetch=1, grid=(nr,),
            in_specs=[pl.BlockSpec((None, tr, cs), lambda i, qc_ref: (qc_ref[0], i, 0)),
                      pl.BlockSpec((3, tr, cs), lambda i, qc_ref: (0, i, 0))],
            out_specs=pl.BlockSpec((tr, cs), lambda i, qc_ref: (qc_ref[1] * nr + i, 0))),
        out_shape=jax.ShapeDtypeStruct((rs, cs), F32),
        compiler_params=_params(("parallel",)),
    )(qc_arr, sums, got)


def _share_carry(grads, shapes):
    nw = len(grads)

    def copies(ops, out, send_sems, recv_sems):
        x, y, c, _ = _position()
        res = []
        for w in range(nw):
            hr = shapes[w][0] // 2
            mine = out[w].at[pl.ds(c * hr, hr), :]
            res.append(_remote(mine, mine, send_sems, recv_sems, w, (x, y, 1 - c)))
        return res

    return _Carry(grads, _same(grads), {i: i for i in range(nw)}, nw, copies)


def _allreduce_small(packed):
    rows = packed.shape[0]

    def body(x_ref, out_ref, gat_ref, send_sems, recv_sems, local_sem):
        x, y, c, chips = _position()
        me, sibling = (x, y, c), (x, y, 1 - c)

        def block(px, py, pc):
            return gat_ref.at[pl.ds((4 * px + 2 * py + pc) * rows, rows), :]

        def copy(k, blk, to, src=None):
            return pltpu.make_async_remote_copy(
                src_ref=block(*blk) if src is None else src, dst_ref=block(*blk), send_sem=send_sems.at[k],
                recv_sem=recv_sems.at[k], device_id=to, device_id_type=MESH)

        mine = pltpu.make_async_copy(x_ref, block(*me), local_sem)
        mine.start()
        first = [copy(0, me, sibling, src=x_ref)]
        first += [copy(1 + j, me, (*chip, c), src=x_ref) for j, chip in enumerate(chips)]
        for cp in first:
            cp.start()
        passed = [copy(4 + j, (*chip, c), sibling) for j, chip in enumerate(chips)]
        for j, chip in enumerate(chips):
            copy(1 + j, (*chip, c), me).wait_recv()
            passed[j].start()
        copy(0, sibling, me).wait_recv()
        for j, chip in enumerate(chips):
            copy(4 + j, (*chip, 1 - c), me).wait_recv()
        for cp in first + passed:
            cp.wait_send()
        mine.wait()
        acc = gat_ref[pl.ds(0, rows), :]
        for k in range(1, N_DEV):
            acc = acc + gat_ref[pl.ds(k * rows, rows), :]
        out_ref[...] = acc

    return pl.pallas_call(
        body,
        name="allreduce_small",
        in_specs=[pl.BlockSpec(memory_space=pltpu.VMEM)],
        out_specs=pl.BlockSpec(memory_space=pltpu.VMEM),
        out_shape=jax.ShapeDtypeStruct(packed.shape, F32),
        scratch_shapes=[pltpu.VMEM((N_DEV * rows, LANES), F32), pltpu.SemaphoreType.DMA((7,)),
                        pltpu.SemaphoreType.DMA((7,)), pltpu.SemaphoreType.DMA],
        compiler_params=pltpu.CompilerParams(vmem_limit_bytes=VMEM_LIMIT_BYTES),
    )(packed)


def _adamw(name, w, g, m, v):
    def fn(rows, pars):
        wv, gv, mv, vv = rows
        m_new = ADAM_B1 * mv + (1.0 - ADAM_B1) * gv
        v_new = ADAM_B2 * vv + (1.0 - ADAM_B2) * (gv * gv)
        m_hat = m_new / (1.0 - ADAM_B1 ** ADAM_STEP)
        v_hat = v_new / (1.0 - ADAM_B2 ** ADAM_STEP)
        delta = -ADAM_LR * (m_hat / (jnp.sqrt(v_hat) + ADAM_EPS) + ADAM_WD * wv)
        return [delta, m_new, v_new], []

    c = w.shape[1]
    return _rowwise(name, fn, [w, g, m, v], [], [(c, F32)] * 3, tr=256)


def _pack(arrays):
    rows = []
    for a in arrays:
        flat = a.reshape(-1).astype(F32)
        pad = (-flat.shape[0]) % LANES
        rows.append(jnp.pad(flat, (0, pad)).reshape(-1, LANES))
    stacked = jnp.concatenate(rows, axis=0)
    pad_rows = (-stacked.shape[0]) % 8
    return jnp.pad(stacked, ((0, pad_rows), (0, 0)))


def _unpack(packed, shapes):
    out, r = [], 0
    for s in shapes:
        n = math.prod(s)
        nr = -(-n // LANES)
        out.append(packed[r:r + nr].reshape(-1)[:n].reshape(s))
        r += nr
    return out


BIG = ["w1_gate", "w1_up", "w1_down", "w_in", "ssm_w_glu", "w_out", "w2_gate", "w2_up", "w2_down", "w_ple_gate",
       "w_ple_proj"]
KIND = {"w1_gate": "col", "w1_up": "col", "w1_down": "row", "w_in": "col", "ssm_w_glu": "row", "w_out": "row",
        "w2_gate": "col", "w2_up": "col", "w2_down": "row", "w_ple_gate": "row", "w_ple_proj": "col"}
SMALL = ["norm_ffn1", "norm_mix", "ssm_log_dt", "ssm_a_re", "ssm_a_im", "ssm_b_re", "ssm_b_im", "ssm_c_re", "ssm_c_im",
         "ssm_d", "gmlp_norm_v", "gmlp_w_s", "gmlp_b_s", "norm_ssm_out", "norm_gmlp_out", "norm_ffn2", "norm_ple",
         "norm_final"]
WEIGHTS = ["norm_ffn1", "w1_gate", "w1_up", "w1_down", "norm_mix", "w_in", "ssm_log_dt", "ssm_a_re", "ssm_a_im",
           "ssm_b_re", "ssm_b_im", "ssm_c_re", "ssm_c_im", "ssm_d", "ssm_w_glu", "gmlp_norm_v", "gmlp_w_s", "gmlp_b_s",
           "norm_ssm_out", "norm_gmlp_out", "w_out", "norm_ffn2", "w2_gate", "w2_up", "w2_down", "norm_ple",
           "w_ple_gate", "w_ple_proj", "norm_final"]


class _Trip:
    def __init__(self, names, sums, carry):
        self.names, self.sums, self.carry = names, sums, carry


class _Reducer:
    def __init__(self, shard_shape, c_arr, qc_arr):
        self.shard_shape, self.c_arr, self.qc_arr = shard_shape, c_arr, qc_arr
        self.halves = {}

    def begin(self, names, grads):
        kinds = [KIND[n] for n in names]
        shapes = [self.shard_shape[n] for n in names]
        swapped = _comm_call("pairs_" + names[0], _pairs_carry(grads, kinds, shapes))
        sums = [_pair_sum("pair_sum_" + n, g, s, k, sh, self.c_arr)
                for n, g, s, k, sh in zip(names, grads, swapped, kinds, shapes)]
        return _Trip(names, sums, _scatter_carry(sums, shapes))

    def end(self, trip, got):
        for n, ps, g in zip(trip.names, trip.sums, got):
            self.halves[n] = _owner_sum("owner_sum_" + n, ps, g, self.shard_shape[n], self.qc_arr)


def _step(x, p, tgt, w, m, v):
    d_model = x.shape[1]
    d_ssm = w["ssm_d"].shape[1]
    n_groups = d_ssm // SSM_GROUP
    row = lambda a: a.reshape(1, -1)

    xi, yi, ci = lax.axis_index("x"), lax.axis_index("y"), lax.axis_index("c")
    c_arr = jnp.reshape(ci, (1,)).astype(jnp.int32)
    q_arr = jnp.reshape(2 * xi + yi, (1,)).astype(jnp.int32)
    qc_arr = jnp.stack([2 * xi + yi, ci]).astype(jnp.int32)
    shard_shape = {n: w[n].shape for n in BIG}
    full = {n: _cast_into_gathered("cast_" + n, w[n], KIND[n], q_arr) for n in BIG}

    def gather(stage, names):
        return stage([full[n] for n in names], [KIND[n] for n in names], [shard_shape[n] for n in names])

    def gathered(names, arrays):
        full.update(zip(names, arrays))

    first, second = ["w1_gate", "w1_up"], ["w1_down", "w_in"]
    third, fourth = ["ssm_w_glu", "w_out", "w2_gate"], ["w2_up", "w2_down", "w_ple_gate", "w_ple_proj"]
    gathered(first, _comm_call("gather_first_ici", gather(_gather_ici_carry, first)))
    gathered(first, _comm_call("gather_first_d2d", gather(_gather_d2d_carry, first)))
    n1 = _rms_fwd("ffn1_norm", x, w["norm_ffn1"])
    (a1, gate1, up1), landed = _ffn_gateup("ffn1", n1, full["w1_gate"], full["w1_up"],
                                           gather(_gather_ici_carry, second))
    gathered(second, landed)
    gathered(second, _comm_call("gather_second_d2d", gather(_gather_d2d_carry, second)))
    h1, landed = _ffn_down("ffn1", a1, full["w1_down"], x, gather(_gather_ici_carry, third))
    gathered(third, landed)
    ffn1 = (n1, a1, gate1, up1)
    nm = _rms_fwd("mix_norm", h1, w["norm_mix"])
    z, landed = _mm_nn("in_proj", nm, full["w_in"], F32, tm=1024, tn=512, tk=2048,
                       carry=gather(_gather_d2d_carry, third))
    gathered(third, landed)

    disc, disc_vjp = jax.vjp(_ssm_discretize, w["ssm_log_dt"][0], w["ssm_a_re"], w["ssm_a_im"], w["ssm_b_re"],
                             w["ssm_b_im"])
    abar_r, abar_i, bbar_r, bbar_i = disc
    nb = n_groups // GROUPS_PER_BLOCK
    wb = jnp.concatenate([_blockdiag_in(bbar_r), _blockdiag_in(bbar_i)], axis=-1).astype(BF16)
    wc = jnp.concatenate([_blockdiag_out(w["ssm_c_re"]), -_blockdiag_out(w["ssm_c_im"])], axis=1).astype(BF16)
    abar = jnp.concatenate([abar_r.reshape(nb, 1, -1), abar_i.reshape(nb, 1, -1)], axis=-1)
    abar_conj = jnp.concatenate([abar_r.reshape(nb, 1, -1), -abar_i.reshape(nb, 1, -1)], axis=-1)
    (states, y_pre, yg), landed = _ssm_scan_fwd(z, wb, wc, abar, w["ssm_d"], gather(_gather_ici_carry, fourth))
    gathered(fourth, landed)
    q = _mm_nn("glu_proj", yg, full["ssm_w_glu"], F32, tm=1024, tn=1024, tk=1024)

    def glu_norm(rows, pars):
        yv = _gelu(rows[0]) * _sigmoid(rows[1])
        yh, _ = _rms_stats(yv)
        return [yh * pars[0]], []

    yn_ssm = _rowwise("ssm_glu_norm", glu_norm, [y_pre, q], [w["norm_ssm_out"]], [(d_ssm, BF16)])[0]

    tril = jnp.tril(jnp.ones((CHUNK, CHUNK), dtype=bool))
    wm = jnp.where(tril[None], w["gmlp_w_s"], 0.0).astype(BF16)
    bias = jnp.repeat(w["gmlp_b_s"].T, GMLP_HEAD, axis=1)
    yn_gmlp = _gmlp_fwd(z, w["gmlp_norm_v"], wm, bias, w["norm_gmlp_out"])
    ycat = jnp.concatenate([yn_ssm, yn_gmlp], axis=1)
    h2, landed = _mm_nn("out_proj", ycat, full["w_out"], F32, res=h1, alpha=1.0, tm=512, tn=1024, tk=2048,
                        carry=gather(_gather_d2d_carry, fourth))
    gathered(fourth, landed)

    n2 = _rms_fwd("ffn2_norm", h2, w["norm_ffn2"])
    a2, gate2, up2 = _ffn_gateup("ffn2", n2, full["w2_gate"], full["w2_up"], None)
    h3 = _ffn_down("ffn2", a2, full["w2_down"], h2)
    ffn2 = (n2, a2, gate2, up2)
    npl = _rms_fwd("ple_norm", h3, w["norm_ple"])
    gq = _mm_nn("ple_gate", npl, full["w_ple_gate"], F32, tm=1024, tn=1024, tk=2048)
    pp = _mm_nn("ple_proj", p, full["w_ple_proj"], F32, tm=1024, tn=1024, tk=2048)

    def ple_combine(rows, pars):
        return [rows[0] + _sigmoid(rows[1]) * rows[2]], []

    h4 = _rowwise("ple_combine", ple_combine, [h3, gq, pp], [], [(d_model, F32)], tr=256)[0]

    def head(rows, pars):
        hv, tv = rows
        xh, _ = _rms_stats(hv)
        err = xh * pars[0] - tv
        dx, dg = _rms_backward(hv, pars[0], err * (1.0 / d_model))
        part = 0.5 * jnp.sum(err * err) * (1.0 / d_model)
        return [dx], [dg, jnp.full((1, LANES), part, F32)]

    dh4, g_norm_final, loss_part = _rowwise("loss_head", head, [h4, tgt], [row(w["norm_final"])], [(d_model, F32)],
                                            [(1, d_model), (1, LANES)], tr=256)

    def ple_bwd(rows, pars):
        dh, gqv, ppv = rows
        gate = _sigmoid(gqv)
        return [dh * ppv * gate * (1.0 - gate), dh * gate], []

    dgq, dpp = _rowwise("ple_dgate", ple_bwd, [dh4, gq, pp], [], [(d_model, BF16)] * 2, tr=256)
    reducer = _Reducer(shard_shape, c_arr, qc_arr)
    (g_w_ple_proj,) = _mm_tn("ple_dwproj", p, [dpp], BF16, tm=256, tn=1024, tk=1024)
    (g_w_ple_gate,) = _mm_tn("ple_dwgate", npl, [dgq], BF16, tm=1024, tn=1024, tk=1024)
    trip = reducer.begin(["w_ple_gate", "w_ple_proj"], [g_w_ple_gate, g_w_ple_proj])
    dnpl, got = _mm_nt_sum("ple_dnorm_in", [dgq], [full["w_ple_gate"]], F32, tm=512, tn=1024, tk=2048, carry=trip.carry)
    reducer.end(trip, got)
    dh3, dh3_b, g_norm_ple = _rms_bwd("ple_dnorm", h3, w["norm_ple"], dnpl, dh4, 0.5)

    dh2, dh2_b, g_norm_ffn2 = _ffn_bwd("ffn2", ["w2_gate", "w2_up", "w2_down"], h2, w["norm_ffn2"], full["w2_gate"],
                                       full["w2_up"], full["w2_down"], ffn2, dh3, dh3_b, 1.0, reducer)

    dycat = _mm_nt_sum("out_dproj", [dh2_b], [full["w_out"]], F32, tm=512, tn=1024, tk=2048)
    (g_w_out,) = _mm_tn("out_dw", ycat, [dh2_b], BF16, tm=1024, tn=1024, tk=1024)

    dzu, dzv, g_norm_gmlp_out, g_gmlp_norm_v, g_wm, g_s = _gmlp_bwd(z, dycat, w["gmlp_norm_v"], wm, bias,
                                                                  w["norm_gmlp_out"])
    g_gmlp_w_s = jnp.where(tril[None], g_wm, 0.0)
    g_gmlp_b_s = g_s.reshape(CHUNK, -1, GMLP_HEAD).sum(axis=-1).T

    def glu_bwd(rows, pars):
        dyn, ypre, qv = rows
        ygv = _gelu(ypre)
        sg = _sigmoid(qv)
        dy, dg = _rms_backward(ygv * sg, pars[0], dyn)
        return [dy * ygv * sg * (1.0 - sg), dy * sg], [dg]

    dq, dyg_part, g_norm_ssm_out = _rowwise("ssm_dglu", glu_bwd, [(dycat, d_ssm, 0), y_pre, q], [w["norm_ssm_out"]],
                                            [(d_ssm, BF16), (d_ssm, F32)], [(1, d_ssm)])
    dyg_proj = _mm_nt_sum("glu_dproj", [dq], [full["ssm_w_glu"]], F32, tm=1024, tn=1024, tk=1024)
    (g_ssm_w_glu,) = _mm_tn("glu_dw", yg, [dq], BF16, tm=1024, tn=1024, tk=1024)

    def gelu_bwd(rows, pars):
        return [(rows[0] + rows[1]) * _gelu_grad(rows[2])], []

    dy_pre = _rowwise("ssm_dgelu", gelu_bwd, [dyg_part, dyg_proj, y_pre], [], [(d_ssm, F32)])[0]
    trip = reducer.begin(["w_out", "ssm_w_glu"], [g_w_out, g_ssm_w_glu])
    (dz_ssm, g_wb, g_wc, g_abar, g_ssm_d), got = _ssm_scan_bwd(z, dy_pre, states, wb, wc, abar_conj, w["ssm_d"],
                                                              trip.carry)
    reducer.end(trip, got)
    sw = g_abar.shape[-1] // 2
    g_bbar_r = _blockdiag_in_grad(g_wb[..., :sw], SSM_STATE, SSM_GROUP)
    g_bbar_i = _blockdiag_in_grad(g_wb[..., sw:], SSM_STATE, SSM_GROUP)
    g_ssm_c_re = _blockdiag_out_grad(g_wc[:, :sw, :], SSM_GROUP, SSM_STATE)
    g_ssm_c_im = -_blockdiag_out_grad(g_wc[:, sw:, :], SSM_GROUP, SSM_STATE)
    g_abar_r = g_abar[..., :sw].reshape(n_groups, SSM_STATE)
    g_abar_i = g_abar[..., sw:].reshape(n_groups, SSM_STATE)
    g_ssm_log_dt, g_ssm_a_re, g_ssm_a_im, g_ssm_b_re, g_ssm_b_im = disc_vjp((g_abar_r, g_abar_i, g_bbar_r, g_bbar_i))

    dz = jnp.concatenate([dz_ssm, dzu, dzv], axis=1)
    (g_w_in,) = _mm_tn("in_dw", nm, [dz], BF16, tm=1024, tn=768, tk=1024)
    trip = reducer.begin(["w_in"], [g_w_in])
    dnm, got = _mm_nt_sum("in_dproj", [dz], [full["w_in"]], F32, tm=512, tn=1024, tk=1024, carry=trip.carry)
    reducer.end(trip, got)
    dh1, dh1_b, g_norm_mix = _rms_bwd("mix_dnorm", h1, w["norm_mix"], dnm, dh2, 0.5)
    dx, _, g_norm_ffn1 = _ffn_bwd("ffn1", ["w1_gate", "w1_up", "w1_down"], x, w["norm_ffn1"], full["w1_gate"],
                                  full["w1_up"], full["w1_down"], ffn1, dh1, dh1_b, 1.0, reducer)

    halves = [reducer.halves[n] for n in BIG]
    grad = dict(zip(BIG, _comm_call("share_halves", _share_carry(halves, [shard_shape[n] for n in BIG]))))

    small = {"norm_ffn1": g_norm_ffn1, "norm_mix": g_norm_mix, "ssm_log_dt": g_ssm_log_dt, "ssm_a_re": g_ssm_a_re,
             "ssm_a_im": g_ssm_a_im, "ssm_b_re": g_ssm_b_re, "ssm_b_im": g_ssm_b_im, "ssm_c_re": g_ssm_c_re,
             "ssm_c_im": g_ssm_c_im, "ssm_d": g_ssm_d, "gmlp_norm_v": g_gmlp_norm_v, "gmlp_w_s": g_gmlp_w_s,
             "gmlp_b_s": g_gmlp_b_s, "norm_ssm_out": g_norm_ssm_out, "norm_gmlp_out": g_norm_gmlp_out,
             "norm_ffn2": g_norm_ffn2, "norm_ple": g_norm_ple, "norm_final": g_norm_final}
    small_shapes = [w[n].shape for n in SMALL]
    reduced = _allreduce_small(_pack([small[n] for n in SMALL] + [loss_part[:, :1]]))
    small_grads = _unpack(reduced, small_shapes + [(1,)])
    loss = small_grads[-1].reshape(())
    for n, g in zip(SMALL, small_grads[:-1]):
        grad[n] = g

    delta, new_m, new_v = {}, {}, {}
    for n in BIG:
        delta[n], new_m[n], new_v[n] = _adamw("adamw_" + n, w[n], grad[n], m[n], v[n])
    g_packed = reduced[: _pack([small[n] for n in SMALL]).shape[0]]
    d_p, m_p, v_p = _adamw("adamw_small", _pack([w[n] for n in SMALL]), g_packed, _pack([m[n] for n in SMALL]),
                           _pack([v[n] for n in SMALL]))
    for name_list, packed in ((delta, d_p), (new_m, m_p), (new_v, v_p)):
        for n, a in zip(SMALL, _unpack(packed, small_shapes)):
            name_list[n] = a
    return loss, dx, grad, delta, new_m, new_v


def kernel(x, p, norm_ffn1, w1_gate, w1_up, w1_down, norm_mix, w_in, ssm_log_dt, ssm_a_re, ssm_a_im, ssm_b_re, ssm_b_im, ssm_c_re, ssm_c_im, ssm_d, ssm_w_glu, gmlp_norm_v, gmlp_w_s, gmlp_b_s, norm_ssm_out, norm_gmlp_out, w_out, norm_ffn2, w2_gate, w2_up, w2_down, norm_ple, w_ple_gate, w_ple_proj, norm_final, loss_target, m_norm_ffn1, m_w1_gate, m_w1_up, m_w1_down, m_norm_mix, m_w_in, m_ssm_log_dt, m_ssm_a_re, m_ssm_a_im, m_ssm_b_re, m_ssm_b_im, m_ssm_c_re, m_ssm_c_im, m_ssm_d, m_ssm_w_glu, m_gmlp_norm_v, m_gmlp_w_s, m_gmlp_b_s, m_norm_ssm_out, m_norm_gmlp_out, m_w_out, m_norm_ffn2, m_w2_gate, m_w2_up, m_w2_down, m_norm_ple, m_w_ple_gate, m_w_ple_proj, m_norm_final, v_norm_ffn1, v_w1_gate, v_w1_up, v_w1_down, v_norm_mix, v_w_in, v_ssm_log_dt, v_ssm_a_re, v_ssm_a_im, v_ssm_b_re, v_ssm_b_im, v_ssm_c_re, v_ssm_c_im, v_ssm_d, v_ssm_w_glu, v_gmlp_norm_v, v_gmlp_w_s, v_gmlp_b_s, v_norm_ssm_out, v_norm_gmlp_out, v_w_out, v_norm_ffn2, v_w2_gate, v_w2_up, v_w2_down, v_norm_ple, v_w_ple_gate, v_w_ple_proj, v_norm_final):
    given = dict(locals())
    shapes = {n: given[n].shape for n in WEIGHTS}

    def block(name):
        a = given[name]
        if a.ndim == 1:
            return a.reshape(1, -1)
        return a[0] if a.ndim >= 3 else a

    w = {n: block(n) for n in WEIGHTS}
    m = {n: block("m_" + n) for n in WEIGHTS}
    v = {n: block("v_" + n) for n in WEIGHTS}
    loss, dx, grad, delta, new_m, new_v = _step(x[0], p[0, 0], loss_target[0], w, m, v)
    outs = [loss, dx[None]]
    for tree in (grad, delta, new_m, new_v):
        outs += [tree[n].reshape(shapes[n]) for n in WEIGHTS]
    return tuple(outs)
```

```python
import functools
import math

import jax
import jax.numpy as jnp
from jax import lax
from jax.experimental import pallas as pl
from jax.experimental.pallas import tpu as pltpu

F32 = jnp.float32
BF16 = jnp.bfloat16
EPS = 1e-6
SSM_GROUP = 16
SSM_STATE = 64
GROUPS_PER_BLOCK = 8
GMLP_HEAD = 128
CHUNK = 128
ADAM_LR = 0.001
ADAM_B1 = 0.9
ADAM_B2 = 0.999
ADAM_EPS = 1e-08
ADAM_WD = 0.01
ADAM_STEP = 10
N_CHIPS = 4
N_DEV = 8
LANES = 128
VMEM_LIMIT_BYTES = 56 * 1024 * 1024
MESH = pl.DeviceIdType.MESH
GELU_C = math.sqrt(2.0 / math.pi)
GELU_A = 0.044715

_DOT_DIMS = {
    "nn": (((1,), (0,)), ((), ())),
    "nt": (((1,), (1,)), ((), ())),
    "tn": (((0,), (0,)), ((), ())),
}


def _tile(dim, pref, align):
    if dim <= pref:
        return dim
    t = (pref // align) * align
    while t >= align:
        if dim % t == 0:
            return t
        t -= align
    return dim


def _params(semantics):
    return pltpu.CompilerParams(dimension_semantics=semantics, vmem_limit_bytes=VMEM_LIMIT_BYTES)


def _gelu(x):
    return 0.5 * x * (1.0 + jnp.tanh(GELU_C * (x + GELU_A * x * x * x)))


def _gelu_grad(x):
    t = jnp.tanh(GELU_C * (x + GELU_A * x * x * x))
    return 0.5 * (1.0 + t) + 0.5 * x * (1.0 - t * t) * GELU_C * (1.0 + 3.0 * GELU_A * x * x)


def _sigmoid(x):
    return 1.0 / (1.0 + jnp.exp(-x))


def _dot(a, b, mode):
    return lax.dot_general(a.astype(BF16), b.astype(BF16), _DOT_DIMS[mode], preferred_element_type=F32)


class _Carry:
    def __init__(self, arrays, out_shapes, aliases, n_copies, copies):
        self.arrays = list(arrays)
        self.out_shapes = list(out_shapes)
        self.aliases = dict(aliases)
        self.n_copies = n_copies
        self.copies = copies

    def scratch(self):
        return [pltpu.SemaphoreType.DMA((self.n_copies,)), pltpu.SemaphoreType.DMA((self.n_copies,))]

    def split(self, refs):
        n_in, n_out = len(self.arrays), len(self.out_shapes)
        return refs[:n_in], refs[n_in:n_in + n_out], refs[n_in + n_out], refs[n_in + n_out + 1]

    def start(self, refs):
        for cp in self.copies(*self.split(refs)):
            cp.start()

    def wait(self, refs):
        for cp in self.copies(*self.split(refs)):
            cp.wait()


class _SemRange:
    def __init__(self, sems, offset):
        self.sems, self.offset = sems, offset

    @property
    def at(self):
        return self

    def __getitem__(self, k):
        return self.sems.at[self.offset + k]


def _join(first, second):
    n_in, n_out = len(first.arrays), len(first.out_shapes)
    aliases = dict(first.aliases)
    aliases.update({n_in + i: n_out + o for i, o in second.aliases.items()})

    def copies(ops, res, send_sems, recv_sems):
        return (first.copies(ops[:n_in], res[:n_out], send_sems, recv_sems)
                + second.copies(ops[n_in:], res[n_out:], _SemRange(send_sems, first.n_copies),
                                _SemRange(recv_sems, first.n_copies)))

    return _Carry(first.arrays + second.arrays, first.out_shapes + second.out_shapes, aliases,
                  first.n_copies + second.n_copies, copies)


_ANY = pl.BlockSpec(memory_space=pl.ANY)


def _comm_call(name, carry):
    def body(*refs):
        carry.start(refs)
        carry.wait(refs)

    n_in = len(carry.arrays)
    return pl.pallas_call(
        body,
        name=name,
        in_specs=[_ANY] * n_in,
        out_specs=[_ANY] * len(carry.out_shapes),
        out_shape=carry.out_shapes,
        input_output_aliases=carry.aliases,
        scratch_shapes=carry.scratch(),
    )(*carry.arrays)


def _carried_call(body, carry, *, name, grid, in_specs, out_specs, out_shape, scratch_shapes, semantics, args):
    if carry is None:
        res = pl.pallas_call(body, name=name, grid=grid, in_specs=in_specs, out_specs=out_specs, out_shape=out_shape,
                             scratch_shapes=scratch_shapes, compiler_params=_params(semantics))(*args)
        return res, []
    n_in, n_out, n_scr = len(in_specs), len(out_specs), len(scratch_shapes)
    nci, nco = len(carry.arrays), len(carry.out_shapes)

    def wrapped(*refs):
        ins = refs[:n_in]
        outs = refs[n_in + nci:n_in + nci + n_out]
        scr = refs[n_in + nci + n_out + nco:n_in + nci + n_out + nco + n_scr]
        c_refs = (refs[n_in:n_in + nci] + refs[n_in + nci + n_out:n_in + nci + n_out + nco]
                  + refs[n_in + nci + n_out + nco + n_scr:])
        first = functools.reduce(jnp.logical_and, [pl.program_id(d) == 0 for d in range(len(grid))])
        last = functools.reduce(jnp.logical_and, [pl.program_id(d) == grid[d] - 1 for d in range(len(grid))])

        @pl.when(first)
        def _():
            carry.start(c_refs)

        body(*ins, *outs, *scr)

        @pl.when(last)
        def _():
            carry.wait(c_refs)

    res = pl.pallas_call(
        wrapped,
        name=name,
        grid=grid,
        in_specs=list(in_specs) + [_ANY] * nci,
        out_specs=list(out_specs) + [_ANY] * nco,
        out_shape=list(out_shape) + carry.out_shapes,
        input_output_aliases={n_in + i: n_out + o for i, o in carry.aliases.items()},
        scratch_shapes=list(scratch_shapes) + carry.scratch(),
        compiler_params=_params(("arbitrary",) * len(grid)),
    )(*args, *carry.arrays)
    return res[:n_out], res[n_out:]


def _matmul(name, mode, a_list, b_list, products, out_dtypes, epilogue, extras=(), tm=512, tn=512, tk=2048,
            carry=None):
    a0, b0 = a_list[0], b_list[0]
    if mode == "tn":
        k_dim, m_dim = a0.shape
    else:
        m_dim, k_dim = a0.shape
    n_dim = b0.shape[0] if mode == "nt" else b0.shape[1]
    tm = _tile(m_dim, tm, LANES)
    tn = _tile(n_dim, tn, LANES)
    tk = _tile(k_dim, tk, LANES)
    nk = k_dim // tk
    n_acc = 1 + max(p[2] for p in products)
    na, nb, ne, no = len(a_list), len(b_list), len(extras), len(out_dtypes)

    if mode == "tn":
        a_spec = pl.BlockSpec((tk, tm), lambda i, j, k: (k, i))
    else:
        a_spec = pl.BlockSpec((tm, tk), lambda i, j, k: (i, k))
    if mode == "nt":
        b_spec = pl.BlockSpec((tn, tk), lambda i, j, k: (j, k))
    else:
        b_spec = pl.BlockSpec((tk, tn), lambda i, j, k: (k, j))
    t_spec = pl.BlockSpec((tm, tn), lambda i, j, k: (i, j))

    def body(*refs):
        a_refs = refs[:na]
        b_refs = refs[na:na + nb]
        e_refs = refs[na + nb:na + nb + ne]
        o_refs = refs[na + nb + ne:na + nb + ne + no]
        acc_refs = refs[na + nb + ne + no:]

        sums = [None] * n_acc
        for ai, bi, ci in products:
            d = _dot(a_refs[ai][...], b_refs[bi][...], mode)
            sums[ci] = d if sums[ci] is None else sums[ci] + d

        def finish(accs):
            outs = epilogue(accs, [e[...] for e in e_refs])
            for o_ref, o in zip(o_refs, outs):
                o_ref[...] = o.astype(o_ref.dtype)

        if nk == 1:
            finish(sums)
        else:
            k = pl.program_id(2)

            @pl.when(k == 0)
            def _():
                for acc, s in zip(acc_refs, sums):
                    acc[...] = s

            @pl.when(k > 0)
            def _():
                for acc, s in zip(acc_refs, sums):
                    acc[...] += s

            @pl.when(k == nk - 1)
            def _():
                finish([acc[...] for acc in acc_refs])

    scratch = [pltpu.VMEM((tm, tn), F32) for _ in range(n_acc)] if nk > 1 else []
    outs, carried = _carried_call(
        body, carry,
        name=name,
        grid=(m_dim // tm, n_dim // tn, nk),
        in_specs=[a_spec] * na + [b_spec] * nb + [t_spec] * ne,
        out_specs=[t_spec] * no,
        out_shape=[jax.ShapeDtypeStruct((m_dim, n_dim), dt) for dt in out_dtypes],
        scratch_shapes=scratch,
        semantics=("parallel", "parallel", "arbitrary"),
        args=[*a_list, *b_list, *extras],
    )
    return (outs, carried) if carry else outs


def _identity(accs, extras):
    return accs


def _single(result, carry):
    return (result[0][0], result[1]) if carry else result[0]


def _mm_nn(name, a, b, out_dtype, res=None, alpha=1.0, carry=None, **tiles):
    if res is None:
        return _single(_matmul(name, "nn", [a], [b], [(0, 0, 0)], [out_dtype], _identity, carry=carry, **tiles), carry)

    def epilogue(accs, extras):
        return [extras[0] + alpha * accs[0]]

    return _single(_matmul(name, "nn", [a], [b], [(0, 0, 0)], [out_dtype], epilogue, extras=(res,), carry=carry,
                           **tiles), carry)


def _mm_nt_sum(name, a_list, b_list, out_dtype, carry=None, **tiles):
    products = [(i, i, 0) for i in range(len(a_list))]
    return _single(_matmul(name, "nt", a_list, b_list, products, [out_dtype], _identity, carry=carry, **tiles), carry)


def _mm_tn(name, a, b_list, out_dtype, carry=None, **tiles):
    products = [(0, i, i) for i in range(len(b_list))]
    return _matmul(name, "tn", [a], b_list, products, [out_dtype] * len(b_list), _identity, carry=carry, **tiles)


def _rowwise(name, fn, row_ins, par_ins, row_outs, acc_outs=(), tr=512):
    first = row_ins[0][0] if isinstance(row_ins[0], tuple) else row_ins[0]
    t_dim = first.shape[0]
    tr = _tile(t_dim, tr, 16)
    arrays, specs = [], []
    for r in row_ins:
        if isinstance(r, tuple):
            arr, width, blk = r
            specs.append(pl.BlockSpec((tr, width), lambda i, blk=blk: (i, blk)))
        else:
            arr = r
            specs.append(pl.BlockSpec((tr, arr.shape[1]), lambda i: (i, 0)))
        arrays.append(arr)
    for p in par_ins:
        arrays.append(p)
        specs.append(pl.BlockSpec(p.shape, lambda i, nd=p.ndim: (0,) * nd))
    nr, npar, nro, nacc = len(row_ins), len(par_ins), len(row_outs), len(acc_outs)

    def body(*refs):
        rows = [r[...] for r in refs[:nr]]
        pars = [p[...] for p in refs[nr:nr + npar]]
        o_refs = refs[nr + npar:nr + npar + nro]
        acc_refs = refs[nr + npar + nro:]
        outs, accs = fn(rows, pars)
        for o_ref, o in zip(o_refs, outs):
            o_ref[...] = o.astype(o_ref.dtype)
        if nacc:
            @pl.when(pl.program_id(0) == 0)
            def _():
                for a_ref in acc_refs:
                    a_ref[...] = jnp.zeros_like(a_ref)

            for a_ref, a in zip(acc_refs, accs):
                a_ref[...] += a

    out_shape = [jax.ShapeDtypeStruct((t_dim, c), dt) for c, dt in row_outs]
    out_shape += [jax.ShapeDtypeStruct(s, F32) for s in acc_outs]
    out_specs = [pl.BlockSpec((tr, c), lambda i: (i, 0)) for c, _ in row_outs]
    out_specs += [pl.BlockSpec(s, lambda i: (0, 0)) for s in acc_outs]
    return pl.pallas_call(
        body,
        name=name,
        grid=(t_dim // tr,),
        in_specs=specs,
        out_specs=out_specs,
        out_shape=out_shape,
        compiler_params=_params(("arbitrary",)),
    )(*arrays)


def _rms_stats(x):
    r = lax.rsqrt(jnp.mean(x * x, axis=-1, keepdims=True) + EPS)
    return x * r, r


def _rms_backward(x, g, dy):
    xh, r = _rms_stats(x)
    a = dy * g
    dx = r * (a - xh * jnp.mean(a * xh, axis=-1, keepdims=True))
    return dx, jnp.sum(dy * xh, axis=0, keepdims=True)


def _rms_fwd(name, x, g):
    def fn(rows, pars):
        xh, _ = _rms_stats(rows[0])
        return [xh * pars[0]], []

    return _rowwise(name, fn, [x], [g], [(x.shape[1], BF16)])[0]


def _rms_bwd(name, x, g, dy, dres, scale):
    def fn(rows, pars):
        dx, dg = _rms_backward(rows[0], pars[0], rows[1])
        tot = rows[2] + dx
        return [tot, scale * tot], [dg]

    d = x.shape[1]
    return _rowwise(name, fn, [x, dy, dres], [g], [(d, F32), (d, BF16)], [(1, d)], tr=256)


def _cast_into_gathered(name, w, kind, q_arr):
    rs, cs = w.shape
    tr = _tile(rs, 256, 16)
    nr = rs // tr

    def body(q_ref, w_ref, o_ref):
        o_ref[...] = w_ref[...].astype(BF16)

    if kind == "row":
        o_spec = pl.BlockSpec((tr, cs), lambda i, q_ref: (q_ref[0] * nr + i, 0))
    else:
        o_spec = pl.BlockSpec((tr, cs), lambda i, q_ref: (i, q_ref[0]))
    return pl.pallas_call(
        body,
        name=name,
        grid_spec=pltpu.PrefetchScalarGridSpec(num_scalar_prefetch=1, grid=(nr,),
                                               in_specs=[pl.BlockSpec((tr, cs), lambda i, q_ref: (i, 0))],
                                               out_specs=o_spec),
        out_shape=jax.ShapeDtypeStruct(_full_shape(kind, (rs, cs)), BF16),
        compiler_params=_params(("parallel",)),
    )(q_arr, w)


def _ffn_gateup(tag, n, wg, wu, carry):
    def act(accs, extras):
        gate, up = accs
        return [gate * _sigmoid(gate) * up, gate, up]

    return _matmul(tag + "_gateup", "nn", [n], [wg, wu], [(0, 0, 0), (0, 1, 1)], [BF16] * 3, act,
                   tm=1024, tn=512, tk=2048, carry=carry)


def _ffn_down(tag, a, wd, h, carry=None):
    return _mm_nn(tag + "_down", a, wd, F32, res=h, alpha=0.5, tm=512, tn=512, tk=5632, carry=carry)


def _ffn_bwd(tag, names, h, g, wg, wu, wd, saved, dh, dfb, next_scale, reducer):
    n, a, gate, up = saved

    def act_bwd(accs, extras):
        da = accs[0]
        gt, u = extras[0].astype(F32), extras[1].astype(F32)
        s = _sigmoid(gt)
        return [da * u * (s * (1.0 + gt * (1.0 - s))), da * (gt * s)]

    dgp, du = _matmul(tag + "_dact", "nt", [dfb], [wd], [(0, 0, 0)], [BF16, BF16], act_bwd, extras=(gate, up),
                      tm=1024, tn=512, tk=2048)
    (dwd,) = _mm_tn(tag + "_dwd", a, [dfb], BF16, tm=512, tn=2048, tk=2048)
    trip = reducer.begin(names[2:], [dwd])
    (dwg,), got = _mm_tn(tag + "_dwg", n, [dgp], BF16, tm=1024, tn=512, tk=2048, carry=trip.carry)
    reducer.end(trip, got)
    trip = reducer.begin(names[:1], [dwg])
    (dwu,), got = _mm_tn(tag + "_dwu", n, [du], BF16, tm=1024, tn=512, tk=2048, carry=trip.carry)
    reducer.end(trip, got)
    trip = reducer.begin(names[1:2], [dwu])
    dn, got = _mm_nt_sum(tag + "_dn", [dgp, du], [wg, wu], F32, tm=1024, tn=1024, tk=1408, carry=trip.carry)
    reducer.end(trip, got)
    dh_in, dh_in_b, dg = _rms_bwd(tag + "_dnorm", h, g, dn, dh, next_scale)
    return dh_in, dh_in_b, dg


def _ssm_discretize(log_dt, a_re, a_im, b_re, b_im):
    dt = jnp.exp(log_dt)[:, None]
    lr = jnp.minimum(a_re, -1e-4)
    li = a_im
    mag = jnp.exp(lr * dt)
    ang = li * dt
    abar_r = mag * jnp.cos(ang)
    abar_i = mag * jnp.sin(ang)
    den = lr * lr + li * li
    xr = abar_r - 1.0
    xi = abar_i
    zr = (xr * lr + xi * li) / den
    zi = (xi * lr - xr * li) / den
    bbar_r = zr[..., None] * b_re - zi[..., None] * b_im
    bbar_i = zr[..., None] * b_im + zi[..., None] * b_re
    return abar_r, abar_i, bbar_r, bbar_i


def _blockdiag_in(b):
    g, n, p = b.shape
    nb = g // GROUPS_PER_BLOCK
    eye = jnp.eye(GROUPS_PER_BLOCK, dtype=b.dtype)
    b4 = b.reshape(nb, GROUPS_PER_BLOCK, n, p)
    return jnp.einsum("sgnp,gh->sgphn", b4, eye).reshape(nb, GROUPS_PER_BLOCK * p, GROUPS_PER_BLOCK * n)


def _blockdiag_in_grad(gw, n, p):
    nb = gw.shape[0]
    eye = jnp.eye(GROUPS_PER_BLOCK, dtype=gw.dtype)
    g5 = gw.reshape(nb, GROUPS_PER_BLOCK, p, GROUPS_PER_BLOCK, n)
    return jnp.einsum("sgphn,gh->sgnp", g5, eye).reshape(nb * GROUPS_PER_BLOCK, n, p)


def _blockdiag_out(c):
    g, p, n = c.shape
    nb = g // GROUPS_PER_BLOCK
    eye = jnp.eye(GROUPS_PER_BLOCK, dtype=c.dtype)
    c4 = c.reshape(nb, GROUPS_PER_BLOCK, p, n)
    return jnp.einsum("sgpn,gh->shngp", c4, eye).reshape(nb, GROUPS_PER_BLOCK * n, GROUPS_PER_BLOCK * p)


def _blockdiag_out_grad(gw, p, n):
    nb = gw.shape[0]
    eye = jnp.eye(GROUPS_PER_BLOCK, dtype=gw.dtype)
    g5 = gw.reshape(nb, GROUPS_PER_BLOCK, n, GROUPS_PER_BLOCK, p)
    return jnp.einsum("shngp,gh->sgpn", g5, eye).reshape(nb * GROUPS_PER_BLOCK, p, n)


def _ssm_scan_fwd(z, wb, wc, abar, d, carry=None):
    t_dim = z.shape[0]
    nb, cb, sw2 = wb.shape
    nl = sw2 // LANES
    hl = nl // 2
    tt = _tile(t_dim, 256, 8)
    nt = t_dim // tt

    def body(z_ref, wb_ref, wc_ref, a_ref, d_ref, s_ref, y_ref, yg_ref, drive_ref, st_ref):
        @pl.when(pl.program_id(0) == 0)
        def _():
            st_ref[...] = jnp.zeros_like(st_ref)

        u = z_ref[...]
        ub = u.astype(BF16)
        for b in range(nb):
            drive = _dot(ub[:, b * cb:(b + 1) * cb], wb_ref[b], "nn")
            for l in range(nl):
                drive_ref[l, pl.ds(b, tt, stride=nb), :] = drive[:, l * LANES:(l + 1) * LANES]
        a = a_ref[...]
        chunk = lambda v, l: v[:, l * LANES:(l + 1) * LANES]

        def step(t, state):
            rows = pl.ds(pl.multiple_of(t * nb, nb), nb)
            re, im = [], []
            for l in range(hl):
                ar, ai, sr, si = chunk(a, l), chunk(a, hl + l), state[l], state[hl + l]
                nr = ar * sr - ai * si + drive_ref[l, rows, :]
                ni = ar * si + ai * sr + drive_ref[hl + l, rows, :]
                s_ref[l, rows, :] = nr
                s_ref[hl + l, rows, :] = ni
                re.append(nr)
                im.append(ni)
            return tuple(re + im)

        state = lax.fori_loop(0, tt, step, tuple(st_ref[l] for l in range(nl)), unroll=8)
        for l in range(nl):
            st_ref[l] = state[l]
        parts = []
        for b in range(nb):
            s_b = jnp.concatenate([s_ref[l, pl.ds(b, tt, stride=nb), :] for l in range(nl)], axis=1)
            parts.append(_dot(s_b, wc_ref[b], "nn"))
        y = jnp.concatenate(parts, axis=1) + d_ref[...] * u
        y_ref[...] = y
        yg_ref[...] = _gelu(y).astype(BF16)

    full = lambda a: pl.BlockSpec(a.shape, lambda t, nd=a.ndim: (0,) * nd)
    return _carried_call(
        body, carry,
        name="ssm_scan_fwd",
        grid=(nt,),
        in_specs=[pl.BlockSpec((tt, nb * cb), lambda t: (t, 0)), full(wb), full(wc), full(abar), full(d)],
        out_specs=[
            pl.BlockSpec((nl, tt * nb, LANES), lambda t: (0, t, 0)),
            pl.BlockSpec((tt, nb * cb), lambda t: (t, 0)),
            pl.BlockSpec((tt, nb * cb), lambda t: (t, 0)),
        ],
        out_shape=[
            jax.ShapeDtypeStruct((nl, t_dim * nb, LANES), F32),
            jax.ShapeDtypeStruct((t_dim, nb * cb), F32),
            jax.ShapeDtypeStruct((t_dim, nb * cb), BF16),
        ],
        scratch_shapes=[pltpu.VMEM((nl, tt * nb, LANES), F32), pltpu.VMEM((nl, nb, LANES), F32)],
        semantics=("arbitrary",),
        args=[z, wb, wc, abar, d],
    )


def _ssm_scan_bwd(z, dy, states, wb, wc, abar_conj, d, carry=None):
    t_dim = z.shape[0]
    nb, cb, sw2 = wb.shape
    nl = sw2 // LANES
    hl = nl // 2
    tt = _tile(t_dim, 128, 8)
    nt = t_dim // tt
    edges = states.reshape(nl, nt, tt * nb, LANES)[:, :, (tt - 1) * nb:, :]
    before = jnp.concatenate([jnp.zeros((nl, 1, nb, LANES), F32), edges[:, :-1]], axis=1).reshape(nl, nt * nb, LANES)

    def body(z_ref, dy_ref, s_ref, sp_ref, wb_ref, wc_ref, a_ref, d_ref,
             dz_ref, gwb_ref, gwc_ref, ga_ref, gd_ref, gin_ref, gs_ref, st_ref):
        @pl.when(pl.program_id(0) == 0)
        def _():
            st_ref[...] = jnp.zeros_like(st_ref)
            gwb_ref[...] = jnp.zeros_like(gwb_ref)
            gwc_ref[...] = jnp.zeros_like(gwc_ref)
            ga_ref[...] = jnp.zeros_like(ga_ref)
            gd_ref[...] = jnp.zeros_like(gd_ref)

        u = z_ref[...]
        dyv = dy_ref[...]
        ub = u.astype(BF16)
        dyb = dyv.astype(BF16)
        for b in range(nb):
            gin = _dot(dyb[:, b * cb:(b + 1) * cb], wc_ref[b], "nt")
            for l in range(nl):
                gin_ref[l, pl.ds(b, tt, stride=nb), :] = gin[:, l * LANES:(l + 1) * LANES]
        a = a_ref[...]
        chunk = lambda v, l: v[:, l * LANES:(l + 1) * LANES]

        def step(k, state):
            rows = pl.ds(pl.multiple_of((tt - 1 - k) * nb, nb), nb)
            re, im = [], []
            for l in range(hl):
                ar, ai, gr, gi = chunk(a, l), chunk(a, hl + l), state[l], state[hl + l]
                nr = ar * gr - ai * gi + gin_ref[l, rows, :]
                ni = ar * gi + ai * gr + gin_ref[hl + l, rows, :]
                gs_ref[l, rows, :] = nr
                gs_ref[hl + l, rows, :] = ni
                re.append(nr)
                im.append(ni)
            return tuple(re + im)

        state = lax.fori_loop(0, tt, step, tuple(st_ref[l] for l in range(nl)), unroll=8)
        for l in range(nl):
            st_ref[l] = state[l]

        parts = []
        for b in range(nb):
            cols = slice(b * cb, (b + 1) * cb)
            gs_b = jnp.concatenate([gs_ref[l, pl.ds(b, tt, stride=nb), :] for l in range(nl)], axis=1)
            s_b = jnp.concatenate([s_ref[l, pl.ds(b, tt, stride=nb), :] for l in range(nl)], axis=1)
            parts.append(_dot(gs_b, wb_ref[b], "nt"))
            gwb_ref[b] += _dot(ub[:, cols], gs_b, "tn")
            gwc_ref[b] += _dot(s_b, dyb[:, cols], "tn")
        dz_ref[...] = (jnp.concatenate(parts, axis=1) + d_ref[...] * dyv).astype(BF16)
        gd_ref[...] += jnp.sum(dyv * u, axis=0, keepdims=True)

        row = lax.broadcasted_iota(jnp.int32, (tt * nb, LANES), 0)
        shifted = lambda v: jnp.where(row < nb, 0.0, pltpu.roll(v, nb, 0))
        over_time = lambda v: jnp.sum(v.reshape(tt, nb, LANES), axis=0)
        for l in range(hl):
            g_r, g_i = gs_ref[l], gs_ref[hl + l]
            p_r, p_i = shifted(s_ref[l]), shifted(s_ref[hl + l])
            f_r, f_i = sp_ref[l], sp_ref[hl + l]
            g0_r, g0_i = gs_ref[l, pl.ds(0, nb), :], gs_ref[hl + l, pl.ds(0, nb), :]
            ga_ref[l] += over_time(g_r * p_r + g_i * p_i) + g0_r * f_r + g0_i * f_i
            ga_ref[hl + l] += over_time(g_i * p_r - g_r * p_i) + g0_i * f_r - g0_r * f_i

    rev = lambda t: (nt - 1 - t, 0)
    rev3 = lambda t: (0, nt - 1 - t, 0)
    full = lambda a: pl.BlockSpec(a.shape, lambda t, nd=a.ndim: (0,) * nd)
    return _carried_call(
        body, carry,
        name="ssm_scan_bwd",
        grid=(nt,),
        in_specs=[
            pl.BlockSpec((tt, nb * cb), rev),
            pl.BlockSpec((tt, nb * cb), rev),
            pl.BlockSpec((nl, tt * nb, LANES), rev3),
            pl.BlockSpec((nl, nb, LANES), rev3),
            full(wb), full(wc), full(abar_conj), full(d),
        ],
        out_specs=[
            pl.BlockSpec((tt, nb * cb), rev),
            pl.BlockSpec((nb, cb, sw2), lambda t: (0, 0, 0)),
            pl.BlockSpec((nb, sw2, cb), lambda t: (0, 0, 0)),
            pl.BlockSpec((nl, nb, LANES), lambda t: (0, 0, 0)),
            pl.BlockSpec((1, nb * cb), lambda t: (0, 0)),
        ],
        out_shape=[
            jax.ShapeDtypeStruct((t_dim, nb * cb), BF16),
            jax.ShapeDtypeStruct((nb, cb, sw2), F32),
            jax.ShapeDtypeStruct((nb, sw2, cb), F32),
            jax.ShapeDtypeStruct((nl, nb, LANES), F32),
            jax.ShapeDtypeStruct((1, nb * cb), F32),
        ],
        scratch_shapes=[pltpu.VMEM((nl, tt * nb, LANES), F32), pltpu.VMEM((nl, tt * nb, LANES), F32),
                        pltpu.VMEM((nl, nb, LANES), F32)],
        semantics=("arbitrary",),
        args=[z, dy, states, before, wb, wc, abar_conj, d],
    )


def _gmlp_chunk(zu, zv, gv, wm_ref, bias, n_heads):
    ua = _gelu(zu)
    vg = _gelu(zv)
    xc = vg - jnp.mean(vg, axis=-1, keepdims=True)
    r = lax.rsqrt(jnp.mean(xc * xc, axis=-1, keepdims=True) + EPS)
    vh = xc * r
    vb = (vh * gv).astype(BF16)
    parts = []
    for h in range(n_heads):
        cols = slice(h * GMLP_HEAD, (h + 1) * GMLP_HEAD)
        parts.append(_dot(wm_ref[h], vb[:, cols], "nn"))
    s = jnp.concatenate(parts, axis=1) + bias
    return ua, vh, r, vb, s


def _gmlp_fwd(z, gv, wm, bias, ggo):
    t_dim = z.shape[0]
    dg = gv.shape[1]
    n_heads = dg // GMLP_HEAD
    tr = _tile(t_dim, 256, CHUNK)

    def body(zu_ref, zv_ref, gv_ref, wm_ref, b_ref, ggo_ref, o_ref):
        for ck in range(tr // CHUNK):
            rows = pl.ds(ck * CHUNK, CHUNK)
            ua, _, _, _, s = _gmlp_chunk(zu_ref[rows, :], zv_ref[rows, :], gv_ref[...], wm_ref, b_ref[...], n_heads)
            yh, _ = _rms_stats(ua * s)
            o_ref[rows, :] = (yh * ggo_ref[...]).astype(BF16)

    full = lambda a: pl.BlockSpec(a.shape, lambda i, nd=a.ndim: (0,) * nd)
    return pl.pallas_call(
        body,
        name="gmlp_fwd",
        grid=(t_dim // tr,),
        in_specs=[pl.BlockSpec((tr, dg), lambda i: (i, 1)), pl.BlockSpec((tr, dg), lambda i: (i, 2)),
                  full(gv), full(wm), full(bias), full(ggo)],
        out_specs=pl.BlockSpec((tr, dg), lambda i: (i, 0)),
        out_shape=jax.ShapeDtypeStruct((t_dim, dg), BF16),
        compiler_params=_params(("parallel",)),
    )(z, z, gv, wm, bias, ggo)


def _gmlp_bwd(z, dycat, gv, wm, bias, ggo):
    t_dim = z.shape[0]
    dg = gv.shape[1]
    n_heads = dg // GMLP_HEAD
    tr = _tile(t_dim, 256, CHUNK)

    def body(zu_ref, zv_ref, dy_ref, gv_ref, wm_ref, b_ref, ggo_ref,
             dzu_ref, dzv_ref, dggo_ref, dgv_ref, dwm_ref, dsum_ref):
        @pl.when(pl.program_id(0) == 0)
        def _():
            dggo_ref[...] = jnp.zeros_like(dggo_ref)
            dgv_ref[...] = jnp.zeros_like(dgv_ref)
            dwm_ref[...] = jnp.zeros_like(dwm_ref)
            dsum_ref[...] = jnp.zeros_like(dsum_ref)

        for ck in range(tr // CHUNK):
            rows = pl.ds(ck * CHUNK, CHUNK)
            zu = zu_ref[rows, :]
            zv = zv_ref[rows, :]
            gvv = gv_ref[...]
            ua, vh, r, vb, s = _gmlp_chunk(zu, zv, gvv, wm_ref, b_ref[...], n_heads)
            dy, dggo = _rms_backward(ua * s, ggo_ref[...], dy_ref[rows, :])
            dggo_ref[...] += dggo
            ds = dy * ua
            dsum_ref[...] += ds
            dsb = ds.astype(BF16)
            parts = []
            for h in range(n_heads):
                cols = slice(h * GMLP_HEAD, (h + 1) * GMLP_HEAD)
                dwm_ref[h] += _dot(dsb[:, cols], vb[:, cols], "nt")
                parts.append(_dot(wm_ref[h], dsb[:, cols], "tn"))
            dv = jnp.concatenate(parts, axis=1)
            dgv_ref[...] += jnp.sum(dv * vh, axis=0, keepdims=True)
            dvh = dv * gvv
            dvg = r * (dvh - jnp.mean(dvh, axis=-1, keepdims=True) - vh * jnp.mean(dvh * vh, axis=-1, keepdims=True))
            dzv_ref[rows, :] = (dvg * _gelu_grad(zv)).astype(BF16)
            dzu_ref[rows, :] = (dy * s * _gelu_grad(zu)).astype(BF16)

    full = lambda a: pl.BlockSpec(a.shape, lambda i, nd=a.ndim: (0,) * nd)
    return pl.pallas_call(
        body,
        name="gmlp_bwd",
        grid=(t_dim // tr,),
        in_specs=[pl.BlockSpec((tr, dg), lambda i: (i, 1)), pl.BlockSpec((tr, dg), lambda i: (i, 2)),
                  pl.BlockSpec((tr, dg), lambda i: (i, 1)), full(gv), full(wm), full(bias), full(ggo)],
        out_specs=[pl.BlockSpec((tr, dg), lambda i: (i, 0)), pl.BlockSpec((tr, dg), lambda i: (i, 0)),
                   pl.BlockSpec((1, dg), lambda i: (0, 0)), pl.BlockSpec((1, dg), lambda i: (0, 0)),
                   pl.BlockSpec(wm.shape, lambda i: (0, 0, 0)), pl.BlockSpec((CHUNK, dg), lambda i: (0, 0))],
        out_shape=[jax.ShapeDtypeStruct((t_dim, dg), BF16), jax.ShapeDtypeStruct((t_dim, dg), BF16),
                   jax.ShapeDtypeStruct((1, dg), F32), jax.ShapeDtypeStruct((1, dg), F32),
                   jax.ShapeDtypeStruct(wm.shape, F32), jax.ShapeDtypeStruct((CHUNK, dg), F32)],
        compiler_params=_params(("arbitrary",)),
    )(z, z, dycat, gv, wm, bias, ggo)


def _position():
    x, y, c = lax.axis_index("x"), lax.axis_index("y"), lax.axis_index("c")
    chips = [(1 - x, y), (x, 1 - y), (1 - x, 1 - y)]
    return x, y, c, chips


def _region(ref, kind, shard_shape, q, half):
    rs, cs = shard_shape
    r0, nr = (0, rs) if half is None else (half * (rs // 2), rs // 2)
    if kind == "row":
        return ref.at[pl.ds(q * rs + r0, nr), :]
    return ref.at[pl.ds(r0, nr), pl.ds(q * cs, cs)]


def _full_shape(kind, shard_shape):
    rs, cs = shard_shape
    return (N_CHIPS * rs, cs) if kind == "row" else (rs, N_CHIPS * cs)


def _remote(src, dst, send_sems, recv_sems, k, to):
    return pltpu.make_async_remote_copy(src_ref=src, dst_ref=dst, send_sem=send_sems.at[k], recv_sem=recv_sems.at[k],
                                        device_id=to, device_id_type=MESH)


def _same(arrays):
    return [jax.ShapeDtypeStruct(a.shape, a.dtype) for a in arrays]


def _gather_ici_carry(gathered, kinds, shapes):
    nw = len(gathered)

    def copies(ops, full, send_sems, recv_sems):
        x, y, c, chips = _position()
        out = []
        for w in range(nw):
            mine = _region(full[w], kinds[w], shapes[w], 2 * x + y, c)
            for j, (cx, cy) in enumerate(chips):
                out.append(_remote(mine, mine, send_sems, recv_sems, 3 * w + j, (cx, cy, c)))
        return out

    return _Carry(gathered, _same(gathered), {i: i for i in range(nw)}, 3 * nw, copies)


def _gather_d2d_carry(gathered, kinds, shapes):
    nw = len(gathered)

    def copies(ops, full, send_sems, recv_sems):
        x, y, c, chips = _position()
        out = []
        for w in range(nw):
            for j, (cx, cy) in enumerate(chips):
                landed = _region(full[w], kinds[w], shapes[w], 2 * cx + cy, c)
                out.append(_remote(landed, landed, send_sems, recv_sems, 3 * w + j, (x, y, 1 - c)))
        return out

    return _Carry(gathered, _same(gathered), {i: i for i in range(nw)}, 3 * nw, copies)


def _pairs_carry(grads, kinds, shapes):
    nw = len(grads)

    def copies(g, got, send_sems, recv_sems):
        x, y, c, _ = _position()
        out = []
        for w in range(nw):
            for q in range(N_CHIPS):
                out.append(_remote(_region(g[w], kinds[w], shapes[w], q, 1 - c), got[w].at[q], send_sems, recv_sems,
                                   N_CHIPS * w + q, (x, y, 1 - c)))
        return out

    outs = [jax.ShapeDtypeStruct((N_CHIPS, s[0] // 2, s[1]), BF16) for s in shapes]
    return _Carry(grads, outs, {}, N_CHIPS * nw, copies)


def _pair_sum(name, grad, got, kind, shard_shape, c_arr):
    rs, cs = shard_shape
    hr = rs // 2
    tr = _tile(hr, 512, 16)
    nr = hr // tr

    def body(c_ref, g_ref, s_ref, o_ref):
        o_ref[...] = (g_ref[...].astype(F32) + s_ref[...].astype(F32)).astype(BF16)

    if kind == "row":
        g_spec = pl.BlockSpec((tr, cs), lambda q, i, c_ref: (q * (rs // tr) + c_ref[0] * nr + i, 0))
    else:
        g_spec = pl.BlockSpec((tr, cs), lambda q, i, c_ref: (c_ref[0] * nr + i, q))
    blk = pl.BlockSpec((None, tr, cs), lambda q, i, c_ref: (q, i, 0))
    return pl.pallas_call(
        body,
        name=name,
        grid_spec=pltpu.PrefetchScalarGridSpec(num_scalar_prefetch=1, grid=(N_CHIPS, nr), in_specs=[g_spec, blk],
                                               out_specs=blk),
        out_shape=jax.ShapeDtypeStruct((N_CHIPS, hr, cs), BF16),
        compiler_params=_params(("parallel", "parallel")),
    )(c_arr, grad, got)


def _scatter_carry(sums, shapes):
    nw = len(sums)

    def copies(ps, got, send_sems, recv_sems):
        x, y, c, chips = _position()
        out = []
        for w in range(nw):
            for j, (cx, cy) in enumerate(chips):
                out.append(_remote(ps[w].at[2 * cx + cy], got[w].at[j], send_sems, recv_sems, 3 * w + j, (cx, cy, c)))
        return out

    outs = [jax.ShapeDtypeStruct((3, s[0] // 2, s[1]), BF16) for s in shapes]
    return _Carry(sums, outs, {}, 3 * nw, copies)


def _owner_sum(name, sums, got, shard_shape, qc_arr):
    rs, cs = shard_shape
    hr = rs // 2
    tr = _tile(hr, 512, 16)
    nr = hr // tr

    def body(qc_ref, mine_ref, got_ref, o_ref):
        acc = mine_ref[...].astype(F32)
        for j in range(3):
            acc = acc + got_ref[j].astype(F32)
        o_ref[...] = acc

    return pl.pallas_call(
        body,
        name=name,
        grid_spec=pltpu.PrefetchScalarGridSpec(
            num_scalar_prefetch=1, grid=(nr,),
            in_specs=[pl.BlockSpec((None, tr, cs), lambda i, qc_ref: (qc_ref[0], i, 0)),
                      pl.BlockSpec((3, tr, cs), lambda i, qc_ref: (0, i, 0))],
            out_specs=pl.BlockSpec((tr, cs), lambda i, qc_ref: (qc_ref[1] * nr + i, 0))),
        out_shape=jax.ShapeDtypeStruct((rs, cs), F32),
        compiler_params=_params(("parallel",)),
    )(qc_arr, sums, got)


def _share_carry(grads, shapes):
    nw = len(grads)

    def copies(ops, out, send_sems, recv_sems):
        x, y, c, _ = _position()
        res = []
        for w in range(nw):
            hr = shapes[w][0] // 2
            mine = out[w].at[pl.ds(c * hr, hr), :]
            res.append(_remote(mine, mine, send_sems, recv_sems, w, (x, y, 1 - c)))
        return res

    return _Carry(grads, _same(grads), {i: i for i in range(nw)}, nw, copies)


def _allreduce_small(packed):
    rows = packed.shape[0]

    def body(x_ref, out_ref, gat_ref, send_sems, recv_sems, local_sem):
        x, y, c, chips = _position()
        me, sibling = (x, y, c), (x, y, 1 - c)

        def block(px, py, pc):
            return gat_ref.at[pl.ds((4 * px + 2 * py + pc) * rows, rows), :]

        def copy(k, blk, to, src=None):
            return pltpu.make_async_remote_copy(
                src_ref=block(*blk) if src is None else src, dst_ref=block(*blk), send_sem=send_sems.at[k],
                recv_sem=recv_sems.at[k], device_id=to, device_id_type=MESH)

        mine = pltpu.make_async_copy(x_ref, block(*me), local_sem)
        mine.start()
        first = [copy(0, me, sibling, src=x_ref)]
        first += [copy(1 + j, me, (*chip, c), src=x_ref) for j, chip in enumerate(chips)]
        for cp in first:
            cp.start()
        passed = [copy(4 + j, (*chip, c), sibling) for j, chip in enumerate(chips)]
        for j, chip in enumerate(chips):
            copy(1 + j, (*chip, c), me).wait_recv()
            passed[j].start()
        copy(0, sibling, me).wait_recv()
        for j, chip in enumerate(chips):
            copy(4 + j, (*chip, 1 - c), me).wait_recv()
        for cp in first + passed:
            cp.wait_send()
        mine.wait()
        acc = gat_ref[pl.ds(0, rows), :]
        for k in range(1, N_DEV):
            acc = acc + gat_ref[pl.ds(k * rows, rows), :]
        out_ref[...] = acc

    return pl.pallas_call(
        body,
        name="allreduce_small",
        in_specs=[pl.BlockSpec(memory_space=pltpu.VMEM)],
        out_specs=pl.BlockSpec(memory_space=pltpu.VMEM),
        out_shape=jax.ShapeDtypeStruct(packed.shape, F32),
        scratch_shapes=[pltpu.VMEM((N_DEV * rows, LANES), F32), pltpu.SemaphoreType.DMA((7,)),
                        pltpu.SemaphoreType.DMA((7,)), pltpu.SemaphoreType.DMA],
        compiler_params=pltpu.CompilerParams(vmem_limit_bytes=VMEM_LIMIT_BYTES),
    )(packed)


def _adamw(name, w, g, m, v):
    def fn(rows, pars):
        wv, gv, mv, vv = rows
        m_new = ADAM_B1 * mv + (1.0 - ADAM_B1) * gv
        v_new = ADAM_B2 * vv + (1.0 - ADAM_B2) * (gv * gv)
        m_hat = m_new / (1.0 - ADAM_B1 ** ADAM_STEP)
        v_hat = v_new / (1.0 - ADAM_B2 ** ADAM_STEP)
        delta = -ADAM_LR * (m_hat / (jnp.sqrt(v_hat) + ADAM_EPS) + ADAM_WD * wv)
        return [delta, m_new, v_new], []

    c = w.shape[1]
    return _rowwise(name, fn, [w, g, m, v], [], [(c, F32)] * 3, tr=256)


def _pack(arrays):
    rows = []
    for a in arrays:
        flat = a.reshape(-1).astype(F32)
        pad = (-flat.shape[0]) % LANES
        rows.append(jnp.pad(flat, (0, pad)).reshape(-1, LANES))
    stacked = jnp.concatenate(rows, axis=0)
    pad_rows = (-stacked.shape[0]) % 8
    return jnp.pad(stacked, ((0, pad_rows), (0, 0)))


def _unpack(packed, shapes):
    out, r = [], 0
    for s in shapes:
        n = math.prod(s)
        nr = -(-n // LANES)
        out.append(packed[r:r + nr].reshape(-1)[:n].reshape(s))
        r += nr
    return out


BIG = ["w1_gate", "w1_up", "w1_down", "w_in", "ssm_w_glu", "w_out", "w2_gate", "w2_up", "w2_down", "w_ple_gate",
       "w_ple_proj"]
KIND = {"w1_gate": "col", "w1_up": "col", "w1_down": "row", "w_in": "col", "ssm_w_glu": "row", "w_out": "row",
        "w2_gate": "col", "w2_up": "col", "w2_down": "row", "w_ple_gate": "row", "w_ple_proj": "col"}
SMALL = ["norm_ffn1", "norm_mix", "ssm_log_dt", "ssm_a_re", "ssm_a_im", "ssm_b_re", "ssm_b_im", "ssm_c_re", "ssm_c_im",
         "ssm_d", "gmlp_norm_v", "gmlp_w_s", "gmlp_b_s", "norm_ssm_out", "norm_gmlp_out", "norm_ffn2", "norm_ple",
         "norm_final"]
WEIGHTS = ["norm_ffn1", "w1_gate", "w1_up", "w1_down", "norm_mix", "w_in", "ssm_log_dt", "ssm_a_re", "ssm_a_im",
           "ssm_b_re", "ssm_b_im", "ssm_c_re", "ssm_c_im", "ssm_d", "ssm_w_glu", "gmlp_norm_v", "gmlp_w_s", "gmlp_b_s",
           "norm_ssm_out", "norm_gmlp_out", "w_out", "norm_ffn2", "w2_gate", "w2_up", "w2_down", "norm_ple",
           "w_ple_gate", "w_ple_proj", "norm_final"]


class _Trip:
    def __init__(self, names, sums, carry):
        self.names, self.sums, self.carry = names, sums, carry


class _Reducer:
    def __init__(self, shard_shape, c_arr, qc_arr):
        self.shard_shape, self.c_arr, self.qc_arr = shard_shape, c_arr, qc_arr
        self.halves = {}

    def begin(self, names, grads):
        kinds = [KIND[n] for n in names]
        shapes = [self.shard_shape[n] for n in names]
        swapped = _comm_call("pairs_" + names[0], _pairs_carry(grads, kinds, shapes))
        sums = [_pair_sum("pair_sum_" + n, g, s, k, sh, self.c_arr)
                for n, g, s, k, sh in zip(names, grads, swapped, kinds, shapes)]
        return _Trip(names, sums, _scatter_carry(sums, shapes))

    def end(self, trip, got):
        for n, ps, g in zip(trip.names, trip.sums, got):
            self.halves[n] = _owner_sum("owner_sum_" + n, ps, g, self.shard_shape[n], self.qc_arr)


def _step(x, p, tgt, w, m, v):
    d_model = x.shape[1]
    d_ssm = w["ssm_d"].shape[1]
    n_groups = d_ssm // SSM_GROUP
    row = lambda a: a.reshape(1, -1)

    xi, yi, ci = lax.axis_index("x"), lax.axis_index("y"), lax.axis_index("c")
    c_arr = jnp.reshape(ci, (1,)).astype(jnp.int32)
    q_arr = jnp.reshape(2 * xi + yi, (1,)).astype(jnp.int32)
    qc_arr = jnp.stack([2 * xi + yi, ci]).astype(jnp.int32)
    shard_shape = {n: w[n].shape for n in BIG}
    full = {n: _cast_into_gathered("cast_" + n, w[n], KIND[n], q_arr) for n in BIG}

    def gather(stage, names):
        return stage([full[n] for n in names], [KIND[n] for n in names], [shard_shape[n] for n in names])

    def gathered(names, arrays):
        full.update(zip(names, arrays))

    first, second, third = ["w1_gate", "w1_up"], ["w1_down", "w_in"], ["w2_gate"]
    fourth, fifth, sixth = ["ssm_w_glu", "w_out"], ["w2_up"], ["w2_down", "w_ple_gate", "w_ple_proj"]

    def two_stages(passed_on, landing):
        return _join(gather(_gather_d2d_carry, passed_on), gather(_gather_ici_carry, landing)), passed_on + landing

    gathered(first, _comm_call("gather_first_ici", gather(_gather_ici_carry, first)))
    gathered(first, _comm_call("gather_first_d2d", gather(_gather_d2d_carry, first)))
    n1 = _rms_fwd("ffn1_norm", x, w["norm_ffn1"])
    (a1, gate1, up1), landed = _ffn_gateup("ffn1", n1, full["w1_gate"], full["w1_up"],
                                           gather(_gather_ici_carry, second))
    gathered(second, landed)
    gathered(second, _comm_call("gather_second_d2d", gather(_gather_d2d_carry, second)))
    h1, landed = _ffn_down("ffn1", a1, full["w1_down"], x, gather(_gather_ici_carry, third))
    gathered(third, landed)
    ffn1 = (n1, a1, gate1, up1)
    nm = _rms_fwd("mix_norm", h1, w["norm_mix"])
    carry, names = two_stages(third, fourth)
    z, landed = _mm_nn("in_proj", nm, full["w_in"], F32, tm=1024, tn=512, tk=2048, carry=carry)
    gathered(names, landed)

    disc, disc_vjp = jax.vjp(_ssm_discretize, w["ssm_log_dt"][0], w["ssm_a_re"], w["ssm_a_im"], w["ssm_b_re"],
                             w["ssm_b_im"])
    abar_r, abar_i, bbar_r, bbar_i = disc
    nb = n_groups // GROUPS_PER_BLOCK
    wb = jnp.concatenate([_blockdiag_in(bbar_r), _blockdiag_in(bbar_i)], axis=-1).astype(BF16)
    wc = jnp.concatenate([_blockdiag_out(w["ssm_c_re"]), -_blockdiag_out(w["ssm_c_im"])], axis=1).astype(BF16)
    abar = jnp.concatenate([abar_r.reshape(nb, -1), abar_i.reshape(nb, -1)], axis=-1)
    abar_conj = jnp.concatenate([abar_r.reshape(nb, -1), -abar_i.reshape(nb, -1)], axis=-1)
    carry, names = two_stages(fourth, fifth)
    (states, y_pre, yg), landed = _ssm_scan_fwd(z, wb, wc, abar, w["ssm_d"], carry)
    gathered(names, landed)
    q = _mm_nn("glu_proj", yg, full["ssm_w_glu"], F32, tm=1024, tn=1024, tk=1024)

    def glu_norm(rows, pars):
        yv = _gelu(rows[0]) * _sigmoid(rows[1])
        yh, _ = _rms_stats(yv)
        return [yh * pars[0]], []

    yn_ssm = _rowwise("ssm_glu_norm", glu_norm, [y_pre, q], [w["norm_ssm_out"]], [(d_ssm, BF16)])[0]

    tril = jnp.tril(jnp.ones((CHUNK, CHUNK), dtype=bool))
    wm = jnp.where(tril[None], w["gmlp_w_s"], 0.0).astype(BF16)
    bias = jnp.repeat(w["gmlp_b_s"].T, GMLP_HEAD, axis=1)
    yn_gmlp = _gmlp_fwd(z, w["gmlp_norm_v"], wm, bias, w["norm_gmlp_out"])
    ycat = jnp.concatenate([yn_ssm, yn_gmlp], axis=1)
    h2, landed = _mm_nn("out_proj", ycat, full["w_out"], F32, res=h1, alpha=1.0, tm=512, tn=1024, tk=2048,
                        carry=gather(_gather_d2d_carry, fifth))
    gathered(fifth, landed)

    n2 = _rms_fwd("ffn2_norm", h2, w["norm_ffn2"])
    (a2, gate2, up2), landed = _ffn_gateup("ffn2", n2, full["w2_gate"], full["w2_up"],
                                           gather(_gather_ici_carry, sixth))
    gathered(sixth, landed)
    gathered(sixth, _comm_call("gather_sixth_d2d", gather(_gather_d2d_carry, sixth)))
    h3 = _ffn_down("ffn2", a2, full["w2_down"], h2)
    ffn2 = (n2, a2, gate2, up2)
    npl = _rms_fwd("ple_norm", h3, w["norm_ple"])
    gq = _mm_nn("ple_gate", npl, full["w_ple_gate"], F32, tm=1024, tn=1024, tk=2048)
    pp = _mm_nn("ple_proj", p, full["w_ple_proj"], F32, tm=1024, tn=1024, tk=2048)

    def ple_combine(rows, pars):
        return [rows[0] + _sigmoid(rows[1]) * rows[2]], []

    h4 = _rowwise("ple_combine", ple_combine, [h3, gq, pp], [], [(d_model, F32)], tr=256)[0]

    def head(rows, pars):
        hv, tv = rows
        xh, _ = _rms_stats(hv)
        err = xh * pars[0] - tv
        dx, dg = _rms_backward(hv, pars[0], err * (1.0 / d_model))
        part = 0.5 * jnp.sum(err * err) * (1.0 / d_model)
        return [dx], [dg, jnp.full((1, LANES), part, F32)]

    dh4, g_norm_final, loss_part = _rowwise("loss_head", head, [h4, tgt], [row(w["norm_final"])], [(d_model, F32)],
                                            [(1, d_model), (1, LANES)], tr=256)

    def ple_bwd(rows, pars):
        dh, gqv, ppv = rows
        gate = _sigmoid(gqv)
        return [dh * ppv * gate * (1.0 - gate), dh * gate], []

    dgq, dpp = _rowwise("ple_dgate", ple_bwd, [dh4, gq, pp], [], [(d_model, BF16)] * 2, tr=256)
    reducer = _Reducer(shard_shape, c_arr, qc_arr)
    (g_w_ple_proj,) = _mm_tn("ple_dwproj", p, [dpp], BF16, tm=256, tn=1024, tk=1024)
    (g_w_ple_gate,) = _mm_tn("ple_dwgate", npl, [dgq], BF16, tm=1024, tn=1024, tk=1024)
    trip = reducer.begin(["w_ple_gate", "w_ple_proj"], [g_w_ple_gate, g_w_ple_proj])
    dnpl, got = _mm_nt_sum("ple_dnorm_in", [dgq], [full["w_ple_gate"]], F32, tm=512, tn=1024, tk=2048, carry=trip.carry)
    reducer.end(trip, got)
    dh3, dh3_b, g_norm_ple = _rms_bwd("ple_dnorm", h3, w["norm_ple"], dnpl, dh4, 0.5)

    dh2, dh2_b, g_norm_ffn2 = _ffn_bwd("ffn2", ["w2_gate", "w2_up", "w2_down"], h2, w["norm_ffn2"], full["w2_gate"],
                                       full["w2_up"], full["w2_down"], ffn2, dh3, dh3_b, 1.0, reducer)

    dycat = _mm_nt_sum("out_dproj", [dh2_b], [full["w_out"]], F32, tm=512, tn=1024, tk=2048)
    (g_w_out,) = _mm_tn("out_dw", ycat, [dh2_b], BF16, tm=1024, tn=1024, tk=1024)

    dzu, dzv, g_norm_gmlp_out, g_gmlp_norm_v, g_wm, g_s = _gmlp_bwd(z, dycat, w["gmlp_norm_v"], wm, bias,
                                                                  w["norm_gmlp_out"])
    g_gmlp_w_s = jnp.where(tril[None], g_wm, 0.0)
    g_gmlp_b_s = g_s.reshape(CHUNK, -1, GMLP_HEAD).sum(axis=-1).T

    def glu_bwd(rows, pars):
        dyn, ypre, qv = rows
        ygv = _gelu(ypre)
        sg = _sigmoid(qv)
        dy, dg = _rms_backward(ygv * sg, pars[0], dyn)
        return [dy * ygv * sg * (1.0 - sg), dy * sg], [dg]

    dq, dyg_part, g_norm_ssm_out = _rowwise("ssm_dglu", glu_bwd, [(dycat, d_ssm, 0), y_pre, q], [w["norm_ssm_out"]],
                                            [(d_ssm, BF16), (d_ssm, F32)], [(1, d_ssm)])
    dyg_proj = _mm_nt_sum("glu_dproj", [dq], [full["ssm_w_glu"]], F32, tm=1024, tn=1024, tk=1024)
    (g_ssm_w_glu,) = _mm_tn("glu_dw", yg, [dq], BF16, tm=1024, tn=1024, tk=1024)

    def gelu_bwd(rows, pars):
        return [(rows[0] + rows[1]) * _gelu_grad(rows[2])], []

    dy_pre = _rowwise("ssm_dgelu", gelu_bwd, [dyg_part, dyg_proj, y_pre], [], [(d_ssm, F32)])[0]
    trip = reducer.begin(["w_out", "ssm_w_glu"], [g_w_out, g_ssm_w_glu])
    (dz_ssm, g_wb, g_wc, g_abar, g_ssm_d), got = _ssm_scan_bwd(z, dy_pre, states, wb, wc, abar_conj, w["ssm_d"],
                                                              trip.carry)
    reducer.end(trip, got)
    g_abar = jnp.transpose(g_abar, (1, 0, 2)).reshape(nb, -1)
    sw = g_abar.shape[-1] // 2
    g_bbar_r = _blockdiag_in_grad(g_wb[..., :sw], SSM_STATE, SSM_GROUP)
    g_bbar_i = _blockdiag_in_grad(g_wb[..., sw:], SSM_STATE, SSM_GROUP)
    g_ssm_c_re = _blockdiag_out_grad(g_wc[:, :sw, :], SSM_GROUP, SSM_STATE)
    g_ssm_c_im = -_blockdiag_out_grad(g_wc[:, sw:, :], SSM_GROUP, SSM_STATE)
    g_abar_r = g_abar[..., :sw].reshape(n_groups, SSM_STATE)
    g_abar_i = g_abar[..., sw:].reshape(n_groups, SSM_STATE)
    g_ssm_log_dt, g_ssm_a_re, g_ssm_a_im, g_ssm_b_re, g_ssm_b_im = disc_vjp((g_abar_r, g_abar_i, g_bbar_r, g_bbar_i))

    dz = jnp.concatenate([dz_ssm, dzu, dzv], axis=1)
    (g_w_in,) = _mm_tn("in_dw", nm, [dz], BF16, tm=1024, tn=768, tk=1024)
    trip = reducer.begin(["w_in"], [g_w_in])
    dnm, got = _mm_nt_sum("in_dproj", [dz], [full["w_in"]], F32, tm=512, tn=1024, tk=1024, carry=trip.carry)
    reducer.end(trip, got)
    dh1, dh1_b, g_norm_mix = _rms_bwd("mix_dnorm", h1, w["norm_mix"], dnm, dh2, 0.5)
    dx, _, g_norm_ffn1 = _ffn_bwd("ffn1", ["w1_gate", "w1_up", "w1_down"], x, w["norm_ffn1"], full["w1_gate"],
                                  full["w1_up"], full["w1_down"], ffn1, dh1, dh1_b, 1.0, reducer)

    halves = [reducer.halves[n] for n in BIG]
    grad = dict(zip(BIG, _comm_call("share_halves", _share_carry(halves, [shard_shape[n] for n in BIG]))))

    small = {"norm_ffn1": g_norm_ffn1, "norm_mix": g_norm_mix, "ssm_log_dt": g_ssm_log_dt, "ssm_a_re": g_ssm_a_re,
             "ssm_a_im": g_ssm_a_im, "ssm_b_re": g_ssm_b_re, "ssm_b_im": g_ssm_b_im, "ssm_c_re": g_ssm_c_re,
             "ssm_c_im": g_ssm_c_im, "ssm_d": g_ssm_d, "gmlp_norm_v": g_gmlp_norm_v, "gmlp_w_s": g_gmlp_w_s,
             "gmlp_b_s": g_gmlp_b_s, "norm_ssm_out": g_norm_ssm_out, "norm_gmlp_out": g_norm_gmlp_out,
             "norm_ffn2": g_norm_ffn2, "norm_ple": g_norm_ple, "norm_final": g_norm_final}
    small_shapes = [w[n].shape for n in SMALL]
    reduced = _allreduce_small(_pack([small[n] for n in SMALL] + [loss_part[:, :1]]))
    small_grads = _unpack(reduced, small_shapes + [(1,)])
    loss = small_grads[-1].reshape(())
    for n, g in zip(SMALL, small_grads[:-1]):
        grad[n] = g

    delta, new_m, new_v = {}, {}, {}
    for n in BIG:
        delta[n], new_m[n], new_v[n] = _adamw("adamw_" + n, w[n], grad[n], m[n], v[n])
    g_packed = reduced[: _pack([small[n] for n in SMALL]).shape[0]]
    d_p, m_p, v_p = _adamw("adamw_small", _pack([w[n] for n in SMALL]), g_packed, _pack([m[n] for n in SMALL]),
                           _pack([v[n] for n in SMALL]))
    for name_list, packed in ((delta, d_p), (new_m, m_p), (new_v, v_p)):
        for n, a in zip(SMALL, _unpack(packed, small_shapes)):
            name_list[n] = a
    return loss, dx, grad, delta, new_m, new_v


def kernel(x, p, norm_ffn1, w1_gate, w1_up, w1_down, norm_mix, w_in, ssm_log_dt, ssm_a_re, ssm_a_im, ssm_b_re, ssm_b_im, ssm_c_re, ssm_c_im, ssm_d, ssm_w_glu, gmlp_norm_v, gmlp_w_s, gmlp_b_s, norm_ssm_out, norm_gmlp_out, w_out, norm_ffn2, w2_gate, w2_up, w2_down, norm_ple, w_ple_gate, w_ple_proj, norm_final, loss_target, m_norm_ffn1, m_w1_gate, m_w1_up, m_w1_down, m_norm_mix, m_w_in, m_ssm_log_dt, m_ssm_a_re, m_ssm_a_im, m_ssm_b_re, m_ssm_b_im, m_ssm_c_re, m_ssm_c_im, m_ssm_d, m_ssm_w_glu, m_gmlp_norm_v, m_gmlp_w_s, m_gmlp_b_s, m_norm_ssm_out, m_norm_gmlp_out, m_w_out, m_norm_ffn2, m_w2_gate, m_w2_up, m_w2_down, m_norm_ple, m_w_ple_gate, m_w_ple_proj, m_norm_final, v_norm_ffn1, v_w1_gate, v_w1_up, v_w1_down, v_norm_mix, v_w_in, v_ssm_log_dt, v_ssm_a_re, v_ssm_a_im, v_ssm_b_re, v_ssm_b_im, v_ssm_c_re, v_ssm_c_im, v_ssm_d, v_ssm_w_glu, v_gmlp_norm_v, v_gmlp_w_s, v_gmlp_b_s, v_norm_ssm_out, v_norm_gmlp_out, v_w_out, v_norm_ffn2, v_w2_gate, v_w2_up, v_w2_down, v_norm_ple, v_w_ple_gate, v_w_ple_proj, v_norm_final):
    given = dict(locals())
    shapes = {n: given[n].shape for n in WEIGHTS}

    def block(name):
        a = given[name]
        if a.ndim == 1:
            return a.reshape(1, -1)
        return a[0] if a.ndim >= 3 else a

    w = {n: block(n) for n in WEIGHTS}
    m = {n: block("m_" + n) for n in WEIGHTS}
    v = {n: block("v_" + n) for n in WEIGHTS}
    loss, dx, grad, delta, new_m, new_v = _step(x[0], p[0, 0], loss_target[0], w, m, v)
    outs = [loss, dx[None]]
    for tree in (grad, delta, new_m, new_v):
        outs += [tree[n].reshape(shapes[n]) for n in WEIGHTS]
    return tuple(outs)
```

```python
import functools
import math

import jax
import jax.numpy as jnp
from jax import lax
from jax.experimental import pallas as pl
from jax.experimental.pallas import tpu as pltpu

F32 = jnp.float32
BF16 = jnp.bfloat16
EPS = 1e-6
SSM_GROUP = 16
SSM_STATE = 64
GROUPS_PER_BLOCK = 8
GMLP_HEAD = 128
CHUNK = 128
ADAM_LR = 0.001
ADAM_B1 = 0.9
ADAM_B2 = 0.999
ADAM_EPS = 1e-08
ADAM_WD = 0.01
ADAM_STEP = 10
N_CHIPS = 4
N_DEV = 8
LANES = 128
VMEM_LIMIT_BYTES = 56 * 1024 * 1024
MESH = pl.DeviceIdType.MESH
GELU_C = math.sqrt(2.0 / math.pi)
GELU_A = 0.044715

_DOT_DIMS = {
    "nn": (((1,), (0,)), ((), ())),
    "nt": (((1,), (1,)), ((), ())),
    "tn": (((0,), (0,)), ((), ())),
}


def _tile(dim, pref, align):
    if dim <= pref:
        return dim
    t = (pref // align) * align
    while t >= align:
        if dim % t == 0:
            return t
        t -= align
    return dim


def _params(semantics):
    return pltpu.CompilerParams(dimension_semantics=semantics, vmem_limit_bytes=VMEM_LIMIT_BYTES)


def _gelu(x):
    return 0.5 * x * (1.0 + jnp.tanh(GELU_C * (x + GELU_A * x * x * x)))


def _gelu_grad(x):
    t = jnp.tanh(GELU_C * (x + GELU_A * x * x * x))
    return 0.5 * (1.0 + t) + 0.5 * x * (1.0 - t * t) * GELU_C * (1.0 + 3.0 * GELU_A * x * x)


def _sigmoid(x):
    return 1.0 / (1.0 + jnp.exp(-x))


def _dot(a, b, mode):
    return lax.dot_general(a.astype(BF16), b.astype(BF16), _DOT_DIMS[mode], preferred_element_type=F32)


class _Carry:
    def __init__(self, arrays, out_shapes, aliases, n_copies, copies):
        self.arrays = list(arrays)
        self.out_shapes = list(out_shapes)
        self.aliases = dict(aliases)
        self.n_copies = n_copies
        self.copies = copies

    def scratch(self):
        return [pltpu.SemaphoreType.DMA((self.n_copies,)), pltpu.SemaphoreType.DMA((self.n_copies,))]

    def split(self, refs):
        n_in, n_out = len(self.arrays), len(self.out_shapes)
        return refs[:n_in], refs[n_in:n_in + n_out], refs[n_in + n_out], refs[n_in + n_out + 1]

    def start(self, refs):
        for cp in self.copies(*self.split(refs)):
            cp.start()

    def wait(self, refs):
        for cp in self.copies(*self.split(refs)):
            cp.wait()


class _SemRange:
    def __init__(self, sems, offset):
        self.sems, self.offset = sems, offset

    @property
    def at(self):
        return self

    def __getitem__(self, k):
        return self.sems.at[self.offset + k]


def _join(first, second):
    n_in, n_out = len(first.arrays), len(first.out_shapes)
    aliases = dict(first.aliases)
    aliases.update({n_in + i: n_out + o for i, o in second.aliases.items()})

    def copies(ops, res, send_sems, recv_sems):
        return (first.copies(ops[:n_in], res[:n_out], send_sems, recv_sems)
                + second.copies(ops[n_in:], res[n_out:], _SemRange(send_sems, first.n_copies),
                                _SemRange(recv_sems, first.n_copies)))

    return _Carry(first.arrays + second.arrays, first.out_shapes + second.out_shapes, aliases,
                  first.n_copies + second.n_copies, copies)


_ANY = pl.BlockSpec(memory_space=pl.ANY)


def _comm_call(name, carry):
    def body(*refs):
        carry.start(refs)
        carry.wait(refs)

    n_in = len(carry.arrays)
    return pl.pallas_call(
        body,
        name=name,
        in_specs=[_ANY] * n_in,
        out_specs=[_ANY] * len(carry.out_shapes),
        out_shape=carry.out_shapes,
        input_output_aliases=carry.aliases,
        scratch_shapes=carry.scratch(),
    )(*carry.arrays)


def _carried_call(body, carry, *, name, grid, in_specs, out_specs, out_shape, scratch_shapes, semantics, args):
    if carry is None:
        res = pl.pallas_call(body, name=name, grid=grid, in_specs=in_specs, out_specs=out_specs, out_shape=out_shape,
                             scratch_shapes=scratch_shapes, compiler_params=_params(semantics))(*args)
        return res, []
    n_in, n_out, n_scr = len(in_specs), len(out_specs), len(scratch_shapes)
    nci, nco = len(carry.arrays), len(carry.out_shapes)

    def wrapped(*refs):
        ins = refs[:n_in]
        outs = refs[n_in + nci:n_in + nci + n_out]
        scr = refs[n_in + nci + n_out + nco:n_in + nci + n_out + nco + n_scr]
        c_refs = (refs[n_in:n_in + nci] + refs[n_in + nci + n_out:n_in + nci + n_out + nco]
                  + refs[n_in + nci + n_out + nco + n_scr:])
        first = functools.reduce(jnp.logical_and, [pl.program_id(d) == 0 for d in range(len(grid))])
        last = functools.reduce(jnp.logical_and, [pl.program_id(d) == grid[d] - 1 for d in range(len(grid))])

        @pl.when(first)
        def _():
            carry.start(c_refs)

        body(*ins, *outs, *scr)

        @pl.when(last)
        def _():
            carry.wait(c_refs)

    res = pl.pallas_call(
        wrapped,
        name=name,
        grid=grid,
        in_specs=list(in_specs) + [_ANY] * nci,
        out_specs=list(out_specs) + [_ANY] * nco,
        out_shape=list(out_shape) + carry.out_shapes,
        input_output_aliases={n_in + i: n_out + o for i, o in carry.aliases.items()},
        scratch_shapes=list(scratch_shapes) + carry.scratch(),
        compiler_params=_params(("arbitrary",) * len(grid)),
    )(*args, *carry.arrays)
    return res[:n_out], res[n_out:]


def _matmul(name, mode, a_list, b_list, products, out_dtypes, epilogue, extras=(), tm=512, tn=512, tk=2048,
            carry=None):
    a0, b0 = a_list[0], b_list[0]
    if mode == "tn":
        k_dim, m_dim = a0.shape
    else:
        m_dim, k_dim = a0.shape
    n_dim = b0.shape[0] if mode == "nt" else b0.shape[1]
    tm = _tile(m_dim, tm, LANES)
    tn = _tile(n_dim, tn, LANES)
    tk = _tile(k_dim, tk, LANES)
    nk = k_dim // tk
    chunk = 2 * LANES if (nk == 1 and epilogue is not _identity and tn % (2 * LANES) == 0) else tn
    n_acc = 1 + max(p[2] for p in products)
    na, nb, ne, no = len(a_list), len(b_list), len(extras), len(out_dtypes)

    if mode == "tn":
        a_spec = pl.BlockSpec((tk, tm), lambda i, j, k: (k, i))
    else:
        a_spec = pl.BlockSpec((tm, tk), lambda i, j, k: (i, k))
    if mode == "nt":
        b_spec = pl.BlockSpec((tn, tk), lambda i, j, k: (j, k))
    else:
        b_spec = pl.BlockSpec((tk, tn), lambda i, j, k: (k, j))
    t_spec = pl.BlockSpec((tm, tn), lambda i, j, k: (i, j))

    def body(*refs):
        a_refs = refs[:na]
        b_refs = refs[na:na + nb]
        e_refs = refs[na + nb:na + nb + ne]
        o_refs = refs[na + nb + ne:na + nb + ne + no]
        acc_refs = refs[na + nb + ne + no:]

        def partial_sums(cols):
            sums = [None] * n_acc
            for ai, bi, ci in products:
                b = b_refs[bi][cols, :] if mode == "nt" else b_refs[bi][:, cols]
                d = _dot(a_refs[ai][...], b, mode)
                sums[ci] = d if sums[ci] is None else sums[ci] + d
            return sums

        def finish(accs, cols):
            outs = epilogue(accs, [e[:, cols] for e in e_refs])
            for o_ref, o in zip(o_refs, outs):
                o_ref[:, cols] = o.astype(o_ref.dtype)

        if nk == 1:
            for c0 in range(0, tn, chunk):
                finish(partial_sums(slice(c0, c0 + chunk)), slice(c0, c0 + chunk))
        else:
            sums = partial_sums(slice(None))
            finish = functools.partial(finish, cols=slice(None))
            k = pl.program_id(2)

            @pl.when(k == 0)
            def _():
                for acc, s in zip(acc_refs, sums):
                    acc[...] = s

            @pl.when(k > 0)
            def _():
                for acc, s in zip(acc_refs, sums):
                    acc[...] += s

            @pl.when(k == nk - 1)
            def _():
                finish([acc[...] for acc in acc_refs])

    scratch = [pltpu.VMEM((tm, tn), F32) for _ in range(n_acc)] if nk > 1 else []
    outs, carried = _carried_call(
        body, carry,
        name=name,
        grid=(m_dim // tm, n_dim // tn, nk),
        in_specs=[a_spec] * na + [b_spec] * nb + [t_spec] * ne,
        out_specs=[t_spec] * no,
        out_shape=[jax.ShapeDtypeStruct((m_dim, n_dim), dt) for dt in out_dtypes],
        scratch_shapes=scratch,
        semantics=("parallel", "parallel", "arbitrary"),
        args=[*a_list, *b_list, *extras],
    )
    return (outs, carried) if carry else outs


def _identity(accs, extras):
    return accs


def _single(result, carry):
    return (result[0][0], result[1]) if carry else result[0]


def _mm_nn(name, a, b, out_dtype, res=None, alpha=1.0, carry=None, **tiles):
    if res is None:
        return _single(_matmul(name, "nn", [a], [b], [(0, 0, 0)], [out_dtype], _identity, carry=carry, **tiles), carry)

    def epilogue(accs, extras):
        return [extras[0] + alpha * accs[0]]

    return _single(_matmul(name, "nn", [a], [b], [(0, 0, 0)], [out_dtype], epilogue, extras=(res,), carry=carry,
                           **tiles), carry)


def _mm_nt_sum(name, a_list, b_list, out_dtype, carry=None, **tiles):
    products = [(i, i, 0) for i in range(len(a_list))]
    return _single(_matmul(name, "nt", a_list, b_list, products, [out_dtype], _identity, carry=carry, **tiles), carry)


def _mm_tn(name, a, b_list, out_dtype, carry=None, **tiles):
    products = [(0, i, i) for i in range(len(b_list))]
    return _matmul(name, "tn", [a], b_list, products, [out_dtype] * len(b_list), _identity, carry=carry, **tiles)


def _rowwise(name, fn, row_ins, par_ins, row_outs, acc_outs=(), tr=512):
    first = row_ins[0][0] if isinstance(row_ins[0], tuple) else row_ins[0]
    t_dim = first.shape[0]
    tr = _tile(t_dim, tr, 16)
    arrays, specs = [], []
    for r in row_ins:
        if isinstance(r, tuple):
            arr, width, blk = r
            specs.append(pl.BlockSpec((tr, width), lambda i, blk=blk: (i, blk)))
        else:
            arr = r
            specs.append(pl.BlockSpec((tr, arr.shape[1]), lambda i: (i, 0)))
        arrays.append(arr)
    for p in par_ins:
        arrays.append(p)
        specs.append(pl.BlockSpec(p.shape, lambda i, nd=p.ndim: (0,) * nd))
    nr, npar, nro, nacc = len(row_ins), len(par_ins), len(row_outs), len(acc_outs)

    def body(*refs):
        rows = [r[...] for r in refs[:nr]]
        pars = [p[...] for p in refs[nr:nr + npar]]
        o_refs = refs[nr + npar:nr + npar + nro]
        acc_refs = refs[nr + npar + nro:]
        outs, accs = fn(rows, pars)
        for o_ref, o in zip(o_refs, outs):
            o_ref[...] = o.astype(o_ref.dtype)
        if nacc:
            @pl.when(pl.program_id(0) == 0)
            def _():
                for a_ref in acc_refs:
                    a_ref[...] = jnp.zeros_like(a_ref)

            for a_ref, a in zip(acc_refs, accs):
                a_ref[...] += a

    out_shape = [jax.ShapeDtypeStruct((t_dim, c), dt) for c, dt in row_outs]
    out_shape += [jax.ShapeDtypeStruct(s, F32) for s in acc_outs]
    out_specs = [pl.BlockSpec((tr, c), lambda i: (i, 0)) for c, _ in row_outs]
    out_specs += [pl.BlockSpec(s, lambda i: (0, 0)) for s in acc_outs]
    return pl.pallas_call(
        body,
        name=name,
        grid=(t_dim // tr,),
        in_specs=specs,
        out_specs=out_specs,
        out_shape=out_shape,
        compiler_params=_params(("arbitrary",)),
    )(*arrays)


def _rms_stats(x):
    r = lax.rsqrt(jnp.mean(x * x, axis=-1, keepdims=True) + EPS)
    return x * r, r


def _rms_backward(x, g, dy):
    xh, r = _rms_stats(x)
    a = dy * g
    dx = r * (a - xh * jnp.mean(a * xh, axis=-1, keepdims=True))
    return dx, jnp.sum(dy * xh, axis=0, keepdims=True)


def _rms_fwd(name, x, g):
    def fn(rows, pars):
        xh, _ = _rms_stats(rows[0])
        return [xh * pars[0]], []

    return _rowwise(name, fn, [x], [g], [(x.shape[1], BF16)])[0]


def _rms_bwd(name, x, g, dy, dres, scale):
    def fn(rows, pars):
        dx, dg = _rms_backward(rows[0], pars[0], rows[1])
        tot = rows[2] + dx
        return [tot, scale * tot], [dg]

    d = x.shape[1]
    return _rowwise(name, fn, [x, dy, dres], [g], [(d, F32), (d, BF16)], [(1, d)], tr=256)


def _cast_into_gathered(name, w, kind, q_arr):
    rs, cs = w.shape
    tr = _tile(rs, 256, 16)
    nr = rs // tr

    def body(q_ref, w_ref, o_ref):
        o_ref[...] = w_ref[...].astype(BF16)

    if kind == "row":
        o_spec = pl.BlockSpec((tr, cs), lambda i, q_ref: (q_ref[0] * nr + i, 0))
    else:
        o_spec = pl.BlockSpec((tr, cs), lambda i, q_ref: (i, q_ref[0]))
    return pl.pallas_call(
        body,
        name=name,
        grid_spec=pltpu.PrefetchScalarGridSpec(num_scalar_prefetch=1, grid=(nr,),
                                               in_specs=[pl.BlockSpec((tr, cs), lambda i, q_ref: (i, 0))],
                                               out_specs=o_spec),
        out_shape=jax.ShapeDtypeStruct(_full_shape(kind, (rs, cs)), BF16),
        compiler_params=_params(("parallel",)),
    )(q_arr, w)


def _ffn_gateup(tag, n, wg, wu, carry):
    def act(accs, extras):
        gate, up = accs
        return [gate * _sigmoid(gate) * up, gate, up]

    return _matmul(tag + "_gateup", "nn", [n], [wg, wu], [(0, 0, 0), (0, 1, 1)], [BF16] * 3, act,
                   tm=1024, tn=512, tk=2048, carry=carry)


def _ffn_up(tag, n, wu, gate, carry):
    def act(accs, extras):
        g = extras[0].astype(F32)
        return [g * _sigmoid(g) * accs[0], accs[0]]

    return _matmul(tag + "_up", "nn", [n], [wu], [(0, 0, 0)], [BF16] * 2, act, extras=(gate,),
                   tm=1024, tn=512, tk=2048, carry=carry)


def _ffn_down(tag, a, wd, h, carry=None):
    return _mm_nn(tag + "_down", a, wd, F32, res=h, alpha=0.5, tm=512, tn=512, tk=5632, carry=carry)


def _ffn_bwd(tag, names, h, g, wg, wu, wd, saved, dh, dfb, next_scale, reducer, riding=None):
    n, a, gate, up = saved

    def act_bwd(accs, extras):
        da = accs[0]
        gt, u = extras[0].astype(F32), extras[1].astype(F32)
        s = _sigmoid(gt)
        return [da * u * (s * (1.0 + gt * (1.0 - s))), da * (gt * s)]

    dact = _matmul(tag + "_dact", "nt", [dfb], [wd], [(0, 0, 0)], [BF16, BF16], act_bwd, extras=(gate, up),
                   tm=1024, tn=512, tk=2048, carry=riding)
    (dgp, du), rode = dact if riding else (dact, [])
    (dwd,) = _mm_tn(tag + "_dwd", a, [dfb], BF16, tm=512, tn=2048, tk=2048)
    trip = reducer.begin(names[2:], [dwd])
    (dwg,), got = _mm_tn(tag + "_dwg", n, [dgp], BF16, tm=1024, tn=512, tk=2048, carry=trip.carry)
    reducer.end(trip, got)
    trip = reducer.begin(names[:1], [dwg])
    (dwu,), got = _mm_tn(tag + "_dwu", n, [du], BF16, tm=1024, tn=512, tk=2048, carry=trip.carry)
    reducer.end(trip, got)
    trip = reducer.begin(names[1:2], [dwu])
    dn, got = _mm_nt_sum(tag + "_dn", [dgp, du], [wg, wu], F32, tm=1024, tn=1024, tk=1408, carry=trip.carry)
    reducer.end(trip, got)
    dh_in, dh_in_b, dg = _rms_bwd(tag + "_dnorm", h, g, dn, dh, next_scale)
    return dh_in, dh_in_b, dg, rode


def _ssm_discretize(log_dt, a_re, a_im, b_re, b_im):
    dt = jnp.exp(log_dt)[:, None]
    lr = jnp.minimum(a_re, -1e-4)
    li = a_im
    mag = jnp.exp(lr * dt)
    ang = li * dt
    abar_r = mag * jnp.cos(ang)
    abar_i = mag * jnp.sin(ang)
    den = lr * lr + li * li
    xr = abar_r - 1.0
    xi = abar_i
    zr = (xr * lr + xi * li) / den
    zi = (xi * lr - xr * li) / den
    bbar_r = zr[..., None] * b_re - zi[..., None] * b_im
    bbar_i = zr[..., None] * b_im + zi[..., None] * b_re
    return abar_r, abar_i, bbar_r, bbar_i


def _blockdiag_in(b):
    g, n, p = b.shape
    nb = g // GROUPS_PER_BLOCK
    eye = jnp.eye(GROUPS_PER_BLOCK, dtype=b.dtype)
    b4 = b.reshape(nb, GROUPS_PER_BLOCK, n, p)
    return jnp.einsum("sgnp,gh->sgphn", b4, eye).reshape(nb, GROUPS_PER_BLOCK * p, GROUPS_PER_BLOCK * n)


def _blockdiag_in_grad(gw, n, p):
    nb = gw.shape[0]
    eye = jnp.eye(GROUPS_PER_BLOCK, dtype=gw.dtype)
    g5 = gw.reshape(nb, GROUPS_PER_BLOCK, p, GROUPS_PER_BLOCK, n)
    return jnp.einsum("sgphn,gh->sgnp", g5, eye).reshape(nb * GROUPS_PER_BLOCK, n, p)


def _blockdiag_out(c):
    g, p, n = c.shape
    nb = g // GROUPS_PER_BLOCK
    eye = jnp.eye(GROUPS_PER_BLOCK, dtype=c.dtype)
    c4 = c.reshape(nb, GROUPS_PER_BLOCK, p, n)
    return jnp.einsum("sgpn,gh->shngp", c4, eye).reshape(nb, GROUPS_PER_BLOCK * n, GROUPS_PER_BLOCK * p)


def _blockdiag_out_grad(gw, p, n):
    nb = gw.shape[0]
    eye = jnp.eye(GROUPS_PER_BLOCK, dtype=gw.dtype)
    g5 = gw.reshape(nb, GROUPS_PER_BLOCK, n, GROUPS_PER_BLOCK, p)
    return jnp.einsum("shngp,gh->sgpn", g5, eye).reshape(nb * GROUPS_PER_BLOCK, p, n)


def _ssm_scan_fwd(z, wb, wc, abar, d, carry=None):
    t_dim = z.shape[0]
    nb, cb, sw2 = wb.shape
    nl = sw2 // LANES
    hl = nl // 2
    tt = _tile(t_dim, 256, 8)
    nt = t_dim // tt

    def body(z_ref, wb_ref, wc_ref, a_ref, d_ref, s_ref, y_ref, yg_ref, drive_ref, st_ref):
        @pl.when(pl.program_id(0) == 0)
        def _():
            st_ref[...] = jnp.zeros_like(st_ref)

        u = z_ref[...]
        ub = u.astype(BF16)
        for b in range(nb):
            drive = _dot(ub[:, b * cb:(b + 1) * cb], wb_ref[b], "nn")
            for l in range(nl):
                drive_ref[l, pl.ds(b, tt, stride=nb), :] = drive[:, l * LANES:(l + 1) * LANES]
        a = a_ref[...]
        chunk = lambda v, l: v[:, l * LANES:(l + 1) * LANES]

        def step(t, state):
            rows = pl.ds(pl.multiple_of(t * nb, nb), nb)
            re, im = [], []
            for l in range(hl):
                ar, ai, sr, si = chunk(a, l), chunk(a, hl + l), state[l], state[hl + l]
                nr = ar * sr - ai * si + drive_ref[l, rows, :]
                ni = ar * si + ai * sr + drive_ref[hl + l, rows, :]
                s_ref[l, rows, :] = nr
                s_ref[hl + l, rows, :] = ni
                re.append(nr)
                im.append(ni)
            return tuple(re + im)

        state = lax.fori_loop(0, tt, step, tuple(st_ref[l] for l in range(nl)), unroll=8)
        for l in range(nl):
            st_ref[l] = state[l]
        parts = []
        for b in range(nb):
            s_b = jnp.concatenate([s_ref[l, pl.ds(b, tt, stride=nb), :] for l in range(nl)], axis=1)
            parts.append(_dot(s_b, wc_ref[b], "nn"))
        y = jnp.concatenate(parts, axis=1) + d_ref[...] * u
        y_ref[...] = y
        yg_ref[...] = _gelu(y).astype(BF16)

    full = lambda a: pl.BlockSpec(a.shape, lambda t, nd=a.ndim: (0,) * nd)
    return _carried_call(
        body, carry,
        name="ssm_scan_fwd",
        grid=(nt,),
        in_specs=[pl.BlockSpec((tt, nb * cb), lambda t: (t, 0)), full(wb), full(wc), full(abar), full(d)],
        out_specs=[
            pl.BlockSpec((nl, tt * nb, LANES), lambda t: (0, t, 0)),
            pl.BlockSpec((tt, nb * cb), lambda t: (t, 0)),
            pl.BlockSpec((tt, nb * cb), lambda t: (t, 0)),
        ],
        out_shape=[
            jax.ShapeDtypeStruct((nl, t_dim * nb, LANES), F32),
            jax.ShapeDtypeStruct((t_dim, nb * cb), F32),
            jax.ShapeDtypeStruct((t_dim, nb * cb), BF16),
        ],
        scratch_shapes=[pltpu.VMEM((nl, tt * nb, LANES), F32), pltpu.VMEM((nl, nb, LANES), F32)],
        semantics=("arbitrary",),
        args=[z, wb, wc, abar, d],
    )


def _ssm_scan_bwd(z, dy, states, wb, wc, abar_conj, d, carry=None):
    t_dim = z.shape[0]
    nb, cb, sw2 = wb.shape
    nl = sw2 // LANES
    hl = nl // 2
    tt = _tile(t_dim, 128, 8)
    nt = t_dim // tt
    edges = states.reshape(nl, nt, tt * nb, LANES)[:, :, (tt - 1) * nb:, :]
    before = jnp.concatenate([jnp.zeros((nl, 1, nb, LANES), F32), edges[:, :-1]], axis=1).reshape(nl, nt * nb, LANES)

    def body(z_ref, dy_ref, s_ref, sp_ref, wb_ref, wc_ref, a_ref, d_ref,
             dz_ref, gwb_ref, gwc_ref, ga_ref, gd_ref, gin_ref, gs_ref, st_ref):
        @pl.when(pl.program_id(0) == 0)
        def _():
            st_ref[...] = jnp.zeros_like(st_ref)
            gwb_ref[...] = jnp.zeros_like(gwb_ref)
            gwc_ref[...] = jnp.zeros_like(gwc_ref)
            ga_ref[...] = jnp.zeros_like(ga_ref)
            gd_ref[...] = jnp.zeros_like(gd_ref)

        u = z_ref[...]
        dyv = dy_ref[...]
        ub = u.astype(BF16)
        dyb = dyv.astype(BF16)
        for b in range(nb):
            gin = _dot(dyb[:, b * cb:(b + 1) * cb], wc_ref[b], "nt")
            for l in range(nl):
                gin_ref[l, pl.ds(b, tt, stride=nb), :] = gin[:, l * LANES:(l + 1) * LANES]
        a = a_ref[...]
        chunk = lambda v, l: v[:, l * LANES:(l + 1) * LANES]

        def step(k, state):
            rows = pl.ds(pl.multiple_of((tt - 1 - k) * nb, nb), nb)
            re, im = [], []
            for l in range(hl):
                ar, ai, gr, gi = chunk(a, l), chunk(a, hl + l), state[l], state[hl + l]
                nr = ar * gr - ai * gi + gin_ref[l, rows, :]
                ni = ar * gi + ai * gr + gin_ref[hl + l, rows, :]
                gs_ref[l, rows, :] = nr
                gs_ref[hl + l, rows, :] = ni
                re.append(nr)
                im.append(ni)
            return tuple(re + im)

        state = lax.fori_loop(0, tt, step, tuple(st_ref[l] for l in range(nl)), unroll=8)
        for l in range(nl):
            st_ref[l] = state[l]

        parts = []
        for b in range(nb):
            cols = slice(b * cb, (b + 1) * cb)
            gs_b = jnp.concatenate([gs_ref[l, pl.ds(b, tt, stride=nb), :] for l in range(nl)], axis=1)
            s_b = jnp.concatenate([s_ref[l, pl.ds(b, tt, stride=nb), :] for l in range(nl)], axis=1)
            parts.append(_dot(gs_b, wb_ref[b], "nt"))
            gwb_ref[b] += _dot(ub[:, cols], gs_b, "tn")
            gwc_ref[b] += _dot(s_b, dyb[:, cols], "tn")
        dz_ref[...] = (jnp.concatenate(parts, axis=1) + d_ref[...] * dyv).astype(BF16)
        gd_ref[...] += jnp.sum(dyv * u, axis=0, keepdims=True)

        row = lax.broadcasted_iota(jnp.int32, (tt * nb, LANES), 0)
        shifted = lambda v: jnp.where(row < nb, 0.0, pltpu.roll(v, nb, 0))
        over_time = lambda v: jnp.sum(v.reshape(tt, nb, LANES), axis=0)
        for l in range(hl):
            g_r, g_i = gs_ref[l], gs_ref[hl + l]
            p_r, p_i = shifted(s_ref[l]), shifted(s_ref[hl + l])
            f_r, f_i = sp_ref[l], sp_ref[hl + l]
            g0_r, g0_i = gs_ref[l, pl.ds(0, nb), :], gs_ref[hl + l, pl.ds(0, nb), :]
            ga_ref[l] += over_time(g_r * p_r + g_i * p_i) + g0_r * f_r + g0_i * f_i
            ga_ref[hl + l] += over_time(g_i * p_r - g_r * p_i) + g0_i * f_r - g0_r * f_i

    rev = lambda t: (nt - 1 - t, 0)
    rev3 = lambda t: (0, nt - 1 - t, 0)
    full = lambda a: pl.BlockSpec(a.shape, lambda t, nd=a.ndim: (0,) * nd)
    return _carried_call(
        body, carry,
        name="ssm_scan_bwd",
        grid=(nt,),
        in_specs=[
            pl.BlockSpec((tt, nb * cb), rev),
            pl.BlockSpec((tt, nb * cb), rev),
            pl.BlockSpec((nl, tt * nb, LANES), rev3),
            pl.BlockSpec((nl, nb, LANES), rev3),
            full(wb), full(wc), full(abar_conj), full(d),
        ],
        out_specs=[
            pl.BlockSpec((tt, nb * cb), rev),
            pl.BlockSpec((nb, cb, sw2), lambda t: (0, 0, 0)),
            pl.BlockSpec((nb, sw2, cb), lambda t: (0, 0, 0)),
            pl.BlockSpec((nl, nb, LANES), lambda t: (0, 0, 0)),
            pl.BlockSpec((1, nb * cb), lambda t: (0, 0)),
        ],
        out_shape=[
            jax.ShapeDtypeStruct((t_dim, nb * cb), BF16),
            jax.ShapeDtypeStruct((nb, cb, sw2), F32),
            jax.ShapeDtypeStruct((nb, sw2, cb), F32),
            jax.ShapeDtypeStruct((nl, nb, LANES), F32),
            jax.ShapeDtypeStruct((1, nb * cb), F32),
        ],
        scratch_shapes=[pltpu.VMEM((nl, tt * nb, LANES), F32), pltpu.VMEM((nl, tt * nb, LANES), F32),
                        pltpu.VMEM((nl, nb, LANES), F32)],
        semantics=("arbitrary",),
        args=[z, dy, states, before, wb, wc, abar_conj, d],
    )


def _gmlp_chunk(zu, zv, gv, wm_ref, bias, n_heads):
    ua = _gelu(zu)
    vg = _gelu(zv)
    xc = vg - jnp.mean(vg, axis=-1, keepdims=True)
    r = lax.rsqrt(jnp.mean(xc * xc, axis=-1, keepdims=True) + EPS)
    vh = xc * r
    vb = (vh * gv).astype(BF16)
    parts = []
    for h in range(n_heads):
        cols = slice(h * GMLP_HEAD, (h + 1) * GMLP_HEAD)
        parts.append(_dot(wm_ref[h], vb[:, cols], "nn"))
    s = jnp.concatenate(parts, axis=1) + bias
    return ua, vh, r, vb, s


def _gmlp_fwd(z, gv, wm, bias, ggo):
    t_dim = z.shape[0]
    dg = gv.shape[1]
    n_heads = dg // GMLP_HEAD
    tr = _tile(t_dim, 256, CHUNK)

    def body(zu_ref, zv_ref, gv_ref, wm_ref, b_ref, ggo_ref, o_ref):
        for ck in range(tr // CHUNK):
            rows = pl.ds(ck * CHUNK, CHUNK)
            ua, _, _, _, s = _gmlp_chunk(zu_ref[rows, :], zv_ref[rows, :], gv_ref[...], wm_ref, b_ref[...], n_heads)
            yh, _ = _rms_stats(ua * s)
            o_ref[rows, :] = (yh * ggo_ref[...]).astype(BF16)

    full = lambda a: pl.BlockSpec(a.shape, lambda i, nd=a.ndim: (0,) * nd)
    return pl.pallas_call(
        body,
        name="gmlp_fwd",
        grid=(t_dim // tr,),
        in_specs=[pl.BlockSpec((tr, dg), lambda i: (i, 1)), pl.BlockSpec((tr, dg), lambda i: (i, 2)),
                  full(gv), full(wm), full(bias), full(ggo)],
        out_specs=pl.BlockSpec((tr, dg), lambda i: (i, 0)),
        out_shape=jax.ShapeDtypeStruct((t_dim, dg), BF16),
        compiler_params=_params(("parallel",)),
    )(z, z, gv, wm, bias, ggo)


def _gmlp_bwd(z, dycat, gv, wm, bias, ggo):
    t_dim = z.shape[0]
    dg = gv.shape[1]
    n_heads = dg // GMLP_HEAD
    tr = _tile(t_dim, 256, CHUNK)

    def body(zu_ref, zv_ref, dy_ref, gv_ref, wm_ref, b_ref, ggo_ref,
             dzu_ref, dzv_ref, dggo_ref, dgv_ref, dwm_ref, dsum_ref):
        @pl.when(pl.program_id(0) == 0)
        def _():
            dggo_ref[...] = jnp.zeros_like(dggo_ref)
            dgv_ref[...] = jnp.zeros_like(dgv_ref)
            dwm_ref[...] = jnp.zeros_like(dwm_ref)
            dsum_ref[...] = jnp.zeros_like(dsum_ref)

        for ck in range(tr // CHUNK):
            rows = pl.ds(ck * CHUNK, CHUNK)
            zu = zu_ref[rows, :]
            zv = zv_ref[rows, :]
            gvv = gv_ref[...]
            ua, vh, r, vb, s = _gmlp_chunk(zu, zv, gvv, wm_ref, b_ref[...], n_heads)
            dy, dggo = _rms_backward(ua * s, ggo_ref[...], dy_ref[rows, :])
            dggo_ref[...] += dggo
            ds = dy * ua
            dsum_ref[...] += ds
            dsb = ds.astype(BF16)
            parts = []
            for h in range(n_heads):
                cols = slice(h * GMLP_HEAD, (h + 1) * GMLP_HEAD)
                dwm_ref[h] += _dot(dsb[:, cols], vb[:, cols], "nt")
                parts.append(_dot(wm_ref[h], dsb[:, cols], "tn"))
            dv = jnp.concatenate(parts, axis=1)
            dgv_ref[...] += jnp.sum(dv * vh, axis=0, keepdims=True)
            dvh = dv * gvv
            dvg = r * (dvh - jnp.mean(dvh, axis=-1, keepdims=True) - vh * jnp.mean(dvh * vh, axis=-1, keepdims=True))
            dzv_ref[rows, :] = (dvg * _gelu_grad(zv)).astype(BF16)
            dzu_ref[rows, :] = (dy * s * _gelu_grad(zu)).astype(BF16)

    full = lambda a: pl.BlockSpec(a.shape, lambda i, nd=a.ndim: (0,) * nd)
    return pl.pallas_call(
        body,
        name="gmlp_bwd",
        grid=(t_dim // tr,),
        in_specs=[pl.BlockSpec((tr, dg), lambda i: (i, 1)), pl.BlockSpec((tr, dg), lambda i: (i, 2)),
                  pl.BlockSpec((tr, dg), lambda i: (i, 1)), full(gv), full(wm), full(bias), full(ggo)],
        out_specs=[pl.BlockSpec((tr, dg), lambda i: (i, 0)), pl.BlockSpec((tr, dg), lambda i: (i, 0)),
                   pl.BlockSpec((1, dg), lambda i: (0, 0)), pl.BlockSpec((1, dg), lambda i: (0, 0)),
                   pl.BlockSpec(wm.shape, lambda i: (0, 0, 0)), pl.BlockSpec((CHUNK, dg), lambda i: (0, 0))],
        out_shape=[jax.ShapeDtypeStruct((t_dim, dg), BF16), jax.ShapeDtypeStruct((t_dim, dg), BF16),
                   jax.ShapeDtypeStruct((1, dg), F32), jax.ShapeDtypeStruct((1, dg), F32),
                   jax.ShapeDtypeStruct(wm.shape, F32), jax.ShapeDtypeStruct((CHUNK, dg), F32)],
        compiler_params=_params(("arbitrary",)),
    )(z, z, dycat, gv, wm, bias, ggo)


def _position():
    x, y, c = lax.axis_index("x"), lax.axis_index("y"), lax.axis_index("c")
    chips = [(1 - x, y), (x, 1 - y), (1 - x, 1 - y)]
    return x, y, c, chips


def _region(ref, kind, shard_shape, q, half):
    rs, cs = shard_shape
    r0, nr = (0, rs) if half is None else (half * (rs // 2), rs // 2)
    if kind == "row":
        return ref.at[pl.ds(q * rs + r0, nr), :]
    return ref.at[pl.ds(r0, nr), pl.ds(q * cs, cs)]


def _full_shape(kind, shard_shape):
    rs, cs = shard_shape
    return (N_CHIPS * rs, cs) if kind == "row" else (rs, N_CHIPS * cs)


def _remote(src, dst, send_sems, recv_sems, k, to):
    return pltpu.make_async_remote_copy(src_ref=src, dst_ref=dst, send_sem=send_sems.at[k], recv_sem=recv_sems.at[k],
                                        device_id=to, device_id_type=MESH)


def _same(arrays):
    return [jax.ShapeDtypeStruct(a.shape, a.dtype) for a in arrays]


def _gather_ici_carry(gathered, kinds, shapes):
    nw = len(gathered)

    def copies(ops, full, send_sems, recv_sems):
        x, y, c, chips = _position()
        out = []
        for w in range(nw):
            mine = _region(full[w], kinds[w], shapes[w], 2 * x + y, c)
            for j, (cx, cy) in enumerate(chips):
                out.append(_remote(mine, mine, send_sems, recv_sems, 3 * w + j, (cx, cy, c)))
        return out

    return _Carry(gathered, _same(gathered), {i: i for i in range(nw)}, 3 * nw, copies)


def _gather_d2d_carry(gathered, kinds, shapes):
    nw = len(gathered)

    def copies(ops, full, send_sems, recv_sems):
        x, y, c, chips = _position()
        out = []
        for w in range(nw):
            for j, (cx, cy) in enumerate(chips):
                landed = _region(full[w], kinds[w], shapes[w], 2 * cx + cy, c)
                out.append(_remote(landed, landed, send_sems, recv_sems, 3 * w + j, (x, y, 1 - c)))
        return out

    return _Carry(gathered, _same(gathered), {i: i for i in range(nw)}, 3 * nw, copies)


def _pairs_carry(grads, kinds, shapes):
    nw = len(grads)

    def copies(g, got, send_sems, recv_sems):
        x, y, c, _ = _position()
        out = []
        for w in range(nw):
            for q in range(N_CHIPS):
                out.append(_remote(_region(g[w], kinds[w], shapes[w], q, 1 - c), got[w].at[q], send_sems, recv_sems,
                                   N_CHIPS * w + q, (x, y, 1 - c)))
        return out

    outs = [jax.ShapeDtypeStruct((N_CHIPS, s[0] // 2, s[1]), BF16) for s in shapes]
    return _Carry(grads, outs, {}, N_CHIPS * nw, copies)


def _pair_sum(name, grad, got, kind, shard_shape, c_arr):
    rs, cs = shard_shape
    hr = rs // 2
    tr = _tile(hr, 512, 16)
    nr = hr // tr

    def body(c_ref, g_ref, s_ref, o_ref):
        o_ref[...] = (g_ref[...].astype(F32) + s_ref[...].astype(F32)).astype(BF16)

    if kind == "row":
        g_spec = pl.BlockSpec((tr, cs), lambda q, i, c_ref: (q * (rs // tr) + c_ref[0] * nr + i, 0))
    else:
        g_spec = pl.BlockSpec((tr, cs), lambda q, i, c_ref: (c_ref[0] * nr + i, q))
    blk = pl.BlockSpec((None, tr, cs), lambda q, i, c_ref: (q, i, 0))
    return pl.pallas_call(
        body,
        name=name,
        grid_spec=pltpu.PrefetchScalarGridSpec(num_scalar_prefetch=1, grid=(N_CHIPS, nr), in_specs=[g_spec, blk],
                                               out_specs=blk),
        out_shape=jax.ShapeDtypeStruct((N_CHIPS, hr, cs), BF16),
        compiler_params=_params(("parallel", "parallel")),
    )(c_arr, grad, got)


def _scatter_carry(sums, shapes):
    nw = len(sums)

    def copies(ps, got, send_sems, recv_sems):
        x, y, c, chips = _position()
        out = []
        for w in range(nw):
            for j, (cx, cy) in enumerate(chips):
                out.append(_remote(ps[w].at[2 * cx + cy], got[w].at[j], send_sems, recv_sems, 3 * w + j, (cx, cy, c)))
        return out

    outs = [jax.ShapeDtypeStruct((3, s[0] // 2, s[1]), BF16) for s in shapes]
    return _Carry(sums, outs, {}, 3 * nw, copies)


def _owner_sum(name, sums, got, shard_shape, qc_arr):
    rs, cs = shard_shape
    hr = rs // 2
    tr = _tile(hr, 512, 16)
    nr = hr // tr

    def body(qc_ref, mine_ref, got_ref, o_ref):
        acc = mine_ref[...].astype(F32)
        for j in range(3):
            acc = acc + got_ref[j].astype(F32)
        o_ref[...] = acc

    return pl.pallas_call(
        body,
        name=name,
        grid_spec=pltpu.PrefetchScalarGridSpec(
            num_scalar_prefetch=1, grid=(nr,),
            in_specs=[pl.BlockSpec((None, tr, cs), lambda i, qc_ref: (qc_ref[0], i, 0)),
                      pl.BlockSpec((3, tr, cs), lambda i, qc_ref: (0, i, 0))],
            out_specs=pl.BlockSpec((tr, cs), lambda i, qc_ref: (qc_ref[1] * nr + i, 0))),
        out_shape=jax.ShapeDtypeStruct((rs, cs), F32),
        compiler_params=_params(("parallel",)),
    )(qc_arr, sums, got)


def _share_carry(grads, shapes):
    nw = len(grads)

    def copies(ops, out, send_sems, recv_sems):
        x, y, c, _ = _position()
        res = []
        for w in range(nw):
            hr = shapes[w][0] // 2
            mine = out[w].at[pl.ds(c * hr, hr), :]
            res.append(_remote(mine, mine, send_sems, recv_sems, w, (x, y, 1 - c)))
        return res

    return _Carry(grads, _same(grads), {i: i for i in range(nw)}, nw, copies)


def _place_block(packed, me):
    return lax.dynamic_update_slice(jnp.zeros((N_DEV,) + packed.shape, F32), packed[None], (me, 0, 0))


def _exchange_carry(blocks):
    def copies(ops, res, send_sems, recv_sems):
        x, y, c, _ = _position()
        mine = res[0].at[4 * x + 2 * y + c]
        out = []
        for k in range(1, N_DEV):
            to = ((1 - x) if k & 4 else x, (1 - y) if k & 2 else y, (1 - c) if k & 1 else c)
            out.append(_remote(mine, mine, send_sems, recv_sems, k - 1, to))
        return out

    return _Carry([blocks], _same([blocks]), {0: 0}, N_DEV - 1, copies)


def _sum_blocks(name, blocks):
    n, rows, lanes = blocks.shape
    tr = _tile(rows, 512, 8)

    def body(b_ref, o_ref):
        acc = b_ref[0]
        for k in range(1, n):
            acc = acc + b_ref[k]
        o_ref[...] = acc

    return pl.pallas_call(
        body,
        name=name,
        grid=(rows // tr,),
        in_specs=[pl.BlockSpec((n, tr, lanes), lambda i: (0, i, 0))],
        out_specs=pl.BlockSpec((tr, lanes), lambda i: (i, 0)),
        out_shape=jax.ShapeDtypeStruct((rows, lanes), F32),
        compiler_params=_params(("parallel",)),
    )(blocks)


def _adamw(name, w, g, m, v):
    def fn(rows, pars):
        wv, gv, mv, vv = rows
        m_new = ADAM_B1 * mv + (1.0 - ADAM_B1) * gv
        v_new = ADAM_B2 * vv + (1.0 - ADAM_B2) * (gv * gv)
        m_hat = m_new / (1.0 - ADAM_B1 ** ADAM_STEP)
        v_hat = v_new / (1.0 - ADAM_B2 ** ADAM_STEP)
        delta = -ADAM_LR * (m_hat / (jnp.sqrt(v_hat) + ADAM_EPS) + ADAM_WD * wv)
        return [delta, m_new, v_new], []

    c = w.shape[1]
    return _rowwise(name, fn, [w, g, m, v], [], [(c, F32)] * 3, tr=256)


def _pack(arrays):
    rows = []
    for a in arrays:
        flat = a.reshape(-1).astype(F32)
        pad = (-flat.shape[0]) % LANES
        rows.append(jnp.pad(flat, (0, pad)).reshape(-1, LANES))
    stacked = jnp.concatenate(rows, axis=0)
    pad_rows = (-stacked.shape[0]) % 8
    return jnp.pad(stacked, ((0, pad_rows), (0, 0)))


def _unpack(packed, shapes):
    out, r = [], 0
    for s in shapes:
        n = math.prod(s)
        nr = -(-n // LANES)
        out.append(packed[r:r + nr].reshape(-1)[:n].reshape(s))
        r += nr
    return out


BIG = ["w1_gate", "w1_up", "w1_down", "w_in", "ssm_w_glu", "w_out", "w2_gate", "w2_up", "w2_down", "w_ple_gate",
       "w_ple_proj"]
KIND = {"w1_gate": "col", "w1_up": "col", "w1_down": "row", "w_in": "col", "ssm_w_glu": "row", "w_out": "row",
        "w2_gate": "col", "w2_up": "col", "w2_down": "row", "w_ple_gate": "row", "w_ple_proj": "col"}
SMALL = ["norm_ffn1", "norm_mix", "ssm_log_dt", "ssm_a_re", "ssm_a_im", "ssm_b_re", "ssm_b_im", "ssm_c_re", "ssm_c_im",
         "ssm_d", "gmlp_norm_v", "gmlp_w_s", "gmlp_b_s", "norm_ssm_out", "norm_gmlp_out", "norm_ffn2", "norm_ple",
         "norm_final"]
WEIGHTS = ["norm_ffn1", "w1_gate", "w1_up", "w1_down", "norm_mix", "w_in", "ssm_log_dt", "ssm_a_re", "ssm_a_im",
           "ssm_b_re", "ssm_b_im", "ssm_c_re", "ssm_c_im", "ssm_d", "ssm_w_glu", "gmlp_norm_v", "gmlp_w_s", "gmlp_b_s",
           "norm_ssm_out", "norm_gmlp_out", "w_out", "norm_ffn2", "w2_gate", "w2_up", "w2_down", "norm_ple",
           "w_ple_gate", "w_ple_proj", "norm_final"]


class _Trip:
    def __init__(self, names, sums, carry):
        self.names, self.sums, self.carry = names, sums, carry


class _Reducer:
    def __init__(self, shard_shape, c_arr, qc_arr):
        self.shard_shape, self.c_arr, self.qc_arr = shard_shape, c_arr, qc_arr
        self.halves = {}

    def begin(self, names, grads):
        kinds = [KIND[n] for n in names]
        shapes = [self.shard_shape[n] for n in names]
        swapped = _comm_call("pairs_" + names[0], _pairs_carry(grads, kinds, shapes))
        sums = [_pair_sum("pair_sum_" + n, g, s, k, sh, self.c_arr)
                for n, g, s, k, sh in zip(names, grads, swapped, kinds, shapes)]
        return _Trip(names, sums, _scatter_carry(sums, shapes))

    def end(self, trip, got):
        for n, ps, g in zip(trip.names, trip.sums, got):
            self.halves[n] = _owner_sum("owner_sum_" + n, ps, g, self.shard_shape[n], self.qc_arr)


def _step(x, p, tgt, w, m, v):
    d_model = x.shape[1]
    d_ssm = w["ssm_d"].shape[1]
    n_groups = d_ssm // SSM_GROUP
    row = lambda a: a.reshape(1, -1)

    xi, yi, ci = lax.axis_index("x"), lax.axis_index("y"), lax.axis_index("c")
    c_arr = jnp.reshape(ci, (1,)).astype(jnp.int32)
    q_arr = jnp.reshape(2 * xi + yi, (1,)).astype(jnp.int32)
    qc_arr = jnp.stack([2 * xi + yi, ci]).astype(jnp.int32)
    shard_shape = {n: w[n].shape for n in BIG}
    full = {n: _cast_into_gathered("cast_" + n, w[n], KIND[n], q_arr) for n in BIG}

    def gather(stage, names):
        return stage([full[n] for n in names], [KIND[n] for n in names], [shard_shape[n] for n in names])

    def gathered(names, arrays):
        full.update(zip(names, arrays))

    zeroth, first, second, third = ["w1_gate"], ["w1_up"], ["w1_down", "w_in"], ["w2_gate"]
    fourth, fifth, sixth = ["ssm_w_glu", "w_out"], ["w2_up"], ["w2_down", "w_ple_gate", "w_ple_proj"]

    def two_stages(passed_on, landing):
        return _join(gather(_gather_d2d_carry, passed_on), gather(_gather_ici_carry, landing)), passed_on + landing

    gathered(zeroth, _comm_call("gather_zeroth_ici", gather(_gather_ici_carry, zeroth)))
    gathered(zeroth, _comm_call("gather_zeroth_d2d", gather(_gather_d2d_carry, zeroth)))
    n1 = _rms_fwd("ffn1_norm", x, w["norm_ffn1"])
    gate1, landed = _mm_nn("ffn1_gate", n1, full["w1_gate"], BF16, tm=1024, tn=512, tk=2048,
                           carry=gather(_gather_ici_carry, first))
    gathered(first, landed)
    gathered(first, _comm_call("gather_first_d2d", gather(_gather_d2d_carry, first)))
    (a1, up1), landed = _ffn_up("ffn1", n1, full["w1_up"], gate1, gather(_gather_ici_carry, second))
    gathered(second, landed)
    gathered(second, _comm_call("gather_second_d2d", gather(_gather_d2d_carry, second)))
    h1, landed = _ffn_down("ffn1", a1, full["w1_down"], x, gather(_gather_ici_carry, third))
    gathered(third, landed)
    ffn1 = (n1, a1, gate1, up1)
    nm = _rms_fwd("mix_norm", h1, w["norm_mix"])
    carry, names = two_stages(third, fourth)
    z, landed = _mm_nn("in_proj", nm, full["w_in"], F32, tm=1024, tn=512, tk=2048, carry=carry)
    gathered(names, landed)

    disc, disc_vjp = jax.vjp(_ssm_discretize, w["ssm_log_dt"][0], w["ssm_a_re"], w["ssm_a_im"], w["ssm_b_re"],
                             w["ssm_b_im"])
    abar_r, abar_i, bbar_r, bbar_i = disc
    nb = n_groups // GROUPS_PER_BLOCK
    wb = jnp.concatenate([_blockdiag_in(bbar_r), _blockdiag_in(bbar_i)], axis=-1).astype(BF16)
    wc = jnp.concatenate([_blockdiag_out(w["ssm_c_re"]), -_blockdiag_out(w["ssm_c_im"])], axis=1).astype(BF16)
    abar = jnp.concatenate([abar_r.reshape(nb, -1), abar_i.reshape(nb, -1)], axis=-1)
    abar_conj = jnp.concatenate([abar_r.reshape(nb, -1), -abar_i.reshape(nb, -1)], axis=-1)
    carry, names = two_stages(fourth, fifth)
    (states, y_pre, yg), landed = _ssm_scan_fwd(z, wb, wc, abar, w["ssm_d"], carry)
    gathered(names, landed)
    q = _mm_nn("glu_proj", yg, full["ssm_w_glu"], F32, tm=1024, tn=1024, tk=1024)

    def glu_norm(rows, pars):
        yv = _gelu(rows[0]) * _sigmoid(rows[1])
        yh, _ = _rms_stats(yv)
        return [yh * pars[0]], []

    yn_ssm = _rowwise("ssm_glu_norm", glu_norm, [y_pre, q], [w["norm_ssm_out"]], [(d_ssm, BF16)])[0]

    tril = jnp.tril(jnp.ones((CHUNK, CHUNK), dtype=bool))
    wm = jnp.where(tril[None], w["gmlp_w_s"], 0.0).astype(BF16)
    bias = jnp.repeat(w["gmlp_b_s"].T, GMLP_HEAD, axis=1)
    yn_gmlp = _gmlp_fwd(z, w["gmlp_norm_v"], wm, bias, w["norm_gmlp_out"])
    ycat = jnp.concatenate([yn_ssm, yn_gmlp], axis=1)
    h2, landed = _mm_nn("out_proj", ycat, full["w_out"], F32, res=h1, alpha=1.0, tm=512, tn=1024, tk=2048,
                        carry=gather(_gather_d2d_carry, fifth))
    gathered(fifth, landed)

    n2 = _rms_fwd("ffn2_norm", h2, w["norm_ffn2"])
    (a2, gate2, up2), landed = _ffn_gateup("ffn2", n2, full["w2_gate"], full["w2_up"],
                                           gather(_gather_ici_carry, sixth))
    gathered(sixth, landed)
    gathered(sixth, _comm_call("gather_sixth_d2d", gather(_gather_d2d_carry, sixth)))
    h3 = _ffn_down("ffn2", a2, full["w2_down"], h2)
    ffn2 = (n2, a2, gate2, up2)
    npl = _rms_fwd("ple_norm", h3, w["norm_ple"])
    gq = _mm_nn("ple_gate", npl, full["w_ple_gate"], F32, tm=1024, tn=1024, tk=2048)
    pp = _mm_nn("ple_proj", p, full["w_ple_proj"], F32, tm=1024, tn=1024, tk=2048)

    def ple_combine(rows, pars):
        return [rows[0] + _sigmoid(rows[1]) * rows[2]], []

    h4 = _rowwise("ple_combine", ple_combine, [h3, gq, pp], [], [(d_model, F32)], tr=256)[0]

    def head(rows, pars):
        hv, tv = rows
        xh, _ = _rms_stats(hv)
        err = xh * pars[0] - tv
        dx, dg = _rms_backward(hv, pars[0], err * (1.0 / d_model))
        part = 0.5 * jnp.sum(err * err) * (1.0 / d_model)
        return [dx], [dg, jnp.full((1, LANES), part, F32)]

    dh4, g_norm_final, loss_part = _rowwise("loss_head", head, [h4, tgt], [row(w["norm_final"])], [(d_model, F32)],
                                            [(1, d_model), (1, LANES)], tr=256)

    def ple_bwd(rows, pars):
        dh, gqv, ppv = rows
        gate = _sigmoid(gqv)
        return [dh * ppv * gate * (1.0 - gate), dh * gate], []

    dgq, dpp = _rowwise("ple_dgate", ple_bwd, [dh4, gq, pp], [], [(d_model, BF16)] * 2, tr=256)
    reducer = _Reducer(shard_shape, c_arr, qc_arr)
    (g_w_ple_proj,) = _mm_tn("ple_dwproj", p, [dpp], BF16, tm=256, tn=1024, tk=1024)
    (g_w_ple_gate,) = _mm_tn("ple_dwgate", npl, [dgq], BF16, tm=1024, tn=1024, tk=1024)
    trip = reducer.begin(["w_ple_gate", "w_ple_proj"], [g_w_ple_gate, g_w_ple_proj])
    dnpl, got = _mm_nt_sum("ple_dnorm_in", [dgq], [full["w_ple_gate"]], F32, tm=512, tn=1024, tk=2048, carry=trip.carry)
    reducer.end(trip, got)
    dh3, dh3_b, g_norm_ple = _rms_bwd("ple_dnorm", h3, w["norm_ple"], dnpl, dh4, 0.5)

    dh2, dh2_b, g_norm_ffn2, _ = _ffn_bwd("ffn2", ["w2_gate", "w2_up", "w2_down"], h2, w["norm_ffn2"],
                                          full["w2_gate"], full["w2_up"], full["w2_down"], ffn2, dh3, dh3_b, 1.0,
                                          reducer)

    dycat = _mm_nt_sum("out_dproj", [dh2_b], [full["w_out"]], F32, tm=512, tn=1024, tk=2048)
    (g_w_out,) = _mm_tn("out_dw", ycat, [dh2_b], BF16, tm=1024, tn=1024, tk=1024)

    dzu, dzv, g_norm_gmlp_out, g_gmlp_norm_v, g_wm, g_s = _gmlp_bwd(z, dycat, w["gmlp_norm_v"], wm, bias,
                                                                  w["norm_gmlp_out"])
    g_gmlp_w_s = jnp.where(tril[None], g_wm, 0.0)
    g_gmlp_b_s = g_s.reshape(CHUNK, -1, GMLP_HEAD).sum(axis=-1).T

    def glu_bwd(rows, pars):
        dyn, ypre, qv = rows
        ygv = _gelu(ypre)
        sg = _sigmoid(qv)
        dy, dg = _rms_backward(ygv * sg, pars[0], dyn)
        return [dy * ygv * sg * (1.0 - sg), dy * sg], [dg]

    dq, dyg_part, g_norm_ssm_out = _rowwise("ssm_dglu", glu_bwd, [(dycat, d_ssm, 0), y_pre, q], [w["norm_ssm_out"]],
                                            [(d_ssm, BF16), (d_ssm, F32)], [(1, d_ssm)])
    dyg_proj = _mm_nt_sum("glu_dproj", [dq], [full["ssm_w_glu"]], F32, tm=1024, tn=1024, tk=1024)
    (g_ssm_w_glu,) = _mm_tn("glu_dw", yg, [dq], BF16, tm=1024, tn=1024, tk=1024)

    def gelu_bwd(rows, pars):
        return [(rows[0] + rows[1]) * _gelu_grad(rows[2])], []

    dy_pre = _rowwise("ssm_dgelu", gelu_bwd, [dyg_part, dyg_proj, y_pre], [], [(d_ssm, F32)])[0]
    trip = reducer.begin(["w_out", "ssm_w_glu"], [g_w_out, g_ssm_w_glu])
    (dz_ssm, g_wb, g_wc, g_abar, g_ssm_d), got = _ssm_scan_bwd(z, dy_pre, states, wb, wc, abar_conj, w["ssm_d"],
                                                              trip.carry)
    reducer.end(trip, got)
    g_abar = jnp.transpose(g_abar, (1, 0, 2)).reshape(nb, -1)
    sw = g_abar.shape[-1] // 2
    g_bbar_r = _blockdiag_in_grad(g_wb[..., :sw], SSM_STATE, SSM_GROUP)
    g_bbar_i = _blockdiag_in_grad(g_wb[..., sw:], SSM_STATE, SSM_GROUP)
    g_ssm_c_re = _blockdiag_out_grad(g_wc[:, :sw, :], SSM_GROUP, SSM_STATE)
    g_ssm_c_im = -_blockdiag_out_grad(g_wc[:, sw:, :], SSM_GROUP, SSM_STATE)
    g_abar_r = g_abar[..., :sw].reshape(n_groups, SSM_STATE)
    g_abar_i = g_abar[..., sw:].reshape(n_groups, SSM_STATE)
    g_ssm_log_dt, g_ssm_a_re, g_ssm_a_im, g_ssm_b_re, g_ssm_b_im = disc_vjp((g_abar_r, g_abar_i, g_bbar_r, g_bbar_i))

    dz = jnp.concatenate([dz_ssm, dzu, dzv], axis=1)
    (g_w_in,) = _mm_tn("in_dw", nm, [dz], BF16, tm=1024, tn=768, tk=1024)
    trip = reducer.begin(["w_in"], [g_w_in])
    dnm, got = _mm_nt_sum("in_dproj", [dz], [full["w_in"]], F32, tm=512, tn=1024, tk=1024, carry=trip.carry)
    reducer.end(trip, got)
    dh1, dh1_b, g_norm_mix = _rms_bwd("mix_dnorm", h1, w["norm_mix"], dnm, dh2, 0.5)
    small = {"norm_mix": g_norm_mix, "ssm_log_dt": g_ssm_log_dt, "ssm_a_re": g_ssm_a_re,
             "ssm_a_im": g_ssm_a_im, "ssm_b_re": g_ssm_b_re, "ssm_b_im": g_ssm_b_im, "ssm_c_re": g_ssm_c_re,
             "ssm_c_im": g_ssm_c_im, "ssm_d": g_ssm_d, "gmlp_norm_v": g_gmlp_norm_v, "gmlp_w_s": g_gmlp_w_s,
             "gmlp_b_s": g_gmlp_b_s, "norm_ssm_out": g_norm_ssm_out, "norm_gmlp_out": g_norm_gmlp_out,
             "norm_ffn2": g_norm_ffn2, "norm_ple": g_norm_ple, "norm_final": g_norm_final}
    early = [n for n in SMALL if n != "norm_ffn1"]
    me = 4 * xi + 2 * yi + ci
    early_blocks = _place_block(_pack([small[n] for n in early] + [loss_part[:, :1]]), me)
    dx, _, g_norm_ffn1, (early_blocks,) = _ffn_bwd(
        "ffn1", ["w1_gate", "w1_up", "w1_down"], x, w["norm_ffn1"], full["w1_gate"], full["w1_up"], full["w1_down"],
        ffn1, dh1, dh1_b, 1.0, reducer, riding=_exchange_carry(early_blocks))
    early_grads = _unpack(_sum_blocks("sum_small", early_blocks), [w[n].shape for n in early] + [(1,)])
    loss = early_grads[-1].reshape(())

    halves = [reducer.halves[n] for n in BIG]
    late_blocks = _place_block(_pack([g_norm_ffn1]), me)
    last = _join(_share_carry(halves, [shard_shape[n] for n in BIG]), _exchange_carry(late_blocks))
    *shared, late_blocks = _comm_call("share_halves", last)
    grad = dict(zip(BIG, shared))
    grad.update(zip(early, early_grads[:-1]))
    grad["norm_ffn1"] = _unpack(_sum_blocks("sum_first_norm", late_blocks), [w["norm_ffn1"].shape])[0]

    small_shapes = [w[n].shape for n in SMALL]
    delta, new_m, new_v = {}, {}, {}
    for n in BIG:
        delta[n], new_m[n], new_v[n] = _adamw("adamw_" + n, w[n], grad[n], m[n], v[n])
    d_p, m_p, v_p = _adamw("adamw_small", _pack([w[n] for n in SMALL]), _pack([grad[n] for n in SMALL]),
                           _pack([m[n] for n in SMALL]), _pack([v[n] for n in SMALL]))
    for name_list, packed in ((delta, d_p), (new_m, m_p), (new_v, v_p)):
        for n, a in zip(SMALL, _unpack(packed, small_shapes)):
            name_list[n] = a
    return loss, dx, grad, delta, new_m, new_v


def kernel(x, p, norm_ffn1, w1_gate, w1_up, w1_down, norm_mix, w_in, ssm_log_dt, ssm_a_re, ssm_a_im, ssm_b_re, ssm_b_im, ssm_c_re, ssm_c_im, ssm_d, ssm_w_glu, gmlp_norm_v, gmlp_w_s, gmlp_b_s, norm_ssm_out, norm_gmlp_out, w_out, norm_ffn2, w2_gate, w2_up, w2_down, norm_ple, w_ple_gate, w_ple_proj, norm_final, loss_target, m_norm_ffn1, m_w1_gate, m_w1_up, m_w1_down, m_norm_mix, m_w_in, m_ssm_log_dt, m_ssm_a_re, m_ssm_a_im, m_ssm_b_re, m_ssm_b_im, m_ssm_c_re, m_ssm_c_im, m_ssm_d, m_ssm_w_glu, m_gmlp_norm_v, m_gmlp_w_s, m_gmlp_b_s, m_norm_ssm_out, m_norm_gmlp_out, m_w_out, m_norm_ffn2, m_w2_gate, m_w2_up, m_w2_down, m_norm_ple, m_w_ple_gate, m_w_ple_proj, m_norm_final, v_norm_ffn1, v_w1_gate, v_w1_up, v_w1_down, v_norm_mix, v_w_in, v_ssm_log_dt, v_ssm_a_re, v_ssm_a_im, v_ssm_b_re, v_ssm_b_im, v_ssm_c_re, v_ssm_c_im, v_ssm_d, v_ssm_w_glu, v_gmlp_norm_v, v_gmlp_w_s, v_gmlp_b_s, v_norm_ssm_out, v_norm_gmlp_out, v_w_out, v_norm_ffn2, v_w2_gate, v_w2_up, v_w2_down, v_norm_ple, v_w_ple_gate, v_w_ple_proj, v_norm_final):
    given = dict(locals())
    shapes = {n: given[n].shape for n in WEIGHTS}

    def block(name):
        a = given[name]
        if a.ndim == 1:
            return a.reshape(1, -1)
        return a[0] if a.ndim >= 3 else a

    w = {n: block(n) for n in WEIGHTS}
    m = {n: block("m_" + n) for n in WEIGHTS}
    v = {n: block("v_" + n) for n in WEIGHTS}
    loss, dx, grad, delta, new_m, new_v = _step(x[0], p[0, 0], loss_target[0], w, m, v)
    outs = [loss, dx[None]]
    for tree in (grad, delta, new_m, new_v):
        outs += [tree[n].reshape(shapes[n]) for n in WEIGHTS]
    return tuple(outs)
```

```python
import functools
import math

import jax
import jax.numpy as jnp
from jax import lax
from jax.experimental import pallas as pl
from jax.experimental.pallas import tpu as pltpu

F32 = jnp.float32
BF16 = jnp.bfloat16
EPS = 1e-6
SSM_GROUP = 16
SSM_STATE = 64
GROUPS_PER_BLOCK = 8
GMLP_HEAD = 128
CHUNK = 128
ADAM_LR = 0.001
ADAM_B1 = 0.9
ADAM_B2 = 0.999
ADAM_EPS = 1e-08
ADAM_WD = 0.01
ADAM_STEP = 10
N_CHIPS = 4
N_DEV = 8
LANES = 128
VMEM_LIMIT_BYTES = 56 * 1024 * 1024
MESH = pl.DeviceIdType.MESH
GELU_C = math.sqrt(2.0 / math.pi)
GELU_A = 0.044715

_DOT_DIMS = {
    "nn": (((1,), (0,)), ((), ())),
    "nt": (((1,), (1,)), ((), ())),
    "tn": (((0,), (0,)), ((), ())),
}


def _tile(dim, pref, align):
    if dim <= pref:
        return dim
    t = (pref // align) * align
    while t >= align:
        if dim % t == 0:
            return t
        t -= align
    return dim


def _params(semantics):
    return pltpu.CompilerParams(dimension_semantics=semantics, vmem_limit_bytes=VMEM_LIMIT_BYTES)


def _gelu(x):
    return 0.5 * x * (1.0 + jnp.tanh(GELU_C * (x + GELU_A * x * x * x)))


def _gelu_grad(x):
    t = jnp.tanh(GELU_C * (x + GELU_A * x * x * x))
    return 0.5 * (1.0 + t) + 0.5 * x * (1.0 - t * t) * GELU_C * (1.0 + 3.0 * GELU_A * x * x)


def _sigmoid(x):
    return 1.0 / (1.0 + jnp.exp(-x))


def _dot(a, b, mode):
    return lax.dot_general(a.astype(BF16), b.astype(BF16), _DOT_DIMS[mode], preferred_element_type=F32)


class _Carry:
    def __init__(self, arrays, out_shapes, aliases, n_copies, copies):
        self.arrays = list(arrays)
        self.out_shapes = list(out_shapes)
        self.aliases = dict(aliases)
        self.n_copies = n_copies
        self.copies = copies

    def scratch(self):
        return [pltpu.SemaphoreType.DMA((self.n_copies,)), pltpu.SemaphoreType.DMA((self.n_copies,))]

    def split(self, refs):
        n_in, n_out = len(self.arrays), len(self.out_shapes)
        return refs[:n_in], refs[n_in:n_in + n_out], refs[n_in + n_out], refs[n_in + n_out + 1]

    def start(self, refs):
        for cp in self.copies(*self.split(refs)):
            cp.start()

    def wait(self, refs):
        for cp in self.copies(*self.split(refs)):
            cp.wait()


class _SemRange:
    def __init__(self, sems, offset):
        self.sems, self.offset = sems, offset

    @property
    def at(self):
        return self

    def __getitem__(self, k):
        return self.sems.at[self.offset + k]


def _join(first, second):
    n_in, n_out = len(first.arrays), len(first.out_shapes)
    aliases = dict(first.aliases)
    aliases.update({n_in + i: n_out + o for i, o in second.aliases.items()})

    def copies(ops, res, send_sems, recv_sems):
        return (first.copies(ops[:n_in], res[:n_out], send_sems, recv_sems)
                + second.copies(ops[n_in:], res[n_out:], _SemRange(send_sems, first.n_copies),
                                _SemRange(recv_sems, first.n_copies)))

    return _Carry(first.arrays + second.arrays, first.out_shapes + second.out_shapes, aliases,
                  first.n_copies + second.n_copies, copies)


_ANY = pl.BlockSpec(memory_space=pl.ANY)


def _comm_call(name, carry):
    def body(*refs):
        carry.start(refs)
        carry.wait(refs)

    n_in = len(carry.arrays)
    return pl.pallas_call(
        body,
        name=name,
        in_specs=[_ANY] * n_in,
        out_specs=[_ANY] * len(carry.out_shapes),
        out_shape=carry.out_shapes,
        input_output_aliases=carry.aliases,
        scratch_shapes=carry.scratch(),
    )(*carry.arrays)


def _carried_call(body, carry, *, name, grid, in_specs, out_specs, out_shape, scratch_shapes, semantics, args):
    if carry is None:
        res = pl.pallas_call(body, name=name, grid=grid, in_specs=in_specs, out_specs=out_specs, out_shape=out_shape,
                             scratch_shapes=scratch_shapes, compiler_params=_params(semantics))(*args)
        return res, []
    n_in, n_out, n_scr = len(in_specs), len(out_specs), len(scratch_shapes)
    nci, nco = len(carry.arrays), len(carry.out_shapes)

    def wrapped(*refs):
        ins = refs[:n_in]
        outs = refs[n_in + nci:n_in + nci + n_out]
        scr = refs[n_in + nci + n_out + nco:n_in + nci + n_out + nco + n_scr]
        c_refs = (refs[n_in:n_in + nci] + refs[n_in + nci + n_out:n_in + nci + n_out + nco]
                  + refs[n_in + nci + n_out + nco + n_scr:])
        first = functools.reduce(jnp.logical_and, [pl.program_id(d) == 0 for d in range(len(grid))])
        last = functools.reduce(jnp.logical_and, [pl.program_id(d) == grid[d] - 1 for d in range(len(grid))])

        @pl.when(first)
        def _():
            carry.start(c_refs)

        body(*ins, *outs, *scr)

        @pl.when(last)
        def _():
            carry.wait(c_refs)

    res = pl.pallas_call(
        wrapped,
        name=name,
        grid=grid,
        in_specs=list(in_specs) + [_ANY] * nci,
        out_specs=list(out_specs) + [_ANY] * nco,
        out_shape=list(out_shape) + carry.out_shapes,
        input_output_aliases={n_in + i: n_out + o for i, o in carry.aliases.items()},
        scratch_shapes=list(scratch_shapes) + carry.scratch(),
        compiler_params=_params(("arbitrary",) * len(grid)),
    )(*args, *carry.arrays)
    return res[:n_out], res[n_out:]


def _matmul(name, mode, a_list, b_list, products, out_dtypes, epilogue, extras=(), tm=512, tn=512, tk=2048,
            carry=None, n_part=(0, 1)):
    a0, b0 = a_list[0], b_list[0]
    if mode == "tn":
        k_dim, m_dim = a0.shape
    else:
        m_dim, k_dim = a0.shape
    n_dim = (b0.shape[0] if mode == "nt" else b0.shape[1]) // n_part[1]
    tm = _tile(m_dim, tm, LANES)
    tn = _tile(n_dim, tn, LANES)
    tk = _tile(k_dim, tk, LANES)
    nk = k_dim // tk
    j0 = n_part[0] * (n_dim // tn)
    chunk = 2 * LANES if (nk == 1 and epilogue is not _identity and tn % (2 * LANES) == 0) else tn
    n_acc = 1 + max(p[2] for p in products)
    na, nb, ne, no = len(a_list), len(b_list), len(extras), len(out_dtypes)

    if mode == "tn":
        a_spec = pl.BlockSpec((tk, tm), lambda i, j, k: (k, i))
    else:
        a_spec = pl.BlockSpec((tm, tk), lambda i, j, k: (i, k))
    if mode == "nt":
        b_spec = pl.BlockSpec((tn, tk), lambda i, j, k: (j0 + j, k))
    else:
        b_spec = pl.BlockSpec((tk, tn), lambda i, j, k: (k, j0 + j))
    t_spec = pl.BlockSpec((tm, tn), lambda i, j, k: (i, j))

    def body(*refs):
        a_refs = refs[:na]
        b_refs = refs[na:na + nb]
        e_refs = refs[na + nb:na + nb + ne]
        o_refs = refs[na + nb + ne:na + nb + ne + no]
        acc_refs = refs[na + nb + ne + no:]

        def partial_sums(cols):
            sums = [None] * n_acc
            for ai, bi, ci in products:
                b = b_refs[bi][cols, :] if mode == "nt" else b_refs[bi][:, cols]
                d = _dot(a_refs[ai][...], b, mode)
                sums[ci] = d if sums[ci] is None else sums[ci] + d
            return sums

        def finish(accs, cols):
            outs = epilogue(accs, [e[:, cols] for e in e_refs])
            for o_ref, o in zip(o_refs, outs):
                o_ref[:, cols] = o.astype(o_ref.dtype)

        if nk == 1:
            for c0 in range(0, tn, chunk):
                finish(partial_sums(slice(c0, c0 + chunk)), slice(c0, c0 + chunk))
        else:
            sums = partial_sums(slice(None))
            finish = functools.partial(finish, cols=slice(None))
            k = pl.program_id(2)

            @pl.when(k == 0)
            def _():
                for acc, s in zip(acc_refs, sums):
                    acc[...] = s

            @pl.when(k > 0)
            def _():
                for acc, s in zip(acc_refs, sums):
                    acc[...] += s

            @pl.when(k == nk - 1)
            def _():
                finish([acc[...] for acc in acc_refs])

    scratch = [pltpu.VMEM((tm, tn), F32) for _ in range(n_acc)] if nk > 1 else []
    outs, carried = _carried_call(
        body, carry,
        name=name,
        grid=(m_dim // tm, n_dim // tn, nk),
        in_specs=[a_spec] * na + [b_spec] * nb + [t_spec] * ne,
        out_specs=[t_spec] * no,
        out_shape=[jax.ShapeDtypeStruct((m_dim, n_dim), dt) for dt in out_dtypes],
        scratch_shapes=scratch,
        semantics=("parallel", "parallel", "arbitrary"),
        args=[*a_list, *b_list, *extras],
    )
    return (outs, carried) if carry else outs


def _identity(accs, extras):
    return accs


def _single(result, carry):
    return (result[0][0], result[1]) if carry else result[0]


def _mm_nn(name, a, b, out_dtype, res=None, alpha=1.0, carry=None, **tiles):
    if res is None:
        return _single(_matmul(name, "nn", [a], [b], [(0, 0, 0)], [out_dtype], _identity, carry=carry, **tiles), carry)

    def epilogue(accs, extras):
        return [extras[0] + alpha * accs[0]]

    return _single(_matmul(name, "nn", [a], [b], [(0, 0, 0)], [out_dtype], epilogue, extras=(res,), carry=carry,
                           **tiles), carry)


def _mm_nt_sum(name, a_list, b_list, out_dtype, carry=None, n_part=(0, 1), **tiles):
    products = [(i, i, 0) for i in range(len(a_list))]
    return _single(_matmul(name, "nt", a_list, b_list, products, [out_dtype], _identity, carry=carry, n_part=n_part,
                           **tiles), carry)


def _mm_tn(name, a, b_list, out_dtype, carry=None, **tiles):
    products = [(0, i, i) for i in range(len(b_list))]
    return _matmul(name, "tn", [a], b_list, products, [out_dtype] * len(b_list), _identity, carry=carry, **tiles)


def _rowwise(name, fn, row_ins, par_ins, row_outs, acc_outs=(), tr=512):
    first = row_ins[0][0] if isinstance(row_ins[0], tuple) else row_ins[0]
    t_dim = first.shape[0]
    tr = _tile(t_dim, tr, 16)
    arrays, specs = [], []
    for r in row_ins:
        if isinstance(r, tuple):
            arr, width, blk = r
            specs.append(pl.BlockSpec((tr, width), lambda i, blk=blk: (i, blk)))
        else:
            arr = r
            specs.append(pl.BlockSpec((tr, arr.shape[1]), lambda i: (i, 0)))
        arrays.append(arr)
    for p in par_ins:
        arrays.append(p)
        specs.append(pl.BlockSpec(p.shape, lambda i, nd=p.ndim: (0,) * nd))
    nr, npar, nro, nacc = len(row_ins), len(par_ins), len(row_outs), len(acc_outs)

    def body(*refs):
        rows = [r[...] for r in refs[:nr]]
        pars = [p[...] for p in refs[nr:nr + npar]]
        o_refs = refs[nr + npar:nr + npar + nro]
        acc_refs = refs[nr + npar + nro:]
        outs, accs = fn(rows, pars)
        for o_ref, o in zip(o_refs, outs):
            o_ref[...] = o.astype(o_ref.dtype)
        if nacc:
            @pl.when(pl.program_id(0) == 0)
            def _():
                for a_ref in acc_refs:
                    a_ref[...] = jnp.zeros_like(a_ref)

            for a_ref, a in zip(acc_refs, accs):
                a_ref[...] += a

    out_shape = [jax.ShapeDtypeStruct((t_dim, c), dt) for c, dt in row_outs]
    out_shape += [jax.ShapeDtypeStruct(s, F32) for s in acc_outs]
    out_specs = [pl.BlockSpec((tr, c), lambda i: (i, 0)) for c, _ in row_outs]
    out_specs += [pl.BlockSpec(s, lambda i: (0, 0)) for s in acc_outs]
    return pl.pallas_call(
        body,
        name=name,
        grid=(t_dim // tr,),
        in_specs=specs,
        out_specs=out_specs,
        out_shape=out_shape,
        compiler_params=_params(("arbitrary",)),
    )(*arrays)


def _rms_stats(x):
    r = lax.rsqrt(jnp.mean(x * x, axis=-1, keepdims=True) + EPS)
    return x * r, r


def _rms_backward(x, g, dy):
    xh, r = _rms_stats(x)
    a = dy * g
    dx = r * (a - xh * jnp.mean(a * xh, axis=-1, keepdims=True))
    return dx, jnp.sum(dy * xh, axis=0, keepdims=True)


def _rms_fwd(name, x, g):
    def fn(rows, pars):
        xh, _ = _rms_stats(rows[0])
        return [xh * pars[0]], []

    return _rowwise(name, fn, [x], [g], [(x.shape[1], BF16)])[0]


def _rms_bwd(name, x, g, dy_parts, dres, scale):
    def fn(rows, pars):
        dy = rows[2] if len(rows) == 3 else jnp.concatenate(rows[2:], axis=1)
        dx, dg = _rms_backward(rows[0], pars[0], dy)
        tot = rows[1] + dx
        return [tot, scale * tot], [dg]

    d = x.shape[1]
    return _rowwise(name, fn, [x, dres, *dy_parts], [g], [(d, F32), (d, BF16)], [(1, d)], tr=256)


def _cast_into_gathered(name, w, kind, q_arr):
    rs, cs = w.shape
    tr = _tile(rs, 256, 16)
    nr = rs // tr

    def body(q_ref, w_ref, o_ref):
        o_ref[...] = w_ref[...].astype(BF16)

    if kind == "row":
        o_spec = pl.BlockSpec((tr, cs), lambda i, q_ref: (q_ref[0] * nr + i, 0))
    else:
        o_spec = pl.BlockSpec((tr, cs), lambda i, q_ref: (i, q_ref[0]))
    return pl.pallas_call(
        body,
        name=name,
        grid_spec=pltpu.PrefetchScalarGridSpec(num_scalar_prefetch=1, grid=(nr,),
                                               in_specs=[pl.BlockSpec((tr, cs), lambda i, q_ref: (i, 0))],
                                               out_specs=o_spec),
        out_shape=jax.ShapeDtypeStruct(_full_shape(kind, (rs, cs)), BF16),
        compiler_params=_params(("parallel",)),
    )(q_arr, w)


def _ffn_gateup(tag, n, wg, wu, carry):
    def act(accs, extras):
        gate, up = accs
        return [gate * _sigmoid(gate) * up, gate, up]

    return _matmul(tag + "_gateup", "nn", [n], [wg, wu], [(0, 0, 0), (0, 1, 1)], [BF16] * 3, act,
                   tm=1024, tn=512, tk=2048, carry=carry)


def _ffn_up(tag, n, wu, gate, carry):
    def act(accs, extras):
        g = extras[0].astype(F32)
        return [g * _sigmoid(g) * accs[0], accs[0]]

    return _matmul(tag + "_up", "nn", [n], [wu], [(0, 0, 0)], [BF16] * 2, act, extras=(gate,),
                   tm=1024, tn=512, tk=2048, carry=carry)


def _ffn_down(tag, a, wd, h, carry=None):
    return _mm_nn(tag + "_down", a, wd, F32, res=h, alpha=0.5, tm=512, tn=512, tk=5632, carry=carry)


def _ffn_bwd(tag, names, h, g, wg, wu, wd, saved, dh, dfb, next_scale, reducer, riding=None, riding_dwd=None):
    n, a, gate, up = saved

    def act_bwd(accs, extras):
        da = accs[0]
        gt, u = extras[0].astype(F32), extras[1].astype(F32)
        s = _sigmoid(gt)
        return [da * u * (s * (1.0 + gt * (1.0 - s))), da * (gt * s)]

    dact = _matmul(tag + "_dact", "nt", [dfb], [wd], [(0, 0, 0)], [BF16, BF16], act_bwd, extras=(gate, up),
                   tm=1024, tn=512, tk=2048, carry=riding)
    (dgp, du), rode = dact if riding else (dact, [])
    dwd_call = _mm_tn(tag + "_dwd", a, [dfb], BF16, tm=512, tn=2048, tk=2048, carry=riding_dwd)
    (dwd,), rode_dwd = dwd_call if riding_dwd else (dwd_call, [])
    down = reducer.swap(names[2:], [dwd])
    (dwg,), swapped = _mm_tn(tag + "_dwg", n, [dgp], BF16, tm=1024, tn=512, tk=2048, carry=down.carry)
    down = reducer.send(down, swapped)
    gate_w = reducer.swap(names[:1], [dwg])
    carry, split = _ride(down.carry, gate_w.carry)
    (dwu,), results = _mm_tn(tag + "_dwu", n, [du], BF16, tm=1024, tn=512, tk=2048, carry=carry)
    got, swapped = split(results)
    reducer.end(down, got)
    gate_w = reducer.send(gate_w, swapped)
    up_w = reducer.swap(names[1:2], [dwu])
    carry, split = _ride(gate_w.carry, up_w.carry)
    dn_lo, results = _mm_nt_sum(tag + "_dn_lo", [dgp, du], [wg, wu], F32, tm=1024, tn=1024, tk=1408, carry=carry,
                                n_part=(0, 2))
    got, swapped = split(results)
    reducer.end(gate_w, got)
    up_w = reducer.send(up_w, swapped)
    dn_hi, got = _mm_nt_sum(tag + "_dn_hi", [dgp, du], [wg, wu], F32, tm=1024, tn=1024, tk=1408, carry=up_w.carry,
                            n_part=(1, 2))
    reducer.end(up_w, got)
    dh_in, dh_in_b, dg = _rms_bwd(tag + "_dnorm", h, g, [dn_lo, dn_hi], dh, next_scale)
    return dh_in, dh_in_b, dg, rode, rode_dwd


def _ssm_discretize(log_dt, a_re, a_im, b_re, b_im):
    dt = jnp.exp(log_dt)[:, None]
    lr = jnp.minimum(a_re, -1e-4)
    li = a_im
    mag = jnp.exp(lr * dt)
    ang = li * dt
    abar_r = mag * jnp.cos(ang)
    abar_i = mag * jnp.sin(ang)
    den = lr * lr + li * li
    xr = abar_r - 1.0
    xi = abar_i
    zr = (xr * lr + xi * li) / den
    zi = (xi * lr - xr * li) / den
    bbar_r = zr[..., None] * b_re - zi[..., None] * b_im
    bbar_i = zr[..., None] * b_im + zi[..., None] * b_re
    return abar_r, abar_i, bbar_r, bbar_i


def _blockdiag_in(b):
    g, n, p = b.shape
    nb = g // GROUPS_PER_BLOCK
    eye = jnp.eye(GROUPS_PER_BLOCK, dtype=b.dtype)
    b4 = b.reshape(nb, GROUPS_PER_BLOCK, n, p)
    return jnp.einsum("sgnp,gh->sgphn", b4, eye).reshape(nb, GROUPS_PER_BLOCK * p, GROUPS_PER_BLOCK * n)


def _blockdiag_in_grad(gw, n, p):
    nb = gw.shape[0]
    eye = jnp.eye(GROUPS_PER_BLOCK, dtype=gw.dtype)
    g5 = gw.reshape(nb, GROUPS_PER_BLOCK, p, GROUPS_PER_BLOCK, n)
    return jnp.einsum("sgphn,gh->sgnp", g5, eye).reshape(nb * GROUPS_PER_BLOCK, n, p)


def _blockdiag_out(c):
    g, p, n = c.shape
    nb = g // GROUPS_PER_BLOCK
    eye = jnp.eye(GROUPS_PER_BLOCK, dtype=c.dtype)
    c4 = c.reshape(nb, GROUPS_PER_BLOCK, p, n)
    return jnp.einsum("sgpn,gh->shngp", c4, eye).reshape(nb, GROUPS_PER_BLOCK * n, GROUPS_PER_BLOCK * p)


def _blockdiag_out_grad(gw, p, n):
    nb = gw.shape[0]
    eye = jnp.eye(GROUPS_PER_BLOCK, dtype=gw.dtype)
    g5 = gw.reshape(nb, GROUPS_PER_BLOCK, n, GROUPS_PER_BLOCK, p)
    return jnp.einsum("shngp,gh->sgpn", g5, eye).reshape(nb * GROUPS_PER_BLOCK, p, n)


def _ssm_scan_fwd(z, wb, wc, abar, d, carry=None):
    t_dim = z.shape[0]
    nb, cb, sw2 = wb.shape
    nl = sw2 // LANES
    hl = nl // 2
    tt = _tile(t_dim, 256, 8)
    nt = t_dim // tt

    def body(z_ref, wb_ref, wc_ref, a_ref, d_ref, s_ref, y_ref, yg_ref, drive_ref, st_ref):
        @pl.when(pl.program_id(0) == 0)
        def _():
            st_ref[...] = jnp.zeros_like(st_ref)

        u = z_ref[...]
        ub = u.astype(BF16)
        for b in range(nb):
            drive = _dot(ub[:, b * cb:(b + 1) * cb], wb_ref[b], "nn")
            for l in range(nl):
                drive_ref[l, pl.ds(b, tt, stride=nb), :] = drive[:, l * LANES:(l + 1) * LANES]
        a = a_ref[...]
        chunk = lambda v, l: v[:, l * LANES:(l + 1) * LANES]

        def step(t, state):
            rows = pl.ds(pl.multiple_of(t * nb, nb), nb)
            re, im = [], []
            for l in range(hl):
                ar, ai, sr, si = chunk(a, l), chunk(a, hl + l), state[l], state[hl + l]
                nr = ar * sr - ai * si + drive_ref[l, rows, :]
                ni = ar * si + ai * sr + drive_ref[hl + l, rows, :]
                s_ref[l, rows, :] = nr
                s_ref[hl + l, rows, :] = ni
                re.append(nr)
                im.append(ni)
            return tuple(re + im)

        state = lax.fori_loop(0, tt, step, tuple(st_ref[l] for l in range(nl)), unroll=8)
        for l in range(nl):
            st_ref[l] = state[l]
        parts = []
        for b in range(nb):
            s_b = jnp.concatenate([s_ref[l, pl.ds(b, tt, stride=nb), :] for l in range(nl)], axis=1)
            parts.append(_dot(s_b, wc_ref[b], "nn"))
        y = jnp.concatenate(parts, axis=1) + d_ref[...] * u
        y_ref[...] = y
        yg_ref[...] = _gelu(y).astype(BF16)

    full = lambda a: pl.BlockSpec(a.shape, lambda t, nd=a.ndim: (0,) * nd)
    return _carried_call(
        body, carry,
        name="ssm_scan_fwd",
        grid=(nt,),
        in_specs=[pl.BlockSpec((tt, nb * cb), lambda t: (t, 0)), full(wb), full(wc), full(abar), full(d)],
        out_specs=[
            pl.BlockSpec((nl, tt * nb, LANES), lambda t: (0, t, 0)),
            pl.BlockSpec((tt, nb * cb), lambda t: (t, 0)),
            pl.BlockSpec((tt, nb * cb), lambda t: (t, 0)),
        ],
        out_shape=[
            jax.ShapeDtypeStruct((nl, t_dim * nb, LANES), F32),
            jax.ShapeDtypeStruct((t_dim, nb * cb), F32),
            jax.ShapeDtypeStruct((t_dim, nb * cb), BF16),
        ],
        scratch_shapes=[pltpu.VMEM((nl, tt * nb, LANES), F32), pltpu.VMEM((nl, nb, LANES), F32)],
        semantics=("arbitrary",),
        args=[z, wb, wc, abar, d],
    )


def _ssm_scan_bwd(z, dy, states, wb, wc, abar_conj, d, carry=None):
    t_dim = z.shape[0]
    nb, cb, sw2 = wb.shape
    nl = sw2 // LANES
    hl = nl // 2
    tt = _tile(t_dim, 128, 8)
    nt = t_dim // tt
    edges = states.reshape(nl, nt, tt * nb, LANES)[:, :, (tt - 1) * nb:, :]
    before = jnp.concatenate([jnp.zeros((nl, 1, nb, LANES), F32), edges[:, :-1]], axis=1).reshape(nl, nt * nb, LANES)

    def body(z_ref, dy_ref, s_ref, sp_ref, wb_ref, wc_ref, a_ref, d_ref,
             dz_ref, gwb_ref, gwc_ref, ga_ref, gd_ref, gin_ref, gs_ref, st_ref):
        @pl.when(pl.program_id(0) == 0)
        def _():
            st_ref[...] = jnp.zeros_like(st_ref)
            gwb_ref[...] = jnp.zeros_like(gwb_ref)
            gwc_ref[...] = jnp.zeros_like(gwc_ref)
            ga_ref[...] = jnp.zeros_like(ga_ref)
            gd_ref[...] = jnp.zeros_like(gd_ref)

        u = z_ref[...]
        dyv = dy_ref[...]
        ub = u.astype(BF16)
        dyb = dyv.astype(BF16)
        for b in range(nb):
            gin = _dot(dyb[:, b * cb:(b + 1) * cb], wc_ref[b], "nt")
            for l in range(nl):
                gin_ref[l, pl.ds(b, tt, stride=nb), :] = gin[:, l * LANES:(l + 1) * LANES]
        a = a_ref[...]
        chunk = lambda v, l: v[:, l * LANES:(l + 1) * LANES]

        def step(k, state):
            rows = pl.ds(pl.multiple_of((tt - 1 - k) * nb, nb), nb)
            re, im = [], []
            for l in range(hl):
                ar, ai, gr, gi = chunk(a, l), chunk(a, hl + l), state[l], state[hl + l]
                nr = ar * gr - ai * gi + gin_ref[l, rows, :]
                ni = ar * gi + ai * gr + gin_ref[hl + l, rows, :]
                gs_ref[l, rows, :] = nr
                gs_ref[hl + l, rows, :] = ni
                re.append(nr)
                im.append(ni)
            return tuple(re + im)

        state = lax.fori_loop(0, tt, step, tuple(st_ref[l] for l in range(nl)), unroll=8)
        for l in range(nl):
            st_ref[l] = state[l]

        parts = []
        for b in range(nb):
            cols = slice(b * cb, (b + 1) * cb)
            gs_b = jnp.concatenate([gs_ref[l, pl.ds(b, tt, stride=nb), :] for l in range(nl)], axis=1)
            s_b = jnp.concatenate([s_ref[l, pl.ds(b, tt, stride=nb), :] for l in range(nl)], axis=1)
            parts.append(_dot(gs_b, wb_ref[b], "nt"))
            gwb_ref[b] += _dot(ub[:, cols], gs_b, "tn")
            gwc_ref[b] += _dot(s_b, dyb[:, cols], "tn")
        dz_ref[...] = (jnp.concatenate(parts, axis=1) + d_ref[...] * dyv).astype(BF16)
        gd_ref[...] += jnp.sum(dyv * u, axis=0, keepdims=True)

        row = lax.broadcasted_iota(jnp.int32, (tt * nb, LANES), 0)
        shifted = lambda v: jnp.where(row < nb, 0.0, pltpu.roll(v, nb, 0))
        over_time = lambda v: jnp.sum(v.reshape(tt, nb, LANES), axis=0)
        for l in range(hl):
            g_r, g_i = gs_ref[l], gs_ref[hl + l]
            p_r, p_i = shifted(s_ref[l]), shifted(s_ref[hl + l])
            f_r, f_i = sp_ref[l], sp_ref[hl + l]
            g0_r, g0_i = gs_ref[l, pl.ds(0, nb), :], gs_ref[hl + l, pl.ds(0, nb), :]
            ga_ref[l] += over_time(g_r * p_r + g_i * p_i) + g0_r * f_r + g0_i * f_i
            ga_ref[hl + l] += over_time(g_i * p_r - g_r * p_i) + g0_i * f_r - g0_r * f_i

    rev = lambda t: (nt - 1 - t, 0)
    rev3 = lambda t: (0, nt - 1 - t, 0)
    full = lambda a: pl.BlockSpec(a.shape, lambda t, nd=a.ndim: (0,) * nd)
    return _carried_call(
        body, carry,
        name="ssm_scan_bwd",
        grid=(nt,),
        in_specs=[
            pl.BlockSpec((tt, nb * cb), rev),
            pl.BlockSpec((tt, nb * cb), rev),
            pl.BlockSpec((nl, tt * nb, LANES), rev3),
            pl.BlockSpec((nl, nb, LANES), rev3),
            full(wb), full(wc), full(abar_conj), full(d),
        ],
        out_specs=[
            pl.BlockSpec((tt, nb * cb), rev),
            pl.BlockSpec((nb, cb, sw2), lambda t: (0, 0, 0)),
            pl.BlockSpec((nb, sw2, cb), lambda t: (0, 0, 0)),
            pl.BlockSpec((nl, nb, LANES), lambda t: (0, 0, 0)),
            pl.BlockSpec((1, nb * cb), lambda t: (0, 0)),
        ],
        out_shape=[
            jax.ShapeDtypeStruct((t_dim, nb * cb), BF16),
            jax.ShapeDtypeStruct((nb, cb, sw2), F32),
            jax.ShapeDtypeStruct((nb, sw2, cb), F32),
            jax.ShapeDtypeStruct((nl, nb, LANES), F32),
            jax.ShapeDtypeStruct((1, nb * cb), F32),
        ],
        scratch_shapes=[pltpu.VMEM((nl, tt * nb, LANES), F32), pltpu.VMEM((nl, tt * nb, LANES), F32),
                        pltpu.VMEM((nl, nb, LANES), F32)],
        semantics=("arbitrary",),
        args=[z, dy, states, before, wb, wc, abar_conj, d],
    )


def _gmlp_chunk(zu, zv, gv, wm_ref, bias, n_heads):
    ua = _gelu(zu)
    vg = _gelu(zv)
    xc = vg - jnp.mean(vg, axis=-1, keepdims=True)
    r = lax.rsqrt(jnp.mean(xc * xc, axis=-1, keepdims=True) + EPS)
    vh = xc * r
    vb = (vh * gv).astype(BF16)
    parts = []
    for h in range(n_heads):
        cols = slice(h * GMLP_HEAD, (h + 1) * GMLP_HEAD)
        parts.append(_dot(wm_ref[h], vb[:, cols], "nn"))
    s = jnp.concatenate(parts, axis=1) + bias
    return ua, vh, r, vb, s


def _gmlp_fwd(z, gv, wm, bias, ggo):
    t_dim = z.shape[0]
    dg = gv.shape[1]
    n_heads = dg // GMLP_HEAD
    tr = _tile(t_dim, 256, CHUNK)

    def body(zu_ref, zv_ref, gv_ref, wm_ref, b_ref, ggo_ref, o_ref):
        for ck in range(tr // CHUNK):
            rows = pl.ds(ck * CHUNK, CHUNK)
            ua, _, _, _, s = _gmlp_chunk(zu_ref[rows, :], zv_ref[rows, :], gv_ref[...], wm_ref, b_ref[...], n_heads)
            yh, _ = _rms_stats(ua * s)
            o_ref[rows, :] = (yh * ggo_ref[...]).astype(BF16)

    full = lambda a: pl.BlockSpec(a.shape, lambda i, nd=a.ndim: (0,) * nd)
    return pl.pallas_call(
        body,
        name="gmlp_fwd",
        grid=(t_dim // tr,),
        in_specs=[pl.BlockSpec((tr, dg), lambda i: (i, 1)), pl.BlockSpec((tr, dg), lambda i: (i, 2)),
                  full(gv), full(wm), full(bias), full(ggo)],
        out_specs=pl.BlockSpec((tr, dg), lambda i: (i, 0)),
        out_shape=jax.ShapeDtypeStruct((t_dim, dg), BF16),
        compiler_params=_params(("parallel",)),
    )(z, z, gv, wm, bias, ggo)


def _gmlp_bwd(z, dycat, gv, wm, bias, ggo):
    t_dim = z.shape[0]
    dg = gv.shape[1]
    n_heads = dg // GMLP_HEAD
    tr = _tile(t_dim, 256, CHUNK)

    def body(zu_ref, zv_ref, dy_ref, gv_ref, wm_ref, b_ref, ggo_ref,
             dzu_ref, dzv_ref, dggo_ref, dgv_ref, dwm_ref, dsum_ref):
        @pl.when(pl.program_id(0) == 0)
        def _():
            dggo_ref[...] = jnp.zeros_like(dggo_ref)
            dgv_ref[...] = jnp.zeros_like(dgv_ref)
            dwm_ref[...] = jnp.zeros_like(dwm_ref)
            dsum_ref[...] = jnp.zeros_like(dsum_ref)

        for ck in range(tr // CHUNK):
            rows = pl.ds(ck * CHUNK, CHUNK)
            zu = zu_ref[rows, :]
            zv = zv_ref[rows, :]
            gvv = gv_ref[...]
            ua, vh, r, vb, s = _gmlp_chunk(zu, zv, gvv, wm_ref, b_ref[...], n_heads)
            dy, dggo = _rms_backward(ua * s, ggo_ref[...], dy_ref[rows, :])
            dggo_ref[...] += dggo
            ds = dy * ua
            dsum_ref[...] += ds
            dsb = ds.astype(BF16)
            parts = []
            for h in range(n_heads):
                cols = slice(h * GMLP_HEAD, (h + 1) * GMLP_HEAD)
                dwm_ref[h] += _dot(dsb[:, cols], vb[:, cols], "nt")
                parts.append(_dot(wm_ref[h], dsb[:, cols], "tn"))
            dv = jnp.concatenate(parts, axis=1)
            dgv_ref[...] += jnp.sum(dv * vh, axis=0, keepdims=True)
            dvh = dv * gvv
            dvg = r * (dvh - jnp.mean(dvh, axis=-1, keepdims=True) - vh * jnp.mean(dvh * vh, axis=-1, keepdims=True))
            dzv_ref[rows, :] = (dvg * _gelu_grad(zv)).astype(BF16)
            dzu_ref[rows, :] = (dy * s * _gelu_grad(zu)).astype(BF16)

    full = lambda a: pl.BlockSpec(a.shape, lambda i, nd=a.ndim: (0,) * nd)
    return pl.pallas_call(
        body,
        name="gmlp_bwd",
        grid=(t_dim // tr,),
        in_specs=[pl.BlockSpec((tr, dg), lambda i: (i, 1)), pl.BlockSpec((tr, dg), lambda i: (i, 2)),
                  pl.BlockSpec((tr, dg), lambda i: (i, 1)), full(gv), full(wm), full(bias), full(ggo)],
        out_specs=[pl.BlockSpec((tr, dg), lambda i: (i, 0)), pl.BlockSpec((tr, dg), lambda i: (i, 0)),
                   pl.BlockSpec((1, dg), lambda i: (0, 0)), pl.BlockSpec((1, dg), lambda i: (0, 0)),
                   pl.BlockSpec(wm.shape, lambda i: (0, 0, 0)), pl.BlockSpec((CHUNK, dg), lambda i: (0, 0))],
        out_shape=[jax.ShapeDtypeStruct((t_dim, dg), BF16), jax.ShapeDtypeStruct((t_dim, dg), BF16),
                   jax.ShapeDtypeStruct((1, dg), F32), jax.ShapeDtypeStruct((1, dg), F32),
                   jax.ShapeDtypeStruct(wm.shape, F32), jax.ShapeDtypeStruct((CHUNK, dg), F32)],
        compiler_params=_params(("arbitrary",)),
    )(z, z, dycat, gv, wm, bias, ggo)


def _position():
    x, y, c = lax.axis_index("x"), lax.axis_index("y"), lax.axis_index("c")
    chips = [(1 - x, y), (x, 1 - y), (1 - x, 1 - y)]
    return x, y, c, chips


def _region(ref, kind, shard_shape, q, half):
    rs, cs = shard_shape
    r0, nr = (0, rs) if half is None else (half * (rs // 2), rs // 2)
    if kind == "row":
        return ref.at[pl.ds(q * rs + r0, nr), :]
    return ref.at[pl.ds(r0, nr), pl.ds(q * cs, cs)]


def _full_shape(kind, shard_shape):
    rs, cs = shard_shape
    return (N_CHIPS * rs, cs) if kind == "row" else (rs, N_CHIPS * cs)


def _remote(src, dst, send_sems, recv_sems, k, to):
    return pltpu.make_async_remote_copy(src_ref=src, dst_ref=dst, send_sem=send_sems.at[k], recv_sem=recv_sems.at[k],
                                        device_id=to, device_id_type=MESH)


def _same(arrays):
    return [jax.ShapeDtypeStruct(a.shape, a.dtype) for a in arrays]


def _gather_ici_carry(gathered, kinds, shapes):
    nw = len(gathered)

    def copies(ops, full, send_sems, recv_sems):
        x, y, c, chips = _position()
        out = []
        for w in range(nw):
            mine = _region(full[w], kinds[w], shapes[w], 2 * x + y, c)
            for j, (cx, cy) in enumerate(chips):
                out.append(_remote(mine, mine, send_sems, recv_sems, 3 * w + j, (cx, cy, c)))
        return out

    return _Carry(gathered, _same(gathered), {i: i for i in range(nw)}, 3 * nw, copies)


def _gather_d2d_carry(gathered, kinds, shapes):
    nw = len(gathered)

    def copies(ops, full, send_sems, recv_sems):
        x, y, c, chips = _position()
        out = []
        for w in range(nw):
            for j, (cx, cy) in enumerate(chips):
                landed = _region(full[w], kinds[w], shapes[w], 2 * cx + cy, c)
                out.append(_remote(landed, landed, send_sems, recv_sems, 3 * w + j, (x, y, 1 - c)))
        return out

    return _Carry(gathered, _same(gathered), {i: i for i in range(nw)}, 3 * nw, copies)


def _pairs_carry(grads, kinds, shapes):
    nw = len(grads)

    def copies(g, got, send_sems, recv_sems):
        x, y, c, _ = _position()
        out = []
        for w in range(nw):
            for q in range(N_CHIPS):
                out.append(_remote(_region(g[w], kinds[w], shapes[w], q, 1 - c), got[w].at[q], send_sems, recv_sems,
                                   N_CHIPS * w + q, (x, y, 1 - c)))
        return out

    outs = [jax.ShapeDtypeStruct((N_CHIPS, s[0] // 2, s[1]), BF16) for s in shapes]
    return _Carry(grads, outs, {}, N_CHIPS * nw, copies)


def _pair_sum(name, grad, got, kind, shard_shape, c_arr):
    rs, cs = shard_shape
    hr = rs // 2
    tr = _tile(hr, 512, 16)
    nr = hr // tr

    def body(c_ref, g_ref, s_ref, o_ref):
        o_ref[...] = (g_ref[...].astype(F32) + s_ref[...].astype(F32)).astype(BF16)

    if kind == "row":
        g_spec = pl.BlockSpec((tr, cs), lambda q, i, c_ref: (q * (rs // tr) + c_ref[0] * nr + i, 0))
    else:
        g_spec = pl.BlockSpec((tr, cs), lambda q, i, c_ref: (c_ref[0] * nr + i, q))
    blk = pl.BlockSpec((None, tr, cs), lambda q, i, c_ref: (q, i, 0))
    return pl.pallas_call(
        body,
        name=name,
        grid_spec=pltpu.PrefetchScalarGridSpec(num_scalar_prefetch=1, grid=(N_CHIPS, nr), in_specs=[g_spec, blk],
                                               out_specs=blk),
        out_shape=jax.ShapeDtypeStruct((N_CHIPS, hr, cs), BF16),
        compiler_params=_params(("parallel", "parallel")),
    )(c_arr, grad, got)


def _scatter_carry(sums, shapes):
    nw = len(sums)

    def copies(ps, got, send_sems, recv_sems):
        x, y, c, chips = _position()
        out = []
        for w in range(nw):
            for j, (cx, cy) in enumerate(chips):
                out.append(_remote(ps[w].at[2 * cx + cy], got[w].at[j], send_sems, recv_sems, 3 * w + j, (cx, cy, c)))
        return out

    outs = [jax.ShapeDtypeStruct((3, s[0] // 2, s[1]), BF16) for s in shapes]
    return _Carry(sums, outs, {}, 3 * nw, copies)


def _owner_sum(name, sums, got, shard_shape, qc_arr):
    rs, cs = shard_shape
    hr = rs // 2
    tr = _tile(hr, 512, 16)
    nr = hr // tr

    def body(qc_ref, mine_ref, got_ref, o_ref):
        acc = mine_ref[...].astype(F32)
        for j in range(3):
            acc = acc + got_ref[j].astype(F32)
        o_ref[...] = acc

    return pl.pallas_call(
        body,
        name=name,
        grid_spec=pltpu.PrefetchScalarGridSpec(
            num_scalar_prefetch=1, grid=(nr,),
            in_specs=[pl.BlockSpec((None, tr, cs), lambda i, qc_ref: (qc_ref[0], i, 0)),
                      pl.BlockSpec((3, tr, cs), lambda i, qc_ref: (0, i, 0))],
            out_specs=pl.BlockSpec((tr, cs), lambda i, qc_ref: (qc_ref[1] * nr + i, 0))),
        out_shape=jax.ShapeDtypeStruct((rs, cs), F32),
        compiler_params=_params(("parallel",)),
    )(qc_arr, sums, got)


def _share_carry(grads, shapes):
    nw = len(grads)

    def copies(ops, out, send_sems, recv_sems):
        x, y, c, _ = _position()
        res = []
        for w in range(nw):
            hr = shapes[w][0] // 2
            mine = out[w].at[pl.ds(c * hr, hr), :]
            res.append(_remote(mine, mine, send_sems, recv_sems, w, (x, y, 1 - c)))
        return res

    return _Carry(grads, _same(grads), {i: i for i in range(nw)}, nw, copies)


def _place_block(packed, me):
    return lax.dynamic_update_slice(jnp.zeros((N_DEV,) + packed.shape, F32), packed[None], (me, 0, 0))


def _exchange_carry(blocks):
    def copies(ops, res, send_sems, recv_sems):
        x, y, c, _ = _position()
        mine = res[0].at[4 * x + 2 * y + c]
        out = []
        for k in range(1, N_DEV):
            to = ((1 - x) if k & 4 else x, (1 - y) if k & 2 else y, (1 - c) if k & 1 else c)
            out.append(_remote(mine, mine, send_sems, recv_sems, k - 1, to))
        return out

    return _Carry([blocks], _same([blocks]), {0: 0}, N_DEV - 1, copies)


def _sum_blocks(name, blocks):
    n, rows, lanes = blocks.shape
    tr = _tile(rows, 512, 8)

    def body(b_ref, o_ref):
        acc = b_ref[0]
        for k in range(1, n):
            acc = acc + b_ref[k]
        o_ref[...] = acc

    return pl.pallas_call(
        body,
        name=name,
        grid=(rows // tr,),
        in_specs=[pl.BlockSpec((n, tr, lanes), lambda i: (0, i, 0))],
        out_specs=pl.BlockSpec((tr, lanes), lambda i: (i, 0)),
        out_shape=jax.ShapeDtypeStruct((rows, lanes), F32),
        compiler_params=_params(("parallel",)),
    )(blocks)


def _adamw(name, w, g, m, v):
    def fn(rows, pars):
        wv, gv, mv, vv = rows
        m_new = ADAM_B1 * mv + (1.0 - ADAM_B1) * gv
        v_new = ADAM_B2 * vv + (1.0 - ADAM_B2) * (gv * gv)
        m_hat = m_new / (1.0 - ADAM_B1 ** ADAM_STEP)
        v_hat = v_new / (1.0 - ADAM_B2 ** ADAM_STEP)
        delta = -ADAM_LR * (m_hat / (jnp.sqrt(v_hat) + ADAM_EPS) + ADAM_WD * wv)
        return [delta, m_new, v_new, gv], []

    c = w.shape[1]
    return _rowwise(name, fn, [w, g, m, v], [], [(c, F32)] * 4, tr=256)


def _pack(arrays):
    rows = []
    for a in arrays:
        flat = a.reshape(-1).astype(F32)
        pad = (-flat.shape[0]) % LANES
        rows.append(jnp.pad(flat, (0, pad)).reshape(-1, LANES))
    stacked = jnp.concatenate(rows, axis=0)
    pad_rows = (-stacked.shape[0]) % 8
    return jnp.pad(stacked, ((0, pad_rows), (0, 0)))


def _unpack(packed, shapes):
    out, r = [], 0
    for s in shapes:
        n = math.prod(s)
        nr = -(-n // LANES)
        out.append(packed[r:r + nr].reshape(-1)[:n].reshape(s))
        r += nr
    return out


BIG = ["w1_gate", "w1_up", "w1_down", "w_in", "ssm_w_glu", "w_out", "w2_gate", "w2_up", "w2_down", "w_ple_gate",
       "w_ple_proj"]
KIND = {"w1_gate": "col", "w1_up": "col", "w1_down": "row", "w_in": "col", "ssm_w_glu": "row", "w_out": "row",
        "w2_gate": "col", "w2_up": "col", "w2_down": "row", "w_ple_gate": "row", "w_ple_proj": "col"}
SMALL = ["norm_ffn1", "norm_mix", "ssm_log_dt", "ssm_a_re", "ssm_a_im", "ssm_b_re", "ssm_b_im", "ssm_c_re", "ssm_c_im",
         "ssm_d", "gmlp_norm_v", "gmlp_w_s", "gmlp_b_s", "norm_ssm_out", "norm_gmlp_out", "norm_ffn2", "norm_ple",
         "norm_final"]
WEIGHTS = ["norm_ffn1", "w1_gate", "w1_up", "w1_down", "norm_mix", "w_in", "ssm_log_dt", "ssm_a_re", "ssm_a_im",
           "ssm_b_re", "ssm_b_im", "ssm_c_re", "ssm_c_im", "ssm_d", "ssm_w_glu", "gmlp_norm_v", "gmlp_w_s", "gmlp_b_s",
           "norm_ssm_out", "norm_gmlp_out", "w_out", "norm_ffn2", "w2_gate", "w2_up", "w2_down", "norm_ple",
           "w_ple_gate", "w_ple_proj", "norm_final"]


class _Trip:
    def __init__(self, names, arrays, carry):
        self.names, self.arrays, self.carry = names, arrays, carry


class _Reducer:
    def __init__(self, shard_shape, c_arr, qc_arr):
        self.shard_shape, self.c_arr, self.qc_arr = shard_shape, c_arr, qc_arr
        self.halves = {}

    def swap(self, names, grads):
        kinds = [KIND[n] for n in names]
        shapes = [self.shard_shape[n] for n in names]
        return _Trip(names, grads, _pairs_carry(grads, kinds, shapes))

    def send(self, trip, swapped):
        shapes = [self.shard_shape[n] for n in trip.names]
        sums = [_pair_sum("pair_sum_" + n, g, s, KIND[n], sh, self.c_arr)
                for n, g, s, sh in zip(trip.names, trip.arrays, swapped, shapes)]
        return _Trip(trip.names, sums, _scatter_carry(sums, shapes))

    def end(self, trip, got):
        for n, ps, g in zip(trip.names, trip.arrays, got):
            self.halves[n] = _owner_sum("owner_sum_" + n, ps, g, self.shard_shape[n], self.qc_arr)


def _ride(*carries):
    present = [c for c in carries if c is not None]
    joined = functools.reduce(_join, present) if present else None

    def split(results):
        out, at = [], 0
        for c in carries:
            n = len(c.out_shapes) if c is not None else 0
            out.append(list(results[at:at + n]))
            at += n
        return out

    return joined, split


def _step(x, p, tgt, w, m, v):
    d_model = x.shape[1]
    d_ssm = w["ssm_d"].shape[1]
    n_groups = d_ssm // SSM_GROUP
    row = lambda a: a.reshape(1, -1)

    xi, yi, ci = lax.axis_index("x"), lax.axis_index("y"), lax.axis_index("c")
    c_arr = jnp.reshape(ci, (1,)).astype(jnp.int32)
    q_arr = jnp.reshape(2 * xi + yi, (1,)).astype(jnp.int32)
    qc_arr = jnp.stack([2 * xi + yi, ci]).astype(jnp.int32)
    shard_shape = {n: w[n].shape for n in BIG}
    full = {n: _cast_into_gathered("cast_" + n, w[n], KIND[n], q_arr) for n in BIG}

    def gather(stage, names):
        return stage([full[n] for n in names], [KIND[n] for n in names], [shard_shape[n] for n in names])

    def gathered(names, arrays):
        full.update(zip(names, arrays))

    zeroth, first, second, third = ["w1_gate"], ["w1_up"], ["w1_down", "w_in"], ["w2_gate"]
    fourth, fifth, sixth = ["ssm_w_glu", "w_out"], ["w2_up"], ["w2_down", "w_ple_gate", "w_ple_proj"]

    def two_stages(passed_on, landing):
        return _join(gather(_gather_d2d_carry, passed_on), gather(_gather_ici_carry, landing)), passed_on + landing

    gathered(zeroth, _comm_call("gather_zeroth_ici", gather(_gather_ici_carry, zeroth)))
    gathered(zeroth, _comm_call("gather_zeroth_d2d", gather(_gather_d2d_carry, zeroth)))
    n1 = _rms_fwd("ffn1_norm", x, w["norm_ffn1"])
    gate1, landed = _mm_nn("ffn1_gate", n1, full["w1_gate"], BF16, tm=1024, tn=512, tk=2048,
                           carry=gather(_gather_ici_carry, first))
    gathered(first, landed)
    gathered(first, _comm_call("gather_first_d2d", gather(_gather_d2d_carry, first)))
    (a1, up1), landed = _ffn_up("ffn1", n1, full["w1_up"], gate1, gather(_gather_ici_carry, second))
    gathered(second, landed)
    gathered(second, _comm_call("gather_second_d2d", gather(_gather_d2d_carry, second)))
    h1, landed = _ffn_down("ffn1", a1, full["w1_down"], x, gather(_gather_ici_carry, third))
    gathered(third, landed)
    ffn1 = (n1, a1, gate1, up1)
    nm = _rms_fwd("mix_norm", h1, w["norm_mix"])
    carry, names = two_stages(third, fourth)
    z, landed = _mm_nn("in_proj", nm, full["w_in"], F32, tm=1024, tn=512, tk=2048, carry=carry)
    gathered(names, landed)

    disc, disc_vjp = jax.vjp(_ssm_discretize, w["ssm_log_dt"][0], w["ssm_a_re"], w["ssm_a_im"], w["ssm_b_re"],
                             w["ssm_b_im"])
    abar_r, abar_i, bbar_r, bbar_i = disc
    nb = n_groups // GROUPS_PER_BLOCK
    wb = jnp.concatenate([_blockdiag_in(bbar_r), _blockdiag_in(bbar_i)], axis=-1).astype(BF16)
    wc = jnp.concatenate([_blockdiag_out(w["ssm_c_re"]), -_blockdiag_out(w["ssm_c_im"])], axis=1).astype(BF16)
    abar = jnp.concatenate([abar_r.reshape(nb, -1), abar_i.reshape(nb, -1)], axis=-1)
    abar_conj = jnp.concatenate([abar_r.reshape(nb, -1), -abar_i.reshape(nb, -1)], axis=-1)
    carry, names = two_stages(fourth, fifth)
    (states, y_pre, yg), landed = _ssm_scan_fwd(z, wb, wc, abar, w["ssm_d"], carry)
    gathered(names, landed)
    q = _mm_nn("glu_proj", yg, full["ssm_w_glu"], F32, tm=1024, tn=1024, tk=1024)

    def glu_norm(rows, pars):
        yv = _gelu(rows[0]) * _sigmoid(rows[1])
        yh, _ = _rms_stats(yv)
        return [yh * pars[0]], []

    yn_ssm = _rowwise("ssm_glu_norm", glu_norm, [y_pre, q], [w["norm_ssm_out"]], [(d_ssm, BF16)])[0]

    tril = jnp.tril(jnp.ones((CHUNK, CHUNK), dtype=bool))
    wm = jnp.where(tril[None], w["gmlp_w_s"], 0.0).astype(BF16)
    bias = jnp.repeat(w["gmlp_b_s"].T, GMLP_HEAD, axis=1)
    yn_gmlp = _gmlp_fwd(z, w["gmlp_norm_v"], wm, bias, w["norm_gmlp_out"])
    ycat = jnp.concatenate([yn_ssm, yn_gmlp], axis=1)
    h2, landed = _mm_nn("out_proj", ycat, full["w_out"], F32, res=h1, alpha=1.0, tm=512, tn=1024, tk=2048,
                        carry=gather(_gather_d2d_carry, fifth))
    gathered(fifth, landed)

    n2 = _rms_fwd("ffn2_norm", h2, w["norm_ffn2"])
    (a2, gate2, up2), landed = _ffn_gateup("ffn2", n2, full["w2_gate"], full["w2_up"],
                                           gather(_gather_ici_carry, sixth))
    gathered(sixth, landed)
    gathered(sixth, _comm_call("gather_sixth_d2d", gather(_gather_d2d_carry, sixth)))
    h3 = _ffn_down("ffn2", a2, full["w2_down"], h2)
    ffn2 = (n2, a2, gate2, up2)
    npl = _rms_fwd("ple_norm", h3, w["norm_ple"])
    gq = _mm_nn("ple_gate", npl, full["w_ple_gate"], F32, tm=1024, tn=1024, tk=2048)
    pp = _mm_nn("ple_proj", p, full["w_ple_proj"], F32, tm=1024, tn=1024, tk=2048)

    def ple_combine(rows, pars):
        return [rows[0] + _sigmoid(rows[1]) * rows[2]], []

    h4 = _rowwise("ple_combine", ple_combine, [h3, gq, pp], [], [(d_model, F32)], tr=256)[0]

    def head(rows, pars):
        hv, tv = rows
        xh, _ = _rms_stats(hv)
        err = xh * pars[0] - tv
        dx, dg = _rms_backward(hv, pars[0], err * (1.0 / d_model))
        part = 0.5 * jnp.sum(err * err) * (1.0 / d_model)
        return [dx], [dg, jnp.full((1, LANES), part, F32)]

    dh4, g_norm_final, loss_part = _rowwise("loss_head", head, [h4, tgt], [row(w["norm_final"])], [(d_model, F32)],
                                            [(1, d_model), (1, LANES)], tr=256)

    def ple_bwd(rows, pars):
        dh, gqv, ppv = rows
        gate = _sigmoid(gqv)
        return [dh * ppv * gate * (1.0 - gate), dh * gate], []

    dgq, dpp = _rowwise("ple_dgate", ple_bwd, [dh4, gq, pp], [], [(d_model, BF16)] * 2, tr=256)
    reducer = _Reducer(shard_shape, c_arr, qc_arr)
    (g_w_ple_proj,) = _mm_tn("ple_dwproj", p, [dpp], BF16, tm=256, tn=1024, tk=1024)
    (g_w_ple_gate,) = _mm_tn("ple_dwgate", npl, [dgq], BF16, tm=1024, tn=1024, tk=1024)
    ple = reducer.swap(["w_ple_gate", "w_ple_proj"], [g_w_ple_gate, g_w_ple_proj])
    dnpl, swapped = _mm_nt_sum("ple_dnorm_in", [dgq], [full["w_ple_gate"]], F32, tm=512, tn=1024, tk=2048,
                               carry=ple.carry)
    ple = reducer.send(ple, swapped)
    dh3, dh3_b, g_norm_ple = _rms_bwd("ple_dnorm", h3, w["norm_ple"], [dnpl], dh4, 0.5)

    dh2, dh2_b, g_norm_ffn2, got, _ = _ffn_bwd("ffn2", ["w2_gate", "w2_up", "w2_down"], h2, w["norm_ffn2"],
                                               full["w2_gate"], full["w2_up"], full["w2_down"], ffn2, dh3, dh3_b,
                                               1.0, reducer, riding=ple.carry)
    reducer.end(ple, got)

    dycat = _mm_nt_sum("out_dproj", [dh2_b], [full["w_out"]], F32, tm=512, tn=1024, tk=2048)
    (g_w_out,) = _mm_tn("out_dw", ycat, [dh2_b], BF16, tm=1024, tn=1024, tk=1024)

    dzu, dzv, g_norm_gmlp_out, g_gmlp_norm_v, g_wm, g_s = _gmlp_bwd(z, dycat, w["gmlp_norm_v"], wm, bias,
                                                                  w["norm_gmlp_out"])
    g_gmlp_w_s = jnp.where(tril[None], g_wm, 0.0)
    g_gmlp_b_s = g_s.reshape(CHUNK, -1, GMLP_HEAD).sum(axis=-1).T

    def glu_bwd(rows, pars):
        dyn, ypre, qv = rows
        ygv = _gelu(ypre)
        sg = _sigmoid(qv)
        dy, dg = _rms_backward(ygv * sg, pars[0], dyn)
        return [dy * ygv * sg * (1.0 - sg), dy * sg], [dg]

    dq, dyg_part, g_norm_ssm_out = _rowwise("ssm_dglu", glu_bwd, [(dycat, d_ssm, 0), y_pre, q], [w["norm_ssm_out"]],
                                            [(d_ssm, BF16), (d_ssm, F32)], [(1, d_ssm)])
    dyg_proj = _mm_nt_sum("glu_dproj", [dq], [full["ssm_w_glu"]], F32, tm=1024, tn=1024, tk=1024)
    (g_ssm_w_glu,) = _mm_tn("glu_dw", yg, [dq], BF16, tm=1024, tn=1024, tk=1024)

    def gelu_bwd(rows, pars):
        return [(rows[0] + rows[1]) * _gelu_grad(rows[2])], []

    dy_pre = _rowwise("ssm_dgelu", gelu_bwd, [dyg_part, dyg_proj, y_pre], [], [(d_ssm, F32)])[0]
    mixers = reducer.swap(["w_out", "ssm_w_glu"], [g_w_out, g_ssm_w_glu])
    (dz_ssm, g_wb, g_wc, g_abar, g_ssm_d), swapped = _ssm_scan_bwd(z, dy_pre, states, wb, wc, abar_conj, w["ssm_d"],
                                                                  mixers.carry)
    mixers = reducer.send(mixers, swapped)
    g_abar = jnp.transpose(g_abar, (1, 0, 2)).reshape(nb, -1)
    sw = g_abar.shape[-1] // 2
    g_bbar_r = _blockdiag_in_grad(g_wb[..., :sw], SSM_STATE, SSM_GROUP)
    g_bbar_i = _blockdiag_in_grad(g_wb[..., sw:], SSM_STATE, SSM_GROUP)
    g_ssm_c_re = _blockdiag_out_grad(g_wc[:, :sw, :], SSM_GROUP, SSM_STATE)
    g_ssm_c_im = -_blockdiag_out_grad(g_wc[:, sw:, :], SSM_GROUP, SSM_STATE)
    g_abar_r = g_abar[..., :sw].reshape(n_groups, SSM_STATE)
    g_abar_i = g_abar[..., sw:].reshape(n_groups, SSM_STATE)
    g_ssm_log_dt, g_ssm_a_re, g_ssm_a_im, g_ssm_b_re, g_ssm_b_im = disc_vjp((g_abar_r, g_abar_i, g_bbar_r, g_bbar_i))

    dz = jnp.concatenate([dz_ssm, dzu, dzv], axis=1)
    (g_w_in,), got = _mm_tn("in_dw", nm, [dz], BF16, tm=1024, tn=768, tk=1024, carry=mixers.carry)
    reducer.end(mixers, got)
    in_w = reducer.swap(["w_in"], [g_w_in])
    dnm, swapped = _mm_nt_sum("in_dproj", [dz], [full["w_in"]], F32, tm=512, tn=1024, tk=1024, carry=in_w.carry)
    in_w = reducer.send(in_w, swapped)
    dh1, dh1_b, g_norm_mix = _rms_bwd("mix_dnorm", h1, w["norm_mix"], [dnm], dh2, 0.5)
    small = {"norm_mix": g_norm_mix, "ssm_log_dt": g_ssm_log_dt, "ssm_a_re": g_ssm_a_re,
             "ssm_a_im": g_ssm_a_im, "ssm_b_re": g_ssm_b_re, "ssm_b_im": g_ssm_b_im, "ssm_c_re": g_ssm_c_re,
             "ssm_c_im": g_ssm_c_im, "ssm_d": g_ssm_d, "gmlp_norm_v": g_gmlp_norm_v, "gmlp_w_s": g_gmlp_w_s,
             "gmlp_b_s": g_gmlp_b_s, "norm_ssm_out": g_norm_ssm_out, "norm_gmlp_out": g_norm_gmlp_out,
             "norm_ffn2": g_norm_ffn2, "norm_ple": g_norm_ple, "norm_final": g_norm_final}
    early = [n for n in SMALL if n != "norm_ffn1"]
    me = 4 * xi + 2 * yi + ci
    early_blocks = _place_block(_pack([small[n] for n in early] + [loss_part[:, :1]]), me)
    dx, _, g_norm_ffn1, got, (early_blocks,) = _ffn_bwd(
        "ffn1", ["w1_gate", "w1_up", "w1_down"], x, w["norm_ffn1"], full["w1_gate"], full["w1_up"], full["w1_down"],
        ffn1, dh1, dh1_b, 1.0, reducer, riding=in_w.carry, riding_dwd=_exchange_carry(early_blocks))
    reducer.end(in_w, got)
    early_grads = _unpack(_sum_blocks("sum_small", early_blocks), [w[n].shape for n in early] + [(1,)])
    loss = early_grads[-1].reshape(())

    halves = [reducer.halves[n] for n in BIG]
    late_blocks = _place_block(_pack([g_norm_ffn1]), me)
    last = _join(_share_carry(halves, [shard_shape[n] for n in BIG]), _exchange_carry(late_blocks))
    *shared, late_blocks = _comm_call("share_halves", last)
    grad = dict(zip(BIG, shared))
    grad.update(zip(early, early_grads[:-1]))
    grad["norm_ffn1"] = _unpack(_sum_blocks("sum_first_norm", late_blocks), [w["norm_ffn1"].shape])[0]

    small_shapes = [w[n].shape for n in SMALL]
    delta, new_m, new_v = {}, {}, {}
    for n in BIG:
        delta[n], new_m[n], new_v[n], grad[n] = _adamw("adamw_" + n, w[n], grad[n], m[n], v[n])
    d_p, m_p, v_p, _ = _adamw("adamw_small", _pack([w[n] for n in SMALL]), _pack([grad[n] for n in SMALL]),
                              _pack([m[n] for n in SMALL]), _pack([v[n] for n in SMALL]))
    for name_list, packed in ((delta, d_p), (new_m, m_p), (new_v, v_p)):
        for n, a in zip(SMALL, _unpack(packed, small_shapes)):
            name_list[n] = a
    return loss, dx, grad, delta, new_m, new_v


def kernel(x, p, norm_ffn1, w1_gate, w1_up, w1_down, norm_mix, w_in, ssm_log_dt, ssm_a_re, ssm_a_im, ssm_b_re, ssm_b_im, ssm_c_re, ssm_c_im, ssm_d, ssm_w_glu, gmlp_norm_v, gmlp_w_s, gmlp_b_s, norm_ssm_out, norm_gmlp_out, w_out, norm_ffn2, w2_gate, w2_up, w2_down, norm_ple, w_ple_gate, w_ple_proj, norm_final, loss_target, m_norm_ffn1, m_w1_gate, m_w1_up, m_w1_down, m_norm_mix, m_w_in, m_ssm_log_dt, m_ssm_a_re, m_ssm_a_im, m_ssm_b_re, m_ssm_b_im, m_ssm_c_re, m_ssm_c_im, m_ssm_d, m_ssm_w_glu, m_gmlp_norm_v, m_gmlp_w_s, m_gmlp_b_s, m_norm_ssm_out, m_norm_gmlp_out, m_w_out, m_norm_ffn2, m_w2_gate, m_w2_up, m_w2_down, m_norm_ple, m_w_ple_gate, m_w_ple_proj, m_norm_final, v_norm_ffn1, v_w1_gate, v_w1_up, v_w1_down, v_norm_mix, v_w_in, v_ssm_log_dt, v_ssm_a_re, v_ssm_a_im, v_ssm_b_re, v_ssm_b_im, v_ssm_c_re, v_ssm_c_im, v_ssm_d, v_ssm_w_glu, v_gmlp_norm_v, v_gmlp_w_s, v_gmlp_b_s, v_norm_ssm_out, v_norm_gmlp_out, v_w_out, v_norm_ffn2, v_w2_gate, v_w2_up, v_w2_down, v_norm_ple, v_w_ple_gate, v_w_ple_proj, v_norm_final):
    given = dict(locals())
    shapes = {n: given[n].shape for n in WEIGHTS}

    def block(name):
        a = given[name]
        if a.ndim == 1:
            return a.reshape(1, -1)
        return a[0] if a.ndim >= 3 else a

    w = {n: block(n) for n in WEIGHTS}
    m = {n: block("m_" + n) for n in WEIGHTS}
    v = {n: block("v_" + n) for n in WEIGHTS}
    loss, dx, grad, delta, new_m, new_v = _step(x[0], p[0, 0], loss_target[0], w, m, v)
    outs = [loss, dx[None]]
    for tree in (grad, delta, new_m, new_v):
        outs += [tree[n].reshape(shapes[n]) for n in WEIGHTS]
    return tuple(outs)
```

```python
import functools
import math

import jax
import jax.numpy as jnp
from jax import lax
from jax.experimental import pallas as pl
from jax.experimental.pallas import tpu as pltpu

F32 = jnp.float32
BF16 = jnp.bfloat16
EPS = 1e-6
SSM_GROUP = 16
SSM_STATE = 64
GROUPS_PER_BLOCK = 8
GMLP_HEAD = 128
CHUNK = 128
ADAM_LR = 0.001
ADAM_B1 = 0.9
ADAM_B2 = 0.999
ADAM_EPS = 1e-08
ADAM_WD = 0.01
ADAM_STEP = 10
N_CHIPS = 4
N_DEV = 8
LANES = 128
VMEM_LIMIT_BYTES = 56 * 1024 * 1024
MESH = pl.DeviceIdType.MESH
GELU_C = math.sqrt(2.0 / math.pi)
GELU_A = 0.044715

_DOT_DIMS = {
    "nn": (((1,), (0,)), ((), ())),
    "nt": (((1,), (1,)), ((), ())),
    "tn": (((0,), (0,)), ((), ())),
}


def _tile(dim, pref, align):
    if dim <= pref:
        return dim
    t = (pref // align) * align
    while t >= align:
        if dim % t == 0:
            return t
        t -= align
    return dim


def _params(semantics):
    return pltpu.CompilerParams(dimension_semantics=semantics, vmem_limit_bytes=VMEM_LIMIT_BYTES)


def _gelu(x):
    return 0.5 * x * (1.0 + jnp.tanh(GELU_C * (x + GELU_A * x * x * x)))


def _gelu_grad(x):
    t = jnp.tanh(GELU_C * (x + GELU_A * x * x * x))
    return 0.5 * (1.0 + t) + 0.5 * x * (1.0 - t * t) * GELU_C * (1.0 + 3.0 * GELU_A * x * x)


def _sigmoid(x):
    return 1.0 / (1.0 + jnp.exp(-x))


def _dot(a, b, mode):
    return lax.dot_general(a.astype(BF16), b.astype(BF16), _DOT_DIMS[mode], preferred_element_type=F32)


class _Carry:
    def __init__(self, arrays, out_shapes, aliases, n_copies, copies):
        self.arrays = list(arrays)
        self.out_shapes = list(out_shapes)
        self.aliases = dict(aliases)
        self.n_copies = n_copies
        self.copies = copies

    def scratch(self):
        return [pltpu.SemaphoreType.DMA((self.n_copies,)), pltpu.SemaphoreType.DMA((self.n_copies,))]

    def split(self, refs):
        n_in, n_out = len(self.arrays), len(self.out_shapes)
        return refs[:n_in], refs[n_in:n_in + n_out], refs[n_in + n_out], refs[n_in + n_out + 1]

    def start(self, refs):
        for cp in self.copies(*self.split(refs)):
            cp.start()

    def wait(self, refs):
        for cp in self.copies(*self.split(refs)):
            cp.wait()


class _SemRange:
    def __init__(self, sems, offset):
        self.sems, self.offset = sems, offset

    @property
    def at(self):
        return self

    def __getitem__(self, k):
        return self.sems.at[self.offset + k]


def _join(first, second):
    n_in, n_out = len(first.arrays), len(first.out_shapes)
    aliases = dict(first.aliases)
    aliases.update({n_in + i: n_out + o for i, o in second.aliases.items()})

    def copies(ops, res, send_sems, recv_sems):
        return (first.copies(ops[:n_in], res[:n_out], send_sems, recv_sems)
                + second.copies(ops[n_in:], res[n_out:], _SemRange(send_sems, first.n_copies),
                                _SemRange(recv_sems, first.n_copies)))

    return _Carry(first.arrays + second.arrays, first.out_shapes + second.out_shapes, aliases,
                  first.n_copies + second.n_copies, copies)


_ANY = pl.BlockSpec(memory_space=pl.ANY)


def _comm_call(name, carry):
    def body(*refs):
        carry.start(refs)
        carry.wait(refs)

    n_in = len(carry.arrays)
    return pl.pallas_call(
        body,
        name=name,
        in_specs=[_ANY] * n_in,
        out_specs=[_ANY] * len(carry.out_shapes),
        out_shape=carry.out_shapes,
        input_output_aliases=carry.aliases,
        scratch_shapes=carry.scratch(),
    )(*carry.arrays)


def _carried_call(body, carry, *, name, grid, in_specs, out_specs, out_shape, scratch_shapes, semantics, args):
    if carry is None:
        res = pl.pallas_call(body, name=name, grid=grid, in_specs=in_specs, out_specs=out_specs, out_shape=out_shape,
                             scratch_shapes=scratch_shapes, compiler_params=_params(semantics))(*args)
        return res, []
    n_in, n_out, n_scr = len(in_specs), len(out_specs), len(scratch_shapes)
    nci, nco = len(carry.arrays), len(carry.out_shapes)

    def wrapped(*refs):
        ins = refs[:n_in]
        outs = refs[n_in + nci:n_in + nci + n_out]
        scr = refs[n_in + nci + n_out + nco:n_in + nci + n_out + nco + n_scr]
        c_refs = (refs[n_in:n_in + nci] + refs[n_in + nci + n_out:n_in + nci + n_out + nco]
                  + refs[n_in + nci + n_out + nco + n_scr:])
        first = functools.reduce(jnp.logical_and, [pl.program_id(d) == 0 for d in range(len(grid))])
        last = functools.reduce(jnp.logical_and, [pl.program_id(d) == grid[d] - 1 for d in range(len(grid))])

        @pl.when(first)
        def _():
            carry.start(c_refs)

        body(*ins, *outs, *scr)

        @pl.when(last)
        def _():
            carry.wait(c_refs)

    res = pl.pallas_call(
        wrapped,
        name=name,
        grid=grid,
        in_specs=list(in_specs) + [_ANY] * nci,
        out_specs=list(out_specs) + [_ANY] * nco,
        out_shape=list(out_shape) + carry.out_shapes,
        input_output_aliases={n_in + i: n_out + o for i, o in carry.aliases.items()},
        scratch_shapes=list(scratch_shapes) + carry.scratch(),
        compiler_params=_params(("arbitrary",) * len(grid)),
    )(*args, *carry.arrays)
    return res[:n_out], res[n_out:]


def _matmul(name, mode, a_list, b_list, products, out_dtypes, epilogue, extras=(), tm=512, tn=512, tk=2048,
            carry=None, n_part=(0, 1)):
    a0, b0 = a_list[0], b_list[0]
    if mode == "tn":
        k_dim, m_dim = a0.shape
    else:
        m_dim, k_dim = a0.shape
    n_dim = (b0.shape[0] if mode == "nt" else b0.shape[1]) // n_part[1]
    tm = _tile(m_dim, tm, LANES)
    tn = _tile(n_dim, tn, LANES)
    tk = _tile(k_dim, tk, LANES)
    nk = k_dim // tk
    j0 = n_part[0] * (n_dim // tn)
    chunk = 2 * LANES if (nk == 1 and epilogue is not _identity and tn % (2 * LANES) == 0) else tn
    n_acc = 1 + max(p[2] for p in products)
    na, nb, ne, no = len(a_list), len(b_list), len(extras), len(out_dtypes)

    if mode == "tn":
        a_spec = pl.BlockSpec((tk, tm), lambda i, j, k: (k, i))
    else:
        a_spec = pl.BlockSpec((tm, tk), lambda i, j, k: (i, k))
    if mode == "nt":
        b_spec = pl.BlockSpec((tn, tk), lambda i, j, k: (j0 + j, k))
    else:
        b_spec = pl.BlockSpec((tk, tn), lambda i, j, k: (k, j0 + j))
    t_spec = pl.BlockSpec((tm, tn), lambda i, j, k: (i, j))

    def body(*refs):
        a_refs = refs[:na]
        b_refs = refs[na:na + nb]
        e_refs = refs[na + nb:na + nb + ne]
        o_refs = refs[na + nb + ne:na + nb + ne + no]
        acc_refs = refs[na + nb + ne + no:]

        def partial_sums(cols):
            sums = [None] * n_acc
            for ai, bi, ci in products:
                b = b_refs[bi][cols, :] if mode == "nt" else b_refs[bi][:, cols]
                d = _dot(a_refs[ai][...], b, mode)
                sums[ci] = d if sums[ci] is None else sums[ci] + d
            return sums

        def finish(accs, cols):
            outs = epilogue(accs, [e[:, cols] for e in e_refs])
            for o_ref, o in zip(o_refs, outs):
                o_ref[:, cols] = o.astype(o_ref.dtype)

        if nk == 1:
            for c0 in range(0, tn, chunk):
                finish(partial_sums(slice(c0, c0 + chunk)), slice(c0, c0 + chunk))
        else:
            sums = partial_sums(slice(None))
            finish = functools.partial(finish, cols=slice(None))
            k = pl.program_id(2)

            @pl.when(k == 0)
            def _():
                for acc, s in zip(acc_refs, sums):
                    acc[...] = s

            @pl.when(k > 0)
            def _():
                for acc, s in zip(acc_refs, sums):
                    acc[...] += s

            @pl.when(k == nk - 1)
            def _():
                finish([acc[...] for acc in acc_refs])

    scratch = [pltpu.VMEM((tm, tn), F32) for _ in range(n_acc)] if nk > 1 else []
    outs, carried = _carried_call(
        body, carry,
        name=name,
        grid=(m_dim // tm, n_dim // tn, nk),
        in_specs=[a_spec] * na + [b_spec] * nb + [t_spec] * ne,
        out_specs=[t_spec] * no,
        out_shape=[jax.ShapeDtypeStruct((m_dim, n_dim), dt) for dt in out_dtypes],
        scratch_shapes=scratch,
        semantics=("parallel", "parallel", "arbitrary"),
        args=[*a_list, *b_list, *extras],
    )
    return (outs, carried) if carry else outs


def _identity(accs, extras):
    return accs


def _single(result, carry):
    return (result[0][0], result[1]) if carry else result[0]


def _mm_nn(name, a, b, out_dtype, res=None, alpha=1.0, carry=None, **tiles):
    if res is None:
        return _single(_matmul(name, "nn", [a], [b], [(0, 0, 0)], [out_dtype], _identity, carry=carry, **tiles), carry)

    def epilogue(accs, extras):
        return [extras[0] + alpha * accs[0]]

    return _single(_matmul(name, "nn", [a], [b], [(0, 0, 0)], [out_dtype], epilogue, extras=(res,), carry=carry,
                           **tiles), carry)


def _mm_nt_sum(name, a_list, b_list, out_dtype, carry=None, n_part=(0, 1), **tiles):
    products = [(i, i, 0) for i in range(len(a_list))]
    return _single(_matmul(name, "nt", a_list, b_list, products, [out_dtype], _identity, carry=carry, n_part=n_part,
                           **tiles), carry)


def _mm_tn(name, a, b_list, out_dtype, carry=None, **tiles):
    products = [(0, i, i) for i in range(len(b_list))]
    return _matmul(name, "tn", [a], b_list, products, [out_dtype] * len(b_list), _identity, carry=carry, **tiles)


def _rowwise(name, fn, row_ins, par_ins, row_outs, acc_outs=(), tr=512):
    first = row_ins[0][0] if isinstance(row_ins[0], tuple) else row_ins[0]
    t_dim = first.shape[0]
    tr = _tile(t_dim, tr, 16)
    arrays, specs = [], []
    for r in row_ins:
        if isinstance(r, tuple):
            arr, width, blk = r
            specs.append(pl.BlockSpec((tr, width), lambda i, blk=blk: (i, blk)))
        else:
            arr = r
            specs.append(pl.BlockSpec((tr, arr.shape[1]), lambda i: (i, 0)))
        arrays.append(arr)
    for p in par_ins:
        arrays.append(p)
        specs.append(pl.BlockSpec(p.shape, lambda i, nd=p.ndim: (0,) * nd))
    nr, npar, nro, nacc = len(row_ins), len(par_ins), len(row_outs), len(acc_outs)

    def body(*refs):
        rows = [r[...] for r in refs[:nr]]
        pars = [p[...] for p in refs[nr:nr + npar]]
        o_refs = refs[nr + npar:nr + npar + nro]
        acc_refs = refs[nr + npar + nro:]
        outs, accs = fn(rows, pars)
        for o_ref, o in zip(o_refs, outs):
            o_ref[...] = o.astype(o_ref.dtype)
        if nacc:
            @pl.when(pl.program_id(0) == 0)
            def _():
                for a_ref in acc_refs:
                    a_ref[...] = jnp.zeros_like(a_ref)

            for a_ref, a in zip(acc_refs, accs):
                a_ref[...] += a

    out_shape = [jax.ShapeDtypeStruct((t_dim, c), dt) for c, dt in row_outs]
    out_shape += [jax.ShapeDtypeStruct(s, F32) for s in acc_outs]
    out_specs = [pl.BlockSpec((tr, c), lambda i: (i, 0)) for c, _ in row_outs]
    out_specs += [pl.BlockSpec(s, lambda i: (0, 0)) for s in acc_outs]
    return pl.pallas_call(
        body,
        name=name,
        grid=(t_dim // tr,),
        in_specs=specs,
        out_specs=out_specs,
        out_shape=out_shape,
        compiler_params=_params(("arbitrary",)),
    )(*arrays)


def _rms_stats(x):
    r = lax.rsqrt(jnp.mean(x * x, axis=-1, keepdims=True) + EPS)
    return x * r, r


def _rms_backward(x, g, dy):
    xh, r = _rms_stats(x)
    a = dy * g
    dx = r * (a - xh * jnp.mean(a * xh, axis=-1, keepdims=True))
    return dx, jnp.sum(dy * xh, axis=0, keepdims=True)


def _rms_fwd(name, x, g):
    def fn(rows, pars):
        xh, _ = _rms_stats(rows[0])
        return [xh * pars[0]], []

    return _rowwise(name, fn, [x], [g], [(x.shape[1], BF16)])[0]


def _rms_bwd(name, x, g, dy_parts, dres, scale):
    def fn(rows, pars):
        dy = rows[2] if len(rows) == 3 else jnp.concatenate(rows[2:], axis=1)
        dx, dg = _rms_backward(rows[0], pars[0], dy)
        tot = rows[1] + dx
        return [tot, scale * tot], [dg]

    d = x.shape[1]
    return _rowwise(name, fn, [x, dres, *dy_parts], [g], [(d, F32), (d, BF16)], [(1, d)], tr=256)


def _cast_into_gathered(name, w, kind, q_arr):
    rs, cs = w.shape
    tr = _tile(rs, 256, 16)
    nr = rs // tr

    def body(q_ref, w_ref, o_ref):
        o_ref[...] = w_ref[...].astype(BF16)

    if kind == "row":
        o_spec = pl.BlockSpec((tr, cs), lambda i, q_ref: (q_ref[0] * nr + i, 0))
    else:
        o_spec = pl.BlockSpec((tr, cs), lambda i, q_ref: (i, q_ref[0]))
    return pl.pallas_call(
        body,
        name=name,
        grid_spec=pltpu.PrefetchScalarGridSpec(num_scalar_prefetch=1, grid=(nr,),
                                               in_specs=[pl.BlockSpec((tr, cs), lambda i, q_ref: (i, 0))],
                                               out_specs=o_spec),
        out_shape=jax.ShapeDtypeStruct(_full_shape(kind, (rs, cs)), BF16),
        compiler_params=_params(("parallel",)),
    )(q_arr, w)


def _ffn_gateup(tag, n, wg, wu, carry):
    def act(accs, extras):
        gate, up = accs
        return [gate * _sigmoid(gate) * up, gate, up]

    return _matmul(tag + "_gateup", "nn", [n], [wg, wu], [(0, 0, 0), (0, 1, 1)], [BF16] * 3, act,
                   tm=1024, tn=512, tk=2048, carry=carry)


def _ffn_up(tag, n, wu, gate, carry):
    def act(accs, extras):
        g = extras[0].astype(F32)
        return [g * _sigmoid(g) * accs[0], accs[0]]

    return _matmul(tag + "_up", "nn", [n], [wu], [(0, 0, 0)], [BF16] * 2, act, extras=(gate,),
                   tm=1024, tn=512, tk=2048, carry=carry)


def _ffn_down(tag, a, wd, h, carry=None):
    return _mm_nn(tag + "_down", a, wd, F32, res=h, alpha=0.5, tm=512, tn=512, tk=5632, carry=carry)


def _ffn_bwd(tag, names, h, g, wg, wu, wd, saved, dh, dfb, next_scale, reducer, riding=None, riding_dwd=None,
             last_hop_later=False):
    n, a, gate, up = saved

    def act_bwd(accs, extras):
        da = accs[0]
        gt, u = extras[0].astype(F32), extras[1].astype(F32)
        s = _sigmoid(gt)
        return [da * u * (s * (1.0 + gt * (1.0 - s))), da * (gt * s)]

    dact = _matmul(tag + "_dact", "nt", [dfb], [wd], [(0, 0, 0)], [BF16, BF16], act_bwd, extras=(gate, up),
                   tm=1024, tn=512, tk=2048, carry=riding)
    (dgp, du), rode = dact if riding else (dact, [])
    dwd_call = _mm_tn(tag + "_dwd", a, [dfb], BF16, tm=512, tn=2048, tk=2048, carry=riding_dwd)
    (dwd,), rode_dwd = dwd_call if riding_dwd else (dwd_call, [])
    down = reducer.swap(names[2:], [dwd])
    (dwg,), swapped = _mm_tn(tag + "_dwg", n, [dgp], BF16, tm=512, tn=2816, tk=2048, carry=down.carry)
    down = reducer.send(down, swapped)
    gate_w = reducer.swap(names[:1], [dwg])
    carry, split = _ride(down.carry, gate_w.carry)
    (dwu,), results = _mm_tn(tag + "_dwu", n, [du], BF16, tm=512, tn=2816, tk=2048, carry=carry)
    got, swapped = split(results)
    reducer.end(down, got)
    gate_w = reducer.send(gate_w, swapped)
    up_w = reducer.swap(names[1:2], [dwu])
    carry, split = _ride(gate_w.carry, up_w.carry)
    halves = 1 if last_hop_later else 2
    dn_lo, results = _mm_nt_sum(tag + "_dn_lo", [dgp, du], [wg, wu], F32, tm=1024, tn=1024, tk=1408, carry=carry,
                                n_part=(0, halves))
    got, swapped = split(results)
    reducer.end(gate_w, got)
    up_w = reducer.send(up_w, swapped)
    dn = [dn_lo]
    if not last_hop_later:
        dn_hi, got = _mm_nt_sum(tag + "_dn_hi", [dgp, du], [wg, wu], F32, tm=1024, tn=1024, tk=1408,
                                carry=up_w.carry, n_part=(1, 2))
        reducer.end(up_w, got)
        dn.append(dn_hi)
    dh_in, dh_in_b, dg = _rms_bwd(tag + "_dnorm", h, g, dn, dh, next_scale)
    return dh_in, dh_in_b, dg, rode, rode_dwd, (up_w if last_hop_later else None)


def _ssm_discretize(log_dt, a_re, a_im, b_re, b_im):
    dt = jnp.exp(log_dt)[:, None]
    lr = jnp.minimum(a_re, -1e-4)
    li = a_im
    mag = jnp.exp(lr * dt)
    ang = li * dt
    abar_r = mag * jnp.cos(ang)
    abar_i = mag * jnp.sin(ang)
    den = lr * lr + li * li
    xr = abar_r - 1.0
    xi = abar_i
    zr = (xr * lr + xi * li) / den
    zi = (xi * lr - xr * li) / den
    bbar_r = zr[..., None] * b_re - zi[..., None] * b_im
    bbar_i = zr[..., None] * b_im + zi[..., None] * b_re
    return abar_r, abar_i, bbar_r, bbar_i


def _blockdiag_in(b):
    g, n, p = b.shape
    nb = g // GROUPS_PER_BLOCK
    eye = jnp.eye(GROUPS_PER_BLOCK, dtype=b.dtype)
    b4 = b.reshape(nb, GROUPS_PER_BLOCK, n, p)
    return jnp.einsum("sgnp,gh->sgphn", b4, eye).reshape(nb, GROUPS_PER_BLOCK * p, GROUPS_PER_BLOCK * n)


def _blockdiag_in_grad(gw, n, p):
    nb = gw.shape[0]
    eye = jnp.eye(GROUPS_PER_BLOCK, dtype=gw.dtype)
    g5 = gw.reshape(nb, GROUPS_PER_BLOCK, p, GROUPS_PER_BLOCK, n)
    return jnp.einsum("sgphn,gh->sgnp", g5, eye).reshape(nb * GROUPS_PER_BLOCK, n, p)


def _blockdiag_out(c):
    g, p, n = c.shape
    nb = g // GROUPS_PER_BLOCK
    eye = jnp.eye(GROUPS_PER_BLOCK, dtype=c.dtype)
    c4 = c.reshape(nb, GROUPS_PER_BLOCK, p, n)
    return jnp.einsum("sgpn,gh->shngp", c4, eye).reshape(nb, GROUPS_PER_BLOCK * n, GROUPS_PER_BLOCK * p)


def _blockdiag_out_grad(gw, p, n):
    nb = gw.shape[0]
    eye = jnp.eye(GROUPS_PER_BLOCK, dtype=gw.dtype)
    g5 = gw.reshape(nb, GROUPS_PER_BLOCK, n, GROUPS_PER_BLOCK, p)
    return jnp.einsum("shngp,gh->sgpn", g5, eye).reshape(nb * GROUPS_PER_BLOCK, p, n)


def _ssm_scan_fwd(z, wb, wc, abar, d, carry=None):
    t_dim = z.shape[0]
    nb, cb, sw2 = wb.shape
    nl = sw2 // LANES
    hl = nl // 2
    tt = _tile(t_dim, 256, 8)
    nt = t_dim // tt

    def body(z_ref, wb_ref, wc_ref, a_ref, d_ref, s_ref, y_ref, yg_ref, drive_ref, st_ref):
        @pl.when(pl.program_id(0) == 0)
        def _():
            st_ref[...] = jnp.zeros_like(st_ref)

        u = z_ref[...]
        ub = u.astype(BF16)
        for b in range(nb):
            drive = _dot(ub[:, b * cb:(b + 1) * cb], wb_ref[b], "nn")
            for l in range(nl):
                drive_ref[l, pl.ds(b, tt, stride=nb), :] = drive[:, l * LANES:(l + 1) * LANES]
        a = a_ref[...]
        chunk = lambda v, l: v[:, l * LANES:(l + 1) * LANES]

        def step(t, state):
            rows = pl.ds(pl.multiple_of(t * nb, nb), nb)
            re, im = [], []
            for l in range(hl):
                ar, ai, sr, si = chunk(a, l), chunk(a, hl + l), state[l], state[hl + l]
                nr = ar * sr - ai * si + drive_ref[l, rows, :]
                ni = ar * si + ai * sr + drive_ref[hl + l, rows, :]
                s_ref[l, rows, :] = nr
                s_ref[hl + l, rows, :] = ni
                re.append(nr)
                im.append(ni)
            return tuple(re + im)

        state = lax.fori_loop(0, tt, step, tuple(st_ref[l] for l in range(nl)), unroll=8)
        for l in range(nl):
            st_ref[l] = state[l]
        parts = []
        for b in range(nb):
            s_b = jnp.concatenate([s_ref[l, pl.ds(b, tt, stride=nb), :] for l in range(nl)], axis=1)
            parts.append(_dot(s_b, wc_ref[b], "nn"))
        y = jnp.concatenate(parts, axis=1) + d_ref[...] * u
        y_ref[...] = y
        yg_ref[...] = _gelu(y).astype(BF16)

    full = lambda a: pl.BlockSpec(a.shape, lambda t, nd=a.ndim: (0,) * nd)
    return _carried_call(
        body, carry,
        name="ssm_scan_fwd",
        grid=(nt,),
        in_specs=[pl.BlockSpec((tt, nb * cb), lambda t: (t, 0)), full(wb), full(wc), full(abar), full(d)],
        out_specs=[
            pl.BlockSpec((nl, tt * nb, LANES), lambda t: (0, t, 0)),
            pl.BlockSpec((tt, nb * cb), lambda t: (t, 0)),
            pl.BlockSpec((tt, nb * cb), lambda t: (t, 0)),
        ],
        out_shape=[
            jax.ShapeDtypeStruct((nl, t_dim * nb, LANES), F32),
            jax.ShapeDtypeStruct((t_dim, nb * cb), F32),
            jax.ShapeDtypeStruct((t_dim, nb * cb), BF16),
        ],
        scratch_shapes=[pltpu.VMEM((nl, tt * nb, LANES), F32), pltpu.VMEM((nl, nb, LANES), F32)],
        semantics=("arbitrary",),
        args=[z, wb, wc, abar, d],
    )


def _ssm_scan_bwd(z, dy, states, wb, wc, abar_conj, d, carry=None):
    t_dim = z.shape[0]
    nb, cb, sw2 = wb.shape
    nl = sw2 // LANES
    hl = nl // 2
    tt = _tile(t_dim, 128, 8)
    nt = t_dim // tt
    edges = states.reshape(nl, nt, tt * nb, LANES)[:, :, (tt - 1) * nb:, :]
    before = jnp.concatenate([jnp.zeros((nl, 1, nb, LANES), F32), edges[:, :-1]], axis=1).reshape(nl, nt * nb, LANES)

    def body(z_ref, dy_ref, s_ref, sp_ref, wb_ref, wc_ref, a_ref, d_ref,
             dz_ref, gwb_ref, gwc_ref, ga_ref, gd_ref, gin_ref, gs_ref, st_ref):
        @pl.when(pl.program_id(0) == 0)
        def _():
            st_ref[...] = jnp.zeros_like(st_ref)
            gwb_ref[...] = jnp.zeros_like(gwb_ref)
            gwc_ref[...] = jnp.zeros_like(gwc_ref)
            ga_ref[...] = jnp.zeros_like(ga_ref)
            gd_ref[...] = jnp.zeros_like(gd_ref)

        u = z_ref[...]
        dyv = dy_ref[...]
        ub = u.astype(BF16)
        dyb = dyv.astype(BF16)
        for b in range(nb):
            gin = _dot(dyb[:, b * cb:(b + 1) * cb], wc_ref[b], "nt")
            for l in range(nl):
                gin_ref[l, pl.ds(b, tt, stride=nb), :] = gin[:, l * LANES:(l + 1) * LANES]
        a = a_ref[...]
        chunk = lambda v, l: v[:, l * LANES:(l + 1) * LANES]

        def step(k, state):
            rows = pl.ds(pl.multiple_of((tt - 1 - k) * nb, nb), nb)
            re, im = [], []
            for l in range(hl):
                ar, ai, gr, gi = chunk(a, l), chunk(a, hl + l), state[l], state[hl + l]
                nr = ar * gr - ai * gi + gin_ref[l, rows, :]
                ni = ar * gi + ai * gr + gin_ref[hl + l, rows, :]
                gs_ref[l, rows, :] = nr
                gs_ref[hl + l, rows, :] = ni
                re.append(nr)
                im.append(ni)
            return tuple(re + im)

        state = lax.fori_loop(0, tt, step, tuple(st_ref[l] for l in range(nl)), unroll=8)
        for l in range(nl):
            st_ref[l] = state[l]

        parts = []
        for b in range(nb):
            cols = slice(b * cb, (b + 1) * cb)
            gs_b = jnp.concatenate([gs_ref[l, pl.ds(b, tt, stride=nb), :] for l in range(nl)], axis=1)
            s_b = jnp.concatenate([s_ref[l, pl.ds(b, tt, stride=nb), :] for l in range(nl)], axis=1)
            parts.append(_dot(gs_b, wb_ref[b], "nt"))
            gwb_ref[b] += _dot(ub[:, cols], gs_b, "tn")
            gwc_ref[b] += _dot(s_b, dyb[:, cols], "tn")
        dz_ref[...] = (jnp.concatenate(parts, axis=1) + d_ref[...] * dyv).astype(BF16)
        gd_ref[...] += jnp.sum(dyv * u, axis=0, keepdims=True)

        row = lax.broadcasted_iota(jnp.int32, (tt * nb, LANES), 0)
        shifted = lambda v: jnp.where(row < nb, 0.0, pltpu.roll(v, nb, 0))
        over_time = lambda v: jnp.sum(v.reshape(tt, nb, LANES), axis=0)
        for l in range(hl):
            g_r, g_i = gs_ref[l], gs_ref[hl + l]
            p_r, p_i = shifted(s_ref[l]), shifted(s_ref[hl + l])
            f_r, f_i = sp_ref[l], sp_ref[hl + l]
            g0_r, g0_i = gs_ref[l, pl.ds(0, nb), :], gs_ref[hl + l, pl.ds(0, nb), :]
            ga_ref[l] += over_time(g_r * p_r + g_i * p_i) + g0_r * f_r + g0_i * f_i
            ga_ref[hl + l] += over_time(g_i * p_r - g_r * p_i) + g0_i * f_r - g0_r * f_i

    rev = lambda t: (nt - 1 - t, 0)
    rev3 = lambda t: (0, nt - 1 - t, 0)
    full = lambda a: pl.BlockSpec(a.shape, lambda t, nd=a.ndim: (0,) * nd)
    return _carried_call(
        body, carry,
        name="ssm_scan_bwd",
        grid=(nt,),
        in_specs=[
            pl.BlockSpec((tt, nb * cb), rev),
            pl.BlockSpec((tt, nb * cb), rev),
            pl.BlockSpec((nl, tt * nb, LANES), rev3),
            pl.BlockSpec((nl, nb, LANES), rev3),
            full(wb), full(wc), full(abar_conj), full(d),
        ],
        out_specs=[
            pl.BlockSpec((tt, nb * cb), rev),
            pl.BlockSpec((nb, cb, sw2), lambda t: (0, 0, 0)),
            pl.BlockSpec((nb, sw2, cb), lambda t: (0, 0, 0)),
            pl.BlockSpec((nl, nb, LANES), lambda t: (0, 0, 0)),
            pl.BlockSpec((1, nb * cb), lambda t: (0, 0)),
        ],
        out_shape=[
            jax.ShapeDtypeStruct((t_dim, nb * cb), BF16),
            jax.ShapeDtypeStruct((nb, cb, sw2), F32),
            jax.ShapeDtypeStruct((nb, sw2, cb), F32),
            jax.ShapeDtypeStruct((nl, nb, LANES), F32),
            jax.ShapeDtypeStruct((1, nb * cb), F32),
        ],
        scratch_shapes=[pltpu.VMEM((nl, tt * nb, LANES), F32), pltpu.VMEM((nl, tt * nb, LANES), F32),
                        pltpu.VMEM((nl, nb, LANES), F32)],
        semantics=("arbitrary",),
        args=[z, dy, states, before, wb, wc, abar_conj, d],
    )


def _gmlp_chunk(zu, zv, gv, wm_ref, bias, n_heads):
    ua = _gelu(zu)
    vg = _gelu(zv)
    xc = vg - jnp.mean(vg, axis=-1, keepdims=True)
    r = lax.rsqrt(jnp.mean(xc * xc, axis=-1, keepdims=True) + EPS)
    vh = xc * r
    vb = (vh * gv).astype(BF16)
    parts = []
    for h in range(n_heads):
        cols = slice(h * GMLP_HEAD, (h + 1) * GMLP_HEAD)
        parts.append(_dot(wm_ref[h], vb[:, cols], "nn"))
    s = jnp.concatenate(parts, axis=1) + bias
    return ua, vh, r, vb, s


def _gmlp_fwd(z, gv, wm, bias, ggo):
    t_dim = z.shape[0]
    dg = gv.shape[1]
    n_heads = dg // GMLP_HEAD
    tr = _tile(t_dim, 256, CHUNK)

    def body(zu_ref, zv_ref, gv_ref, wm_ref, b_ref, ggo_ref, o_ref):
        for ck in range(tr // CHUNK):
            rows = pl.ds(ck * CHUNK, CHUNK)
            ua, _, _, _, s = _gmlp_chunk(zu_ref[rows, :], zv_ref[rows, :], gv_ref[...], wm_ref, b_ref[...], n_heads)
            yh, _ = _rms_stats(ua * s)
            o_ref[rows, :] = (yh * ggo_ref[...]).astype(BF16)

    full = lambda a: pl.BlockSpec(a.shape, lambda i, nd=a.ndim: (0,) * nd)
    return pl.pallas_call(
        body,
        name="gmlp_fwd",
        grid=(t_dim // tr,),
        in_specs=[pl.BlockSpec((tr, dg), lambda i: (i, 1)), pl.BlockSpec((tr, dg), lambda i: (i, 2)),
                  full(gv), full(wm), full(bias), full(ggo)],
        out_specs=pl.BlockSpec((tr, dg), lambda i: (i, 0)),
        out_shape=jax.ShapeDtypeStruct((t_dim, dg), BF16),
        compiler_params=_params(("parallel",)),
    )(z, z, gv, wm, bias, ggo)


def _gmlp_bwd(z, dycat, gv, wm, bias, ggo):
    t_dim = z.shape[0]
    dg = gv.shape[1]
    n_heads = dg // GMLP_HEAD
    tr = _tile(t_dim, 256, CHUNK)

    def body(zu_ref, zv_ref, dy_ref, gv_ref, wm_ref, b_ref, ggo_ref,
             dzu_ref, dzv_ref, dggo_ref, dgv_ref, dwm_ref, dsum_ref):
        @pl.when(pl.program_id(0) == 0)
        def _():
            dggo_ref[...] = jnp.zeros_like(dggo_ref)
            dgv_ref[...] = jnp.zeros_like(dgv_ref)
            dwm_ref[...] = jnp.zeros_like(dwm_ref)
            dsum_ref[...] = jnp.zeros_like(dsum_ref)

        for ck in range(tr // CHUNK):
            rows = pl.ds(ck * CHUNK, CHUNK)
            zu = zu_ref[rows, :]
            zv = zv_ref[rows, :]
            gvv = gv_ref[...]
            ua, vh, r, vb, s = _gmlp_chunk(zu, zv, gvv, wm_ref, b_ref[...], n_heads)
            dy, dggo = _rms_backward(ua * s, ggo_ref[...], dy_ref[rows, :])
            dggo_ref[...] += dggo
            ds = dy * ua
            dsum_ref[...] += ds
            dsb = ds.astype(BF16)
            parts = []
            for h in range(n_heads):
                cols = slice(h * GMLP_HEAD, (h + 1) * GMLP_HEAD)
                dwm_ref[h] += _dot(dsb[:, cols], vb[:, cols], "nt")
                parts.append(_dot(wm_ref[h], dsb[:, cols], "tn"))
            dv = jnp.concatenate(parts, axis=1)
            dgv_ref[...] += jnp.sum(dv * vh, axis=0, keepdims=True)
            dvh = dv * gvv
            dvg = r * (dvh - jnp.mean(dvh, axis=-1, keepdims=True) - vh * jnp.mean(dvh * vh, axis=-1, keepdims=True))
            dzv_ref[rows, :] = (dvg * _gelu_grad(zv)).astype(BF16)
            dzu_ref[rows, :] = (dy * s * _gelu_grad(zu)).astype(BF16)

    full = lambda a: pl.BlockSpec(a.shape, lambda i, nd=a.ndim: (0,) * nd)
    return pl.pallas_call(
        body,
        name="gmlp_bwd",
        grid=(t_dim // tr,),
        in_specs=[pl.BlockSpec((tr, dg), lambda i: (i, 1)), pl.BlockSpec((tr, dg), lambda i: (i, 2)),
                  pl.BlockSpec((tr, dg), lambda i: (i, 1)), full(gv), full(wm), full(bias), full(ggo)],
        out_specs=[pl.BlockSpec((tr, dg), lambda i: (i, 0)), pl.BlockSpec((tr, dg), lambda i: (i, 0)),
                   pl.BlockSpec((1, dg), lambda i: (0, 0)), pl.BlockSpec((1, dg), lambda i: (0, 0)),
                   pl.BlockSpec(wm.shape, lambda i: (0, 0, 0)), pl.BlockSpec((CHUNK, dg), lambda i: (0, 0))],
        out_shape=[jax.ShapeDtypeStruct((t_dim, dg), BF16), jax.ShapeDtypeStruct((t_dim, dg), BF16),
                   jax.ShapeDtypeStruct((1, dg), F32), jax.ShapeDtypeStruct((1, dg), F32),
                   jax.ShapeDtypeStruct(wm.shape, F32), jax.ShapeDtypeStruct((CHUNK, dg), F32)],
        compiler_params=_params(("arbitrary",)),
    )(z, z, dycat, gv, wm, bias, ggo)


def _ple_head(npl, w_gate, h3, pp, tgt, g_final):
    t_dim, d = h3.shape
    tr = _tile(t_dim, 256, 16)

    def body(n_ref, w_ref, h_ref, pp_ref, t_ref, g_ref, dgq_ref, dpp_ref, dh_ref, dg_ref, loss_ref):
        @pl.when(pl.program_id(0) == 0)
        def _():
            dg_ref[...] = jnp.zeros_like(dg_ref)
            loss_ref[...] = jnp.zeros_like(loss_ref)

        gate = _sigmoid(_dot(n_ref[...], w_ref[...], "nn"))
        ppv = pp_ref[...]
        h4 = h_ref[...] + gate * ppv
        xh, _ = _rms_stats(h4)
        err = xh * g_ref[...] - t_ref[...]
        dh4, dg = _rms_backward(h4, g_ref[...], err * (1.0 / d))
        dh_ref[...] = dh4
        dgq_ref[...] = (dh4 * ppv * gate * (1.0 - gate)).astype(BF16)
        dpp_ref[...] = (dh4 * gate).astype(BF16)
        dg_ref[...] += dg
        loss_ref[...] += jnp.full((1, LANES), 0.5 * jnp.sum(err * err) * (1.0 / d), F32)

    rows = pl.BlockSpec((tr, d), lambda i: (i, 0))
    whole = lambda a: pl.BlockSpec(a.shape, lambda i: (0, 0))
    return pl.pallas_call(
        body,
        name="ple_head",
        grid=(t_dim // tr,),
        in_specs=[rows, whole(w_gate), rows, rows, rows, whole(g_final)],
        out_specs=[rows, rows, rows, pl.BlockSpec((1, d), lambda i: (0, 0)), pl.BlockSpec((1, LANES), lambda i: (0, 0))],
        out_shape=[jax.ShapeDtypeStruct((t_dim, d), BF16), jax.ShapeDtypeStruct((t_dim, d), BF16),
                   jax.ShapeDtypeStruct((t_dim, d), F32), jax.ShapeDtypeStruct((1, d), F32),
                   jax.ShapeDtypeStruct((1, LANES), F32)],
        compiler_params=_params(("arbitrary",)),
    )(npl, w_gate, h3, pp, tgt, g_final)


def _position():
    x, y, c = lax.axis_index("x"), lax.axis_index("y"), lax.axis_index("c")
    chips = [(1 - x, y), (x, 1 - y), (1 - x, 1 - y)]
    return x, y, c, chips


def _region(ref, kind, shard_shape, q, half):
    rs, cs = shard_shape
    r0, nr = (0, rs) if half is None else (half * (rs // 2), rs // 2)
    if kind == "row":
        return ref.at[pl.ds(q * rs + r0, nr), :]
    return ref.at[pl.ds(r0, nr), pl.ds(q * cs, cs)]


def _full_shape(kind, shard_shape):
    rs, cs = shard_shape
    return (N_CHIPS * rs, cs) if kind == "row" else (rs, N_CHIPS * cs)


def _remote(src, dst, send_sems, recv_sems, k, to):
    return pltpu.make_async_remote_copy(src_ref=src, dst_ref=dst, send_sem=send_sems.at[k], recv_sem=recv_sems.at[k],
                                        device_id=to, device_id_type=MESH)


def _same(arrays):
    return [jax.ShapeDtypeStruct(a.shape, a.dtype) for a in arrays]


def _gather_ici_carry(gathered, kinds, shapes):
    nw = len(gathered)

    def copies(ops, full, send_sems, recv_sems):
        x, y, c, chips = _position()
        out = []
        for w in range(nw):
            mine = _region(full[w], kinds[w], shapes[w], 2 * x + y, c)
            for j, (cx, cy) in enumerate(chips):
                out.append(_remote(mine, mine, send_sems, recv_sems, 3 * w + j, (cx, cy, c)))
        return out

    return _Carry(gathered, _same(gathered), {i: i for i in range(nw)}, 3 * nw, copies)


def _gather_d2d_carry(gathered, kinds, shapes):
    nw = len(gathered)

    def copies(ops, full, send_sems, recv_sems):
        x, y, c, chips = _position()
        out = []
        for w in range(nw):
            for j, (cx, cy) in enumerate(chips):
                landed = _region(full[w], kinds[w], shapes[w], 2 * cx + cy, c)
                out.append(_remote(landed, landed, send_sems, recv_sems, 3 * w + j, (x, y, 1 - c)))
        return out

    return _Carry(gathered, _same(gathered), {i: i for i in range(nw)}, 3 * nw, copies)


def _pairs_carry(grads, kinds, shapes):
    nw = len(grads)

    def copies(g, got, send_sems, recv_sems):
        x, y, c, _ = _position()
        out = []
        for w in range(nw):
            for q in range(N_CHIPS):
                out.append(_remote(_region(g[w], kinds[w], shapes[w], q, 1 - c), got[w].at[q], send_sems, recv_sems,
                                   N_CHIPS * w + q, (x, y, 1 - c)))
        return out

    outs = [jax.ShapeDtypeStruct((N_CHIPS, s[0] // 2, s[1]), BF16) for s in shapes]
    return _Carry(grads, outs, {}, N_CHIPS * nw, copies)


def _pair_sum(name, grad, got, kind, shard_shape, c_arr):
    rs, cs = shard_shape
    hr = rs // 2
    tr = _tile(hr, 512, 16)
    nr = hr // tr

    def body(c_ref, g_ref, s_ref, o_ref):
        o_ref[...] = (g_ref[...].astype(F32) + s_ref[...].astype(F32)).astype(BF16)

    if kind == "row":
        g_spec = pl.BlockSpec((tr, cs), lambda q, i, c_ref: (q * (rs // tr) + c_ref[0] * nr + i, 0))
    else:
        g_spec = pl.BlockSpec((tr, cs), lambda q, i, c_ref: (c_ref[0] * nr + i, q))
    blk = pl.BlockSpec((None, tr, cs), lambda q, i, c_ref: (q, i, 0))
    return pl.pallas_call(
        body,
        name=name,
        grid_spec=pltpu.PrefetchScalarGridSpec(num_scalar_prefetch=1, grid=(N_CHIPS, nr), in_specs=[g_spec, blk],
                                               out_specs=blk),
        out_shape=jax.ShapeDtypeStruct((N_CHIPS, hr, cs), BF16),
        compiler_params=_params(("parallel", "parallel")),
    )(c_arr, grad, got)


def _scatter_carry(sums, shapes):
    nw = len(sums)

    def copies(ps, got, send_sems, recv_sems):
        x, y, c, chips = _position()
        out = []
        for w in range(nw):
            for j, (cx, cy) in enumerate(chips):
                out.append(_remote(ps[w].at[2 * cx + cy], got[w].at[j], send_sems, recv_sems, 3 * w + j, (cx, cy, c)))
        return out

    outs = [jax.ShapeDtypeStruct((3, s[0] // 2, s[1]), BF16) for s in shapes]
    return _Carry(sums, outs, {}, 3 * nw, copies)


def _owner_sum(name, sums, got, shard_shape, qc_arr):
    rs, cs = shard_shape
    hr = rs // 2
    tr = _tile(hr, 512, 16)
    nr = hr // tr

    def body(qc_ref, mine_ref, got_ref, o_ref):
        acc = mine_ref[...].astype(F32)
        for j in range(3):
            acc = acc + got_ref[j].astype(F32)
        o_ref[...] = acc

    return pl.pallas_call(
        body,
        name=name,
        grid_spec=pltpu.PrefetchScalarGridSpec(
            num_scalar_prefetch=1, grid=(nr,),
            in_specs=[pl.BlockSpec((None, tr, cs), lambda i, qc_ref: (qc_ref[0], i, 0)),
                      pl.BlockSpec((3, tr, cs), lambda i, qc_ref: (0, i, 0))],
            out_specs=pl.BlockSpec((tr, cs), lambda i, qc_ref: (qc_ref[1] * nr + i, 0))),
        out_shape=jax.ShapeDtypeStruct((rs, cs), F32),
        compiler_params=_params(("parallel",)),
    )(qc_arr, sums, got)


def _share_carry(grads, shapes):
    nw = len(grads)

    def copies(ops, out, send_sems, recv_sems):
        x, y, c, _ = _position()
        res = []
        for w in range(nw):
            hr = shapes[w][0] // 2
            mine = out[w].at[pl.ds(c * hr, hr), :]
            res.append(_remote(mine, mine, send_sems, recv_sems, w, (x, y, 1 - c)))
        return res

    return _Carry(grads, _same(grads), {i: i for i in range(nw)}, nw, copies)


def _place_block(packed, me):
    return lax.dynamic_update_slice(jnp.zeros((N_DEV,) + packed.shape, F32), packed[None], (me, 0, 0))


def _exchange_carry(blocks):
    def copies(ops, res, send_sems, recv_sems):
        x, y, c, _ = _position()
        mine = res[0].at[4 * x + 2 * y + c]
        out = []
        for k in range(1, N_DEV):
            to = ((1 - x) if k & 4 else x, (1 - y) if k & 2 else y, (1 - c) if k & 1 else c)
            out.append(_remote(mine, mine, send_sems, recv_sems, k - 1, to))
        return out

    return _Carry([blocks], _same([blocks]), {0: 0}, N_DEV - 1, copies)


def _sum_blocks(name, blocks):
    n, rows, lanes = blocks.shape
    tr = _tile(rows, 512, 8)

    def body(b_ref, o_ref):
        acc = b_ref[0]
        for k in range(1, n):
            acc = acc + b_ref[k]
        o_ref[...] = acc

    return pl.pallas_call(
        body,
        name=name,
        grid=(rows // tr,),
        in_specs=[pl.BlockSpec((n, tr, lanes), lambda i: (0, i, 0))],
        out_specs=pl.BlockSpec((tr, lanes), lambda i: (i, 0)),
        out_shape=jax.ShapeDtypeStruct((rows, lanes), F32),
        compiler_params=_params(("parallel",)),
    )(blocks)


def _adamw(name, w, g, m, v):
    def fn(rows, pars):
        wv, gv, mv, vv = rows
        m_new = ADAM_B1 * mv + (1.0 - ADAM_B1) * gv
        v_new = ADAM_B2 * vv + (1.0 - ADAM_B2) * (gv * gv)
        m_hat = m_new / (1.0 - ADAM_B1 ** ADAM_STEP)
        v_hat = v_new / (1.0 - ADAM_B2 ** ADAM_STEP)
        delta = -ADAM_LR * (m_hat / (jnp.sqrt(v_hat) + ADAM_EPS) + ADAM_WD * wv)
        return [delta, m_new, v_new, gv], []

    c = w.shape[1]
    return _rowwise(name, fn, [w, g, m, v], [], [(c, F32)] * 4, tr=256)


def _pack(arrays):
    rows = []
    for a in arrays:
        flat = a.reshape(-1).astype(F32)
        pad = (-flat.shape[0]) % LANES
        rows.append(jnp.pad(flat, (0, pad)).reshape(-1, LANES))
    stacked = jnp.concatenate(rows, axis=0)
    pad_rows = (-stacked.shape[0]) % 8
    return jnp.pad(stacked, ((0, pad_rows), (0, 0)))


def _unpack(packed, shapes):
    out, r = [], 0
    for s in shapes:
        n = math.prod(s)
        nr = -(-n // LANES)
        out.append(packed[r:r + nr].reshape(-1)[:n].reshape(s))
        r += nr
    return out


BIG = ["w1_gate", "w1_up", "w1_down", "w_in", "ssm_w_glu", "w_out", "w2_gate", "w2_up", "w2_down", "w_ple_gate",
       "w_ple_proj"]
KIND = {"w1_gate": "col", "w1_up": "col", "w1_down": "row", "w_in": "col", "ssm_w_glu": "row", "w_out": "row",
        "w2_gate": "col", "w2_up": "col", "w2_down": "row", "w_ple_gate": "row", "w_ple_proj": "col"}
SMALL = ["norm_ffn1", "norm_mix", "ssm_log_dt", "ssm_a_re", "ssm_a_im", "ssm_b_re", "ssm_b_im", "ssm_c_re", "ssm_c_im",
         "ssm_d", "gmlp_norm_v", "gmlp_w_s", "gmlp_b_s", "norm_ssm_out", "norm_gmlp_out", "norm_ffn2", "norm_ple",
         "norm_final"]
WEIGHTS = ["norm_ffn1", "w1_gate", "w1_up", "w1_down", "norm_mix", "w_in", "ssm_log_dt", "ssm_a_re", "ssm_a_im",
           "ssm_b_re", "ssm_b_im", "ssm_c_re", "ssm_c_im", "ssm_d", "ssm_w_glu", "gmlp_norm_v", "gmlp_w_s", "gmlp_b_s",
           "norm_ssm_out", "norm_gmlp_out", "w_out", "norm_ffn2", "w2_gate", "w2_up", "w2_down", "norm_ple",
           "w_ple_gate", "w_ple_proj", "norm_final"]


class _Trip:
    def __init__(self, names, arrays, carry):
        self.names, self.arrays, self.carry = names, arrays, carry


class _Reducer:
    def __init__(self, shard_shape, c_arr, qc_arr):
        self.shard_shape, self.c_arr, self.qc_arr = shard_shape, c_arr, qc_arr
        self.halves = {}

    def swap(self, names, grads):
        kinds = [KIND[n] for n in names]
        shapes = [self.shard_shape[n] for n in names]
        return _Trip(names, grads, _pairs_carry(grads, kinds, shapes))

    def send(self, trip, swapped):
        shapes = [self.shard_shape[n] for n in trip.names]
        sums = [_pair_sum("pair_sum_" + n, g, s, KIND[n], sh, self.c_arr)
                for n, g, s, sh in zip(trip.names, trip.arrays, swapped, shapes)]
        return _Trip(trip.names, sums, _scatter_carry(sums, shapes))

    def end(self, trip, got):
        for n, ps, g in zip(trip.names, trip.arrays, got):
            self.halves[n] = _owner_sum("owner_sum_" + n, ps, g, self.shard_shape[n], self.qc_arr)


def _ride(*carries):
    present = [c for c in carries if c is not None]
    joined = functools.reduce(_join, present) if present else None

    def split(results):
        out, at = [], 0
        for c in carries:
            n = len(c.out_shapes) if c is not None else 0
            out.append(list(results[at:at + n]))
            at += n
        return out

    return joined, split


def _step(x, p, tgt, w, m, v):
    d_model = x.shape[1]
    d_ssm = w["ssm_d"].shape[1]
    n_groups = d_ssm // SSM_GROUP
    row = lambda a: a.reshape(1, -1)

    xi, yi, ci = lax.axis_index("x"), lax.axis_index("y"), lax.axis_index("c")
    c_arr = jnp.reshape(ci, (1,)).astype(jnp.int32)
    q_arr = jnp.reshape(2 * xi + yi, (1,)).astype(jnp.int32)
    qc_arr = jnp.stack([2 * xi + yi, ci]).astype(jnp.int32)
    shard_shape = {n: w[n].shape for n in BIG}
    full = {n: _cast_into_gathered("cast_" + n, w[n], KIND[n], q_arr) for n in BIG}

    def gather(stage, names):
        return stage([full[n] for n in names], [KIND[n] for n in names], [shard_shape[n] for n in names])

    def gathered(names, arrays):
        full.update(zip(names, arrays))

    zeroth, first, second, third = ["w1_gate"], ["w1_up"], ["w1_down", "w_in"], ["w2_gate"]
    fourth, fifth, sixth = ["ssm_w_glu", "w_out"], ["w2_up"], ["w2_down", "w_ple_gate", "w_ple_proj"]

    def two_stages(passed_on, landing):
        return _join(gather(_gather_d2d_carry, passed_on), gather(_gather_ici_carry, landing)), passed_on + landing

    gathered(zeroth, _comm_call("gather_zeroth_ici", gather(_gather_ici_carry, zeroth)))
    gathered(zeroth, _comm_call("gather_zeroth_d2d", gather(_gather_d2d_carry, zeroth)))
    n1 = _rms_fwd("ffn1_norm", x, w["norm_ffn1"])
    gate1, landed = _mm_nn("ffn1_gate", n1, full["w1_gate"], BF16, tm=1024, tn=512, tk=2048,
                           carry=gather(_gather_ici_carry, first))
    gathered(first, landed)
    gathered(first, _comm_call("gather_first_d2d", gather(_gather_d2d_carry, first)))
    (a1, up1), landed = _ffn_up("ffn1", n1, full["w1_up"], gate1, gather(_gather_ici_carry, second))
    gathered(second, landed)
    gathered(second, _comm_call("gather_second_d2d", gather(_gather_d2d_carry, second)))
    h1, landed = _ffn_down("ffn1", a1, full["w1_down"], x, gather(_gather_ici_carry, third))
    gathered(third, landed)
    ffn1 = (n1, a1, gate1, up1)
    nm = _rms_fwd("mix_norm", h1, w["norm_mix"])
    carry, names = two_stages(third, fourth)
    z, landed = _mm_nn("in_proj", nm, full["w_in"], F32, tm=1024, tn=512, tk=2048, carry=carry)
    gathered(names, landed)

    disc, disc_vjp = jax.vjp(_ssm_discretize, w["ssm_log_dt"][0], w["ssm_a_re"], w["ssm_a_im"], w["ssm_b_re"],
                             w["ssm_b_im"])
    abar_r, abar_i, bbar_r, bbar_i = disc
    nb = n_groups // GROUPS_PER_BLOCK
    wb = jnp.concatenate([_blockdiag_in(bbar_r), _blockdiag_in(bbar_i)], axis=-1).astype(BF16)
    wc = jnp.concatenate([_blockdiag_out(w["ssm_c_re"]), -_blockdiag_out(w["ssm_c_im"])], axis=1).astype(BF16)
    abar = jnp.concatenate([abar_r.reshape(nb, -1), abar_i.reshape(nb, -1)], axis=-1)
    abar_conj = jnp.concatenate([abar_r.reshape(nb, -1), -abar_i.reshape(nb, -1)], axis=-1)
    carry, names = two_stages(fourth, fifth)
    (states, y_pre, yg), landed = _ssm_scan_fwd(z, wb, wc, abar, w["ssm_d"], carry)
    gathered(names, landed)
    q = _mm_nn("glu_proj", yg, full["ssm_w_glu"], F32, tm=1024, tn=1024, tk=1024)

    def glu_norm(rows, pars):
        yv = _gelu(rows[0]) * _sigmoid(rows[1])
        yh, _ = _rms_stats(yv)
        return [yh * pars[0]], []

    yn_ssm = _rowwise("ssm_glu_norm", glu_norm, [y_pre, q], [w["norm_ssm_out"]], [(d_ssm, BF16)])[0]

    tril = jnp.tril(jnp.ones((CHUNK, CHUNK), dtype=bool))
    wm = jnp.where(tril[None], w["gmlp_w_s"], 0.0).astype(BF16)
    bias = jnp.repeat(w["gmlp_b_s"].T, GMLP_HEAD, axis=1)
    yn_gmlp = _gmlp_fwd(z, w["gmlp_norm_v"], wm, bias, w["norm_gmlp_out"])
    ycat = jnp.concatenate([yn_ssm, yn_gmlp], axis=1)
    h2, landed = _mm_nn("out_proj", ycat, full["w_out"], F32, res=h1, alpha=1.0, tm=512, tn=1024, tk=2048,
                        carry=gather(_gather_d2d_carry, fifth))
    gathered(fifth, landed)

    n2 = _rms_fwd("ffn2_norm", h2, w["norm_ffn2"])
    (a2, gate2, up2), landed = _ffn_gateup("ffn2", n2, full["w2_gate"], full["w2_up"],
                                           gather(_gather_ici_carry, sixth))
    gathered(sixth, landed)
    gathered(sixth, _comm_call("gather_sixth_d2d", gather(_gather_d2d_carry, sixth)))
    h3 = _ffn_down("ffn2", a2, full["w2_down"], h2)
    ffn2 = (n2, a2, gate2, up2)
    npl = _rms_fwd("ple_norm", h3, w["norm_ple"])
    pp = _mm_nn("ple_proj", p, full["w_ple_proj"], F32, tm=1024, tn=1024, tk=2048)
    dgq, dpp, dh4, g_norm_final, loss_part = _ple_head(npl, full["w_ple_gate"], h3, pp, tgt, row(w["norm_final"]))
    reducer = _Reducer(shard_shape, c_arr, qc_arr)
    (g_w_ple_proj,) = _mm_tn("ple_dwproj", p, [dpp], BF16, tm=256, tn=1024, tk=1024)
    (g_w_ple_gate,) = _mm_tn("ple_dwgate", npl, [dgq], BF16, tm=1024, tn=1024, tk=1024)
    ple = reducer.swap(["w_ple_gate", "w_ple_proj"], [g_w_ple_gate, g_w_ple_proj])
    dnpl, swapped = _mm_nt_sum("ple_dnorm_in", [dgq], [full["w_ple_gate"]], F32, tm=512, tn=1024, tk=2048,
                               carry=ple.carry)
    ple = reducer.send(ple, swapped)
    dh3, dh3_b, g_norm_ple = _rms_bwd("ple_dnorm", h3, w["norm_ple"], [dnpl], dh4, 0.5)

    dh2, dh2_b, g_norm_ffn2, got, _, up2_w = _ffn_bwd(
        "ffn2", ["w2_gate", "w2_up", "w2_down"], h2, w["norm_ffn2"], full["w2_gate"], full["w2_up"], full["w2_down"],
        ffn2, dh3, dh3_b, 1.0, reducer, riding=ple.carry, last_hop_later=True)
    reducer.end(ple, got)

    dycat = _mm_nt_sum("out_dproj", [dh2_b], [full["w_out"]], F32, tm=512, tn=1024, tk=2048)
    (g_w_out,) = _mm_tn("out_dw", ycat, [dh2_b], BF16, tm=1024, tn=1024, tk=1024)

    dzu, dzv, g_norm_gmlp_out, g_gmlp_norm_v, g_wm, g_s = _gmlp_bwd(z, dycat, w["gmlp_norm_v"], wm, bias,
                                                                  w["norm_gmlp_out"])
    g_gmlp_w_s = jnp.where(tril[None], g_wm, 0.0)
    g_gmlp_b_s = g_s.reshape(CHUNK, -1, GMLP_HEAD).sum(axis=-1).T

    def glu_bwd(rows, pars):
        dyn, ypre, qv = rows
        ygv = _gelu(ypre)
        sg = _sigmoid(qv)
        dy, dg = _rms_backward(ygv * sg, pars[0], dyn)
        return [dy * ygv * sg * (1.0 - sg), dy * sg], [dg]

    dq, dyg_part, g_norm_ssm_out = _rowwise("ssm_dglu", glu_bwd, [(dycat, d_ssm, 0), y_pre, q], [w["norm_ssm_out"]],
                                            [(d_ssm, BF16), (d_ssm, F32)], [(1, d_ssm)])
    dyg_proj = _mm_nt_sum("glu_dproj", [dq], [full["ssm_w_glu"]], F32, tm=1024, tn=1024, tk=1024)
    (g_ssm_w_glu,) = _mm_tn("glu_dw", yg, [dq], BF16, tm=1024, tn=1024, tk=1024)

    def gelu_bwd(rows, pars):
        return [(rows[0] + rows[1]) * _gelu_grad(rows[2])], []

    dy_pre = _rowwise("ssm_dgelu", gelu_bwd, [dyg_part, dyg_proj, y_pre], [], [(d_ssm, F32)])[0]
    mixers = reducer.swap(["w_out", "ssm_w_glu"], [g_w_out, g_ssm_w_glu])
    carry, split = _ride(up2_w.carry, mixers.carry)
    (dz_ssm, g_wb, g_wc, g_abar, g_ssm_d), results = _ssm_scan_bwd(z, dy_pre, states, wb, wc, abar_conj, w["ssm_d"],
                                                                  carry)
    got, swapped = split(results)
    reducer.end(up2_w, got)
    mixers = reducer.send(mixers, swapped)
    g_abar = jnp.transpose(g_abar, (1, 0, 2)).reshape(nb, -1)
    sw = g_abar.shape[-1] // 2
    g_bbar_r = _blockdiag_in_grad(g_wb[..., :sw], SSM_STATE, SSM_GROUP)
    g_bbar_i = _blockdiag_in_grad(g_wb[..., sw:], SSM_STATE, SSM_GROUP)
    g_ssm_c_re = _blockdiag_out_grad(g_wc[:, :sw, :], SSM_GROUP, SSM_STATE)
    g_ssm_c_im = -_blockdiag_out_grad(g_wc[:, sw:, :], SSM_GROUP, SSM_STATE)
    g_abar_r = g_abar[..., :sw].reshape(n_groups, SSM_STATE)
    g_abar_i = g_abar[..., sw:].reshape(n_groups, SSM_STATE)
    g_ssm_log_dt, g_ssm_a_re, g_ssm_a_im, g_ssm_b_re, g_ssm_b_im = disc_vjp((g_abar_r, g_abar_i, g_bbar_r, g_bbar_i))

    dz = jnp.concatenate([dz_ssm, dzu, dzv], axis=1)
    (g_w_in,), got = _mm_tn("in_dw", nm, [dz], BF16, tm=1024, tn=768, tk=1024, carry=mixers.carry)
    reducer.end(mixers, got)
    in_w = reducer.swap(["w_in"], [g_w_in])
    dnm, swapped = _mm_nt_sum("in_dproj", [dz], [full["w_in"]], F32, tm=512, tn=1024, tk=1024, carry=in_w.carry)
    in_w = reducer.send(in_w, swapped)
    dh1, dh1_b, g_norm_mix = _rms_bwd("mix_dnorm", h1, w["norm_mix"], [dnm], dh2, 0.5)
    small = {"norm_mix": g_norm_mix, "ssm_log_dt": g_ssm_log_dt, "ssm_a_re": g_ssm_a_re,
             "ssm_a_im": g_ssm_a_im, "ssm_b_re": g_ssm_b_re, "ssm_b_im": g_ssm_b_im, "ssm_c_re": g_ssm_c_re,
             "ssm_c_im": g_ssm_c_im, "ssm_d": g_ssm_d, "gmlp_norm_v": g_gmlp_norm_v, "gmlp_w_s": g_gmlp_w_s,
             "gmlp_b_s": g_gmlp_b_s, "norm_ssm_out": g_norm_ssm_out, "norm_gmlp_out": g_norm_gmlp_out,
             "norm_ffn2": g_norm_ffn2, "norm_ple": g_norm_ple, "norm_final": g_norm_final}
    early = [n for n in SMALL if n != "norm_ffn1"]
    me = 4 * xi + 2 * yi + ci
    early_blocks = _place_block(_pack([small[n] for n in early] + [loss_part[:, :1]]), me)
    dx, _, g_norm_ffn1, got, (early_blocks,), _ = _ffn_bwd(
        "ffn1", ["w1_gate", "w1_up", "w1_down"], x, w["norm_ffn1"], full["w1_gate"], full["w1_up"], full["w1_down"],
        ffn1, dh1, dh1_b, 1.0, reducer, riding=in_w.carry, riding_dwd=_exchange_carry(early_blocks))
    reducer.end(in_w, got)
    early_grads = _unpack(_sum_blocks("sum_small", early_blocks), [w[n].shape for n in early] + [(1,)])
    loss = early_grads[-1].reshape(())

    halves = [reducer.halves[n] for n in BIG]
    late_blocks = _place_block(_pack([g_norm_ffn1]), me)
    last = _join(_share_carry(halves, [shard_shape[n] for n in BIG]), _exchange_carry(late_blocks))
    *shared, late_blocks = _comm_call("share_halves", last)
    grad = dict(zip(BIG, shared))
    grad.update(zip(early, early_grads[:-1]))
    grad["norm_ffn1"] = _unpack(_sum_blocks("sum_first_norm", late_blocks), [w["norm_ffn1"].shape])[0]

    small_shapes = [w[n].shape for n in SMALL]
    delta, new_m, new_v = {}, {}, {}
    for n in BIG:
        delta[n], new_m[n], new_v[n], grad[n] = _adamw("adamw_" + n, w[n], grad[n], m[n], v[n])
    d_p, m_p, v_p, _ = _adamw("adamw_small", _pack([w[n] for n in SMALL]), _pack([grad[n] for n in SMALL]),
                              _pack([m[n] for n in SMALL]), _pack([v[n] for n in SMALL]))
    for name_list, packed in ((delta, d_p), (new_m, m_p), (new_v, v_p)):
        for n, a in zip(SMALL, _unpack(packed, small_shapes)):
            name_list[n] = a
    return loss, dx, grad, delta, new_m, new_v


def kernel(x, p, norm_ffn1, w1_gate, w1_up, w1_down, norm_mix, w_in, ssm_log_dt, ssm_a_re, ssm_a_im, ssm_b_re, ssm_b_im, ssm_c_re, ssm_c_im, ssm_d, ssm_w_glu, gmlp_norm_v, gmlp_w_s, gmlp_b_s, norm_ssm_out, norm_gmlp_out, w_out, norm_ffn2, w2_gate, w2_up, w2_down, norm_ple, w_ple_gate, w_ple_proj, norm_final, loss_target, m_norm_ffn1, m_w1_gate, m_w1_up, m_w1_down, m_norm_mix, m_w_in, m_ssm_log_dt, m_ssm_a_re, m_ssm_a_im, m_ssm_b_re, m_ssm_b_im, m_ssm_c_re, m_ssm_c_im, m_ssm_d, m_ssm_w_glu, m_gmlp_norm_v, m_gmlp_w_s, m_gmlp_b_s, m_norm_ssm_out, m_norm_gmlp_out, m_w_out, m_norm_ffn2, m_w2_gate, m_w2_up, m_w2_down, m_norm_ple, m_w_ple_gate, m_w_ple_proj, m_norm_final, v_norm_ffn1, v_w1_gate, v_w1_up, v_w1_down, v_norm_mix, v_w_in, v_ssm_log_dt, v_ssm_a_re, v_ssm_a_im, v_ssm_b_re, v_ssm_b_im, v_ssm_c_re, v_ssm_c_im, v_ssm_d, v_ssm_w_glu, v_gmlp_norm_v, v_gmlp_w_s, v_gmlp_b_s, v_norm_ssm_out, v_norm_gmlp_out, v_w_out, v_norm_ffn2, v_w2_gate, v_w2_up, v_w2_down, v_norm_ple, v_w_ple_gate, v_w_ple_proj, v_norm_final):
    given = dict(locals())
    shapes = {n: given[n].shape for n in WEIGHTS}

    def block(name):
        a = given[name]
        if a.ndim == 1:
            return a.reshape(1, -1)
        return a[0] if a.ndim >= 3 else a

    w = {n: block(n) for n in WEIGHTS}
    m = {n: block("m_" + n) for n in WEIGHTS}
    v = {n: block("v_" + n) for n in WEIGHTS}
    loss, dx, grad, delta, new_m, new_v = _step(x[0], p[0, 0], loss_target[0], w, m, v)
    outs = [loss, dx[None]]
    for tree in (grad, delta, new_m, new_v):
        outs += [tree[n].reshape(shapes[n]) for n in WEIGHTS]
    return tuple(outs)
```

```python
import functools
import math

import jax
import jax.numpy as jnp
from jax import lax
from jax.experimental import pallas as pl
from jax.experimental.pallas import tpu as pltpu

F32 = jnp.float32
BF16 = jnp.bfloat16
EPS = 1e-6
SSM_GROUP = 16
SSM_STATE = 64
GROUPS_PER_BLOCK = 8
GMLP_HEAD = 128
CHUNK = 128
ADAM_LR = 0.001
ADAM_B1 = 0.9
ADAM_B2 = 0.999
ADAM_EPS = 1e-08
ADAM_WD = 0.01
ADAM_STEP = 10
N_CHIPS = 4
N_DEV = 8
LANES = 128
VMEM_LIMIT_BYTES = 56 * 1024 * 1024
MESH = pl.DeviceIdType.MESH
GELU_C = math.sqrt(2.0 / math.pi)
GELU_A = 0.044715

_DOT_DIMS = {
    "nn": (((1,), (0,)), ((), ())),
    "nt": (((1,), (1,)), ((), ())),
    "tn": (((0,), (0,)), ((), ())),
}


def _tile(dim, pref, align):
    if dim <= pref:
        return dim
    t = (pref // align) * align
    while t >= align:
        if dim % t == 0:
            return t
        t -= align
    return dim


def _params(semantics):
    return pltpu.CompilerParams(dimension_semantics=semantics, vmem_limit_bytes=VMEM_LIMIT_BYTES)


def _gelu(x):
    return 0.5 * x * (1.0 + jnp.tanh(GELU_C * (x + GELU_A * x * x * x)))


def _gelu_grad(x):
    t = jnp.tanh(GELU_C * (x + GELU_A * x * x * x))
    return 0.5 * (1.0 + t) + 0.5 * x * (1.0 - t * t) * GELU_C * (1.0 + 3.0 * GELU_A * x * x)


def _sigmoid(x):
    return 1.0 / (1.0 + jnp.exp(-x))


def _dot(a, b, mode):
    return lax.dot_general(a.astype(BF16), b.astype(BF16), _DOT_DIMS[mode], preferred_element_type=F32)


class _Carry:
    def __init__(self, arrays, out_shapes, aliases, n_copies, copies):
        self.arrays = list(arrays)
        self.out_shapes = list(out_shapes)
        self.aliases = dict(aliases)
        self.n_copies = n_copies
        self.copies = copies

    def scratch(self):
        return [pltpu.SemaphoreType.DMA((self.n_copies,)), pltpu.SemaphoreType.DMA((self.n_copies,))]

    def split(self, refs):
        n_in, n_out = len(self.arrays), len(self.out_shapes)
        return refs[:n_in], refs[n_in:n_in + n_out], refs[n_in + n_out], refs[n_in + n_out + 1]

    def start(self, refs):
        for cp in self.copies(*self.split(refs)):
            cp.start()

    def wait(self, refs):
        for cp in self.copies(*self.split(refs)):
            cp.wait()


class _SemRange:
    def __init__(self, sems, offset):
        self.sems, self.offset = sems, offset

    @property
    def at(self):
        return self

    def __getitem__(self, k):
        return self.sems.at[self.offset + k]


def _join(first, second):
    n_in, n_out = len(first.arrays), len(first.out_shapes)
    aliases = dict(first.aliases)
    aliases.update({n_in + i: n_out + o for i, o in second.aliases.items()})

    def copies(ops, res, send_sems, recv_sems):
        return (first.copies(ops[:n_in], res[:n_out], send_sems, recv_sems)
                + second.copies(ops[n_in:], res[n_out:], _SemRange(send_sems, first.n_copies),
                                _SemRange(recv_sems, first.n_copies)))

    return _Carry(first.arrays + second.arrays, first.out_shapes + second.out_shapes, aliases,
                  first.n_copies + second.n_copies, copies)


_ANY = pl.BlockSpec(memory_space=pl.ANY)


def _comm_call(name, carry):
    def body(*refs):
        carry.start(refs)
        carry.wait(refs)

    n_in = len(carry.arrays)
    return pl.pallas_call(
        body,
        name=name,
        in_specs=[_ANY] * n_in,
        out_specs=[_ANY] * len(carry.out_shapes),
        out_shape=carry.out_shapes,
        input_output_aliases=carry.aliases,
        scratch_shapes=carry.scratch(),
    )(*carry.arrays)


def _carried_call(body, carry, *, name, grid, in_specs, out_specs, out_shape, scratch_shapes, semantics, args):
    if carry is None:
        res = pl.pallas_call(body, name=name, grid=grid, in_specs=in_specs, out_specs=out_specs, out_shape=out_shape,
                             scratch_shapes=scratch_shapes, compiler_params=_params(semantics))(*args)
        return res, []
    n_in, n_out, n_scr = len(in_specs), len(out_specs), len(scratch_shapes)
    nci, nco = len(carry.arrays), len(carry.out_shapes)

    def wrapped(*refs):
        ins = refs[:n_in]
        outs = refs[n_in + nci:n_in + nci + n_out]
        scr = refs[n_in + nci + n_out + nco:n_in + nci + n_out + nco + n_scr]
        c_refs = (refs[n_in:n_in + nci] + refs[n_in + nci + n_out:n_in + nci + n_out + nco]
                  + refs[n_in + nci + n_out + nco + n_scr:])
        first = functools.reduce(jnp.logical_and, [pl.program_id(d) == 0 for d in range(len(grid))])
        last = functools.reduce(jnp.logical_and, [pl.program_id(d) == grid[d] - 1 for d in range(len(grid))])

        @pl.when(first)
        def _():
            carry.start(c_refs)

        body(*ins, *outs, *scr)

        @pl.when(last)
        def _():
            carry.wait(c_refs)

    res = pl.pallas_call(
        wrapped,
        name=name,
        grid=grid,
        in_specs=list(in_specs) + [_ANY] * nci,
        out_specs=list(out_specs) + [_ANY] * nco,
        out_shape=list(out_shape) + carry.out_shapes,
        input_output_aliases={n_in + i: n_out + o for i, o in carry.aliases.items()},
        scratch_shapes=list(scratch_shapes) + carry.scratch(),
        compiler_params=_params(("arbitrary",) * len(grid)),
    )(*args, *carry.arrays)
    return res[:n_out], res[n_out:]


def _matmul(name, mode, a_list, b_list, products, out_dtypes, epilogue, extras=(), tm=512, tn=512, tk=2048,
            carry=None, n_part=(0, 1)):
    a0, b0 = a_list[0], b_list[0]
    if mode == "tn":
        k_dim, m_dim = a0.shape
    else:
        m_dim, k_dim = a0.shape
    n_dim = (b0.shape[0] if mode == "nt" else b0.shape[1]) // n_part[1]
    tm = _tile(m_dim, tm, LANES)
    tn = _tile(n_dim, tn, LANES)
    tk = _tile(k_dim, tk, LANES)
    nk = k_dim // tk
    j0 = n_part[0] * (n_dim // tn)
    chunk = 2 * LANES if (nk == 1 and epilogue is not _identity and tn % (2 * LANES) == 0) else tn
    n_acc = 1 + max(p[2] for p in products)
    na, nb, ne, no = len(a_list), len(b_list), len(extras), len(out_dtypes)

    if mode == "tn":
        a_spec = pl.BlockSpec((tk, tm), lambda i, j, k: (k, i))
    else:
        a_spec = pl.BlockSpec((tm, tk), lambda i, j, k: (i, k))
    if mode == "nt":
        b_spec = pl.BlockSpec((tn, tk), lambda i, j, k: (j0 + j, k))
    else:
        b_spec = pl.BlockSpec((tk, tn), lambda i, j, k: (k, j0 + j))
    t_spec = pl.BlockSpec((tm, tn), lambda i, j, k: (i, j))

    def body(*refs):
        a_refs = refs[:na]
        b_refs = refs[na:na + nb]
        e_refs = refs[na + nb:na + nb + ne]
        o_refs = refs[na + nb + ne:na + nb + ne + no]
        acc_refs = refs[na + nb + ne + no:]

        def partial_sums(cols):
            sums = [None] * n_acc
            for ai, bi, ci in products:
                b = b_refs[bi][cols, :] if mode == "nt" else b_refs[bi][:, cols]
                d = _dot(a_refs[ai][...], b, mode)
                sums[ci] = d if sums[ci] is None else sums[ci] + d
            return sums

        def finish(accs, cols):
            outs = epilogue(accs, [e[:, cols] for e in e_refs])
            for o_ref, o in zip(o_refs, outs):
                o_ref[:, cols] = o.astype(o_ref.dtype)

        if nk == 1:
            for c0 in range(0, tn, chunk):
                finish(partial_sums(slice(c0, c0 + chunk)), slice(c0, c0 + chunk))
        else:
            sums = partial_sums(slice(None))
            finish = functools.partial(finish, cols=slice(None))
            k = pl.program_id(2)

            @pl.when(k == 0)
            def _():
                for acc, s in zip(acc_refs, sums):
                    acc[...] = s

            @pl.when(k > 0)
            def _():
                for acc, s in zip(acc_refs, sums):
                    acc[...] += s

            @pl.when(k == nk - 1)
            def _():
                finish([acc[...] for acc in acc_refs])

    scratch = [pltpu.VMEM((tm, tn), F32) for _ in range(n_acc)] if nk > 1 else []
    outs, carried = _carried_call(
        body, carry,
        name=name,
        grid=(m_dim // tm, n_dim // tn, nk),
        in_specs=[a_spec] * na + [b_spec] * nb + [t_spec] * ne,
        out_specs=[t_spec] * no,
        out_shape=[jax.ShapeDtypeStruct((m_dim, n_dim), dt) for dt in out_dtypes],
        scratch_shapes=scratch,
        semantics=("parallel", "parallel", "arbitrary"),
        args=[*a_list, *b_list, *extras],
    )
    return (outs, carried) if carry else outs


def _identity(accs, extras):
    return accs


def _single(result, carry):
    return (result[0][0], result[1]) if carry else result[0]


def _mm_nn(name, a, b, out_dtype, res=None, alpha=1.0, carry=None, **tiles):
    if res is None:
        return _single(_matmul(name, "nn", [a], [b], [(0, 0, 0)], [out_dtype], _identity, carry=carry, **tiles), carry)

    def epilogue(accs, extras):
        return [extras[0] + alpha * accs[0]]

    return _single(_matmul(name, "nn", [a], [b], [(0, 0, 0)], [out_dtype], epilogue, extras=(res,), carry=carry,
                           **tiles), carry)


def _mm_nt_sum(name, a_list, b_list, out_dtype, carry=None, n_part=(0, 1), **tiles):
    products = [(i, i, 0) for i in range(len(a_list))]
    return _single(_matmul(name, "nt", a_list, b_list, products, [out_dtype], _identity, carry=carry, n_part=n_part,
                           **tiles), carry)


def _mm_tn(name, a, b_list, out_dtype, carry=None, **tiles):
    products = [(0, i, i) for i in range(len(b_list))]
    return _matmul(name, "tn", [a], b_list, products, [out_dtype] * len(b_list), _identity, carry=carry, **tiles)


def _rowwise(name, fn, row_ins, par_ins, row_outs, acc_outs=(), tr=512):
    first = row_ins[0][0] if isinstance(row_ins[0], tuple) else row_ins[0]
    t_dim = first.shape[0]
    tr = _tile(t_dim, tr, 16)
    arrays, specs = [], []
    for r in row_ins:
        if isinstance(r, tuple):
            arr, width, blk = r
            specs.append(pl.BlockSpec((tr, width), lambda i, blk=blk: (i, blk)))
        else:
            arr = r
            specs.append(pl.BlockSpec((tr, arr.shape[1]), lambda i: (i, 0)))
        arrays.append(arr)
    for p in par_ins:
        arrays.append(p)
        specs.append(pl.BlockSpec(p.shape, lambda i, nd=p.ndim: (0,) * nd))
    nr, npar, nro, nacc = len(row_ins), len(par_ins), len(row_outs), len(acc_outs)

    def body(*refs):
        rows = [r[...] for r in refs[:nr]]
        pars = [p[...] for p in refs[nr:nr + npar]]
        o_refs = refs[nr + npar:nr + npar + nro]
        acc_refs = refs[nr + npar + nro:]
        outs, accs = fn(rows, pars)
        for o_ref, o in zip(o_refs, outs):
            o_ref[...] = o.astype(o_ref.dtype)
        if nacc:
            @pl.when(pl.program_id(0) == 0)
            def _():
                for a_ref in acc_refs:
                    a_ref[...] = jnp.zeros_like(a_ref)

            for a_ref, a in zip(acc_refs, accs):
                a_ref[...] += a

    out_shape = [jax.ShapeDtypeStruct((t_dim, c), dt) for c, dt in row_outs]
    out_shape += [jax.ShapeDtypeStruct(s, F32) for s in acc_outs]
    out_specs = [pl.BlockSpec((tr, c), lambda i: (i, 0)) for c, _ in row_outs]
    out_specs += [pl.BlockSpec(s, lambda i: (0, 0)) for s in acc_outs]
    return pl.pallas_call(
        body,
        name=name,
        grid=(t_dim // tr,),
        in_specs=specs,
        out_specs=out_specs,
        out_shape=out_shape,
        compiler_params=_params(("arbitrary",)),
    )(*arrays)


def _rms_stats(x):
    r = lax.rsqrt(jnp.mean(x * x, axis=-1, keepdims=True) + EPS)
    return x * r, r


def _rms_backward(x, g, dy):
    xh, r = _rms_stats(x)
    a = dy * g
    dx = r * (a - xh * jnp.mean(a * xh, axis=-1, keepdims=True))
    return dx, jnp.sum(dy * xh, axis=0, keepdims=True)


def _rms_fwd(name, x, g):
    def fn(rows, pars):
        xh, _ = _rms_stats(rows[0])
        return [xh * pars[0]], []

    return _rowwise(name, fn, [x], [g], [(x.shape[1], BF16)])[0]


def _rms_bwd(name, x, g, dy_parts, dres, scale):
    def fn(rows, pars):
        dy = rows[2] if len(rows) == 3 else jnp.concatenate(rows[2:], axis=1)
        dx, dg = _rms_backward(rows[0], pars[0], dy)
        tot = rows[1] + dx
        return [tot, scale * tot], [dg]

    d = x.shape[1]
    return _rowwise(name, fn, [x, dres, *dy_parts], [g], [(d, F32), (d, BF16)], [(1, d)], tr=256)


def _cast_into_gathered(name, w, kind, q_arr):
    rs, cs = w.shape
    tr = _tile(rs, 256, 16)
    nr = rs // tr

    def body(q_ref, w_ref, o_ref):
        o_ref[...] = w_ref[...].astype(BF16)

    if kind == "row":
        o_spec = pl.BlockSpec((tr, cs), lambda i, q_ref: (q_ref[0] * nr + i, 0))
    else:
        o_spec = pl.BlockSpec((tr, cs), lambda i, q_ref: (i, q_ref[0]))
    return pl.pallas_call(
        body,
        name=name,
        grid_spec=pltpu.PrefetchScalarGridSpec(num_scalar_prefetch=1, grid=(nr,),
                                               in_specs=[pl.BlockSpec((tr, cs), lambda i, q_ref: (i, 0))],
                                               out_specs=o_spec),
        out_shape=jax.ShapeDtypeStruct(_full_shape(kind, (rs, cs)), BF16),
        compiler_params=_params(("parallel",)),
    )(q_arr, w)


def _ffn_gateup(tag, n, wg, wu, carry):
    def act(accs, extras):
        gate, up = accs
        return [gate * _sigmoid(gate) * up, gate, up]

    return _matmul(tag + "_gateup", "nn", [n], [wg, wu], [(0, 0, 0), (0, 1, 1)], [BF16] * 3, act,
                   tm=1024, tn=512, tk=2048, carry=carry)


def _ffn_up(tag, n, wu, gate, carry):
    def act(accs, extras):
        g = extras[0].astype(F32)
        return [g * _sigmoid(g) * accs[0], accs[0]]

    return _matmul(tag + "_up", "nn", [n], [wu], [(0, 0, 0)], [BF16] * 2, act, extras=(gate,),
                   tm=1024, tn=512, tk=2048, carry=carry)


def _ffn_down(tag, a, wd, h, carry=None):
    return _mm_nn(tag + "_down", a, wd, F32, res=h, alpha=0.5, tm=1024, tn=512, tk=5632, carry=carry)


def _ffn_bwd(tag, names, h, g, wg, wu, wd, saved, dh, dfb, next_scale, reducer, riding=None, riding_dwd=None,
             last_hop_later=False):
    n, a, gate, up = saved

    def act_bwd(accs, extras):
        da = accs[0]
        gt, u = extras[0].astype(F32), extras[1].astype(F32)
        s = _sigmoid(gt)
        return [da * u * (s * (1.0 + gt * (1.0 - s))), da * (gt * s)]

    dact = _matmul(tag + "_dact", "nt", [dfb], [wd], [(0, 0, 0)], [BF16, BF16], act_bwd, extras=(gate, up),
                   tm=1024, tn=512, tk=2048, carry=riding)
    (dgp, du), rode = dact if riding else (dact, [])
    dwd_call = _mm_tn(tag + "_dwd", a, [dfb], BF16, tm=512, tn=2048, tk=4096, carry=riding_dwd)
    (dwd,), rode_dwd = dwd_call if riding_dwd else (dwd_call, [])
    down = reducer.swap(names[2:], [dwd])
    (dwg,), swapped = _mm_tn(tag + "_dwg", n, [dgp], BF16, tm=512, tn=1408, tk=4096, carry=down.carry)
    down = reducer.send(down, swapped)
    gate_w = reducer.swap(names[:1], [dwg])
    carry, split = _ride(down.carry, gate_w.carry)
    (dwu,), results = _mm_tn(tag + "_dwu", n, [du], BF16, tm=512, tn=1408, tk=4096, carry=carry)
    got, swapped = split(results)
    reducer.end(down, got)
    gate_w = reducer.send(gate_w, swapped)
    up_w = reducer.swap(names[1:2], [dwu])
    carry, split = _ride(gate_w.carry, up_w.carry)
    halves = 1 if last_hop_later else 2
    dn_lo, results = _mm_nt_sum(tag + "_dn_lo", [dgp, du], [wg, wu], F32, tm=1024, tn=1024, tk=1408, carry=carry,
                                n_part=(0, halves))
    got, swapped = split(results)
    reducer.end(gate_w, got)
    up_w = reducer.send(up_w, swapped)
    dn = [dn_lo]
    if not last_hop_later:
        dn_hi, got = _mm_nt_sum(tag + "_dn_hi", [dgp, du], [wg, wu], F32, tm=1024, tn=1024, tk=1408,
                                carry=up_w.carry, n_part=(1, 2))
        reducer.end(up_w, got)
        dn.append(dn_hi)
    dh_in, dh_in_b, dg = _rms_bwd(tag + "_dnorm", h, g, dn, dh, next_scale)
    return dh_in, dh_in_b, dg, rode, rode_dwd, (up_w if last_hop_later else None)


def _ssm_discretize(log_dt, a_re, a_im, b_re, b_im):
    dt = jnp.exp(log_dt)[:, None]
    lr = jnp.minimum(a_re, -1e-4)
    li = a_im
    mag = jnp.exp(lr * dt)
    ang = li * dt
    abar_r = mag * jnp.cos(ang)
    abar_i = mag * jnp.sin(ang)
    den = lr * lr + li * li
    xr = abar_r - 1.0
    xi = abar_i
    zr = (xr * lr + xi * li) / den
    zi = (xi * lr - xr * li) / den
    bbar_r = zr[..., None] * b_re - zi[..., None] * b_im
    bbar_i = zr[..., None] * b_im + zi[..., None] * b_re
    return abar_r, abar_i, bbar_r, bbar_i


def _blockdiag_in(b):
    g, n, p = b.shape
    nb = g // GROUPS_PER_BLOCK
    eye = jnp.eye(GROUPS_PER_BLOCK, dtype=b.dtype)
    b4 = b.reshape(nb, GROUPS_PER_BLOCK, n, p)
    return jnp.einsum("sgnp,gh->sgphn", b4, eye).reshape(nb, GROUPS_PER_BLOCK * p, GROUPS_PER_BLOCK * n)


def _blockdiag_in_grad(gw, n, p):
    nb = gw.shape[0]
    eye = jnp.eye(GROUPS_PER_BLOCK, dtype=gw.dtype)
    g5 = gw.reshape(nb, GROUPS_PER_BLOCK, p, GROUPS_PER_BLOCK, n)
    return jnp.einsum("sgphn,gh->sgnp", g5, eye).reshape(nb * GROUPS_PER_BLOCK, n, p)


def _blockdiag_out(c):
    g, p, n = c.shape
    nb = g // GROUPS_PER_BLOCK
    eye = jnp.eye(GROUPS_PER_BLOCK, dtype=c.dtype)
    c4 = c.reshape(nb, GROUPS_PER_BLOCK, p, n)
    return jnp.einsum("sgpn,gh->shngp", c4, eye).reshape(nb, GROUPS_PER_BLOCK * n, GROUPS_PER_BLOCK * p)


def _blockdiag_out_grad(gw, p, n):
    nb = gw.shape[0]
    eye = jnp.eye(GROUPS_PER_BLOCK, dtype=gw.dtype)
    g5 = gw.reshape(nb, GROUPS_PER_BLOCK, n, GROUPS_PER_BLOCK, p)
    return jnp.einsum("shngp,gh->sgpn", g5, eye).reshape(nb * GROUPS_PER_BLOCK, p, n)


def _ssm_scan_fwd(z, wb, wc, abar, d, carry=None):
    t_dim = z.shape[0]
    nb, cb, sw2 = wb.shape
    nl = sw2 // LANES
    hl = nl // 2
    tt = _tile(t_dim, 256, 8)
    nt = t_dim // tt

    def body(z_ref, wb_ref, wc_ref, a_ref, d_ref, s_ref, y_ref, yg_ref, drive_ref, st_ref):
        @pl.when(pl.program_id(0) == 0)
        def _():
            st_ref[...] = jnp.zeros_like(st_ref)

        u = z_ref[...]
        ub = u.astype(BF16)
        for b in range(nb):
            drive = _dot(ub[:, b * cb:(b + 1) * cb], wb_ref[b], "nn")
            for l in range(nl):
                drive_ref[l, pl.ds(b, tt, stride=nb), :] = drive[:, l * LANES:(l + 1) * LANES]
        a = a_ref[...]
        chunk = lambda v, l: v[:, l * LANES:(l + 1) * LANES]

        def step(t, state):
            rows = pl.ds(pl.multiple_of(t * nb, nb), nb)
            re, im = [], []
            for l in range(hl):
                ar, ai, sr, si = chunk(a, l), chunk(a, hl + l), state[l], state[hl + l]
                nr = ar * sr - ai * si + drive_ref[l, rows, :]
                ni = ar * si + ai * sr + drive_ref[hl + l, rows, :]
                s_ref[l, rows, :] = nr
                s_ref[hl + l, rows, :] = ni
                re.append(nr)
                im.append(ni)
            return tuple(re + im)

        state = lax.fori_loop(0, tt, step, tuple(st_ref[l] for l in range(nl)), unroll=8)
        for l in range(nl):
            st_ref[l] = state[l]
        parts = []
        for b in range(nb):
            s_b = jnp.concatenate([s_ref[l, pl.ds(b, tt, stride=nb), :] for l in range(nl)], axis=1)
            parts.append(_dot(s_b, wc_ref[b], "nn"))
        y = jnp.concatenate(parts, axis=1) + d_ref[...] * u
        y_ref[...] = y
        yg_ref[...] = _gelu(y).astype(BF16)

    full = lambda a: pl.BlockSpec(a.shape, lambda t, nd=a.ndim: (0,) * nd)
    return _carried_call(
        body, carry,
        name="ssm_scan_fwd",
        grid=(nt,),
        in_specs=[pl.BlockSpec((tt, nb * cb), lambda t: (t, 0)), full(wb), full(wc), full(abar), full(d)],
        out_specs=[
            pl.BlockSpec((nl, tt * nb, LANES), lambda t: (0, t, 0)),
            pl.BlockSpec((tt, nb * cb), lambda t: (t, 0)),
            pl.BlockSpec((tt, nb * cb), lambda t: (t, 0)),
        ],
        out_shape=[
            jax.ShapeDtypeStruct((nl, t_dim * nb, LANES), F32),
            jax.ShapeDtypeStruct((t_dim, nb * cb), F32),
            jax.ShapeDtypeStruct((t_dim, nb * cb), BF16),
        ],
        scratch_shapes=[pltpu.VMEM((nl, tt * nb, LANES), F32), pltpu.VMEM((nl, nb, LANES), F32)],
        semantics=("arbitrary",),
        args=[z, wb, wc, abar, d],
    )


def _ssm_scan_bwd(z, dy, states, wb, wc, abar_conj, d, carry=None):
    t_dim = z.shape[0]
    nb, cb, sw2 = wb.shape
    nl = sw2 // LANES
    hl = nl // 2
    tt = _tile(t_dim, 128, 8)
    nt = t_dim // tt
    edges = states.reshape(nl, nt, tt * nb, LANES)[:, :, (tt - 1) * nb:, :]
    before = jnp.concatenate([jnp.zeros((nl, 1, nb, LANES), F32), edges[:, :-1]], axis=1).reshape(nl, nt * nb, LANES)

    def body(z_ref, dy_ref, s_ref, sp_ref, wb_ref, wc_ref, a_ref, d_ref,
             dz_ref, gwb_ref, gwc_ref, ga_ref, gd_ref, gin_ref, gs_ref, st_ref):
        @pl.when(pl.program_id(0) == 0)
        def _():
            st_ref[...] = jnp.zeros_like(st_ref)
            gwb_ref[...] = jnp.zeros_like(gwb_ref)
            gwc_ref[...] = jnp.zeros_like(gwc_ref)
            ga_ref[...] = jnp.zeros_like(ga_ref)
            gd_ref[...] = jnp.zeros_like(gd_ref)

        u = z_ref[...]
        dyv = dy_ref[...]
        ub = u.astype(BF16)
        dyb = dyv.astype(BF16)
        for b in range(nb):
            gin = _dot(dyb[:, b * cb:(b + 1) * cb], wc_ref[b], "nt")
            for l in range(nl):
                gin_ref[l, pl.ds(b, tt, stride=nb), :] = gin[:, l * LANES:(l + 1) * LANES]
        a = a_ref[...]
        chunk = lambda v, l: v[:, l * LANES:(l + 1) * LANES]

        def step(k, state):
            rows = pl.ds(pl.multiple_of((tt - 1 - k) * nb, nb), nb)
            re, im = [], []
            for l in range(hl):
                ar, ai, gr, gi = chunk(a, l), chunk(a, hl + l), state[l], state[hl + l]
                nr = ar * gr - ai * gi + gin_ref[l, rows, :]
                ni = ar * gi + ai * gr + gin_ref[hl + l, rows, :]
                gs_ref[l, rows, :] = nr
                gs_ref[hl + l, rows, :] = ni
                re.append(nr)
                im.append(ni)
            return tuple(re + im)

        state = lax.fori_loop(0, tt, step, tuple(st_ref[l] for l in range(nl)), unroll=8)
        for l in range(nl):
            st_ref[l] = state[l]

        parts = []
        for b in range(nb):
            cols = slice(b * cb, (b + 1) * cb)
            gs_b = jnp.concatenate([gs_ref[l, pl.ds(b, tt, stride=nb), :] for l in range(nl)], axis=1)
            s_b = jnp.concatenate([s_ref[l, pl.ds(b, tt, stride=nb), :] for l in range(nl)], axis=1)
            parts.append(_dot(gs_b, wb_ref[b], "nt"))
            gwb_ref[b] += _dot(ub[:, cols], gs_b, "tn")
            gwc_ref[b] += _dot(s_b, dyb[:, cols], "tn")
        dz_ref[...] = (jnp.concatenate(parts, axis=1) + d_ref[...] * dyv).astype(BF16)
        gd_ref[...] += jnp.sum(dyv * u, axis=0, keepdims=True)

        row = lax.broadcasted_iota(jnp.int32, (tt * nb, LANES), 0)
        shifted = lambda v: jnp.where(row < nb, 0.0, pltpu.roll(v, nb, 0))
        over_time = lambda v: jnp.sum(v.reshape(tt, nb, LANES), axis=0)
        for l in range(hl):
            g_r, g_i = gs_ref[l], gs_ref[hl + l]
            p_r, p_i = shifted(s_ref[l]), shifted(s_ref[hl + l])
            f_r, f_i = sp_ref[l], sp_ref[hl + l]
            g0_r, g0_i = gs_ref[l, pl.ds(0, nb), :], gs_ref[hl + l, pl.ds(0, nb), :]
            ga_ref[l] += over_time(g_r * p_r + g_i * p_i) + g0_r * f_r + g0_i * f_i
            ga_ref[hl + l] += over_time(g_i * p_r - g_r * p_i) + g0_i * f_r - g0_r * f_i

    rev = lambda t: (nt - 1 - t, 0)
    rev3 = lambda t: (0, nt - 1 - t, 0)
    full = lambda a: pl.BlockSpec(a.shape, lambda t, nd=a.ndim: (0,) * nd)
    return _carried_call(
        body, carry,
        name="ssm_scan_bwd",
        grid=(nt,),
        in_specs=[
            pl.BlockSpec((tt, nb * cb), rev),
            pl.BlockSpec((tt, nb * cb), rev),
            pl.BlockSpec((nl, tt * nb, LANES), rev3),
            pl.BlockSpec((nl, nb, LANES), rev3),
            full(wb), full(wc), full(abar_conj), full(d),
        ],
        out_specs=[
            pl.BlockSpec((tt, nb * cb), rev),
            pl.BlockSpec((nb, cb, sw2), lambda t: (0, 0, 0)),
            pl.BlockSpec((nb, sw2, cb), lambda t: (0, 0, 0)),
            pl.BlockSpec((nl, nb, LANES), lambda t: (0, 0, 0)),
            pl.BlockSpec((1, nb * cb), lambda t: (0, 0)),
        ],
        out_shape=[
            jax.ShapeDtypeStruct((t_dim, nb * cb), BF16),
            jax.ShapeDtypeStruct((nb, cb, sw2), F32),
            jax.ShapeDtypeStruct((nb, sw2, cb), F32),
            jax.ShapeDtypeStruct((nl, nb, LANES), F32),
            jax.ShapeDtypeStruct((1, nb * cb), F32),
        ],
        scratch_shapes=[pltpu.VMEM((nl, tt * nb, LANES), F32), pltpu.VMEM((nl, tt * nb, LANES), F32),
                        pltpu.VMEM((nl, nb, LANES), F32)],
        semantics=("arbitrary",),
        args=[z, dy, states, before, wb, wc, abar_conj, d],
    )


def _gmlp_chunk(zu, zv, gv, wm_ref, bias, n_heads):
    ua = _gelu(zu)
    vg = _gelu(zv)
    xc = vg - jnp.mean(vg, axis=-1, keepdims=True)
    r = lax.rsqrt(jnp.mean(xc * xc, axis=-1, keepdims=True) + EPS)
    vh = xc * r
    vb = (vh * gv).astype(BF16)
    parts = []
    for h in range(n_heads):
        cols = slice(h * GMLP_HEAD, (h + 1) * GMLP_HEAD)
        parts.append(_dot(wm_ref[h], vb[:, cols], "nn"))
    s = jnp.concatenate(parts, axis=1) + bias
    return ua, vh, r, vb, s


def _gmlp_fwd(z, gv, wm, bias, ggo):
    t_dim = z.shape[0]
    dg = gv.shape[1]
    n_heads = dg // GMLP_HEAD
    tr = _tile(t_dim, 256, CHUNK)

    def body(zu_ref, zv_ref, gv_ref, wm_ref, b_ref, ggo_ref, o_ref):
        for ck in range(tr // CHUNK):
            rows = pl.ds(ck * CHUNK, CHUNK)
            ua, _, _, _, s = _gmlp_chunk(zu_ref[rows, :], zv_ref[rows, :], gv_ref[...], wm_ref, b_ref[...], n_heads)
            yh, _ = _rms_stats(ua * s)
            o_ref[rows, :] = (yh * ggo_ref[...]).astype(BF16)

    full = lambda a: pl.BlockSpec(a.shape, lambda i, nd=a.ndim: (0,) * nd)
    return pl.pallas_call(
        body,
        name="gmlp_fwd",
        grid=(t_dim // tr,),
        in_specs=[pl.BlockSpec((tr, dg), lambda i: (i, 1)), pl.BlockSpec((tr, dg), lambda i: (i, 2)),
                  full(gv), full(wm), full(bias), full(ggo)],
        out_specs=pl.BlockSpec((tr, dg), lambda i: (i, 0)),
        out_shape=jax.ShapeDtypeStruct((t_dim, dg), BF16),
        compiler_params=_params(("parallel",)),
    )(z, z, gv, wm, bias, ggo)


def _gmlp_bwd(z, dycat, gv, wm, bias, ggo):
    t_dim = z.shape[0]
    dg = gv.shape[1]
    n_heads = dg // GMLP_HEAD
    tr = _tile(t_dim, 256, CHUNK)

    def body(zu_ref, zv_ref, dy_ref, gv_ref, wm_ref, b_ref, ggo_ref,
             dzu_ref, dzv_ref, dggo_ref, dgv_ref, dwm_ref, dsum_ref):
        @pl.when(pl.program_id(0) == 0)
        def _():
            dggo_ref[...] = jnp.zeros_like(dggo_ref)
            dgv_ref[...] = jnp.zeros_like(dgv_ref)
            dwm_ref[...] = jnp.zeros_like(dwm_ref)
            dsum_ref[...] = jnp.zeros_like(dsum_ref)

        for ck in range(tr // CHUNK):
            rows = pl.ds(ck * CHUNK, CHUNK)
            zu = zu_ref[rows, :]
            zv = zv_ref[rows, :]
            gvv = gv_ref[...]
            ua, vh, r, vb, s = _gmlp_chunk(zu, zv, gvv, wm_ref, b_ref[...], n_heads)
            dy, dggo = _rms_backward(ua * s, ggo_ref[...], dy_ref[rows, :])
            dggo_ref[...] += dggo
            ds = dy * ua
            dsum_ref[...] += ds
            dsb = ds.astype(BF16)
            parts = []
            for h in range(n_heads):
                cols = slice(h * GMLP_HEAD, (h + 1) * GMLP_HEAD)
                dwm_ref[h] += _dot(dsb[:, cols], vb[:, cols], "nt")
                parts.append(_dot(wm_ref[h], dsb[:, cols], "tn"))
            dv = jnp.concatenate(parts, axis=1)
            dgv_ref[...] += jnp.sum(dv * vh, axis=0, keepdims=True)
            dvh = dv * gvv
            dvg = r * (dvh - jnp.mean(dvh, axis=-1, keepdims=True) - vh * jnp.mean(dvh * vh, axis=-1, keepdims=True))
            dzv_ref[rows, :] = (dvg * _gelu_grad(zv)).astype(BF16)
            dzu_ref[rows, :] = (dy * s * _gelu_grad(zu)).astype(BF16)

    full = lambda a: pl.BlockSpec(a.shape, lambda i, nd=a.ndim: (0,) * nd)
    return pl.pallas_call(
        body,
        name="gmlp_bwd",
        grid=(t_dim // tr,),
        in_specs=[pl.BlockSpec((tr, dg), lambda i: (i, 1)), pl.BlockSpec((tr, dg), lambda i: (i, 2)),
                  pl.BlockSpec((tr, dg), lambda i: (i, 1)), full(gv), full(wm), full(bias), full(ggo)],
        out_specs=[pl.BlockSpec((tr, dg), lambda i: (i, 0)), pl.BlockSpec((tr, dg), lambda i: (i, 0)),
                   pl.BlockSpec((1, dg), lambda i: (0, 0)), pl.BlockSpec((1, dg), lambda i: (0, 0)),
                   pl.BlockSpec(wm.shape, lambda i: (0, 0, 0)), pl.BlockSpec((CHUNK, dg), lambda i: (0, 0))],
        out_shape=[jax.ShapeDtypeStruct((t_dim, dg), BF16), jax.ShapeDtypeStruct((t_dim, dg), BF16),
                   jax.ShapeDtypeStruct((1, dg), F32), jax.ShapeDtypeStruct((1, dg), F32),
                   jax.ShapeDtypeStruct(wm.shape, F32), jax.ShapeDtypeStruct((CHUNK, dg), F32)],
        compiler_params=_params(("arbitrary",)),
    )(z, z, dycat, gv, wm, bias, ggo)


def _ple_head(npl, w_gate, h3, pp, tgt, g_final):
    t_dim, d = h3.shape
    tr = _tile(t_dim, 256, 16)

    def body(n_ref, w_ref, h_ref, pp_ref, t_ref, g_ref, dgq_ref, dpp_ref, dh_ref, dg_ref, loss_ref):
        @pl.when(pl.program_id(0) == 0)
        def _():
            dg_ref[...] = jnp.zeros_like(dg_ref)
            loss_ref[...] = jnp.zeros_like(loss_ref)

        gate = _sigmoid(_dot(n_ref[...], w_ref[...], "nn"))
        ppv = pp_ref[...]
        h4 = h_ref[...] + gate * ppv
        xh, _ = _rms_stats(h4)
        err = xh * g_ref[...] - t_ref[...]
        dh4, dg = _rms_backward(h4, g_ref[...], err * (1.0 / d))
        dh_ref[...] = dh4
        dgq_ref[...] = (dh4 * ppv * gate * (1.0 - gate)).astype(BF16)
        dpp_ref[...] = (dh4 * gate).astype(BF16)
        dg_ref[...] += dg
        loss_ref[...] += jnp.full((1, LANES), 0.5 * jnp.sum(err * err) * (1.0 / d), F32)

    rows = pl.BlockSpec((tr, d), lambda i: (i, 0))
    whole = lambda a: pl.BlockSpec(a.shape, lambda i: (0, 0))
    return pl.pallas_call(
        body,
        name="ple_head",
        grid=(t_dim // tr,),
        in_specs=[rows, whole(w_gate), rows, rows, rows, whole(g_final)],
        out_specs=[rows, rows, rows, pl.BlockSpec((1, d), lambda i: (0, 0)), pl.BlockSpec((1, LANES), lambda i: (0, 0))],
        out_shape=[jax.ShapeDtypeStruct((t_dim, d), BF16), jax.ShapeDtypeStruct((t_dim, d), BF16),
                   jax.ShapeDtypeStruct((t_dim, d), F32), jax.ShapeDtypeStruct((1, d), F32),
                   jax.ShapeDtypeStruct((1, LANES), F32)],
        compiler_params=_params(("arbitrary",)),
    )(npl, w_gate, h3, pp, tgt, g_final)


def _position():
    x, y, c = lax.axis_index("x"), lax.axis_index("y"), lax.axis_index("c")
    chips = [(1 - x, y), (x, 1 - y), (1 - x, 1 - y)]
    return x, y, c, chips


def _region(ref, kind, shard_shape, q, half):
    rs, cs = shard_shape
    r0, nr = (0, rs) if half is None else (half * (rs // 2), rs // 2)
    if kind == "row":
        return ref.at[pl.ds(q * rs + r0, nr), :]
    return ref.at[pl.ds(r0, nr), pl.ds(q * cs, cs)]


def _full_shape(kind, shard_shape):
    rs, cs = shard_shape
    return (N_CHIPS * rs, cs) if kind == "row" else (rs, N_CHIPS * cs)


def _remote(src, dst, send_sems, recv_sems, k, to):
    return pltpu.make_async_remote_copy(src_ref=src, dst_ref=dst, send_sem=send_sems.at[k], recv_sem=recv_sems.at[k],
                                        device_id=to, device_id_type=MESH)


def _same(arrays):
    return [jax.ShapeDtypeStruct(a.shape, a.dtype) for a in arrays]


def _gather_ici_carry(gathered, kinds, shapes):
    nw = len(gathered)

    def copies(ops, full, send_sems, recv_sems):
        x, y, c, chips = _position()
        out = []
        for w in range(nw):
            mine = _region(full[w], kinds[w], shapes[w], 2 * x + y, c)
            for j, (cx, cy) in enumerate(chips):
                out.append(_remote(mine, mine, send_sems, recv_sems, 3 * w + j, (cx, cy, c)))
        return out

    return _Carry(gathered, _same(gathered), {i: i for i in range(nw)}, 3 * nw, copies)


def _gather_d2d_carry(gathered, kinds, shapes):
    nw = len(gathered)

    def copies(ops, full, send_sems, recv_sems):
        x, y, c, chips = _position()
        out = []
        for w in range(nw):
            for j, (cx, cy) in enumerate(chips):
                landed = _region(full[w], kinds[w], shapes[w], 2 * cx + cy, c)
                out.append(_remote(landed, landed, send_sems, recv_sems, 3 * w + j, (x, y, 1 - c)))
        return out

    return _Carry(gathered, _same(gathered), {i: i for i in range(nw)}, 3 * nw, copies)


def _pairs_carry(grads, kinds, shapes):
    nw = len(grads)

    def copies(g, got, send_sems, recv_sems):
        x, y, c, _ = _position()
        out = []
        for w in range(nw):
            for q in range(N_CHIPS):
                out.append(_remote(_region(g[w], kinds[w], shapes[w], q, 1 - c), got[w].at[q], send_sems, recv_sems,
                                   N_CHIPS * w + q, (x, y, 1 - c)))
        return out

    outs = [jax.ShapeDtypeStruct((N_CHIPS, s[0] // 2, s[1]), BF16) for s in shapes]
    return _Carry(grads, outs, {}, N_CHIPS * nw, copies)


def _pair_sum(name, grad, got, kind, shard_shape, c_arr):
    rs, cs = shard_shape
    hr = rs // 2
    tr = _tile(hr, 512, 16)
    nr = hr // tr

    def body(c_ref, g_ref, s_ref, o_ref):
        o_ref[...] = (g_ref[...].astype(F32) + s_ref[...].astype(F32)).astype(BF16)

    if kind == "row":
        g_spec = pl.BlockSpec((tr, cs), lambda q, i, c_ref: (q * (rs // tr) + c_ref[0] * nr + i, 0))
    else:
        g_spec = pl.BlockSpec((tr, cs), lambda q, i, c_ref: (c_ref[0] * nr + i, q))
    blk = pl.BlockSpec((None, tr, cs), lambda q, i, c_ref: (q, i, 0))
    return pl.pallas_call(
        body,
        name=name,
        grid_spec=pltpu.PrefetchScalarGridSpec(num_scalar_prefetch=1, grid=(N_CHIPS, nr), in_specs=[g_spec, blk],
                                               out_specs=blk),
        out_shape=jax.ShapeDtypeStruct((N_CHIPS, hr, cs), BF16),
        compiler_params=_params(("parallel", "parallel")),
    )(c_arr, grad, got)


def _scatter_carry(sums, shapes):
    nw = len(sums)

    def copies(ps, got, send_sems, recv_sems):
        x, y, c, chips = _position()
        out = []
        for w in range(nw):
            for j, (cx, cy) in enumerate(chips):
                out.append(_remote(ps[w].at[2 * cx + cy], got[w].at[j], send_sems, recv_sems, 3 * w + j, (cx, cy, c)))
        return out

    outs = [jax.ShapeDtypeStruct((3, s[0] // 2, s[1]), BF16) for s in shapes]
    return _Carry(sums, outs, {}, 3 * nw, copies)


def _owner_sum(name, sums, got, shard_shape, qc_arr):
    rs, cs = shard_shape
    hr = rs // 2
    tr = _tile(hr, 512, 16)
    nr = hr // tr

    def body(qc_ref, mine_ref, got_ref, o_ref):
        acc = mine_ref[...].astype(F32)
        for j in range(3):
            acc = acc + got_ref[j].astype(F32)
        o_ref[...] = acc

    return pl.pallas_call(
        body,
        name=name,
        grid_spec=pltpu.PrefetchScalarGridSpec(
            num_scalar_prefetch=1, grid=(nr,),
            in_specs=[pl.BlockSpec((None, tr, cs), lambda i, qc_ref: (qc_ref[0], i, 0)),
                      pl.BlockSpec((3, tr, cs), lambda i, qc_ref: (0, i, 0))],
            out_specs=pl.BlockSpec((tr, cs), lambda i, qc_ref: (qc_ref[1] * nr + i, 0))),
        out_shape=jax.ShapeDtypeStruct((rs, cs), F32),
        compiler_params=_params(("parallel",)),
    )(qc_arr, sums, got)


def _share_carry(grads, shapes):
    nw = len(grads)

    def copies(ops, out, send_sems, recv_sems):
        x, y, c, _ = _position()
        res = []
        for w in range(nw):
            hr = shapes[w][0] // 2
            mine = out[w].at[pl.ds(c * hr, hr), :]
            res.append(_remote(mine, mine, send_sems, recv_sems, w, (x, y, 1 - c)))
        return res

    return _Carry(grads, _same(grads), {i: i for i in range(nw)}, nw, copies)


def _place_block(packed, me):
    return lax.dynamic_update_slice(jnp.zeros((N_DEV,) + packed.shape, F32), packed[None], (me, 0, 0))


def _exchange_carry(blocks):
    def copies(ops, res, send_sems, recv_sems):
        x, y, c, _ = _position()
        mine = res[0].at[4 * x + 2 * y + c]
        out = []
        for k in range(1, N_DEV):
            to = ((1 - x) if k & 4 else x, (1 - y) if k & 2 else y, (1 - c) if k & 1 else c)
            out.append(_remote(mine, mine, send_sems, recv_sems, k - 1, to))
        return out

    return _Carry([blocks], _same([blocks]), {0: 0}, N_DEV - 1, copies)


def _sum_blocks(name, blocks):
    n, rows, lanes = blocks.shape
    tr = _tile(rows, 512, 8)

    def body(b_ref, o_ref):
        acc = b_ref[0]
        for k in range(1, n):
            acc = acc + b_ref[k]
        o_ref[...] = acc

    return pl.pallas_call(
        body,
        name=name,
        grid=(rows // tr,),
        in_specs=[pl.BlockSpec((n, tr, lanes), lambda i: (0, i, 0))],
        out_specs=pl.BlockSpec((tr, lanes), lambda i: (i, 0)),
        out_shape=jax.ShapeDtypeStruct((rows, lanes), F32),
        compiler_params=_params(("parallel",)),
    )(blocks)


def _adamw(name, w, g, m, v):
    def fn(rows, pars):
        wv, gv, mv, vv = rows
        m_new = ADAM_B1 * mv + (1.0 - ADAM_B1) * gv
        v_new = ADAM_B2 * vv + (1.0 - ADAM_B2) * (gv * gv)
        m_hat = m_new / (1.0 - ADAM_B1 ** ADAM_STEP)
        v_hat = v_new / (1.0 - ADAM_B2 ** ADAM_STEP)
        delta = -ADAM_LR * (m_hat / (jnp.sqrt(v_hat) + ADAM_EPS) + ADAM_WD * wv)
        return [delta, m_new, v_new, gv], []

    c = w.shape[1]
    return _rowwise(name, fn, [w, g, m, v], [], [(c, F32)] * 4, tr=256)


def _pack(arrays):
    rows = []
    for a in arrays:
        flat = a.reshape(-1).astype(F32)
        pad = (-flat.shape[0]) % LANES
        rows.append(jnp.pad(flat, (0, pad)).reshape(-1, LANES))
    stacked = jnp.concatenate(rows, axis=0)
    pad_rows = (-stacked.shape[0]) % 8
    return jnp.pad(stacked, ((0, pad_rows), (0, 0)))


def _unpack(packed, shapes):
    out, r = [], 0
    for s in shapes:
        n = math.prod(s)
        nr = -(-n // LANES)
        out.append(packed[r:r + nr].reshape(-1)[:n].reshape(s))
        r += nr
    return out


BIG = ["w1_gate", "w1_up", "w1_down", "w_in", "ssm_w_glu", "w_out", "w2_gate", "w2_up", "w2_down", "w_ple_gate",
       "w_ple_proj"]
KIND = {"w1_gate": "col", "w1_up": "col", "w1_down": "row", "w_in": "col", "ssm_w_glu": "row", "w_out": "row",
        "w2_gate": "col", "w2_up": "col", "w2_down": "row", "w_ple_gate": "row", "w_ple_proj": "col"}
SMALL = ["norm_ffn1", "norm_mix", "ssm_log_dt", "ssm_a_re", "ssm_a_im", "ssm_b_re", "ssm_b_im", "ssm_c_re", "ssm_c_im",
         "ssm_d", "gmlp_norm_v", "gmlp_w_s", "gmlp_b_s", "norm_ssm_out", "norm_gmlp_out", "norm_ffn2", "norm_ple",
         "norm_final"]
WEIGHTS = ["norm_ffn1", "w1_gate", "w1_up", "w1_down", "norm_mix", "w_in", "ssm_log_dt", "ssm_a_re", "ssm_a_im",
           "ssm_b_re", "ssm_b_im", "ssm_c_re", "ssm_c_im", "ssm_d", "ssm_w_glu", "gmlp_norm_v", "gmlp_w_s", "gmlp_b_s",
           "norm_ssm_out", "norm_gmlp_out", "w_out", "norm_ffn2", "w2_gate", "w2_up", "w2_down", "norm_ple",
           "w_ple_gate", "w_ple_proj", "norm_final"]


class _Trip:
    def __init__(self, names, arrays, carry):
        self.names, self.arrays, self.carry = names, arrays, carry


class _Reducer:
    def __init__(self, shard_shape, c_arr, qc_arr):
        self.shard_shape, self.c_arr, self.qc_arr = shard_shape, c_arr, qc_arr
        self.halves = {}

    def swap(self, names, grads):
        kinds = [KIND[n] for n in names]
        shapes = [self.shard_shape[n] for n in names]
        return _Trip(names, grads, _pairs_carry(grads, kinds, shapes))

    def send(self, trip, swapped):
        shapes = [self.shard_shape[n] for n in trip.names]
        sums = [_pair_sum("pair_sum_" + n, g, s, KIND[n], sh, self.c_arr)
                for n, g, s, sh in zip(trip.names, trip.arrays, swapped, shapes)]
        return _Trip(trip.names, sums, _scatter_carry(sums, shapes))

    def end(self, trip, got):
        for n, ps, g in zip(trip.names, trip.arrays, got):
            self.halves[n] = _owner_sum("owner_sum_" + n, ps, g, self.shard_shape[n], self.qc_arr)


def _ride(*carries):
    present = [c for c in carries if c is not None]
    joined = functools.reduce(_join, present) if present else None

    def split(results):
        out, at = [], 0
        for c in carries:
            n = len(c.out_shapes) if c is not None else 0
            out.append(list(results[at:at + n]))
            at += n
        return out

    return joined, split


def _step(x, p, tgt, w, m, v):
    d_model = x.shape[1]
    d_ssm = w["ssm_d"].shape[1]
    n_groups = d_ssm // SSM_GROUP
    row = lambda a: a.reshape(1, -1)

    xi, yi, ci = lax.axis_index("x"), lax.axis_index("y"), lax.axis_index("c")
    c_arr = jnp.reshape(ci, (1,)).astype(jnp.int32)
    q_arr = jnp.reshape(2 * xi + yi, (1,)).astype(jnp.int32)
    qc_arr = jnp.stack([2 * xi + yi, ci]).astype(jnp.int32)
    shard_shape = {n: w[n].shape for n in BIG}
    full = {n: _cast_into_gathered("cast_" + n, w[n], KIND[n], q_arr) for n in BIG}

    def gather(stage, names):
        return stage([full[n] for n in names], [KIND[n] for n in names], [shard_shape[n] for n in names])

    def gathered(names, arrays):
        full.update(zip(names, arrays))

    zeroth, first, second, third = ["w1_gate"], ["w1_up"], ["w1_down", "w_in"], ["w2_gate"]
    fourth, fifth, sixth = ["ssm_w_glu", "w_out"], ["w2_up"], ["w2_down", "w_ple_gate", "w_ple_proj"]

    def two_stages(passed_on, landing):
        return _join(gather(_gather_d2d_carry, passed_on), gather(_gather_ici_carry, landing)), passed_on + landing

    gathered(zeroth, _comm_call("gather_zeroth_ici", gather(_gather_ici_carry, zeroth)))
    gathered(zeroth, _comm_call("gather_zeroth_d2d", gather(_gather_d2d_carry, zeroth)))
    n1 = _rms_fwd("ffn1_norm", x, w["norm_ffn1"])
    gate1, landed = _mm_nn("ffn1_gate", n1, full["w1_gate"], BF16, tm=1024, tn=512, tk=2048,
                           carry=gather(_gather_ici_carry, first))
    gathered(first, landed)
    gathered(first, _comm_call("gather_first_d2d", gather(_gather_d2d_carry, first)))
    (a1, up1), landed = _ffn_up("ffn1", n1, full["w1_up"], gate1, gather(_gather_ici_carry, second))
    gathered(second, landed)
    gathered(second, _comm_call("gather_second_d2d", gather(_gather_d2d_carry, second)))
    h1, landed = _ffn_down("ffn1", a1, full["w1_down"], x, gather(_gather_ici_carry, third))
    gathered(third, landed)
    ffn1 = (n1, a1, gate1, up1)
    nm = _rms_fwd("mix_norm", h1, w["norm_mix"])
    carry, names = two_stages(third, fourth)
    z, landed = _mm_nn("in_proj", nm, full["w_in"], F32, tm=1024, tn=512, tk=2048, carry=carry)
    gathered(names, landed)

    disc, disc_vjp = jax.vjp(_ssm_discretize, w["ssm_log_dt"][0], w["ssm_a_re"], w["ssm_a_im"], w["ssm_b_re"],
                             w["ssm_b_im"])
    abar_r, abar_i, bbar_r, bbar_i = disc
    nb = n_groups // GROUPS_PER_BLOCK
    wb = jnp.concatenate([_blockdiag_in(bbar_r), _blockdiag_in(bbar_i)], axis=-1).astype(BF16)
    wc = jnp.concatenate([_blockdiag_out(w["ssm_c_re"]), -_blockdiag_out(w["ssm_c_im"])], axis=1).astype(BF16)
    abar = jnp.concatenate([abar_r.reshape(nb, -1), abar_i.reshape(nb, -1)], axis=-1)
    abar_conj = jnp.concatenate([abar_r.reshape(nb, -1), -abar_i.reshape(nb, -1)], axis=-1)
    carry, names = two_stages(fourth, fifth)
    (states, y_pre, yg), landed = _ssm_scan_fwd(z, wb, wc, abar, w["ssm_d"], carry)
    gathered(names, landed)
    q = _mm_nn("glu_proj", yg, full["ssm_w_glu"], F32, tm=1024, tn=1024, tk=1024)

    def glu_norm(rows, pars):
        yv = _gelu(rows[0]) * _sigmoid(rows[1])
        yh, _ = _rms_stats(yv)
        return [yh * pars[0]], []

    yn_ssm = _rowwise("ssm_glu_norm", glu_norm, [y_pre, q], [w["norm_ssm_out"]], [(d_ssm, BF16)])[0]

    tril = jnp.tril(jnp.ones((CHUNK, CHUNK), dtype=bool))
    wm = jnp.where(tril[None], w["gmlp_w_s"], 0.0).astype(BF16)
    bias = jnp.repeat(w["gmlp_b_s"].T, GMLP_HEAD, axis=1)
    yn_gmlp = _gmlp_fwd(z, w["gmlp_norm_v"], wm, bias, w["norm_gmlp_out"])
    ycat = jnp.concatenate([yn_ssm, yn_gmlp], axis=1)
    h2, landed = _mm_nn("out_proj", ycat, full["w_out"], F32, res=h1, alpha=1.0, tm=512, tn=1024, tk=2048,
                        carry=gather(_gather_d2d_carry, fifth))
    gathered(fifth, landed)

    n2 = _rms_fwd("ffn2_norm", h2, w["norm_ffn2"])
    (a2, gate2, up2), landed = _ffn_gateup("ffn2", n2, full["w2_gate"], full["w2_up"],
                                           gather(_gather_ici_carry, sixth))
    gathered(sixth, landed)
    gathered(sixth, _comm_call("gather_sixth_d2d", gather(_gather_d2d_carry, sixth)))
    h3 = _ffn_down("ffn2", a2, full["w2_down"], h2)
    ffn2 = (n2, a2, gate2, up2)
    npl = _rms_fwd("ple_norm", h3, w["norm_ple"])
    pp = _mm_nn("ple_proj", p, full["w_ple_proj"], F32, tm=1024, tn=1024, tk=2048)
    dgq, dpp, dh4, g_norm_final, loss_part = _ple_head(npl, full["w_ple_gate"], h3, pp, tgt, row(w["norm_final"]))
    reducer = _Reducer(shard_shape, c_arr, qc_arr)
    (g_w_ple_proj,) = _mm_tn("ple_dwproj", p, [dpp], BF16, tm=256, tn=1024, tk=4096)
    (g_w_ple_gate,) = _mm_tn("ple_dwgate", npl, [dgq], BF16, tm=512, tn=1024, tk=4096)
    ple = reducer.swap(["w_ple_gate", "w_ple_proj"], [g_w_ple_gate, g_w_ple_proj])
    dnpl, swapped = _mm_nt_sum("ple_dnorm_in", [dgq], [full["w_ple_gate"]], F32, tm=512, tn=1024, tk=2048,
                               carry=ple.carry)
    ple = reducer.send(ple, swapped)
    dh3, dh3_b, g_norm_ple = _rms_bwd("ple_dnorm", h3, w["norm_ple"], [dnpl], dh4, 0.5)

    dh2, dh2_b, g_norm_ffn2, got, _, up2_w = _ffn_bwd(
        "ffn2", ["w2_gate", "w2_up", "w2_down"], h2, w["norm_ffn2"], full["w2_gate"], full["w2_up"], full["w2_down"],
        ffn2, dh3, dh3_b, 1.0, reducer, riding=ple.carry, last_hop_later=True)
    reducer.end(ple, got)

    dycat = _mm_nt_sum("out_dproj", [dh2_b], [full["w_out"]], F32, tm=512, tn=1024, tk=2048)
    (g_w_out,) = _mm_tn("out_dw", ycat, [dh2_b], BF16, tm=512, tn=1024, tk=4096)

    dzu, dzv, g_norm_gmlp_out, g_gmlp_norm_v, g_wm, g_s = _gmlp_bwd(z, dycat, w["gmlp_norm_v"], wm, bias,
                                                                  w["norm_gmlp_out"])
    g_gmlp_w_s = jnp.where(tril[None], g_wm, 0.0)
    g_gmlp_b_s = g_s.reshape(CHUNK, -1, GMLP_HEAD).sum(axis=-1).T

    def glu_bwd(rows, pars):
        dyn, ypre, qv = rows
        ygv = _gelu(ypre)
        sg = _sigmoid(qv)
        dy, dg = _rms_backward(ygv * sg, pars[0], dyn)
        return [dy * ygv * sg * (1.0 - sg), dy * sg], [dg]

    dq, dyg_part, g_norm_ssm_out = _rowwise("ssm_dglu", glu_bwd, [(dycat, d_ssm, 0), y_pre, q], [w["norm_ssm_out"]],
                                            [(d_ssm, BF16), (d_ssm, F32)], [(1, d_ssm)])
    dyg_proj = _mm_nt_sum("glu_dproj", [dq], [full["ssm_w_glu"]], F32, tm=1024, tn=1024, tk=1024)
    (g_ssm_w_glu,) = _mm_tn("glu_dw", yg, [dq], BF16, tm=512, tn=1024, tk=4096)

    def gelu_bwd(rows, pars):
        return [(rows[0] + rows[1]) * _gelu_grad(rows[2])], []

    dy_pre = _rowwise("ssm_dgelu", gelu_bwd, [dyg_part, dyg_proj, y_pre], [], [(d_ssm, F32)])[0]
    mixers = reducer.swap(["w_out", "ssm_w_glu"], [g_w_out, g_ssm_w_glu])
    me = 4 * xi + 2 * yi + ci
    small = {"gmlp_norm_v": g_gmlp_norm_v, "gmlp_w_s": g_gmlp_w_s, "gmlp_b_s": g_gmlp_b_s,
             "norm_gmlp_out": g_norm_gmlp_out, "norm_ssm_out": g_norm_ssm_out, "norm_ffn2": g_norm_ffn2,
             "norm_ple": g_norm_ple, "norm_final": g_norm_final}
    before_scan = [n for n in SMALL if n in small]
    scan_blocks = _place_block(_pack([small[n] for n in before_scan] + [loss_part[:, :1]]), me)
    carry, split = _ride(up2_w.carry, mixers.carry, _exchange_carry(scan_blocks))
    (dz_ssm, g_wb, g_wc, g_abar, g_ssm_d), results = _ssm_scan_bwd(z, dy_pre, states, wb, wc, abar_conj, w["ssm_d"],
                                                                  carry)
    got, swapped, (scan_blocks,) = split(results)
    reducer.end(up2_w, got)
    mixers = reducer.send(mixers, swapped)
    g_abar = jnp.transpose(g_abar, (1, 0, 2)).reshape(nb, -1)
    sw = g_abar.shape[-1] // 2
    g_bbar_r = _blockdiag_in_grad(g_wb[..., :sw], SSM_STATE, SSM_GROUP)
    g_bbar_i = _blockdiag_in_grad(g_wb[..., sw:], SSM_STATE, SSM_GROUP)
    g_ssm_c_re = _blockdiag_out_grad(g_wc[:, :sw, :], SSM_GROUP, SSM_STATE)
    g_ssm_c_im = -_blockdiag_out_grad(g_wc[:, sw:, :], SSM_GROUP, SSM_STATE)
    g_abar_r = g_abar[..., :sw].reshape(n_groups, SSM_STATE)
    g_abar_i = g_abar[..., sw:].reshape(n_groups, SSM_STATE)
    g_ssm_log_dt, g_ssm_a_re, g_ssm_a_im, g_ssm_b_re, g_ssm_b_im = disc_vjp((g_abar_r, g_abar_i, g_bbar_r, g_bbar_i))

    dz = jnp.concatenate([dz_ssm, dzu, dzv], axis=1)
    (g_w_in,), got = _mm_tn("in_dw", nm, [dz], BF16, tm=512, tn=1536, tk=4096, carry=mixers.carry)
    reducer.end(mixers, got)
    in_w = reducer.swap(["w_in"], [g_w_in])
    dnm, swapped = _mm_nt_sum("in_dproj", [dz], [full["w_in"]], F32, tm=1024, tn=1024, tk=3072, carry=in_w.carry)
    in_w = reducer.send(in_w, swapped)
    dh1, dh1_b, g_norm_mix = _rms_bwd("mix_dnorm", h1, w["norm_mix"], [dnm], dh2, 0.5)
    small = {"norm_mix": g_norm_mix, "ssm_log_dt": g_ssm_log_dt, "ssm_a_re": g_ssm_a_re,
             "ssm_a_im": g_ssm_a_im, "ssm_b_re": g_ssm_b_re, "ssm_b_im": g_ssm_b_im, "ssm_c_re": g_ssm_c_re,
             "ssm_c_im": g_ssm_c_im, "ssm_d": g_ssm_d}
    after_scan = [n for n in SMALL if n in small]
    ffn_blocks = _place_block(_pack([small[n] for n in after_scan]), me)
    dx, _, g_norm_ffn1, got, (ffn_blocks,), _ = _ffn_bwd(
        "ffn1", ["w1_gate", "w1_up", "w1_down"], x, w["norm_ffn1"], full["w1_gate"], full["w1_up"], full["w1_down"],
        ffn1, dh1, dh1_b, 1.0, reducer, riding=in_w.carry, riding_dwd=_exchange_carry(ffn_blocks))
    reducer.end(in_w, got)
    scan_grads = _unpack(_sum_blocks("sum_before_scan", scan_blocks), [w[n].shape for n in before_scan] + [(1,)])
    loss = scan_grads[-1].reshape(())
    early = before_scan + after_scan
    early_grads = scan_grads[:-1] + _unpack(_sum_blocks("sum_after_scan", ffn_blocks), [w[n].shape for n in after_scan])

    halves = [reducer.halves[n] for n in BIG]
    late_blocks = _place_block(_pack([g_norm_ffn1]), me)
    last = _join(_share_carry(halves, [shard_shape[n] for n in BIG]), _exchange_carry(late_blocks))
    *shared, late_blocks = _comm_call("share_halves", last)
    grad = dict(zip(BIG, shared))
    grad.update(zip(early, early_grads))
    grad["norm_ffn1"] = _unpack(_sum_blocks("sum_first_norm", late_blocks), [w["norm_ffn1"].shape])[0]

    small_shapes = [w[n].shape for n in SMALL]
    delta, new_m, new_v = {}, {}, {}
    for n in BIG:
        delta[n], new_m[n], new_v[n], grad[n] = _adamw("adamw_" + n, w[n], grad[n], m[n], v[n])
    d_p, m_p, v_p, _ = _adamw("adamw_small", _pack([w[n] for n in SMALL]), _pack([grad[n] for n in SMALL]),
                              _pack([m[n] for n in SMALL]), _pack([v[n] for n in SMALL]))
    for name_list, packed in ((delta, d_p), (new_m, m_p), (new_v, v_p)):
        for n, a in zip(SMALL, _unpack(packed, small_shapes)):
            name_list[n] = a
    return loss, dx, grad, delta, new_m, new_v


def kernel(x, p, norm_ffn1, w1_gate, w1_up, w1_down, norm_mix, w_in, ssm_log_dt, ssm_a_re, ssm_a_im, ssm_b_re, ssm_b_im, ssm_c_re, ssm_c_im, ssm_d, ssm_w_glu, gmlp_norm_v, gmlp_w_s, gmlp_b_s, norm_ssm_out, norm_gmlp_out, w_out, norm_ffn2, w2_gate, w2_up, w2_down, norm_ple, w_ple_gate, w_ple_proj, norm_final, loss_target, m_norm_ffn1, m_w1_gate, m_w1_up, m_w1_down, m_norm_mix, m_w_in, m_ssm_log_dt, m_ssm_a_re, m_ssm_a_im, m_ssm_b_re, m_ssm_b_im, m_ssm_c_re, m_ssm_c_im, m_ssm_d, m_ssm_w_glu, m_gmlp_norm_v, m_gmlp_w_s, m_gmlp_b_s, m_norm_ssm_out, m_norm_gmlp_out, m_w_out, m_norm_ffn2, m_w2_gate, m_w2_up, m_w2_down, m_norm_ple, m_w_ple_gate, m_w_ple_proj, m_norm_final, v_norm_ffn1, v_w1_gate, v_w1_up, v_w1_down, v_norm_mix, v_w_in, v_ssm_log_dt, v_ssm_a_re, v_ssm_a_im, v_ssm_b_re, v_ssm_b_im, v_ssm_c_re, v_ssm_c_im, v_ssm_d, v_ssm_w_glu, v_gmlp_norm_v, v_gmlp_w_s, v_gmlp_b_s, v_norm_ssm_out, v_norm_gmlp_out, v_w_out, v_norm_ffn2, v_w2_gate, v_w2_up, v_w2_down, v_norm_ple, v_w_ple_gate, v_w_ple_proj, v_norm_final):
    given = dict(locals())
    shapes = {n: given[n].shape for n in WEIGHTS}

    def block(name):
        a = given[name]
        if a.ndim == 1:
            return a.reshape(1, -1)
        return a[0] if a.ndim >= 3 else a

    w = {n: block(n) for n in WEIGHTS}
    m = {n: block("m_" + n) for n in WEIGHTS}
    v = {n: block("v_" + n) for n in WEIGHTS}
    loss, dx, grad, delta, new_m, new_v = _step(x[0], p[0, 0], loss_target[0], w, m, v)
    outs = [loss, dx[None]]
    for tree in (grad, delta, new_m, new_v):
        outs += [tree[n].reshape(shapes[n]) for n in WEIGHTS]
    return tuple(outs)
```

```python
import functools
import math

import jax
import jax.numpy as jnp
from jax import lax
from jax.experimental import pallas as pl
from jax.experimental.pallas import tpu as pltpu

F32 = jnp.float32
BF16 = jnp.bfloat16
EPS = 1e-6
SSM_GROUP = 16
SSM_STATE = 64
GROUPS_PER_BLOCK = 8
GMLP_HEAD = 128
CHUNK = 128
ADAM_LR = 0.001
ADAM_B1 = 0.9
ADAM_B2 = 0.999
ADAM_EPS = 1e-08
ADAM_WD = 0.01
ADAM_STEP = 10
N_CHIPS = 4
N_DEV = 8
LANES = 128
VMEM_LIMIT_BYTES = 56 * 1024 * 1024
MESH = pl.DeviceIdType.MESH
GELU_C = math.sqrt(2.0 / math.pi)
GELU_A = 0.044715

_DOT_DIMS = {
    "nn": (((1,), (0,)), ((), ())),
    "nt": (((1,), (1,)), ((), ())),
    "tn": (((0,), (0,)), ((), ())),
}


def _tile(dim, pref, align):
    if dim <= pref:
        return dim
    t = (pref // align) * align
    while t >= align:
        if dim % t == 0:
            return t
        t -= align
    return dim


def _params(semantics):
    return pltpu.CompilerParams(dimension_semantics=semantics, vmem_limit_bytes=VMEM_LIMIT_BYTES)


def _gelu(x):
    return 0.5 * x * (1.0 + jnp.tanh(GELU_C * (x + GELU_A * x * x * x)))


def _gelu_grad(x):
    t = jnp.tanh(GELU_C * (x + GELU_A * x * x * x))
    return 0.5 * (1.0 + t) + 0.5 * x * (1.0 - t * t) * GELU_C * (1.0 + 3.0 * GELU_A * x * x)


def _sigmoid(x):
    return 1.0 / (1.0 + jnp.exp(-x))


def _dot(a, b, mode):
    return lax.dot_general(a.astype(BF16), b.astype(BF16), _DOT_DIMS[mode], preferred_element_type=F32)


class _Carry:
    def __init__(self, arrays, out_shapes, aliases, n_copies, copies):
        self.arrays = list(arrays)
        self.out_shapes = list(out_shapes)
        self.aliases = dict(aliases)
        self.n_copies = n_copies
        self.copies = copies

    def scratch(self):
        return [pltpu.SemaphoreType.DMA((self.n_copies,)), pltpu.SemaphoreType.DMA((self.n_copies,))]

    def split(self, refs):
        n_in, n_out = len(self.arrays), len(self.out_shapes)
        return refs[:n_in], refs[n_in:n_in + n_out], refs[n_in + n_out], refs[n_in + n_out + 1]

    def start(self, refs):
        for cp in self.copies(*self.split(refs)):
            cp.start()

    def wait(self, refs):
        for cp in self.copies(*self.split(refs)):
            cp.wait()


class _SemRange:
    def __init__(self, sems, offset):
        self.sems, self.offset = sems, offset

    @property
    def at(self):
        return self

    def __getitem__(self, k):
        return self.sems.at[self.offset + k]


def _join(first, second):
    n_in, n_out = len(first.arrays), len(first.out_shapes)
    aliases = dict(first.aliases)
    aliases.update({n_in + i: n_out + o for i, o in second.aliases.items()})

    def copies(ops, res, send_sems, recv_sems):
        return (first.copies(ops[:n_in], res[:n_out], send_sems, recv_sems)
                + second.copies(ops[n_in:], res[n_out:], _SemRange(send_sems, first.n_copies),
                                _SemRange(recv_sems, first.n_copies)))

    return _Carry(first.arrays + second.arrays, first.out_shapes + second.out_shapes, aliases,
                  first.n_copies + second.n_copies, copies)


_ANY = pl.BlockSpec(memory_space=pl.ANY)


def _comm_call(name, carry):
    def body(*refs):
        carry.start(refs)
        carry.wait(refs)

    n_in = len(carry.arrays)
    return pl.pallas_call(
        body,
        name=name,
        in_specs=[_ANY] * n_in,
        out_specs=[_ANY] * len(carry.out_shapes),
        out_shape=carry.out_shapes,
        input_output_aliases=carry.aliases,
        scratch_shapes=carry.scratch(),
    )(*carry.arrays)


def _carried_call(body, carry, *, name, grid, in_specs, out_specs, out_shape, scratch_shapes, semantics, args):
    if carry is None:
        res = pl.pallas_call(body, name=name, grid=grid, in_specs=in_specs, out_specs=out_specs, out_shape=out_shape,
                             scratch_shapes=scratch_shapes, compiler_params=_params(semantics))(*args)
        return res, []
    n_in, n_out, n_scr = len(in_specs), len(out_specs), len(scratch_shapes)
    nci, nco = len(carry.arrays), len(carry.out_shapes)

    def wrapped(*refs):
        ins = refs[:n_in]
        outs = refs[n_in + nci:n_in + nci + n_out]
        scr = refs[n_in + nci + n_out + nco:n_in + nci + n_out + nco + n_scr]
        c_refs = (refs[n_in:n_in + nci] + refs[n_in + nci + n_out:n_in + nci + n_out + nco]
                  + refs[n_in + nci + n_out + nco + n_scr:])
        first = functools.reduce(jnp.logical_and, [pl.program_id(d) == 0 for d in range(len(grid))])
        last = functools.reduce(jnp.logical_and, [pl.program_id(d) == grid[d] - 1 for d in range(len(grid))])

        @pl.when(first)
        def _():
            carry.start(c_refs)

        body(*ins, *outs, *scr)

        @pl.when(last)
        def _():
            carry.wait(c_refs)

    res = pl.pallas_call(
        wrapped,
        name=name,
        grid=grid,
        in_specs=list(in_specs) + [_ANY] * nci,
        out_specs=list(out_specs) + [_ANY] * nco,
        out_shape=list(out_shape) + carry.out_shapes,
        input_output_aliases={n_in + i: n_out + o for i, o in carry.aliases.items()},
        scratch_shapes=list(scratch_shapes) + carry.scratch(),
        compiler_params=_params(("arbitrary",) * len(grid)),
    )(*args, *carry.arrays)
    return res[:n_out], res[n_out:]


def _matmul(name, mode, a_list, b_list, products, out_dtypes, epilogue, extras=(), tm=512, tn=512, tk=2048,
            carry=None, n_part=(0, 1)):
    a0, b0 = a_list[0], b_list[0]
    if mode == "tn":
        k_dim, m_dim = a0.shape
    else:
        m_dim, k_dim = a0.shape
    n_dim = (b0.shape[0] if mode == "nt" else b0.shape[1]) // n_part[1]
    tm = _tile(m_dim, tm, LANES)
    tn = _tile(n_dim, tn, LANES)
    tk = _tile(k_dim, tk, LANES)
    nk = k_dim // tk
    j0 = n_part[0] * (n_dim // tn)
    chunk = 2 * LANES if (nk == 1 and epilogue is not _identity and tn % (2 * LANES) == 0) else tn
    n_acc = 1 + max(p[2] for p in products)
    na, nb, ne, no = len(a_list), len(b_list), len(extras), len(out_dtypes)

    if mode == "tn":
        a_spec = pl.BlockSpec((tk, tm), lambda i, j, k: (k, i))
    else:
        a_spec = pl.BlockSpec((tm, tk), lambda i, j, k: (i, k))
    if mode == "nt":
        b_spec = pl.BlockSpec((tn, tk), lambda i, j, k: (j0 + j, k))
    else:
        b_spec = pl.BlockSpec((tk, tn), lambda i, j, k: (k, j0 + j))
    t_spec = pl.BlockSpec((tm, tn), lambda i, j, k: (i, j))

    def body(*refs):
        a_refs = refs[:na]
        b_refs = refs[na:na + nb]
        e_refs = refs[na + nb:na + nb + ne]
        o_refs = refs[na + nb + ne:na + nb + ne + no]
        acc_refs = refs[na + nb + ne + no:]

        def partial_sums(cols):
            sums = [None] * n_acc
            for ai, bi, ci in products:
                b = b_refs[bi][cols, :] if mode == "nt" else b_refs[bi][:, cols]
                d = _dot(a_refs[ai][...], b, mode)
                sums[ci] = d if sums[ci] is None else sums[ci] + d
            return sums

        def finish(accs, cols):
            outs = epilogue(accs, [e[:, cols] for e in e_refs])
            for o_ref, o in zip(o_refs, outs):
                o_ref[:, cols] = o.astype(o_ref.dtype)

        if nk == 1:
            for c0 in range(0, tn, chunk):
                finish(partial_sums(slice(c0, c0 + chunk)), slice(c0, c0 + chunk))
        else:
            sums = partial_sums(slice(None))
            finish = functools.partial(finish, cols=slice(None))
            k = pl.program_id(2)

            @pl.when(k == 0)
            def _():
                for acc, s in zip(acc_refs, sums):
                    acc[...] = s

            @pl.when(k > 0)
            def _():
                for acc, s in zip(acc_refs, sums):
                    acc[...] += s

            @pl.when(k == nk - 1)
            def _():
                finish([acc[...] for acc in acc_refs])

    scratch = [pltpu.VMEM((tm, tn), F32) for _ in range(n_acc)] if nk > 1 else []
    outs, carried = _carried_call(
        body, carry,
        name=name,
        grid=(m_dim // tm, n_dim // tn, nk),
        in_specs=[a_spec] * na + [b_spec] * nb + [t_spec] * ne,
        out_specs=[t_spec] * no,
        out_shape=[jax.ShapeDtypeStruct((m_dim, n_dim), dt) for dt in out_dtypes],
        scratch_shapes=scratch,
        semantics=("parallel", "parallel", "arbitrary"),
        args=[*a_list, *b_list, *extras],
    )
    return (outs, carried) if carry else outs


def _identity(accs, extras):
    return accs


def _single(result, carry):
    return (result[0][0], result[1]) if carry else result[0]


def _mm_nn(name, a, b, out_dtype, res=None, alpha=1.0, carry=None, **tiles):
    if res is None:
        return _single(_matmul(name, "nn", [a], [b], [(0, 0, 0)], [out_dtype], _identity, carry=carry, **tiles), carry)

    def epilogue(accs, extras):
        return [extras[0] + alpha * accs[0]]

    return _single(_matmul(name, "nn", [a], [b], [(0, 0, 0)], [out_dtype], epilogue, extras=(res,), carry=carry,
                           **tiles), carry)


def _mm_nt_sum(name, a_list, b_list, out_dtype, carry=None, n_part=(0, 1), **tiles):
    products = [(i, i, 0) for i in range(len(a_list))]
    return _single(_matmul(name, "nt", a_list, b_list, products, [out_dtype], _identity, carry=carry, n_part=n_part,
                           **tiles), carry)


def _mm_tn(name, a, b_list, out_dtype, carry=None, **tiles):
    products = [(0, i, i) for i in range(len(b_list))]
    return _matmul(name, "tn", [a], b_list, products, [out_dtype] * len(b_list), _identity, carry=carry, **tiles)


def _rowwise(name, fn, row_ins, par_ins, row_outs, acc_outs=(), tr=512):
    first = row_ins[0][0] if isinstance(row_ins[0], tuple) else row_ins[0]
    t_dim = first.shape[0]
    tr = _tile(t_dim, tr, 16)
    arrays, specs = [], []
    for r in row_ins:
        if isinstance(r, tuple):
            arr, width, blk = r
            specs.append(pl.BlockSpec((tr, width), lambda i, blk=blk: (i, blk)))
        else:
            arr = r
            specs.append(pl.BlockSpec((tr, arr.shape[1]), lambda i: (i, 0)))
        arrays.append(arr)
    for p in par_ins:
        arrays.append(p)
        specs.append(pl.BlockSpec(p.shape, lambda i, nd=p.ndim: (0,) * nd))
    nr, npar, nro, nacc = len(row_ins), len(par_ins), len(row_outs), len(acc_outs)

    def body(*refs):
        rows = [r[...] for r in refs[:nr]]
        pars = [p[...] for p in refs[nr:nr + npar]]
        o_refs = refs[nr + npar:nr + npar + nro]
        acc_refs = refs[nr + npar + nro:]
        outs, accs = fn(rows, pars)
        for o_ref, o in zip(o_refs, outs):
            o_ref[...] = o.astype(o_ref.dtype)
        if nacc:
            @pl.when(pl.program_id(0) == 0)
            def _():
                for a_ref in acc_refs:
                    a_ref[...] = jnp.zeros_like(a_ref)

            for a_ref, a in zip(acc_refs, accs):
                a_ref[...] += a

    out_shape = [jax.ShapeDtypeStruct((t_dim, c), dt) for c, dt in row_outs]
    out_shape += [jax.ShapeDtypeStruct(s, F32) for s in acc_outs]
    out_specs = [pl.BlockSpec((tr, c), lambda i: (i, 0)) for c, _ in row_outs]
    out_specs += [pl.BlockSpec(s, lambda i: (0, 0)) for s in acc_outs]
    return pl.pallas_call(
        body,
        name=name,
        grid=(t_dim // tr,),
        in_specs=specs,
        out_specs=out_specs,
        out_shape=out_shape,
        compiler_params=_params(("arbitrary",)),
    )(*arrays)


def _rms_stats(x):
    r = lax.rsqrt(jnp.mean(x * x, axis=-1, keepdims=True) + EPS)
    return x * r, r


def _rms_backward(x, g, dy):
    xh, r = _rms_stats(x)
    a = dy * g
    dx = r * (a - xh * jnp.mean(a * xh, axis=-1, keepdims=True))
    return dx, jnp.sum(dy * xh, axis=0, keepdims=True)


def _rms_fwd(name, x, g):
    def fn(rows, pars):
        xh, _ = _rms_stats(rows[0])
        return [xh * pars[0]], []

    return _rowwise(name, fn, [x], [g], [(x.shape[1], BF16)])[0]


def _rms_bwd(name, x, g, dy_parts, dres, scale):
    def fn(rows, pars):
        dy = rows[2] if len(rows) == 3 else jnp.concatenate(rows[2:], axis=1)
        dx, dg = _rms_backward(rows[0], pars[0], dy)
        tot = rows[1] + dx
        return [tot, scale * tot], [dg]

    d = x.shape[1]
    return _rowwise(name, fn, [x, dres, *dy_parts], [g], [(d, F32), (d, BF16)], [(1, d)], tr=256)


def _cast_into_gathered(name, w, kind, q_arr):
    rs, cs = w.shape
    tr = _tile(rs, 256, 16)
    nr = rs // tr

    def body(q_ref, w_ref, o_ref):
        o_ref[...] = w_ref[...].astype(BF16)

    if kind == "row":
        o_spec = pl.BlockSpec((tr, cs), lambda i, q_ref: (q_ref[0] * nr + i, 0))
    else:
        o_spec = pl.BlockSpec((tr, cs), lambda i, q_ref: (i, q_ref[0]))
    return pl.pallas_call(
        body,
        name=name,
        grid_spec=pltpu.PrefetchScalarGridSpec(num_scalar_prefetch=1, grid=(nr,),
                                               in_specs=[pl.BlockSpec((tr, cs), lambda i, q_ref: (i, 0))],
                                               out_specs=o_spec),
        out_shape=jax.ShapeDtypeStruct(_full_shape(kind, (rs, cs)), BF16),
        compiler_params=_params(("parallel",)),
    )(q_arr, w)


def _swiglu_tiles(gate, up):
    s = _sigmoid(gate)
    silu = gate * s
    return [silu * up, up * (s * (1.0 + gate * (1.0 - s))), silu]


def _ffn_gateup(tag, n, wg, wu, carry):
    def act(accs, extras):
        return _swiglu_tiles(accs[0], accs[1])

    return _matmul(tag + "_gateup", "nn", [n], [wg, wu], [(0, 0, 0), (0, 1, 1)], [BF16] * 3, act,
                   tm=1024, tn=512, tk=2048, carry=carry)


def _ffn_up(tag, n, wu, gate, carry):
    def act(accs, extras):
        return _swiglu_tiles(extras[0].astype(F32), accs[0])

    return _matmul(tag + "_up", "nn", [n], [wu], [(0, 0, 0)], [BF16] * 3, act, extras=(gate,),
                   tm=1024, tn=512, tk=2048, carry=carry)


def _ffn_down(tag, a, wd, h, carry=None):
    return _mm_nn(tag + "_down", a, wd, F32, res=h, alpha=0.5, tm=1024, tn=512, tk=5632, carry=carry)


def _ffn_bwd(tag, names, h, g, wg, wu, wd, saved, dh, dfb, next_scale, reducer, riding=None, riding_dwd=None,
             last_hop_later=False):
    n, a, da_dgate, da_dup = saved

    def act_bwd(accs, extras):
        return [accs[0] * extras[0].astype(F32), accs[0] * extras[1].astype(F32)]

    dact = _matmul(tag + "_dact", "nt", [dfb], [wd], [(0, 0, 0)], [BF16, BF16], act_bwd, extras=(da_dgate, da_dup),
                   tm=1024, tn=512, tk=2048, carry=riding)
    (dgp, du), rode = dact if riding else (dact, [])
    dwd_call = _mm_tn(tag + "_dwd", a, [dfb], BF16, tm=512, tn=2048, tk=4096, carry=riding_dwd)
    (dwd,), rode_dwd = dwd_call if riding_dwd else (dwd_call, [])
    down = reducer.swap(names[2:], [dwd])
    (dwg,), swapped = _mm_tn(tag + "_dwg", n, [dgp], BF16, tm=512, tn=1408, tk=4096, carry=down.carry)
    down = reducer.send(down, swapped)
    gate_w = reducer.swap(names[:1], [dwg])
    carry, split = _ride(down.carry, gate_w.carry)
    (dwu,), results = _mm_tn(tag + "_dwu", n, [du], BF16, tm=512, tn=1408, tk=4096, carry=carry)
    got, swapped = split(results)
    reducer.end(down, got)
    gate_w = reducer.send(gate_w, swapped)
    up_w = reducer.swap(names[1:2], [dwu])
    carry, split = _ride(gate_w.carry, up_w.carry)
    halves = 1 if last_hop_later else 2
    dn_lo, results = _mm_nt_sum(tag + "_dn_lo", [dgp, du], [wg, wu], F32, tm=1024, tn=1024, tk=1408, carry=carry,
                                n_part=(0, halves))
    got, swapped = split(results)
    reducer.end(gate_w, got)
    up_w = reducer.send(up_w, swapped)
    dn = [dn_lo]
    if not last_hop_later:
        dn_hi, got = _mm_nt_sum(tag + "_dn_hi", [dgp, du], [wg, wu], F32, tm=1024, tn=1024, tk=1408,
                                carry=up_w.carry, n_part=(1, 2))
        reducer.end(up_w, got)
        dn.append(dn_hi)
    dh_in, dh_in_b, dg = _rms_bwd(tag + "_dnorm", h, g, dn, dh, next_scale)
    return dh_in, dh_in_b, dg, rode, rode_dwd, (up_w if last_hop_later else None)


def _ssm_discretize(log_dt, a_re, a_im, b_re, b_im):
    dt = jnp.exp(log_dt)[:, None]
    lr = jnp.minimum(a_re, -1e-4)
    li = a_im
    mag = jnp.exp(lr * dt)
    ang = li * dt
    abar_r = mag * jnp.cos(ang)
    abar_i = mag * jnp.sin(ang)
    den = lr * lr + li * li
    xr = abar_r - 1.0
    xi = abar_i
    zr = (xr * lr + xi * li) / den
    zi = (xi * lr - xr * li) / den
    bbar_r = zr[..., None] * b_re - zi[..., None] * b_im
    bbar_i = zr[..., None] * b_im + zi[..., None] * b_re
    return abar_r, abar_i, bbar_r, bbar_i


def _blockdiag_in(b):
    g, n, p = b.shape
    nb = g // GROUPS_PER_BLOCK
    eye = jnp.eye(GROUPS_PER_BLOCK, dtype=b.dtype)
    b4 = b.reshape(nb, GROUPS_PER_BLOCK, n, p)
    return jnp.einsum("sgnp,gh->sgphn", b4, eye).reshape(nb, GROUPS_PER_BLOCK * p, GROUPS_PER_BLOCK * n)


def _blockdiag_in_grad(gw, n, p):
    nb = gw.shape[0]
    eye = jnp.eye(GROUPS_PER_BLOCK, dtype=gw.dtype)
    g5 = gw.reshape(nb, GROUPS_PER_BLOCK, p, GROUPS_PER_BLOCK, n)
    return jnp.einsum("sgphn,gh->sgnp", g5, eye).reshape(nb * GROUPS_PER_BLOCK, n, p)


def _blockdiag_out(c):
    g, p, n = c.shape
    nb = g // GROUPS_PER_BLOCK
    eye = jnp.eye(GROUPS_PER_BLOCK, dtype=c.dtype)
    c4 = c.reshape(nb, GROUPS_PER_BLOCK, p, n)
    return jnp.einsum("sgpn,gh->shngp", c4, eye).reshape(nb, GROUPS_PER_BLOCK * n, GROUPS_PER_BLOCK * p)


def _blockdiag_out_grad(gw, p, n):
    nb = gw.shape[0]
    eye = jnp.eye(GROUPS_PER_BLOCK, dtype=gw.dtype)
    g5 = gw.reshape(nb, GROUPS_PER_BLOCK, n, GROUPS_PER_BLOCK, p)
    return jnp.einsum("shngp,gh->sgpn", g5, eye).reshape(nb * GROUPS_PER_BLOCK, p, n)


def _ssm_scan_fwd(z, wb, wc, abar, d, carry=None):
    t_dim = z.shape[0]
    nb, cb, sw2 = wb.shape
    nl = sw2 // LANES
    hl = nl // 2
    tt = _tile(t_dim, 256, 8)
    nt = t_dim // tt

    def body(z_ref, wb_ref, wc_ref, a_ref, d_ref, s_ref, y_ref, yg_ref, drive_ref, st_ref):
        @pl.when(pl.program_id(0) == 0)
        def _():
            st_ref[...] = jnp.zeros_like(st_ref)

        u = z_ref[...]
        ub = u.astype(BF16)
        for b in range(nb):
            drive = _dot(ub[:, b * cb:(b + 1) * cb], wb_ref[b], "nn")
            for l in range(nl):
                drive_ref[l, pl.ds(b, tt, stride=nb), :] = drive[:, l * LANES:(l + 1) * LANES]
        a = a_ref[...]
        chunk = lambda v, l: v[:, l * LANES:(l + 1) * LANES]

        def step(t, state):
            rows = pl.ds(pl.multiple_of(t * nb, nb), nb)
            re, im = [], []
            for l in range(hl):
                ar, ai, sr, si = chunk(a, l), chunk(a, hl + l), state[l], state[hl + l]
                nr = ar * sr - ai * si + drive_ref[l, rows, :]
                ni = ar * si + ai * sr + drive_ref[hl + l, rows, :]
                s_ref[l, rows, :] = nr
                s_ref[hl + l, rows, :] = ni
                re.append(nr)
                im.append(ni)
            return tuple(re + im)

        state = lax.fori_loop(0, tt, step, tuple(st_ref[l] for l in range(nl)), unroll=8)
        for l in range(nl):
            st_ref[l] = state[l]
        parts = []
        for b in range(nb):
            s_b = jnp.concatenate([s_ref[l, pl.ds(b, tt, stride=nb), :] for l in range(nl)], axis=1)
            parts.append(_dot(s_b, wc_ref[b], "nn"))
        y = jnp.concatenate(parts, axis=1) + d_ref[...] * u
        y_ref[...] = y
        yg_ref[...] = _gelu(y).astype(BF16)

    full = lambda a: pl.BlockSpec(a.shape, lambda t, nd=a.ndim: (0,) * nd)
    return _carried_call(
        body, carry,
        name="ssm_scan_fwd",
        grid=(nt,),
        in_specs=[pl.BlockSpec((tt, nb * cb), lambda t: (t, 0)), full(wb), full(wc), full(abar), full(d)],
        out_specs=[
            pl.BlockSpec((nl, tt * nb, LANES), lambda t: (0, t, 0)),
            pl.BlockSpec((tt, nb * cb), lambda t: (t, 0)),
            pl.BlockSpec((tt, nb * cb), lambda t: (t, 0)),
        ],
        out_shape=[
            jax.ShapeDtypeStruct((nl, t_dim * nb, LANES), F32),
            jax.ShapeDtypeStruct((t_dim, nb * cb), F32),
            jax.ShapeDtypeStruct((t_dim, nb * cb), BF16),
        ],
        scratch_shapes=[pltpu.VMEM((nl, tt * nb, LANES), F32), pltpu.VMEM((nl, nb, LANES), F32)],
        semantics=("arbitrary",),
        args=[z, wb, wc, abar, d],
    )


def _ssm_scan_bwd(z, dy, states, wb, wc, abar_conj, d, carry=None):
    t_dim = z.shape[0]
    nb, cb, sw2 = wb.shape
    nl = sw2 // LANES
    hl = nl // 2
    tt = _tile(t_dim, 128, 8)
    nt = t_dim // tt
    edges = states.reshape(nl, nt, tt * nb, LANES)[:, :, (tt - 1) * nb:, :]
    before = jnp.concatenate([jnp.zeros((nl, 1, nb, LANES), F32), edges[:, :-1]], axis=1).reshape(nl, nt * nb, LANES)

    def body(z_ref, dy_ref, s_ref, sp_ref, wb_ref, wc_ref, a_ref, d_ref,
             dz_ref, gwb_ref, gwc_ref, ga_ref, gd_ref, gin_ref, gs_ref, st_ref):
        @pl.when(pl.program_id(0) == 0)
        def _():
            st_ref[...] = jnp.zeros_like(st_ref)
            gwb_ref[...] = jnp.zeros_like(gwb_ref)
            gwc_ref[...] = jnp.zeros_like(gwc_ref)
            ga_ref[...] = jnp.zeros_like(ga_ref)
            gd_ref[...] = jnp.zeros_like(gd_ref)

        u = z_ref[...]
        dyv = dy_ref[...]
        ub = u.astype(BF16)
        dyb = dyv.astype(BF16)
        for b in range(nb):
            gin = _dot(dyb[:, b * cb:(b + 1) * cb], wc_ref[b], "nt")
            for l in range(nl):
                gin_ref[l, pl.ds(b, tt, stride=nb), :] = gin[:, l * LANES:(l + 1) * LANES]
        a = a_ref[...]
        chunk = lambda v, l: v[:, l * LANES:(l + 1) * LANES]

        def step(k, state):
            rows = pl.ds(pl.multiple_of((tt - 1 - k) * nb, nb), nb)
            re, im = [], []
            for l in range(hl):
                ar, ai, gr, gi = chunk(a, l), chunk(a, hl + l), state[l], state[hl + l]
                nr = ar * gr - ai * gi + gin_ref[l, rows, :]
                ni = ar * gi + ai * gr + gin_ref[hl + l, rows, :]
                gs_ref[l, rows, :] = nr
                gs_ref[hl + l, rows, :] = ni
                re.append(nr)
                im.append(ni)
            return tuple(re + im)

        state = lax.fori_loop(0, tt, step, tuple(st_ref[l] for l in range(nl)), unroll=8)
        for l in range(nl):
            st_ref[l] = state[l]

        parts = []
        for b in range(nb):
            cols = slice(b * cb, (b + 1) * cb)
            gs_b = jnp.concatenate([gs_ref[l, pl.ds(b, tt, stride=nb), :] for l in range(nl)], axis=1)
            s_b = jnp.concatenate([s_ref[l, pl.ds(b, tt, stride=nb), :] for l in range(nl)], axis=1)
            parts.append(_dot(gs_b, wb_ref[b], "nt"))
            gwb_ref[b] += _dot(ub[:, cols], gs_b, "tn")
            gwc_ref[b] += _dot(s_b, dyb[:, cols], "tn")
        dz_ref[...] = (jnp.concatenate(parts, axis=1) + d_ref[...] * dyv).astype(BF16)
        gd_ref[...] += jnp.sum(dyv * u, axis=0, keepdims=True)

        row = lax.broadcasted_iota(jnp.int32, (tt * nb, LANES), 0)
        shifted = lambda v: jnp.where(row < nb, 0.0, pltpu.roll(v, nb, 0))
        over_time = lambda v: jnp.sum(v.reshape(tt, nb, LANES), axis=0)
        for l in range(hl):
            g_r, g_i = gs_ref[l], gs_ref[hl + l]
            p_r, p_i = shifted(s_ref[l]), shifted(s_ref[hl + l])
            f_r, f_i = sp_ref[l], sp_ref[hl + l]
            g0_r, g0_i = gs_ref[l, pl.ds(0, nb), :], gs_ref[hl + l, pl.ds(0, nb), :]
            ga_ref[l] += over_time(g_r * p_r + g_i * p_i) + g0_r * f_r + g0_i * f_i
            ga_ref[hl + l] += over_time(g_i * p_r - g_r * p_i) + g0_i * f_r - g0_r * f_i

    rev = lambda t: (nt - 1 - t, 0)
    rev3 = lambda t: (0, nt - 1 - t, 0)
    full = lambda a: pl.BlockSpec(a.shape, lambda t, nd=a.ndim: (0,) * nd)
    return _carried_call(
        body, carry,
        name="ssm_scan_bwd",
        grid=(nt,),
        in_specs=[
            pl.BlockSpec((tt, nb * cb), rev),
            pl.BlockSpec((tt, nb * cb), rev),
            pl.BlockSpec((nl, tt * nb, LANES), rev3),
            pl.BlockSpec((nl, nb, LANES), rev3),
            full(wb), full(wc), full(abar_conj), full(d),
        ],
        out_specs=[
            pl.BlockSpec((tt, nb * cb), rev),
            pl.BlockSpec((nb, cb, sw2), lambda t: (0, 0, 0)),
            pl.BlockSpec((nb, sw2, cb), lambda t: (0, 0, 0)),
            pl.BlockSpec((nl, nb, LANES), lambda t: (0, 0, 0)),
            pl.BlockSpec((1, nb * cb), lambda t: (0, 0)),
        ],
        out_shape=[
            jax.ShapeDtypeStruct((t_dim, nb * cb), BF16),
            jax.ShapeDtypeStruct((nb, cb, sw2), F32),
            jax.ShapeDtypeStruct((nb, sw2, cb), F32),
            jax.ShapeDtypeStruct((nl, nb, LANES), F32),
            jax.ShapeDtypeStruct((1, nb * cb), F32),
        ],
        scratch_shapes=[pltpu.VMEM((nl, tt * nb, LANES), F32), pltpu.VMEM((nl, tt * nb, LANES), F32),
                        pltpu.VMEM((nl, nb, LANES), F32)],
        semantics=("arbitrary",),
        args=[z, dy, states, before, wb, wc, abar_conj, d],
    )


def _gmlp_chunk(zu, zv, gv, wm_ref, bias, n_heads):
    ua = _gelu(zu)
    vg = _gelu(zv)
    xc = vg - jnp.mean(vg, axis=-1, keepdims=True)
    r = lax.rsqrt(jnp.mean(xc * xc, axis=-1, keepdims=True) + EPS)
    vh = xc * r
    vb = (vh * gv).astype(BF16)
    parts = []
    for h in range(n_heads):
        cols = slice(h * GMLP_HEAD, (h + 1) * GMLP_HEAD)
        parts.append(_dot(wm_ref[h], vb[:, cols], "nn"))
    s = jnp.concatenate(parts, axis=1) + bias
    return ua, vh, r, vb, s


def _gmlp_fwd(z, gv, wm, bias, ggo):
    t_dim = z.shape[0]
    dg = gv.shape[1]
    n_heads = dg // GMLP_HEAD
    tr = _tile(t_dim, 256, CHUNK)

    def body(zu_ref, zv_ref, gv_ref, wm_ref, b_ref, ggo_ref, o_ref):
        for ck in range(tr // CHUNK):
            rows = pl.ds(ck * CHUNK, CHUNK)
            ua, _, _, _, s = _gmlp_chunk(zu_ref[rows, :], zv_ref[rows, :], gv_ref[...], wm_ref, b_ref[...], n_heads)
            yh, _ = _rms_stats(ua * s)
            o_ref[rows, :] = (yh * ggo_ref[...]).astype(BF16)

    full = lambda a: pl.BlockSpec(a.shape, lambda i, nd=a.ndim: (0,) * nd)
    return pl.pallas_call(
        body,
        name="gmlp_fwd",
        grid=(t_dim // tr,),
        in_specs=[pl.BlockSpec((tr, dg), lambda i: (i, 1)), pl.BlockSpec((tr, dg), lambda i: (i, 2)),
                  full(gv), full(wm), full(bias), full(ggo)],
        out_specs=pl.BlockSpec((tr, dg), lambda i: (i, 0)),
        out_shape=jax.ShapeDtypeStruct((t_dim, dg), BF16),
        compiler_params=_params(("parallel",)),
    )(z, z, gv, wm, bias, ggo)


def _gmlp_bwd(z, dycat, gv, wm, bias, ggo):
    t_dim = z.shape[0]
    dg = gv.shape[1]
    n_heads = dg // GMLP_HEAD
    tr = _tile(t_dim, 256, CHUNK)

    def body(zu_ref, zv_ref, dy_ref, gv_ref, wm_ref, b_ref, ggo_ref,
             dzu_ref, dzv_ref, dggo_ref, dgv_ref, dwm_ref, dsum_ref):
        @pl.when(pl.program_id(0) == 0)
        def _():
            dggo_ref[...] = jnp.zeros_like(dggo_ref)
            dgv_ref[...] = jnp.zeros_like(dgv_ref)
            dwm_ref[...] = jnp.zeros_like(dwm_ref)
            dsum_ref[...] = jnp.zeros_like(dsum_ref)

        for ck in range(tr // CHUNK):
            rows = pl.ds(ck * CHUNK, CHUNK)
            zu = zu_ref[rows, :]
            zv = zv_ref[rows, :]
            gvv = gv_ref[...]
            ua, vh, r, vb, s = _gmlp_chunk(zu, zv, gvv, wm_ref, b_ref[...], n_heads)
            dy, dggo = _rms_backward(ua * s, ggo_ref[...], dy_ref[rows, :])
            dggo_ref[...] += dggo
            ds = dy * ua
            dsum_ref[...] += ds
            dsb = ds.astype(BF16)
            parts = []
            for h in range(n_heads):
                cols = slice(h * GMLP_HEAD, (h + 1) * GMLP_HEAD)
                dwm_ref[h] += _dot(dsb[:, cols], vb[:, cols], "nt")
                parts.append(_dot(wm_ref[h], dsb[:, cols], "tn"))
            dv = jnp.concatenate(parts, axis=1)
            dgv_ref[...] += jnp.sum(dv * vh, axis=0, keepdims=True)
            dvh = dv * gvv
            dvg = r * (dvh - jnp.mean(dvh, axis=-1, keepdims=True) - vh * jnp.mean(dvh * vh, axis=-1, keepdims=True))
            dzv_ref[rows, :] = (dvg * _gelu_grad(zv)).astype(BF16)
            dzu_ref[rows, :] = (dy * s * _gelu_grad(zu)).astype(BF16)

    full = lambda a: pl.BlockSpec(a.shape, lambda i, nd=a.ndim: (0,) * nd)
    return pl.pallas_call(
        body,
        name="gmlp_bwd",
        grid=(t_dim // tr,),
        in_specs=[pl.BlockSpec((tr, dg), lambda i: (i, 1)), pl.BlockSpec((tr, dg), lambda i: (i, 2)),
                  pl.BlockSpec((tr, dg), lambda i: (i, 1)), full(gv), full(wm), full(bias), full(ggo)],
        out_specs=[pl.BlockSpec((tr, dg), lambda i: (i, 0)), pl.BlockSpec((tr, dg), lambda i: (i, 0)),
                   pl.BlockSpec((1, dg), lambda i: (0, 0)), pl.BlockSpec((1, dg), lambda i: (0, 0)),
                   pl.BlockSpec(wm.shape, lambda i: (0, 0, 0)), pl.BlockSpec((CHUNK, dg), lambda i: (0, 0))],
        out_shape=[jax.ShapeDtypeStruct((t_dim, dg), BF16), jax.ShapeDtypeStruct((t_dim, dg), BF16),
                   jax.ShapeDtypeStruct((1, dg), F32), jax.ShapeDtypeStruct((1, dg), F32),
                   jax.ShapeDtypeStruct(wm.shape, F32), jax.ShapeDtypeStruct((CHUNK, dg), F32)],
        compiler_params=_params(("arbitrary",)),
    )(z, z, dycat, gv, wm, bias, ggo)


def _ple_head(npl, w_gate, h3, pp, tgt, g_final):
    t_dim, d = h3.shape
    tr = _tile(t_dim, 256, 16)

    def body(n_ref, w_ref, h_ref, pp_ref, t_ref, g_ref, dgq_ref, dpp_ref, dh_ref, dg_ref, loss_ref):
        @pl.when(pl.program_id(0) == 0)
        def _():
            dg_ref[...] = jnp.zeros_like(dg_ref)
            loss_ref[...] = jnp.zeros_like(loss_ref)

        gate = _sigmoid(_dot(n_ref[...], w_ref[...], "nn"))
        ppv = pp_ref[...]
        h4 = h_ref[...] + gate * ppv
        xh, _ = _rms_stats(h4)
        err = xh * g_ref[...] - t_ref[...]
        dh4, dg = _rms_backward(h4, g_ref[...], err * (1.0 / d))
        dh_ref[...] = dh4
        dgq_ref[...] = (dh4 * ppv * gate * (1.0 - gate)).astype(BF16)
        dpp_ref[...] = (dh4 * gate).astype(BF16)
        dg_ref[...] += dg
        loss_ref[...] += jnp.full((1, LANES), 0.5 * jnp.sum(err * err) * (1.0 / d), F32)

    rows = pl.BlockSpec((tr, d), lambda i: (i, 0))
    whole = lambda a: pl.BlockSpec(a.shape, lambda i: (0, 0))
    return pl.pallas_call(
        body,
        name="ple_head",
        grid=(t_dim // tr,),
        in_specs=[rows, whole(w_gate), rows, rows, rows, whole(g_final)],
        out_specs=[rows, rows, rows, pl.BlockSpec((1, d), lambda i: (0, 0)), pl.BlockSpec((1, LANES), lambda i: (0, 0))],
        out_shape=[jax.ShapeDtypeStruct((t_dim, d), BF16), jax.ShapeDtypeStruct((t_dim, d), BF16),
                   jax.ShapeDtypeStruct((t_dim, d), F32), jax.ShapeDtypeStruct((1, d), F32),
                   jax.ShapeDtypeStruct((1, LANES), F32)],
        compiler_params=_params(("arbitrary",)),
    )(npl, w_gate, h3, pp, tgt, g_final)


def _position():
    x, y, c = lax.axis_index("x"), lax.axis_index("y"), lax.axis_index("c")
    chips = [(1 - x, y), (x, 1 - y), (1 - x, 1 - y)]
    return x, y, c, chips


def _region(ref, kind, shard_shape, q, half):
    rs, cs = shard_shape
    r0, nr = (0, rs) if half is None else (half * (rs // 2), rs // 2)
    if kind == "row":
        return ref.at[pl.ds(q * rs + r0, nr), :]
    return ref.at[pl.ds(r0, nr), pl.ds(q * cs, cs)]


def _full_shape(kind, shard_shape):
    rs, cs = shard_shape
    return (N_CHIPS * rs, cs) if kind == "row" else (rs, N_CHIPS * cs)


def _remote(src, dst, send_sems, recv_sems, k, to):
    return pltpu.make_async_remote_copy(src_ref=src, dst_ref=dst, send_sem=send_sems.at[k], recv_sem=recv_sems.at[k],
                                        device_id=to, device_id_type=MESH)


def _same(arrays):
    return [jax.ShapeDtypeStruct(a.shape, a.dtype) for a in arrays]


def _gather_ici_carry(gathered, kinds, shapes):
    nw = len(gathered)

    def copies(ops, full, send_sems, recv_sems):
        x, y, c, chips = _position()
        out = []
        for w in range(nw):
            mine = _region(full[w], kinds[w], shapes[w], 2 * x + y, c)
            for j, (cx, cy) in enumerate(chips):
                out.append(_remote(mine, mine, send_sems, recv_sems, 3 * w + j, (cx, cy, c)))
        return out

    return _Carry(gathered, _same(gathered), {i: i for i in range(nw)}, 3 * nw, copies)


def _gather_d2d_carry(gathered, kinds, shapes):
    nw = len(gathered)

    def copies(ops, full, send_sems, recv_sems):
        x, y, c, chips = _position()
        out = []
        for w in range(nw):
            for j, (cx, cy) in enumerate(chips):
                landed = _region(full[w], kinds[w], shapes[w], 2 * cx + cy, c)
                out.append(_remote(landed, landed, send_sems, recv_sems, 3 * w + j, (x, y, 1 - c)))
        return out

    return _Carry(gathered, _same(gathered), {i: i for i in range(nw)}, 3 * nw, copies)


def _pairs_carry(grads, kinds, shapes):
    nw = len(grads)

    def copies(g, got, send_sems, recv_sems):
        x, y, c, _ = _position()
        out = []
        for w in range(nw):
            for q in range(N_CHIPS):
                out.append(_remote(_region(g[w], kinds[w], shapes[w], q, 1 - c), got[w].at[q], send_sems, recv_sems,
                                   N_CHIPS * w + q, (x, y, 1 - c)))
        return out

    outs = [jax.ShapeDtypeStruct((N_CHIPS, s[0] // 2, s[1]), BF16) for s in shapes]
    return _Carry(grads, outs, {}, N_CHIPS * nw, copies)


def _pair_sum(name, grad, got, kind, shard_shape, c_arr):
    rs, cs = shard_shape
    hr = rs // 2
    tr = _tile(hr, 512, 16)
    nr = hr // tr

    def body(c_ref, g_ref, s_ref, o_ref):
        o_ref[...] = (g_ref[...].astype(F32) + s_ref[...].astype(F32)).astype(BF16)

    if kind == "row":
        g_spec = pl.BlockSpec((tr, cs), lambda q, i, c_ref: (q * (rs // tr) + c_ref[0] * nr + i, 0))
    else:
        g_spec = pl.BlockSpec((tr, cs), lambda q, i, c_ref: (c_ref[0] * nr + i, q))
    blk = pl.BlockSpec((None, tr, cs), lambda q, i, c_ref: (q, i, 0))
    return pl.pallas_call(
        body,
        name=name,
        grid_spec=pltpu.PrefetchScalarGridSpec(num_scalar_prefetch=1, grid=(N_CHIPS, nr), in_specs=[g_spec, blk],
                                               out_specs=blk),
        out_shape=jax.ShapeDtypeStruct((N_CHIPS, hr, cs), BF16),
        compiler_params=_params(("parallel", "parallel")),
    )(c_arr, grad, got)


def _scatter_carry(sums, shapes):
    nw = len(sums)

    def copies(ps, got, send_sems, recv_sems):
        x, y, c, chips = _position()
        out = []
        for w in range(nw):
            for j, (cx, cy) in enumerate(chips):
                out.append(_remote(ps[w].at[2 * cx + cy], got[w].at[j], send_sems, recv_sems, 3 * w + j, (cx, cy, c)))
        return out

    outs = [jax.ShapeDtypeStruct((3, s[0] // 2, s[1]), BF16) for s in shapes]
    return _Carry(sums, outs, {}, 3 * nw, copies)


def _owner_sum(name, sums, got, shard_shape, qc_arr):
    rs, cs = shard_shape
    hr = rs // 2
    tr = _tile(hr, 512, 16)
    nr = hr // tr

    def body(qc_ref, mine_ref, got_ref, o_ref):
        acc = mine_ref[...].astype(F32)
        for j in range(3):
            acc = acc + got_ref[j].astype(F32)
        o_ref[...] = acc

    return pl.pallas_call(
        body,
        name=name,
        grid_spec=pltpu.PrefetchScalarGridSpec(
            num_scalar_prefetch=1, grid=(nr,),
            in_specs=[pl.BlockSpec((None, tr, cs), lambda i, qc_ref: (qc_ref[0], i, 0)),
                      pl.BlockSpec((3, tr, cs), lambda i, qc_ref: (0, i, 0))],
            out_specs=pl.BlockSpec((tr, cs), lambda i, qc_ref: (qc_ref[1] * nr + i, 0))),
        out_shape=jax.ShapeDtypeStruct((rs, cs), F32),
        compiler_params=_params(("parallel",)),
    )(qc_arr, sums, got)


def _share_carry(grads, shapes):
    nw = len(grads)

    def copies(ops, out, send_sems, recv_sems):
        x, y, c, _ = _position()
        res = []
        for w in range(nw):
            hr = shapes[w][0] // 2
            mine = out[w].at[pl.ds(c * hr, hr), :]
            res.append(_remote(mine, mine, send_sems, recv_sems, w, (x, y, 1 - c)))
        return res

    return _Carry(grads, _same(grads), {i: i for i in range(nw)}, nw, copies)


def _place_block(packed, me):
    return lax.dynamic_update_slice(jnp.zeros((N_DEV,) + packed.shape, F32), packed[None], (me, 0, 0))


def _exchange_carry(blocks):
    def copies(ops, res, send_sems, recv_sems):
        x, y, c, _ = _position()
        mine = res[0].at[4 * x + 2 * y + c]
        out = []
        for k in range(1, N_DEV):
            to = ((1 - x) if k & 4 else x, (1 - y) if k & 2 else y, (1 - c) if k & 1 else c)
            out.append(_remote(mine, mine, send_sems, recv_sems, k - 1, to))
        return out

    return _Carry([blocks], _same([blocks]), {0: 0}, N_DEV - 1, copies)


def _sum_blocks(name, blocks):
    n, rows, lanes = blocks.shape
    tr = _tile(rows, 4096, 8)

    def body(b_ref, o_ref):
        acc = b_ref[0]
        for k in range(1, n):
            acc = acc + b_ref[k]
        o_ref[...] = acc

    return pl.pallas_call(
        body,
        name=name,
        grid=(rows // tr,),
        in_specs=[pl.BlockSpec((n, tr, lanes), lambda i: (0, i, 0))],
        out_specs=pl.BlockSpec((tr, lanes), lambda i: (i, 0)),
        out_shape=jax.ShapeDtypeStruct((rows, lanes), F32),
        compiler_params=_params(("parallel",)),
    )(blocks)


def _adamw(name, w, g, m, v):
    def fn(rows, pars):
        wv, gv, mv, vv = rows
        m_new = ADAM_B1 * mv + (1.0 - ADAM_B1) * gv
        v_new = ADAM_B2 * vv + (1.0 - ADAM_B2) * (gv * gv)
        m_hat = m_new / (1.0 - ADAM_B1 ** ADAM_STEP)
        v_hat = v_new / (1.0 - ADAM_B2 ** ADAM_STEP)
        delta = -ADAM_LR * (m_hat / (jnp.sqrt(v_hat) + ADAM_EPS) + ADAM_WD * wv)
        return [delta, m_new, v_new, gv], []

    c = w.shape[1]
    return _rowwise(name, fn, [w, g, m, v], [], [(c, F32)] * 4, tr=256)


def _pack(arrays):
    rows = []
    for a in arrays:
        flat = a.reshape(-1).astype(F32)
        pad = (-flat.shape[0]) % LANES
        rows.append(jnp.pad(flat, (0, pad)).reshape(-1, LANES))
    stacked = jnp.concatenate(rows, axis=0)
    pad_rows = (-stacked.shape[0]) % 8
    return jnp.pad(stacked, ((0, pad_rows), (0, 0)))


def _unpack(packed, shapes):
    out, r = [], 0
    for s in shapes:
        n = math.prod(s)
        nr = -(-n // LANES)
        out.append(packed[r:r + nr].reshape(-1)[:n].reshape(s))
        r += nr
    return out


BIG = ["w1_gate", "w1_up", "w1_down", "w_in", "ssm_w_glu", "w_out", "w2_gate", "w2_up", "w2_down", "w_ple_gate",
       "w_ple_proj"]
KIND = {"w1_gate": "col", "w1_up": "col", "w1_down": "row", "w_in": "col", "ssm_w_glu": "row", "w_out": "row",
        "w2_gate": "col", "w2_up": "col", "w2_down": "row", "w_ple_gate": "row", "w_ple_proj": "col"}
SMALL = ["norm_ffn1", "norm_mix", "ssm_log_dt", "ssm_a_re", "ssm_a_im", "ssm_b_re", "ssm_b_im", "ssm_c_re", "ssm_c_im",
         "ssm_d", "gmlp_norm_v", "gmlp_w_s", "gmlp_b_s", "norm_ssm_out", "norm_gmlp_out", "norm_ffn2", "norm_ple",
         "norm_final"]
WEIGHTS = ["norm_ffn1", "w1_gate", "w1_up", "w1_down", "norm_mix", "w_in", "ssm_log_dt", "ssm_a_re", "ssm_a_im",
           "ssm_b_re", "ssm_b_im", "ssm_c_re", "ssm_c_im", "ssm_d", "ssm_w_glu", "gmlp_norm_v", "gmlp_w_s", "gmlp_b_s",
           "norm_ssm_out", "norm_gmlp_out", "w_out", "norm_ffn2", "w2_gate", "w2_up", "w2_down", "norm_ple",
           "w_ple_gate", "w_ple_proj", "norm_final"]


class _Trip:
    def __init__(self, names, arrays, carry):
        self.names, self.arrays, self.carry = names, arrays, carry


class _Reducer:
    def __init__(self, shard_shape, c_arr, qc_arr):
        self.shard_shape, self.c_arr, self.qc_arr = shard_shape, c_arr, qc_arr
        self.halves = {}

    def swap(self, names, grads):
        kinds = [KIND[n] for n in names]
        shapes = [self.shard_shape[n] for n in names]
        return _Trip(names, grads, _pairs_carry(grads, kinds, shapes))

    def send(self, trip, swapped):
        shapes = [self.shard_shape[n] for n in trip.names]
        sums = [_pair_sum("pair_sum_" + n, g, s, KIND[n], sh, self.c_arr)
                for n, g, s, sh in zip(trip.names, trip.arrays, swapped, shapes)]
        return _Trip(trip.names, sums, _scatter_carry(sums, shapes))

    def end(self, trip, got):
        for n, ps, g in zip(trip.names, trip.arrays, got):
            self.halves[n] = _owner_sum("owner_sum_" + n, ps, g, self.shard_shape[n], self.qc_arr)


def _ride(*carries):
    present = [c for c in carries if c is not None]
    joined = functools.reduce(_join, present) if present else None

    def split(results):
        out, at = [], 0
        for c in carries:
            n = len(c.out_shapes) if c is not None else 0
            out.append(list(results[at:at + n]))
            at += n
        return out

    return joined, split


def _step(x, p, tgt, w, m, v):
    d_model = x.shape[1]
    d_ssm = w["ssm_d"].shape[1]
    n_groups = d_ssm // SSM_GROUP
    row = lambda a: a.reshape(1, -1)

    xi, yi, ci = lax.axis_index("x"), lax.axis_index("y"), lax.axis_index("c")
    c_arr = jnp.reshape(ci, (1,)).astype(jnp.int32)
    q_arr = jnp.reshape(2 * xi + yi, (1,)).astype(jnp.int32)
    qc_arr = jnp.stack([2 * xi + yi, ci]).astype(jnp.int32)
    shard_shape = {n: w[n].shape for n in BIG}
    full = {n: _cast_into_gathered("cast_" + n, w[n], KIND[n], q_arr) for n in BIG}

    def gather(stage, names):
        return stage([full[n] for n in names], [KIND[n] for n in names], [shard_shape[n] for n in names])

    def gathered(names, arrays):
        full.update(zip(names, arrays))

    zeroth, first, second, third = ["w1_gate"], ["w1_up"], ["w1_down", "w_in"], ["w2_gate"]
    fourth, fifth, sixth = ["ssm_w_glu", "w_out"], ["w2_up"], ["w2_down", "w_ple_gate", "w_ple_proj"]

    def two_stages(passed_on, landing):
        return _join(gather(_gather_d2d_carry, passed_on), gather(_gather_ici_carry, landing)), passed_on + landing

    gathered(zeroth, _comm_call("gather_zeroth_ici", gather(_gather_ici_carry, zeroth)))
    gathered(zeroth, _comm_call("gather_zeroth_d2d", gather(_gather_d2d_carry, zeroth)))
    n1 = _rms_fwd("ffn1_norm", x, w["norm_ffn1"])
    gate1, landed = _mm_nn("ffn1_gate", n1, full["w1_gate"], BF16, tm=1024, tn=512, tk=2048,
                           carry=gather(_gather_ici_carry, first))
    gathered(first, landed)
    gathered(first, _comm_call("gather_first_d2d", gather(_gather_d2d_carry, first)))
    (a1, da_dgate1, da_dup1), landed = _ffn_up("ffn1", n1, full["w1_up"], gate1, gather(_gather_ici_carry, second))
    gathered(second, landed)
    gathered(second, _comm_call("gather_second_d2d", gather(_gather_d2d_carry, second)))
    h1, landed = _ffn_down("ffn1", a1, full["w1_down"], x, gather(_gather_ici_carry, third))
    gathered(third, landed)
    ffn1 = (n1, a1, da_dgate1, da_dup1)
    nm = _rms_fwd("mix_norm", h1, w["norm_mix"])
    carry, names = two_stages(third, fourth)
    z, landed = _mm_nn("in_proj", nm, full["w_in"], F32, tm=1024, tn=512, tk=2048, carry=carry)
    gathered(names, landed)

    disc, disc_vjp = jax.vjp(_ssm_discretize, w["ssm_log_dt"][0], w["ssm_a_re"], w["ssm_a_im"], w["ssm_b_re"],
                             w["ssm_b_im"])
    abar_r, abar_i, bbar_r, bbar_i = disc
    nb = n_groups // GROUPS_PER_BLOCK
    wb = jnp.concatenate([_blockdiag_in(bbar_r), _blockdiag_in(bbar_i)], axis=-1).astype(BF16)
    wc = jnp.concatenate([_blockdiag_out(w["ssm_c_re"]), -_blockdiag_out(w["ssm_c_im"])], axis=1).astype(BF16)
    abar = jnp.concatenate([abar_r.reshape(nb, -1), abar_i.reshape(nb, -1)], axis=-1)
    abar_conj = jnp.concatenate([abar_r.reshape(nb, -1), -abar_i.reshape(nb, -1)], axis=-1)
    carry, names = two_stages(fourth, fifth)
    (states, y_pre, yg), landed = _ssm_scan_fwd(z, wb, wc, abar, w["ssm_d"], carry)
    gathered(names, landed)
    q = _mm_nn("glu_proj", yg, full["ssm_w_glu"], F32, tm=1024, tn=1024, tk=1024)

    def glu_norm(rows, pars):
        yv = _gelu(rows[0]) * _sigmoid(rows[1])
        yh, _ = _rms_stats(yv)
        return [yh * pars[0]], []

    yn_ssm = _rowwise("ssm_glu_norm", glu_norm, [y_pre, q], [w["norm_ssm_out"]], [(d_ssm, BF16)])[0]

    tril = jnp.tril(jnp.ones((CHUNK, CHUNK), dtype=bool))
    wm = jnp.where(tril[None], w["gmlp_w_s"], 0.0).astype(BF16)
    bias = jnp.repeat(w["gmlp_b_s"].T, GMLP_HEAD, axis=1)
    yn_gmlp = _gmlp_fwd(z, w["gmlp_norm_v"], wm, bias, w["norm_gmlp_out"])
    ycat = jnp.concatenate([yn_ssm, yn_gmlp], axis=1)
    h2, landed = _mm_nn("out_proj", ycat, full["w_out"], F32, res=h1, alpha=1.0, tm=512, tn=1024, tk=2048,
                        carry=gather(_gather_d2d_carry, fifth))
    gathered(fifth, landed)

    n2 = _rms_fwd("ffn2_norm", h2, w["norm_ffn2"])
    (a2, da_dgate2, da_dup2), landed = _ffn_gateup("ffn2", n2, full["w2_gate"], full["w2_up"],
                                           gather(_gather_ici_carry, sixth))
    gathered(sixth, landed)
    gathered(sixth, _comm_call("gather_sixth_d2d", gather(_gather_d2d_carry, sixth)))
    h3 = _ffn_down("ffn2", a2, full["w2_down"], h2)
    ffn2 = (n2, a2, da_dgate2, da_dup2)
    npl = _rms_fwd("ple_norm", h3, w["norm_ple"])
    pp = _mm_nn("ple_proj", p, full["w_ple_proj"], F32, tm=1024, tn=1024, tk=2048)
    dgq, dpp, dh4, g_norm_final, loss_part = _ple_head(npl, full["w_ple_gate"], h3, pp, tgt, row(w["norm_final"]))
    reducer = _Reducer(shard_shape, c_arr, qc_arr)
    (g_w_ple_proj,) = _mm_tn("ple_dwproj", p, [dpp], BF16, tm=256, tn=1024, tk=4096)
    (g_w_ple_gate,) = _mm_tn("ple_dwgate", npl, [dgq], BF16, tm=512, tn=1024, tk=4096)
    ple = reducer.swap(["w_ple_gate", "w_ple_proj"], [g_w_ple_gate, g_w_ple_proj])
    dnpl, swapped = _mm_nt_sum("ple_dnorm_in", [dgq], [full["w_ple_gate"]], F32, tm=512, tn=1024, tk=2048,
                               carry=ple.carry)
    ple = reducer.send(ple, swapped)
    dh3, dh3_b, g_norm_ple = _rms_bwd("ple_dnorm", h3, w["norm_ple"], [dnpl], dh4, 0.5)

    dh2, dh2_b, g_norm_ffn2, got, _, up2_w = _ffn_bwd(
        "ffn2", ["w2_gate", "w2_up", "w2_down"], h2, w["norm_ffn2"], full["w2_gate"], full["w2_up"], full["w2_down"],
        ffn2, dh3, dh3_b, 1.0, reducer, riding=ple.carry, last_hop_later=True)
    reducer.end(ple, got)

    dycat = _mm_nt_sum("out_dproj", [dh2_b], [full["w_out"]], F32, tm=512, tn=1024, tk=2048)
    (g_w_out,) = _mm_tn("out_dw", ycat, [dh2_b], BF16, tm=512, tn=1024, tk=4096)

    dzu, dzv, g_norm_gmlp_out, g_gmlp_norm_v, g_wm, g_s = _gmlp_bwd(z, dycat, w["gmlp_norm_v"], wm, bias,
                                                                  w["norm_gmlp_out"])
    g_gmlp_w_s = jnp.where(tril[None], g_wm, 0.0)
    g_gmlp_b_s = g_s.reshape(CHUNK, -1, GMLP_HEAD).sum(axis=-1).T

    def glu_bwd(rows, pars):
        dyn, ypre, qv = rows
        ygv = _gelu(ypre)
        sg = _sigmoid(qv)
        dy, dg = _rms_backward(ygv * sg, pars[0], dyn)
        return [dy * ygv * sg * (1.0 - sg), dy * sg], [dg]

    dq, dyg_part, g_norm_ssm_out = _rowwise("ssm_dglu", glu_bwd, [(dycat, d_ssm, 0), y_pre, q], [w["norm_ssm_out"]],
                                            [(d_ssm, BF16), (d_ssm, F32)], [(1, d_ssm)])
    dyg_proj = _mm_nt_sum("glu_dproj", [dq], [full["ssm_w_glu"]], F32, tm=1024, tn=1024, tk=1024)
    (g_ssm_w_glu,) = _mm_tn("glu_dw", yg, [dq], BF16, tm=512, tn=1024, tk=4096)

    def gelu_bwd(rows, pars):
        return [(rows[0] + rows[1]) * _gelu_grad(rows[2])], []

    dy_pre = _rowwise("ssm_dgelu", gelu_bwd, [dyg_part, dyg_proj, y_pre], [], [(d_ssm, F32)])[0]
    mixers = reducer.swap(["w_out", "ssm_w_glu"], [g_w_out, g_ssm_w_glu])
    me = 4 * xi + 2 * yi + ci
    small = {"gmlp_norm_v": g_gmlp_norm_v, "gmlp_w_s": g_gmlp_w_s, "gmlp_b_s": g_gmlp_b_s,
             "norm_gmlp_out": g_norm_gmlp_out, "norm_ssm_out": g_norm_ssm_out, "norm_ffn2": g_norm_ffn2,
             "norm_ple": g_norm_ple, "norm_final": g_norm_final}
    before_scan = [n for n in SMALL if n in small]
    scan_blocks = _place_block(_pack([small[n] for n in before_scan] + [loss_part[:, :1]]), me)
    carry, split = _ride(up2_w.carry, mixers.carry, _exchange_carry(scan_blocks))
    (dz_ssm, g_wb, g_wc, g_abar, g_ssm_d), results = _ssm_scan_bwd(z, dy_pre, states, wb, wc, abar_conj, w["ssm_d"],
                                                                  carry)
    got, swapped, (scan_blocks,) = split(results)
    reducer.end(up2_w, got)
    mixers = reducer.send(mixers, swapped)
    g_abar = jnp.transpose(g_abar, (1, 0, 2)).reshape(nb, -1)
    sw = g_abar.shape[-1] // 2
    g_bbar_r = _blockdiag_in_grad(g_wb[..., :sw], SSM_STATE, SSM_GROUP)
    g_bbar_i = _blockdiag_in_grad(g_wb[..., sw:], SSM_STATE, SSM_GROUP)
    g_ssm_c_re = _blockdiag_out_grad(g_wc[:, :sw, :], SSM_GROUP, SSM_STATE)
    g_ssm_c_im = -_blockdiag_out_grad(g_wc[:, sw:, :], SSM_GROUP, SSM_STATE)
    g_abar_r = g_abar[..., :sw].reshape(n_groups, SSM_STATE)
    g_abar_i = g_abar[..., sw:].reshape(n_groups, SSM_STATE)
    g_ssm_log_dt, g_ssm_a_re, g_ssm_a_im, g_ssm_b_re, g_ssm_b_im = disc_vjp((g_abar_r, g_abar_i, g_bbar_r, g_bbar_i))

    dz = jnp.concatenate([dz_ssm, dzu, dzv], axis=1)
    (g_w_in,), got = _mm_tn("in_dw", nm, [dz], BF16, tm=512, tn=1536, tk=4096, carry=mixers.carry)
    reducer.end(mixers, got)
    in_w = reducer.swap(["w_in"], [g_w_in])
    dnm, swapped = _mm_nt_sum("in_dproj", [dz], [full["w_in"]], F32, tm=1024, tn=1024, tk=3072, carry=in_w.carry)
    in_w = reducer.send(in_w, swapped)
    dh1, dh1_b, g_norm_mix = _rms_bwd("mix_dnorm", h1, w["norm_mix"], [dnm], dh2, 0.5)
    small = {"norm_mix": g_norm_mix, "ssm_log_dt": g_ssm_log_dt, "ssm_a_re": g_ssm_a_re,
             "ssm_a_im": g_ssm_a_im, "ssm_b_re": g_ssm_b_re, "ssm_b_im": g_ssm_b_im, "ssm_c_re": g_ssm_c_re,
             "ssm_c_im": g_ssm_c_im, "ssm_d": g_ssm_d}
    after_scan = [n for n in SMALL if n in small]
    ffn_blocks = _place_block(_pack([small[n] for n in after_scan]), me)
    dx, _, g_norm_ffn1, got, (ffn_blocks,), _ = _ffn_bwd(
        "ffn1", ["w1_gate", "w1_up", "w1_down"], x, w["norm_ffn1"], full["w1_gate"], full["w1_up"], full["w1_down"],
        ffn1, dh1, dh1_b, 1.0, reducer, riding=in_w.carry, riding_dwd=_exchange_carry(ffn_blocks))
    reducer.end(in_w, got)
    scan_grads = _unpack(_sum_blocks("sum_before_scan", scan_blocks), [w[n].shape for n in before_scan] + [(1,)])
    loss = scan_grads[-1].reshape(())
    early = before_scan + after_scan
    early_grads = scan_grads[:-1] + _unpack(_sum_blocks("sum_after_scan", ffn_blocks), [w[n].shape for n in after_scan])

    halves = [reducer.halves[n] for n in BIG]
    late_blocks = _place_block(_pack([g_norm_ffn1]), me)
    last = _join(_share_carry(halves, [shard_shape[n] for n in BIG]), _exchange_carry(late_blocks))
    *shared, late_blocks = _comm_call("share_halves", last)
    grad = dict(zip(BIG, shared))
    grad.update(zip(early, early_grads))
    grad["norm_ffn1"] = _unpack(_sum_blocks("sum_first_norm", late_blocks), [w["norm_ffn1"].shape])[0]

    small_shapes = [w[n].shape for n in SMALL]
    delta, new_m, new_v = {}, {}, {}
    for n in BIG:
        delta[n], new_m[n], new_v[n], grad[n] = _adamw("adamw_" + n, w[n], grad[n], m[n], v[n])
    d_p, m_p, v_p, _ = _adamw("adamw_small", _pack([w[n] for n in SMALL]), _pack([grad[n] for n in SMALL]),
                              _pack([m[n] for n in SMALL]), _pack([v[n] for n in SMALL]))
    for name_list, packed in ((delta, d_p), (new_m, m_p), (new_v, v_p)):
        for n, a in zip(SMALL, _unpack(packed, small_shapes)):
            name_list[n] = a
    return loss, dx, grad, delta, new_m, new_v


def kernel(x, p, norm_ffn1, w1_gate, w1_up, w1_down, norm_mix, w_in, ssm_log_dt, ssm_a_re, ssm_a_im, ssm_b_re, ssm_b_im, ssm_c_re, ssm_c_im, ssm_d, ssm_w_glu, gmlp_norm_v, gmlp_w_s, gmlp_b_s, norm_ssm_out, norm_gmlp_out, w_out, norm_ffn2, w2_gate, w2_up, w2_down, norm_ple, w_ple_gate, w_ple_proj, norm_final, loss_target, m_norm_ffn1, m_w1_gate, m_w1_up, m_w1_down, m_norm_mix, m_w_in, m_ssm_log_dt, m_ssm_a_re, m_ssm_a_im, m_ssm_b_re, m_ssm_b_im, m_ssm_c_re, m_ssm_c_im, m_ssm_d, m_ssm_w_glu, m_gmlp_norm_v, m_gmlp_w_s, m_gmlp_b_s, m_norm_ssm_out, m_norm_gmlp_out, m_w_out, m_norm_ffn2, m_w2_gate, m_w2_up, m_w2_down, m_norm_ple, m_w_ple_gate, m_w_ple_proj, m_norm_final, v_norm_ffn1, v_w1_gate, v_w1_up, v_w1_down, v_norm_mix, v_w_in, v_ssm_log_dt, v_ssm_a_re, v_ssm_a_im, v_ssm_b_re, v_ssm_b_im, v_ssm_c_re, v_ssm_c_im, v_ssm_d, v_ssm_w_glu, v_gmlp_norm_v, v_gmlp_w_s, v_gmlp_b_s, v_norm_ssm_out, v_norm_gmlp_out, v_w_out, v_norm_ffn2, v_w2_gate, v_w2_up, v_w2_down, v_norm_ple, v_w_ple_gate, v_w_ple_proj, v_norm_final):
    given = dict(locals())
    shapes = {n: given[n].shape for n in WEIGHTS}

    def block(name):
        a = given[name]
        if a.ndim == 1:
            return a.reshape(1, -1)
        return a[0] if a.ndim >= 3 else a

    w = {n: block(n) for n in WEIGHTS}
    m = {n: block("m_" + n) for n in WEIGHTS}
    v = {n: block("v_" + n) for n in WEIGHTS}
    loss, dx, grad, delta, new_m, new_v = _step(x[0], p[0, 0], loss_target[0], w, m, v)
    outs = [loss, dx[None]]
    for tree in (grad, delta, new_m, new_v):
        outs += [tree[n].reshape(shapes[n]) for n in WEIGHTS]
    return tuple(outs)
```

```python
import functools
import math

import jax
import jax.numpy as jnp
from jax import lax
from jax.experimental import pallas as pl
from jax.experimental.pallas import tpu as pltpu

F32 = jnp.float32
BF16 = jnp.bfloat16
EPS = 1e-6
SSM_GROUP = 16
SSM_STATE = 64
GROUPS_PER_BLOCK = 8
GMLP_HEAD = 128
CHUNK = 128
ADAM_LR = 0.001
ADAM_B1 = 0.9
ADAM_B2 = 0.999
ADAM_EPS = 1e-08
ADAM_WD = 0.01
ADAM_STEP = 10
N_CHIPS = 4
N_DEV = 8
LANES = 128
VMEM_LIMIT_BYTES = 56 * 1024 * 1024
MESH = pl.DeviceIdType.MESH
GELU_C = math.sqrt(2.0 / math.pi)
GELU_A = 0.044715

_DOT_DIMS = {
    "nn": (((1,), (0,)), ((), ())),
    "nt": (((1,), (1,)), ((), ())),
    "tn": (((0,), (0,)), ((), ())),
}


def _tile(dim, pref, align):
    if dim <= pref:
        return dim
    t = (pref // align) * align
    while t >= align:
        if dim % t == 0:
            return t
        t -= align
    return dim


def _params(semantics):
    return pltpu.CompilerParams(dimension_semantics=semantics, vmem_limit_bytes=VMEM_LIMIT_BYTES)


def _gelu(x):
    return 0.5 * x * (1.0 + jnp.tanh(GELU_C * (x + GELU_A * x * x * x)))


def _gelu_grad(x):
    t = jnp.tanh(GELU_C * (x + GELU_A * x * x * x))
    return 0.5 * (1.0 + t) + 0.5 * x * (1.0 - t * t) * GELU_C * (1.0 + 3.0 * GELU_A * x * x)


def _sigmoid(x):
    return 1.0 / (1.0 + jnp.exp(-x))


def _dot(a, b, mode):
    return lax.dot_general(a.astype(BF16), b.astype(BF16), _DOT_DIMS[mode], preferred_element_type=F32)


class _Carry:
    def __init__(self, arrays, out_shapes, aliases, n_copies, copies):
        self.arrays = list(arrays)
        self.out_shapes = list(out_shapes)
        self.aliases = dict(aliases)
        self.n_copies = n_copies
        self.copies = copies

    def scratch(self):
        return [pltpu.SemaphoreType.DMA((self.n_copies,)), pltpu.SemaphoreType.DMA((self.n_copies,))]

    def split(self, refs):
        n_in, n_out = len(self.arrays), len(self.out_shapes)
        return refs[:n_in], refs[n_in:n_in + n_out], refs[n_in + n_out], refs[n_in + n_out + 1]

    def start(self, refs):
        for cp in self.copies(*self.split(refs)):
            cp.start()

    def wait(self, refs):
        for cp in self.copies(*self.split(refs)):
            cp.wait()


class _SemRange:
    def __init__(self, sems, offset):
        self.sems, self.offset = sems, offset

    @property
    def at(self):
        return self

    def __getitem__(self, k):
        return self.sems.at[self.offset + k]


def _join(first, second):
    n_in, n_out = len(first.arrays), len(first.out_shapes)
    aliases = dict(first.aliases)
    aliases.update({n_in + i: n_out + o for i, o in second.aliases.items()})

    def copies(ops, res, send_sems, recv_sems):
        return (first.copies(ops[:n_in], res[:n_out], send_sems, recv_sems)
                + second.copies(ops[n_in:], res[n_out:], _SemRange(send_sems, first.n_copies),
                                _SemRange(recv_sems, first.n_copies)))

    return _Carry(first.arrays + second.arrays, first.out_shapes + second.out_shapes, aliases,
                  first.n_copies + second.n_copies, copies)


_ANY = pl.BlockSpec(memory_space=pl.ANY)


def _comm_call(name, carry):
    def body(*refs):
        carry.start(refs)
        carry.wait(refs)

    n_in = len(carry.arrays)
    return pl.pallas_call(
        body,
        name=name,
        in_specs=[_ANY] * n_in,
        out_specs=[_ANY] * len(carry.out_shapes),
        out_shape=carry.out_shapes,
        input_output_aliases=carry.aliases,
        scratch_shapes=carry.scratch(),
    )(*carry.arrays)


def _carried_call(body, carry, *, name, grid, in_specs, out_specs, out_shape, scratch_shapes, semantics, args):
    if carry is None:
        res = pl.pallas_call(body, name=name, grid=grid, in_specs=in_specs, out_specs=out_specs, out_shape=out_shape,
                             scratch_shapes=scratch_shapes, compiler_params=_params(semantics))(*args)
        return res, []
    n_in, n_out, n_scr = len(in_specs), len(out_specs), len(scratch_shapes)
    nci, nco = len(carry.arrays), len(carry.out_shapes)

    def wrapped(*refs):
        ins = refs[:n_in]
        outs = refs[n_in + nci:n_in + nci + n_out]
        scr = refs[n_in + nci + n_out + nco:n_in + nci + n_out + nco + n_scr]
        c_refs = (refs[n_in:n_in + nci] + refs[n_in + nci + n_out:n_in + nci + n_out + nco]
                  + refs[n_in + nci + n_out + nco + n_scr:])
        first = functools.reduce(jnp.logical_and, [pl.program_id(d) == 0 for d in range(len(grid))])
        last = functools.reduce(jnp.logical_and, [pl.program_id(d) == grid[d] - 1 for d in range(len(grid))])

        @pl.when(first)
        def _():
            carry.start(c_refs)

        body(*ins, *outs, *scr)

        @pl.when(last)
        def _():
            carry.wait(c_refs)

    res = pl.pallas_call(
        wrapped,
        name=name,
        grid=grid,
        in_specs=list(in_specs) + [_ANY] * nci,
        out_specs=list(out_specs) + [_ANY] * nco,
        out_shape=list(out_shape) + carry.out_shapes,
        input_output_aliases={n_in + i: n_out + o for i, o in carry.aliases.items()},
        scratch_shapes=list(scratch_shapes) + carry.scratch(),
        compiler_params=_params(("arbitrary",) * len(grid)),
    )(*args, *carry.arrays)
    return res[:n_out], res[n_out:]


def _matmul(name, mode, a_list, b_list, products, out_dtypes, epilogue, extras=(), tm=512, tn=512, tk=2048,
            carry=None, n_part=(0, 1)):
    a0, b0 = a_list[0], b_list[0]
    if mode == "tn":
        k_dim, m_dim = a0.shape
    else:
        m_dim, k_dim = a0.shape
    n_dim = (b0.shape[0] if mode == "nt" else b0.shape[1]) // n_part[1]
    tm = _tile(m_dim, tm, LANES)
    tn = _tile(n_dim, tn, LANES)
    tk = _tile(k_dim, tk, LANES)
    nk = k_dim // tk
    j0 = n_part[0] * (n_dim // tn)
    chunk = 2 * LANES if (nk == 1 and epilogue is not _identity and tn % (2 * LANES) == 0) else tn
    n_acc = 1 + max(p[2] for p in products)
    na, nb, ne, no = len(a_list), len(b_list), len(extras), len(out_dtypes)

    if mode == "tn":
        a_spec = pl.BlockSpec((tk, tm), lambda i, j, k: (k, i))
    else:
        a_spec = pl.BlockSpec((tm, tk), lambda i, j, k: (i, k))
    if mode == "nt":
        b_spec = pl.BlockSpec((tn, tk), lambda i, j, k: (j0 + j, k))
    else:
        b_spec = pl.BlockSpec((tk, tn), lambda i, j, k: (k, j0 + j))
    t_spec = pl.BlockSpec((tm, tn), lambda i, j, k: (i, j))

    def body(*refs):
        a_refs = refs[:na]
        b_refs = refs[na:na + nb]
        e_refs = refs[na + nb:na + nb + ne]
        o_refs = refs[na + nb + ne:na + nb + ne + no]
        acc_refs = refs[na + nb + ne + no:]

        def partial_sums(cols):
            sums = [None] * n_acc
            for ai, bi, ci in products:
                b = b_refs[bi][cols, :] if mode == "nt" else b_refs[bi][:, cols]
                d = _dot(a_refs[ai][...], b, mode)
                sums[ci] = d if sums[ci] is None else sums[ci] + d
            return sums

        def finish(accs, cols):
            outs = epilogue(accs, [e[:, cols] for e in e_refs])
            for o_ref, o in zip(o_refs, outs):
                o_ref[:, cols] = o.astype(o_ref.dtype)

        if nk == 1:
            for c0 in range(0, tn, chunk):
                finish(partial_sums(slice(c0, c0 + chunk)), slice(c0, c0 + chunk))
        else:
            sums = partial_sums(slice(None))
            finish = functools.partial(finish, cols=slice(None))
            k = pl.program_id(2)

            @pl.when(k == 0)
            def _():
                for acc, s in zip(acc_refs, sums):
                    acc[...] = s

            @pl.when(k > 0)
            def _():
                for acc, s in zip(acc_refs, sums):
                    acc[...] += s

            @pl.when(k == nk - 1)
            def _():
                finish([acc[...] for acc in acc_refs])

    scratch = [pltpu.VMEM((tm, tn), F32) for _ in range(n_acc)] if nk > 1 else []
    outs, carried = _carried_call(
        body, carry,
        name=name,
        grid=(m_dim // tm, n_dim // tn, nk),
        in_specs=[a_spec] * na + [b_spec] * nb + [t_spec] * ne,
        out_specs=[t_spec] * no,
        out_shape=[jax.ShapeDtypeStruct((m_dim, n_dim), dt) for dt in out_dtypes],
        scratch_shapes=scratch,
        semantics=("parallel", "parallel", "arbitrary"),
        args=[*a_list, *b_list, *extras],
    )
    return (outs, carried) if carry else outs


def _identity(accs, extras):
    return accs


def _single(result, carry):
    return (result[0][0], result[1]) if carry else result[0]


def _mm_nn(name, a, b, out_dtype, res=None, alpha=1.0, carry=None, **tiles):
    if res is None:
        return _single(_matmul(name, "nn", [a], [b], [(0, 0, 0)], [out_dtype], _identity, carry=carry, **tiles), carry)

    def epilogue(accs, extras):
        return [extras[0] + alpha * accs[0]]

    return _single(_matmul(name, "nn", [a], [b], [(0, 0, 0)], [out_dtype], epilogue, extras=(res,), carry=carry,
                           **tiles), carry)


def _mm_nt_sum(name, a_list, b_list, out_dtype, carry=None, n_part=(0, 1), **tiles):
    products = [(i, i, 0) for i in range(len(a_list))]
    return _single(_matmul(name, "nt", a_list, b_list, products, [out_dtype], _identity, carry=carry, n_part=n_part,
                           **tiles), carry)


def _mm_tn(name, a, b_list, out_dtype, carry=None, **tiles):
    products = [(0, i, i) for i in range(len(b_list))]
    return _matmul(name, "tn", [a], b_list, products, [out_dtype] * len(b_list), _identity, carry=carry, **tiles)


def _rowwise(name, fn, row_ins, par_ins, row_outs, acc_outs=(), tr=512):
    first = row_ins[0][0] if isinstance(row_ins[0], tuple) else row_ins[0]
    t_dim = first.shape[0]
    tr = _tile(t_dim, tr, 16)
    arrays, specs = [], []
    for r in row_ins:
        if isinstance(r, tuple):
            arr, width, blk = r
            specs.append(pl.BlockSpec((tr, width), lambda i, blk=blk: (i, blk)))
        else:
            arr = r
            specs.append(pl.BlockSpec((tr, arr.shape[1]), lambda i: (i, 0)))
        arrays.append(arr)
    for p in par_ins:
        arrays.append(p)
        specs.append(pl.BlockSpec(p.shape, lambda i, nd=p.ndim: (0,) * nd))
    nr, npar, nro, nacc = len(row_ins), len(par_ins), len(row_outs), len(acc_outs)

    def body(*refs):
        rows = [r[...] for r in refs[:nr]]
        pars = [p[...] for p in refs[nr:nr + npar]]
        o_refs = refs[nr + npar:nr + npar + nro]
        acc_refs = refs[nr + npar + nro:]
        outs, accs = fn(rows, pars)
        for o_ref, o in zip(o_refs, outs):
            o_ref[...] = o.astype(o_ref.dtype)
        if nacc:
            @pl.when(pl.program_id(0) == 0)
            def _():
                for a_ref in acc_refs:
                    a_ref[...] = jnp.zeros_like(a_ref)

            for a_ref, a in zip(acc_refs, accs):
                a_ref[...] += a

    out_shape = [jax.ShapeDtypeStruct((t_dim, c), dt) for c, dt in row_outs]
    out_shape += [jax.ShapeDtypeStruct(s, F32) for s in acc_outs]
    out_specs = [pl.BlockSpec((tr, c), lambda i: (i, 0)) for c, _ in row_outs]
    out_specs += [pl.BlockSpec(s, lambda i: (0, 0)) for s in acc_outs]
    return pl.pallas_call(
        body,
        name=name,
        grid=(t_dim // tr,),
        in_specs=specs,
        out_specs=out_specs,
        out_shape=out_shape,
        compiler_params=_params(("arbitrary",)),
    )(*arrays)


def _rms_stats(x):
    r = lax.rsqrt(jnp.mean(x * x, axis=-1, keepdims=True) + EPS)
    return x * r, r


def _rms_backward(x, g, dy):
    xh, r = _rms_stats(x)
    a = dy * g
    dx = r * (a - xh * jnp.mean(a * xh, axis=-1, keepdims=True))
    return dx, jnp.sum(dy * xh, axis=0, keepdims=True)


def _rms_fwd(name, x, g):
    def fn(rows, pars):
        xh, _ = _rms_stats(rows[0])
        return [xh * pars[0]], []

    return _rowwise(name, fn, [x], [g], [(x.shape[1], BF16)])[0]


def _rms_bwd(name, x, g, dy_parts, dres, scale):
    def fn(rows, pars):
        dy = rows[2] if len(rows) == 3 else jnp.concatenate(rows[2:], axis=1)
        dx, dg = _rms_backward(rows[0], pars[0], dy)
        tot = rows[1] + dx
        return [tot, scale * tot], [dg]

    d = x.shape[1]
    return _rowwise(name, fn, [x, dres, *dy_parts], [g], [(d, F32), (d, BF16)], [(1, d)], tr=256)


def _cast_into_gathered(name, w, kind, q_arr):
    rs, cs = w.shape
    tr = _tile(rs, 256, 16)
    nr = rs // tr

    def body(q_ref, w_ref, o_ref):
        o_ref[...] = w_ref[...].astype(BF16)

    if kind == "row":
        o_spec = pl.BlockSpec((tr, cs), lambda i, q_ref: (q_ref[0] * nr + i, 0))
    else:
        o_spec = pl.BlockSpec((tr, cs), lambda i, q_ref: (i, q_ref[0]))
    return pl.pallas_call(
        body,
        name=name,
        grid_spec=pltpu.PrefetchScalarGridSpec(num_scalar_prefetch=1, grid=(nr,),
                                               in_specs=[pl.BlockSpec((tr, cs), lambda i, q_ref: (i, 0))],
                                               out_specs=o_spec),
        out_shape=jax.ShapeDtypeStruct(_full_shape(kind, (rs, cs)), BF16),
        compiler_params=_params(("parallel",)),
    )(q_arr, w)


def _swiglu_tiles(gate, up):
    s = _sigmoid(gate)
    silu = gate * s
    return [silu * up, up * (s * (1.0 + gate * (1.0 - s))), silu]


def _ffn_gateup(tag, n, wg, wu, carry):
    def act(accs, extras):
        return _swiglu_tiles(accs[0], accs[1])

    return _matmul(tag + "_gateup", "nn", [n], [wg, wu], [(0, 0, 0), (0, 1, 1)], [BF16] * 3, act,
                   tm=1024, tn=512, tk=2048, carry=carry)


def _ffn_up(tag, n, wu, gate, carry):
    def act(accs, extras):
        return _swiglu_tiles(extras[0].astype(F32), accs[0])

    return _matmul(tag + "_up", "nn", [n], [wu], [(0, 0, 0)], [BF16] * 3, act, extras=(gate,),
                   tm=1024, tn=512, tk=2048, carry=carry)


def _ffn_down(tag, a, wd, h, carry=None):
    return _mm_nn(tag + "_down", a, wd, F32, res=h, alpha=0.5, tm=1024, tn=512, tk=5632, carry=carry)


def _ffn_bwd(tag, names, h, g, wg, wu, wd, saved, dh, dfb, next_scale, reducer, riding=None, riding_dwd=None,
             last_hop_later=False):
    n, a, da_dgate, da_dup = saved

    def act_bwd(accs, extras):
        return [accs[0] * extras[0].astype(F32), accs[0] * extras[1].astype(F32)]

    dact = _matmul(tag + "_dact", "nt", [dfb], [wd], [(0, 0, 0)], [BF16, BF16], act_bwd, extras=(da_dgate, da_dup),
                   tm=1024, tn=512, tk=2048, carry=riding)
    (dgp, du), rode = dact if riding else (dact, [])
    dwd_call = _mm_tn(tag + "_dwd", a, [dfb], BF16, tm=512, tn=2048, tk=4096, carry=riding_dwd)
    (dwd,), rode_dwd = dwd_call if riding_dwd else (dwd_call, [])
    down = reducer.swap(names[2:], [dwd])
    (dwg,), swapped = _mm_tn(tag + "_dwg", n, [dgp], BF16, tm=512, tn=1408, tk=4096, carry=down.carry)
    down = reducer.send(down, swapped)
    gate_w = reducer.swap(names[:1], [dwg])
    carry, split = _ride(down.carry, gate_w.carry)
    (dwu,), results = _mm_tn(tag + "_dwu", n, [du], BF16, tm=512, tn=1408, tk=4096, carry=carry)
    got, swapped = split(results)
    reducer.end(down, got)
    gate_w = reducer.send(gate_w, swapped)
    up_w = reducer.swap(names[1:2], [dwu])
    carry, split = _ride(gate_w.carry, up_w.carry)
    halves = 1 if last_hop_later else 2
    dn_lo, results = _mm_nt_sum(tag + "_dn_lo", [dgp, du], [wg, wu], F32, tm=1024, tn=1024, tk=1408, carry=carry,
                                n_part=(0, halves))
    got, swapped = split(results)
    reducer.end(gate_w, got)
    up_w = reducer.send(up_w, swapped)
    dn = [dn_lo]
    if not last_hop_later:
        dn_hi, got = _mm_nt_sum(tag + "_dn_hi", [dgp, du], [wg, wu], F32, tm=1024, tn=1024, tk=1408,
                                carry=up_w.carry, n_part=(1, 2))
        reducer.end(up_w, got)
        dn.append(dn_hi)
    dh_in, dh_in_b, dg = _rms_bwd(tag + "_dnorm", h, g, dn, dh, next_scale)
    return dh_in, dh_in_b, dg, rode, rode_dwd, (up_w if last_hop_later else None)


def _ssm_discretize(log_dt, a_re, a_im, b_re, b_im):
    dt = jnp.exp(log_dt)[:, None]
    lr = jnp.minimum(a_re, -1e-4)
    li = a_im
    mag = jnp.exp(lr * dt)
    ang = li * dt
    abar_r = mag * jnp.cos(ang)
    abar_i = mag * jnp.sin(ang)
    den = lr * lr + li * li
    xr = abar_r - 1.0
    xi = abar_i
    zr = (xr * lr + xi * li) / den
    zi = (xi * lr - xr * li) / den
    bbar_r = zr[..., None] * b_re - zi[..., None] * b_im
    bbar_i = zr[..., None] * b_im + zi[..., None] * b_re
    return abar_r, abar_i, bbar_r, bbar_i


def _blockdiag_in(b):
    g, n, p = b.shape
    nb = g // GROUPS_PER_BLOCK
    eye = jnp.eye(GROUPS_PER_BLOCK, dtype=b.dtype)
    b4 = b.reshape(nb, GROUPS_PER_BLOCK, n, p)
    return jnp.einsum("sgnp,gh->sgphn", b4, eye).reshape(nb, GROUPS_PER_BLOCK * p, GROUPS_PER_BLOCK * n)


def _blockdiag_in_grad(gw, n, p):
    nb = gw.shape[0]
    eye = jnp.eye(GROUPS_PER_BLOCK, dtype=gw.dtype)
    g5 = gw.reshape(nb, GROUPS_PER_BLOCK, p, GROUPS_PER_BLOCK, n)
    return jnp.einsum("sgphn,gh->sgnp", g5, eye).reshape(nb * GROUPS_PER_BLOCK, n, p)


def _blockdiag_out(c):
    g, p, n = c.shape
    nb = g // GROUPS_PER_BLOCK
    eye = jnp.eye(GROUPS_PER_BLOCK, dtype=c.dtype)
    c4 = c.reshape(nb, GROUPS_PER_BLOCK, p, n)
    return jnp.einsum("sgpn,gh->shngp", c4, eye).reshape(nb, GROUPS_PER_BLOCK * n, GROUPS_PER_BLOCK * p)


def _blockdiag_out_grad(gw, p, n):
    nb = gw.shape[0]
    eye = jnp.eye(GROUPS_PER_BLOCK, dtype=gw.dtype)
    g5 = gw.reshape(nb, GROUPS_PER_BLOCK, n, GROUPS_PER_BLOCK, p)
    return jnp.einsum("shngp,gh->sgpn", g5, eye).reshape(nb * GROUPS_PER_BLOCK, p, n)


def _ssm_scan_fwd(z, wb, wc, abar, d, carry=None):
    t_dim = z.shape[0]
    nb, cb, sw2 = wb.shape
    nl = sw2 // LANES
    hl = nl // 2
    tt = _tile(t_dim, 256, 8)
    nt = t_dim // tt

    def body(z_ref, wb_ref, wc_ref, a_ref, d_ref, s_ref, y_ref, yg_ref, drive_ref, st_ref):
        @pl.when(pl.program_id(0) == 0)
        def _():
            st_ref[...] = jnp.zeros_like(st_ref)

        u = z_ref[...]
        ub = u.astype(BF16)
        for b in range(nb):
            drive = _dot(ub[:, b * cb:(b + 1) * cb], wb_ref[b], "nn")
            for l in range(nl):
                drive_ref[l, pl.ds(b, tt, stride=nb), :] = drive[:, l * LANES:(l + 1) * LANES]
        a = a_ref[...]
        chunk = lambda v, l: v[:, l * LANES:(l + 1) * LANES]

        def step(t, state):
            rows = pl.ds(pl.multiple_of(t * nb, nb), nb)
            re, im = [], []
            for l in range(hl):
                ar, ai, sr, si = chunk(a, l), chunk(a, hl + l), state[l], state[hl + l]
                nr = ar * sr - ai * si + drive_ref[l, rows, :]
                ni = ar * si + ai * sr + drive_ref[hl + l, rows, :]
                s_ref[l, rows, :] = nr
                s_ref[hl + l, rows, :] = ni
                re.append(nr)
                im.append(ni)
            return tuple(re + im)

        state = lax.fori_loop(0, tt, step, tuple(st_ref[l] for l in range(nl)), unroll=8)
        for l in range(nl):
            st_ref[l] = state[l]
        parts = []
        for b in range(nb):
            s_b = jnp.concatenate([s_ref[l, pl.ds(b, tt, stride=nb), :] for l in range(nl)], axis=1)
            parts.append(_dot(s_b, wc_ref[b], "nn"))
        y = jnp.concatenate(parts, axis=1) + d_ref[...] * u
        y_ref[...] = y
        yg_ref[...] = _gelu(y).astype(BF16)

    full = lambda a: pl.BlockSpec(a.shape, lambda t, nd=a.ndim: (0,) * nd)
    return _carried_call(
        body, carry,
        name="ssm_scan_fwd",
        grid=(nt,),
        in_specs=[pl.BlockSpec((tt, nb * cb), lambda t: (t, 0)), full(wb), full(wc), full(abar), full(d)],
        out_specs=[
            pl.BlockSpec((nl, tt * nb, LANES), lambda t: (0, t, 0)),
            pl.BlockSpec((tt, nb * cb), lambda t: (t, 0)),
            pl.BlockSpec((tt, nb * cb), lambda t: (t, 0)),
        ],
        out_shape=[
            jax.ShapeDtypeStruct((nl, t_dim * nb, LANES), F32),
            jax.ShapeDtypeStruct((t_dim, nb * cb), F32),
            jax.ShapeDtypeStruct((t_dim, nb * cb), BF16),
        ],
        scratch_shapes=[pltpu.VMEM((nl, tt * nb, LANES), F32), pltpu.VMEM((nl, nb, LANES), F32)],
        semantics=("arbitrary",),
        args=[z, wb, wc, abar, d],
    )


def _ssm_scan_bwd(z, dy, states, wb, wc, abar_conj, d, carry=None):
    t_dim = z.shape[0]
    nb, cb, sw2 = wb.shape
    nl = sw2 // LANES
    hl = nl // 2
    tt = _tile(t_dim, 128, 8)
    nt = t_dim // tt
    edges = states.reshape(nl, nt, tt * nb, LANES)[:, :, (tt - 1) * nb:, :]
    before = jnp.concatenate([jnp.zeros((nl, 1, nb, LANES), F32), edges[:, :-1]], axis=1).reshape(nl, nt * nb, LANES)

    def body(z_ref, dy_ref, s_ref, sp_ref, wb_ref, wc_ref, a_ref, d_ref,
             dz_ref, gwb_ref, gwc_ref, ga_ref, gd_ref, gin_ref, gs_ref, st_ref):
        @pl.when(pl.program_id(0) == 0)
        def _():
            st_ref[...] = jnp.zeros_like(st_ref)
            gwb_ref[...] = jnp.zeros_like(gwb_ref)
            gwc_ref[...] = jnp.zeros_like(gwc_ref)
            ga_ref[...] = jnp.zeros_like(ga_ref)
            gd_ref[...] = jnp.zeros_like(gd_ref)

        u = z_ref[...]
        dyv = dy_ref[...]
        ub = u.astype(BF16)
        dyb = dyv.astype(BF16)
        for b in range(nb):
            gin = _dot(dyb[:, b * cb:(b + 1) * cb], wc_ref[b], "nt")
            for l in range(nl):
                gin_ref[l, pl.ds(b, tt, stride=nb), :] = gin[:, l * LANES:(l + 1) * LANES]
        a = a_ref[...]
        chunk = lambda v, l: v[:, l * LANES:(l + 1) * LANES]

        def step(k, state):
            rows = pl.ds(pl.multiple_of((tt - 1 - k) * nb, nb), nb)
            re, im = [], []
            for l in range(hl):
                ar, ai, gr, gi = chunk(a, l), chunk(a, hl + l), state[l], state[hl + l]
                nr = ar * gr - ai * gi + gin_ref[l, rows, :]
                ni = ar * gi + ai * gr + gin_ref[hl + l, rows, :]
                gs_ref[l, rows, :] = nr
                gs_ref[hl + l, rows, :] = ni
                re.append(nr)
                im.append(ni)
            return tuple(re + im)

        state = lax.fori_loop(0, tt, step, tuple(st_ref[l] for l in range(nl)), unroll=8)
        for l in range(nl):
            st_ref[l] = state[l]

        parts = []
        for b in range(nb):
            cols = slice(b * cb, (b + 1) * cb)
            gs_b = jnp.concatenate([gs_ref[l, pl.ds(b, tt, stride=nb), :] for l in range(nl)], axis=1)
            s_b = jnp.concatenate([s_ref[l, pl.ds(b, tt, stride=nb), :] for l in range(nl)], axis=1)
            parts.append(_dot(gs_b, wb_ref[b], "nt"))
            gwb_ref[b] += _dot(ub[:, cols], gs_b, "tn")
            gwc_ref[b] += _dot(s_b, dyb[:, cols], "tn")
        dz_ref[...] = (jnp.concatenate(parts, axis=1) + d_ref[...] * dyv).astype(BF16)
        gd_ref[...] += jnp.sum(dyv * u, axis=0, keepdims=True)

        row = lax.broadcasted_iota(jnp.int32, (tt * nb, LANES), 0)
        shifted = lambda v: jnp.where(row < nb, 0.0, pltpu.roll(v, nb, 0))
        over_time = lambda v: jnp.sum(v.reshape(tt, nb, LANES), axis=0)
        for l in range(hl):
            g_r, g_i = gs_ref[l], gs_ref[hl + l]
            p_r, p_i = shifted(s_ref[l]), shifted(s_ref[hl + l])
            f_r, f_i = sp_ref[l], sp_ref[hl + l]
            g0_r, g0_i = gs_ref[l, pl.ds(0, nb), :], gs_ref[hl + l, pl.ds(0, nb), :]
            ga_ref[l] += over_time(g_r * p_r + g_i * p_i) + g0_r * f_r + g0_i * f_i
            ga_ref[hl + l] += over_time(g_i * p_r - g_r * p_i) + g0_i * f_r - g0_r * f_i

    rev = lambda t: (nt - 1 - t, 0)
    rev3 = lambda t: (0, nt - 1 - t, 0)
    full = lambda a: pl.BlockSpec(a.shape, lambda t, nd=a.ndim: (0,) * nd)
    return _carried_call(
        body, carry,
        name="ssm_scan_bwd",
        grid=(nt,),
        in_specs=[
            pl.BlockSpec((tt, nb * cb), rev),
            pl.BlockSpec((tt, nb * cb), rev),
            pl.BlockSpec((nl, tt * nb, LANES), rev3),
            pl.BlockSpec((nl, nb, LANES), rev3),
            full(wb), full(wc), full(abar_conj), full(d),
        ],
        out_specs=[
            pl.BlockSpec((tt, nb * cb), rev),
            pl.BlockSpec((nb, cb, sw2), lambda t: (0, 0, 0)),
            pl.BlockSpec((nb, sw2, cb), lambda t: (0, 0, 0)),
            pl.BlockSpec((nl, nb, LANES), lambda t: (0, 0, 0)),
            pl.BlockSpec((1, nb * cb), lambda t: (0, 0)),
        ],
        out_shape=[
            jax.ShapeDtypeStruct((t_dim, nb * cb), BF16),
            jax.ShapeDtypeStruct((nb, cb, sw2), F32),
            jax.ShapeDtypeStruct((nb, sw2, cb), F32),
            jax.ShapeDtypeStruct((nl, nb, LANES), F32),
            jax.ShapeDtypeStruct((1, nb * cb), F32),
        ],
        scratch_shapes=[pltpu.VMEM((nl, tt * nb, LANES), F32), pltpu.VMEM((nl, tt * nb, LANES), F32),
                        pltpu.VMEM((nl, nb, LANES), F32)],
        semantics=("arbitrary",),
        args=[z, dy, states, before, wb, wc, abar_conj, d],
    )


def _gmlp_chunk(zu, zv, gv, wm_ref, bias, n_heads):
    ua = _gelu(zu)
    vg = _gelu(zv)
    xc = vg - jnp.mean(vg, axis=-1, keepdims=True)
    r = lax.rsqrt(jnp.mean(xc * xc, axis=-1, keepdims=True) + EPS)
    vh = xc * r
    vb = (vh * gv).astype(BF16)
    parts = []
    for h in range(n_heads):
        cols = slice(h * GMLP_HEAD, (h + 1) * GMLP_HEAD)
        parts.append(_dot(wm_ref[h], vb[:, cols], "nn"))
    s = jnp.concatenate(parts, axis=1) + bias
    return ua, vh, r, vb, s


def _gmlp_fwd(z, gv, wm, bias, ggo):
    t_dim = z.shape[0]
    dg = gv.shape[1]
    n_heads = dg // GMLP_HEAD
    tr = _tile(t_dim, 256, CHUNK)

    def body(zu_ref, zv_ref, gv_ref, wm_ref, b_ref, ggo_ref, o_ref):
        for ck in range(tr // CHUNK):
            rows = pl.ds(ck * CHUNK, CHUNK)
            ua, _, _, _, s = _gmlp_chunk(zu_ref[rows, :], zv_ref[rows, :], gv_ref[...], wm_ref, b_ref[...], n_heads)
            yh, _ = _rms_stats(ua * s)
            o_ref[rows, :] = (yh * ggo_ref[...]).astype(BF16)

    full = lambda a: pl.BlockSpec(a.shape, lambda i, nd=a.ndim: (0,) * nd)
    return pl.pallas_call(
        body,
        name="gmlp_fwd",
        grid=(t_dim // tr,),
        in_specs=[pl.BlockSpec((tr, dg), lambda i: (i, 1)), pl.BlockSpec((tr, dg), lambda i: (i, 2)),
                  full(gv), full(wm), full(bias), full(ggo)],
        out_specs=pl.BlockSpec((tr, dg), lambda i: (i, 0)),
        out_shape=jax.ShapeDtypeStruct((t_dim, dg), BF16),
        compiler_params=_params(("parallel",)),
    )(z, z, gv, wm, bias, ggo)


def _gmlp_bwd(z, dycat, gv, wm, bias, ggo):
    t_dim = z.shape[0]
    dg = gv.shape[1]
    n_heads = dg // GMLP_HEAD
    tr = _tile(t_dim, 256, CHUNK)

    def body(zu_ref, zv_ref, dy_ref, gv_ref, wm_ref, b_ref, ggo_ref,
             dzu_ref, dzv_ref, dggo_ref, dgv_ref, dwm_ref, dsum_ref):
        @pl.when(pl.program_id(0) == 0)
        def _():
            dggo_ref[...] = jnp.zeros_like(dggo_ref)
            dgv_ref[...] = jnp.zeros_like(dgv_ref)
            dwm_ref[...] = jnp.zeros_like(dwm_ref)
            dsum_ref[...] = jnp.zeros_like(dsum_ref)

        for ck in range(tr // CHUNK):
            rows = pl.ds(ck * CHUNK, CHUNK)
            zu = zu_ref[rows, :]
            zv = zv_ref[rows, :]
            gvv = gv_ref[...]
            ua, vh, r, vb, s = _gmlp_chunk(zu, zv, gvv, wm_ref, b_ref[...], n_heads)
            dy, dggo = _rms_backward(ua * s, ggo_ref[...], dy_ref[rows, :])
            dggo_ref[...] += dggo
            ds = dy * ua
            dsum_ref[...] += ds
            dsb = ds.astype(BF16)
            parts = []
            for h in range(n_heads):
                cols = slice(h * GMLP_HEAD, (h + 1) * GMLP_HEAD)
                dwm_ref[h] += _dot(dsb[:, cols], vb[:, cols], "nt")
                parts.append(_dot(wm_ref[h], dsb[:, cols], "tn"))
            dv = jnp.concatenate(parts, axis=1)
            dgv_ref[...] += jnp.sum(dv * vh, axis=0, keepdims=True)
            dvh = dv * gvv
            dvg = r * (dvh - jnp.mean(dvh, axis=-1, keepdims=True) - vh * jnp.mean(dvh * vh, axis=-1, keepdims=True))
            dzv_ref[rows, :] = (dvg * _gelu_grad(zv)).astype(BF16)
            dzu_ref[rows, :] = (dy * s * _gelu_grad(zu)).astype(BF16)

    full = lambda a: pl.BlockSpec(a.shape, lambda i, nd=a.ndim: (0,) * nd)
    return pl.pallas_call(
        body,
        name="gmlp_bwd",
        grid=(t_dim // tr,),
        in_specs=[pl.BlockSpec((tr, dg), lambda i: (i, 1)), pl.BlockSpec((tr, dg), lambda i: (i, 2)),
                  pl.BlockSpec((tr, dg), lambda i: (i, 1)), full(gv), full(wm), full(bias), full(ggo)],
        out_specs=[pl.BlockSpec((tr, dg), lambda i: (i, 0)), pl.BlockSpec((tr, dg), lambda i: (i, 0)),
                   pl.BlockSpec((1, dg), lambda i: (0, 0)), pl.BlockSpec((1, dg), lambda i: (0, 0)),
                   pl.BlockSpec(wm.shape, lambda i: (0, 0, 0)), pl.BlockSpec((CHUNK, dg), lambda i: (0, 0))],
        out_shape=[jax.ShapeDtypeStruct((t_dim, dg), BF16), jax.ShapeDtypeStruct((t_dim, dg), BF16),
                   jax.ShapeDtypeStruct((1, dg), F32), jax.ShapeDtypeStruct((1, dg), F32),
                   jax.ShapeDtypeStruct(wm.shape, F32), jax.ShapeDtypeStruct((CHUNK, dg), F32)],
        compiler_params=_params(("arbitrary",)),
    )(z, z, dycat, gv, wm, bias, ggo)


def _ple_head(npl, w_gate, h3, pp, tgt, g_final):
    t_dim, d = h3.shape
    tr = _tile(t_dim, 256, 16)

    def body(n_ref, w_ref, h_ref, pp_ref, t_ref, g_ref, dgq_ref, dpp_ref, dh_ref, dg_ref, loss_ref):
        @pl.when(pl.program_id(0) == 0)
        def _():
            dg_ref[...] = jnp.zeros_like(dg_ref)
            loss_ref[...] = jnp.zeros_like(loss_ref)

        gate = _sigmoid(_dot(n_ref[...], w_ref[...], "nn"))
        ppv = pp_ref[...]
        h4 = h_ref[...] + gate * ppv
        xh, _ = _rms_stats(h4)
        err = xh * g_ref[...] - t_ref[...]
        dh4, dg = _rms_backward(h4, g_ref[...], err * (1.0 / d))
        dh_ref[...] = dh4
        dgq_ref[...] = (dh4 * ppv * gate * (1.0 - gate)).astype(BF16)
        dpp_ref[...] = (dh4 * gate).astype(BF16)
        dg_ref[...] += dg
        loss_ref[...] += jnp.full((1, LANES), 0.5 * jnp.sum(err * err) * (1.0 / d), F32)

    rows = pl.BlockSpec((tr, d), lambda i: (i, 0))
    whole = lambda a: pl.BlockSpec(a.shape, lambda i: (0, 0))
    return pl.pallas_call(
        body,
        name="ple_head",
        grid=(t_dim // tr,),
        in_specs=[rows, whole(w_gate), rows, rows, rows, whole(g_final)],
        out_specs=[rows, rows, rows, pl.BlockSpec((1, d), lambda i: (0, 0)), pl.BlockSpec((1, LANES), lambda i: (0, 0))],
        out_shape=[jax.ShapeDtypeStruct((t_dim, d), BF16), jax.ShapeDtypeStruct((t_dim, d), BF16),
                   jax.ShapeDtypeStruct((t_dim, d), F32), jax.ShapeDtypeStruct((1, d), F32),
                   jax.ShapeDtypeStruct((1, LANES), F32)],
        compiler_params=_params(("arbitrary",)),
    )(npl, w_gate, h3, pp, tgt, g_final)


def _position():
    x, y, c = lax.axis_index("x"), lax.axis_index("y"), lax.axis_index("c")
    chips = [(1 - x, y), (x, 1 - y), (1 - x, 1 - y)]
    return x, y, c, chips


def _region(ref, kind, shard_shape, q, half, part=None):
    rs, cs = shard_shape
    r0, nr = (0, rs) if half is None else (half * (rs // 2), rs // 2)
    if part is not None:
        r0, nr = r0 + part * (rs // 4), rs // 4
    if kind == "row":
        return ref.at[pl.ds(q * rs + r0, nr), :]
    return ref.at[pl.ds(r0, nr), pl.ds(q * cs, cs)]


def _full_shape(kind, shard_shape):
    rs, cs = shard_shape
    return (N_CHIPS * rs, cs) if kind == "row" else (rs, N_CHIPS * cs)


def _remote(src, dst, send_sems, recv_sems, k, to):
    return pltpu.make_async_remote_copy(src_ref=src, dst_ref=dst, send_sem=send_sems.at[k], recv_sem=recv_sems.at[k],
                                        device_id=to, device_id_type=MESH)


def _same(arrays):
    return [jax.ShapeDtypeStruct(a.shape, a.dtype) for a in arrays]


def _gather_near_carry(gathered, kinds, shapes):
    nw = len(gathered)

    def copies(ops, full, send_sems, recv_sems):
        x, y, c, _ = _position()
        out = []
        for w in range(nw):
            mine = _region(full[w], kinds[w], shapes[w], 2 * x + y, c)
            out.append(_remote(mine, mine, send_sems, recv_sems, 2 * w, (1 - x, y, c)))
            out.append(_remote(mine, mine, send_sems, recv_sems, 2 * w + 1, (x, 1 - y, c)))
        return out

    return _Carry(gathered, _same(gathered), {i: i for i in range(nw)}, 2 * nw, copies)


def _gather_far_carry(gathered, kinds, shapes):
    nw = len(gathered)

    def copies(ops, full, send_sems, recv_sems):
        x, y, c, _ = _position()
        out = []
        for w in range(nw):
            from_x = _region(full[w], kinds[w], shapes[w], 2 * (1 - x) + y, c, part=1)
            from_y = _region(full[w], kinds[w], shapes[w], 2 * x + (1 - y), c, part=0)
            out.append(_remote(from_x, from_x, send_sems, recv_sems, 2 * w, (x, 1 - y, c)))
            out.append(_remote(from_y, from_y, send_sems, recv_sems, 2 * w + 1, (1 - x, y, c)))
        return out

    return _Carry(gathered, _same(gathered), {i: i for i in range(nw)}, 2 * nw, copies)


def _gather_d2d_carry(gathered, kinds, shapes):
    nw = len(gathered)

    def copies(ops, full, send_sems, recv_sems):
        x, y, c, chips = _position()
        out = []
        for w in range(nw):
            for j, (cx, cy) in enumerate(chips):
                landed = _region(full[w], kinds[w], shapes[w], 2 * cx + cy, c)
                out.append(_remote(landed, landed, send_sems, recv_sems, 3 * w + j, (x, y, 1 - c)))
        return out

    return _Carry(gathered, _same(gathered), {i: i for i in range(nw)}, 3 * nw, copies)


def _pairs_carry(grads, kinds, shapes):
    nw = len(grads)

    def copies(g, got, send_sems, recv_sems):
        x, y, c, _ = _position()
        out = []
        for w in range(nw):
            for q in range(N_CHIPS):
                out.append(_remote(_region(g[w], kinds[w], shapes[w], q, 1 - c), got[w].at[q], send_sems, recv_sems,
                                   N_CHIPS * w + q, (x, y, 1 - c)))
        return out

    outs = [jax.ShapeDtypeStruct((N_CHIPS, s[0] // 2, s[1]), BF16) for s in shapes]
    return _Carry(grads, outs, {}, N_CHIPS * nw, copies)


def _pair_sum(name, grad, got, kind, shard_shape, c_arr):
    rs, cs = shard_shape
    hr = rs // 2
    tr = _tile(hr, 512, 16)
    nr = hr // tr

    def body(c_ref, g_ref, s_ref, o_ref):
        o_ref[...] = (g_ref[...].astype(F32) + s_ref[...].astype(F32)).astype(BF16)

    if kind == "row":
        g_spec = pl.BlockSpec((tr, cs), lambda q, i, c_ref: (q * (rs // tr) + c_ref[0] * nr + i, 0))
    else:
        g_spec = pl.BlockSpec((tr, cs), lambda q, i, c_ref: (c_ref[0] * nr + i, q))
    blk = pl.BlockSpec((None, tr, cs), lambda q, i, c_ref: (q, i, 0))
    return pl.pallas_call(
        body,
        name=name,
        grid_spec=pltpu.PrefetchScalarGridSpec(num_scalar_prefetch=1, grid=(N_CHIPS, nr), in_specs=[g_spec, blk],
                                               out_specs=blk),
        out_shape=jax.ShapeDtypeStruct((N_CHIPS, hr, cs), BF16),
        compiler_params=_params(("parallel", "parallel")),
    )(c_arr, grad, got)


def _scatter_carry(sums, shapes):
    nw = len(sums)

    def copies(ps, got, send_sems, recv_sems):
        x, y, c, chips = _position()
        out = []
        for w in range(nw):
            for j, (cx, cy) in enumerate(chips):
                out.append(_remote(ps[w].at[2 * cx + cy], got[w].at[j], send_sems, recv_sems, 3 * w + j, (cx, cy, c)))
        return out

    outs = [jax.ShapeDtypeStruct((3, s[0] // 2, s[1]), BF16) for s in shapes]
    return _Carry(sums, outs, {}, 3 * nw, copies)


def _owner_sum(name, sums, got, shard_shape, qc_arr):
    rs, cs = shard_shape
    hr = rs // 2
    tr = _tile(hr, 512, 16)
    nr = hr // tr

    def body(qc_ref, mine_ref, got_ref, o_ref):
        acc = mine_ref[...].astype(F32)
        for j in range(3):
            acc = acc + got_ref[j].astype(F32)
        o_ref[...] = acc

    return pl.pallas_call(
        body,
        name=name,
        grid_spec=pltpu.PrefetchScalarGridSpec(
            num_scalar_prefetch=1, grid=(nr,),
            in_specs=[pl.BlockSpec((None, tr, cs), lambda i, qc_ref: (qc_ref[0], i, 0)),
                      pl.BlockSpec((3, tr, cs), lambda i, qc_ref: (0, i, 0))],
            out_specs=pl.BlockSpec((tr, cs), lambda i, qc_ref: (qc_ref[1] * nr + i, 0))),
        out_shape=jax.ShapeDtypeStruct((rs, cs), F32),
        compiler_params=_params(("parallel",)),
    )(qc_arr, sums, got)


def _share_carry(grads, shapes):
    nw = len(grads)

    def copies(ops, out, send_sems, recv_sems):
        x, y, c, _ = _position()
        res = []
        for w in range(nw):
            hr = shapes[w][0] // 2
            mine = out[w].at[pl.ds(c * hr, hr), :]
            res.append(_remote(mine, mine, send_sems, recv_sems, w, (x, y, 1 - c)))
        return res

    return _Carry(grads, _same(grads), {i: i for i in range(nw)}, nw, copies)


def _place_block(packed, me):
    return lax.dynamic_update_slice(jnp.zeros((N_DEV,) + packed.shape, F32), packed[None], (me, 0, 0))


def _exchange_carry(blocks):
    def copies(ops, res, send_sems, recv_sems):
        x, y, c, _ = _position()
        mine = res[0].at[4 * x + 2 * y + c]
        out = []
        for k in range(1, N_DEV):
            to = ((1 - x) if k & 4 else x, (1 - y) if k & 2 else y, (1 - c) if k & 1 else c)
            out.append(_remote(mine, mine, send_sems, recv_sems, k - 1, to))
        return out

    return _Carry([blocks], _same([blocks]), {0: 0}, N_DEV - 1, copies)


def _sum_blocks(name, blocks):
    n, rows, lanes = blocks.shape
    tr = _tile(rows, 4096, 8)

    def body(b_ref, o_ref):
        acc = b_ref[0]
        for k in range(1, n):
            acc = acc + b_ref[k]
        o_ref[...] = acc

    return pl.pallas_call(
        body,
        name=name,
        grid=(rows // tr,),
        in_specs=[pl.BlockSpec((n, tr, lanes), lambda i: (0, i, 0))],
        out_specs=pl.BlockSpec((tr, lanes), lambda i: (i, 0)),
        out_shape=jax.ShapeDtypeStruct((rows, lanes), F32),
        compiler_params=_params(("parallel",)),
    )(blocks)


def _adamw(name, w, g, m, v):
    def fn(rows, pars):
        wv, gv, mv, vv = rows
        m_new = ADAM_B1 * mv + (1.0 - ADAM_B1) * gv
        v_new = ADAM_B2 * vv + (1.0 - ADAM_B2) * (gv * gv)
        m_hat = m_new / (1.0 - ADAM_B1 ** ADAM_STEP)
        v_hat = v_new / (1.0 - ADAM_B2 ** ADAM_STEP)
        delta = -ADAM_LR * (m_hat / (jnp.sqrt(v_hat) + ADAM_EPS) + ADAM_WD * wv)
        return [delta, m_new, v_new, gv], []

    c = w.shape[1]
    return _rowwise(name, fn, [w, g, m, v], [], [(c, F32)] * 4, tr=256)


def _pack(arrays):
    rows = []
    for a in arrays:
        flat = a.reshape(-1).astype(F32)
        pad = (-flat.shape[0]) % LANES
        rows.append(jnp.pad(flat, (0, pad)).reshape(-1, LANES))
    stacked = jnp.concatenate(rows, axis=0)
    pad_rows = (-stacked.shape[0]) % 8
    return jnp.pad(stacked, ((0, pad_rows), (0, 0)))


def _unpack(packed, shapes):
    out, r = [], 0
    for s in shapes:
        n = math.prod(s)
        nr = -(-n // LANES)
        out.append(packed[r:r + nr].reshape(-1)[:n].reshape(s))
        r += nr
    return out


BIG = ["w1_gate", "w1_up", "w1_down", "w_in", "ssm_w_glu", "w_out", "w2_gate", "w2_up", "w2_down", "w_ple_gate",
       "w_ple_proj"]
KIND = {"w1_gate": "col", "w1_up": "col", "w1_down": "row", "w_in": "col", "ssm_w_glu": "row", "w_out": "row",
        "w2_gate": "col", "w2_up": "col", "w2_down": "row", "w_ple_gate": "row", "w_ple_proj": "col"}
SMALL = ["norm_ffn1", "norm_mix", "ssm_log_dt", "ssm_a_re", "ssm_a_im", "ssm_b_re", "ssm_b_im", "ssm_c_re", "ssm_c_im",
         "ssm_d", "gmlp_norm_v", "gmlp_w_s", "gmlp_b_s", "norm_ssm_out", "norm_gmlp_out", "norm_ffn2", "norm_ple",
         "norm_final"]
WEIGHTS = ["norm_ffn1", "w1_gate", "w1_up", "w1_down", "norm_mix", "w_in", "ssm_log_dt", "ssm_a_re", "ssm_a_im",
           "ssm_b_re", "ssm_b_im", "ssm_c_re", "ssm_c_im", "ssm_d", "ssm_w_glu", "gmlp_norm_v", "gmlp_w_s", "gmlp_b_s",
           "norm_ssm_out", "norm_gmlp_out", "w_out", "norm_ffn2", "w2_gate", "w2_up", "w2_down", "norm_ple",
           "w_ple_gate", "w_ple_proj", "norm_final"]


class _Trip:
    def __init__(self, names, arrays, carry):
        self.names, self.arrays, self.carry = names, arrays, carry


class _Reducer:
    def __init__(self, shard_shape, c_arr, qc_arr):
        self.shard_shape, self.c_arr, self.qc_arr = shard_shape, c_arr, qc_arr
        self.halves = {}

    def swap(self, names, grads):
        kinds = [KIND[n] for n in names]
        shapes = [self.shard_shape[n] for n in names]
        return _Trip(names, grads, _pairs_carry(grads, kinds, shapes))

    def send(self, trip, swapped):
        shapes = [self.shard_shape[n] for n in trip.names]
        sums = [_pair_sum("pair_sum_" + n, g, s, KIND[n], sh, self.c_arr)
                for n, g, s, sh in zip(trip.names, trip.arrays, swapped, shapes)]
        return _Trip(trip.names, sums, _scatter_carry(sums, shapes))

    def end(self, trip, got):
        for n, ps, g in zip(trip.names, trip.arrays, got):
            self.halves[n] = _owner_sum("owner_sum_" + n, ps, g, self.shard_shape[n], self.qc_arr)


def _ride(*carries):
    present = [c for c in carries if c is not None]
    joined = functools.reduce(_join, present) if present else None

    def split(results):
        out, at = [], 0
        for c in carries:
            n = len(c.out_shapes) if c is not None else 0
            out.append(list(results[at:at + n]))
            at += n
        return out

    return joined, split


def _step(x, p, tgt, w, m, v):
    d_model = x.shape[1]
    d_ssm = w["ssm_d"].shape[1]
    n_groups = d_ssm // SSM_GROUP
    row = lambda a: a.reshape(1, -1)

    xi, yi, ci = lax.axis_index("x"), lax.axis_index("y"), lax.axis_index("c")
    c_arr = jnp.reshape(ci, (1,)).astype(jnp.int32)
    q_arr = jnp.reshape(2 * xi + yi, (1,)).astype(jnp.int32)
    qc_arr = jnp.stack([2 * xi + yi, ci]).astype(jnp.int32)
    shard_shape = {n: w[n].shape for n in BIG}
    full = {n: _cast_into_gathered("cast_" + n, w[n], KIND[n], q_arr) for n in BIG}

    def gather(stage, names):
        return stage([full[n] for n in names], [KIND[n] for n in names], [shard_shape[n] for n in names])

    def gathered(names, arrays):
        full.update(zip(names, arrays))

    groups = [["w1_gate"], ["w1_up"], ["w1_down", "w_in"], ["w2_gate"], ["ssm_w_glu", "w_out"], ["w2_up"],
              ["w2_down", "w_ple_gate", "w_ple_proj"]]
    near, far, d2d = _gather_near_carry, _gather_far_carry, _gather_d2d_carry

    def stages(*work):
        carries = [gather(stage, groups[g]) for stage, g in work]
        return functools.reduce(_join, carries), [n for _, g in work for n in groups[g]]

    def alone(name, *work):
        carry, names = stages(*work)
        gathered(names, _comm_call(name, carry))

    alone("gather_a", (near, 0))
    alone("gather_b", (far, 0), (near, 1))
    alone("gather_c", (d2d, 0), (far, 1))
    n1 = _rms_fwd("ffn1_norm", x, w["norm_ffn1"])
    carry, names = stages((d2d, 1), (near, 2))
    gate1, landed = _mm_nn("ffn1_gate", n1, full["w1_gate"], BF16, tm=1024, tn=512, tk=2048, carry=carry)
    gathered(names, landed)
    carry, names = stages((far, 2), (near, 3))
    (a1, da_dgate1, da_dup1), landed = _ffn_up("ffn1", n1, full["w1_up"], gate1, carry)
    gathered(names, landed)
    alone("gather_d", (d2d, 2), (far, 3))
    carry, names = stages((d2d, 3), (near, 4), (near, 5))
    h1, landed = _ffn_down("ffn1", a1, full["w1_down"], x, carry)
    gathered(names, landed)
    ffn1 = (n1, a1, da_dgate1, da_dup1)
    nm = _rms_fwd("mix_norm", h1, w["norm_mix"])
    carry, names = stages((far, 4), (far, 5))
    z, landed = _mm_nn("in_proj", nm, full["w_in"], F32, tm=1024, tn=512, tk=2048, carry=carry)
    gathered(names, landed)

    disc, disc_vjp = jax.vjp(_ssm_discretize, w["ssm_log_dt"][0], w["ssm_a_re"], w["ssm_a_im"], w["ssm_b_re"],
                             w["ssm_b_im"])
    abar_r, abar_i, bbar_r, bbar_i = disc
    nb = n_groups // GROUPS_PER_BLOCK
    wb = jnp.concatenate([_blockdiag_in(bbar_r), _blockdiag_in(bbar_i)], axis=-1).astype(BF16)
    wc = jnp.concatenate([_blockdiag_out(w["ssm_c_re"]), -_blockdiag_out(w["ssm_c_im"])], axis=1).astype(BF16)
    abar = jnp.concatenate([abar_r.reshape(nb, -1), abar_i.reshape(nb, -1)], axis=-1)
    abar_conj = jnp.concatenate([abar_r.reshape(nb, -1), -abar_i.reshape(nb, -1)], axis=-1)
    carry, names = stages((d2d, 4), (d2d, 5), (near, 6))
    (states, y_pre, yg), landed = _ssm_scan_fwd(z, wb, wc, abar, w["ssm_d"], carry)
    gathered(names, landed)
    q = _mm_nn("glu_proj", yg, full["ssm_w_glu"], F32, tm=1024, tn=1024, tk=1024)

    def glu_norm(rows, pars):
        yv = _gelu(rows[0]) * _sigmoid(rows[1])
        yh, _ = _rms_stats(yv)
        return [yh * pars[0]], []

    yn_ssm = _rowwise("ssm_glu_norm", glu_norm, [y_pre, q], [w["norm_ssm_out"]], [(d_ssm, BF16)])[0]

    tril = jnp.tril(jnp.ones((CHUNK, CHUNK), dtype=bool))
    wm = jnp.where(tril[None], w["gmlp_w_s"], 0.0).astype(BF16)
    bias = jnp.repeat(w["gmlp_b_s"].T, GMLP_HEAD, axis=1)
    yn_gmlp = _gmlp_fwd(z, w["gmlp_norm_v"], wm, bias, w["norm_gmlp_out"])
    ycat = jnp.concatenate([yn_ssm, yn_gmlp], axis=1)
    carry, names = stages((far, 6))
    h2, landed = _mm_nn("out_proj", ycat, full["w_out"], F32, res=h1, alpha=1.0, tm=512, tn=1024, tk=2048,
                        carry=carry)
    gathered(names, landed)

    n2 = _rms_fwd("ffn2_norm", h2, w["norm_ffn2"])
    carry, names = stages((d2d, 6))
    (a2, da_dgate2, da_dup2), landed = _ffn_gateup("ffn2", n2, full["w2_gate"], full["w2_up"], carry)
    gathered(names, landed)
    h3 = _ffn_down("ffn2", a2, full["w2_down"], h2)
    ffn2 = (n2, a2, da_dgate2, da_dup2)
    npl = _rms_fwd("ple_norm", h3, w["norm_ple"])
    pp = _mm_nn("ple_proj", p, full["w_ple_proj"], F32, tm=1024, tn=1024, tk=2048)
    dgq, dpp, dh4, g_norm_final, loss_part = _ple_head(npl, full["w_ple_gate"], h3, pp, tgt, row(w["norm_final"]))
    reducer = _Reducer(shard_shape, c_arr, qc_arr)
    (g_w_ple_proj,) = _mm_tn("ple_dwproj", p, [dpp], BF16, tm=256, tn=1024, tk=4096)
    (g_w_ple_gate,) = _mm_tn("ple_dwgate", npl, [dgq], BF16, tm=512, tn=1024, tk=4096)
    ple = reducer.swap(["w_ple_gate", "w_ple_proj"], [g_w_ple_gate, g_w_ple_proj])
    dnpl, swapped = _mm_nt_sum("ple_dnorm_in", [dgq], [full["w_ple_gate"]], F32, tm=512, tn=1024, tk=2048,
                               carry=ple.carry)
    ple = reducer.send(ple, swapped)
    dh3, dh3_b, g_norm_ple = _rms_bwd("ple_dnorm", h3, w["norm_ple"], [dnpl], dh4, 0.5)

    dh2, dh2_b, g_norm_ffn2, got, _, up2_w = _ffn_bwd(
        "ffn2", ["w2_gate", "w2_up", "w2_down"], h2, w["norm_ffn2"], full["w2_gate"], full["w2_up"], full["w2_down"],
        ffn2, dh3, dh3_b, 1.0, reducer, riding=ple.carry, last_hop_later=True)
    reducer.end(ple, got)

    dycat = _mm_nt_sum("out_dproj", [dh2_b], [full["w_out"]], F32, tm=512, tn=1024, tk=2048)
    (g_w_out,) = _mm_tn("out_dw", ycat, [dh2_b], BF16, tm=512, tn=1024, tk=4096)

    dzu, dzv, g_norm_gmlp_out, g_gmlp_norm_v, g_wm, g_s = _gmlp_bwd(z, dycat, w["gmlp_norm_v"], wm, bias,
                                                                  w["norm_gmlp_out"])
    g_gmlp_w_s = jnp.where(tril[None], g_wm, 0.0)
    g_gmlp_b_s = g_s.reshape(CHUNK, -1, GMLP_HEAD).sum(axis=-1).T

    def glu_bwd(rows, pars):
        dyn, ypre, qv = rows
        ygv = _gelu(ypre)
        sg = _sigmoid(qv)
        dy, dg = _rms_backward(ygv * sg, pars[0], dyn)
        return [dy * ygv * sg * (1.0 - sg), dy * sg], [dg]

    dq, dyg_part, g_norm_ssm_out = _rowwise("ssm_dglu", glu_bwd, [(dycat, d_ssm, 0), y_pre, q], [w["norm_ssm_out"]],
                                            [(d_ssm, BF16), (d_ssm, F32)], [(1, d_ssm)])
    dyg_proj = _mm_nt_sum("glu_dproj", [dq], [full["ssm_w_glu"]], F32, tm=1024, tn=1024, tk=1024)
    (g_ssm_w_glu,) = _mm_tn("glu_dw", yg, [dq], BF16, tm=512, tn=1024, tk=4096)

    def gelu_bwd(rows, pars):
        return [(rows[0] + rows[1]) * _gelu_grad(rows[2])], []

    dy_pre = _rowwise("ssm_dgelu", gelu_bwd, [dyg_part, dyg_proj, y_pre], [], [(d_ssm, F32)])[0]
    mixers = reducer.swap(["w_out", "ssm_w_glu"], [g_w_out, g_ssm_w_glu])
    me = 4 * xi + 2 * yi + ci
    small = {"gmlp_norm_v": g_gmlp_norm_v, "gmlp_w_s": g_gmlp_w_s, "gmlp_b_s": g_gmlp_b_s,
             "norm_gmlp_out": g_norm_gmlp_out, "norm_ssm_out": g_norm_ssm_out, "norm_ffn2": g_norm_ffn2,
             "norm_ple": g_norm_ple, "norm_final": g_norm_final}
    before_scan = [n for n in SMALL if n in small]
    scan_blocks = _place_block(_pack([small[n] for n in before_scan] + [loss_part[:, :1]]), me)
    carry, split = _ride(up2_w.carry, mixers.carry, _exchange_carry(scan_blocks))
    (dz_ssm, g_wb, g_wc, g_abar, g_ssm_d), results = _ssm_scan_bwd(z, dy_pre, states, wb, wc, abar_conj, w["ssm_d"],
                                                                  carry)
    got, swapped, (scan_blocks,) = split(results)
    reducer.end(up2_w, got)
    mixers = reducer.send(mixers, swapped)
    g_abar = jnp.transpose(g_abar, (1, 0, 2)).reshape(nb, -1)
    sw = g_abar.shape[-1] // 2
    g_bbar_r = _blockdiag_in_grad(g_wb[..., :sw], SSM_STATE, SSM_GROUP)
    g_bbar_i = _blockdiag_in_grad(g_wb[..., sw:], SSM_STATE, SSM_GROUP)
    g_ssm_c_re = _blockdiag_out_grad(g_wc[:, :sw, :], SSM_GROUP, SSM_STATE)
    g_ssm_c_im = -_blockdiag_out_grad(g_wc[:, sw:, :], SSM_GROUP, SSM_STATE)
    g_abar_r = g_abar[..., :sw].reshape(n_groups, SSM_STATE)
    g_abar_i = g_abar[..., sw:].reshape(n_groups, SSM_STATE)
    g_ssm_log_dt, g_ssm_a_re, g_ssm_a_im, g_ssm_b_re, g_ssm_b_im = disc_vjp((g_abar_r, g_abar_i, g_bbar_r, g_bbar_i))

    dz = jnp.concatenate([dz_ssm, dzu, dzv], axis=1)
    (g_w_in,), got = _mm_tn("in_dw", nm, [dz], BF16, tm=512, tn=1536, tk=4096, carry=mixers.carry)
    reducer.end(mixers, got)
    in_w = reducer.swap(["w_in"], [g_w_in])
    dnm, swapped = _mm_nt_sum("in_dproj", [dz], [full["w_in"]], F32, tm=1024, tn=1024, tk=3072, carry=in_w.carry)
    in_w = reducer.send(in_w, swapped)
    dh1, dh1_b, g_norm_mix = _rms_bwd("mix_dnorm", h1, w["norm_mix"], [dnm], dh2, 0.5)
    small = {"norm_mix": g_norm_mix, "ssm_log_dt": g_ssm_log_dt, "ssm_a_re": g_ssm_a_re,
             "ssm_a_im": g_ssm_a_im, "ssm_b_re": g_ssm_b_re, "ssm_b_im": g_ssm_b_im, "ssm_c_re": g_ssm_c_re,
             "ssm_c_im": g_ssm_c_im, "ssm_d": g_ssm_d}
    after_scan = [n for n in SMALL if n in small]
    ffn_blocks = _place_block(_pack([small[n] for n in after_scan]), me)
    dx, _, g_norm_ffn1, got, (ffn_blocks,), _ = _ffn_bwd(
        "ffn1", ["w1_gate", "w1_up", "w1_down"], x, w["norm_ffn1"], full["w1_gate"], full["w1_up"], full["w1_down"],
        ffn1, dh1, dh1_b, 1.0, reducer, riding=in_w.carry, riding_dwd=_exchange_carry(ffn_blocks))
    reducer.end(in_w, got)
    scan_grads = _unpack(_sum_blocks("sum_before_scan", scan_blocks), [w[n].shape for n in before_scan] + [(1,)])
    loss = scan_grads[-1].reshape(())
    early = before_scan + after_scan
    early_grads = scan_grads[:-1] + _unpack(_sum_blocks("sum_after_scan", ffn_blocks), [w[n].shape for n in after_scan])

    halves = [reducer.halves[n] for n in BIG]
    late_blocks = _place_block(_pack([g_norm_ffn1]), me)
    last = _join(_share_carry(halves, [shard_shape[n] for n in BIG]), _exchange_carry(late_blocks))
    *shared, late_blocks = _comm_call("share_halves", last)
    grad = dict(zip(BIG, shared))
    grad.update(zip(early, early_grads))
    grad["norm_ffn1"] = _unpack(_sum_blocks("sum_first_norm", late_blocks), [w["norm_ffn1"].shape])[0]

    small_shapes = [w[n].shape for n in SMALL]
    delta, new_m, new_v = {}, {}, {}
    for n in BIG:
        delta[n], new_m[n], new_v[n], grad[n] = _adamw("adamw_" + n, w[n], grad[n], m[n], v[n])
    d_p, m_p, v_p, _ = _adamw("adamw_small", _pack([w[n] for n in SMALL]), _pack([grad[n] for n in SMALL]),
                              _pack([m[n] for n in SMALL]), _pack([v[n] for n in SMALL]))
    for name_list, packed in ((delta, d_p), (new_m, m_p), (new_v, v_p)):
        for n, a in zip(SMALL, _unpack(packed, small_shapes)):
            name_list[n] = a
    return loss, dx, grad, delta, new_m, new_v


def kernel(x, p, norm_ffn1, w1_gate, w1_up, w1_down, norm_mix, w_in, ssm_log_dt, ssm_a_re, ssm_a_im, ssm_b_re, ssm_b_im, ssm_c_re, ssm_c_im, ssm_d, ssm_w_glu, gmlp_norm_v, gmlp_w_s, gmlp_b_s, norm_ssm_out, norm_gmlp_out, w_out, norm_ffn2, w2_gate, w2_up, w2_down, norm_ple, w_ple_gate, w_ple_proj, norm_final, loss_target, m_norm_ffn1, m_w1_gate, m_w1_up, m_w1_down, m_norm_mix, m_w_in, m_ssm_log_dt, m_ssm_a_re, m_ssm_a_im, m_ssm_b_re, m_ssm_b_im, m_ssm_c_re, m_ssm_c_im, m_ssm_d, m_ssm_w_glu, m_gmlp_norm_v, m_gmlp_w_s, m_gmlp_b_s, m_norm_ssm_out, m_norm_gmlp_out, m_w_out, m_norm_ffn2, m_w2_gate, m_w2_up, m_w2_down, m_norm_ple, m_w_ple_gate, m_w_ple_proj, m_norm_final, v_norm_ffn1, v_w1_gate, v_w1_up, v_w1_down, v_norm_mix, v_w_in, v_ssm_log_dt, v_ssm_a_re, v_ssm_a_im, v_ssm_b_re, v_ssm_b_im, v_ssm_c_re, v_ssm_c_im, v_ssm_d, v_ssm_w_glu, v_gmlp_norm_v, v_gmlp_w_s, v_gmlp_b_s, v_norm_ssm_out, v_norm_gmlp_out, v_w_out, v_norm_ffn2, v_w2_gate, v_w2_up, v_w2_down, v_norm_ple, v_w_ple_gate, v_w_ple_proj, v_norm_final):
    given = dict(locals())
    shapes = {n: given[n].shape for n in WEIGHTS}

    def block(name):
        a = given[name]
        if a.ndim == 1:
            return a.reshape(1, -1)
        return a[0] if a.ndim >= 3 else a

    w = {n: block(n) for n in WEIGHTS}
    m = {n: block("m_" + n) for n in WEIGHTS}
    v = {n: block("v_" + n) for n in WEIGHTS}
    loss, dx, grad, delta, new_m, new_v = _step(x[0], p[0, 0], loss_target[0], w, m, v)
    outs = [loss, dx[None]]
    for tree in (grad, delta, new_m, new_v):
        outs += [tree[n].reshape(shapes[n]) for n in WEIGHTS]
    return tuple(outs)
```

```python
import functools
import math

import jax
import jax.numpy as jnp
from jax import lax
from jax.experimental import pallas as pl
from jax.experimental.pallas import tpu as pltpu

F32 = jnp.float32
BF16 = jnp.bfloat16
EPS = 1e-6
SSM_GROUP = 16
SSM_STATE = 64
GROUPS_PER_BLOCK = 8
GMLP_HEAD = 128
CHUNK = 128
ADAM_LR = 0.001
ADAM_B1 = 0.9
ADAM_B2 = 0.999
ADAM_EPS = 1e-08
ADAM_WD = 0.01
ADAM_STEP = 10
N_CHIPS = 4
N_DEV = 8
LANES = 128
VMEM_LIMIT_BYTES = 56 * 1024 * 1024
MESH = pl.DeviceIdType.MESH
GELU_C = math.sqrt(2.0 / math.pi)
GELU_A = 0.044715

_DOT_DIMS = {
    "nn": (((1,), (0,)), ((), ())),
    "nt": (((1,), (1,)), ((), ())),
    "tn": (((0,), (0,)), ((), ())),
}


def _tile(dim, pref, align):
    if dim <= pref:
        return dim
    t = (pref // align) * align
    while t >= align:
        if dim % t == 0:
            return t
        t -= align
    return dim


def _params(semantics):
    return pltpu.CompilerParams(dimension_semantics=semantics, vmem_limit_bytes=VMEM_LIMIT_BYTES)


def _gelu(x):
    return 0.5 * x * (1.0 + jnp.tanh(GELU_C * (x + GELU_A * x * x * x)))


def _gelu_grad(x):
    t = jnp.tanh(GELU_C * (x + GELU_A * x * x * x))
    return 0.5 * (1.0 + t) + 0.5 * x * (1.0 - t * t) * GELU_C * (1.0 + 3.0 * GELU_A * x * x)


def _sigmoid(x):
    return 1.0 / (1.0 + jnp.exp(-x))


def _dot(a, b, mode):
    return lax.dot_general(a.astype(BF16), b.astype(BF16), _DOT_DIMS[mode], preferred_element_type=F32)


class _Carry:
    def __init__(self, arrays, out_shapes, aliases, n_copies, copies):
        self.arrays = list(arrays)
        self.out_shapes = list(out_shapes)
        self.aliases = dict(aliases)
        self.n_copies = n_copies
        self.copies = copies

    def scratch(self):
        return [pltpu.SemaphoreType.DMA((self.n_copies,)), pltpu.SemaphoreType.DMA((self.n_copies,))]

    def split(self, refs):
        n_in, n_out = len(self.arrays), len(self.out_shapes)
        return refs[:n_in], refs[n_in:n_in + n_out], refs[n_in + n_out], refs[n_in + n_out + 1]

    def start(self, refs):
        for cp in self.copies(*self.split(refs)):
            cp.start()

    def wait(self, refs):
        for cp in self.copies(*self.split(refs)):
            cp.wait()


class _SemRange:
    def __init__(self, sems, offset):
        self.sems, self.offset = sems, offset

    @property
    def at(self):
        return self

    def __getitem__(self, k):
        return self.sems.at[self.offset + k]


def _join(first, second):
    n_in, n_out = len(first.arrays), len(first.out_shapes)
    aliases = dict(first.aliases)
    aliases.update({n_in + i: n_out + o for i, o in second.aliases.items()})

    def copies(ops, res, send_sems, recv_sems):
        return (first.copies(ops[:n_in], res[:n_out], send_sems, recv_sems)
                + second.copies(ops[n_in:], res[n_out:], _SemRange(send_sems, first.n_copies),
                                _SemRange(recv_sems, first.n_copies)))

    return _Carry(first.arrays + second.arrays, first.out_shapes + second.out_shapes, aliases,
                  first.n_copies + second.n_copies, copies)


_ANY = pl.BlockSpec(memory_space=pl.ANY)


def _comm_call(name, carry):
    def body(*refs):
        carry.start(refs)
        carry.wait(refs)

    n_in = len(carry.arrays)
    return pl.pallas_call(
        body,
        name=name,
        in_specs=[_ANY] * n_in,
        out_specs=[_ANY] * len(carry.out_shapes),
        out_shape=carry.out_shapes,
        input_output_aliases=carry.aliases,
        scratch_shapes=carry.scratch(),
    )(*carry.arrays)


def _carried_call(body, carry, *, name, grid, in_specs, out_specs, out_shape, scratch_shapes, semantics, args):
    if carry is None:
        res = pl.pallas_call(body, name=name, grid=grid, in_specs=in_specs, out_specs=out_specs, out_shape=out_shape,
                             scratch_shapes=scratch_shapes, compiler_params=_params(semantics))(*args)
        return res, []
    n_in, n_out, n_scr = len(in_specs), len(out_specs), len(scratch_shapes)
    nci, nco = len(carry.arrays), len(carry.out_shapes)

    def wrapped(*refs):
        ins = refs[:n_in]
        outs = refs[n_in + nci:n_in + nci + n_out]
        scr = refs[n_in + nci + n_out + nco:n_in + nci + n_out + nco + n_scr]
        c_refs = (refs[n_in:n_in + nci] + refs[n_in + nci + n_out:n_in + nci + n_out + nco]
                  + refs[n_in + nci + n_out + nco + n_scr:])
        first = functools.reduce(jnp.logical_and, [pl.program_id(d) == 0 for d in range(len(grid))])
        last = functools.reduce(jnp.logical_and, [pl.program_id(d) == grid[d] - 1 for d in range(len(grid))])

        @pl.when(first)
        def _():
            carry.start(c_refs)

        body(*ins, *outs, *scr)

        @pl.when(last)
        def _():
            carry.wait(c_refs)

    res = pl.pallas_call(
        wrapped,
        name=name,
        grid=grid,
        in_specs=list(in_specs) + [_ANY] * nci,
        out_specs=list(out_specs) + [_ANY] * nco,
        out_shape=list(out_shape) + carry.out_shapes,
        input_output_aliases={n_in + i: n_out + o for i, o in carry.aliases.items()},
        scratch_shapes=list(scratch_shapes) + carry.scratch(),
        compiler_params=_params(("arbitrary",) * len(grid)),
    )(*args, *carry.arrays)
    return res[:n_out], res[n_out:]


def _matmul(name, mode, a_list, b_list, products, out_dtypes, epilogue, extras=(), tm=512, tn=512, tk=2048,
            carry=None, n_part=(0, 1)):
    a0, b0 = a_list[0], b_list[0]
    if mode == "tn":
        k_dim, m_dim = a0.shape
    else:
        m_dim, k_dim = a0.shape
    n_dim = (b0.shape[0] if mode == "nt" else b0.shape[1]) // n_part[1]
    tm = _tile(m_dim, tm, LANES)
    tn = _tile(n_dim, tn, LANES)
    tk = _tile(k_dim, tk, LANES)
    nk = k_dim // tk
    j0 = n_part[0] * (n_dim // tn)
    chunk = 2 * LANES if (nk == 1 and epilogue is not _identity and tn % (2 * LANES) == 0) else tn
    n_acc = 1 + max(p[2] for p in products)
    na, nb, ne, no = len(a_list), len(b_list), len(extras), len(out_dtypes)

    if mode == "tn":
        a_spec = pl.BlockSpec((tk, tm), lambda i, j, k: (k, i))
    else:
        a_spec = pl.BlockSpec((tm, tk), lambda i, j, k: (i, k))
    if mode == "nt":
        b_spec = pl.BlockSpec((tn, tk), lambda i, j, k: (j0 + j, k))
    else:
        b_spec = pl.BlockSpec((tk, tn), lambda i, j, k: (k, j0 + j))
    t_spec = pl.BlockSpec((tm, tn), lambda i, j, k: (i, j))

    def body(*refs):
        a_refs = refs[:na]
        b_refs = refs[na:na + nb]
        e_refs = refs[na + nb:na + nb + ne]
        o_refs = refs[na + nb + ne:na + nb + ne + no]
        acc_refs = refs[na + nb + ne + no:]

        def partial_sums(cols):
            sums = [None] * n_acc
            for ai, bi, ci in products:
                b = b_refs[bi][cols, :] if mode == "nt" else b_refs[bi][:, cols]
                d = _dot(a_refs[ai][...], b, mode)
                sums[ci] = d if sums[ci] is None else sums[ci] + d
            return sums

        def finish(accs, cols):
            outs = epilogue(accs, [e[:, cols] for e in e_refs])
            for o_ref, o in zip(o_refs, outs):
                o_ref[:, cols] = o.astype(o_ref.dtype)

        if nk == 1:
            for c0 in range(0, tn, chunk):
                finish(partial_sums(slice(c0, c0 + chunk)), slice(c0, c0 + chunk))
        else:
            sums = partial_sums(slice(None))
            finish = functools.partial(finish, cols=slice(None))
            k = pl.program_id(2)

            @pl.when(k == 0)
            def _():
                for acc, s in zip(acc_refs, sums):
                    acc[...] = s

            @pl.when(k > 0)
            def _():
                for acc, s in zip(acc_refs, sums):
                    acc[...] += s

            @pl.when(k == nk - 1)
            def _():
                finish([acc[...] for acc in acc_refs])

    scratch = [pltpu.VMEM((tm, tn), F32) for _ in range(n_acc)] if nk > 1 else []
    outs, carried = _carried_call(
        body, carry,
        name=name,
        grid=(m_dim // tm, n_dim // tn, nk),
        in_specs=[a_spec] * na + [b_spec] * nb + [t_spec] * ne,
        out_specs=[t_spec] * no,
        out_shape=[jax.ShapeDtypeStruct((m_dim, n_dim), dt) for dt in out_dtypes],
        scratch_shapes=scratch,
        semantics=("parallel", "parallel", "arbitrary"),
        args=[*a_list, *b_list, *extras],
    )
    return (outs, carried) if carry else outs


def _identity(accs, extras):
    return accs


def _single(result, carry):
    return (result[0][0], result[1]) if carry else result[0]


def _mm_nn(name, a, b, out_dtype, res=None, alpha=1.0, carry=None, **tiles):
    if res is None:
        return _single(_matmul(name, "nn", [a], [b], [(0, 0, 0)], [out_dtype], _identity, carry=carry, **tiles), carry)

    def epilogue(accs, extras):
        return [extras[0] + alpha * accs[0]]

    return _single(_matmul(name, "nn", [a], [b], [(0, 0, 0)], [out_dtype], epilogue, extras=(res,), carry=carry,
                           **tiles), carry)


def _mm_nt_sum(name, a_list, b_list, out_dtype, carry=None, n_part=(0, 1), **tiles):
    products = [(i, i, 0) for i in range(len(a_list))]
    return _single(_matmul(name, "nt", a_list, b_list, products, [out_dtype], _identity, carry=carry, n_part=n_part,
                           **tiles), carry)


def _mm_tn(name, a, b_list, out_dtype, carry=None, **tiles):
    products = [(0, i, i) for i in range(len(b_list))]
    return _matmul(name, "tn", [a], b_list, products, [out_dtype] * len(b_list), _identity, carry=carry, **tiles)


def _rowwise(name, fn, row_ins, par_ins, row_outs, acc_outs=(), tr=512):
    first = row_ins[0][0] if isinstance(row_ins[0], tuple) else row_ins[0]
    t_dim = first.shape[0]
    tr = _tile(t_dim, tr, 16)
    arrays, specs = [], []
    for r in row_ins:
        if isinstance(r, tuple):
            arr, width, blk = r
            specs.append(pl.BlockSpec((tr, width), lambda i, blk=blk: (i, blk)))
        else:
            arr = r
            specs.append(pl.BlockSpec((tr, arr.shape[1]), lambda i: (i, 0)))
        arrays.append(arr)
    for p in par_ins:
        arrays.append(p)
        specs.append(pl.BlockSpec(p.shape, lambda i, nd=p.ndim: (0,) * nd))
    nr, npar, nro, nacc = len(row_ins), len(par_ins), len(row_outs), len(acc_outs)

    def body(*refs):
        rows = [r[...] for r in refs[:nr]]
        pars = [p[...] for p in refs[nr:nr + npar]]
        o_refs = refs[nr + npar:nr + npar + nro]
        acc_refs = refs[nr + npar + nro:]
        outs, accs = fn(rows, pars)
        for o_ref, o in zip(o_refs, outs):
            o_ref[...] = o.astype(o_ref.dtype)
        if nacc:
            @pl.when(pl.program_id(0) == 0)
            def _():
                for a_ref in acc_refs:
                    a_ref[...] = jnp.zeros_like(a_ref)

            for a_ref, a in zip(acc_refs, accs):
                a_ref[...] += a

    out_shape = [jax.ShapeDtypeStruct((t_dim, c), dt) for c, dt in row_outs]
    out_shape += [jax.ShapeDtypeStruct(s, F32) for s in acc_outs]
    out_specs = [pl.BlockSpec((tr, c), lambda i: (i, 0)) for c, _ in row_outs]
    out_specs += [pl.BlockSpec(s, lambda i: (0, 0)) for s in acc_outs]
    return pl.pallas_call(
        body,
        name=name,
        grid=(t_dim // tr,),
        in_specs=specs,
        out_specs=out_specs,
        out_shape=out_shape,
        compiler_params=_params(("arbitrary",)),
    )(*arrays)


def _rms_stats(x):
    r = lax.rsqrt(jnp.mean(x * x, axis=-1, keepdims=True) + EPS)
    return x * r, r


def _rms_backward(x, g, dy):
    xh, r = _rms_stats(x)
    a = dy * g
    dx = r * (a - xh * jnp.mean(a * xh, axis=-1, keepdims=True))
    return dx, jnp.sum(dy * xh, axis=0, keepdims=True)


def _rms_fwd(name, x, g):
    def fn(rows, pars):
        xh, _ = _rms_stats(rows[0])
        return [xh * pars[0]], []

    return _rowwise(name, fn, [x], [g], [(x.shape[1], BF16)])[0]


def _rms_bwd(name, x, g, dy_parts, dres, scale):
    def fn(rows, pars):
        dy = (rows[2] if len(rows) == 3 else jnp.concatenate(rows[2:], axis=1)).astype(F32)
        dx, dg = _rms_backward(rows[0], pars[0], dy)
        tot = rows[1] + dx
        return [tot, scale * tot], [dg]

    d = x.shape[1]
    return _rowwise(name, fn, [x, dres, *dy_parts], [g], [(d, F32), (d, BF16)], [(1, d)], tr=256)


def _cast_into_gathered(name, w, kind, q_arr):
    rs, cs = w.shape
    tr = _tile(rs, 256, 16)
    nr = rs // tr

    def body(q_ref, w_ref, o_ref):
        o_ref[...] = w_ref[...].astype(BF16)

    if kind == "row":
        o_spec = pl.BlockSpec((tr, cs), lambda i, q_ref: (q_ref[0] * nr + i, 0))
    else:
        o_spec = pl.BlockSpec((tr, cs), lambda i, q_ref: (i, q_ref[0]))
    return pl.pallas_call(
        body,
        name=name,
        grid_spec=pltpu.PrefetchScalarGridSpec(num_scalar_prefetch=1, grid=(nr,),
                                               in_specs=[pl.BlockSpec((tr, cs), lambda i, q_ref: (i, 0))],
                                               out_specs=o_spec),
        out_shape=jax.ShapeDtypeStruct(_full_shape(kind, (rs, cs)), BF16),
        compiler_params=_params(("parallel",)),
    )(q_arr, w)


def _swiglu_tiles(gate, up):
    s = _sigmoid(gate)
    silu = gate * s
    return [silu * up, up * (s * (1.0 + gate * (1.0 - s))), silu]


def _ffn_gateup(tag, n, wg, wu, carry):
    def act(accs, extras):
        return _swiglu_tiles(accs[0], accs[1])

    return _matmul(tag + "_gateup", "nn", [n], [wg, wu], [(0, 0, 0), (0, 1, 1)], [BF16] * 3, act,
                   tm=1024, tn=512, tk=2048, carry=carry)


def _ffn_up(tag, n, wu, gate, carry):
    def act(accs, extras):
        return _swiglu_tiles(extras[0].astype(F32), accs[0])

    return _matmul(tag + "_up", "nn", [n], [wu], [(0, 0, 0)], [BF16] * 3, act, extras=(gate,),
                   tm=1024, tn=512, tk=2048, carry=carry)


def _ffn_down(tag, a, wd, h, carry=None):
    return _mm_nn(tag + "_down", a, wd, F32, res=h, alpha=0.5, tm=1024, tn=512, tk=5632, carry=carry)


def _ffn_bwd(tag, names, h, g, wg, wu, wd, saved, dh, dfb, next_scale, reducer, riding=None, riding_dwd=None,
             last_hop_later=False):
    n, a, da_dgate, da_dup = saved

    def act_bwd(accs, extras):
        return [accs[0] * extras[0].astype(F32), accs[0] * extras[1].astype(F32)]

    dact = _matmul(tag + "_dact", "nt", [dfb], [wd], [(0, 0, 0)], [BF16, BF16], act_bwd, extras=(da_dgate, da_dup),
                   tm=1024, tn=512, tk=2048, carry=riding)
    (dgp, du), rode = dact if riding else (dact, [])
    dwd_call = _mm_tn(tag + "_dwd", a, [dfb], BF16, tm=512, tn=2048, tk=4096, carry=riding_dwd)
    (dwd,), rode_dwd = dwd_call if riding_dwd else (dwd_call, [])
    down = reducer.swap(names[2:], [dwd])
    (dwg,), swapped = _mm_tn(tag + "_dwg", n, [dgp], BF16, tm=512, tn=1408, tk=4096, carry=down.carry)
    down = reducer.send(down, swapped)
    gate_w = reducer.swap(names[:1], [dwg])
    carry, split = _ride(down.carry, gate_w.carry)
    (dwu,), results = _mm_tn(tag + "_dwu", n, [du], BF16, tm=512, tn=1408, tk=4096, carry=carry)
    got, swapped = split(results)
    reducer.end(down, got)
    gate_w = reducer.send(gate_w, swapped)
    up_w = reducer.swap(names[1:2], [dwu])
    carry, split = _ride(gate_w.carry, up_w.carry)
    halves = 1 if last_hop_later else 2
    dn_lo, results = _mm_nt_sum(tag + "_dn_lo", [dgp, du], [wg, wu], BF16, tm=1024, tn=1024, tk=1408, carry=carry,
                                n_part=(0, halves))
    got, swapped = split(results)
    reducer.end(gate_w, got)
    up_w = reducer.send(up_w, swapped)
    dn = [dn_lo]
    if not last_hop_later:
        dn_hi, got = _mm_nt_sum(tag + "_dn_hi", [dgp, du], [wg, wu], BF16, tm=1024, tn=1024, tk=1408,
                                carry=up_w.carry, n_part=(1, 2))
        reducer.end(up_w, got)
        dn.append(dn_hi)
    dh_in, dh_in_b, dg = _rms_bwd(tag + "_dnorm", h, g, dn, dh, next_scale)
    return dh_in, dh_in_b, dg, rode, rode_dwd, (up_w if last_hop_later else None)


def _ssm_discretize(log_dt, a_re, a_im, b_re, b_im):
    dt = jnp.exp(log_dt)[:, None]
    lr = jnp.minimum(a_re, -1e-4)
    li = a_im
    mag = jnp.exp(lr * dt)
    ang = li * dt
    abar_r = mag * jnp.cos(ang)
    abar_i = mag * jnp.sin(ang)
    den = lr * lr + li * li
    xr = abar_r - 1.0
    xi = abar_i
    zr = (xr * lr + xi * li) / den
    zi = (xi * lr - xr * li) / den
    bbar_r = zr[..., None] * b_re - zi[..., None] * b_im
    bbar_i = zr[..., None] * b_im + zi[..., None] * b_re
    return abar_r, abar_i, bbar_r, bbar_i


def _blockdiag_in(b):
    g, n, p = b.shape
    nb = g // GROUPS_PER_BLOCK
    eye = jnp.eye(GROUPS_PER_BLOCK, dtype=b.dtype)
    b4 = b.reshape(nb, GROUPS_PER_BLOCK, n, p)
    return jnp.einsum("sgnp,gh->sgphn", b4, eye).reshape(nb, GROUPS_PER_BLOCK * p, GROUPS_PER_BLOCK * n)


def _blockdiag_in_grad(gw, n, p):
    nb = gw.shape[0]
    eye = jnp.eye(GROUPS_PER_BLOCK, dtype=gw.dtype)
    g5 = gw.reshape(nb, GROUPS_PER_BLOCK, p, GROUPS_PER_BLOCK, n)
    return jnp.einsum("sgphn,gh->sgnp", g5, eye).reshape(nb * GROUPS_PER_BLOCK, n, p)


def _blockdiag_out(c):
    g, p, n = c.shape
    nb = g // GROUPS_PER_BLOCK
    eye = jnp.eye(GROUPS_PER_BLOCK, dtype=c.dtype)
    c4 = c.reshape(nb, GROUPS_PER_BLOCK, p, n)
    return jnp.einsum("sgpn,gh->shngp", c4, eye).reshape(nb, GROUPS_PER_BLOCK * n, GROUPS_PER_BLOCK * p)


def _blockdiag_out_grad(gw, p, n):
    nb = gw.shape[0]
    eye = jnp.eye(GROUPS_PER_BLOCK, dtype=gw.dtype)
    g5 = gw.reshape(nb, GROUPS_PER_BLOCK, n, GROUPS_PER_BLOCK, p)
    return jnp.einsum("shngp,gh->sgpn", g5, eye).reshape(nb * GROUPS_PER_BLOCK, p, n)


def _ssm_scan_fwd(z, wb, wc, abar, d, carry=None):
    t_dim = z.shape[0]
    nb, cb, sw2 = wb.shape
    nl = sw2 // LANES
    hl = nl // 2
    tt = _tile(t_dim, 256, 8)
    nt = t_dim // tt

    def body(z_ref, wb_ref, wc_ref, a_ref, d_ref, s_ref, y_ref, yg_ref, drive_ref, st_ref):
        @pl.when(pl.program_id(0) == 0)
        def _():
            st_ref[...] = jnp.zeros_like(st_ref)

        u = z_ref[...]
        ub = u.astype(BF16)
        for b in range(nb):
            drive = _dot(ub[:, b * cb:(b + 1) * cb], wb_ref[b], "nn")
            for l in range(nl):
                drive_ref[l, pl.ds(b, tt, stride=nb), :] = drive[:, l * LANES:(l + 1) * LANES]
        a = a_ref[...]
        chunk = lambda v, l: v[:, l * LANES:(l + 1) * LANES]

        def step(t, state):
            rows = pl.ds(pl.multiple_of(t * nb, nb), nb)
            re, im = [], []
            for l in range(hl):
                ar, ai, sr, si = chunk(a, l), chunk(a, hl + l), state[l], state[hl + l]
                nr = ar * sr - ai * si + drive_ref[l, rows, :]
                ni = ar * si + ai * sr + drive_ref[hl + l, rows, :]
                s_ref[l, rows, :] = nr
                s_ref[hl + l, rows, :] = ni
                re.append(nr)
                im.append(ni)
            return tuple(re + im)

        state = lax.fori_loop(0, tt, step, tuple(st_ref[l] for l in range(nl)), unroll=8)
        for l in range(nl):
            st_ref[l] = state[l]
        parts = []
        for b in range(nb):
            s_b = jnp.concatenate([s_ref[l, pl.ds(b, tt, stride=nb), :] for l in range(nl)], axis=1)
            parts.append(_dot(s_b, wc_ref[b], "nn"))
        y = jnp.concatenate(parts, axis=1) + d_ref[...] * u
        y_ref[...] = y
        yg_ref[...] = _gelu(y).astype(BF16)

    full = lambda a: pl.BlockSpec(a.shape, lambda t, nd=a.ndim: (0,) * nd)
    return _carried_call(
        body, carry,
        name="ssm_scan_fwd",
        grid=(nt,),
        in_specs=[pl.BlockSpec((tt, nb * cb), lambda t: (t, 0)), full(wb), full(wc), full(abar), full(d)],
        out_specs=[
            pl.BlockSpec((nl, tt * nb, LANES), lambda t: (0, t, 0)),
            pl.BlockSpec((tt, nb * cb), lambda t: (t, 0)),
            pl.BlockSpec((tt, nb * cb), lambda t: (t, 0)),
        ],
        out_shape=[
            jax.ShapeDtypeStruct((nl, t_dim * nb, LANES), F32),
            jax.ShapeDtypeStruct((t_dim, nb * cb), F32),
            jax.ShapeDtypeStruct((t_dim, nb * cb), BF16),
        ],
        scratch_shapes=[pltpu.VMEM((nl, tt * nb, LANES), F32), pltpu.VMEM((nl, nb, LANES), F32)],
        semantics=("arbitrary",),
        args=[z, wb, wc, abar, d],
    )


def _ssm_scan_bwd(z, dy, states, wb, wc, abar_conj, d, carry=None):
    t_dim = z.shape[0]
    nb, cb, sw2 = wb.shape
    nl = sw2 // LANES
    hl = nl // 2
    tt = _tile(t_dim, 128, 8)
    nt = t_dim // tt
    edges = states.reshape(nl, nt, tt * nb, LANES)[:, :, (tt - 1) * nb:, :]
    before = jnp.concatenate([jnp.zeros((nl, 1, nb, LANES), F32), edges[:, :-1]], axis=1).reshape(nl, nt * nb, LANES)

    def body(z_ref, dy_ref, s_ref, sp_ref, wb_ref, wc_ref, a_ref, d_ref,
             dz_ref, gwb_ref, gwc_ref, ga_ref, gd_ref, gin_ref, gs_ref, st_ref):
        @pl.when(pl.program_id(0) == 0)
        def _():
            st_ref[...] = jnp.zeros_like(st_ref)
            gwb_ref[...] = jnp.zeros_like(gwb_ref)
            gwc_ref[...] = jnp.zeros_like(gwc_ref)
            ga_ref[...] = jnp.zeros_like(ga_ref)
            gd_ref[...] = jnp.zeros_like(gd_ref)

        u = z_ref[...]
        dyv = dy_ref[...]
        ub = u.astype(BF16)
        dyb = dyv.astype(BF16)
        for b in range(nb):
            gin = _dot(dyb[:, b * cb:(b + 1) * cb], wc_ref[b], "nt")
            for l in range(nl):
                gin_ref[l, pl.ds(b, tt, stride=nb), :] = gin[:, l * LANES:(l + 1) * LANES]
        a = a_ref[...]
        chunk = lambda v, l: v[:, l * LANES:(l + 1) * LANES]

        def step(k, state):
            rows = pl.ds(pl.multiple_of((tt - 1 - k) * nb, nb), nb)
            re, im = [], []
            for l in range(hl):
                ar, ai, gr, gi = chunk(a, l), chunk(a, hl + l), state[l], state[hl + l]
                nr = ar * gr - ai * gi + gin_ref[l, rows, :]
                ni = ar * gi + ai * gr + gin_ref[hl + l, rows, :]
                gs_ref[l, rows, :] = nr
                gs_ref[hl + l, rows, :] = ni
                re.append(nr)
                im.append(ni)
            return tuple(re + im)

        state = lax.fori_loop(0, tt, step, tuple(st_ref[l] for l in range(nl)), unroll=8)
        for l in range(nl):
            st_ref[l] = state[l]

        parts = []
        for b in range(nb):
            cols = slice(b * cb, (b + 1) * cb)
            gs_b = jnp.concatenate([gs_ref[l, pl.ds(b, tt, stride=nb), :] for l in range(nl)], axis=1)
            s_b = jnp.concatenate([s_ref[l, pl.ds(b, tt, stride=nb), :] for l in range(nl)], axis=1)
            parts.append(_dot(gs_b, wb_ref[b], "nt"))
            gwb_ref[b] += _dot(ub[:, cols], gs_b, "tn")
            gwc_ref[b] += _dot(s_b, dyb[:, cols], "tn")
        dz_ref[...] = (jnp.concatenate(parts, axis=1) + d_ref[...] * dyv).astype(BF16)
        gd_ref[...] += jnp.sum(dyv * u, axis=0, keepdims=True)

        row = lax.broadcasted_iota(jnp.int32, (tt * nb, LANES), 0)
        shifted = lambda v: jnp.where(row < nb, 0.0, pltpu.roll(v, nb, 0))
        over_time = lambda v: jnp.sum(v.reshape(tt, nb, LANES), axis=0)
        for l in range(hl):
            g_r, g_i = gs_ref[l], gs_ref[hl + l]
            p_r, p_i = shifted(s_ref[l]), shifted(s_ref[hl + l])
            f_r, f_i = sp_ref[l], sp_ref[hl + l]
            g0_r, g0_i = gs_ref[l, pl.ds(0, nb), :], gs_ref[hl + l, pl.ds(0, nb), :]
            ga_ref[l] += over_time(g_r * p_r + g_i * p_i) + g0_r * f_r + g0_i * f_i
            ga_ref[hl + l] += over_time(g_i * p_r - g_r * p_i) + g0_i * f_r - g0_r * f_i

    rev = lambda t: (nt - 1 - t, 0)
    rev3 = lambda t: (0, nt - 1 - t, 0)
    full = lambda a: pl.BlockSpec(a.shape, lambda t, nd=a.ndim: (0,) * nd)
    return _carried_call(
        body, carry,
        name="ssm_scan_bwd",
        grid=(nt,),
        in_specs=[
            pl.BlockSpec((tt, nb * cb), rev),
            pl.BlockSpec((tt, nb * cb), rev),
            pl.BlockSpec((nl, tt * nb, LANES), rev3),
            pl.BlockSpec((nl, nb, LANES), rev3),
            full(wb), full(wc), full(abar_conj), full(d),
        ],
        out_specs=[
            pl.BlockSpec((tt, nb * cb), rev),
            pl.BlockSpec((nb, cb, sw2), lambda t: (0, 0, 0)),
            pl.BlockSpec((nb, sw2, cb), lambda t: (0, 0, 0)),
            pl.BlockSpec((nl, nb, LANES), lambda t: (0, 0, 0)),
            pl.BlockSpec((1, nb * cb), lambda t: (0, 0)),
        ],
        out_shape=[
            jax.ShapeDtypeStruct((t_dim, nb * cb), BF16),
            jax.ShapeDtypeStruct((nb, cb, sw2), F32),
            jax.ShapeDtypeStruct((nb, sw2, cb), F32),
            jax.ShapeDtypeStruct((nl, nb, LANES), F32),
            jax.ShapeDtypeStruct((1, nb * cb), F32),
        ],
        scratch_shapes=[pltpu.VMEM((nl, tt * nb, LANES), F32), pltpu.VMEM((nl, tt * nb, LANES), F32),
                        pltpu.VMEM((nl, nb, LANES), F32)],
        semantics=("arbitrary",),
        args=[z, dy, states, before, wb, wc, abar_conj, d],
    )


def _gmlp_chunk(zu, zv, gv, wm_ref, bias, n_heads):
    ua = _gelu(zu)
    vg = _gelu(zv)
    xc = vg - jnp.mean(vg, axis=-1, keepdims=True)
    r = lax.rsqrt(jnp.mean(xc * xc, axis=-1, keepdims=True) + EPS)
    vh = xc * r
    vb = (vh * gv).astype(BF16)
    parts = []
    for h in range(n_heads):
        cols = slice(h * GMLP_HEAD, (h + 1) * GMLP_HEAD)
        parts.append(_dot(wm_ref[h], vb[:, cols], "nn"))
    s = jnp.concatenate(parts, axis=1) + bias
    return ua, vh, r, vb, s


def _gmlp_fwd(z, gv, wm, bias, ggo):
    t_dim = z.shape[0]
    dg = gv.shape[1]
    n_heads = dg // GMLP_HEAD
    tr = _tile(t_dim, 256, CHUNK)

    def body(zu_ref, zv_ref, gv_ref, wm_ref, b_ref, ggo_ref, o_ref):
        for ck in range(tr // CHUNK):
            rows = pl.ds(ck * CHUNK, CHUNK)
            ua, _, _, _, s = _gmlp_chunk(zu_ref[rows, :], zv_ref[rows, :], gv_ref[...], wm_ref, b_ref[...], n_heads)
            yh, _ = _rms_stats(ua * s)
            o_ref[rows, :] = (yh * ggo_ref[...]).astype(BF16)

    full = lambda a: pl.BlockSpec(a.shape, lambda i, nd=a.ndim: (0,) * nd)
    return pl.pallas_call(
        body,
        name="gmlp_fwd",
        grid=(t_dim // tr,),
        in_specs=[pl.BlockSpec((tr, dg), lambda i: (i, 1)), pl.BlockSpec((tr, dg), lambda i: (i, 2)),
                  full(gv), full(wm), full(bias), full(ggo)],
        out_specs=pl.BlockSpec((tr, dg), lambda i: (i, 0)),
        out_shape=jax.ShapeDtypeStruct((t_dim, dg), BF16),
        compiler_params=_params(("parallel",)),
    )(z, z, gv, wm, bias, ggo)


def _gmlp_bwd(z, dycat, gv, wm, bias, ggo):
    t_dim = z.shape[0]
    dg = gv.shape[1]
    n_heads = dg // GMLP_HEAD
    tr = _tile(t_dim, 256, CHUNK)

    def body(zu_ref, zv_ref, dy_ref, gv_ref, wm_ref, b_ref, ggo_ref,
             dzu_ref, dzv_ref, dggo_ref, dgv_ref, dwm_ref, dsum_ref):
        @pl.when(pl.program_id(0) == 0)
        def _():
            dggo_ref[...] = jnp.zeros_like(dggo_ref)
            dgv_ref[...] = jnp.zeros_like(dgv_ref)
            dwm_ref[...] = jnp.zeros_like(dwm_ref)
            dsum_ref[...] = jnp.zeros_like(dsum_ref)

        for ck in range(tr // CHUNK):
            rows = pl.ds(ck * CHUNK, CHUNK)
            zu = zu_ref[rows, :]
            zv = zv_ref[rows, :]
            gvv = gv_ref[...]
            ua, vh, r, vb, s = _gmlp_chunk(zu, zv, gvv, wm_ref, b_ref[...], n_heads)
            dy, dggo = _rms_backward(ua * s, ggo_ref[...], dy_ref[rows, :])
            dggo_ref[...] += dggo
            ds = dy * ua
            dsum_ref[...] += ds
            dsb = ds.astype(BF16)
            parts = []
            for h in range(n_heads):
                cols = slice(h * GMLP_HEAD, (h + 1) * GMLP_HEAD)
                dwm_ref[h] += _dot(dsb[:, cols], vb[:, cols], "nt")
                parts.append(_dot(wm_ref[h], dsb[:, cols], "tn"))
            dv = jnp.concatenate(parts, axis=1)
            dgv_ref[...] += jnp.sum(dv * vh, axis=0, keepdims=True)
            dvh = dv * gvv
            dvg = r * (dvh - jnp.mean(dvh, axis=-1, keepdims=True) - vh * jnp.mean(dvh * vh, axis=-1, keepdims=True))
            dzv_ref[rows, :] = (dvg * _gelu_grad(zv)).astype(BF16)
            dzu_ref[rows, :] = (dy * s * _gelu_grad(zu)).astype(BF16)

    full = lambda a: pl.BlockSpec(a.shape, lambda i, nd=a.ndim: (0,) * nd)
    return pl.pallas_call(
        body,
        name="gmlp_bwd",
        grid=(t_dim // tr,),
        in_specs=[pl.BlockSpec((tr, dg), lambda i: (i, 1)), pl.BlockSpec((tr, dg), lambda i: (i, 2)),
                  pl.BlockSpec((tr, dg), lambda i: (i, 1)), full(gv), full(wm), full(bias), full(ggo)],
        out_specs=[pl.BlockSpec((tr, dg), lambda i: (i, 0)), pl.BlockSpec((tr, dg), lambda i: (i, 0)),
                   pl.BlockSpec((1, dg), lambda i: (0, 0)), pl.BlockSpec((1, dg), lambda i: (0, 0)),
                   pl.BlockSpec(wm.shape, lambda i: (0, 0, 0)), pl.BlockSpec((CHUNK, dg), lambda i: (0, 0))],
        out_shape=[jax.ShapeDtypeStruct((t_dim, dg), BF16), jax.ShapeDtypeStruct((t_dim, dg), BF16),
                   jax.ShapeDtypeStruct((1, dg), F32), jax.ShapeDtypeStruct((1, dg), F32),
                   jax.ShapeDtypeStruct(wm.shape, F32), jax.ShapeDtypeStruct((CHUNK, dg), F32)],
        compiler_params=_params(("arbitrary",)),
    )(z, z, dycat, gv, wm, bias, ggo)


def _ple_head(npl, w_gate, h3, pp, tgt, g_final):
    t_dim, d = h3.shape
    tr = _tile(t_dim, 256, 16)

    def body(n_ref, w_ref, h_ref, pp_ref, t_ref, g_ref, dgq_ref, dpp_ref, dh_ref, dg_ref, loss_ref):
        @pl.when(pl.program_id(0) == 0)
        def _():
            dg_ref[...] = jnp.zeros_like(dg_ref)
            loss_ref[...] = jnp.zeros_like(loss_ref)

        gate = _sigmoid(_dot(n_ref[...], w_ref[...], "nn"))
        ppv = pp_ref[...]
        h4 = h_ref[...] + gate * ppv
        xh, _ = _rms_stats(h4)
        err = xh * g_ref[...] - t_ref[...]
        dh4, dg = _rms_backward(h4, g_ref[...], err * (1.0 / d))
        dh_ref[...] = dh4
        dgq_ref[...] = (dh4 * ppv * gate * (1.0 - gate)).astype(BF16)
        dpp_ref[...] = (dh4 * gate).astype(BF16)
        dg_ref[...] += dg
        loss_ref[...] += jnp.full((1, LANES), 0.5 * jnp.sum(err * err) * (1.0 / d), F32)

    rows = pl.BlockSpec((tr, d), lambda i: (i, 0))
    whole = lambda a: pl.BlockSpec(a.shape, lambda i: (0, 0))
    return pl.pallas_call(
        body,
        name="ple_head",
        grid=(t_dim // tr,),
        in_specs=[rows, whole(w_gate), rows, rows, rows, whole(g_final)],
        out_specs=[rows, rows, rows, pl.BlockSpec((1, d), lambda i: (0, 0)), pl.BlockSpec((1, LANES), lambda i: (0, 0))],
        out_shape=[jax.ShapeDtypeStruct((t_dim, d), BF16), jax.ShapeDtypeStruct((t_dim, d), BF16),
                   jax.ShapeDtypeStruct((t_dim, d), F32), jax.ShapeDtypeStruct((1, d), F32),
                   jax.ShapeDtypeStruct((1, LANES), F32)],
        compiler_params=_params(("arbitrary",)),
    )(npl, w_gate, h3, pp, tgt, g_final)


def _position():
    x, y, c = lax.axis_index("x"), lax.axis_index("y"), lax.axis_index("c")
    chips = [(1 - x, y), (x, 1 - y), (1 - x, 1 - y)]
    return x, y, c, chips


def _region(ref, kind, shard_shape, q, half, part=None):
    rs, cs = shard_shape
    r0, nr = (0, rs) if half is None else (half * (rs // 2), rs // 2)
    if part is not None:
        r0, nr = r0 + part * (rs // 4), rs // 4
    if kind == "row":
        return ref.at[pl.ds(q * rs + r0, nr), :]
    return ref.at[pl.ds(r0, nr), pl.ds(q * cs, cs)]


def _full_shape(kind, shard_shape):
    rs, cs = shard_shape
    return (N_CHIPS * rs, cs) if kind == "row" else (rs, N_CHIPS * cs)


def _remote(src, dst, send_sems, recv_sems, k, to):
    return pltpu.make_async_remote_copy(src_ref=src, dst_ref=dst, send_sem=send_sems.at[k], recv_sem=recv_sems.at[k],
                                        device_id=to, device_id_type=MESH)


def _same(arrays):
    return [jax.ShapeDtypeStruct(a.shape, a.dtype) for a in arrays]


def _gather_near_carry(gathered, kinds, shapes):
    nw = len(gathered)

    def copies(ops, full, send_sems, recv_sems):
        x, y, c, _ = _position()
        out = []
        for w in range(nw):
            mine = _region(full[w], kinds[w], shapes[w], 2 * x + y, c)
            out.append(_remote(mine, mine, send_sems, recv_sems, 2 * w, (1 - x, y, c)))
            out.append(_remote(mine, mine, send_sems, recv_sems, 2 * w + 1, (x, 1 - y, c)))
        return out

    return _Carry(gathered, _same(gathered), {i: i for i in range(nw)}, 2 * nw, copies)


def _gather_far_carry(gathered, kinds, shapes):
    nw = len(gathered)

    def copies(ops, full, send_sems, recv_sems):
        x, y, c, _ = _position()
        out = []
        for w in range(nw):
            from_x = _region(full[w], kinds[w], shapes[w], 2 * (1 - x) + y, c, part=1)
            from_y = _region(full[w], kinds[w], shapes[w], 2 * x + (1 - y), c, part=0)
            out.append(_remote(from_x, from_x, send_sems, recv_sems, 2 * w, (x, 1 - y, c)))
            out.append(_remote(from_y, from_y, send_sems, recv_sems, 2 * w + 1, (1 - x, y, c)))
        return out

    return _Carry(gathered, _same(gathered), {i: i for i in range(nw)}, 2 * nw, copies)


def _gather_d2d_carry(gathered, kinds, shapes):
    nw = len(gathered)

    def copies(ops, full, send_sems, recv_sems):
        x, y, c, chips = _position()
        out = []
        for w in range(nw):
            for j, (cx, cy) in enumerate(chips):
                landed = _region(full[w], kinds[w], shapes[w], 2 * cx + cy, c)
                out.append(_remote(landed, landed, send_sems, recv_sems, 3 * w + j, (x, y, 1 - c)))
        return out

    return _Carry(gathered, _same(gathered), {i: i for i in range(nw)}, 3 * nw, copies)


def _pairs_carry(grads, kinds, shapes):
    nw = len(grads)

    def copies(g, got, send_sems, recv_sems):
        x, y, c, _ = _position()
        out = []
        for w in range(nw):
            for q in range(N_CHIPS):
                out.append(_remote(_region(g[w], kinds[w], shapes[w], q, 1 - c), got[w].at[q], send_sems, recv_sems,
                                   N_CHIPS * w + q, (x, y, 1 - c)))
        return out

    outs = [jax.ShapeDtypeStruct((N_CHIPS, s[0] // 2, s[1]), BF16) for s in shapes]
    return _Carry(grads, outs, {}, N_CHIPS * nw, copies)


def _pair_sum(name, grad, got, kind, shard_shape, c_arr):
    rs, cs = shard_shape
    hr = rs // 2
    tr = _tile(hr, 512, 16)
    nr = hr // tr

    def body(c_ref, g_ref, s_ref, o_ref):
        o_ref[...] = (g_ref[...].astype(F32) + s_ref[...].astype(F32)).astype(BF16)

    if kind == "row":
        g_spec = pl.BlockSpec((tr, cs), lambda q, i, c_ref: (q * (rs // tr) + c_ref[0] * nr + i, 0))
    else:
        g_spec = pl.BlockSpec((tr, cs), lambda q, i, c_ref: (c_ref[0] * nr + i, q))
    blk = pl.BlockSpec((None, tr, cs), lambda q, i, c_ref: (q, i, 0))
    return pl.pallas_call(
        body,
        name=name,
        grid_spec=pltpu.PrefetchScalarGridSpec(num_scalar_prefetch=1, grid=(N_CHIPS, nr), in_specs=[g_spec, blk],
                                               out_specs=blk),
        out_shape=jax.ShapeDtypeStruct((N_CHIPS, hr, cs), BF16),
        compiler_params=_params(("parallel", "parallel")),
    )(c_arr, grad, got)


def _scatter_carry(sums, shapes):
    nw = len(sums)

    def copies(ps, got, send_sems, recv_sems):
        x, y, c, chips = _position()
        out = []
        for w in range(nw):
            for j, (cx, cy) in enumerate(chips):
                out.append(_remote(ps[w].at[2 * cx + cy], got[w].at[j], send_sems, recv_sems, 3 * w + j, (cx, cy, c)))
        return out

    outs = [jax.ShapeDtypeStruct((3, s[0] // 2, s[1]), BF16) for s in shapes]
    return _Carry(sums, outs, {}, 3 * nw, copies)


def _owner_sum(name, sums, got, shard_shape, qc_arr):
    rs, cs = shard_shape
    hr = rs // 2
    tr = _tile(hr, 512, 16)
    nr = hr // tr

    def body(qc_ref, mine_ref, got_ref, o_ref):
        acc = mine_ref[...].astype(F32)
        for j in range(3):
            acc = acc + got_ref[j].astype(F32)
        o_ref[...] = acc

    return pl.pallas_call(
        body,
        name=name,
        grid_spec=pltpu.PrefetchScalarGridSpec(
            num_scalar_prefetch=1, grid=(nr,),
            in_specs=[pl.BlockSpec((None, tr, cs), lambda i, qc_ref: (qc_ref[0], i, 0)),
                      pl.BlockSpec((3, tr, cs), lambda i, qc_ref: (0, i, 0))],
            out_specs=pl.BlockSpec((tr, cs), lambda i, qc_ref: (qc_ref[1] * nr + i, 0))),
        out_shape=jax.ShapeDtypeStruct((rs, cs), F32),
        compiler_params=_params(("parallel",)),
    )(qc_arr, sums, got)


def _share_carry(grads, shapes):
    nw = len(grads)

    def copies(ops, out, send_sems, recv_sems):
        x, y, c, _ = _position()
        res = []
        for w in range(nw):
            hr = shapes[w][0] // 2
            mine = out[w].at[pl.ds(c * hr, hr), :]
            res.append(_remote(mine, mine, send_sems, recv_sems, w, (x, y, 1 - c)))
        return res

    return _Carry(grads, _same(grads), {i: i for i in range(nw)}, nw, copies)


def _place_block(packed, me):
    return lax.dynamic_update_slice(jnp.zeros((N_DEV,) + packed.shape, F32), packed[None], (me, 0, 0))


def _exchange_carry(blocks):
    def copies(ops, res, send_sems, recv_sems):
        x, y, c, _ = _position()
        mine = res[0].at[4 * x + 2 * y + c]
        out = []
        for k in range(1, N_DEV):
            to = ((1 - x) if k & 4 else x, (1 - y) if k & 2 else y, (1 - c) if k & 1 else c)
            out.append(_remote(mine, mine, send_sems, recv_sems, k - 1, to))
        return out

    return _Carry([blocks], _same([blocks]), {0: 0}, N_DEV - 1, copies)


def _sum_blocks(name, blocks):
    n, rows, lanes = blocks.shape
    tr = _tile(rows, 4096, 8)

    def body(b_ref, o_ref):
        acc = b_ref[0]
        for k in range(1, n):
            acc = acc + b_ref[k]
        o_ref[...] = acc

    return pl.pallas_call(
        body,
        name=name,
        grid=(rows // tr,),
        in_specs=[pl.BlockSpec((n, tr, lanes), lambda i: (0, i, 0))],
        out_specs=pl.BlockSpec((tr, lanes), lambda i: (i, 0)),
        out_shape=jax.ShapeDtypeStruct((rows, lanes), F32),
        compiler_params=_params(("parallel",)),
    )(blocks)


def _adamw(name, w, g, m, v):
    def fn(rows, pars):
        wv, gv, mv, vv = rows
        m_new = ADAM_B1 * mv + (1.0 - ADAM_B1) * gv
        v_new = ADAM_B2 * vv + (1.0 - ADAM_B2) * (gv * gv)
        m_hat = m_new / (1.0 - ADAM_B1 ** ADAM_STEP)
        v_hat = v_new / (1.0 - ADAM_B2 ** ADAM_STEP)
        delta = -ADAM_LR * (m_hat / (jnp.sqrt(v_hat) + ADAM_EPS) + ADAM_WD * wv)
        return [delta, m_new, v_new, gv], []

    c = w.shape[1]
    return _rowwise(name, fn, [w, g, m, v], [], [(c, F32)] * 4, tr=256)


def _pack(arrays):
    rows = []
    for a in arrays:
        flat = a.reshape(-1).astype(F32)
        pad = (-flat.shape[0]) % LANES
        rows.append(jnp.pad(flat, (0, pad)).reshape(-1, LANES))
    stacked = jnp.concatenate(rows, axis=0)
    pad_rows = (-stacked.shape[0]) % 8
    return jnp.pad(stacked, ((0, pad_rows), (0, 0)))


def _unpack(packed, shapes):
    out, r = [], 0
    for s in shapes:
        n = math.prod(s)
        nr = -(-n // LANES)
        out.append(packed[r:r + nr].reshape(-1)[:n].reshape(s))
        r += nr
    return out


BIG = ["w1_gate", "w1_up", "w1_down", "w_in", "ssm_w_glu", "w_out", "w2_gate", "w2_up", "w2_down", "w_ple_gate",
       "w_ple_proj"]
KIND = {"w1_gate": "col", "w1_up": "col", "w1_down": "row", "w_in": "col", "ssm_w_glu": "row", "w_out": "row",
        "w2_gate": "col", "w2_up": "col", "w2_down": "row", "w_ple_gate": "row", "w_ple_proj": "col"}
SMALL = ["norm_ffn1", "norm_mix", "ssm_log_dt", "ssm_a_re", "ssm_a_im", "ssm_b_re", "ssm_b_im", "ssm_c_re", "ssm_c_im",
         "ssm_d", "gmlp_norm_v", "gmlp_w_s", "gmlp_b_s", "norm_ssm_out", "norm_gmlp_out", "norm_ffn2", "norm_ple",
         "norm_final"]
WEIGHTS = ["norm_ffn1", "w1_gate", "w1_up", "w1_down", "norm_mix", "w_in", "ssm_log_dt", "ssm_a_re", "ssm_a_im",
           "ssm_b_re", "ssm_b_im", "ssm_c_re", "ssm_c_im", "ssm_d", "ssm_w_glu", "gmlp_norm_v", "gmlp_w_s", "gmlp_b_s",
           "norm_ssm_out", "norm_gmlp_out", "w_out", "norm_ffn2", "w2_gate", "w2_up", "w2_down", "norm_ple",
           "w_ple_gate", "w_ple_proj", "norm_final"]


class _Trip:
    def __init__(self, names, arrays, carry):
        self.names, self.arrays, self.carry = names, arrays, carry


class _Reducer:
    def __init__(self, shard_shape, c_arr, qc_arr):
        self.shard_shape, self.c_arr, self.qc_arr = shard_shape, c_arr, qc_arr
        self.halves = {}

    def swap(self, names, grads):
        kinds = [KIND[n] for n in names]
        shapes = [self.shard_shape[n] for n in names]
        return _Trip(names, grads, _pairs_carry(grads, kinds, shapes))

    def send(self, trip, swapped):
        shapes = [self.shard_shape[n] for n in trip.names]
        sums = [_pair_sum("pair_sum_" + n, g, s, KIND[n], sh, self.c_arr)
                for n, g, s, sh in zip(trip.names, trip.arrays, swapped, shapes)]
        return _Trip(trip.names, sums, _scatter_carry(sums, shapes))

    def end(self, trip, got):
        for n, ps, g in zip(trip.names, trip.arrays, got):
            self.halves[n] = _owner_sum("owner_sum_" + n, ps, g, self.shard_shape[n], self.qc_arr)


def _ride(*carries):
    present = [c for c in carries if c is not None]
    joined = functools.reduce(_join, present) if present else None

    def split(results):
        out, at = [], 0
        for c in carries:
            n = len(c.out_shapes) if c is not None else 0
            out.append(list(results[at:at + n]))
            at += n
        return out

    return joined, split


def _step(x, p, tgt, w, m, v):
    d_model = x.shape[1]
    d_ssm = w["ssm_d"].shape[1]
    n_groups = d_ssm // SSM_GROUP
    row = lambda a: a.reshape(1, -1)

    xi, yi, ci = lax.axis_index("x"), lax.axis_index("y"), lax.axis_index("c")
    c_arr = jnp.reshape(ci, (1,)).astype(jnp.int32)
    q_arr = jnp.reshape(2 * xi + yi, (1,)).astype(jnp.int32)
    qc_arr = jnp.stack([2 * xi + yi, ci]).astype(jnp.int32)
    shard_shape = {n: w[n].shape for n in BIG}
    full = {n: _cast_into_gathered("cast_" + n, w[n], KIND[n], q_arr) for n in BIG}

    def gather(stage, names):
        return stage([full[n] for n in names], [KIND[n] for n in names], [shard_shape[n] for n in names])

    def gathered(names, arrays):
        full.update(zip(names, arrays))

    groups = [["w1_gate"], ["w1_up"], ["w1_down"], ["w_in"], ["w2_gate"], ["ssm_w_glu", "w_out"], ["w2_up"],
              ["w2_down", "w_ple_gate", "w_ple_proj"]]
    near, far, d2d = _gather_near_carry, _gather_far_carry, _gather_d2d_carry

    def stages(*work):
        carries = [gather(stage, groups[g]) for stage, g in work]
        return functools.reduce(_join, carries), [n for _, g in work for n in groups[g]]

    def alone(name, *work):
        carry, names = stages(*work)
        gathered(names, _comm_call(name, carry))

    alone("gather_a", (near, 0))
    alone("gather_b", (far, 0), (near, 1))
    alone("gather_c", (d2d, 0), (far, 1))
    n1 = _rms_fwd("ffn1_norm", x, w["norm_ffn1"])
    carry, names = stages((d2d, 1), (near, 2), (near, 3))
    gate1, landed = _mm_nn("ffn1_gate", n1, full["w1_gate"], BF16, tm=1024, tn=512, tk=2048, carry=carry)
    gathered(names, landed)
    carry, names = stages((far, 2), (far, 3), (near, 4))
    (a1, da_dgate1, da_dup1), landed = _ffn_up("ffn1", n1, full["w1_up"], gate1, carry)
    gathered(names, landed)
    alone("gather_d", (d2d, 2))
    carry, names = stages((d2d, 3), (far, 4), (near, 5), (near, 6))
    h1, landed = _ffn_down("ffn1", a1, full["w1_down"], x, carry)
    gathered(names, landed)
    ffn1 = (n1, a1, da_dgate1, da_dup1)
    nm = _rms_fwd("mix_norm", h1, w["norm_mix"])
    carry, names = stages((d2d, 4), (far, 5), (far, 6))
    z, landed = _mm_nn("in_proj", nm, full["w_in"], F32, tm=1024, tn=512, tk=2048, carry=carry)
    gathered(names, landed)

    disc, disc_vjp = jax.vjp(_ssm_discretize, w["ssm_log_dt"][0], w["ssm_a_re"], w["ssm_a_im"], w["ssm_b_re"],
                             w["ssm_b_im"])
    abar_r, abar_i, bbar_r, bbar_i = disc
    nb = n_groups // GROUPS_PER_BLOCK
    wb = jnp.concatenate([_blockdiag_in(bbar_r), _blockdiag_in(bbar_i)], axis=-1).astype(BF16)
    wc = jnp.concatenate([_blockdiag_out(w["ssm_c_re"]), -_blockdiag_out(w["ssm_c_im"])], axis=1).astype(BF16)
    abar = jnp.concatenate([abar_r.reshape(nb, -1), abar_i.reshape(nb, -1)], axis=-1)
    abar_conj = jnp.concatenate([abar_r.reshape(nb, -1), -abar_i.reshape(nb, -1)], axis=-1)
    carry, names = stages((d2d, 5), (d2d, 6), (near, 7))
    (states, y_pre, yg), landed = _ssm_scan_fwd(z, wb, wc, abar, w["ssm_d"], carry)
    gathered(names, landed)
    q = _mm_nn("glu_proj", yg, full["ssm_w_glu"], F32, tm=1024, tn=1024, tk=1024)

    def glu_norm(rows, pars):
        yv = _gelu(rows[0]) * _sigmoid(rows[1])
        yh, _ = _rms_stats(yv)
        return [yh * pars[0]], []

    yn_ssm = _rowwise("ssm_glu_norm", glu_norm, [y_pre, q], [w["norm_ssm_out"]], [(d_ssm, BF16)])[0]

    tril = jnp.tril(jnp.ones((CHUNK, CHUNK), dtype=bool))
    wm = jnp.where(tril[None], w["gmlp_w_s"], 0.0).astype(BF16)
    bias = jnp.repeat(w["gmlp_b_s"].T, GMLP_HEAD, axis=1)
    yn_gmlp = _gmlp_fwd(z, w["gmlp_norm_v"], wm, bias, w["norm_gmlp_out"])
    ycat = jnp.concatenate([yn_ssm, yn_gmlp], axis=1)
    carry, names = stages((far, 7))
    h2, landed = _mm_nn("out_proj", ycat, full["w_out"], F32, res=h1, alpha=1.0, tm=512, tn=1024, tk=2048,
                        carry=carry)
    gathered(names, landed)

    n2 = _rms_fwd("ffn2_norm", h2, w["norm_ffn2"])
    carry, names = stages((d2d, 7))
    (a2, da_dgate2, da_dup2), landed = _ffn_gateup("ffn2", n2, full["w2_gate"], full["w2_up"], carry)
    gathered(names, landed)
    h3 = _ffn_down("ffn2", a2, full["w2_down"], h2)
    ffn2 = (n2, a2, da_dgate2, da_dup2)
    npl = _rms_fwd("ple_norm", h3, w["norm_ple"])
    pp = _mm_nn("ple_proj", p, full["w_ple_proj"], F32, tm=1024, tn=1024, tk=2048)
    dgq, dpp, dh4, g_norm_final, loss_part = _ple_head(npl, full["w_ple_gate"], h3, pp, tgt, row(w["norm_final"]))
    reducer = _Reducer(shard_shape, c_arr, qc_arr)
    (g_w_ple_proj,) = _mm_tn("ple_dwproj", p, [dpp], BF16, tm=256, tn=1024, tk=4096)
    (g_w_ple_gate,) = _mm_tn("ple_dwgate", npl, [dgq], BF16, tm=512, tn=1024, tk=4096)
    ple = reducer.swap(["w_ple_gate", "w_ple_proj"], [g_w_ple_gate, g_w_ple_proj])
    dnpl, swapped = _mm_nt_sum("ple_dnorm_in", [dgq], [full["w_ple_gate"]], BF16, tm=512, tn=1024, tk=2048,
                               carry=ple.carry)
    ple = reducer.send(ple, swapped)
    dh3, dh3_b, g_norm_ple = _rms_bwd("ple_dnorm", h3, w["norm_ple"], [dnpl], dh4, 0.5)

    dh2, dh2_b, g_norm_ffn2, got, _, up2_w = _ffn_bwd(
        "ffn2", ["w2_gate", "w2_up", "w2_down"], h2, w["norm_ffn2"], full["w2_gate"], full["w2_up"], full["w2_down"],
        ffn2, dh3, dh3_b, 1.0, reducer, riding=ple.carry, last_hop_later=True)
    reducer.end(ple, got)

    dycat = _mm_nt_sum("out_dproj", [dh2_b], [full["w_out"]], F32, tm=512, tn=1024, tk=2048)
    (g_w_out,) = _mm_tn("out_dw", ycat, [dh2_b], BF16, tm=512, tn=1024, tk=4096)

    dzu, dzv, g_norm_gmlp_out, g_gmlp_norm_v, g_wm, g_s = _gmlp_bwd(z, dycat, w["gmlp_norm_v"], wm, bias,
                                                                  w["norm_gmlp_out"])
    g_gmlp_w_s = jnp.where(tril[None], g_wm, 0.0)
    g_gmlp_b_s = g_s.reshape(CHUNK, -1, GMLP_HEAD).sum(axis=-1).T

    def glu_bwd(rows, pars):
        dyn, ypre, qv = rows
        ygv = _gelu(ypre)
        sg = _sigmoid(qv)
        dy, dg = _rms_backward(ygv * sg, pars[0], dyn)
        return [dy * ygv * sg * (1.0 - sg), dy * sg], [dg]

    dq, dyg_part, g_norm_ssm_out = _rowwise("ssm_dglu", glu_bwd, [(dycat, d_ssm, 0), y_pre, q], [w["norm_ssm_out"]],
                                            [(d_ssm, BF16), (d_ssm, F32)], [(1, d_ssm)])
    dyg_proj = _mm_nt_sum("glu_dproj", [dq], [full["ssm_w_glu"]], F32, tm=1024, tn=1024, tk=1024)
    (g_ssm_w_glu,) = _mm_tn("glu_dw", yg, [dq], BF16, tm=512, tn=1024, tk=4096)

    def gelu_bwd(rows, pars):
        return [(rows[0] + rows[1]) * _gelu_grad(rows[2])], []

    dy_pre = _rowwise("ssm_dgelu", gelu_bwd, [dyg_part, dyg_proj, y_pre], [], [(d_ssm, F32)])[0]
    mixers = reducer.swap(["w_out", "ssm_w_glu"], [g_w_out, g_ssm_w_glu])
    me = 4 * xi + 2 * yi + ci
    small = {"gmlp_norm_v": g_gmlp_norm_v, "gmlp_w_s": g_gmlp_w_s, "gmlp_b_s": g_gmlp_b_s,
             "norm_gmlp_out": g_norm_gmlp_out, "norm_ssm_out": g_norm_ssm_out, "norm_ffn2": g_norm_ffn2,
             "norm_ple": g_norm_ple, "norm_final": g_norm_final}
    before_scan = [n for n in SMALL if n in small]
    scan_blocks = _place_block(_pack([small[n] for n in before_scan] + [loss_part[:, :1]]), me)
    carry, split = _ride(up2_w.carry, mixers.carry, _exchange_carry(scan_blocks))
    (dz_ssm, g_wb, g_wc, g_abar, g_ssm_d), results = _ssm_scan_bwd(z, dy_pre, states, wb, wc, abar_conj, w["ssm_d"],
                                                                  carry)
    got, swapped, (scan_blocks,) = split(results)
    reducer.end(up2_w, got)
    mixers = reducer.send(mixers, swapped)
    g_abar = jnp.transpose(g_abar, (1, 0, 2)).reshape(nb, -1)
    sw = g_abar.shape[-1] // 2
    g_bbar_r = _blockdiag_in_grad(g_wb[..., :sw], SSM_STATE, SSM_GROUP)
    g_bbar_i = _blockdiag_in_grad(g_wb[..., sw:], SSM_STATE, SSM_GROUP)
    g_ssm_c_re = _blockdiag_out_grad(g_wc[:, :sw, :], SSM_GROUP, SSM_STATE)
    g_ssm_c_im = -_blockdiag_out_grad(g_wc[:, sw:, :], SSM_GROUP, SSM_STATE)
    g_abar_r = g_abar[..., :sw].reshape(n_groups, SSM_STATE)
    g_abar_i = g_abar[..., sw:].reshape(n_groups, SSM_STATE)
    g_ssm_log_dt, g_ssm_a_re, g_ssm_a_im, g_ssm_b_re, g_ssm_b_im = disc_vjp((g_abar_r, g_abar_i, g_bbar_r, g_bbar_i))

    dz = jnp.concatenate([dz_ssm, dzu, dzv], axis=1)
    (g_w_in,), got = _mm_tn("in_dw", nm, [dz], BF16, tm=512, tn=1536, tk=4096, carry=mixers.carry)
    reducer.end(mixers, got)
    in_w = reducer.swap(["w_in"], [g_w_in])
    dnm, swapped = _mm_nt_sum("in_dproj", [dz], [full["w_in"]], BF16, tm=1024, tn=1024, tk=3072, carry=in_w.carry)
    in_w = reducer.send(in_w, swapped)
    dh1, dh1_b, g_norm_mix = _rms_bwd("mix_dnorm", h1, w["norm_mix"], [dnm], dh2, 0.5)
    small = {"norm_mix": g_norm_mix, "ssm_log_dt": g_ssm_log_dt, "ssm_a_re": g_ssm_a_re,
             "ssm_a_im": g_ssm_a_im, "ssm_b_re": g_ssm_b_re, "ssm_b_im": g_ssm_b_im, "ssm_c_re": g_ssm_c_re,
             "ssm_c_im": g_ssm_c_im, "ssm_d": g_ssm_d}
    after_scan = [n for n in SMALL if n in small]
    ffn_blocks = _place_block(_pack([small[n] for n in after_scan]), me)
    dx, _, g_norm_ffn1, got, (ffn_blocks,), _ = _ffn_bwd(
        "ffn1", ["w1_gate", "w1_up", "w1_down"], x, w["norm_ffn1"], full["w1_gate"], full["w1_up"], full["w1_down"],
        ffn1, dh1, dh1_b, 1.0, reducer, riding=in_w.carry, riding_dwd=_exchange_carry(ffn_blocks))
    reducer.end(in_w, got)
    scan_grads = _unpack(_sum_blocks("sum_before_scan", scan_blocks), [w[n].shape for n in before_scan] + [(1,)])
    loss = scan_grads[-1].reshape(())
    early = before_scan + after_scan
    early_grads = scan_grads[:-1] + _unpack(_sum_blocks("sum_after_scan", ffn_blocks), [w[n].shape for n in after_scan])

    halves = [reducer.halves[n] for n in BIG]
    late_blocks = _place_block(_pack([g_norm_ffn1]), me)
    last = _join(_share_carry(halves, [shard_shape[n] for n in BIG]), _exchange_carry(late_blocks))
    *shared, late_blocks = _comm_call("share_halves", last)
    grad = dict(zip(BIG, shared))
    grad.update(zip(early, early_grads))
    grad["norm_ffn1"] = _unpack(_sum_blocks("sum_first_norm", late_blocks), [w["norm_ffn1"].shape])[0]

    small_shapes = [w[n].shape for n in SMALL]
    delta, new_m, new_v = {}, {}, {}
    for n in BIG:
        delta[n], new_m[n], new_v[n], grad[n] = _adamw("adamw_" + n, w[n], grad[n], m[n], v[n])
    d_p, m_p, v_p, _ = _adamw("adamw_small", _pack([w[n] for n in SMALL]), _pack([grad[n] for n in SMALL]),
                              _pack([m[n] for n in SMALL]), _pack([v[n] for n in SMALL]))
    for name_list, packed in ((delta, d_p), (new_m, m_p), (new_v, v_p)):
        for n, a in zip(SMALL, _unpack(packed, small_shapes)):
            name_list[n] = a
    return loss, dx, grad, delta, new_m, new_v


def kernel(x, p, norm_ffn1, w1_gate, w1_up, w1_down, norm_mix, w_in, ssm_log_dt, ssm_a_re, ssm_a_im, ssm_b_re, ssm_b_im, ssm_c_re, ssm_c_im, ssm_d, ssm_w_glu, gmlp_norm_v, gmlp_w_s, gmlp_b_s, norm_ssm_out, norm_gmlp_out, w_out, norm_ffn2, w2_gate, w2_up, w2_down, norm_ple, w_ple_gate, w_ple_proj, norm_final, loss_target, m_norm_ffn1, m_w1_gate, m_w1_up, m_w1_down, m_norm_mix, m_w_in, m_ssm_log_dt, m_ssm_a_re, m_ssm_a_im, m_ssm_b_re, m_ssm_b_im, m_ssm_c_re, m_ssm_c_im, m_ssm_d, m_ssm_w_glu, m_gmlp_norm_v, m_gmlp_w_s, m_gmlp_b_s, m_norm_ssm_out, m_norm_gmlp_out, m_w_out, m_norm_ffn2, m_w2_gate, m_w2_up, m_w2_down, m_norm_ple, m_w_ple_gate, m_w_ple_proj, m_norm_final, v_norm_ffn1, v_w1_gate, v_w1_up, v_w1_down, v_norm_mix, v_w_in, v_ssm_log_dt, v_ssm_a_re, v_ssm_a_im, v_ssm_b_re, v_ssm_b_im, v_ssm_c_re, v_ssm_c_im, v_ssm_d, v_ssm_w_glu, v_gmlp_norm_v, v_gmlp_w_s, v_gmlp_b_s, v_norm_ssm_out, v_norm_gmlp_out, v_w_out, v_norm_ffn2, v_w2_gate, v_w2_up, v_w2_down, v_norm_ple, v_w_ple_gate, v_w_ple_proj, v_norm_final):
    given = dict(locals())
    shapes = {n: given[n].shape for n in WEIGHTS}

    def block(name):
        a = given[name]
        if a.ndim == 1:
            return a.reshape(1, -1)
        return a[0] if a.ndim >= 3 else a

    w = {n: block(n) for n in WEIGHTS}
    m = {n: block("m_" + n) for n in WEIGHTS}
    v = {n: block("v_" + n) for n in WEIGHTS}
    loss, dx, grad, delta, new_m, new_v = _step(x[0], p[0, 0], loss_target[0], w, m, v)
    outs = [loss, dx[None]]
    for tree in (grad, delta, new_m, new_v):
        outs += [tree[n].reshape(shapes[n]) for n in WEIGHTS]
    return tuple(outs)
```

```python
import functools
import math

import jax
import jax.numpy as jnp
from jax import lax
from jax.experimental import pallas as pl
from jax.experimental.pallas import tpu as pltpu

F32 = jnp.float32
BF16 = jnp.bfloat16
EPS = 1e-6
SSM_GROUP = 16
SSM_STATE = 64
GROUPS_PER_BLOCK = 8
GMLP_HEAD = 128
CHUNK = 128
ADAM_LR = 0.001
ADAM_B1 = 0.9
ADAM_B2 = 0.999
ADAM_EPS = 1e-08
ADAM_WD = 0.01
ADAM_STEP = 10
N_CHIPS = 4
N_DEV = 8
LANES = 128
VMEM_LIMIT_BYTES = 56 * 1024 * 1024
MESH = pl.DeviceIdType.MESH
GELU_C = math.sqrt(2.0 / math.pi)
GELU_A = 0.044715

_DOT_DIMS = {
    "nn": (((1,), (0,)), ((), ())),
    "nt": (((1,), (1,)), ((), ())),
    "tn": (((0,), (0,)), ((), ())),
}


def _tile(dim, pref, align):
    if dim <= pref:
        return dim
    t = (pref // align) * align
    while t >= align:
        if dim % t == 0:
            return t
        t -= align
    return dim


def _params(semantics):
    return pltpu.CompilerParams(dimension_semantics=semantics, vmem_limit_bytes=VMEM_LIMIT_BYTES)


def _gelu(x):
    return 0.5 * x * (1.0 + jnp.tanh(GELU_C * (x + GELU_A * x * x * x)))


def _gelu_grad(x):
    t = jnp.tanh(GELU_C * (x + GELU_A * x * x * x))
    return 0.5 * (1.0 + t) + 0.5 * x * (1.0 - t * t) * GELU_C * (1.0 + 3.0 * GELU_A * x * x)


def _sigmoid(x):
    return 1.0 / (1.0 + jnp.exp(-x))


def _dot(a, b, mode):
    return lax.dot_general(a.astype(BF16), b.astype(BF16), _DOT_DIMS[mode], preferred_element_type=F32)


class _Carry:
    def __init__(self, arrays, out_shapes, aliases, n_copies, copies):
        self.arrays = list(arrays)
        self.out_shapes = list(out_shapes)
        self.aliases = dict(aliases)
        self.n_copies = n_copies
        self.copies = copies

    def scratch(self):
        return [pltpu.SemaphoreType.DMA((self.n_copies,)), pltpu.SemaphoreType.DMA((self.n_copies,))]

    def split(self, refs):
        n_in, n_out = len(self.arrays), len(self.out_shapes)
        return refs[:n_in], refs[n_in:n_in + n_out], refs[n_in + n_out], refs[n_in + n_out + 1]

    def start(self, refs):
        for cp in self.copies(*self.split(refs)):
            cp.start()

    def wait(self, refs):
        for cp in self.copies(*self.split(refs)):
            cp.wait()


class _SemRange:
    def __init__(self, sems, offset):
        self.sems, self.offset = sems, offset

    @property
    def at(self):
        return self

    def __getitem__(self, k):
        return self.sems.at[self.offset + k]


def _join(first, second):
    n_in, n_out = len(first.arrays), len(first.out_shapes)
    aliases = dict(first.aliases)
    aliases.update({n_in + i: n_out + o for i, o in second.aliases.items()})

    def copies(ops, res, send_sems, recv_sems):
        return (first.copies(ops[:n_in], res[:n_out], send_sems, recv_sems)
                + second.copies(ops[n_in:], res[n_out:], _SemRange(send_sems, first.n_copies),
                                _SemRange(recv_sems, first.n_copies)))

    return _Carry(first.arrays + second.arrays, first.out_shapes + second.out_shapes, aliases,
                  first.n_copies + second.n_copies, copies)


_ANY = pl.BlockSpec(memory_space=pl.ANY)


def _comm_call(name, carry):
    def body(*refs):
        carry.start(refs)
        carry.wait(refs)

    n_in = len(carry.arrays)
    return pl.pallas_call(
        body,
        name=name,
        in_specs=[_ANY] * n_in,
        out_specs=[_ANY] * len(carry.out_shapes),
        out_shape=carry.out_shapes,
        input_output_aliases=carry.aliases,
        scratch_shapes=carry.scratch(),
    )(*carry.arrays)


def _carried_call(body, carry, *, name, grid, in_specs, out_specs, out_shape, scratch_shapes, semantics, args,
                  aliases=None):
    aliases = dict(aliases or {})
    if carry is None:
        res = pl.pallas_call(body, name=name, grid=grid, in_specs=in_specs, out_specs=out_specs, out_shape=out_shape,
                             scratch_shapes=scratch_shapes, input_output_aliases=aliases,
                             compiler_params=_params(semantics))(*args)
        return res, []
    n_in, n_out, n_scr = len(in_specs), len(out_specs), len(scratch_shapes)
    nci, nco = len(carry.arrays), len(carry.out_shapes)

    def wrapped(*refs):
        ins = refs[:n_in]
        outs = refs[n_in + nci:n_in + nci + n_out]
        scr = refs[n_in + nci + n_out + nco:n_in + nci + n_out + nco + n_scr]
        c_refs = (refs[n_in:n_in + nci] + refs[n_in + nci + n_out:n_in + nci + n_out + nco]
                  + refs[n_in + nci + n_out + nco + n_scr:])
        first = functools.reduce(jnp.logical_and, [pl.program_id(d) == 0 for d in range(len(grid))])
        last = functools.reduce(jnp.logical_and, [pl.program_id(d) == grid[d] - 1 for d in range(len(grid))])

        @pl.when(first)
        def _():
            carry.start(c_refs)

        body(*ins, *outs, *scr)

        @pl.when(last)
        def _():
            carry.wait(c_refs)

    res = pl.pallas_call(
        wrapped,
        name=name,
        grid=grid,
        in_specs=list(in_specs) + [_ANY] * nci,
        out_specs=list(out_specs) + [_ANY] * nco,
        out_shape=list(out_shape) + carry.out_shapes,
        input_output_aliases={**aliases, **{n_in + i: n_out + o for i, o in carry.aliases.items()}},
        scratch_shapes=list(scratch_shapes) + carry.scratch(),
        compiler_params=_params(("arbitrary",) * len(grid)),
    )(*args, *carry.arrays)
    return res[:n_out], res[n_out:]


def _matmul(name, mode, a_list, b_list, products, out_dtypes, epilogue, extras=(), tm=512, tn=512, tk=2048,
            carry=None, n_part=(0, 1)):
    a0, b0 = a_list[0], b_list[0]
    if mode == "tn":
        k_dim, m_dim = a0.shape
    else:
        m_dim, k_dim = a0.shape
    n_dim = (b0.shape[0] if mode == "nt" else b0.shape[1]) // n_part[1]
    tm = _tile(m_dim, tm, LANES)
    tn = _tile(n_dim, tn, LANES)
    tk = _tile(k_dim, tk, LANES)
    nk = k_dim // tk
    j0 = n_part[0] * (n_dim // tn)
    chunk = 2 * LANES if (nk == 1 and epilogue is not _identity and tn % (2 * LANES) == 0) else tn
    n_acc = 1 + max(p[2] for p in products)
    na, nb, ne, no = len(a_list), len(b_list), len(extras), len(out_dtypes)

    if mode == "tn":
        a_spec = pl.BlockSpec((tk, tm), lambda i, j, k: (k, i))
    else:
        a_spec = pl.BlockSpec((tm, tk), lambda i, j, k: (i, k))
    if mode == "nt":
        b_spec = pl.BlockSpec((tn, tk), lambda i, j, k: (j0 + j, k))
    else:
        b_spec = pl.BlockSpec((tk, tn), lambda i, j, k: (k, j0 + j))
    t_spec = pl.BlockSpec((tm, tn), lambda i, j, k: (i, j))

    def body(*refs):
        a_refs = refs[:na]
        b_refs = refs[na:na + nb]
        e_refs = refs[na + nb:na + nb + ne]
        o_refs = refs[na + nb + ne:na + nb + ne + no]
        acc_refs = refs[na + nb + ne + no:]

        def partial_sums(cols):
            sums = [None] * n_acc
            for ai, bi, ci in products:
                b = b_refs[bi][cols, :] if mode == "nt" else b_refs[bi][:, cols]
                d = _dot(a_refs[ai][...], b, mode)
                sums[ci] = d if sums[ci] is None else sums[ci] + d
            return sums

        def finish(accs, cols):
            outs = epilogue(accs, [e[:, cols] for e in e_refs])
            for o_ref, o in zip(o_refs, outs):
                o_ref[:, cols] = o.astype(o_ref.dtype)

        if nk == 1:
            for c0 in range(0, tn, chunk):
                finish(partial_sums(slice(c0, c0 + chunk)), slice(c0, c0 + chunk))
        else:
            sums = partial_sums(slice(None))
            finish = functools.partial(finish, cols=slice(None))
            k = pl.program_id(2)

            @pl.when(k == 0)
            def _():
                for acc, s in zip(acc_refs, sums):
                    acc[...] = s

            @pl.when(k > 0)
            def _():
                for acc, s in zip(acc_refs, sums):
                    acc[...] += s

            @pl.when(k == nk - 1)
            def _():
                finish([acc[...] for acc in acc_refs])

    scratch = [pltpu.VMEM((tm, tn), F32) for _ in range(n_acc)] if nk > 1 else []
    outs, carried = _carried_call(
        body, carry,
        name=name,
        grid=(m_dim // tm, n_dim // tn, nk),
        in_specs=[a_spec] * na + [b_spec] * nb + [t_spec] * ne,
        out_specs=[t_spec] * no,
        out_shape=[jax.ShapeDtypeStruct((m_dim, n_dim), dt) for dt in out_dtypes],
        scratch_shapes=scratch,
        semantics=("parallel", "parallel", "arbitrary"),
        args=[*a_list, *b_list, *extras],
    )
    return (outs, carried) if carry else outs


def _identity(accs, extras):
    return accs


def _single(result, carry):
    return (result[0][0], result[1]) if carry else result[0]


def _mm_nn(name, a, b, out_dtype, res=None, alpha=1.0, carry=None, **tiles):
    if res is None:
        return _single(_matmul(name, "nn", [a], [b], [(0, 0, 0)], [out_dtype], _identity, carry=carry, **tiles), carry)

    def epilogue(accs, extras):
        return [extras[0] + alpha * accs[0]]

    return _single(_matmul(name, "nn", [a], [b], [(0, 0, 0)], [out_dtype], epilogue, extras=(res,), carry=carry,
                           **tiles), carry)


def _mm_nt_sum(name, a_list, b_list, out_dtype, carry=None, n_part=(0, 1), **tiles):
    products = [(i, i, 0) for i in range(len(a_list))]
    return _single(_matmul(name, "nt", a_list, b_list, products, [out_dtype], _identity, carry=carry, n_part=n_part,
                           **tiles), carry)


def _mm_tn(name, a, b_list, out_dtype, carry=None, **tiles):
    products = [(0, i, i) for i in range(len(b_list))]
    return _matmul(name, "tn", [a], b_list, products, [out_dtype] * len(b_list), _identity, carry=carry, **tiles)


def _rowwise(name, fn, row_ins, par_ins, row_outs, acc_outs=(), tr=512):
    first = row_ins[0][0] if isinstance(row_ins[0], tuple) else row_ins[0]
    t_dim = first.shape[0]
    tr = _tile(t_dim, tr, 16)
    arrays, specs = [], []
    for r in row_ins:
        if isinstance(r, tuple):
            arr, width, blk = r
            specs.append(pl.BlockSpec((tr, width), lambda i, blk=blk: (i, blk)))
        else:
            arr = r
            specs.append(pl.BlockSpec((tr, arr.shape[1]), lambda i: (i, 0)))
        arrays.append(arr)
    for p in par_ins:
        arrays.append(p)
        specs.append(pl.BlockSpec(p.shape, lambda i, nd=p.ndim: (0,) * nd))
    nr, npar, nro, nacc = len(row_ins), len(par_ins), len(row_outs), len(acc_outs)

    def body(*refs):
        rows = [r[...] for r in refs[:nr]]
        pars = [p[...] for p in refs[nr:nr + npar]]
        o_refs = refs[nr + npar:nr + npar + nro]
        acc_refs = refs[nr + npar + nro:]
        outs, accs = fn(rows, pars)
        for o_ref, o in zip(o_refs, outs):
            o_ref[...] = o.astype(o_ref.dtype)
        if nacc:
            @pl.when(pl.program_id(0) == 0)
            def _():
                for a_ref in acc_refs:
                    a_ref[...] = jnp.zeros_like(a_ref)

            for a_ref, a in zip(acc_refs, accs):
                a_ref[...] += a

    out_shape = [jax.ShapeDtypeStruct((t_dim, c), dt) for c, dt in row_outs]
    out_shape += [jax.ShapeDtypeStruct(s, F32) for s in acc_outs]
    out_specs = [pl.BlockSpec((tr, c), lambda i: (i, 0)) for c, _ in row_outs]
    out_specs += [pl.BlockSpec(s, lambda i: (0, 0)) for s in acc_outs]
    return pl.pallas_call(
        body,
        name=name,
        grid=(t_dim // tr,),
        in_specs=specs,
        out_specs=out_specs,
        out_shape=out_shape,
        compiler_params=_params(("arbitrary",)),
    )(*arrays)


def _rms_stats(x):
    r = lax.rsqrt(jnp.mean(x * x, axis=-1, keepdims=True) + EPS)
    return x * r, r


def _rms_backward(x, g, dy):
    xh, r = _rms_stats(x)
    a = dy * g
    dx = r * (a - xh * jnp.mean(a * xh, axis=-1, keepdims=True))
    return dx, jnp.sum(dy * xh, axis=0, keepdims=True)


def _rms_fwd(name, x, g):
    def fn(rows, pars):
        xh, _ = _rms_stats(rows[0])
        return [xh * pars[0]], []

    return _rowwise(name, fn, [x], [g], [(x.shape[1], BF16)])[0]


def _rms_bwd(name, x, g, dy_parts, dres, scale):
    def fn(rows, pars):
        dy = (rows[2] if len(rows) == 3 else jnp.concatenate(rows[2:], axis=1)).astype(F32)
        dx, dg = _rms_backward(rows[0], pars[0], dy)
        tot = rows[1] + dx
        return [tot, scale * tot], [dg]

    d = x.shape[1]
    return _rowwise(name, fn, [x, dres, *dy_parts], [g], [(d, F32), (d, BF16)], [(1, d)], tr=256)


def _cast_into_gathered(name, w, kind, q_arr):
    rs, cs = w.shape
    tr = _tile(rs, 256, 16)
    nr = rs // tr

    def body(q_ref, w_ref, o_ref):
        o_ref[...] = w_ref[...].astype(BF16)

    if kind == "row":
        o_spec = pl.BlockSpec((tr, cs), lambda i, q_ref: (q_ref[0] * nr + i, 0))
    else:
        o_spec = pl.BlockSpec((tr, cs), lambda i, q_ref: (i, q_ref[0]))
    return pl.pallas_call(
        body,
        name=name,
        grid_spec=pltpu.PrefetchScalarGridSpec(num_scalar_prefetch=1, grid=(nr,),
                                               in_specs=[pl.BlockSpec((tr, cs), lambda i, q_ref: (i, 0))],
                                               out_specs=o_spec),
        out_shape=jax.ShapeDtypeStruct(_full_shape(kind, (rs, cs)), BF16),
        compiler_params=_params(("parallel",)),
    )(q_arr, w)


def _swiglu_tiles(gate, up):
    s = _sigmoid(gate)
    silu = gate * s
    return [silu * up, up * (s * (1.0 + gate * (1.0 - s))), silu]


def _ffn_gateup(tag, n, wg, wu, carry):
    def act(accs, extras):
        return _swiglu_tiles(accs[0], accs[1])

    return _matmul(tag + "_gateup", "nn", [n], [wg, wu], [(0, 0, 0), (0, 1, 1)], [BF16] * 3, act,
                   tm=1024, tn=512, tk=2048, carry=carry)


def _ffn_up(tag, n, wu, gate, carry):
    def act(accs, extras):
        return _swiglu_tiles(extras[0].astype(F32), accs[0])

    return _matmul(tag + "_up", "nn", [n], [wu], [(0, 0, 0)], [BF16] * 3, act, extras=(gate,),
                   tm=1024, tn=512, tk=2048, carry=carry)


def _ffn_down(tag, a, wd, h, carry=None):
    return _mm_nn(tag + "_down", a, wd, F32, res=h, alpha=0.5, tm=1024, tn=512, tk=5632, carry=carry)


def _ffn_bwd(tag, names, h, g, wg, wu, wd, saved, dh, dfb, next_scale, reducer, riding=None, riding_dwd=None,
             last_hop_later=False):
    n, a, da_dgate, da_dup = saved

    def act_bwd(accs, extras):
        return [accs[0] * extras[0].astype(F32), accs[0] * extras[1].astype(F32)]

    dact = _matmul(tag + "_dact", "nt", [dfb], [wd], [(0, 0, 0)], [BF16, BF16], act_bwd, extras=(da_dgate, da_dup),
                   tm=1024, tn=512, tk=2048, carry=riding)
    (dgp, du), rode = dact if riding else (dact, [])
    dwd_call = _mm_tn(tag + "_dwd", a, [dfb], BF16, tm=512, tn=2048, tk=4096, carry=riding_dwd)
    (dwd,), rode_dwd = dwd_call if riding_dwd else (dwd_call, [])
    down = reducer.swap(names[2:], [dwd])
    (dwg,), swapped = _mm_tn(tag + "_dwg", n, [dgp], BF16, tm=512, tn=1408, tk=4096, carry=down.carry)
    down = reducer.send(down, swapped)
    gate_w = reducer.swap(names[:1], [dwg])
    carry, split = _ride(down.carry, gate_w.carry)
    (dwu,), results = _mm_tn(tag + "_dwu", n, [du], BF16, tm=512, tn=1408, tk=4096, carry=carry)
    got, swapped = split(results)
    reducer.end(down, got)
    gate_w = reducer.send(gate_w, swapped)
    up_w = reducer.swap(names[1:2], [dwu])
    carry, split = _ride(gate_w.carry, up_w.carry)
    halves = 1 if last_hop_later else 2
    dn_lo, results = _mm_nt_sum(tag + "_dn_lo", [dgp, du], [wg, wu], BF16, tm=1024, tn=1024, tk=1408, carry=carry,
                                n_part=(0, halves))
    got, swapped = split(results)
    reducer.end(gate_w, got)
    up_w = reducer.send(up_w, swapped)
    dn = [dn_lo]
    if not last_hop_later:
        dn_hi, got = _mm_nt_sum(tag + "_dn_hi", [dgp, du], [wg, wu], BF16, tm=1024, tn=1024, tk=1408,
                                carry=up_w.carry, n_part=(1, 2))
        reducer.end(up_w, got)
        dn.append(dn_hi)
    dh_in, dh_in_b, dg = _rms_bwd(tag + "_dnorm", h, g, dn, dh, next_scale)
    return dh_in, dh_in_b, dg, rode, rode_dwd, (up_w if last_hop_later else None)


def _ssm_discretize(log_dt, a_re, a_im, b_re, b_im):
    dt = jnp.exp(log_dt)[:, None]
    lr = jnp.minimum(a_re, -1e-4)
    li = a_im
    mag = jnp.exp(lr * dt)
    ang = li * dt
    abar_r = mag * jnp.cos(ang)
    abar_i = mag * jnp.sin(ang)
    den = lr * lr + li * li
    xr = abar_r - 1.0
    xi = abar_i
    zr = (xr * lr + xi * li) / den
    zi = (xi * lr - xr * li) / den
    bbar_r = zr[..., None] * b_re - zi[..., None] * b_im
    bbar_i = zr[..., None] * b_im + zi[..., None] * b_re
    return abar_r, abar_i, bbar_r, bbar_i


def _blockdiag_in(b):
    g, n, p = b.shape
    nb = g // GROUPS_PER_BLOCK
    eye = jnp.eye(GROUPS_PER_BLOCK, dtype=b.dtype)
    b4 = b.reshape(nb, GROUPS_PER_BLOCK, n, p)
    return jnp.einsum("sgnp,gh->sgphn", b4, eye).reshape(nb, GROUPS_PER_BLOCK * p, GROUPS_PER_BLOCK * n)


def _blockdiag_in_grad(gw, n, p):
    nb = gw.shape[0]
    eye = jnp.eye(GROUPS_PER_BLOCK, dtype=gw.dtype)
    g5 = gw.reshape(nb, GROUPS_PER_BLOCK, p, GROUPS_PER_BLOCK, n)
    return jnp.einsum("sgphn,gh->sgnp", g5, eye).reshape(nb * GROUPS_PER_BLOCK, n, p)


def _blockdiag_out(c):
    g, p, n = c.shape
    nb = g // GROUPS_PER_BLOCK
    eye = jnp.eye(GROUPS_PER_BLOCK, dtype=c.dtype)
    c4 = c.reshape(nb, GROUPS_PER_BLOCK, p, n)
    return jnp.einsum("sgpn,gh->shngp", c4, eye).reshape(nb, GROUPS_PER_BLOCK * n, GROUPS_PER_BLOCK * p)


def _blockdiag_out_grad(gw, p, n):
    nb = gw.shape[0]
    eye = jnp.eye(GROUPS_PER_BLOCK, dtype=gw.dtype)
    g5 = gw.reshape(nb, GROUPS_PER_BLOCK, n, GROUPS_PER_BLOCK, p)
    return jnp.einsum("shngp,gh->sgpn", g5, eye).reshape(nb * GROUPS_PER_BLOCK, p, n)


def _ssm_scan_fwd(z, wb, wc, abar, d, carry=None):
    t_dim = z.shape[0]
    nb, cb, sw2 = wb.shape
    nl = sw2 // LANES
    hl = nl // 2
    tt = _tile(t_dim, 256, 8)
    nt = t_dim // tt

    def body(z_ref, wb_ref, wc_ref, a_ref, d_ref, s_ref, y_ref, yg_ref, drive_ref, st_ref):
        @pl.when(pl.program_id(0) == 0)
        def _():
            st_ref[...] = jnp.zeros_like(st_ref)

        u = z_ref[...]
        ub = u.astype(BF16)
        for b in range(nb):
            drive = _dot(ub[:, b * cb:(b + 1) * cb], wb_ref[b], "nn")
            for l in range(nl):
                drive_ref[l, pl.ds(b, tt, stride=nb), :] = drive[:, l * LANES:(l + 1) * LANES]
        a = a_ref[...]
        chunk = lambda v, l: v[:, l * LANES:(l + 1) * LANES]

        def step(t, state):
            rows = pl.ds(pl.multiple_of(t * nb, nb), nb)
            re, im = [], []
            for l in range(hl):
                ar, ai, sr, si = chunk(a, l), chunk(a, hl + l), state[l], state[hl + l]
                nr = ar * sr - ai * si + drive_ref[l, rows, :]
                ni = ar * si + ai * sr + drive_ref[hl + l, rows, :]
                s_ref[l, rows, :] = nr
                s_ref[hl + l, rows, :] = ni
                re.append(nr)
                im.append(ni)
            return tuple(re + im)

        state = lax.fori_loop(0, tt, step, tuple(st_ref[l] for l in range(nl)), unroll=8)
        for l in range(nl):
            st_ref[l] = state[l]
        parts = []
        for b in range(nb):
            s_b = jnp.concatenate([s_ref[l, pl.ds(b, tt, stride=nb), :] for l in range(nl)], axis=1)
            parts.append(_dot(s_b, wc_ref[b], "nn"))
        y = jnp.concatenate(parts, axis=1) + d_ref[...] * u
        y_ref[...] = y
        yg_ref[...] = _gelu(y).astype(BF16)

    full = lambda a: pl.BlockSpec(a.shape, lambda t, nd=a.ndim: (0,) * nd)
    return _carried_call(
        body, carry,
        name="ssm_scan_fwd",
        grid=(nt,),
        in_specs=[pl.BlockSpec((tt, nb * cb), lambda t: (t, 0)), full(wb), full(wc), full(abar), full(d)],
        out_specs=[
            pl.BlockSpec((nl, tt * nb, LANES), lambda t: (0, t, 0)),
            pl.BlockSpec((tt, nb * cb), lambda t: (t, 0)),
            pl.BlockSpec((tt, nb * cb), lambda t: (t, 0)),
        ],
        out_shape=[
            jax.ShapeDtypeStruct((nl, t_dim * nb, LANES), F32),
            jax.ShapeDtypeStruct((t_dim, nb * cb), F32),
            jax.ShapeDtypeStruct((t_dim, nb * cb), BF16),
        ],
        scratch_shapes=[pltpu.VMEM((nl, tt * nb, LANES), F32), pltpu.VMEM((nl, nb, LANES), F32)],
        semantics=("arbitrary",),
        args=[z, wb, wc, abar, d],
    )


def _ssm_scan_bwd(z, dy, states, wb, wc, abar_conj, d, dz_all, carry=None):
    t_dim = z.shape[0]
    nb, cb, sw2 = wb.shape
    nl = sw2 // LANES
    hl = nl // 2
    tt = _tile(t_dim, 256, 8)
    nt = t_dim // tt
    edges = states.reshape(nl, nt, tt * nb, LANES)[:, :, (tt - 1) * nb:, :]
    before = jnp.concatenate([jnp.zeros((nl, 1, nb, LANES), F32), edges[:, :-1]], axis=1).reshape(nl, nt * nb, LANES)

    def body(z_ref, dy_ref, s_ref, sp_ref, wb_ref, wc_ref, a_ref, d_ref, dz_all_ref,
             dz_ref, gwb_ref, gwc_ref, ga_ref, gd_ref, gin_ref, gs_ref, st_ref):
        @pl.when(pl.program_id(0) == 0)
        def _():
            st_ref[...] = jnp.zeros_like(st_ref)
            gwb_ref[...] = jnp.zeros_like(gwb_ref)
            gwc_ref[...] = jnp.zeros_like(gwc_ref)
            ga_ref[...] = jnp.zeros_like(ga_ref)
            gd_ref[...] = jnp.zeros_like(gd_ref)

        u = z_ref[...]
        dyv = dy_ref[...]
        ub = u.astype(BF16)
        dyb = dyv.astype(BF16)
        for b in range(nb):
            gin = _dot(dyb[:, b * cb:(b + 1) * cb], wc_ref[b], "nt")
            for l in range(nl):
                gin_ref[l, pl.ds(b, tt, stride=nb), :] = gin[:, l * LANES:(l + 1) * LANES]
        a = a_ref[...]
        chunk = lambda v, l: v[:, l * LANES:(l + 1) * LANES]

        def step(k, state):
            rows = pl.ds(pl.multiple_of((tt - 1 - k) * nb, nb), nb)
            re, im = [], []
            for l in range(hl):
                ar, ai, gr, gi = chunk(a, l), chunk(a, hl + l), state[l], state[hl + l]
                nr = ar * gr - ai * gi + gin_ref[l, rows, :]
                ni = ar * gi + ai * gr + gin_ref[hl + l, rows, :]
                gs_ref[l, rows, :] = nr
                gs_ref[hl + l, rows, :] = ni
                re.append(nr)
                im.append(ni)
            return tuple(re + im)

        state = lax.fori_loop(0, tt, step, tuple(st_ref[l] for l in range(nl)), unroll=8)
        for l in range(nl):
            st_ref[l] = state[l]

        parts = []
        for b in range(nb):
            cols = slice(b * cb, (b + 1) * cb)
            gs_b = jnp.concatenate([gs_ref[l, pl.ds(b, tt, stride=nb), :] for l in range(nl)], axis=1)
            s_b = jnp.concatenate([s_ref[l, pl.ds(b, tt, stride=nb), :] for l in range(nl)], axis=1)
            parts.append(_dot(gs_b, wb_ref[b], "nt"))
            gwb_ref[b] += _dot(ub[:, cols], gs_b, "tn")
            gwc_ref[b] += _dot(s_b, dyb[:, cols], "tn")
        dz_ref[...] = (jnp.concatenate(parts, axis=1) + d_ref[...] * dyv).astype(BF16)
        gd_ref[...] += jnp.sum(dyv * u, axis=0, keepdims=True)

        row = lax.broadcasted_iota(jnp.int32, (tt * nb, LANES), 0)
        shifted = lambda v: jnp.where(row < nb, 0.0, pltpu.roll(v, nb, 0))
        over_time = lambda v: jnp.sum(v.reshape(tt, nb, LANES), axis=0)
        for l in range(hl):
            g_r, g_i = gs_ref[l], gs_ref[hl + l]
            p_r, p_i = shifted(s_ref[l]), shifted(s_ref[hl + l])
            f_r, f_i = sp_ref[l], sp_ref[hl + l]
            g0_r, g0_i = gs_ref[l, pl.ds(0, nb), :], gs_ref[hl + l, pl.ds(0, nb), :]
            ga_ref[l] += over_time(g_r * p_r + g_i * p_i) + g0_r * f_r + g0_i * f_i
            ga_ref[hl + l] += over_time(g_i * p_r - g_r * p_i) + g0_i * f_r - g0_r * f_i

    rev = lambda t: (nt - 1 - t, 0)
    rev3 = lambda t: (0, nt - 1 - t, 0)
    full = lambda a: pl.BlockSpec(a.shape, lambda t, nd=a.ndim: (0,) * nd)
    return _carried_call(
        body, carry,
        name="ssm_scan_bwd",
        grid=(nt,),
        in_specs=[
            pl.BlockSpec((tt, nb * cb), rev),
            pl.BlockSpec((tt, nb * cb), rev),
            pl.BlockSpec((nl, tt * nb, LANES), rev3),
            pl.BlockSpec((nl, nb, LANES), rev3),
            full(wb), full(wc), full(abar_conj), full(d), _ANY,
        ],
        out_specs=[
            pl.BlockSpec((tt, nb * cb), rev),
            pl.BlockSpec((nb, cb, sw2), lambda t: (0, 0, 0)),
            pl.BlockSpec((nb, sw2, cb), lambda t: (0, 0, 0)),
            pl.BlockSpec((nl, nb, LANES), lambda t: (0, 0, 0)),
            pl.BlockSpec((1, nb * cb), lambda t: (0, 0)),
        ],
        out_shape=[
            jax.ShapeDtypeStruct(dz_all.shape, BF16),
            jax.ShapeDtypeStruct((nb, cb, sw2), F32),
            jax.ShapeDtypeStruct((nb, sw2, cb), F32),
            jax.ShapeDtypeStruct((nl, nb, LANES), F32),
            jax.ShapeDtypeStruct((1, nb * cb), F32),
        ],
        scratch_shapes=[pltpu.VMEM((nl, tt * nb, LANES), F32), pltpu.VMEM((nl, tt * nb, LANES), F32),
                        pltpu.VMEM((nl, nb, LANES), F32)],
        semantics=("arbitrary",),
        args=[z, dy, states, before, wb, wc, abar_conj, d, dz_all],
        aliases={8: 0},
    )


def _gmlp_chunk(zu, zv, gv, wm_ref, bias, n_heads):
    ua = _gelu(zu)
    vg = _gelu(zv)
    xc = vg - jnp.mean(vg, axis=-1, keepdims=True)
    r = lax.rsqrt(jnp.mean(xc * xc, axis=-1, keepdims=True) + EPS)
    vh = xc * r
    vb = (vh * gv).astype(BF16)
    parts = []
    for h in range(n_heads):
        cols = slice(h * GMLP_HEAD, (h + 1) * GMLP_HEAD)
        parts.append(_dot(wm_ref[h], vb[:, cols], "nn"))
    s = jnp.concatenate(parts, axis=1) + bias
    return ua, vh, r, vb, s


def _gmlp_fwd(z, gv, wm, bias, ggo):
    t_dim = z.shape[0]
    dg = gv.shape[1]
    n_heads = dg // GMLP_HEAD
    tr = _tile(t_dim, 256, CHUNK)

    def body(zu_ref, zv_ref, gv_ref, wm_ref, b_ref, ggo_ref, o_ref):
        for ck in range(tr // CHUNK):
            rows = pl.ds(ck * CHUNK, CHUNK)
            ua, _, _, _, s = _gmlp_chunk(zu_ref[rows, :], zv_ref[rows, :], gv_ref[...], wm_ref, b_ref[...], n_heads)
            yh, _ = _rms_stats(ua * s)
            o_ref[rows, :] = (yh * ggo_ref[...]).astype(BF16)

    full = lambda a: pl.BlockSpec(a.shape, lambda i, nd=a.ndim: (0,) * nd)
    return pl.pallas_call(
        body,
        name="gmlp_fwd",
        grid=(t_dim // tr,),
        in_specs=[pl.BlockSpec((tr, dg), lambda i: (i, 1)), pl.BlockSpec((tr, dg), lambda i: (i, 2)),
                  full(gv), full(wm), full(bias), full(ggo)],
        out_specs=pl.BlockSpec((tr, dg), lambda i: (i, 0)),
        out_shape=jax.ShapeDtypeStruct((t_dim, dg), BF16),
        compiler_params=_params(("parallel",)),
    )(z, z, gv, wm, bias, ggo)


def _gmlp_bwd(z, dycat, gv, wm, bias, ggo):
    t_dim = z.shape[0]
    dg = gv.shape[1]
    n_heads = dg // GMLP_HEAD
    tr = _tile(t_dim, 256, CHUNK)

    def body(zu_ref, zv_ref, dy_ref, gv_ref, wm_ref, b_ref, ggo_ref,
             dz_ref, dggo_ref, dgv_ref, dwm_ref, dsum_ref):
        @pl.when(pl.program_id(0) == 0)
        def _():
            dggo_ref[...] = jnp.zeros_like(dggo_ref)
            dgv_ref[...] = jnp.zeros_like(dgv_ref)
            dwm_ref[...] = jnp.zeros_like(dwm_ref)
            dsum_ref[...] = jnp.zeros_like(dsum_ref)

        for ck in range(tr // CHUNK):
            rows = pl.ds(ck * CHUNK, CHUNK)
            zu = zu_ref[rows, :]
            zv = zv_ref[rows, :]
            gvv = gv_ref[...]
            ua, vh, r, vb, s = _gmlp_chunk(zu, zv, gvv, wm_ref, b_ref[...], n_heads)
            dy, dggo = _rms_backward(ua * s, ggo_ref[...], dy_ref[rows, :])
            dggo_ref[...] += dggo
            ds = dy * ua
            dsum_ref[...] += ds
            dsb = ds.astype(BF16)
            parts = []
            for h in range(n_heads):
                cols = slice(h * GMLP_HEAD, (h + 1) * GMLP_HEAD)
                dwm_ref[h] += _dot(dsb[:, cols], vb[:, cols], "nt")
                parts.append(_dot(wm_ref[h], dsb[:, cols], "tn"))
            dv = jnp.concatenate(parts, axis=1)
            dgv_ref[...] += jnp.sum(dv * vh, axis=0, keepdims=True)
            dvh = dv * gvv
            dvg = r * (dvh - jnp.mean(dvh, axis=-1, keepdims=True) - vh * jnp.mean(dvh * vh, axis=-1, keepdims=True))
            dz_ref[rows, pl.ds(2 * dg, dg)] = (dvg * _gelu_grad(zv)).astype(BF16)
            dz_ref[rows, pl.ds(dg, dg)] = (dy * s * _gelu_grad(zu)).astype(BF16)

    full = lambda a: pl.BlockSpec(a.shape, lambda i, nd=a.ndim: (0,) * nd)
    return pl.pallas_call(
        body,
        name="gmlp_bwd",
        grid=(t_dim // tr,),
        in_specs=[pl.BlockSpec((tr, dg), lambda i: (i, 1)), pl.BlockSpec((tr, dg), lambda i: (i, 2)),
                  pl.BlockSpec((tr, dg), lambda i: (i, 1)), full(gv), full(wm), full(bias), full(ggo)],
        out_specs=[pl.BlockSpec((tr, 3 * dg), lambda i: (i, 0)),
                   pl.BlockSpec((1, dg), lambda i: (0, 0)), pl.BlockSpec((1, dg), lambda i: (0, 0)),
                   pl.BlockSpec(wm.shape, lambda i: (0, 0, 0)), pl.BlockSpec((CHUNK, dg), lambda i: (0, 0))],
        out_shape=[jax.ShapeDtypeStruct((t_dim, 3 * dg), BF16),
                   jax.ShapeDtypeStruct((1, dg), F32), jax.ShapeDtypeStruct((1, dg), F32),
                   jax.ShapeDtypeStruct(wm.shape, F32), jax.ShapeDtypeStruct((CHUNK, dg), F32)],
        compiler_params=_params(("arbitrary",)),
    )(z, z, dycat, gv, wm, bias, ggo)


def _ple_head(npl, w_gate, h3, pp, tgt, g_final):
    t_dim, d = h3.shape
    tr = _tile(t_dim, 256, 16)

    def body(n_ref, w_ref, h_ref, pp_ref, t_ref, g_ref, dgq_ref, dpp_ref, dh_ref, dg_ref, loss_ref):
        @pl.when(pl.program_id(0) == 0)
        def _():
            dg_ref[...] = jnp.zeros_like(dg_ref)
            loss_ref[...] = jnp.zeros_like(loss_ref)

        gate = _sigmoid(_dot(n_ref[...], w_ref[...], "nn"))
        ppv = pp_ref[...]
        h4 = h_ref[...] + gate * ppv
        xh, _ = _rms_stats(h4)
        err = xh * g_ref[...] - t_ref[...]
        dh4, dg = _rms_backward(h4, g_ref[...], err * (1.0 / d))
        dh_ref[...] = dh4
        dgq_ref[...] = (dh4 * ppv * gate * (1.0 - gate)).astype(BF16)
        dpp_ref[...] = (dh4 * gate).astype(BF16)
        dg_ref[...] += dg
        loss_ref[...] += jnp.full((1, LANES), 0.5 * jnp.sum(err * err) * (1.0 / d), F32)

    rows = pl.BlockSpec((tr, d), lambda i: (i, 0))
    whole = lambda a: pl.BlockSpec(a.shape, lambda i: (0, 0))
    return pl.pallas_call(
        body,
        name="ple_head",
        grid=(t_dim // tr,),
        in_specs=[rows, whole(w_gate), rows, rows, rows, whole(g_final)],
        out_specs=[rows, rows, rows, pl.BlockSpec((1, d), lambda i: (0, 0)), pl.BlockSpec((1, LANES), lambda i: (0, 0))],
        out_shape=[jax.ShapeDtypeStruct((t_dim, d), BF16), jax.ShapeDtypeStruct((t_dim, d), BF16),
                   jax.ShapeDtypeStruct((t_dim, d), F32), jax.ShapeDtypeStruct((1, d), F32),
                   jax.ShapeDtypeStruct((1, LANES), F32)],
        compiler_params=_params(("arbitrary",)),
    )(npl, w_gate, h3, pp, tgt, g_final)


def _position():
    x, y, c = lax.axis_index("x"), lax.axis_index("y"), lax.axis_index("c")
    chips = [(1 - x, y), (x, 1 - y), (1 - x, 1 - y)]
    return x, y, c, chips


def _region(ref, kind, shard_shape, q, half, part=None):
    rs, cs = shard_shape
    r0, nr = (0, rs) if half is None else (half * (rs // 2), rs // 2)
    if part is not None:
        r0, nr = r0 + part * (rs // 4), rs // 4
    if kind == "row":
        return ref.at[pl.ds(q * rs + r0, nr), :]
    return ref.at[pl.ds(r0, nr), pl.ds(q * cs, cs)]


def _full_shape(kind, shard_shape):
    rs, cs = shard_shape
    return (N_CHIPS * rs, cs) if kind == "row" else (rs, N_CHIPS * cs)


def _remote(src, dst, send_sems, recv_sems, k, to):
    return pltpu.make_async_remote_copy(src_ref=src, dst_ref=dst, send_sem=send_sems.at[k], recv_sem=recv_sems.at[k],
                                        device_id=to, device_id_type=MESH)


def _same(arrays):
    return [jax.ShapeDtypeStruct(a.shape, a.dtype) for a in arrays]


def _gather_near_carry(gathered, kinds, shapes):
    nw = len(gathered)

    def copies(ops, full, send_sems, recv_sems):
        x, y, c, _ = _position()
        out = []
        for w in range(nw):
            mine = _region(full[w], kinds[w], shapes[w], 2 * x + y, c)
            out.append(_remote(mine, mine, send_sems, recv_sems, 2 * w, (1 - x, y, c)))
            out.append(_remote(mine, mine, send_sems, recv_sems, 2 * w + 1, (x, 1 - y, c)))
        return out

    return _Carry(gathered, _same(gathered), {i: i for i in range(nw)}, 2 * nw, copies)


def _gather_far_carry(gathered, kinds, shapes):
    nw = len(gathered)

    def copies(ops, full, send_sems, recv_sems):
        x, y, c, _ = _position()
        out = []
        for w in range(nw):
            from_x = _region(full[w], kinds[w], shapes[w], 2 * (1 - x) + y, c, part=1)
            from_y = _region(full[w], kinds[w], shapes[w], 2 * x + (1 - y), c, part=0)
            out.append(_remote(from_x, from_x, send_sems, recv_sems, 2 * w, (x, 1 - y, c)))
            out.append(_remote(from_y, from_y, send_sems, recv_sems, 2 * w + 1, (1 - x, y, c)))
        return out

    return _Carry(gathered, _same(gathered), {i: i for i in range(nw)}, 2 * nw, copies)


def _gather_d2d_carry(gathered, kinds, shapes):
    nw = len(gathered)

    def copies(ops, full, send_sems, recv_sems):
        x, y, c, chips = _position()
        out = []
        for w in range(nw):
            for j, (cx, cy) in enumerate(chips):
                landed = _region(full[w], kinds[w], shapes[w], 2 * cx + cy, c)
                out.append(_remote(landed, landed, send_sems, recv_sems, 3 * w + j, (x, y, 1 - c)))
        return out

    return _Carry(gathered, _same(gathered), {i: i for i in range(nw)}, 3 * nw, copies)


def _pairs_carry(grads, kinds, shapes):
    nw = len(grads)

    def copies(g, got, send_sems, recv_sems):
        x, y, c, _ = _position()
        out = []
        for w in range(nw):
            for q in range(N_CHIPS):
                out.append(_remote(_region(g[w], kinds[w], shapes[w], q, 1 - c), got[w].at[q], send_sems, recv_sems,
                                   N_CHIPS * w + q, (x, y, 1 - c)))
        return out

    outs = [jax.ShapeDtypeStruct((N_CHIPS, s[0] // 2, s[1]), BF16) for s in shapes]
    return _Carry(grads, outs, {}, N_CHIPS * nw, copies)


def _pair_sum(name, grad, got, kind, shard_shape, c_arr):
    rs, cs = shard_shape
    hr = rs // 2
    tr = _tile(hr, 512, 16)
    nr = hr // tr

    def body(c_ref, g_ref, s_ref, o_ref):
        o_ref[...] = (g_ref[...].astype(F32) + s_ref[...].astype(F32)).astype(BF16)

    if kind == "row":
        g_spec = pl.BlockSpec((tr, cs), lambda q, i, c_ref: (q * (rs // tr) + c_ref[0] * nr + i, 0))
    else:
        g_spec = pl.BlockSpec((tr, cs), lambda q, i, c_ref: (c_ref[0] * nr + i, q))
    blk = pl.BlockSpec((None, tr, cs), lambda q, i, c_ref: (q, i, 0))
    return pl.pallas_call(
        body,
        name=name,
        grid_spec=pltpu.PrefetchScalarGridSpec(num_scalar_prefetch=1, grid=(N_CHIPS, nr), in_specs=[g_spec, blk],
                                               out_specs=blk),
        out_shape=jax.ShapeDtypeStruct((N_CHIPS, hr, cs), BF16),
        compiler_params=_params(("parallel", "parallel")),
    )(c_arr, grad, got)


def _scatter_carry(sums, shapes):
    nw = len(sums)

    def copies(ps, got, send_sems, recv_sems):
        x, y, c, chips = _position()
        out = []
        for w in range(nw):
            for j, (cx, cy) in enumerate(chips):
                out.append(_remote(ps[w].at[2 * cx + cy], got[w].at[j], send_sems, recv_sems, 3 * w + j, (cx, cy, c)))
        return out

    outs = [jax.ShapeDtypeStruct((3, s[0] // 2, s[1]), BF16) for s in shapes]
    return _Carry(sums, outs, {}, 3 * nw, copies)


def _owner_sum(name, sums, got, shard_shape, qc_arr):
    rs, cs = shard_shape
    hr = rs // 2
    tr = _tile(hr, 512, 16)
    nr = hr // tr

    def body(qc_ref, mine_ref, got_ref, o_ref):
        acc = mine_ref[...].astype(F32)
        for j in range(3):
            acc = acc + got_ref[j].astype(F32)
        o_ref[...] = acc

    return pl.pallas_call(
        body,
        name=name,
        grid_spec=pltpu.PrefetchScalarGridSpec(
            num_scalar_prefetch=1, grid=(nr,),
            in_specs=[pl.BlockSpec((None, tr, cs), lambda i, qc_ref: (qc_ref[0], i, 0)),
                      pl.BlockSpec((3, tr, cs), lambda i, qc_ref: (0, i, 0))],
            out_specs=pl.BlockSpec((tr, cs), lambda i, qc_ref: (qc_ref[1] * nr + i, 0))),
        out_shape=jax.ShapeDtypeStruct((rs, cs), F32),
        compiler_params=_params(("parallel",)),
    )(qc_arr, sums, got)


def _share_carry(grads, shapes):
    nw = len(grads)

    def copies(ops, out, send_sems, recv_sems):
        x, y, c, _ = _position()
        res = []
        for w in range(nw):
            hr = shapes[w][0] // 2
            mine = out[w].at[pl.ds(c * hr, hr), :]
            res.append(_remote(mine, mine, send_sems, recv_sems, w, (x, y, 1 - c)))
        return res

    return _Carry(grads, _same(grads), {i: i for i in range(nw)}, nw, copies)


def _place_block(packed, me):
    return lax.dynamic_update_slice(jnp.zeros((N_DEV,) + packed.shape, F32), packed[None], (me, 0, 0))


def _exchange_carry(blocks):
    def copies(ops, res, send_sems, recv_sems):
        x, y, c, _ = _position()
        mine = res[0].at[4 * x + 2 * y + c]
        out = []
        for k in range(1, N_DEV):
            to = ((1 - x) if k & 4 else x, (1 - y) if k & 2 else y, (1 - c) if k & 1 else c)
            out.append(_remote(mine, mine, send_sems, recv_sems, k - 1, to))
        return out

    return _Carry([blocks], _same([blocks]), {0: 0}, N_DEV - 1, copies)


def _sum_blocks(name, blocks):
    n, rows, lanes = blocks.shape
    tr = _tile(rows, 4096, 8)

    def body(b_ref, o_ref):
        acc = b_ref[0]
        for k in range(1, n):
            acc = acc + b_ref[k]
        o_ref[...] = acc

    return pl.pallas_call(
        body,
        name=name,
        grid=(rows // tr,),
        in_specs=[pl.BlockSpec((n, tr, lanes), lambda i: (0, i, 0))],
        out_specs=pl.BlockSpec((tr, lanes), lambda i: (i, 0)),
        out_shape=jax.ShapeDtypeStruct((rows, lanes), F32),
        compiler_params=_params(("parallel",)),
    )(blocks)


def _adamw(name, w, g, m, v):
    def fn(rows, pars):
        wv, gv, mv, vv = rows
        m_new = ADAM_B1 * mv + (1.0 - ADAM_B1) * gv
        v_new = ADAM_B2 * vv + (1.0 - ADAM_B2) * (gv * gv)
        m_hat = m_new / (1.0 - ADAM_B1 ** ADAM_STEP)
        v_hat = v_new / (1.0 - ADAM_B2 ** ADAM_STEP)
        delta = -ADAM_LR * (m_hat / (jnp.sqrt(v_hat) + ADAM_EPS) + ADAM_WD * wv)
        return [delta, m_new, v_new, gv], []

    c = w.shape[1]
    return _rowwise(name, fn, [w, g, m, v], [], [(c, F32)] * 4, tr=256)


def _pack(arrays):
    rows = []
    for a in arrays:
        flat = a.reshape(-1).astype(F32)
        pad = (-flat.shape[0]) % LANES
        rows.append(jnp.pad(flat, (0, pad)).reshape(-1, LANES))
    stacked = jnp.concatenate(rows, axis=0)
    pad_rows = (-stacked.shape[0]) % 8
    return jnp.pad(stacked, ((0, pad_rows), (0, 0)))


def _unpack(packed, shapes):
    out, r = [], 0
    for s in shapes:
        n = math.prod(s)
        nr = -(-n // LANES)
        out.append(packed[r:r + nr].reshape(-1)[:n].reshape(s))
        r += nr
    return out


BIG = ["w1_gate", "w1_up", "w1_down", "w_in", "ssm_w_glu", "w_out", "w2_gate", "w2_up", "w2_down", "w_ple_gate",
       "w_ple_proj"]
KIND = {"w1_gate": "col", "w1_up": "col", "w1_down": "row", "w_in": "col", "ssm_w_glu": "row", "w_out": "row",
        "w2_gate": "col", "w2_up": "col", "w2_down": "row", "w_ple_gate": "row", "w_ple_proj": "col"}
SMALL = ["norm_ffn1", "norm_mix", "ssm_log_dt", "ssm_a_re", "ssm_a_im", "ssm_b_re", "ssm_b_im", "ssm_c_re", "ssm_c_im",
         "ssm_d", "gmlp_norm_v", "gmlp_w_s", "gmlp_b_s", "norm_ssm_out", "norm_gmlp_out", "norm_ffn2", "norm_ple",
         "norm_final"]
WEIGHTS = ["norm_ffn1", "w1_gate", "w1_up", "w1_down", "norm_mix", "w_in", "ssm_log_dt", "ssm_a_re", "ssm_a_im",
           "ssm_b_re", "ssm_b_im", "ssm_c_re", "ssm_c_im", "ssm_d", "ssm_w_glu", "gmlp_norm_v", "gmlp_w_s", "gmlp_b_s",
           "norm_ssm_out", "norm_gmlp_out", "w_out", "norm_ffn2", "w2_gate", "w2_up", "w2_down", "norm_ple",
           "w_ple_gate", "w_ple_proj", "norm_final"]


class _Trip:
    def __init__(self, names, arrays, carry):
        self.names, self.arrays, self.carry = names, arrays, carry


class _Reducer:
    def __init__(self, shard_shape, c_arr, qc_arr):
        self.shard_shape, self.c_arr, self.qc_arr = shard_shape, c_arr, qc_arr
        self.halves = {}

    def swap(self, names, grads):
        kinds = [KIND[n] for n in names]
        shapes = [self.shard_shape[n] for n in names]
        return _Trip(names, grads, _pairs_carry(grads, kinds, shapes))

    def send(self, trip, swapped):
        shapes = [self.shard_shape[n] for n in trip.names]
        sums = [_pair_sum("pair_sum_" + n, g, s, KIND[n], sh, self.c_arr)
                for n, g, s, sh in zip(trip.names, trip.arrays, swapped, shapes)]
        return _Trip(trip.names, sums, _scatter_carry(sums, shapes))

    def end(self, trip, got):
        for n, ps, g in zip(trip.names, trip.arrays, got):
            self.halves[n] = _owner_sum("owner_sum_" + n, ps, g, self.shard_shape[n], self.qc_arr)


def _ride(*carries):
    present = [c for c in carries if c is not None]
    joined = functools.reduce(_join, present) if present else None

    def split(results):
        out, at = [], 0
        for c in carries:
            n = len(c.out_shapes) if c is not None else 0
            out.append(list(results[at:at + n]))
            at += n
        return out

    return joined, split


def _step(x, p, tgt, w, m, v):
    d_model = x.shape[1]
    d_ssm = w["ssm_d"].shape[1]
    n_groups = d_ssm // SSM_GROUP
    row = lambda a: a.reshape(1, -1)

    xi, yi, ci = lax.axis_index("x"), lax.axis_index("y"), lax.axis_index("c")
    c_arr = jnp.reshape(ci, (1,)).astype(jnp.int32)
    q_arr = jnp.reshape(2 * xi + yi, (1,)).astype(jnp.int32)
    qc_arr = jnp.stack([2 * xi + yi, ci]).astype(jnp.int32)
    shard_shape = {n: w[n].shape for n in BIG}
    full = {n: _cast_into_gathered("cast_" + n, w[n], KIND[n], q_arr) for n in BIG}

    def gather(stage, names):
        return stage([full[n] for n in names], [KIND[n] for n in names], [shard_shape[n] for n in names])

    def gathered(names, arrays):
        full.update(zip(names, arrays))

    groups = [["w1_gate"], ["w1_up"], ["w1_down"], ["w_in"], ["w2_gate"], ["ssm_w_glu", "w_out"], ["w2_up"],
              ["w2_down", "w_ple_gate", "w_ple_proj"]]
    near, far, d2d = _gather_near_carry, _gather_far_carry, _gather_d2d_carry

    def stages(*work):
        carries = [gather(stage, groups[g]) for stage, g in work]
        return functools.reduce(_join, carries), [n for _, g in work for n in groups[g]]

    def alone(name, *work):
        carry, names = stages(*work)
        gathered(names, _comm_call(name, carry))

    alone("gather_a", (near, 0))
    alone("gather_b", (far, 0), (near, 1))
    alone("gather_c", (d2d, 0), (far, 1))
    n1 = _rms_fwd("ffn1_norm", x, w["norm_ffn1"])
    carry, names = stages((d2d, 1), (near, 2), (near, 3))
    gate1, landed = _mm_nn("ffn1_gate", n1, full["w1_gate"], BF16, tm=1024, tn=512, tk=2048, carry=carry)
    gathered(names, landed)
    carry, names = stages((far, 2), (far, 3), (near, 4))
    (a1, da_dgate1, da_dup1), landed = _ffn_up("ffn1", n1, full["w1_up"], gate1, carry)
    gathered(names, landed)
    alone("gather_d", (d2d, 2))
    carry, names = stages((d2d, 3), (far, 4), (near, 5), (near, 6))
    h1, landed = _ffn_down("ffn1", a1, full["w1_down"], x, carry)
    gathered(names, landed)
    ffn1 = (n1, a1, da_dgate1, da_dup1)
    nm = _rms_fwd("mix_norm", h1, w["norm_mix"])
    carry, names = stages((d2d, 4), (far, 5), (far, 6))
    z, landed = _mm_nn("in_proj", nm, full["w_in"], F32, tm=1024, tn=512, tk=2048, carry=carry)
    gathered(names, landed)

    disc, disc_vjp = jax.vjp(_ssm_discretize, w["ssm_log_dt"][0], w["ssm_a_re"], w["ssm_a_im"], w["ssm_b_re"],
                             w["ssm_b_im"])
    abar_r, abar_i, bbar_r, bbar_i = disc
    nb = n_groups // GROUPS_PER_BLOCK
    wb = jnp.concatenate([_blockdiag_in(bbar_r), _blockdiag_in(bbar_i)], axis=-1).astype(BF16)
    wc = jnp.concatenate([_blockdiag_out(w["ssm_c_re"]), -_blockdiag_out(w["ssm_c_im"])], axis=1).astype(BF16)
    abar = jnp.concatenate([abar_r.reshape(nb, -1), abar_i.reshape(nb, -1)], axis=-1)
    abar_conj = jnp.concatenate([abar_r.reshape(nb, -1), -abar_i.reshape(nb, -1)], axis=-1)
    carry, names = stages((d2d, 5), (d2d, 6), (near, 7))
    (states, y_pre, yg), landed = _ssm_scan_fwd(z, wb, wc, abar, w["ssm_d"], carry)
    gathered(names, landed)
    q = _mm_nn("glu_proj", yg, full["ssm_w_glu"], F32, tm=1024, tn=1024, tk=1024)

    def glu_norm(rows, pars):
        yv = _gelu(rows[0]) * _sigmoid(rows[1])
        yh, _ = _rms_stats(yv)
        return [yh * pars[0]], []

    yn_ssm = _rowwise("ssm_glu_norm", glu_norm, [y_pre, q], [w["norm_ssm_out"]], [(d_ssm, BF16)])[0]

    tril = jnp.tril(jnp.ones((CHUNK, CHUNK), dtype=bool))
    wm = jnp.where(tril[None], w["gmlp_w_s"], 0.0).astype(BF16)
    bias = jnp.repeat(w["gmlp_b_s"].T, GMLP_HEAD, axis=1)
    yn_gmlp = _gmlp_fwd(z, w["gmlp_norm_v"], wm, bias, w["norm_gmlp_out"])
    ycat = jnp.concatenate([yn_ssm, yn_gmlp], axis=1)
    carry, names = stages((far, 7))
    h2, landed = _mm_nn("out_proj", ycat, full["w_out"], F32, res=h1, alpha=1.0, tm=512, tn=1024, tk=2048,
                        carry=carry)
    gathered(names, landed)

    n2 = _rms_fwd("ffn2_norm", h2, w["norm_ffn2"])
    carry, names = stages((d2d, 7))
    (a2, da_dgate2, da_dup2), landed = _ffn_gateup("ffn2", n2, full["w2_gate"], full["w2_up"], carry)
    gathered(names, landed)
    h3 = _ffn_down("ffn2", a2, full["w2_down"], h2)
    ffn2 = (n2, a2, da_dgate2, da_dup2)
    npl = _rms_fwd("ple_norm", h3, w["norm_ple"])
    pp = _mm_nn("ple_proj", p, full["w_ple_proj"], F32, tm=1024, tn=1024, tk=2048)
    dgq, dpp, dh4, g_norm_final, loss_part = _ple_head(npl, full["w_ple_gate"], h3, pp, tgt, row(w["norm_final"]))
    reducer = _Reducer(shard_shape, c_arr, qc_arr)
    (g_w_ple_proj,) = _mm_tn("ple_dwproj", p, [dpp], BF16, tm=256, tn=1024, tk=4096)
    (g_w_ple_gate,) = _mm_tn("ple_dwgate", npl, [dgq], BF16, tm=512, tn=1024, tk=4096)
    ple = reducer.swap(["w_ple_gate", "w_ple_proj"], [g_w_ple_gate, g_w_ple_proj])
    dnpl, swapped = _mm_nt_sum("ple_dnorm_in", [dgq], [full["w_ple_gate"]], BF16, tm=512, tn=1024, tk=2048,
                               carry=ple.carry)
    ple = reducer.send(ple, swapped)
    dh3, dh3_b, g_norm_ple = _rms_bwd("ple_dnorm", h3, w["norm_ple"], [dnpl], dh4, 0.5)

    dh2, dh2_b, g_norm_ffn2, got, _, up2_w = _ffn_bwd(
        "ffn2", ["w2_gate", "w2_up", "w2_down"], h2, w["norm_ffn2"], full["w2_gate"], full["w2_up"], full["w2_down"],
        ffn2, dh3, dh3_b, 1.0, reducer, riding=ple.carry, last_hop_later=True)
    reducer.end(ple, got)

    dycat = _mm_nt_sum("out_dproj", [dh2_b], [full["w_out"]], F32, tm=512, tn=1024, tk=2048)
    (g_w_out,) = _mm_tn("out_dw", ycat, [dh2_b], BF16, tm=512, tn=1024, tk=4096)

    dz, g_norm_gmlp_out, g_gmlp_norm_v, g_wm, g_s = _gmlp_bwd(z, dycat, w["gmlp_norm_v"], wm, bias,
                                                                  w["norm_gmlp_out"])
    g_gmlp_w_s = jnp.where(tril[None], g_wm, 0.0)
    g_gmlp_b_s = g_s.reshape(CHUNK, -1, GMLP_HEAD).sum(axis=-1).T

    def glu_bwd(rows, pars):
        dyn, ypre, qv = rows
        ygv = _gelu(ypre)
        sg = _sigmoid(qv)
        dy, dg = _rms_backward(ygv * sg, pars[0], dyn)
        return [dy * ygv * sg * (1.0 - sg), dy * sg], [dg]

    dq, dyg_part, g_norm_ssm_out = _rowwise("ssm_dglu", glu_bwd, [(dycat, d_ssm, 0), y_pre, q], [w["norm_ssm_out"]],
                                            [(d_ssm, BF16), (d_ssm, F32)], [(1, d_ssm)])
    dyg_proj = _mm_nt_sum("glu_dproj", [dq], [full["ssm_w_glu"]], F32, tm=1024, tn=1024, tk=1024)
    (g_ssm_w_glu,) = _mm_tn("glu_dw", yg, [dq], BF16, tm=512, tn=1024, tk=4096)

    def gelu_bwd(rows, pars):
        return [(rows[0] + rows[1]) * _gelu_grad(rows[2])], []

    dy_pre = _rowwise("ssm_dgelu", gelu_bwd, [dyg_part, dyg_proj, y_pre], [], [(d_ssm, F32)])[0]
    mixers = reducer.swap(["w_out", "ssm_w_glu"], [g_w_out, g_ssm_w_glu])
    me = 4 * xi + 2 * yi + ci
    small = {"gmlp_norm_v": g_gmlp_norm_v, "gmlp_w_s": g_gmlp_w_s, "gmlp_b_s": g_gmlp_b_s,
             "norm_gmlp_out": g_norm_gmlp_out, "norm_ssm_out": g_norm_ssm_out, "norm_ffn2": g_norm_ffn2,
             "norm_ple": g_norm_ple, "norm_final": g_norm_final}
    before_scan = [n for n in SMALL if n in small]
    scan_blocks = _place_block(_pack([small[n] for n in before_scan] + [loss_part[:, :1]]), me)
    carry, split = _ride(up2_w.carry, mixers.carry, _exchange_carry(scan_blocks))
    (dz, g_wb, g_wc, g_abar, g_ssm_d), results = _ssm_scan_bwd(z, dy_pre, states, wb, wc, abar_conj, w["ssm_d"],
                                                              dz, carry)
    got, swapped, (scan_blocks,) = split(results)
    reducer.end(up2_w, got)
    mixers = reducer.send(mixers, swapped)
    g_abar = jnp.transpose(g_abar, (1, 0, 2)).reshape(nb, -1)
    sw = g_abar.shape[-1] // 2
    g_bbar_r = _blockdiag_in_grad(g_wb[..., :sw], SSM_STATE, SSM_GROUP)
    g_bbar_i = _blockdiag_in_grad(g_wb[..., sw:], SSM_STATE, SSM_GROUP)
    g_ssm_c_re = _blockdiag_out_grad(g_wc[:, :sw, :], SSM_GROUP, SSM_STATE)
    g_ssm_c_im = -_blockdiag_out_grad(g_wc[:, sw:, :], SSM_GROUP, SSM_STATE)
    g_abar_r = g_abar[..., :sw].reshape(n_groups, SSM_STATE)
    g_abar_i = g_abar[..., sw:].reshape(n_groups, SSM_STATE)
    g_ssm_log_dt, g_ssm_a_re, g_ssm_a_im, g_ssm_b_re, g_ssm_b_im = disc_vjp((g_abar_r, g_abar_i, g_bbar_r, g_bbar_i))

    (g_w_in,), got = _mm_tn("in_dw", nm, [dz], BF16, tm=512, tn=1536, tk=4096, carry=mixers.carry)
    reducer.end(mixers, got)
    in_w = reducer.swap(["w_in"], [g_w_in])
    dnm, swapped = _mm_nt_sum("in_dproj", [dz], [full["w_in"]], BF16, tm=1024, tn=1024, tk=3072, carry=in_w.carry)
    in_w = reducer.send(in_w, swapped)
    dh1, dh1_b, g_norm_mix = _rms_bwd("mix_dnorm", h1, w["norm_mix"], [dnm], dh2, 0.5)
    small = {"norm_mix": g_norm_mix, "ssm_log_dt": g_ssm_log_dt, "ssm_a_re": g_ssm_a_re,
             "ssm_a_im": g_ssm_a_im, "ssm_b_re": g_ssm_b_re, "ssm_b_im": g_ssm_b_im, "ssm_c_re": g_ssm_c_re,
             "ssm_c_im": g_ssm_c_im, "ssm_d": g_ssm_d}
    after_scan = [n for n in SMALL if n in small]
    ffn_blocks = _place_block(_pack([small[n] for n in after_scan]), me)
    done = [n for n in BIG if n in reducer.halves]
    carry, split = _ride(in_w.carry, _share_carry([reducer.halves[n] for n in done], [shard_shape[n] for n in done]))
    dx, _, g_norm_ffn1, results, (ffn_blocks,), _ = _ffn_bwd(
        "ffn1", ["w1_gate", "w1_up", "w1_down"], x, w["norm_ffn1"], full["w1_gate"], full["w1_up"], full["w1_down"],
        ffn1, dh1, dh1_b, 1.0, reducer, riding=carry, riding_dwd=_exchange_carry(ffn_blocks))
    got, shared = split(results)
    grad = dict(zip(done, shared))
    reducer.end(in_w, got)
    scan_grads = _unpack(_sum_blocks("sum_before_scan", scan_blocks), [w[n].shape for n in before_scan] + [(1,)])
    loss = scan_grads[-1].reshape(())
    early = before_scan + after_scan
    early_grads = scan_grads[:-1] + _unpack(_sum_blocks("sum_after_scan", ffn_blocks), [w[n].shape for n in after_scan])

    rest = [n for n in BIG if n not in done]
    late_blocks = _place_block(_pack([g_norm_ffn1]), me)
    last = _join(_share_carry([reducer.halves[n] for n in rest], [shard_shape[n] for n in rest]),
                 _exchange_carry(late_blocks))
    *shared, late_blocks = _comm_call("share_halves", last)
    grad.update(zip(rest, shared))
    grad.update(zip(early, early_grads))
    grad["norm_ffn1"] = _unpack(_sum_blocks("sum_first_norm", late_blocks), [w["norm_ffn1"].shape])[0]

    small_shapes = [w[n].shape for n in SMALL]
    delta, new_m, new_v = {}, {}, {}
    for n in BIG:
        delta[n], new_m[n], new_v[n], grad[n] = _adamw("adamw_" + n, w[n], grad[n], m[n], v[n])
    d_p, m_p, v_p, _ = _adamw("adamw_small", _pack([w[n] for n in SMALL]), _pack([grad[n] for n in SMALL]),
                              _pack([m[n] for n in SMALL]), _pack([v[n] for n in SMALL]))
    for name_list, packed in ((delta, d_p), (new_m, m_p), (new_v, v_p)):
        for n, a in zip(SMALL, _unpack(packed, small_shapes)):
            name_list[n] = a
    return loss, dx, grad, delta, new_m, new_v


def kernel(x, p, norm_ffn1, w1_gate, w1_up, w1_down, norm_mix, w_in, ssm_log_dt, ssm_a_re, ssm_a_im, ssm_b_re, ssm_b_im, ssm_c_re, ssm_c_im, ssm_d, ssm_w_glu, gmlp_norm_v, gmlp_w_s, gmlp_b_s, norm_ssm_out, norm_gmlp_out, w_out, norm_ffn2, w2_gate, w2_up, w2_down, norm_ple, w_ple_gate, w_ple_proj, norm_final, loss_target, m_norm_ffn1, m_w1_gate, m_w1_up, m_w1_down, m_norm_mix, m_w_in, m_ssm_log_dt, m_ssm_a_re, m_ssm_a_im, m_ssm_b_re, m_ssm_b_im, m_ssm_c_re, m_ssm_c_im, m_ssm_d, m_ssm_w_glu, m_gmlp_norm_v, m_gmlp_w_s, m_gmlp_b_s, m_norm_ssm_out, m_norm_gmlp_out, m_w_out, m_norm_ffn2, m_w2_gate, m_w2_up, m_w2_down, m_norm_ple, m_w_ple_gate, m_w_ple_proj, m_norm_final, v_norm_ffn1, v_w1_gate, v_w1_up, v_w1_down, v_norm_mix, v_w_in, v_ssm_log_dt, v_ssm_a_re, v_ssm_a_im, v_ssm_b_re, v_ssm_b_im, v_ssm_c_re, v_ssm_c_im, v_ssm_d, v_ssm_w_glu, v_gmlp_norm_v, v_gmlp_w_s, v_gmlp_b_s, v_norm_ssm_out, v_norm_gmlp_out, v_w_out, v_norm_ffn2, v_w2_gate, v_w2_up, v_w2_down, v_norm_ple, v_w_ple_gate, v_w_ple_proj, v_norm_final):
    given = dict(locals())
    shapes = {n: given[n].shape for n in WEIGHTS}

    def block(name):
        a = given[name]
        if a.ndim == 1:
            return a.reshape(1, -1)
        return a[0] if a.ndim >= 3 else a

    w = {n: block(n) for n in WEIGHTS}
    m = {n: block("m_" + n) for n in WEIGHTS}
    v = {n: block("v_" + n) for n in WEIGHTS}
    loss, dx, grad, delta, new_m, new_v = _step(x[0], p[0, 0], loss_target[0], w, m, v)
    outs = [loss, dx[None]]
    for tree in (grad, delta, new_m, new_v):
        outs += [tree[n].reshape(shapes[n]) for n in WEIGHTS]
    return tuple(outs)
```

```python
import functools
import math

import jax
import jax.numpy as jnp
from jax import lax
from jax.experimental import pallas as pl
from jax.experimental.pallas import tpu as pltpu

F32 = jnp.float32
BF16 = jnp.bfloat16
EPS = 1e-6
SSM_GROUP = 16
SSM_STATE = 64
GROUPS_PER_BLOCK = 8
GMLP_HEAD = 128
CHUNK = 128
ADAM_LR = 0.001
ADAM_B1 = 0.9
ADAM_B2 = 0.999
ADAM_EPS = 1e-08
ADAM_WD = 0.01
ADAM_STEP = 10
N_CHIPS = 4
N_DEV = 8
LANES = 128
VMEM_LIMIT_BYTES = 56 * 1024 * 1024
MESH = pl.DeviceIdType.MESH
GELU_C = math.sqrt(2.0 / math.pi)
GELU_A = 0.044715

_DOT_DIMS = {
    "nn": (((1,), (0,)), ((), ())),
    "nt": (((1,), (1,)), ((), ())),
    "tn": (((0,), (0,)), ((), ())),
}


def _tile(dim, pref, align):
    if dim <= pref:
        return dim
    t = (pref // align) * align
    while t >= align:
        if dim % t == 0:
            return t
        t -= align
    return dim


def _params(semantics):
    return pltpu.CompilerParams(dimension_semantics=semantics, vmem_limit_bytes=VMEM_LIMIT_BYTES)


def _gelu(x):
    return 0.5 * x * (1.0 + jnp.tanh(GELU_C * (x + GELU_A * x * x * x)))


def _gelu_grad(x):
    t = jnp.tanh(GELU_C * (x + GELU_A * x * x * x))
    return 0.5 * (1.0 + t) + 0.5 * x * (1.0 - t * t) * GELU_C * (1.0 + 3.0 * GELU_A * x * x)


def _sigmoid(x):
    return 1.0 / (1.0 + jnp.exp(-x))


def _dot(a, b, mode):
    return lax.dot_general(a.astype(BF16), b.astype(BF16), _DOT_DIMS[mode], preferred_element_type=F32)


class _Carry:
    def __init__(self, arrays, out_shapes, aliases, n_copies, copies):
        self.arrays = list(arrays)
        self.out_shapes = list(out_shapes)
        self.aliases = dict(aliases)
        self.n_copies = n_copies
        self.copies = copies

    def scratch(self):
        return [pltpu.SemaphoreType.DMA((self.n_copies,)), pltpu.SemaphoreType.DMA((self.n_copies,))]

    def split(self, refs):
        n_in, n_out = len(self.arrays), len(self.out_shapes)
        return refs[:n_in], refs[n_in:n_in + n_out], refs[n_in + n_out], refs[n_in + n_out + 1]

    def start(self, refs):
        for cp in self.copies(*self.split(refs)):
            cp.start()

    def wait(self, refs):
        for cp in self.copies(*self.split(refs)):
            cp.wait()


class _SemRange:
    def __init__(self, sems, offset):
        self.sems, self.offset = sems, offset

    @property
    def at(self):
        return self

    def __getitem__(self, k):
        return self.sems.at[self.offset + k]


def _join(first, second):
    n_in, n_out = len(first.arrays), len(first.out_shapes)
    aliases = dict(first.aliases)
    aliases.update({n_in + i: n_out + o for i, o in second.aliases.items()})

    def copies(ops, res, send_sems, recv_sems):
        return (first.copies(ops[:n_in], res[:n_out], send_sems, recv_sems)
                + second.copies(ops[n_in:], res[n_out:], _SemRange(send_sems, first.n_copies),
                                _SemRange(recv_sems, first.n_copies)))

    return _Carry(first.arrays + second.arrays, first.out_shapes + second.out_shapes, aliases,
                  first.n_copies + second.n_copies, copies)


_ANY = pl.BlockSpec(memory_space=pl.ANY)


def _comm_call(name, carry):
    def body(*refs):
        carry.start(refs)
        carry.wait(refs)

    n_in = len(carry.arrays)
    return pl.pallas_call(
        body,
        name=name,
        in_specs=[_ANY] * n_in,
        out_specs=[_ANY] * len(carry.out_shapes),
        out_shape=carry.out_shapes,
        input_output_aliases=carry.aliases,
        scratch_shapes=carry.scratch(),
    )(*carry.arrays)


def _carried_call(body, carry, *, name, grid, in_specs, out_specs, out_shape, scratch_shapes, semantics, args,
                  aliases=None):
    aliases = dict(aliases or {})
    if carry is None:
        res = pl.pallas_call(body, name=name, grid=grid, in_specs=in_specs, out_specs=out_specs, out_shape=out_shape,
                             scratch_shapes=scratch_shapes, input_output_aliases=aliases,
                             compiler_params=_params(semantics))(*args)
        return res, []
    n_in, n_out, n_scr = len(in_specs), len(out_specs), len(scratch_shapes)
    nci, nco = len(carry.arrays), len(carry.out_shapes)

    def wrapped(*refs):
        ins = refs[:n_in]
        outs = refs[n_in + nci:n_in + nci + n_out]
        scr = refs[n_in + nci + n_out + nco:n_in + nci + n_out + nco + n_scr]
        c_refs = (refs[n_in:n_in + nci] + refs[n_in + nci + n_out:n_in + nci + n_out + nco]
                  + refs[n_in + nci + n_out + nco + n_scr:])
        first = functools.reduce(jnp.logical_and, [pl.program_id(d) == 0 for d in range(len(grid))])
        last = functools.reduce(jnp.logical_and, [pl.program_id(d) == grid[d] - 1 for d in range(len(grid))])

        @pl.when(first)
        def _():
            carry.start(c_refs)

        body(*ins, *outs, *scr)

        @pl.when(last)
        def _():
            carry.wait(c_refs)

    res = pl.pallas_call(
        wrapped,
        name=name,
        grid=grid,
        in_specs=list(in_specs) + [_ANY] * nci,
        out_specs=list(out_specs) + [_ANY] * nco,
        out_shape=list(out_shape) + carry.out_shapes,
        input_output_aliases={**aliases, **{n_in + i: n_out + o for i, o in carry.aliases.items()}},
        scratch_shapes=list(scratch_shapes) + carry.scratch(),
        compiler_params=_params(("arbitrary",) * len(grid)),
    )(*args, *carry.arrays)
    return res[:n_out], res[n_out:]


def _matmul(name, mode, a_list, b_list, products, out_dtypes, epilogue, extras=(), tm=512, tn=512, tk=2048,
            carry=None, n_part=(0, 1)):
    a0, b0 = a_list[0], b_list[0]
    if mode == "tn":
        k_dim, m_dim = a0.shape
    else:
        m_dim, k_dim = a0.shape
    n_dim = (b0.shape[0] if mode == "nt" else b0.shape[1]) // n_part[1]
    tm = _tile(m_dim, tm, LANES)
    tn = _tile(n_dim, tn, LANES)
    tk = _tile(k_dim, tk, LANES)
    nk = k_dim // tk
    j0 = n_part[0] * (n_dim // tn)
    chunk = 2 * LANES if (nk == 1 and epilogue is not _identity and tn % (2 * LANES) == 0) else tn
    n_acc = 1 + max(p[2] for p in products)
    na, nb, ne, no = len(a_list), len(b_list), len(extras), len(out_dtypes)

    if mode == "tn":
        a_spec = pl.BlockSpec((tk, tm), lambda i, j, k: (k, i))
    else:
        a_spec = pl.BlockSpec((tm, tk), lambda i, j, k: (i, k))
    if mode == "nt":
        b_spec = pl.BlockSpec((tn, tk), lambda i, j, k: (j0 + j, k))
    else:
        b_spec = pl.BlockSpec((tk, tn), lambda i, j, k: (k, j0 + j))
    t_spec = pl.BlockSpec((tm, tn), lambda i, j, k: (i, j))

    def body(*refs):
        a_refs = refs[:na]
        b_refs = refs[na:na + nb]
        e_refs = refs[na + nb:na + nb + ne]
        o_refs = refs[na + nb + ne:na + nb + ne + no]
        acc_refs = refs[na + nb + ne + no:]

        def partial_sums(cols):
            sums = [None] * n_acc
            for ai, bi, ci in products:
                b = b_refs[bi][cols, :] if mode == "nt" else b_refs[bi][:, cols]
                d = _dot(a_refs[ai][...], b, mode)
                sums[ci] = d if sums[ci] is None else sums[ci] + d
            return sums

        def finish(accs, cols):
            outs = epilogue(accs, [e[:, cols] for e in e_refs])
            for o_ref, o in zip(o_refs, outs):
                o_ref[:, cols] = o.astype(o_ref.dtype)

        if nk == 1:
            for c0 in range(0, tn, chunk):
                finish(partial_sums(slice(c0, c0 + chunk)), slice(c0, c0 + chunk))
        else:
            sums = partial_sums(slice(None))
            finish = functools.partial(finish, cols=slice(None))
            k = pl.program_id(2)

            @pl.when(k == 0)
            def _():
                for acc, s in zip(acc_refs, sums):
                    acc[...] = s

            @pl.when(k > 0)
            def _():
                for acc, s in zip(acc_refs, sums):
                    acc[...] += s

            @pl.when(k == nk - 1)
            def _():
                finish([acc[...] for acc in acc_refs])

    scratch = [pltpu.VMEM((tm, tn), F32) for _ in range(n_acc)] if nk > 1 else []
    outs, carried = _carried_call(
        body, carry,
        name=name,
        grid=(m_dim // tm, n_dim // tn, nk),
        in_specs=[a_spec] * na + [b_spec] * nb + [t_spec] * ne,
        out_specs=[t_spec] * no,
        out_shape=[jax.ShapeDtypeStruct((m_dim, n_dim), dt) for dt in out_dtypes],
        scratch_shapes=scratch,
        semantics=("parallel", "parallel", "arbitrary"),
        args=[*a_list, *b_list, *extras],
    )
    return (outs, carried) if carry else outs


def _identity(accs, extras):
    return accs


def _single(result, carry):
    return (result[0][0], result[1]) if carry else result[0]


def _mm_nn(name, a, b, out_dtype, res=None, alpha=1.0, carry=None, **tiles):
    if res is None:
        return _single(_matmul(name, "nn", [a], [b], [(0, 0, 0)], [out_dtype], _identity, carry=carry, **tiles), carry)

    def epilogue(accs, extras):
        return [extras[0] + alpha * accs[0]]

    return _single(_matmul(name, "nn", [a], [b], [(0, 0, 0)], [out_dtype], epilogue, extras=(res,), carry=carry,
                           **tiles), carry)


def _mm_nt_sum(name, a_list, b_list, out_dtype, carry=None, n_part=(0, 1), **tiles):
    products = [(i, i, 0) for i in range(len(a_list))]
    return _single(_matmul(name, "nt", a_list, b_list, products, [out_dtype], _identity, carry=carry, n_part=n_part,
                           **tiles), carry)


def _mm_tn(name, a, b_list, out_dtype, carry=None, **tiles):
    products = [(0, i, i) for i in range(len(b_list))]
    return _matmul(name, "tn", [a], b_list, products, [out_dtype] * len(b_list), _identity, carry=carry, **tiles)


def _rowwise(name, fn, row_ins, par_ins, row_outs, acc_outs=(), tr=512):
    first = row_ins[0][0] if isinstance(row_ins[0], tuple) else row_ins[0]
    t_dim = first.shape[0]
    tr = _tile(t_dim, tr, 16)
    arrays, specs = [], []
    for r in row_ins:
        if isinstance(r, tuple):
            arr, width, blk = r
            specs.append(pl.BlockSpec((tr, width), lambda i, blk=blk: (i, blk)))
        else:
            arr = r
            specs.append(pl.BlockSpec((tr, arr.shape[1]), lambda i: (i, 0)))
        arrays.append(arr)
    for p in par_ins:
        arrays.append(p)
        specs.append(pl.BlockSpec(p.shape, lambda i, nd=p.ndim: (0,) * nd))
    nr, npar, nro, nacc = len(row_ins), len(par_ins), len(row_outs), len(acc_outs)

    def body(*refs):
        rows = [r[...] for r in refs[:nr]]
        pars = [p[...] for p in refs[nr:nr + npar]]
        o_refs = refs[nr + npar:nr + npar + nro]
        acc_refs = refs[nr + npar + nro:]
        outs, accs = fn(rows, pars)
        for o_ref, o in zip(o_refs, outs):
            o_ref[...] = o.astype(o_ref.dtype)
        if nacc:
            @pl.when(pl.program_id(0) == 0)
            def _():
                for a_ref in acc_refs:
                    a_ref[...] = jnp.zeros_like(a_ref)

            for a_ref, a in zip(acc_refs, accs):
                a_ref[...] += a

    out_shape = [jax.ShapeDtypeStruct((t_dim, c), dt) for c, dt in row_outs]
    out_shape += [jax.ShapeDtypeStruct(s, F32) for s in acc_outs]
    out_specs = [pl.BlockSpec((tr, c), lambda i: (i, 0)) for c, _ in row_outs]
    out_specs += [pl.BlockSpec(s, lambda i: (0, 0)) for s in acc_outs]
    return pl.pallas_call(
        body,
        name=name,
        grid=(t_dim // tr,),
        in_specs=specs,
        out_specs=out_specs,
        out_shape=out_shape,
        compiler_params=_params(("arbitrary",)),
    )(*arrays)


def _rms_stats(x):
    r = lax.rsqrt(jnp.mean(x * x, axis=-1, keepdims=True) + EPS)
    return x * r, r


def _rms_backward(x, g, dy):
    xh, r = _rms_stats(x)
    a = dy * g
    dx = r * (a - xh * jnp.mean(a * xh, axis=-1, keepdims=True))
    return dx, jnp.sum(dy * xh, axis=0, keepdims=True)


def _rms_fwd(name, x, g):
    def fn(rows, pars):
        xh, _ = _rms_stats(rows[0])
        return [xh * pars[0]], []

    return _rowwise(name, fn, [x], [g], [(x.shape[1], BF16)])[0]


def _rms_fwd_both(name, x, g):
    t_dim, d = x.shape
    tr = _tile(t_dim, 512, LANES)

    def body(x_ref, g_ref, n_ref, nt_ref):
        xh, _ = _rms_stats(x_ref[...])
        y = xh * g_ref[...]
        n_ref[...] = y.astype(BF16)
        nt_ref[...] = y.T.astype(BF16)

    return pl.pallas_call(
        body,
        name=name,
        grid=(t_dim // tr,),
        in_specs=[pl.BlockSpec((tr, d), lambda i: (i, 0)), pl.BlockSpec((1, d), lambda i: (0, 0))],
        out_specs=[pl.BlockSpec((tr, d), lambda i: (i, 0)), pl.BlockSpec((d, tr), lambda i: (0, i))],
        out_shape=[jax.ShapeDtypeStruct((t_dim, d), BF16), jax.ShapeDtypeStruct((d, t_dim), BF16)],
        compiler_params=_params(("parallel",)),
    )(x, g)


def _rms_bwd(name, x, g, dy_parts, dres, scale):
    def fn(rows, pars):
        dy = (rows[2] if len(rows) == 3 else jnp.concatenate(rows[2:], axis=1)).astype(F32)
        dx, dg = _rms_backward(rows[0], pars[0], dy)
        tot = rows[1] + dx
        return [tot, scale * tot], [dg]

    d = x.shape[1]
    return _rowwise(name, fn, [x, dres, *dy_parts], [g], [(d, F32), (d, BF16)], [(1, d)], tr=256)


def _cast_into_gathered(name, w, kind, q_arr):
    rs, cs = w.shape
    tr = _tile(rs, 256, 16)
    nr = rs // tr

    def body(q_ref, w_ref, o_ref):
        o_ref[...] = w_ref[...].astype(BF16)

    if kind == "row":
        o_spec = pl.BlockSpec((tr, cs), lambda i, q_ref: (q_ref[0] * nr + i, 0))
    else:
        o_spec = pl.BlockSpec((tr, cs), lambda i, q_ref: (i, q_ref[0]))
    return pl.pallas_call(
        body,
        name=name,
        grid_spec=pltpu.PrefetchScalarGridSpec(num_scalar_prefetch=1, grid=(nr,),
                                               in_specs=[pl.BlockSpec((tr, cs), lambda i, q_ref: (i, 0))],
                                               out_specs=o_spec),
        out_shape=jax.ShapeDtypeStruct(_full_shape(kind, (rs, cs)), BF16),
        compiler_params=_params(("parallel",)),
    )(q_arr, w)


def _swiglu_tiles(gate, up):
    s = _sigmoid(gate)
    silu = gate * s
    return [silu * up, up * (s * (1.0 + gate * (1.0 - s))), silu]


def _ffn_gateup(tag, n, wg, wu, carry):
    def act(accs, extras):
        return _swiglu_tiles(accs[0], accs[1])

    return _matmul(tag + "_gateup", "nn", [n], [wg, wu], [(0, 0, 0), (0, 1, 1)], [BF16] * 3, act,
                   tm=1024, tn=512, tk=2048, carry=carry)


def _ffn_up(tag, n, wu, gate, carry):
    def act(accs, extras):
        return _swiglu_tiles(extras[0].astype(F32), accs[0])

    return _matmul(tag + "_up", "nn", [n], [wu], [(0, 0, 0)], [BF16] * 3, act, extras=(gate,),
                   tm=1024, tn=512, tk=2048, carry=carry)


def _ffn_down(tag, a, wd, h, carry=None):
    return _mm_nn(tag + "_down", a, wd, F32, res=h, alpha=0.5, tm=1024, tn=512, tk=5632, carry=carry)


def _ffn_bwd(tag, names, h, g, wg, wu, wd, saved, dh, dfb, next_scale, reducer, riding=None, riding_dwd=None,
             last_hop_later=False):
    n_t, a, da_dgate, da_dup = saved

    def act_bwd(accs, extras):
        return [accs[0] * extras[0].astype(F32), accs[0] * extras[1].astype(F32)]

    dact = _matmul(tag + "_dact", "nt", [dfb], [wd], [(0, 0, 0)], [BF16, BF16], act_bwd, extras=(da_dgate, da_dup),
                   tm=1024, tn=512, tk=2048, carry=riding)
    (dgp, du), rode = dact if riding else (dact, [])
    dwd_call = _mm_tn(tag + "_dwd", a, [dfb], BF16, tm=512, tn=2048, tk=4096, carry=riding_dwd)
    (dwd,), rode_dwd = dwd_call if riding_dwd else (dwd_call, [])
    down = reducer.swap(names[2:], [dwd])
    dwg, swapped = _mm_nn(tag + "_dwg", n_t, dgp, BF16, tm=512, tn=1408, tk=4096, carry=down.carry)
    down = reducer.send(down, swapped)
    gate_w = reducer.swap(names[:1], [dwg])
    carry, split = _ride(down.carry, gate_w.carry)
    dwu, results = _mm_nn(tag + "_dwu", n_t, du, BF16, tm=512, tn=1408, tk=4096, carry=carry)
    got, swapped = split(results)
    reducer.end(down, got)
    gate_w = reducer.send(gate_w, swapped)
    up_w = reducer.swap(names[1:2], [dwu])
    carry, split = _ride(gate_w.carry, up_w.carry)
    halves = 1 if last_hop_later else 2
    dn_lo, results = _mm_nt_sum(tag + "_dn_lo", [dgp, du], [wg, wu], BF16, tm=512, tn=512, tk=5632, carry=carry,
                                n_part=(0, halves))
    got, swapped = split(results)
    reducer.end(gate_w, got)
    up_w = reducer.send(up_w, swapped)
    dn = [dn_lo]
    if not last_hop_later:
        dn_hi, got = _mm_nt_sum(tag + "_dn_hi", [dgp, du], [wg, wu], BF16, tm=512, tn=512, tk=5632,
                                carry=up_w.carry, n_part=(1, 2))
        reducer.end(up_w, got)
        dn.append(dn_hi)
    dh_in, dh_in_b, dg = _rms_bwd(tag + "_dnorm", h, g, dn, dh, next_scale)
    return dh_in, dh_in_b, dg, rode, rode_dwd, (up_w if last_hop_later else None)


def _ssm_discretize(log_dt, a_re, a_im, b_re, b_im):
    dt = jnp.exp(log_dt)[:, None]
    lr = jnp.minimum(a_re, -1e-4)
    li = a_im
    mag = jnp.exp(lr * dt)
    ang = li * dt
    abar_r = mag * jnp.cos(ang)
    abar_i = mag * jnp.sin(ang)
    den = lr * lr + li * li
    xr = abar_r - 1.0
    xi = abar_i
    zr = (xr * lr + xi * li) / den
    zi = (xi * lr - xr * li) / den
    bbar_r = zr[..., None] * b_re - zi[..., None] * b_im
    bbar_i = zr[..., None] * b_im + zi[..., None] * b_re
    return abar_r, abar_i, bbar_r, bbar_i


def _blockdiag_in(b):
    g, n, p = b.shape
    nb = g // GROUPS_PER_BLOCK
    eye = jnp.eye(GROUPS_PER_BLOCK, dtype=b.dtype)
    b4 = b.reshape(nb, GROUPS_PER_BLOCK, n, p)
    return jnp.einsum("sgnp,gh->sgphn", b4, eye).reshape(nb, GROUPS_PER_BLOCK * p, GROUPS_PER_BLOCK * n)


def _blockdiag_in_grad(gw, n, p):
    nb = gw.shape[0]
    eye = jnp.eye(GROUPS_PER_BLOCK, dtype=gw.dtype)
    g5 = gw.reshape(nb, GROUPS_PER_BLOCK, p, GROUPS_PER_BLOCK, n)
    return jnp.einsum("sgphn,gh->sgnp", g5, eye).reshape(nb * GROUPS_PER_BLOCK, n, p)


def _blockdiag_out(c):
    g, p, n = c.shape
    nb = g // GROUPS_PER_BLOCK
    eye = jnp.eye(GROUPS_PER_BLOCK, dtype=c.dtype)
    c4 = c.reshape(nb, GROUPS_PER_BLOCK, p, n)
    return jnp.einsum("sgpn,gh->shngp", c4, eye).reshape(nb, GROUPS_PER_BLOCK * n, GROUPS_PER_BLOCK * p)


def _blockdiag_out_grad(gw, p, n):
    nb = gw.shape[0]
    eye = jnp.eye(GROUPS_PER_BLOCK, dtype=gw.dtype)
    g5 = gw.reshape(nb, GROUPS_PER_BLOCK, n, GROUPS_PER_BLOCK, p)
    return jnp.einsum("shngp,gh->sgpn", g5, eye).reshape(nb * GROUPS_PER_BLOCK, p, n)


def _ssm_scan_fwd(z, wb, wc, abar, d, carry=None):
    t_dim = z.shape[0]
    nb, cb, sw2 = wb.shape
    nl = sw2 // LANES
    hl = nl // 2
    tt = _tile(t_dim, 256, 8)
    nt = t_dim // tt

    def body(z_ref, wb_ref, wc_ref, a_ref, d_ref, s_ref, y_ref, yg_ref, drive_ref, st_ref):
        @pl.when(pl.program_id(0) == 0)
        def _():
            st_ref[...] = jnp.zeros_like(st_ref)

        u = z_ref[...]
        ub = u.astype(BF16)
        for b in range(nb):
            drive = _dot(ub[:, b * cb:(b + 1) * cb], wb_ref[b], "nn")
            for l in range(nl):
                drive_ref[l, pl.ds(b, tt, stride=nb), :] = drive[:, l * LANES:(l + 1) * LANES]
        a = a_ref[...]
        chunk = lambda v, l: v[:, l * LANES:(l + 1) * LANES]

        def step(t, state):
            rows = pl.ds(pl.multiple_of(t * nb, nb), nb)
            re, im = [], []
            for l in range(hl):
                ar, ai, sr, si = chunk(a, l), chunk(a, hl + l), state[l], state[hl + l]
                nr = ar * sr - ai * si + drive_ref[l, rows, :]
                ni = ar * si + ai * sr + drive_ref[hl + l, rows, :]
                s_ref[l, rows, :] = nr
                s_ref[hl + l, rows, :] = ni
                re.append(nr)
                im.append(ni)
            return tuple(re + im)

        state = lax.fori_loop(0, tt, step, tuple(st_ref[l] for l in range(nl)), unroll=8)
        for l in range(nl):
            st_ref[l] = state[l]
        parts = []
        for b in range(nb):
            s_b = jnp.concatenate([s_ref[l, pl.ds(b, tt, stride=nb), :] for l in range(nl)], axis=1)
            parts.append(_dot(s_b, wc_ref[b], "nn"))
        y = jnp.concatenate(parts, axis=1) + d_ref[...] * u
        y_ref[...] = y
        yg_ref[...] = _gelu(y).astype(BF16)

    full = lambda a: pl.BlockSpec(a.shape, lambda t, nd=a.ndim: (0,) * nd)
    return _carried_call(
        body, carry,
        name="ssm_scan_fwd",
        grid=(nt,),
        in_specs=[pl.BlockSpec((tt, nb * cb), lambda t: (t, 0)), full(wb), full(wc), full(abar), full(d)],
        out_specs=[
            pl.BlockSpec((nl, tt * nb, LANES), lambda t: (0, t, 0)),
            pl.BlockSpec((tt, nb * cb), lambda t: (t, 0)),
            pl.BlockSpec((tt, nb * cb), lambda t: (t, 0)),
        ],
        out_shape=[
            jax.ShapeDtypeStruct((nl, t_dim * nb, LANES), F32),
            jax.ShapeDtypeStruct((t_dim, nb * cb), F32),
            jax.ShapeDtypeStruct((t_dim, nb * cb), BF16),
        ],
        scratch_shapes=[pltpu.VMEM((nl, tt * nb, LANES), F32), pltpu.VMEM((nl, nb, LANES), F32)],
        semantics=("arbitrary",),
        args=[z, wb, wc, abar, d],
    )


def _ssm_scan_bwd(z, dy, states, wb, wc, abar_conj, d, dz_all, carry=None):
    t_dim = z.shape[0]
    nb, cb, sw2 = wb.shape
    nl = sw2 // LANES
    hl = nl // 2
    tt = _tile(t_dim, 256, 8)
    nt = t_dim // tt
    edges = states.reshape(nl, nt, tt * nb, LANES)[:, :, (tt - 1) * nb:, :]
    before = jnp.concatenate([jnp.zeros((nl, 1, nb, LANES), F32), edges[:, :-1]], axis=1).reshape(nl, nt * nb, LANES)

    def body(z_ref, dy_ref, s_ref, sp_ref, wb_ref, wc_ref, a_ref, d_ref, dz_all_ref,
             dz_ref, gwb_ref, gwc_ref, ga_ref, gd_ref, gin_ref, gs_ref, st_ref):
        @pl.when(pl.program_id(0) == 0)
        def _():
            st_ref[...] = jnp.zeros_like(st_ref)
            gwb_ref[...] = jnp.zeros_like(gwb_ref)
            gwc_ref[...] = jnp.zeros_like(gwc_ref)
            ga_ref[...] = jnp.zeros_like(ga_ref)
            gd_ref[...] = jnp.zeros_like(gd_ref)

        u = z_ref[...]
        dyv = dy_ref[...]
        ub = u.astype(BF16)
        dyb = dyv.astype(BF16)
        for b in range(nb):
            gin = _dot(dyb[:, b * cb:(b + 1) * cb], wc_ref[b], "nt")
            for l in range(nl):
                gin_ref[l, pl.ds(b, tt, stride=nb), :] = gin[:, l * LANES:(l + 1) * LANES]
        a = a_ref[...]
        chunk = lambda v, l: v[:, l * LANES:(l + 1) * LANES]

        def step(k, state):
            rows = pl.ds(pl.multiple_of((tt - 1 - k) * nb, nb), nb)
            re, im = [], []
            for l in range(hl):
                ar, ai, gr, gi = chunk(a, l), chunk(a, hl + l), state[l], state[hl + l]
                nr = ar * gr - ai * gi + gin_ref[l, rows, :]
                ni = ar * gi + ai * gr + gin_ref[hl + l, rows, :]
                gs_ref[l, rows, :] = nr
                gs_ref[hl + l, rows, :] = ni
                re.append(nr)
                im.append(ni)
            return tuple(re + im)

        state = lax.fori_loop(0, tt, step, tuple(st_ref[l] for l in range(nl)), unroll=8)
        for l in range(nl):
            st_ref[l] = state[l]

        parts = []
        for b in range(nb):
            cols = slice(b * cb, (b + 1) * cb)
            gs_b = jnp.concatenate([gs_ref[l, pl.ds(b, tt, stride=nb), :] for l in range(nl)], axis=1)
            s_b = jnp.concatenate([s_ref[l, pl.ds(b, tt, stride=nb), :] for l in range(nl)], axis=1)
            parts.append(_dot(gs_b, wb_ref[b], "nt"))
            gwb_ref[b] += _dot(ub[:, cols], gs_b, "tn")
            gwc_ref[b] += _dot(s_b, dyb[:, cols], "tn")
        dz_ref[...] = (jnp.concatenate(parts, axis=1) + d_ref[...] * dyv).astype(BF16)
        gd_ref[...] += jnp.sum(dyv * u, axis=0, keepdims=True)

        row = lax.broadcasted_iota(jnp.int32, (tt * nb, LANES), 0)
        shifted = lambda v: jnp.where(row < nb, 0.0, pltpu.roll(v, nb, 0))
        over_time = lambda v: jnp.sum(v.reshape(tt, nb, LANES), axis=0)
        for l in range(hl):
            g_r, g_i = gs_ref[l], gs_ref[hl + l]
            p_r, p_i = shifted(s_ref[l]), shifted(s_ref[hl + l])
            f_r, f_i = sp_ref[l], sp_ref[hl + l]
            g0_r, g0_i = gs_ref[l, pl.ds(0, nb), :], gs_ref[hl + l, pl.ds(0, nb), :]
            ga_ref[l] += over_time(g_r * p_r + g_i * p_i) + g0_r * f_r + g0_i * f_i
            ga_ref[hl + l] += over_time(g_i * p_r - g_r * p_i) + g0_i * f_r - g0_r * f_i

    rev = lambda t: (nt - 1 - t, 0)
    rev3 = lambda t: (0, nt - 1 - t, 0)
    full = lambda a: pl.BlockSpec(a.shape, lambda t, nd=a.ndim: (0,) * nd)
    return _carried_call(
        body, carry,
        name="ssm_scan_bwd",
        grid=(nt,),
        in_specs=[
            pl.BlockSpec((tt, nb * cb), rev),
            pl.BlockSpec((tt, nb * cb), rev),
            pl.BlockSpec((nl, tt * nb, LANES), rev3),
            pl.BlockSpec((nl, nb, LANES), rev3),
            full(wb), full(wc), full(abar_conj), full(d), _ANY,
        ],
        out_specs=[
            pl.BlockSpec((tt, nb * cb), rev),
            pl.BlockSpec((nb, cb, sw2), lambda t: (0, 0, 0)),
            pl.BlockSpec((nb, sw2, cb), lambda t: (0, 0, 0)),
            pl.BlockSpec((nl, nb, LANES), lambda t: (0, 0, 0)),
            pl.BlockSpec((1, nb * cb), lambda t: (0, 0)),
        ],
        out_shape=[
            jax.ShapeDtypeStruct(dz_all.shape, BF16),
            jax.ShapeDtypeStruct((nb, cb, sw2), F32),
            jax.ShapeDtypeStruct((nb, sw2, cb), F32),
            jax.ShapeDtypeStruct((nl, nb, LANES), F32),
            jax.ShapeDtypeStruct((1, nb * cb), F32),
        ],
        scratch_shapes=[pltpu.VMEM((nl, tt * nb, LANES), F32), pltpu.VMEM((nl, tt * nb, LANES), F32),
                        pltpu.VMEM((nl, nb, LANES), F32)],
        semantics=("arbitrary",),
        args=[z, dy, states, before, wb, wc, abar_conj, d, dz_all],
        aliases={8: 0},
    )


def _gmlp_chunk(zu, zv, gv, wm_ref, bias, n_heads):
    ua = _gelu(zu)
    vg = _gelu(zv)
    xc = vg - jnp.mean(vg, axis=-1, keepdims=True)
    r = lax.rsqrt(jnp.mean(xc * xc, axis=-1, keepdims=True) + EPS)
    vh = xc * r
    vb = (vh * gv).astype(BF16)
    parts = []
    for h in range(n_heads):
        cols = slice(h * GMLP_HEAD, (h + 1) * GMLP_HEAD)
        parts.append(_dot(wm_ref[h], vb[:, cols], "nn"))
    s = jnp.concatenate(parts, axis=1) + bias
    return ua, vh, r, vb, s


def _gmlp_fwd(z, gv, wm, bias, ggo):
    t_dim = z.shape[0]
    dg = gv.shape[1]
    n_heads = dg // GMLP_HEAD
    tr = _tile(t_dim, 256, CHUNK)

    def body(zu_ref, zv_ref, gv_ref, wm_ref, b_ref, ggo_ref, o_ref):
        for ck in range(tr // CHUNK):
            rows = pl.ds(ck * CHUNK, CHUNK)
            ua, _, _, _, s = _gmlp_chunk(zu_ref[rows, :], zv_ref[rows, :], gv_ref[...], wm_ref, b_ref[...], n_heads)
            yh, _ = _rms_stats(ua * s)
            o_ref[rows, :] = (yh * ggo_ref[...]).astype(BF16)

    full = lambda a: pl.BlockSpec(a.shape, lambda i, nd=a.ndim: (0,) * nd)
    return pl.pallas_call(
        body,
        name="gmlp_fwd",
        grid=(t_dim // tr,),
        in_specs=[pl.BlockSpec((tr, dg), lambda i: (i, 1)), pl.BlockSpec((tr, dg), lambda i: (i, 2)),
                  full(gv), full(wm), full(bias), full(ggo)],
        out_specs=pl.BlockSpec((tr, dg), lambda i: (i, 0)),
        out_shape=jax.ShapeDtypeStruct((t_dim, dg), BF16),
        compiler_params=_params(("parallel",)),
    )(z, z, gv, wm, bias, ggo)


def _gmlp_bwd(z, dycat, gv, wm, bias, ggo):
    t_dim = z.shape[0]
    dg = gv.shape[1]
    n_heads = dg // GMLP_HEAD
    tr = _tile(t_dim, 256, CHUNK)

    def body(zu_ref, zv_ref, dy_ref, gv_ref, wm_ref, b_ref, ggo_ref,
             dz_ref, dggo_ref, dgv_ref, dwm_ref, dsum_ref):
        @pl.when(pl.program_id(0) == 0)
        def _():
            dggo_ref[...] = jnp.zeros_like(dggo_ref)
            dgv_ref[...] = jnp.zeros_like(dgv_ref)
            dwm_ref[...] = jnp.zeros_like(dwm_ref)
            dsum_ref[...] = jnp.zeros_like(dsum_ref)

        for ck in range(tr // CHUNK):
            rows = pl.ds(ck * CHUNK, CHUNK)
            zu = zu_ref[rows, :]
            zv = zv_ref[rows, :]
            gvv = gv_ref[...]
            ua, vh, r, vb, s = _gmlp_chunk(zu, zv, gvv, wm_ref, b_ref[...], n_heads)
            dy, dggo = _rms_backward(ua * s, ggo_ref[...], dy_ref[rows, :])
            dggo_ref[...] += dggo
            ds = dy * ua
            dsum_ref[...] += ds
            dsb = ds.astype(BF16)
            parts = []
            for h in range(n_heads):
                cols = slice(h * GMLP_HEAD, (h + 1) * GMLP_HEAD)
                dwm_ref[h] += _dot(dsb[:, cols], vb[:, cols], "nt")
                parts.append(_dot(wm_ref[h], dsb[:, cols], "tn"))
            dv = jnp.concatenate(parts, axis=1)
            dgv_ref[...] += jnp.sum(dv * vh, axis=0, keepdims=True)
            dvh = dv * gvv
            dvg = r * (dvh - jnp.mean(dvh, axis=-1, keepdims=True) - vh * jnp.mean(dvh * vh, axis=-1, keepdims=True))
            dz_ref[rows, pl.ds(2 * dg, dg)] = (dvg * _gelu_grad(zv)).astype(BF16)
            dz_ref[rows, pl.ds(dg, dg)] = (dy * s * _gelu_grad(zu)).astype(BF16)

    full = lambda a: pl.BlockSpec(a.shape, lambda i, nd=a.ndim: (0,) * nd)
    return pl.pallas_call(
        body,
        name="gmlp_bwd",
        grid=(t_dim // tr,),
        in_specs=[pl.BlockSpec((tr, dg), lambda i: (i, 1)), pl.BlockSpec((tr, dg), lambda i: (i, 2)),
                  pl.BlockSpec((tr, dg), lambda i: (i, 1)), full(gv), full(wm), full(bias), full(ggo)],
        out_specs=[pl.BlockSpec((tr, 3 * dg), lambda i: (i, 0)),
                   pl.BlockSpec((1, dg), lambda i: (0, 0)), pl.BlockSpec((1, dg), lambda i: (0, 0)),
                   pl.BlockSpec(wm.shape, lambda i: (0, 0, 0)), pl.BlockSpec((CHUNK, dg), lambda i: (0, 0))],
        out_shape=[jax.ShapeDtypeStruct((t_dim, 3 * dg), BF16),
                   jax.ShapeDtypeStruct((1, dg), F32), jax.ShapeDtypeStruct((1, dg), F32),
                   jax.ShapeDtypeStruct(wm.shape, F32), jax.ShapeDtypeStruct((CHUNK, dg), F32)],
        compiler_params=_params(("arbitrary",)),
    )(z, z, dycat, gv, wm, bias, ggo)


def _ple_head(npl, w_gate, h3, pp, tgt, g_final):
    t_dim, d = h3.shape
    tr = _tile(t_dim, 256, 16)

    def body(n_ref, w_ref, h_ref, pp_ref, t_ref, g_ref, dgq_ref, dpp_ref, dh_ref, dg_ref, loss_ref):
        @pl.when(pl.program_id(0) == 0)
        def _():
            dg_ref[...] = jnp.zeros_like(dg_ref)
            loss_ref[...] = jnp.zeros_like(loss_ref)

        gate = _sigmoid(_dot(n_ref[...], w_ref[...], "nn"))
        ppv = pp_ref[...]
        h4 = h_ref[...] + gate * ppv
        xh, _ = _rms_stats(h4)
        err = xh * g_ref[...] - t_ref[...]
        dh4, dg = _rms_backward(h4, g_ref[...], err * (1.0 / d))
        dh_ref[...] = dh4
        dgq_ref[...] = (dh4 * ppv * gate * (1.0 - gate)).astype(BF16)
        dpp_ref[...] = (dh4 * gate).astype(BF16)
        dg_ref[...] += dg
        loss_ref[...] += jnp.full((1, LANES), 0.5 * jnp.sum(err * err) * (1.0 / d), F32)

    rows = pl.BlockSpec((tr, d), lambda i: (i, 0))
    whole = lambda a: pl.BlockSpec(a.shape, lambda i: (0, 0))
    return pl.pallas_call(
        body,
        name="ple_head",
        grid=(t_dim // tr,),
        in_specs=[rows, whole(w_gate), rows, rows, rows, whole(g_final)],
        out_specs=[rows, rows, rows, pl.BlockSpec((1, d), lambda i: (0, 0)), pl.BlockSpec((1, LANES), lambda i: (0, 0))],
        out_shape=[jax.ShapeDtypeStruct((t_dim, d), BF16), jax.ShapeDtypeStruct((t_dim, d), BF16),
                   jax.ShapeDtypeStruct((t_dim, d), F32), jax.ShapeDtypeStruct((1, d), F32),
                   jax.ShapeDtypeStruct((1, LANES), F32)],
        compiler_params=_params(("arbitrary",)),
    )(npl, w_gate, h3, pp, tgt, g_final)


def _position():
    x, y, c = lax.axis_index("x"), lax.axis_index("y"), lax.axis_index("c")
    chips = [(1 - x, y), (x, 1 - y), (1 - x, 1 - y)]
    return x, y, c, chips


def _region(ref, kind, shard_shape, q, half, part=None):
    rs, cs = shard_shape
    r0, nr = (0, rs) if half is None else (half * (rs // 2), rs // 2)
    if part is not None:
        r0, nr = r0 + part * (rs // 4), rs // 4
    if kind == "row":
        return ref.at[pl.ds(q * rs + r0, nr), :]
    return ref.at[pl.ds(r0, nr), pl.ds(q * cs, cs)]


def _full_shape(kind, shard_shape):
    rs, cs = shard_shape
    return (N_CHIPS * rs, cs) if kind == "row" else (rs, N_CHIPS * cs)


def _remote(src, dst, send_sems, recv_sems, k, to):
    return pltpu.make_async_remote_copy(src_ref=src, dst_ref=dst, send_sem=send_sems.at[k], recv_sem=recv_sems.at[k],
                                        device_id=to, device_id_type=MESH)


def _same(arrays):
    return [jax.ShapeDtypeStruct(a.shape, a.dtype) for a in arrays]


def _gather_near_carry(gathered, kinds, shapes):
    nw = len(gathered)

    def copies(ops, full, send_sems, recv_sems):
        x, y, c, _ = _position()
        out = []
        for w in range(nw):
            mine = _region(full[w], kinds[w], shapes[w], 2 * x + y, c)
            out.append(_remote(mine, mine, send_sems, recv_sems, 2 * w, (1 - x, y, c)))
            out.append(_remote(mine, mine, send_sems, recv_sems, 2 * w + 1, (x, 1 - y, c)))
        return out

    return _Carry(gathered, _same(gathered), {i: i for i in range(nw)}, 2 * nw, copies)


def _gather_far_carry(gathered, kinds, shapes):
    nw = len(gathered)

    def copies(ops, full, send_sems, recv_sems):
        x, y, c, _ = _position()
        out = []
        for w in range(nw):
            from_x = _region(full[w], kinds[w], shapes[w], 2 * (1 - x) + y, c, part=1)
            from_y = _region(full[w], kinds[w], shapes[w], 2 * x + (1 - y), c, part=0)
            out.append(_remote(from_x, from_x, send_sems, recv_sems, 2 * w, (x, 1 - y, c)))
            out.append(_remote(from_y, from_y, send_sems, recv_sems, 2 * w + 1, (1 - x, y, c)))
        return out

    return _Carry(gathered, _same(gathered), {i: i for i in range(nw)}, 2 * nw, copies)


def _gather_d2d_carry(gathered, kinds, shapes):
    nw = len(gathered)

    def copies(ops, full, send_sems, recv_sems):
        x, y, c, chips = _position()
        out = []
        for w in range(nw):
            for j, (cx, cy) in enumerate(chips):
                landed = _region(full[w], kinds[w], shapes[w], 2 * cx + cy, c)
                out.append(_remote(landed, landed, send_sems, recv_sems, 3 * w + j, (x, y, 1 - c)))
        return out

    return _Carry(gathered, _same(gathered), {i: i for i in range(nw)}, 3 * nw, copies)


def _pairs_carry(grads, kinds, shapes):
    nw = len(grads)

    def copies(g, got, send_sems, recv_sems):
        x, y, c, _ = _position()
        out = []
        for w in range(nw):
            for q in range(N_CHIPS):
                out.append(_remote(_region(g[w], kinds[w], shapes[w], q, 1 - c), got[w].at[q], send_sems, recv_sems,
                                   N_CHIPS * w + q, (x, y, 1 - c)))
        return out

    outs = [jax.ShapeDtypeStruct((N_CHIPS, s[0] // 2, s[1]), BF16) for s in shapes]
    return _Carry(grads, outs, {}, N_CHIPS * nw, copies)


def _pair_sum(name, grad, got, kind, shard_shape, c_arr):
    rs, cs = shard_shape
    hr = rs // 2
    tr = _tile(hr, 512, 16)
    nr = hr // tr

    def body(c_ref, g_ref, s_ref, o_ref):
        o_ref[...] = (g_ref[...].astype(F32) + s_ref[...].astype(F32)).astype(BF16)

    if kind == "row":
        g_spec = pl.BlockSpec((tr, cs), lambda q, i, c_ref: (q * (rs // tr) + c_ref[0] * nr + i, 0))
    else:
        g_spec = pl.BlockSpec((tr, cs), lambda q, i, c_ref: (c_ref[0] * nr + i, q))
    blk = pl.BlockSpec((None, tr, cs), lambda q, i, c_ref: (q, i, 0))
    return pl.pallas_call(
        body,
        name=name,
        grid_spec=pltpu.PrefetchScalarGridSpec(num_scalar_prefetch=1, grid=(N_CHIPS, nr), in_specs=[g_spec, blk],
                                               out_specs=blk),
        out_shape=jax.ShapeDtypeStruct((N_CHIPS, hr, cs), BF16),
        compiler_params=_params(("parallel", "parallel")),
    )(c_arr, grad, got)


def _scatter_carry(sums, shapes):
    nw = len(sums)

    def copies(ps, got, send_sems, recv_sems):
        x, y, c, chips = _position()
        out = []
        for w in range(nw):
            for j, (cx, cy) in enumerate(chips):
                out.append(_remote(ps[w].at[2 * cx + cy], got[w].at[j], send_sems, recv_sems, 3 * w + j, (cx, cy, c)))
        return out

    outs = [jax.ShapeDtypeStruct((3, s[0] // 2, s[1]), BF16) for s in shapes]
    return _Carry(sums, outs, {}, 3 * nw, copies)


def _owner_sum(name, sums, got, shard_shape, qc_arr):
    rs, cs = shard_shape
    hr = rs // 2
    tr = _tile(hr, 512, 16)
    nr = hr // tr

    def body(qc_ref, mine_ref, got_ref, o_ref):
        acc = mine_ref[...].astype(F32)
        for j in range(3):
            acc = acc + got_ref[j].astype(F32)
        o_ref[...] = acc

    return pl.pallas_call(
        body,
        name=name,
        grid_spec=pltpu.PrefetchScalarGridSpec(
            num_scalar_prefetch=1, grid=(nr,),
            in_specs=[pl.BlockSpec((None, tr, cs), lambda i, qc_ref: (qc_ref[0], i, 0)),
                      pl.BlockSpec((3, tr, cs), lambda i, qc_ref: (0, i, 0))],
            out_specs=pl.BlockSpec((tr, cs), lambda i, qc_ref: (qc_ref[1] * nr + i, 0))),
        out_shape=jax.ShapeDtypeStruct((rs, cs), F32),
        compiler_params=_params(("parallel",)),
    )(qc_arr, sums, got)


def _share_carry(grads, shapes):
    nw = len(grads)

    def copies(ops, out, send_sems, recv_sems):
        x, y, c, _ = _position()
        res = []
        for w in range(nw):
            hr = shapes[w][0] // 2
            mine = out[w].at[pl.ds(c * hr, hr), :]
            res.append(_remote(mine, mine, send_sems, recv_sems, w, (x, y, 1 - c)))
        return res

    return _Carry(grads, _same(grads), {i: i for i in range(nw)}, nw, copies)


def _place_block(packed, me):
    return lax.dynamic_update_slice(jnp.zeros((N_DEV,) + packed.shape, F32), packed[None], (me, 0, 0))


def _exchange_carry(blocks):
    def copies(ops, res, send_sems, recv_sems):
        x, y, c, _ = _position()
        mine = res[0].at[4 * x + 2 * y + c]
        out = []
        for k in range(1, N_DEV):
            to = ((1 - x) if k & 4 else x, (1 - y) if k & 2 else y, (1 - c) if k & 1 else c)
            out.append(_remote(mine, mine, send_sems, recv_sems, k - 1, to))
        return out

    return _Carry([blocks], _same([blocks]), {0: 0}, N_DEV - 1, copies)


def _sum_blocks(name, blocks):
    n, rows, lanes = blocks.shape
    tr = _tile(rows, 4096, 8)

    def body(b_ref, o_ref):
        acc = b_ref[0]
        for k in range(1, n):
            acc = acc + b_ref[k]
        o_ref[...] = acc

    return pl.pallas_call(
        body,
        name=name,
        grid=(rows // tr,),
        in_specs=[pl.BlockSpec((n, tr, lanes), lambda i: (0, i, 0))],
        out_specs=pl.BlockSpec((tr, lanes), lambda i: (i, 0)),
        out_shape=jax.ShapeDtypeStruct((rows, lanes), F32),
        compiler_params=_params(("parallel",)),
    )(blocks)


def _adamw(name, w, g, m, v):
    def fn(rows, pars):
        wv, gv, mv, vv = rows
        m_new = ADAM_B1 * mv + (1.0 - ADAM_B1) * gv
        v_new = ADAM_B2 * vv + (1.0 - ADAM_B2) * (gv * gv)
        m_hat = m_new / (1.0 - ADAM_B1 ** ADAM_STEP)
        v_hat = v_new / (1.0 - ADAM_B2 ** ADAM_STEP)
        delta = -ADAM_LR * (m_hat / (jnp.sqrt(v_hat) + ADAM_EPS) + ADAM_WD * wv)
        return [delta, m_new, v_new, gv], []

    c = w.shape[1]
    return _rowwise(name, fn, [w, g, m, v], [], [(c, F32)] * 4, tr=256)


def _pack(arrays):
    rows = []
    for a in arrays:
        flat = a.reshape(-1).astype(F32)
        pad = (-flat.shape[0]) % LANES
        rows.append(jnp.pad(flat, (0, pad)).reshape(-1, LANES))
    stacked = jnp.concatenate(rows, axis=0)
    pad_rows = (-stacked.shape[0]) % 8
    return jnp.pad(stacked, ((0, pad_rows), (0, 0)))


def _unpack(packed, shapes):
    out, r = [], 0
    for s in shapes:
        n = math.prod(s)
        nr = -(-n // LANES)
        out.append(packed[r:r + nr].reshape(-1)[:n].reshape(s))
        r += nr
    return out


BIG = ["w1_gate", "w1_up", "w1_down", "w_in", "ssm_w_glu", "w_out", "w2_gate", "w2_up", "w2_down", "w_ple_gate",
       "w_ple_proj"]
KIND = {"w1_gate": "col", "w1_up": "col", "w1_down": "row", "w_in": "col", "ssm_w_glu": "row", "w_out": "row",
        "w2_gate": "col", "w2_up": "col", "w2_down": "row", "w_ple_gate": "row", "w_ple_proj": "col"}
SMALL = ["norm_ffn1", "norm_mix", "ssm_log_dt", "ssm_a_re", "ssm_a_im", "ssm_b_re", "ssm_b_im", "ssm_c_re", "ssm_c_im",
         "ssm_d", "gmlp_norm_v", "gmlp_w_s", "gmlp_b_s", "norm_ssm_out", "norm_gmlp_out", "norm_ffn2", "norm_ple",
         "norm_final"]
WEIGHTS = ["norm_ffn1", "w1_gate", "w1_up", "w1_down", "norm_mix", "w_in", "ssm_log_dt", "ssm_a_re", "ssm_a_im",
           "ssm_b_re", "ssm_b_im", "ssm_c_re", "ssm_c_im", "ssm_d", "ssm_w_glu", "gmlp_norm_v", "gmlp_w_s", "gmlp_b_s",
           "norm_ssm_out", "norm_gmlp_out", "w_out", "norm_ffn2", "w2_gate", "w2_up", "w2_down", "norm_ple",
           "w_ple_gate", "w_ple_proj", "norm_final"]


class _Trip:
    def __init__(self, names, arrays, carry):
        self.names, self.arrays, self.carry = names, arrays, carry


class _Reducer:
    def __init__(self, shard_shape, c_arr, qc_arr):
        self.shard_shape, self.c_arr, self.qc_arr = shard_shape, c_arr, qc_arr
        self.halves = {}

    def swap(self, names, grads):
        kinds = [KIND[n] for n in names]
        shapes = [self.shard_shape[n] for n in names]
        return _Trip(names, grads, _pairs_carry(grads, kinds, shapes))

    def send(self, trip, swapped):
        shapes = [self.shard_shape[n] for n in trip.names]
        sums = [_pair_sum("pair_sum_" + n, g, s, KIND[n], sh, self.c_arr)
                for n, g, s, sh in zip(trip.names, trip.arrays, swapped, shapes)]
        return _Trip(trip.names, sums, _scatter_carry(sums, shapes))

    def end(self, trip, got):
        for n, ps, g in zip(trip.names, trip.arrays, got):
            self.halves[n] = _owner_sum("owner_sum_" + n, ps, g, self.shard_shape[n], self.qc_arr)


def _ride(*carries):
    present = [c for c in carries if c is not None]
    joined = functools.reduce(_join, present) if present else None

    def split(results):
        out, at = [], 0
        for c in carries:
            n = len(c.out_shapes) if c is not None else 0
            out.append(list(results[at:at + n]))
            at += n
        return out

    return joined, split


def _step(x, p, tgt, w, m, v):
    d_model = x.shape[1]
    d_ssm = w["ssm_d"].shape[1]
    n_groups = d_ssm // SSM_GROUP
    row = lambda a: a.reshape(1, -1)

    xi, yi, ci = lax.axis_index("x"), lax.axis_index("y"), lax.axis_index("c")
    c_arr = jnp.reshape(ci, (1,)).astype(jnp.int32)
    q_arr = jnp.reshape(2 * xi + yi, (1,)).astype(jnp.int32)
    qc_arr = jnp.stack([2 * xi + yi, ci]).astype(jnp.int32)
    shard_shape = {n: w[n].shape for n in BIG}
    full = {n: _cast_into_gathered("cast_" + n, w[n], KIND[n], q_arr) for n in BIG}

    def gather(stage, names):
        return stage([full[n] for n in names], [KIND[n] for n in names], [shard_shape[n] for n in names])

    def gathered(names, arrays):
        full.update(zip(names, arrays))

    groups = [["w1_gate"], ["w1_up"], ["w1_down"], ["w_in"], ["w2_gate"], ["ssm_w_glu", "w_out"], ["w2_up"],
              ["w2_down", "w_ple_gate", "w_ple_proj"]]
    near, far, d2d = _gather_near_carry, _gather_far_carry, _gather_d2d_carry

    def stages(*work):
        carries = [gather(stage, groups[g]) for stage, g in work]
        return functools.reduce(_join, carries), [n for _, g in work for n in groups[g]]

    def alone(name, *work):
        carry, names = stages(*work)
        gathered(names, _comm_call(name, carry))

    alone("gather_a", (near, 0))
    alone("gather_b", (far, 0), (near, 1))
    alone("gather_c", (d2d, 0), (far, 1))
    n1, n1_t = _rms_fwd_both("ffn1_norm", x, w["norm_ffn1"])
    carry, names = stages((d2d, 1), (near, 2), (near, 3))
    gate1, landed = _mm_nn("ffn1_gate", n1, full["w1_gate"], BF16, tm=1024, tn=512, tk=2048, carry=carry)
    gathered(names, landed)
    carry, names = stages((far, 2), (far, 3), (near, 4))
    (a1, da_dgate1, da_dup1), landed = _ffn_up("ffn1", n1, full["w1_up"], gate1, carry)
    gathered(names, landed)
    alone("gather_d", (d2d, 2))
    carry, names = stages((d2d, 3), (far, 4), (near, 5), (near, 6))
    h1, landed = _ffn_down("ffn1", a1, full["w1_down"], x, carry)
    gathered(names, landed)
    ffn1 = (n1_t, a1, da_dgate1, da_dup1)
    nm = _rms_fwd("mix_norm", h1, w["norm_mix"])
    carry, names = stages((d2d, 4), (far, 5), (far, 6))
    z, landed = _mm_nn("in_proj", nm, full["w_in"], F32, tm=1024, tn=512, tk=2048, carry=carry)
    gathered(names, landed)

    disc, disc_vjp = jax.vjp(_ssm_discretize, w["ssm_log_dt"][0], w["ssm_a_re"], w["ssm_a_im"], w["ssm_b_re"],
                             w["ssm_b_im"])
    abar_r, abar_i, bbar_r, bbar_i = disc
    nb = n_groups // GROUPS_PER_BLOCK
    wb = jnp.concatenate([_blockdiag_in(bbar_r), _blockdiag_in(bbar_i)], axis=-1).astype(BF16)
    wc = jnp.concatenate([_blockdiag_out(w["ssm_c_re"]), -_blockdiag_out(w["ssm_c_im"])], axis=1).astype(BF16)
    abar = jnp.concatenate([abar_r.reshape(nb, -1), abar_i.reshape(nb, -1)], axis=-1)
    abar_conj = jnp.concatenate([abar_r.reshape(nb, -1), -abar_i.reshape(nb, -1)], axis=-1)
    carry, names = stages((d2d, 5), (d2d, 6), (near, 7))
    (states, y_pre, yg), landed = _ssm_scan_fwd(z, wb, wc, abar, w["ssm_d"], carry)
    gathered(names, landed)
    q = _mm_nn("glu_proj", yg, full["ssm_w_glu"], F32, tm=1024, tn=1024, tk=1024)

    def glu_norm(rows, pars):
        yv = _gelu(rows[0]) * _sigmoid(rows[1])
        yh, _ = _rms_stats(yv)
        return [yh * pars[0]], []

    yn_ssm = _rowwise("ssm_glu_norm", glu_norm, [y_pre, q], [w["norm_ssm_out"]], [(d_ssm, BF16)])[0]

    tril = jnp.tril(jnp.ones((CHUNK, CHUNK), dtype=bool))
    wm = jnp.where(tril[None], w["gmlp_w_s"], 0.0).astype(BF16)
    bias = jnp.repeat(w["gmlp_b_s"].T, GMLP_HEAD, axis=1)
    yn_gmlp = _gmlp_fwd(z, w["gmlp_norm_v"], wm, bias, w["norm_gmlp_out"])
    ycat = jnp.concatenate([yn_ssm, yn_gmlp], axis=1)
    carry, names = stages((far, 7))
    h2, landed = _mm_nn("out_proj", ycat, full["w_out"], F32, res=h1, alpha=1.0, tm=512, tn=1024, tk=2048,
                        carry=carry)
    gathered(names, landed)

    n2, n2_t = _rms_fwd_both("ffn2_norm", h2, w["norm_ffn2"])
    carry, names = stages((d2d, 7))
    (a2, da_dgate2, da_dup2), landed = _ffn_gateup("ffn2", n2, full["w2_gate"], full["w2_up"], carry)
    gathered(names, landed)
    h3 = _ffn_down("ffn2", a2, full["w2_down"], h2)
    ffn2 = (n2_t, a2, da_dgate2, da_dup2)
    npl = _rms_fwd("ple_norm", h3, w["norm_ple"])
    pp = _mm_nn("ple_proj", p, full["w_ple_proj"], F32, tm=1024, tn=1024, tk=2048)
    dgq, dpp, dh4, g_norm_final, loss_part = _ple_head(npl, full["w_ple_gate"], h3, pp, tgt, row(w["norm_final"]))
    reducer = _Reducer(shard_shape, c_arr, qc_arr)
    (g_w_ple_proj,) = _mm_tn("ple_dwproj", p, [dpp], BF16, tm=256, tn=1024, tk=4096)
    (g_w_ple_gate,) = _mm_tn("ple_dwgate", npl, [dgq], BF16, tm=512, tn=1024, tk=4096)
    ple = reducer.swap(["w_ple_gate", "w_ple_proj"], [g_w_ple_gate, g_w_ple_proj])
    dnpl, swapped = _mm_nt_sum("ple_dnorm_in", [dgq], [full["w_ple_gate"]], BF16, tm=512, tn=1024, tk=2048,
                               carry=ple.carry)
    ple = reducer.send(ple, swapped)
    dh3, dh3_b, g_norm_ple = _rms_bwd("ple_dnorm", h3, w["norm_ple"], [dnpl], dh4, 0.5)

    dh2, dh2_b, g_norm_ffn2, got, _, up2_w = _ffn_bwd(
        "ffn2", ["w2_gate", "w2_up", "w2_down"], h2, w["norm_ffn2"], full["w2_gate"], full["w2_up"], full["w2_down"],
        ffn2, dh3, dh3_b, 1.0, reducer, riding=ple.carry, last_hop_later=True)
    reducer.end(ple, got)

    dycat = _mm_nt_sum("out_dproj", [dh2_b], [full["w_out"]], F32, tm=512, tn=1024, tk=2048)
    (g_w_out,) = _mm_tn("out_dw", ycat, [dh2_b], BF16, tm=512, tn=1024, tk=4096)

    dz, g_norm_gmlp_out, g_gmlp_norm_v, g_wm, g_s = _gmlp_bwd(z, dycat, w["gmlp_norm_v"], wm, bias,
                                                                  w["norm_gmlp_out"])
    g_gmlp_w_s = jnp.where(tril[None], g_wm, 0.0)
    g_gmlp_b_s = g_s.reshape(CHUNK, -1, GMLP_HEAD).sum(axis=-1).T

    def glu_bwd(rows, pars):
        dyn, ypre, qv = rows
        ygv = _gelu(ypre)
        sg = _sigmoid(qv)
        dy, dg = _rms_backward(ygv * sg, pars[0], dyn)
        return [dy * ygv * sg * (1.0 - sg), dy * sg], [dg]

    dq, dyg_part, g_norm_ssm_out = _rowwise("ssm_dglu", glu_bwd, [(dycat, d_ssm, 0), y_pre, q], [w["norm_ssm_out"]],
                                            [(d_ssm, BF16), (d_ssm, F32)], [(1, d_ssm)])
    dyg_proj = _mm_nt_sum("glu_dproj", [dq], [full["ssm_w_glu"]], F32, tm=1024, tn=1024, tk=1024)
    (g_ssm_w_glu,) = _mm_tn("glu_dw", yg, [dq], BF16, tm=512, tn=1024, tk=4096)

    def gelu_bwd(rows, pars):
        return [(rows[0] + rows[1]) * _gelu_grad(rows[2])], []

    dy_pre = _rowwise("ssm_dgelu", gelu_bwd, [dyg_part, dyg_proj, y_pre], [], [(d_ssm, F32)])[0]
    mixers = reducer.swap(["w_out", "ssm_w_glu"], [g_w_out, g_ssm_w_glu])
    me = 4 * xi + 2 * yi + ci
    small = {"gmlp_norm_v": g_gmlp_norm_v, "gmlp_w_s": g_gmlp_w_s, "gmlp_b_s": g_gmlp_b_s,
             "norm_gmlp_out": g_norm_gmlp_out, "norm_ssm_out": g_norm_ssm_out, "norm_ffn2": g_norm_ffn2,
             "norm_ple": g_norm_ple, "norm_final": g_norm_final}
    before_scan = [n for n in SMALL if n in small]
    scan_blocks = _place_block(_pack([small[n] for n in before_scan] + [loss_part[:, :1]]), me)
    carry, split = _ride(up2_w.carry, mixers.carry, _exchange_carry(scan_blocks))
    (dz, g_wb, g_wc, g_abar, g_ssm_d), results = _ssm_scan_bwd(z, dy_pre, states, wb, wc, abar_conj, w["ssm_d"],
                                                              dz, carry)
    got, swapped, (scan_blocks,) = split(results)
    reducer.end(up2_w, got)
    mixers = reducer.send(mixers, swapped)
    g_abar = jnp.transpose(g_abar, (1, 0, 2)).reshape(nb, -1)
    sw = g_abar.shape[-1] // 2
    g_bbar_r = _blockdiag_in_grad(g_wb[..., :sw], SSM_STATE, SSM_GROUP)
    g_bbar_i = _blockdiag_in_grad(g_wb[..., sw:], SSM_STATE, SSM_GROUP)
    g_ssm_c_re = _blockdiag_out_grad(g_wc[:, :sw, :], SSM_GROUP, SSM_STATE)
    g_ssm_c_im = -_blockdiag_out_grad(g_wc[:, sw:, :], SSM_GROUP, SSM_STATE)
    g_abar_r = g_abar[..., :sw].reshape(n_groups, SSM_STATE)
    g_abar_i = g_abar[..., sw:].reshape(n_groups, SSM_STATE)
    g_ssm_log_dt, g_ssm_a_re, g_ssm_a_im, g_ssm_b_re, g_ssm_b_im = disc_vjp((g_abar_r, g_abar_i, g_bbar_r, g_bbar_i))

    (g_w_in,), got = _mm_tn("in_dw", nm, [dz], BF16, tm=512, tn=1536, tk=4096, carry=mixers.carry)
    reducer.end(mixers, got)
    in_w = reducer.swap(["w_in"], [g_w_in])
    dnm, swapped = _mm_nt_sum("in_dproj", [dz], [full["w_in"]], BF16, tm=1024, tn=1024, tk=3072, carry=in_w.carry)
    in_w = reducer.send(in_w, swapped)
    dh1, dh1_b, g_norm_mix = _rms_bwd("mix_dnorm", h1, w["norm_mix"], [dnm], dh2, 0.5)
    small = {"norm_mix": g_norm_mix, "ssm_log_dt": g_ssm_log_dt, "ssm_a_re": g_ssm_a_re,
             "ssm_a_im": g_ssm_a_im, "ssm_b_re": g_ssm_b_re, "ssm_b_im": g_ssm_b_im, "ssm_c_re": g_ssm_c_re,
             "ssm_c_im": g_ssm_c_im, "ssm_d": g_ssm_d}
    after_scan = [n for n in SMALL if n in small]
    ffn_blocks = _place_block(_pack([small[n] for n in after_scan]), me)
    done = [n for n in BIG if n in reducer.halves]
    carry, split = _ride(in_w.carry, _share_carry([reducer.halves[n] for n in done], [shard_shape[n] for n in done]))
    dx, _, g_norm_ffn1, results, (ffn_blocks,), _ = _ffn_bwd(
        "ffn1", ["w1_gate", "w1_up", "w1_down"], x, w["norm_ffn1"], full["w1_gate"], full["w1_up"], full["w1_down"],
        ffn1, dh1, dh1_b, 1.0, reducer, riding=carry, riding_dwd=_exchange_carry(ffn_blocks))
    got, shared = split(results)
    grad = dict(zip(done, shared))
    reducer.end(in_w, got)
    scan_grads = _unpack(_sum_blocks("sum_before_scan", scan_blocks), [w[n].shape for n in before_scan] + [(1,)])
    loss = scan_grads[-1].reshape(())
    early = before_scan + after_scan
    early_grads = scan_grads[:-1] + _unpack(_sum_blocks("sum_after_scan", ffn_blocks), [w[n].shape for n in after_scan])

    rest = [n for n in BIG if n not in done]
    late_blocks = _place_block(_pack([g_norm_ffn1]), me)
    last = _join(_share_carry([reducer.halves[n] for n in rest], [shard_shape[n] for n in rest]),
                 _exchange_carry(late_blocks))
    *shared, late_blocks = _comm_call("share_halves", last)
    grad.update(zip(rest, shared))
    grad.update(zip(early, early_grads))
    grad["norm_ffn1"] = _unpack(_sum_blocks("sum_first_norm", late_blocks), [w["norm_ffn1"].shape])[0]

    small_shapes = [w[n].shape for n in SMALL]
    delta, new_m, new_v = {}, {}, {}
    for n in BIG:
        delta[n], new_m[n], new_v[n], grad[n] = _adamw("adamw_" + n, w[n], grad[n], m[n], v[n])
    d_p, m_p, v_p, _ = _adamw("adamw_small", _pack([w[n] for n in SMALL]), _pack([grad[n] for n in SMALL]),
                              _pack([m[n] for n in SMALL]), _pack([v[n] for n in SMALL]))
    for name_list, packed in ((delta, d_p), (new_m, m_p), (new_v, v_p)):
        for n, a in zip(SMALL, _unpack(packed, small_shapes)):
            name_list[n] = a
    return loss, dx, grad, delta, new_m, new_v


def kernel(x, p, norm_ffn1, w1_gate, w1_up, w1_down, norm_mix, w_in, ssm_log_dt, ssm_a_re, ssm_a_im, ssm_b_re, ssm_b_im, ssm_c_re, ssm_c_im, ssm_d, ssm_w_glu, gmlp_norm_v, gmlp_w_s, gmlp_b_s, norm_ssm_out, norm_gmlp_out, w_out, norm_ffn2, w2_gate, w2_up, w2_down, norm_ple, w_ple_gate, w_ple_proj, norm_final, loss_target, m_norm_ffn1, m_w1_gate, m_w1_up, m_w1_down, m_norm_mix, m_w_in, m_ssm_log_dt, m_ssm_a_re, m_ssm_a_im, m_ssm_b_re, m_ssm_b_im, m_ssm_c_re, m_ssm_c_im, m_ssm_d, m_ssm_w_glu, m_gmlp_norm_v, m_gmlp_w_s, m_gmlp_b_s, m_norm_ssm_out, m_norm_gmlp_out, m_w_out, m_norm_ffn2, m_w2_gate, m_w2_up, m_w2_down, m_norm_ple, m_w_ple_gate, m_w_ple_proj, m_norm_final, v_norm_ffn1, v_w1_gate, v_w1_up, v_w1_down, v_norm_mix, v_w_in, v_ssm_log_dt, v_ssm_a_re, v_ssm_a_im, v_ssm_b_re, v_ssm_b_im, v_ssm_c_re, v_ssm_c_im, v_ssm_d, v_ssm_w_glu, v_gmlp_norm_v, v_gmlp_w_s, v_gmlp_b_s, v_norm_ssm_out, v_norm_gmlp_out, v_w_out, v_norm_ffn2, v_w2_gate, v_w2_up, v_w2_down, v_norm_ple, v_w_ple_gate, v_w_ple_proj, v_norm_final):
    given = dict(locals())
    shapes = {n: given[n].shape for n in WEIGHTS}

    def block(name):
        a = given[name]
        if a.ndim == 1:
            return a.reshape(1, -1)
        return a[0] if a.ndim >= 3 else a

    w = {n: block(n) for n in WEIGHTS}
    m = {n: block("m_" + n) for n in WEIGHTS}
    v = {n: block("v_" + n) for n in WEIGHTS}
    loss, dx, grad, delta, new_m, new_v = _step(x[0], p[0, 0], loss_target[0], w, m, v)
    outs = [loss, dx[None]]
    for tree in (grad, delta, new_m, new_v):
        outs += [tree[n].reshape(shapes[n]) for n in WEIGHTS]
    return tuple(outs)
```

```python
import functools
import math

import jax
import jax.numpy as jnp
from jax import lax
from jax.experimental import pallas as pl
from jax.experimental.pallas import tpu as pltpu

F32 = jnp.float32
BF16 = jnp.bfloat16
EPS = 1e-6
SSM_GROUP = 16
SSM_STATE = 64
GROUPS_PER_BLOCK = 8
GMLP_HEAD = 128
CHUNK = 128
ADAM_LR = 0.001
ADAM_B1 = 0.9
ADAM_B2 = 0.999
ADAM_EPS = 1e-08
ADAM_WD = 0.01
ADAM_STEP = 10
N_CHIPS = 4
N_DEV = 8
LANES = 128
VMEM_LIMIT_BYTES = 56 * 1024 * 1024
MESH = pl.DeviceIdType.MESH
GELU_C = math.sqrt(2.0 / math.pi)
GELU_A = 0.044715

_DOT_DIMS = {
    "nn": (((1,), (0,)), ((), ())),
    "nt": (((1,), (1,)), ((), ())),
    "tn": (((0,), (0,)), ((), ())),
}


def _tile(dim, pref, align):
    if dim <= pref:
        return dim
    t = (pref // align) * align
    while t >= align:
        if dim % t == 0:
            return t
        t -= align
    return dim


def _params(semantics):
    return pltpu.CompilerParams(dimension_semantics=semantics, vmem_limit_bytes=VMEM_LIMIT_BYTES)


def _gelu(x):
    return 0.5 * x * (1.0 + jnp.tanh(GELU_C * (x + GELU_A * x * x * x)))


def _gelu_grad(x):
    t = jnp.tanh(GELU_C * (x + GELU_A * x * x * x))
    return 0.5 * (1.0 + t) + 0.5 * x * (1.0 - t * t) * GELU_C * (1.0 + 3.0 * GELU_A * x * x)


def _sigmoid(x):
    return 1.0 / (1.0 + jnp.exp(-x))


def _dot(a, b, mode):
    return lax.dot_general(a.astype(BF16), b.astype(BF16), _DOT_DIMS[mode], preferred_element_type=F32)


class _Carry:
    def __init__(self, arrays, out_shapes, aliases, n_copies, copies):
        self.arrays = list(arrays)
        self.out_shapes = list(out_shapes)
        self.aliases = dict(aliases)
        self.n_copies = n_copies
        self.copies = copies

    def scratch(self):
        return [pltpu.SemaphoreType.DMA((self.n_copies,)), pltpu.SemaphoreType.DMA((self.n_copies,))]

    def split(self, refs):
        n_in, n_out = len(self.arrays), len(self.out_shapes)
        return refs[:n_in], refs[n_in:n_in + n_out], refs[n_in + n_out], refs[n_in + n_out + 1]

    def start(self, refs):
        for cp in self.copies(*self.split(refs)):
            cp.start()

    def wait(self, refs):
        for cp in self.copies(*self.split(refs)):
            cp.wait()


class _SemRange:
    def __init__(self, sems, offset):
        self.sems, self.offset = sems, offset

    @property
    def at(self):
        return self

    def __getitem__(self, k):
        return self.sems.at[self.offset + k]


def _join(first, second):
    n_in, n_out = len(first.arrays), len(first.out_shapes)
    aliases = dict(first.aliases)
    aliases.update({n_in + i: n_out + o for i, o in second.aliases.items()})

    def copies(ops, res, send_sems, recv_sems):
        return (first.copies(ops[:n_in], res[:n_out], send_sems, recv_sems)
                + second.copies(ops[n_in:], res[n_out:], _SemRange(send_sems, first.n_copies),
                                _SemRange(recv_sems, first.n_copies)))

    return _Carry(first.arrays + second.arrays, first.out_shapes + second.out_shapes, aliases,
                  first.n_copies + second.n_copies, copies)


_ANY = pl.BlockSpec(memory_space=pl.ANY)


def _comm_call(name, carry):
    def body(*refs):
        carry.start(refs)
        carry.wait(refs)

    n_in = len(carry.arrays)
    return pl.pallas_call(
        body,
        name=name,
        in_specs=[_ANY] * n_in,
        out_specs=[_ANY] * len(carry.out_shapes),
        out_shape=carry.out_shapes,
        input_output_aliases=carry.aliases,
        scratch_shapes=carry.scratch(),
    )(*carry.arrays)


def _carried_call(body, carry, *, name, grid, in_specs, out_specs, out_shape, scratch_shapes, semantics, args,
                  aliases=None):
    aliases = dict(aliases or {})
    if carry is None:
        res = pl.pallas_call(body, name=name, grid=grid, in_specs=in_specs, out_specs=out_specs, out_shape=out_shape,
                             scratch_shapes=scratch_shapes, input_output_aliases=aliases,
                             compiler_params=_params(semantics))(*args)
        return res, []
    n_in, n_out, n_scr = len(in_specs), len(out_specs), len(scratch_shapes)
    nci, nco = len(carry.arrays), len(carry.out_shapes)

    def wrapped(*refs):
        ins = refs[:n_in]
        outs = refs[n_in + nci:n_in + nci + n_out]
        scr = refs[n_in + nci + n_out + nco:n_in + nci + n_out + nco + n_scr]
        c_refs = (refs[n_in:n_in + nci] + refs[n_in + nci + n_out:n_in + nci + n_out + nco]
                  + refs[n_in + nci + n_out + nco + n_scr:])
        first = functools.reduce(jnp.logical_and, [pl.program_id(d) == 0 for d in range(len(grid))])
        last = functools.reduce(jnp.logical_and, [pl.program_id(d) == grid[d] - 1 for d in range(len(grid))])

        @pl.when(first)
        def _():
            carry.start(c_refs)

        body(*ins, *outs, *scr)

        @pl.when(last)
        def _():
            carry.wait(c_refs)

    res = pl.pallas_call(
        wrapped,
        name=name,
        grid=grid,
        in_specs=list(in_specs) + [_ANY] * nci,
        out_specs=list(out_specs) + [_ANY] * nco,
        out_shape=list(out_shape) + carry.out_shapes,
        input_output_aliases={**aliases, **{n_in + i: n_out + o for i, o in carry.aliases.items()}},
        scratch_shapes=list(scratch_shapes) + carry.scratch(),
        compiler_params=_params(("arbitrary",) * len(grid)),
    )(*args, *carry.arrays)
    return res[:n_out], res[n_out:]


def _matmul(name, mode, a_list, b_list, products, out_dtypes, epilogue, extras=(), tm=512, tn=512, tk=2048,
            carry=None, n_part=(0, 1)):
    a0, b0 = a_list[0], b_list[0]
    if mode == "tn":
        k_dim, m_dim = a0.shape
    else:
        m_dim, k_dim = a0.shape
    n_dim = (b0.shape[0] if mode == "nt" else b0.shape[1]) // n_part[1]
    tm = _tile(m_dim, tm, LANES)
    tn = _tile(n_dim, tn, LANES)
    tk = _tile(k_dim, tk, LANES)
    nk = k_dim // tk
    j0 = n_part[0] * (n_dim // tn)
    chunk = 2 * LANES if (nk == 1 and epilogue is not _identity and tn % (2 * LANES) == 0) else tn
    n_acc = 1 + max(p[2] for p in products)
    na, nb, ne, no = len(a_list), len(b_list), len(extras), len(out_dtypes)

    if mode == "tn":
        a_spec = pl.BlockSpec((tk, tm), lambda i, j, k: (k, i))
    else:
        a_spec = pl.BlockSpec((tm, tk), lambda i, j, k: (i, k))
    if mode == "nt":
        b_spec = pl.BlockSpec((tn, tk), lambda i, j, k: (j0 + j, k))
    else:
        b_spec = pl.BlockSpec((tk, tn), lambda i, j, k: (k, j0 + j))
    t_spec = pl.BlockSpec((tm, tn), lambda i, j, k: (i, j))

    def body(*refs):
        a_refs = refs[:na]
        b_refs = refs[na:na + nb]
        e_refs = refs[na + nb:na + nb + ne]
        o_refs = refs[na + nb + ne:na + nb + ne + no]
        acc_refs = refs[na + nb + ne + no:]

        def partial_sums(cols):
            sums = [None] * n_acc
            for ai, bi, ci in products:
                b = b_refs[bi][cols, :] if mode == "nt" else b_refs[bi][:, cols]
                d = _dot(a_refs[ai][...], b, mode)
                sums[ci] = d if sums[ci] is None else sums[ci] + d
            return sums

        def finish(accs, cols):
            outs = epilogue(accs, [e[:, cols] for e in e_refs])
            for o_ref, o in zip(o_refs, outs):
                o_ref[:, cols] = o.astype(o_ref.dtype)

        if nk == 1:
            for c0 in range(0, tn, chunk):
                finish(partial_sums(slice(c0, c0 + chunk)), slice(c0, c0 + chunk))
        else:
            sums = partial_sums(slice(None))
            finish = functools.partial(finish, cols=slice(None))
            k = pl.program_id(2)

            @pl.when(k == 0)
            def _():
                for acc, s in zip(acc_refs, sums):
                    acc[...] = s

            @pl.when(k > 0)
            def _():
                for acc, s in zip(acc_refs, sums):
                    acc[...] += s

            @pl.when(k == nk - 1)
            def _():
                finish([acc[...] for acc in acc_refs])

    scratch = [pltpu.VMEM((tm, tn), F32) for _ in range(n_acc)] if nk > 1 else []
    outs, carried = _carried_call(
        body, carry,
        name=name,
        grid=(m_dim // tm, n_dim // tn, nk),
        in_specs=[a_spec] * na + [b_spec] * nb + [t_spec] * ne,
        out_specs=[t_spec] * no,
        out_shape=[jax.ShapeDtypeStruct((m_dim, n_dim), dt) for dt in out_dtypes],
        scratch_shapes=scratch,
        semantics=("parallel", "parallel", "arbitrary"),
        args=[*a_list, *b_list, *extras],
    )
    return (outs, carried) if carry else outs


def _identity(accs, extras):
    return accs


def _single(result, carry):
    return (result[0][0], result[1]) if carry else result[0]


def _mm_nn(name, a, b, out_dtype, res=None, alpha=1.0, carry=None, **tiles):
    if res is None:
        return _single(_matmul(name, "nn", [a], [b], [(0, 0, 0)], [out_dtype], _identity, carry=carry, **tiles), carry)

    def epilogue(accs, extras):
        return [extras[0] + alpha * accs[0]]

    return _single(_matmul(name, "nn", [a], [b], [(0, 0, 0)], [out_dtype], epilogue, extras=(res,), carry=carry,
                           **tiles), carry)


def _mm_nt_sum(name, a_list, b_list, out_dtype, carry=None, n_part=(0, 1), **tiles):
    products = [(i, i, 0) for i in range(len(a_list))]
    return _single(_matmul(name, "nt", a_list, b_list, products, [out_dtype], _identity, carry=carry, n_part=n_part,
                           **tiles), carry)


def _mm_tn(name, a, b_list, out_dtype, carry=None, **tiles):
    products = [(0, i, i) for i in range(len(b_list))]
    return _matmul(name, "tn", [a], b_list, products, [out_dtype] * len(b_list), _identity, carry=carry, **tiles)


def _rowwise(name, fn, row_ins, par_ins, row_outs, acc_outs=(), tr=512):
    first = row_ins[0][0] if isinstance(row_ins[0], tuple) else row_ins[0]
    t_dim = first.shape[0]
    tr = _tile(t_dim, tr, 16)
    arrays, specs = [], []
    for r in row_ins:
        if isinstance(r, tuple):
            arr, width, blk = r
            specs.append(pl.BlockSpec((tr, width), lambda i, blk=blk: (i, blk)))
        else:
            arr = r
            specs.append(pl.BlockSpec((tr, arr.shape[1]), lambda i: (i, 0)))
        arrays.append(arr)
    for p in par_ins:
        arrays.append(p)
        specs.append(pl.BlockSpec(p.shape, lambda i, nd=p.ndim: (0,) * nd))
    nr, npar, nro, nacc = len(row_ins), len(par_ins), len(row_outs), len(acc_outs)

    def body(*refs):
        rows = [r[...] for r in refs[:nr]]
        pars = [p[...] for p in refs[nr:nr + npar]]
        o_refs = refs[nr + npar:nr + npar + nro]
        acc_refs = refs[nr + npar + nro:]
        outs, accs = fn(rows, pars)
        for o_ref, o in zip(o_refs, outs):
            o_ref[...] = o.astype(o_ref.dtype)
        if nacc:
            @pl.when(pl.program_id(0) == 0)
            def _():
                for a_ref in acc_refs:
                    a_ref[...] = jnp.zeros_like(a_ref)

            for a_ref, a in zip(acc_refs, accs):
                a_ref[...] += a

    out_shape = [jax.ShapeDtypeStruct((t_dim, c), dt) for c, dt in row_outs]
    out_shape += [jax.ShapeDtypeStruct(s, F32) for s in acc_outs]
    out_specs = [pl.BlockSpec((tr, c), lambda i: (i, 0)) for c, _ in row_outs]
    out_specs += [pl.BlockSpec(s, lambda i: (0, 0)) for s in acc_outs]
    return pl.pallas_call(
        body,
        name=name,
        grid=(t_dim // tr,),
        in_specs=specs,
        out_specs=out_specs,
        out_shape=out_shape,
        compiler_params=_params(("arbitrary",)),
    )(*arrays)


def _rms_stats(x):
    r = lax.rsqrt(jnp.mean(x * x, axis=-1, keepdims=True) + EPS)
    return x * r, r


def _rms_backward(x, g, dy):
    xh, r = _rms_stats(x)
    a = dy * g
    dx = r * (a - xh * jnp.mean(a * xh, axis=-1, keepdims=True))
    return dx, jnp.sum(dy * xh, axis=0, keepdims=True)


def _rms_fwd(name, x, g):
    def fn(rows, pars):
        xh, _ = _rms_stats(rows[0])
        return [xh * pars[0]], []

    return _rowwise(name, fn, [x], [g], [(x.shape[1], BF16)])[0]


def _rms_bwd(name, x, g, dy_parts, dres, scale):
    def fn(rows, pars):
        dy = (rows[2] if len(rows) == 3 else jnp.concatenate(rows[2:], axis=1)).astype(F32)
        dx, dg = _rms_backward(rows[0], pars[0], dy)
        tot = rows[1] + dx
        return [tot, scale * tot], [dg]

    d = x.shape[1]
    return _rowwise(name, fn, [x, dres, *dy_parts], [g], [(d, F32), (d, BF16)], [(1, d)], tr=256)


def _cast_into_gathered(name, w, kind, q_arr):
    rs, cs = w.shape
    tr = _tile(rs, 256, 16)
    nr = rs // tr

    def body(q_ref, w_ref, o_ref):
        o_ref[...] = w_ref[...].astype(BF16)

    if kind == "row":
        o_spec = pl.BlockSpec((tr, cs), lambda i, q_ref: (q_ref[0] * nr + i, 0))
    else:
        o_spec = pl.BlockSpec((tr, cs), lambda i, q_ref: (i, q_ref[0]))
    return pl.pallas_call(
        body,
        name=name,
        grid_spec=pltpu.PrefetchScalarGridSpec(num_scalar_prefetch=1, grid=(nr,),
                                               in_specs=[pl.BlockSpec((tr, cs), lambda i, q_ref: (i, 0))],
                                               out_specs=o_spec),
        out_shape=jax.ShapeDtypeStruct(_full_shape(kind, (rs, cs)), BF16),
        compiler_params=_params(("parallel",)),
    )(q_arr, w)


def _swiglu_tiles(gate, up):
    s = _sigmoid(gate)
    silu = gate * s
    return [silu * up, up * (s * (1.0 + gate * (1.0 - s))), silu]


def _ffn_gateup(tag, n, wg, wu, carry):
    def act(accs, extras):
        return _swiglu_tiles(accs[0], accs[1])

    return _matmul(tag + "_gateup", "nn", [n], [wg, wu], [(0, 0, 0), (0, 1, 1)], [BF16] * 3, act,
                   tm=1024, tn=512, tk=2048, carry=carry)


def _ffn_up(tag, n, wu, gate, carry):
    def act(accs, extras):
        return _swiglu_tiles(extras[0].astype(F32), accs[0])

    return _matmul(tag + "_up", "nn", [n], [wu], [(0, 0, 0)], [BF16] * 3, act, extras=(gate,),
                   tm=1024, tn=512, tk=2048, carry=carry)


def _ffn_down(tag, a, wd, h, carry=None):
    return _mm_nn(tag + "_down", a, wd, F32, res=h, alpha=0.5, tm=1024, tn=512, tk=5632, carry=carry)


def _ffn_bwd(tag, names, h, g, wg, wu, wd, saved, dh, dfb, next_scale, reducer, riding=None, riding_dwd=None,
             last_hop_later=False):
    n, a, da_dgate, da_dup = saved

    def act_bwd(accs, extras):
        return [accs[0] * extras[0].astype(F32), accs[0] * extras[1].astype(F32)]

    dact = _matmul(tag + "_dact", "nt", [dfb], [wd], [(0, 0, 0)], [BF16, BF16], act_bwd, extras=(da_dgate, da_dup),
                   tm=1024, tn=512, tk=2048, carry=riding)
    (dgp, du), rode = dact if riding else (dact, [])
    dwd_call = _mm_tn(tag + "_dwd", a, [dfb], BF16, tm=512, tn=2048, tk=4096, carry=riding_dwd)
    (dwd,), rode_dwd = dwd_call if riding_dwd else (dwd_call, [])
    down = reducer.swap(names[2:], [dwd])
    (dwg,), swapped = _mm_tn(tag + "_dwg", n, [dgp], BF16, tm=512, tn=1408, tk=4096, carry=down.carry)
    down = reducer.send(down, swapped)
    gate_w = reducer.swap(names[:1], [dwg])
    carry, split = _ride(down.carry, gate_w.carry)
    (dwu,), results = _mm_tn(tag + "_dwu", n, [du], BF16, tm=512, tn=1408, tk=4096, carry=carry)
    got, swapped = split(results)
    reducer.end(down, got)
    gate_w = reducer.send(gate_w, swapped)
    up_w = reducer.swap(names[1:2], [dwu])
    carry, split = _ride(gate_w.carry, up_w.carry)
    halves = 1 if last_hop_later else 2
    tiles = dict(tm=512, tn=512, tk=5632) if last_hop_later else dict(tm=1024, tn=1024, tk=1408)
    dn_lo, results = _mm_nt_sum(tag + "_dn_lo", [dgp, du], [wg, wu], BF16, carry=carry, n_part=(0, halves), **tiles)
    got, swapped = split(results)
    reducer.end(gate_w, got)
    up_w = reducer.send(up_w, swapped)
    dn = [dn_lo]
    if not last_hop_later:
        dn_hi, got = _mm_nt_sum(tag + "_dn_hi", [dgp, du], [wg, wu], BF16, tm=1024, tn=1024, tk=1408,
                                carry=up_w.carry, n_part=(1, 2))
        reducer.end(up_w, got)
        dn.append(dn_hi)
    dh_in, dh_in_b, dg = _rms_bwd(tag + "_dnorm", h, g, dn, dh, next_scale)
    return dh_in, dh_in_b, dg, rode, rode_dwd, (up_w if last_hop_later else None)


def _ssm_discretize(log_dt, a_re, a_im, b_re, b_im):
    dt = jnp.exp(log_dt)[:, None]
    lr = jnp.minimum(a_re, -1e-4)
    li = a_im
    mag = jnp.exp(lr * dt)
    ang = li * dt
    abar_r = mag * jnp.cos(ang)
    abar_i = mag * jnp.sin(ang)
    den = lr * lr + li * li
    xr = abar_r - 1.0
    xi = abar_i
    zr = (xr * lr + xi * li) / den
    zi = (xi * lr - xr * li) / den
    bbar_r = zr[..., None] * b_re - zi[..., None] * b_im
    bbar_i = zr[..., None] * b_im + zi[..., None] * b_re
    return abar_r, abar_i, bbar_r, bbar_i


def _blockdiag_in(b):
    g, n, p = b.shape
    nb = g // GROUPS_PER_BLOCK
    eye = jnp.eye(GROUPS_PER_BLOCK, dtype=b.dtype)
    b4 = b.reshape(nb, GROUPS_PER_BLOCK, n, p)
    return jnp.einsum("sgnp,gh->sgphn", b4, eye).reshape(nb, GROUPS_PER_BLOCK * p, GROUPS_PER_BLOCK * n)


def _blockdiag_in_grad(gw, n, p):
    nb = gw.shape[0]
    eye = jnp.eye(GROUPS_PER_BLOCK, dtype=gw.dtype)
    g5 = gw.reshape(nb, GROUPS_PER_BLOCK, p, GROUPS_PER_BLOCK, n)
    return jnp.einsum("sgphn,gh->sgnp", g5, eye).reshape(nb * GROUPS_PER_BLOCK, n, p)


def _blockdiag_out(c):
    g, p, n = c.shape
    nb = g // GROUPS_PER_BLOCK
    eye = jnp.eye(GROUPS_PER_BLOCK, dtype=c.dtype)
    c4 = c.reshape(nb, GROUPS_PER_BLOCK, p, n)
    return jnp.einsum("sgpn,gh->shngp", c4, eye).reshape(nb, GROUPS_PER_BLOCK * n, GROUPS_PER_BLOCK * p)


def _blockdiag_out_grad(gw, p, n):
    nb = gw.shape[0]
    eye = jnp.eye(GROUPS_PER_BLOCK, dtype=gw.dtype)
    g5 = gw.reshape(nb, GROUPS_PER_BLOCK, n, GROUPS_PER_BLOCK, p)
    return jnp.einsum("shngp,gh->sgpn", g5, eye).reshape(nb * GROUPS_PER_BLOCK, p, n)


def _ssm_scan_fwd(z, wb, wc, abar, d, carry=None):
    t_dim = z.shape[0]
    nb, cb, sw2 = wb.shape
    nl = sw2 // LANES
    hl = nl // 2
    tt = _tile(t_dim, 256, 8)
    nt = t_dim // tt

    def body(z_ref, wb_ref, wc_ref, a_ref, d_ref, s_ref, y_ref, yg_ref, drive_ref, st_ref):
        @pl.when(pl.program_id(0) == 0)
        def _():
            st_ref[...] = jnp.zeros_like(st_ref)

        u = z_ref[...]
        ub = u.astype(BF16)
        for b in range(nb):
            drive = _dot(ub[:, b * cb:(b + 1) * cb], wb_ref[b], "nn")
            for l in range(nl):
                drive_ref[l, pl.ds(b, tt, stride=nb), :] = drive[:, l * LANES:(l + 1) * LANES]
        a = a_ref[...]
        chunk = lambda v, l: v[:, l * LANES:(l + 1) * LANES]

        def step(t, state):
            rows = pl.ds(pl.multiple_of(t * nb, nb), nb)
            re, im = [], []
            for l in range(hl):
                ar, ai, sr, si = chunk(a, l), chunk(a, hl + l), state[l], state[hl + l]
                nr = ar * sr - ai * si + drive_ref[l, rows, :]
                ni = ar * si + ai * sr + drive_ref[hl + l, rows, :]
                s_ref[l, rows, :] = nr
                s_ref[hl + l, rows, :] = ni
                re.append(nr)
                im.append(ni)
            return tuple(re + im)

        state = lax.fori_loop(0, tt, step, tuple(st_ref[l] for l in range(nl)), unroll=8)
        for l in range(nl):
            st_ref[l] = state[l]
        parts = []
        for b in range(nb):
            s_b = jnp.concatenate([s_ref[l, pl.ds(b, tt, stride=nb), :] for l in range(nl)], axis=1)
            parts.append(_dot(s_b, wc_ref[b], "nn"))
        y = jnp.concatenate(parts, axis=1) + d_ref[...] * u
        y_ref[...] = y
        yg_ref[...] = _gelu(y).astype(BF16)

    full = lambda a: pl.BlockSpec(a.shape, lambda t, nd=a.ndim: (0,) * nd)
    return _carried_call(
        body, carry,
        name="ssm_scan_fwd",
        grid=(nt,),
        in_specs=[pl.BlockSpec((tt, nb * cb), lambda t: (t, 0)), full(wb), full(wc), full(abar), full(d)],
        out_specs=[
            pl.BlockSpec((nl, tt * nb, LANES), lambda t: (0, t, 0)),
            pl.BlockSpec((tt, nb * cb), lambda t: (t, 0)),
            pl.BlockSpec((tt, nb * cb), lambda t: (t, 0)),
        ],
        out_shape=[
            jax.ShapeDtypeStruct((nl, t_dim * nb, LANES), F32),
            jax.ShapeDtypeStruct((t_dim, nb * cb), F32),
            jax.ShapeDtypeStruct((t_dim, nb * cb), BF16),
        ],
        scratch_shapes=[pltpu.VMEM((nl, tt * nb, LANES), F32), pltpu.VMEM((nl, nb, LANES), F32)],
        semantics=("arbitrary",),
        args=[z, wb, wc, abar, d],
    )


def _ssm_scan_bwd(z, dy, states, wb, wc, abar_conj, d, dz_all, carry=None):
    t_dim = z.shape[0]
    nb, cb, sw2 = wb.shape
    nl = sw2 // LANES
    hl = nl // 2
    tt = _tile(t_dim, 256, 8)
    nt = t_dim // tt
    edges = states.reshape(nl, nt, tt * nb, LANES)[:, :, (tt - 1) * nb:, :]
    before = jnp.concatenate([jnp.zeros((nl, 1, nb, LANES), F32), edges[:, :-1]], axis=1).reshape(nl, nt * nb, LANES)

    def body(z_ref, dy_ref, s_ref, sp_ref, wb_ref, wc_ref, a_ref, d_ref, dz_all_ref,
             dz_ref, gwb_ref, gwc_ref, ga_ref, gd_ref, gin_ref, gs_ref, st_ref):
        @pl.when(pl.program_id(0) == 0)
        def _():
            st_ref[...] = jnp.zeros_like(st_ref)
            gwb_ref[...] = jnp.zeros_like(gwb_ref)
            gwc_ref[...] = jnp.zeros_like(gwc_ref)
            ga_ref[...] = jnp.zeros_like(ga_ref)
            gd_ref[...] = jnp.zeros_like(gd_ref)

        u = z_ref[...]
        dyv = dy_ref[...]
        ub = u.astype(BF16)
        dyb = dyv.astype(BF16)
        for b in range(nb):
            gin = _dot(dyb[:, b * cb:(b + 1) * cb], wc_ref[b], "nt")
            for l in range(nl):
                gin_ref[l, pl.ds(b, tt, stride=nb), :] = gin[:, l * LANES:(l + 1) * LANES]
        a = a_ref[...]
        chunk = lambda v, l: v[:, l * LANES:(l + 1) * LANES]

        def step(k, state):
            rows = pl.ds(pl.multiple_of((tt - 1 - k) * nb, nb), nb)
            re, im = [], []
            for l in range(hl):
                ar, ai, gr, gi = chunk(a, l), chunk(a, hl + l), state[l], state[hl + l]
                nr = ar * gr - ai * gi + gin_ref[l, rows, :]
                ni = ar * gi + ai * gr + gin_ref[hl + l, rows, :]
                gs_ref[l, rows, :] = nr
                gs_ref[hl + l, rows, :] = ni
                re.append(nr)
                im.append(ni)
            return tuple(re + im)

        state = lax.fori_loop(0, tt, step, tuple(st_ref[l] for l in range(nl)), unroll=8)
        for l in range(nl):
            st_ref[l] = state[l]

        parts = []
        for b in range(nb):
            cols = slice(b * cb, (b + 1) * cb)
            gs_b = jnp.concatenate([gs_ref[l, pl.ds(b, tt, stride=nb), :] for l in range(nl)], axis=1)
            s_b = jnp.concatenate([s_ref[l, pl.ds(b, tt, stride=nb), :] for l in range(nl)], axis=1)
            parts.append(_dot(gs_b, wb_ref[b], "nt"))
            gwb_ref[b] += _dot(ub[:, cols], gs_b, "tn")
            gwc_ref[b] += _dot(s_b, dyb[:, cols], "tn")
        dz_ref[...] = (jnp.concatenate(parts, axis=1) + d_ref[...] * dyv).astype(BF16)
        gd_ref[...] += jnp.sum(dyv * u, axis=0, keepdims=True)

        row = lax.broadcasted_iota(jnp.int32, (tt * nb, LANES), 0)
        shifted = lambda v: jnp.where(row < nb, 0.0, pltpu.roll(v, nb, 0))
        over_time = lambda v: jnp.sum(v.reshape(tt, nb, LANES), axis=0)
        for l in range(hl):
            g_r, g_i = gs_ref[l], gs_ref[hl + l]
            p_r, p_i = shifted(s_ref[l]), shifted(s_ref[hl + l])
            f_r, f_i = sp_ref[l], sp_ref[hl + l]
            g0_r, g0_i = gs_ref[l, pl.ds(0, nb), :], gs_ref[hl + l, pl.ds(0, nb), :]
            ga_ref[l] += over_time(g_r * p_r + g_i * p_i) + g0_r * f_r + g0_i * f_i
            ga_ref[hl + l] += over_time(g_i * p_r - g_r * p_i) + g0_i * f_r - g0_r * f_i

    rev = lambda t: (nt - 1 - t, 0)
    rev3 = lambda t: (0, nt - 1 - t, 0)
    full = lambda a: pl.BlockSpec(a.shape, lambda t, nd=a.ndim: (0,) * nd)
    return _carried_call(
        body, carry,
        name="ssm_scan_bwd",
        grid=(nt,),
        in_specs=[
            pl.BlockSpec((tt, nb * cb), rev),
            pl.BlockSpec((tt, nb * cb), rev),
            pl.BlockSpec((nl, tt * nb, LANES), rev3),
            pl.BlockSpec((nl, nb, LANES), rev3),
            full(wb), full(wc), full(abar_conj), full(d), _ANY,
        ],
        out_specs=[
            pl.BlockSpec((tt, nb * cb), rev),
            pl.BlockSpec((nb, cb, sw2), lambda t: (0, 0, 0)),
            pl.BlockSpec((nb, sw2, cb), lambda t: (0, 0, 0)),
            pl.BlockSpec((nl, nb, LANES), lambda t: (0, 0, 0)),
            pl.BlockSpec((1, nb * cb), lambda t: (0, 0)),
        ],
        out_shape=[
            jax.ShapeDtypeStruct(dz_all.shape, BF16),
            jax.ShapeDtypeStruct((nb, cb, sw2), F32),
            jax.ShapeDtypeStruct((nb, sw2, cb), F32),
            jax.ShapeDtypeStruct((nl, nb, LANES), F32),
            jax.ShapeDtypeStruct((1, nb * cb), F32),
        ],
        scratch_shapes=[pltpu.VMEM((nl, tt * nb, LANES), F32), pltpu.VMEM((nl, tt * nb, LANES), F32),
                        pltpu.VMEM((nl, nb, LANES), F32)],
        semantics=("arbitrary",),
        args=[z, dy, states, before, wb, wc, abar_conj, d, dz_all],
        aliases={8: 0},
    )


def _gmlp_chunk(zu, zv, gv, wm_ref, bias, n_heads):
    ua = _gelu(zu)
    vg = _gelu(zv)
    xc = vg - jnp.mean(vg, axis=-1, keepdims=True)
    r = lax.rsqrt(jnp.mean(xc * xc, axis=-1, keepdims=True) + EPS)
    vh = xc * r
    vb = (vh * gv).astype(BF16)
    parts = []
    for h in range(n_heads):
        cols = slice(h * GMLP_HEAD, (h + 1) * GMLP_HEAD)
        parts.append(_dot(wm_ref[h], vb[:, cols], "nn"))
    s = jnp.concatenate(parts, axis=1) + bias
    return ua, vh, r, vb, s


def _gmlp_fwd(z, gv, wm, bias, ggo):
    t_dim = z.shape[0]
    dg = gv.shape[1]
    n_heads = dg // GMLP_HEAD
    tr = _tile(t_dim, 256, CHUNK)

    def body(zu_ref, zv_ref, gv_ref, wm_ref, b_ref, ggo_ref, o_ref):
        for ck in range(tr // CHUNK):
            rows = pl.ds(ck * CHUNK, CHUNK)
            ua, _, _, _, s = _gmlp_chunk(zu_ref[rows, :], zv_ref[rows, :], gv_ref[...], wm_ref, b_ref[...], n_heads)
            yh, _ = _rms_stats(ua * s)
            o_ref[rows, :] = (yh * ggo_ref[...]).astype(BF16)

    full = lambda a: pl.BlockSpec(a.shape, lambda i, nd=a.ndim: (0,) * nd)
    return pl.pallas_call(
        body,
        name="gmlp_fwd",
        grid=(t_dim // tr,),
        in_specs=[pl.BlockSpec((tr, dg), lambda i: (i, 1)), pl.BlockSpec((tr, dg), lambda i: (i, 2)),
                  full(gv), full(wm), full(bias), full(ggo)],
        out_specs=pl.BlockSpec((tr, dg), lambda i: (i, 0)),
        out_shape=jax.ShapeDtypeStruct((t_dim, dg), BF16),
        compiler_params=_params(("parallel",)),
    )(z, z, gv, wm, bias, ggo)


def _gmlp_bwd(z, dycat, gv, wm, bias, ggo):
    t_dim = z.shape[0]
    dg = gv.shape[1]
    n_heads = dg // GMLP_HEAD
    tr = _tile(t_dim, 256, CHUNK)

    def body(zu_ref, zv_ref, dy_ref, gv_ref, wm_ref, b_ref, ggo_ref,
             dz_ref, dggo_ref, dgv_ref, dwm_ref, dsum_ref):
        @pl.when(pl.program_id(0) == 0)
        def _():
            dggo_ref[...] = jnp.zeros_like(dggo_ref)
            dgv_ref[...] = jnp.zeros_like(dgv_ref)
            dwm_ref[...] = jnp.zeros_like(dwm_ref)
            dsum_ref[...] = jnp.zeros_like(dsum_ref)

        for ck in range(tr // CHUNK):
            rows = pl.ds(ck * CHUNK, CHUNK)
            zu = zu_ref[rows, :]
            zv = zv_ref[rows, :]
            gvv = gv_ref[...]
            ua, vh, r, vb, s = _gmlp_chunk(zu, zv, gvv, wm_ref, b_ref[...], n_heads)
            dy, dggo = _rms_backward(ua * s, ggo_ref[...], dy_ref[rows, :].astype(F32))
            dggo_ref[...] += dggo
            ds = dy * ua
            dsum_ref[...] += ds
            dsb = ds.astype(BF16)
            parts = []
            for h in range(n_heads):
                cols = slice(h * GMLP_HEAD, (h + 1) * GMLP_HEAD)
                dwm_ref[h] += _dot(dsb[:, cols], vb[:, cols], "nt")
                parts.append(_dot(wm_ref[h], dsb[:, cols], "tn"))
            dv = jnp.concatenate(parts, axis=1)
            dgv_ref[...] += jnp.sum(dv * vh, axis=0, keepdims=True)
            dvh = dv * gvv
            dvg = r * (dvh - jnp.mean(dvh, axis=-1, keepdims=True) - vh * jnp.mean(dvh * vh, axis=-1, keepdims=True))
            dz_ref[rows, pl.ds(2 * dg, dg)] = (dvg * _gelu_grad(zv)).astype(BF16)
            dz_ref[rows, pl.ds(dg, dg)] = (dy * s * _gelu_grad(zu)).astype(BF16)

    full = lambda a: pl.BlockSpec(a.shape, lambda i, nd=a.ndim: (0,) * nd)
    return pl.pallas_call(
        body,
        name="gmlp_bwd",
        grid=(t_dim // tr,),
        in_specs=[pl.BlockSpec((tr, dg), lambda i: (i, 1)), pl.BlockSpec((tr, dg), lambda i: (i, 2)),
                  pl.BlockSpec((tr, dg), lambda i: (i, 1)), full(gv), full(wm), full(bias), full(ggo)],
        out_specs=[pl.BlockSpec((tr, 3 * dg), lambda i: (i, 0)),
                   pl.BlockSpec((1, dg), lambda i: (0, 0)), pl.BlockSpec((1, dg), lambda i: (0, 0)),
                   pl.BlockSpec(wm.shape, lambda i: (0, 0, 0)), pl.BlockSpec((CHUNK, dg), lambda i: (0, 0))],
        out_shape=[jax.ShapeDtypeStruct((t_dim, 3 * dg), BF16),
                   jax.ShapeDtypeStruct((1, dg), F32), jax.ShapeDtypeStruct((1, dg), F32),
                   jax.ShapeDtypeStruct(wm.shape, F32), jax.ShapeDtypeStruct((CHUNK, dg), F32)],
        compiler_params=_params(("arbitrary",)),
    )(z, z, dycat, gv, wm, bias, ggo)


def _ple_head(npl, w_gate, h3, pp, tgt, g_final):
    t_dim, d = h3.shape
    tr = _tile(t_dim, 256, 16)

    def body(n_ref, w_ref, h_ref, pp_ref, t_ref, g_ref, dgq_ref, dpp_ref, dh_ref, dg_ref, loss_ref):
        @pl.when(pl.program_id(0) == 0)
        def _():
            dg_ref[...] = jnp.zeros_like(dg_ref)
            loss_ref[...] = jnp.zeros_like(loss_ref)

        gate = _sigmoid(_dot(n_ref[...], w_ref[...], "nn"))
        ppv = pp_ref[...]
        h4 = h_ref[...] + gate * ppv
        xh, _ = _rms_stats(h4)
        err = xh * g_ref[...] - t_ref[...]
        dh4, dg = _rms_backward(h4, g_ref[...], err * (1.0 / d))
        dh_ref[...] = dh4
        dgq_ref[...] = (dh4 * ppv * gate * (1.0 - gate)).astype(BF16)
        dpp_ref[...] = (dh4 * gate).astype(BF16)
        dg_ref[...] += dg
        loss_ref[...] += jnp.full((1, LANES), 0.5 * jnp.sum(err * err) * (1.0 / d), F32)

    rows = pl.BlockSpec((tr, d), lambda i: (i, 0))
    whole = lambda a: pl.BlockSpec(a.shape, lambda i: (0, 0))
    return pl.pallas_call(
        body,
        name="ple_head",
        grid=(t_dim // tr,),
        in_specs=[rows, whole(w_gate), rows, rows, rows, whole(g_final)],
        out_specs=[rows, rows, rows, pl.BlockSpec((1, d), lambda i: (0, 0)), pl.BlockSpec((1, LANES), lambda i: (0, 0))],
        out_shape=[jax.ShapeDtypeStruct((t_dim, d), BF16), jax.ShapeDtypeStruct((t_dim, d), BF16),
                   jax.ShapeDtypeStruct((t_dim, d), F32), jax.ShapeDtypeStruct((1, d), F32),
                   jax.ShapeDtypeStruct((1, LANES), F32)],
        compiler_params=_params(("arbitrary",)),
    )(npl, w_gate, h3, pp, tgt, g_final)


def _position():
    x, y, c = lax.axis_index("x"), lax.axis_index("y"), lax.axis_index("c")
    chips = [(1 - x, y), (x, 1 - y), (1 - x, 1 - y)]
    return x, y, c, chips


def _region(ref, kind, shard_shape, q, half, part=None):
    rs, cs = shard_shape
    r0, nr = (0, rs) if half is None else (half * (rs // 2), rs // 2)
    if part is not None:
        r0, nr = r0 + part * (rs // 4), rs // 4
    if kind == "row":
        return ref.at[pl.ds(q * rs + r0, nr), :]
    return ref.at[pl.ds(r0, nr), pl.ds(q * cs, cs)]


def _full_shape(kind, shard_shape):
    rs, cs = shard_shape
    return (N_CHIPS * rs, cs) if kind == "row" else (rs, N_CHIPS * cs)


def _remote(src, dst, send_sems, recv_sems, k, to):
    return pltpu.make_async_remote_copy(src_ref=src, dst_ref=dst, send_sem=send_sems.at[k], recv_sem=recv_sems.at[k],
                                        device_id=to, device_id_type=MESH)


def _same(arrays):
    return [jax.ShapeDtypeStruct(a.shape, a.dtype) for a in arrays]


def _gather_near_carry(gathered, kinds, shapes):
    nw = len(gathered)

    def copies(ops, full, send_sems, recv_sems):
        x, y, c, _ = _position()
        out = []
        for w in range(nw):
            mine = _region(full[w], kinds[w], shapes[w], 2 * x + y, c)
            out.append(_remote(mine, mine, send_sems, recv_sems, 2 * w, (1 - x, y, c)))
            out.append(_remote(mine, mine, send_sems, recv_sems, 2 * w + 1, (x, 1 - y, c)))
        return out

    return _Carry(gathered, _same(gathered), {i: i for i in range(nw)}, 2 * nw, copies)


def _gather_far_carry(gathered, kinds, shapes):
    nw = len(gathered)

    def copies(ops, full, send_sems, recv_sems):
        x, y, c, _ = _position()
        out = []
        for w in range(nw):
            from_x = _region(full[w], kinds[w], shapes[w], 2 * (1 - x) + y, c, part=1)
            from_y = _region(full[w], kinds[w], shapes[w], 2 * x + (1 - y), c, part=0)
            out.append(_remote(from_x, from_x, send_sems, recv_sems, 2 * w, (x, 1 - y, c)))
            out.append(_remote(from_y, from_y, send_sems, recv_sems, 2 * w + 1, (1 - x, y, c)))
        return out

    return _Carry(gathered, _same(gathered), {i: i for i in range(nw)}, 2 * nw, copies)


def _gather_d2d_carry(gathered, kinds, shapes):
    nw = len(gathered)

    def copies(ops, full, send_sems, recv_sems):
        x, y, c, chips = _position()
        out = []
        for w in range(nw):
            for j, (cx, cy) in enumerate(chips):
                landed = _region(full[w], kinds[w], shapes[w], 2 * cx + cy, c)
                out.append(_remote(landed, landed, send_sems, recv_sems, 3 * w + j, (x, y, 1 - c)))
        return out

    return _Carry(gathered, _same(gathered), {i: i for i in range(nw)}, 3 * nw, copies)


def _pairs_carry(grads, kinds, shapes):
    nw = len(grads)

    def copies(g, got, send_sems, recv_sems):
        x, y, c, _ = _position()
        out = []
        for w in range(nw):
            for q in range(N_CHIPS):
                out.append(_remote(_region(g[w], kinds[w], shapes[w], q, 1 - c), got[w].at[q], send_sems, recv_sems,
                                   N_CHIPS * w + q, (x, y, 1 - c)))
        return out

    outs = [jax.ShapeDtypeStruct((N_CHIPS, s[0] // 2, s[1]), BF16) for s in shapes]
    return _Carry(grads, outs, {}, N_CHIPS * nw, copies)


def _pair_sum(name, grad, got, kind, shard_shape, c_arr):
    rs, cs = shard_shape
    hr = rs // 2
    tr = _tile(hr, 512, 16)
    nr = hr // tr

    def body(c_ref, g_ref, s_ref, o_ref):
        o_ref[...] = (g_ref[...].astype(F32) + s_ref[...].astype(F32)).astype(BF16)

    if kind == "row":
        g_spec = pl.BlockSpec((tr, cs), lambda q, i, c_ref: (q * (rs // tr) + c_ref[0] * nr + i, 0))
    else:
        g_spec = pl.BlockSpec((tr, cs), lambda q, i, c_ref: (c_ref[0] * nr + i, q))
    blk = pl.BlockSpec((None, tr, cs), lambda q, i, c_ref: (q, i, 0))
    return pl.pallas_call(
        body,
        name=name,
        grid_spec=pltpu.PrefetchScalarGridSpec(num_scalar_prefetch=1, grid=(N_CHIPS, nr), in_specs=[g_spec, blk],
                                               out_specs=blk),
        out_shape=jax.ShapeDtypeStruct((N_CHIPS, hr, cs), BF16),
        compiler_params=_params(("parallel", "parallel")),
    )(c_arr, grad, got)


def _scatter_carry(sums, shapes):
    nw = len(sums)

    def copies(ps, got, send_sems, recv_sems):
        x, y, c, chips = _position()
        out = []
        for w in range(nw):
            for j, (cx, cy) in enumerate(chips):
                out.append(_remote(ps[w].at[2 * cx + cy], got[w].at[j], send_sems, recv_sems, 3 * w + j, (cx, cy, c)))
        return out

    outs = [jax.ShapeDtypeStruct((3, s[0] // 2, s[1]), BF16) for s in shapes]
    return _Carry(sums, outs, {}, 3 * nw, copies)


def _owner_sum(name, sums, got, shard_shape, qc_arr):
    rs, cs = shard_shape
    hr = rs // 2
    tr = _tile(hr, 512, 16)
    nr = hr // tr

    def body(qc_ref, mine_ref, got_ref, o_ref):
        acc = mine_ref[...].astype(F32)
        for j in range(3):
            acc = acc + got_ref[j].astype(F32)
        o_ref[...] = acc

    return pl.pallas_call(
        body,
        name=name,
        grid_spec=pltpu.PrefetchScalarGridSpec(
            num_scalar_prefetch=1, grid=(nr,),
            in_specs=[pl.BlockSpec((None, tr, cs), lambda i, qc_ref: (qc_ref[0], i, 0)),
                      pl.BlockSpec((3, tr, cs), lambda i, qc_ref: (0, i, 0))],
            out_specs=pl.BlockSpec((tr, cs), lambda i, qc_ref: (qc_ref[1] * nr + i, 0))),
        out_shape=jax.ShapeDtypeStruct((rs, cs), F32),
        compiler_params=_params(("parallel",)),
    )(qc_arr, sums, got)


def _share_carry(grads, shapes):
    nw = len(grads)

    def copies(ops, out, send_sems, recv_sems):
        x, y, c, _ = _position()
        res = []
        for w in range(nw):
            hr = shapes[w][0] // 2
            mine = out[w].at[pl.ds(c * hr, hr), :]
            res.append(_remote(mine, mine, send_sems, recv_sems, w, (x, y, 1 - c)))
        return res

    return _Carry(grads, _same(grads), {i: i for i in range(nw)}, nw, copies)


def _place_block(packed, me):
    return lax.dynamic_update_slice(jnp.zeros((N_DEV,) + packed.shape, F32), packed[None], (me, 0, 0))


def _exchange_carry(blocks):
    def copies(ops, res, send_sems, recv_sems):
        x, y, c, _ = _position()
        mine = res[0].at[4 * x + 2 * y + c]
        out = []
        for k in range(1, N_DEV):
            to = ((1 - x) if k & 4 else x, (1 - y) if k & 2 else y, (1 - c) if k & 1 else c)
            out.append(_remote(mine, mine, send_sems, recv_sems, k - 1, to))
        return out

    return _Carry([blocks], _same([blocks]), {0: 0}, N_DEV - 1, copies)


def _sum_blocks(name, blocks):
    n, rows, lanes = blocks.shape
    tr = _tile(rows, 4096, 8)

    def body(b_ref, o_ref):
        acc = b_ref[0]
        for k in range(1, n):
            acc = acc + b_ref[k]
        o_ref[...] = acc

    return pl.pallas_call(
        body,
        name=name,
        grid=(rows // tr,),
        in_specs=[pl.BlockSpec((n, tr, lanes), lambda i: (0, i, 0))],
        out_specs=pl.BlockSpec((tr, lanes), lambda i: (i, 0)),
        out_shape=jax.ShapeDtypeStruct((rows, lanes), F32),
        compiler_params=_params(("parallel",)),
    )(blocks)


def _adamw(name, w, g, m, v):
    def fn(rows, pars):
        wv, gv, mv, vv = rows
        m_new = ADAM_B1 * mv + (1.0 - ADAM_B1) * gv
        v_new = ADAM_B2 * vv + (1.0 - ADAM_B2) * (gv * gv)
        m_hat = m_new / (1.0 - ADAM_B1 ** ADAM_STEP)
        v_hat = v_new / (1.0 - ADAM_B2 ** ADAM_STEP)
        delta = -ADAM_LR * (m_hat / (jnp.sqrt(v_hat) + ADAM_EPS) + ADAM_WD * wv)
        return [delta, m_new, v_new, gv], []

    c = w.shape[1]
    return _rowwise(name, fn, [w, g, m, v], [], [(c, F32)] * 4, tr=256)


def _pack(arrays):
    rows = []
    for a in arrays:
        flat = a.reshape(-1).astype(F32)
        pad = (-flat.shape[0]) % LANES
        rows.append(jnp.pad(flat, (0, pad)).reshape(-1, LANES))
    stacked = jnp.concatenate(rows, axis=0)
    pad_rows = (-stacked.shape[0]) % 8
    return jnp.pad(stacked, ((0, pad_rows), (0, 0)))


def _unpack(packed, shapes):
    out, r = [], 0
    for s in shapes:
        n = math.prod(s)
        nr = -(-n // LANES)
        out.append(packed[r:r + nr].reshape(-1)[:n].reshape(s))
        r += nr
    return out


BIG = ["w1_gate", "w1_up", "w1_down", "w_in", "ssm_w_glu", "w_out", "w2_gate", "w2_up", "w2_down", "w_ple_gate",
       "w_ple_proj"]
KIND = {"w1_gate": "col", "w1_up": "col", "w1_down": "row", "w_in": "col", "ssm_w_glu": "row", "w_out": "row",
        "w2_gate": "col", "w2_up": "col", "w2_down": "row", "w_ple_gate": "row", "w_ple_proj": "col"}
SMALL = ["norm_ffn1", "norm_mix", "ssm_log_dt", "ssm_a_re", "ssm_a_im", "ssm_b_re", "ssm_b_im", "ssm_c_re", "ssm_c_im",
         "ssm_d", "gmlp_norm_v", "gmlp_w_s", "gmlp_b_s", "norm_ssm_out", "norm_gmlp_out", "norm_ffn2", "norm_ple",
         "norm_final"]
WEIGHTS = ["norm_ffn1", "w1_gate", "w1_up", "w1_down", "norm_mix", "w_in", "ssm_log_dt", "ssm_a_re", "ssm_a_im",
           "ssm_b_re", "ssm_b_im", "ssm_c_re", "ssm_c_im", "ssm_d", "ssm_w_glu", "gmlp_norm_v", "gmlp_w_s", "gmlp_b_s",
           "norm_ssm_out", "norm_gmlp_out", "w_out", "norm_ffn2", "w2_gate", "w2_up", "w2_down", "norm_ple",
           "w_ple_gate", "w_ple_proj", "norm_final"]


class _Trip:
    def __init__(self, names, arrays, carry):
        self.names, self.arrays, self.carry = names, arrays, carry


class _Reducer:
    def __init__(self, shard_shape, c_arr, qc_arr):
        self.shard_shape, self.c_arr, self.qc_arr = shard_shape, c_arr, qc_arr
        self.halves = {}

    def swap(self, names, grads):
        kinds = [KIND[n] for n in names]
        shapes = [self.shard_shape[n] for n in names]
        return _Trip(names, grads, _pairs_carry(grads, kinds, shapes))

    def send(self, trip, swapped):
        shapes = [self.shard_shape[n] for n in trip.names]
        sums = [_pair_sum("pair_sum_" + n, g, s, KIND[n], sh, self.c_arr)
                for n, g, s, sh in zip(trip.names, trip.arrays, swapped, shapes)]
        return _Trip(trip.names, sums, _scatter_carry(sums, shapes))

    def end(self, trip, got):
        for n, ps, g in zip(trip.names, trip.arrays, got):
            self.halves[n] = _owner_sum("owner_sum_" + n, ps, g, self.shard_shape[n], self.qc_arr)


def _ride(*carries):
    present = [c for c in carries if c is not None]
    joined = functools.reduce(_join, present) if present else None

    def split(results):
        out, at = [], 0
        for c in carries:
            n = len(c.out_shapes) if c is not None else 0
            out.append(list(results[at:at + n]))
            at += n
        return out

    return joined, split


def _step(x, p, tgt, w, m, v):
    d_model = x.shape[1]
    d_ssm = w["ssm_d"].shape[1]
    n_groups = d_ssm // SSM_GROUP
    row = lambda a: a.reshape(1, -1)

    xi, yi, ci = lax.axis_index("x"), lax.axis_index("y"), lax.axis_index("c")
    c_arr = jnp.reshape(ci, (1,)).astype(jnp.int32)
    q_arr = jnp.reshape(2 * xi + yi, (1,)).astype(jnp.int32)
    qc_arr = jnp.stack([2 * xi + yi, ci]).astype(jnp.int32)
    shard_shape = {n: w[n].shape for n in BIG}
    full = {n: _cast_into_gathered("cast_" + n, w[n], KIND[n], q_arr) for n in BIG}

    def gather(stage, names):
        return stage([full[n] for n in names], [KIND[n] for n in names], [shard_shape[n] for n in names])

    def gathered(names, arrays):
        full.update(zip(names, arrays))

    groups = [["w1_gate"], ["w1_up"], ["w1_down"], ["w_in"], ["w2_gate"], ["ssm_w_glu", "w_out"], ["w2_up"],
              ["w2_down", "w_ple_gate", "w_ple_proj"]]
    near, far, d2d = _gather_near_carry, _gather_far_carry, _gather_d2d_carry

    def stages(*work):
        carries = [gather(stage, groups[g]) for stage, g in work]
        return functools.reduce(_join, carries), [n for _, g in work for n in groups[g]]

    def alone(name, *work):
        carry, names = stages(*work)
        gathered(names, _comm_call(name, carry))

    alone("gather_a", (near, 0))
    alone("gather_b", (far, 0), (near, 1))
    alone("gather_c", (d2d, 0), (far, 1))
    n1 = _rms_fwd("ffn1_norm", x, w["norm_ffn1"])
    carry, names = stages((d2d, 1), (near, 2), (near, 3))
    gate1, landed = _mm_nn("ffn1_gate", n1, full["w1_gate"], BF16, tm=1024, tn=512, tk=2048, carry=carry)
    gathered(names, landed)
    carry, names = stages((far, 2), (far, 3), (near, 4))
    (a1, da_dgate1, da_dup1), landed = _ffn_up("ffn1", n1, full["w1_up"], gate1, carry)
    gathered(names, landed)
    alone("gather_d", (d2d, 2))
    carry, names = stages((d2d, 3), (far, 4), (near, 5))
    h1, landed = _ffn_down("ffn1", a1, full["w1_down"], x, carry)
    gathered(names, landed)
    ffn1 = (n1, a1, da_dgate1, da_dup1)
    nm = _rms_fwd("mix_norm", h1, w["norm_mix"])
    carry, names = stages((d2d, 4), (far, 5), (near, 6))
    z, landed = _mm_nn("in_proj", nm, full["w_in"], F32, tm=1024, tn=512, tk=2048, carry=carry)
    gathered(names, landed)

    disc, disc_vjp = jax.vjp(_ssm_discretize, w["ssm_log_dt"][0], w["ssm_a_re"], w["ssm_a_im"], w["ssm_b_re"],
                             w["ssm_b_im"])
    abar_r, abar_i, bbar_r, bbar_i = disc
    nb = n_groups // GROUPS_PER_BLOCK
    wb = jnp.concatenate([_blockdiag_in(bbar_r), _blockdiag_in(bbar_i)], axis=-1).astype(BF16)
    wc = jnp.concatenate([_blockdiag_out(w["ssm_c_re"]), -_blockdiag_out(w["ssm_c_im"])], axis=1).astype(BF16)
    abar = jnp.concatenate([abar_r.reshape(nb, -1), abar_i.reshape(nb, -1)], axis=-1)
    abar_conj = jnp.concatenate([abar_r.reshape(nb, -1), -abar_i.reshape(nb, -1)], axis=-1)
    carry, names = stages((d2d, 5), (far, 6), (near, 7))
    (states, y_pre, yg), landed = _ssm_scan_fwd(z, wb, wc, abar, w["ssm_d"], carry)
    gathered(names, landed)
    q = _mm_nn("glu_proj", yg, full["ssm_w_glu"], F32, tm=1024, tn=1024, tk=1024)

    def glu_norm(rows, pars):
        yv = _gelu(rows[0]) * _sigmoid(rows[1])
        yh, _ = _rms_stats(yv)
        return [yh * pars[0]], []

    yn_ssm = _rowwise("ssm_glu_norm", glu_norm, [y_pre, q], [w["norm_ssm_out"]], [(d_ssm, BF16)])[0]

    tril = jnp.tril(jnp.ones((CHUNK, CHUNK), dtype=bool))
    wm = jnp.where(tril[None], w["gmlp_w_s"], 0.0).astype(BF16)
    bias = jnp.repeat(w["gmlp_b_s"].T, GMLP_HEAD, axis=1)
    yn_gmlp = _gmlp_fwd(z, w["gmlp_norm_v"], wm, bias, w["norm_gmlp_out"])
    ycat = jnp.concatenate([yn_ssm, yn_gmlp], axis=1)
    carry, names = stages((d2d, 6), (far, 7))
    h2, landed = _mm_nn("out_proj", ycat, full["w_out"], F32, res=h1, alpha=1.0, tm=512, tn=1024, tk=2048,
                        carry=carry)
    gathered(names, landed)

    n2 = _rms_fwd("ffn2_norm", h2, w["norm_ffn2"])
    carry, names = stages((d2d, 7))
    (a2, da_dgate2, da_dup2), landed = _ffn_gateup("ffn2", n2, full["w2_gate"], full["w2_up"], carry)
    gathered(names, landed)
    h3 = _ffn_down("ffn2", a2, full["w2_down"], h2)
    ffn2 = (n2, a2, da_dgate2, da_dup2)
    npl = _rms_fwd("ple_norm", h3, w["norm_ple"])
    pp = _mm_nn("ple_proj", p, full["w_ple_proj"], F32, tm=1024, tn=1024, tk=2048)
    dgq, dpp, dh4, g_norm_final, loss_part = _ple_head(npl, full["w_ple_gate"], h3, pp, tgt, row(w["norm_final"]))
    reducer = _Reducer(shard_shape, c_arr, qc_arr)
    (g_w_ple_proj,) = _mm_tn("ple_dwproj", p, [dpp], BF16, tm=256, tn=1024, tk=4096)
    (g_w_ple_gate,) = _mm_tn("ple_dwgate", npl, [dgq], BF16, tm=512, tn=1024, tk=4096)
    ple = reducer.swap(["w_ple_gate", "w_ple_proj"], [g_w_ple_gate, g_w_ple_proj])
    dnpl, swapped = _mm_nt_sum("ple_dnorm_in", [dgq], [full["w_ple_gate"]], BF16, tm=512, tn=1024, tk=2048,
                               carry=ple.carry)
    ple = reducer.send(ple, swapped)
    dh3, dh3_b, g_norm_ple = _rms_bwd("ple_dnorm", h3, w["norm_ple"], [dnpl], dh4, 0.5)

    dh2, dh2_b, g_norm_ffn2, got, _, up2_w = _ffn_bwd(
        "ffn2", ["w2_gate", "w2_up", "w2_down"], h2, w["norm_ffn2"], full["w2_gate"], full["w2_up"], full["w2_down"],
        ffn2, dh3, dh3_b, 1.0, reducer, riding=ple.carry, last_hop_later=True)
    reducer.end(ple, got)

    dycat = _mm_nt_sum("out_dproj", [dh2_b], [full["w_out"]], BF16, tm=512, tn=1024, tk=2048)
    (g_w_out,) = _mm_tn("out_dw", ycat, [dh2_b], BF16, tm=512, tn=1024, tk=4096)

    dz, g_norm_gmlp_out, g_gmlp_norm_v, g_wm, g_s = _gmlp_bwd(z, dycat, w["gmlp_norm_v"], wm, bias,
                                                                  w["norm_gmlp_out"])
    g_gmlp_w_s = jnp.where(tril[None], g_wm, 0.0)
    g_gmlp_b_s = g_s.reshape(CHUNK, -1, GMLP_HEAD).sum(axis=-1).T

    def glu_bwd(rows, pars):
        dyn, ypre, qv = rows
        ygv = _gelu(ypre)
        sg = _sigmoid(qv)
        dy, dg = _rms_backward(ygv * sg, pars[0], dyn.astype(F32))
        return [dy * ygv * sg * (1.0 - sg), dy * sg], [dg]

    dq, dyg_part, g_norm_ssm_out = _rowwise("ssm_dglu", glu_bwd, [(dycat, d_ssm, 0), y_pre, q], [w["norm_ssm_out"]],
                                            [(d_ssm, BF16), (d_ssm, F32)], [(1, d_ssm)])
    dyg_proj = _mm_nt_sum("glu_dproj", [dq], [full["ssm_w_glu"]], F32, tm=1024, tn=1024, tk=1024)
    (g_ssm_w_glu,) = _mm_tn("glu_dw", yg, [dq], BF16, tm=512, tn=1024, tk=4096)

    def gelu_bwd(rows, pars):
        return [(rows[0] + rows[1]) * _gelu_grad(rows[2])], []

    dy_pre = _rowwise("ssm_dgelu", gelu_bwd, [dyg_part, dyg_proj, y_pre], [], [(d_ssm, F32)])[0]
    mixers = reducer.swap(["w_out", "ssm_w_glu"], [g_w_out, g_ssm_w_glu])
    me = 4 * xi + 2 * yi + ci
    small = {"gmlp_norm_v": g_gmlp_norm_v, "gmlp_w_s": g_gmlp_w_s, "gmlp_b_s": g_gmlp_b_s,
             "norm_gmlp_out": g_norm_gmlp_out, "norm_ssm_out": g_norm_ssm_out, "norm_ffn2": g_norm_ffn2,
             "norm_ple": g_norm_ple, "norm_final": g_norm_final}
    before_scan = [n for n in SMALL if n in small]
    scan_blocks = _place_block(_pack([small[n] for n in before_scan] + [loss_part[:, :1]]), me)
    carry, split = _ride(up2_w.carry, mixers.carry, _exchange_carry(scan_blocks))
    (dz, g_wb, g_wc, g_abar, g_ssm_d), results = _ssm_scan_bwd(z, dy_pre, states, wb, wc, abar_conj, w["ssm_d"],
                                                              dz, carry)
    got, swapped, (scan_blocks,) = split(results)
    reducer.end(up2_w, got)
    mixers = reducer.send(mixers, swapped)
    g_abar = jnp.transpose(g_abar, (1, 0, 2)).reshape(nb, -1)
    sw = g_abar.shape[-1] // 2
    g_bbar_r = _blockdiag_in_grad(g_wb[..., :sw], SSM_STATE, SSM_GROUP)
    g_bbar_i = _blockdiag_in_grad(g_wb[..., sw:], SSM_STATE, SSM_GROUP)
    g_ssm_c_re = _blockdiag_out_grad(g_wc[:, :sw, :], SSM_GROUP, SSM_STATE)
    g_ssm_c_im = -_blockdiag_out_grad(g_wc[:, sw:, :], SSM_GROUP, SSM_STATE)
    g_abar_r = g_abar[..., :sw].reshape(n_groups, SSM_STATE)
    g_abar_i = g_abar[..., sw:].reshape(n_groups, SSM_STATE)
    g_ssm_log_dt, g_ssm_a_re, g_ssm_a_im, g_ssm_b_re, g_ssm_b_im = disc_vjp((g_abar_r, g_abar_i, g_bbar_r, g_bbar_i))

    (g_w_in,), got = _mm_tn("in_dw", nm, [dz], BF16, tm=512, tn=1536, tk=4096, carry=mixers.carry)
    reducer.end(mixers, got)
    in_w = reducer.swap(["w_in"], [g_w_in])
    dnm, swapped = _mm_nt_sum("in_dproj", [dz], [full["w_in"]], BF16, tm=1024, tn=1024, tk=3072, carry=in_w.carry)
    in_w = reducer.send(in_w, swapped)
    dh1, dh1_b, g_norm_mix = _rms_bwd("mix_dnorm", h1, w["norm_mix"], [dnm], dh2, 0.5)
    small = {"norm_mix": g_norm_mix, "ssm_log_dt": g_ssm_log_dt, "ssm_a_re": g_ssm_a_re,
             "ssm_a_im": g_ssm_a_im, "ssm_b_re": g_ssm_b_re, "ssm_b_im": g_ssm_b_im, "ssm_c_re": g_ssm_c_re,
             "ssm_c_im": g_ssm_c_im, "ssm_d": g_ssm_d}
    after_scan = [n for n in SMALL if n in small]
    ffn_blocks = _place_block(_pack([small[n] for n in after_scan]), me)
    done = [n for n in BIG if n in reducer.halves]
    carry, split = _ride(in_w.carry, _share_carry([reducer.halves[n] for n in done], [shard_shape[n] for n in done]))
    dx, _, g_norm_ffn1, results, (ffn_blocks,), _ = _ffn_bwd(
        "ffn1", ["w1_gate", "w1_up", "w1_down"], x, w["norm_ffn1"], full["w1_gate"], full["w1_up"], full["w1_down"],
        ffn1, dh1, dh1_b, 1.0, reducer, riding=carry, riding_dwd=_exchange_carry(ffn_blocks))
    got, shared = split(results)
    grad = dict(zip(done, shared))
    reducer.end(in_w, got)
    scan_grads = _unpack(_sum_blocks("sum_before_scan", scan_blocks), [w[n].shape for n in before_scan] + [(1,)])
    loss = scan_grads[-1].reshape(())
    early = before_scan + after_scan
    early_grads = scan_grads[:-1] + _unpack(_sum_blocks("sum_after_scan", ffn_blocks), [w[n].shape for n in after_scan])

    rest = [n for n in BIG if n not in done]
    late_blocks = _place_block(_pack([g_norm_ffn1]), me)
    last = _join(_share_carry([reducer.halves[n] for n in rest], [shard_shape[n] for n in rest]),
                 _exchange_carry(late_blocks))
    *shared, late_blocks = _comm_call("share_halves", last)
    grad.update(zip(rest, shared))
    grad.update(zip(early, early_grads))
    grad["norm_ffn1"] = _unpack(_sum_blocks("sum_first_norm", late_blocks), [w["norm_ffn1"].shape])[0]

    small_shapes = [w[n].shape for n in SMALL]
    delta, new_m, new_v = {}, {}, {}
    for n in BIG:
        delta[n], new_m[n], new_v[n], grad[n] = _adamw("adamw_" + n, w[n], grad[n], m[n], v[n])
    d_p, m_p, v_p, _ = _adamw("adamw_small", _pack([w[n] for n in SMALL]), _pack([grad[n] for n in SMALL]),
                              _pack([m[n] for n in SMALL]), _pack([v[n] for n in SMALL]))
    for name_list, packed in ((delta, d_p), (new_m, m_p), (new_v, v_p)):
        for n, a in zip(SMALL, _unpack(packed, small_shapes)):
            name_list[n] = a
    return loss, dx, grad, delta, new_m, new_v


def kernel(x, p, norm_ffn1, w1_gate, w1_up, w1_down, norm_mix, w_in, ssm_log_dt, ssm_a_re, ssm_a_im, ssm_b_re, ssm_b_im, ssm_c_re, ssm_c_im, ssm_d, ssm_w_glu, gmlp_norm_v, gmlp_w_s, gmlp_b_s, norm_ssm_out, norm_gmlp_out, w_out, norm_ffn2, w2_gate, w2_up, w2_down, norm_ple, w_ple_gate, w_ple_proj, norm_final, loss_target, m_norm_ffn1, m_w1_gate, m_w1_up, m_w1_down, m_norm_mix, m_w_in, m_ssm_log_dt, m_ssm_a_re, m_ssm_a_im, m_ssm_b_re, m_ssm_b_im, m_ssm_c_re, m_ssm_c_im, m_ssm_d, m_ssm_w_glu, m_gmlp_norm_v, m_gmlp_w_s, m_gmlp_b_s, m_norm_ssm_out, m_norm_gmlp_out, m_w_out, m_norm_ffn2, m_w2_gate, m_w2_up, m_w2_down, m_norm_ple, m_w_ple_gate, m_w_ple_proj, m_norm_final, v_norm_ffn1, v_w1_gate, v_w1_up, v_w1_down, v_norm_mix, v_w_in, v_ssm_log_dt, v_ssm_a_re, v_ssm_a_im, v_ssm_b_re, v_ssm_b_im, v_ssm_c_re, v_ssm_c_im, v_ssm_d, v_ssm_w_glu, v_gmlp_norm_v, v_gmlp_w_s, v_gmlp_b_s, v_norm_ssm_out, v_norm_gmlp_out, v_w_out, v_norm_ffn2, v_w2_gate, v_w2_up, v_w2_down, v_norm_ple, v_w_ple_gate, v_w_ple_proj, v_norm_final):
    given = dict(locals())
    shapes = {n: given[n].shape for n in WEIGHTS}

    def block(name):
        a = given[name]
        if a.ndim == 1:
            return a.reshape(1, -1)
        return a[0] if a.ndim >= 3 else a

    w = {n: block(n) for n in WEIGHTS}
    m = {n: block("m_" + n) for n in WEIGHTS}
    v = {n: block("v_" + n) for n in WEIGHTS}
    loss, dx, grad, delta, new_m, new_v = _step(x[0], p[0, 0], loss_target[0], w, m, v)
    outs = [loss, dx[None]]
    for tree in (grad, delta, new_m, new_v):
        outs += [tree[n].reshape(shapes[n]) for n in WEIGHTS]
    return tuple(outs)
```

```python
import functools
import math

import jax
import jax.numpy as jnp
from jax import lax
from jax.experimental import pallas as pl
from jax.experimental.pallas import tpu as pltpu

F32 = jnp.float32
BF16 = jnp.bfloat16
EPS = 1e-6
SSM_GROUP = 16
SSM_STATE = 64
GROUPS_PER_BLOCK = 8
GMLP_HEAD = 128
CHUNK = 128
ADAM_LR = 0.001
ADAM_B1 = 0.9
ADAM_B2 = 0.999
ADAM_EPS = 1e-08
ADAM_WD = 0.01
ADAM_STEP = 10
N_CHIPS = 4
N_DEV = 8
LANES = 128
VMEM_LIMIT_BYTES = 56 * 1024 * 1024
MESH = pl.DeviceIdType.MESH
GELU_C = math.sqrt(2.0 / math.pi)
GELU_A = 0.044715

_DOT_DIMS = {
    "nn": (((1,), (0,)), ((), ())),
    "nt": (((1,), (1,)), ((), ())),
    "tn": (((0,), (0,)), ((), ())),
}


def _tile(dim, pref, align):
    if dim <= pref:
        return dim
    t = (pref // align) * align
    while t >= align:
        if dim % t == 0:
            return t
        t -= align
    return dim


def _params(semantics):
    return pltpu.CompilerParams(dimension_semantics=semantics, vmem_limit_bytes=VMEM_LIMIT_BYTES)


def _gelu(x):
    return 0.5 * x * (1.0 + jnp.tanh(GELU_C * (x + GELU_A * x * x * x)))


def _gelu_grad(x):
    t = jnp.tanh(GELU_C * (x + GELU_A * x * x * x))
    return 0.5 * (1.0 + t) + 0.5 * x * (1.0 - t * t) * GELU_C * (1.0 + 3.0 * GELU_A * x * x)


def _sigmoid(x):
    return 1.0 / (1.0 + jnp.exp(-x))


def _dot(a, b, mode):
    return lax.dot_general(a.astype(BF16), b.astype(BF16), _DOT_DIMS[mode], preferred_element_type=F32)


class _Carry:
    def __init__(self, arrays, out_shapes, aliases, n_copies, copies):
        self.arrays = list(arrays)
        self.out_shapes = list(out_shapes)
        self.aliases = dict(aliases)
        self.n_copies = n_copies
        self.copies = copies

    def scratch(self):
        return [pltpu.SemaphoreType.DMA((self.n_copies,)), pltpu.SemaphoreType.DMA((self.n_copies,))]

    def split(self, refs):
        n_in, n_out = len(self.arrays), len(self.out_shapes)
        return refs[:n_in], refs[n_in:n_in + n_out], refs[n_in + n_out], refs[n_in + n_out + 1]

    def start(self, refs):
        for cp in self.copies(*self.split(refs)):
            cp.start()

    def wait(self, refs):
        for cp in self.copies(*self.split(refs)):
            cp.wait()


class _SemRange:
    def __init__(self, sems, offset):
        self.sems, self.offset = sems, offset

    @property
    def at(self):
        return self

    def __getitem__(self, k):
        return self.sems.at[self.offset + k]


def _join(first, second):
    n_in, n_out = len(first.arrays), len(first.out_shapes)
    aliases = dict(first.aliases)
    aliases.update({n_in + i: n_out + o for i, o in second.aliases.items()})

    def copies(ops, res, send_sems, recv_sems):
        return (first.copies(ops[:n_in], res[:n_out], send_sems, recv_sems)
                + second.copies(ops[n_in:], res[n_out:], _SemRange(send_sems, first.n_copies),
                                _SemRange(recv_sems, first.n_copies)))

    return _Carry(first.arrays + second.arrays, first.out_shapes + second.out_shapes, aliases,
                  first.n_copies + second.n_copies, copies)


_ANY = pl.BlockSpec(memory_space=pl.ANY)


def _comm_call(name, carry):
    def body(*refs):
        carry.start(refs)
        carry.wait(refs)

    n_in = len(carry.arrays)
    return pl.pallas_call(
        body,
        name=name,
        in_specs=[_ANY] * n_in,
        out_specs=[_ANY] * len(carry.out_shapes),
        out_shape=carry.out_shapes,
        input_output_aliases=carry.aliases,
        scratch_shapes=carry.scratch(),
    )(*carry.arrays)


def _carried_call(body, carry, *, name, grid, in_specs, out_specs, out_shape, scratch_shapes, semantics, args,
                  aliases=None):
    aliases = dict(aliases or {})
    if carry is None:
        res = pl.pallas_call(body, name=name, grid=grid, in_specs=in_specs, out_specs=out_specs, out_shape=out_shape,
                             scratch_shapes=scratch_shapes, input_output_aliases=aliases,
                             compiler_params=_params(semantics))(*args)
        return res, []
    n_in, n_out, n_scr = len(in_specs), len(out_specs), len(scratch_shapes)
    nci, nco = len(carry.arrays), len(carry.out_shapes)

    def wrapped(*refs):
        ins = refs[:n_in]
        outs = refs[n_in + nci:n_in + nci + n_out]
        scr = refs[n_in + nci + n_out + nco:n_in + nci + n_out + nco + n_scr]
        c_refs = (refs[n_in:n_in + nci] + refs[n_in + nci + n_out:n_in + nci + n_out + nco]
                  + refs[n_in + nci + n_out + nco + n_scr:])
        first = functools.reduce(jnp.logical_and, [pl.program_id(d) == 0 for d in range(len(grid))])
        last = functools.reduce(jnp.logical_and, [pl.program_id(d) == grid[d] - 1 for d in range(len(grid))])

        @pl.when(first)
        def _():
            carry.start(c_refs)

        body(*ins, *outs, *scr)

        @pl.when(last)
        def _():
            carry.wait(c_refs)

    res = pl.pallas_call(
        wrapped,
        name=name,
        grid=grid,
        in_specs=list(in_specs) + [_ANY] * nci,
        out_specs=list(out_specs) + [_ANY] * nco,
        out_shape=list(out_shape) + carry.out_shapes,
        input_output_aliases={**aliases, **{n_in + i: n_out + o for i, o in carry.aliases.items()}},
        scratch_shapes=list(scratch_shapes) + carry.scratch(),
        compiler_params=_params(("arbitrary",) * len(grid)),
    )(*args, *carry.arrays)
    return res[:n_out], res[n_out:]


def _matmul(name, mode, a_list, b_list, products, out_dtypes, epilogue, extras=(), tm=512, tn=512, tk=2048,
            carry=None, n_part=(0, 1)):
    a0, b0 = a_list[0], b_list[0]
    if mode == "tn":
        k_dim, m_dim = a0.shape
    else:
        m_dim, k_dim = a0.shape
    n_dim = (b0.shape[0] if mode == "nt" else b0.shape[1]) // n_part[1]
    tm = _tile(m_dim, tm, LANES)
    tn = _tile(n_dim, tn, LANES)
    tk = _tile(k_dim, tk, LANES)
    nk = k_dim // tk
    j0 = n_part[0] * (n_dim // tn)
    chunk = 2 * LANES if (nk == 1 and epilogue is not _identity and tn % (2 * LANES) == 0) else tn
    n_acc = 1 + max(p[2] for p in products)
    na, nb, ne, no = len(a_list), len(b_list), len(extras), len(out_dtypes)

    if mode == "tn":
        a_spec = pl.BlockSpec((tk, tm), lambda i, j, k: (k, i))
    else:
        a_spec = pl.BlockSpec((tm, tk), lambda i, j, k: (i, k))
    if mode == "nt":
        b_spec = pl.BlockSpec((tn, tk), lambda i, j, k: (j0 + j, k))
    else:
        b_spec = pl.BlockSpec((tk, tn), lambda i, j, k: (k, j0 + j))
    t_spec = pl.BlockSpec((tm, tn), lambda i, j, k: (i, j))

    def body(*refs):
        a_refs = refs[:na]
        b_refs = refs[na:na + nb]
        e_refs = refs[na + nb:na + nb + ne]
        o_refs = refs[na + nb + ne:na + nb + ne + no]
        acc_refs = refs[na + nb + ne + no:]

        def partial_sums(cols):
            sums = [None] * n_acc
            for ai, bi, ci in products:
                b = b_refs[bi][cols, :] if mode == "nt" else b_refs[bi][:, cols]
                d = _dot(a_refs[ai][...], b, mode)
                sums[ci] = d if sums[ci] is None else sums[ci] + d
            return sums

        def finish(accs, cols):
            outs = epilogue(accs, [e[:, cols] for e in e_refs])
            for o_ref, o in zip(o_refs, outs):
                o_ref[:, cols] = o.astype(o_ref.dtype)

        if nk == 1:
            for c0 in range(0, tn, chunk):
                finish(partial_sums(slice(c0, c0 + chunk)), slice(c0, c0 + chunk))
        else:
            sums = partial_sums(slice(None))
            finish = functools.partial(finish, cols=slice(None))
            k = pl.program_id(2)

            @pl.when(k == 0)
            def _():
                for acc, s in zip(acc_refs, sums):
                    acc[...] = s

            @pl.when(k > 0)
            def _():
                for acc, s in zip(acc_refs, sums):
                    acc[...] += s

            @pl.when(k == nk - 1)
            def _():
                finish([acc[...] for acc in acc_refs])

    scratch = [pltpu.VMEM((tm, tn), F32) for _ in range(n_acc)] if nk > 1 else []
    outs, carried = _carried_call(
        body, carry,
        name=name,
        grid=(m_dim // tm, n_dim // tn, nk),
        in_specs=[a_spec] * na + [b_spec] * nb + [t_spec] * ne,
        out_specs=[t_spec] * no,
        out_shape=[jax.ShapeDtypeStruct((m_dim, n_dim), dt) for dt in out_dtypes],
        scratch_shapes=scratch,
        semantics=("parallel", "parallel", "arbitrary"),
        args=[*a_list, *b_list, *extras],
    )
    return (outs, carried) if carry else outs


def _identity(accs, extras):
    return accs


def _single(result, carry):
    return (result[0][0], result[1]) if carry else result[0]


def _mm_nn(name, a, b, out_dtype, res=None, alpha=1.0, carry=None, **tiles):
    if res is None:
        return _single(_matmul(name, "nn", [a], [b], [(0, 0, 0)], [out_dtype], _identity, carry=carry, **tiles), carry)

    def epilogue(accs, extras):
        return [extras[0] + alpha * accs[0]]

    return _single(_matmul(name, "nn", [a], [b], [(0, 0, 0)], [out_dtype], epilogue, extras=(res,), carry=carry,
                           **tiles), carry)


def _mm_nt_sum(name, a_list, b_list, out_dtype, carry=None, n_part=(0, 1), **tiles):
    products = [(i, i, 0) for i in range(len(a_list))]
    return _single(_matmul(name, "nt", a_list, b_list, products, [out_dtype], _identity, carry=carry, n_part=n_part,
                           **tiles), carry)


def _mm_tn(name, a, b_list, out_dtype, carry=None, **tiles):
    products = [(0, i, i) for i in range(len(b_list))]
    return _matmul(name, "tn", [a], b_list, products, [out_dtype] * len(b_list), _identity, carry=carry, **tiles)


def _rowwise(name, fn, row_ins, par_ins, row_outs, acc_outs=(), tr=512, carry=None):
    first = row_ins[0][0] if isinstance(row_ins[0], tuple) else row_ins[0]
    t_dim = first.shape[0]
    tr = _tile(t_dim, tr, 16)
    arrays, specs = [], []
    for r in row_ins:
        if isinstance(r, tuple):
            arr, width, blk = r
            specs.append(pl.BlockSpec((tr, width), lambda i, blk=blk: (i, blk)))
        else:
            arr = r
            specs.append(pl.BlockSpec((tr, arr.shape[1]), lambda i: (i, 0)))
        arrays.append(arr)
    for p in par_ins:
        arrays.append(p)
        specs.append(pl.BlockSpec(p.shape, lambda i, nd=p.ndim: (0,) * nd))
    nr, npar, nro, nacc = len(row_ins), len(par_ins), len(row_outs), len(acc_outs)

    def body(*refs):
        rows = [r[...] for r in refs[:nr]]
        pars = [p[...] for p in refs[nr:nr + npar]]
        o_refs = refs[nr + npar:nr + npar + nro]
        acc_refs = refs[nr + npar + nro:]
        outs, accs = fn(rows, pars)
        for o_ref, o in zip(o_refs, outs):
            o_ref[...] = o.astype(o_ref.dtype)
        if nacc:
            @pl.when(pl.program_id(0) == 0)
            def _():
                for a_ref in acc_refs:
                    a_ref[...] = jnp.zeros_like(a_ref)

            for a_ref, a in zip(acc_refs, accs):
                a_ref[...] += a

    out_shape = [jax.ShapeDtypeStruct((t_dim, c), dt) for c, dt in row_outs]
    out_shape += [jax.ShapeDtypeStruct(s, F32) for s in acc_outs]
    out_specs = [pl.BlockSpec((tr, c), lambda i: (i, 0)) for c, _ in row_outs]
    out_specs += [pl.BlockSpec(s, lambda i: (0, 0)) for s in acc_outs]
    res, carried = _carried_call(body, carry, name=name, grid=(t_dim // tr,), in_specs=specs, out_specs=out_specs,
                                 out_shape=out_shape, scratch_shapes=[], semantics=("arbitrary",), args=arrays)
    return (res, carried) if carry else res


def _rms_stats(x):
    r = lax.rsqrt(jnp.mean(x * x, axis=-1, keepdims=True) + EPS)
    return x * r, r


def _rms_backward(x, g, dy):
    xh, r = _rms_stats(x)
    a = dy * g
    dx = r * (a - xh * jnp.mean(a * xh, axis=-1, keepdims=True))
    return dx, jnp.sum(dy * xh, axis=0, keepdims=True)


def _rms_fwd(name, x, g, carry=None):
    def fn(rows, pars):
        xh, _ = _rms_stats(rows[0])
        return [xh * pars[0]], []

    res = _rowwise(name, fn, [x], [g], [(x.shape[1], BF16)], carry=carry)
    return (res[0][0], res[1]) if carry else res[0]


def _rms_bwd(name, x, g, dy_parts, dres, scale):
    def fn(rows, pars):
        dy = (rows[2] if len(rows) == 3 else jnp.concatenate(rows[2:], axis=1)).astype(F32)
        dx, dg = _rms_backward(rows[0], pars[0], dy)
        tot = rows[1] + dx
        return [tot, scale * tot], [dg]

    d = x.shape[1]
    return _rowwise(name, fn, [x, dres, *dy_parts], [g], [(d, F32), (d, BF16)], [(1, d)], tr=256)


def _cast_into_gathered(name, w, kind, q_arr):
    rs, cs = w.shape
    tr = _tile(rs, 256, 16)
    nr = rs // tr

    def body(q_ref, w_ref, o_ref):
        o_ref[...] = w_ref[...].astype(BF16)

    if kind == "row":
        o_spec = pl.BlockSpec((tr, cs), lambda i, q_ref: (q_ref[0] * nr + i, 0))
    else:
        o_spec = pl.BlockSpec((tr, cs), lambda i, q_ref: (i, q_ref[0]))
    return pl.pallas_call(
        body,
        name=name,
        grid_spec=pltpu.PrefetchScalarGridSpec(num_scalar_prefetch=1, grid=(nr,),
                                               in_specs=[pl.BlockSpec((tr, cs), lambda i, q_ref: (i, 0))],
                                               out_specs=o_spec),
        out_shape=jax.ShapeDtypeStruct(_full_shape(kind, (rs, cs)), BF16),
        compiler_params=_params(("parallel",)),
    )(q_arr, w)


def _swiglu_tiles(gate, up):
    s = _sigmoid(gate)
    silu = gate * s
    return [silu * up, up * (s * (1.0 + gate * (1.0 - s))), silu]


def _ffn_gateup(tag, n, wg, wu, carry):
    def act(accs, extras):
        return _swiglu_tiles(accs[0], accs[1])

    return _matmul(tag + "_gateup", "nn", [n], [wg, wu], [(0, 0, 0), (0, 1, 1)], [BF16] * 3, act,
                   tm=1024, tn=512, tk=2048, carry=carry)


def _ffn_up(tag, n, wu, gate, carry):
    def act(accs, extras):
        return _swiglu_tiles(extras[0].astype(F32), accs[0])

    return _matmul(tag + "_up", "nn", [n], [wu], [(0, 0, 0)], [BF16] * 3, act, extras=(gate,),
                   tm=1024, tn=512, tk=2048, carry=carry)


def _ffn_down(tag, a, wd, h, carry=None):
    return _mm_nn(tag + "_down", a, wd, F32, res=h, alpha=0.5, tm=1024, tn=512, tk=5632, carry=carry)


def _ffn_bwd(tag, names, h, g, wg, wu, wd, saved, dh, dfb, next_scale, reducer, riding=None, riding_dwd=None,
             last_hop_later=False):
    n, a, da_dgate, da_dup = saved

    def act_bwd(accs, extras):
        return [accs[0] * extras[0].astype(F32), accs[0] * extras[1].astype(F32)]

    dact = _matmul(tag + "_dact", "nt", [dfb], [wd], [(0, 0, 0)], [BF16, BF16], act_bwd, extras=(da_dgate, da_dup),
                   tm=1024, tn=512, tk=2048, carry=riding)
    (dgp, du), rode = dact if riding else (dact, [])
    dwd_call = _mm_tn(tag + "_dwd", a, [dfb], BF16, tm=512, tn=2048, tk=4096, carry=riding_dwd)
    (dwd,), rode_dwd = dwd_call if riding_dwd else (dwd_call, [])
    down = reducer.swap(names[2:], [dwd])
    (dwg,), swapped = _mm_tn(tag + "_dwg", n, [dgp], BF16, tm=512, tn=1408, tk=4096, carry=down.carry)
    down = reducer.send(down, swapped)
    gate_w = reducer.swap(names[:1], [dwg])
    carry, split = _ride(down.carry, gate_w.carry)
    (dwu,), results = _mm_tn(tag + "_dwu", n, [du], BF16, tm=512, tn=1408, tk=4096, carry=carry)
    got, swapped = split(results)
    reducer.end(down, got)
    gate_w = reducer.send(gate_w, swapped)
    up_w = reducer.swap(names[1:2], [dwu])
    carry, split = _ride(gate_w.carry, up_w.carry)
    halves = 1 if last_hop_later else 2
    tiles = dict(tm=512, tn=512, tk=5632) if last_hop_later else dict(tm=1024, tn=1024, tk=1408)
    dn_lo, results = _mm_nt_sum(tag + "_dn_lo", [dgp, du], [wg, wu], BF16, carry=carry, n_part=(0, halves), **tiles)
    got, swapped = split(results)
    reducer.end(gate_w, got)
    up_w = reducer.send(up_w, swapped)
    dn = [dn_lo]
    if not last_hop_later:
        dn_hi, got = _mm_nt_sum(tag + "_dn_hi", [dgp, du], [wg, wu], BF16, tm=1024, tn=1024, tk=1408,
                                carry=up_w.carry, n_part=(1, 2))
        reducer.end(up_w, got)
        dn.append(dn_hi)
    dh_in, dh_in_b, dg = _rms_bwd(tag + "_dnorm", h, g, dn, dh, next_scale)
    return dh_in, dh_in_b, dg, rode, rode_dwd, (up_w if last_hop_later else None)


def _ssm_discretize(log_dt, a_re, a_im, b_re, b_im):
    dt = jnp.exp(log_dt)[:, None]
    lr = jnp.minimum(a_re, -1e-4)
    li = a_im
    mag = jnp.exp(lr * dt)
    ang = li * dt
    abar_r = mag * jnp.cos(ang)
    abar_i = mag * jnp.sin(ang)
    den = lr * lr + li * li
    xr = abar_r - 1.0
    xi = abar_i
    zr = (xr * lr + xi * li) / den
    zi = (xi * lr - xr * li) / den
    bbar_r = zr[..., None] * b_re - zi[..., None] * b_im
    bbar_i = zr[..., None] * b_im + zi[..., None] * b_re
    return abar_r, abar_i, bbar_r, bbar_i


def _blockdiag_in(b):
    g, n, p = b.shape
    nb = g // GROUPS_PER_BLOCK
    eye = jnp.eye(GROUPS_PER_BLOCK, dtype=b.dtype)
    b4 = b.reshape(nb, GROUPS_PER_BLOCK, n, p)
    return jnp.einsum("sgnp,gh->sgphn", b4, eye).reshape(nb, GROUPS_PER_BLOCK * p, GROUPS_PER_BLOCK * n)


def _blockdiag_in_grad(gw, n, p):
    nb = gw.shape[0]
    eye = jnp.eye(GROUPS_PER_BLOCK, dtype=gw.dtype)
    g5 = gw.reshape(nb, GROUPS_PER_BLOCK, p, GROUPS_PER_BLOCK, n)
    return jnp.einsum("sgphn,gh->sgnp", g5, eye).reshape(nb * GROUPS_PER_BLOCK, n, p)


def _blockdiag_out(c):
    g, p, n = c.shape
    nb = g // GROUPS_PER_BLOCK
    eye = jnp.eye(GROUPS_PER_BLOCK, dtype=c.dtype)
    c4 = c.reshape(nb, GROUPS_PER_BLOCK, p, n)
    return jnp.einsum("sgpn,gh->shngp", c4, eye).reshape(nb, GROUPS_PER_BLOCK * n, GROUPS_PER_BLOCK * p)


def _blockdiag_out_grad(gw, p, n):
    nb = gw.shape[0]
    eye = jnp.eye(GROUPS_PER_BLOCK, dtype=gw.dtype)
    g5 = gw.reshape(nb, GROUPS_PER_BLOCK, n, GROUPS_PER_BLOCK, p)
    return jnp.einsum("shngp,gh->sgpn", g5, eye).reshape(nb * GROUPS_PER_BLOCK, p, n)


def _ssm_scan_fwd(z, wb, wc, abar, d, carry=None):
    t_dim = z.shape[0]
    nb, cb, sw2 = wb.shape
    nl = sw2 // LANES
    hl = nl // 2
    tt = _tile(t_dim, 256, 8)
    nt = t_dim // tt

    def body(z_ref, wb_ref, wc_ref, a_ref, d_ref, s_ref, y_ref, yg_ref, drive_ref, st_ref):
        @pl.when(pl.program_id(0) == 0)
        def _():
            st_ref[...] = jnp.zeros_like(st_ref)

        u = z_ref[...]
        ub = u.astype(BF16)
        for b in range(nb):
            drive = _dot(ub[:, b * cb:(b + 1) * cb], wb_ref[b], "nn")
            for l in range(nl):
                drive_ref[l, pl.ds(b, tt, stride=nb), :] = drive[:, l * LANES:(l + 1) * LANES]
        a = a_ref[...]
        chunk = lambda v, l: v[:, l * LANES:(l + 1) * LANES]

        def step(t, state):
            rows = pl.ds(pl.multiple_of(t * nb, nb), nb)
            re, im = [], []
            for l in range(hl):
                ar, ai, sr, si = chunk(a, l), chunk(a, hl + l), state[l], state[hl + l]
                nr = ar * sr - ai * si + drive_ref[l, rows, :]
                ni = ar * si + ai * sr + drive_ref[hl + l, rows, :]
                s_ref[l, rows, :] = nr
                s_ref[hl + l, rows, :] = ni
                re.append(nr)
                im.append(ni)
            return tuple(re + im)

        state = lax.fori_loop(0, tt, step, tuple(st_ref[l] for l in range(nl)), unroll=8)
        for l in range(nl):
            st_ref[l] = state[l]
        parts = []
        for b in range(nb):
            s_b = jnp.concatenate([s_ref[l, pl.ds(b, tt, stride=nb), :] for l in range(nl)], axis=1)
            parts.append(_dot(s_b, wc_ref[b], "nn"))
        y = jnp.concatenate(parts, axis=1) + d_ref[...] * u
        y_ref[...] = y
        yg_ref[...] = _gelu(y).astype(BF16)

    full = lambda a: pl.BlockSpec(a.shape, lambda t, nd=a.ndim: (0,) * nd)
    return _carried_call(
        body, carry,
        name="ssm_scan_fwd",
        grid=(nt,),
        in_specs=[pl.BlockSpec((tt, nb * cb), lambda t: (t, 0)), full(wb), full(wc), full(abar), full(d)],
        out_specs=[
            pl.BlockSpec((nl, tt * nb, LANES), lambda t: (0, t, 0)),
            pl.BlockSpec((tt, nb * cb), lambda t: (t, 0)),
            pl.BlockSpec((tt, nb * cb), lambda t: (t, 0)),
        ],
        out_shape=[
            jax.ShapeDtypeStruct((nl, t_dim * nb, LANES), F32),
            jax.ShapeDtypeStruct((t_dim, nb * cb), F32),
            jax.ShapeDtypeStruct((t_dim, nb * cb), BF16),
        ],
        scratch_shapes=[pltpu.VMEM((nl, tt * nb, LANES), F32), pltpu.VMEM((nl, nb, LANES), F32)],
        semantics=("arbitrary",),
        args=[z, wb, wc, abar, d],
    )


def _ssm_scan_bwd(z, dy, states, wb, wc, abar_conj, d, dz_all, carry=None):
    t_dim = z.shape[0]
    nb, cb, sw2 = wb.shape
    nl = sw2 // LANES
    hl = nl // 2
    tt = _tile(t_dim, 256, 8)
    nt = t_dim // tt
    edges = states.reshape(nl, nt, tt * nb, LANES)[:, :, (tt - 1) * nb:, :]
    before = jnp.concatenate([jnp.zeros((nl, 1, nb, LANES), F32), edges[:, :-1]], axis=1).reshape(nl, nt * nb, LANES)

    def body(z_ref, dy_ref, s_ref, sp_ref, wb_ref, wc_ref, a_ref, d_ref, dz_all_ref,
             dz_ref, gwb_ref, gwc_ref, ga_ref, gd_ref, gin_ref, gs_ref, st_ref):
        @pl.when(pl.program_id(0) == 0)
        def _():
            st_ref[...] = jnp.zeros_like(st_ref)
            gwb_ref[...] = jnp.zeros_like(gwb_ref)
            gwc_ref[...] = jnp.zeros_like(gwc_ref)
            ga_ref[...] = jnp.zeros_like(ga_ref)
            gd_ref[...] = jnp.zeros_like(gd_ref)

        u = z_ref[...]
        dyv = dy_ref[...]
        ub = u.astype(BF16)
        dyb = dyv.astype(BF16)
        for b in range(nb):
            gin = _dot(dyb[:, b * cb:(b + 1) * cb], wc_ref[b], "nt")
            for l in range(nl):
                gin_ref[l, pl.ds(b, tt, stride=nb), :] = gin[:, l * LANES:(l + 1) * LANES]
        a = a_ref[...]
        chunk = lambda v, l: v[:, l * LANES:(l + 1) * LANES]

        def step(k, state):
            rows = pl.ds(pl.multiple_of((tt - 1 - k) * nb, nb), nb)
            re, im = [], []
            for l in range(hl):
                ar, ai, gr, gi = chunk(a, l), chunk(a, hl + l), state[l], state[hl + l]
                nr = ar * gr - ai * gi + gin_ref[l, rows, :]
                ni = ar * gi + ai * gr + gin_ref[hl + l, rows, :]
                gs_ref[l, rows, :] = nr
                gs_ref[hl + l, rows, :] = ni
                re.append(nr)
                im.append(ni)
            return tuple(re + im)

        state = lax.fori_loop(0, tt, step, tuple(st_ref[l] for l in range(nl)), unroll=8)
        for l in range(nl):
            st_ref[l] = state[l]

        parts = []
        for b in range(nb):
            cols = slice(b * cb, (b + 1) * cb)
            gs_b = jnp.concatenate([gs_ref[l, pl.ds(b, tt, stride=nb), :] for l in range(nl)], axis=1)
            s_b = jnp.concatenate([s_ref[l, pl.ds(b, tt, stride=nb), :] for l in range(nl)], axis=1)
            parts.append(_dot(gs_b, wb_ref[b], "nt"))
            gwb_ref[b] += _dot(ub[:, cols], gs_b, "tn")
            gwc_ref[b] += _dot(s_b, dyb[:, cols], "tn")
        dz_ref[...] = (jnp.concatenate(parts, axis=1) + d_ref[...] * dyv).astype(BF16)
        gd_ref[...] += jnp.sum(dyv * u, axis=0, keepdims=True)

        row = lax.broadcasted_iota(jnp.int32, (tt * nb, LANES), 0)
        shifted = lambda v: jnp.where(row < nb, 0.0, pltpu.roll(v, nb, 0))
        over_time = lambda v: jnp.sum(v.reshape(tt, nb, LANES), axis=0)
        for l in range(hl):
            g_r, g_i = gs_ref[l], gs_ref[hl + l]
            p_r, p_i = shifted(s_ref[l]), shifted(s_ref[hl + l])
            f_r, f_i = sp_ref[l], sp_ref[hl + l]
            g0_r, g0_i = gs_ref[l, pl.ds(0, nb), :], gs_ref[hl + l, pl.ds(0, nb), :]
            ga_ref[l] += over_time(g_r * p_r + g_i * p_i) + g0_r * f_r + g0_i * f_i
            ga_ref[hl + l] += over_time(g_i * p_r - g_r * p_i) + g0_i * f_r - g0_r * f_i

    rev = lambda t: (nt - 1 - t, 0)
    rev3 = lambda t: (0, nt - 1 - t, 0)
    full = lambda a: pl.BlockSpec(a.shape, lambda t, nd=a.ndim: (0,) * nd)
    return _carried_call(
        body, carry,
        name="ssm_scan_bwd",
        grid=(nt,),
        in_specs=[
            pl.BlockSpec((tt, nb * cb), rev),
            pl.BlockSpec((tt, nb * cb), rev),
            pl.BlockSpec((nl, tt * nb, LANES), rev3),
            pl.BlockSpec((nl, nb, LANES), rev3),
            full(wb), full(wc), full(abar_conj), full(d), _ANY,
        ],
        out_specs=[
            pl.BlockSpec((tt, nb * cb), rev),
            pl.BlockSpec((nb, cb, sw2), lambda t: (0, 0, 0)),
            pl.BlockSpec((nb, sw2, cb), lambda t: (0, 0, 0)),
            pl.BlockSpec((nl, nb, LANES), lambda t: (0, 0, 0)),
            pl.BlockSpec((1, nb * cb), lambda t: (0, 0)),
        ],
        out_shape=[
            jax.ShapeDtypeStruct(dz_all.shape, BF16),
            jax.ShapeDtypeStruct((nb, cb, sw2), F32),
            jax.ShapeDtypeStruct((nb, sw2, cb), F32),
            jax.ShapeDtypeStruct((nl, nb, LANES), F32),
            jax.ShapeDtypeStruct((1, nb * cb), F32),
        ],
        scratch_shapes=[pltpu.VMEM((nl, tt * nb, LANES), F32), pltpu.VMEM((nl, tt * nb, LANES), F32),
                        pltpu.VMEM((nl, nb, LANES), F32)],
        semantics=("arbitrary",),
        args=[z, dy, states, before, wb, wc, abar_conj, d, dz_all],
        aliases={8: 0},
    )


def _gmlp_chunk(zu, zv, gv, wm_ref, bias, n_heads):
    ua = _gelu(zu)
    vg = _gelu(zv)
    xc = vg - jnp.mean(vg, axis=-1, keepdims=True)
    r = lax.rsqrt(jnp.mean(xc * xc, axis=-1, keepdims=True) + EPS)
    vh = xc * r
    vb = (vh * gv).astype(BF16)
    parts = []
    for h in range(n_heads):
        cols = slice(h * GMLP_HEAD, (h + 1) * GMLP_HEAD)
        parts.append(_dot(wm_ref[h], vb[:, cols], "nn"))
    s = jnp.concatenate(parts, axis=1) + bias
    return ua, vh, r, vb, s


def _gmlp_fwd(z, gv, wm, bias, ggo):
    t_dim = z.shape[0]
    dg = gv.shape[1]
    n_heads = dg // GMLP_HEAD
    tr = _tile(t_dim, 256, CHUNK)

    def body(zu_ref, zv_ref, gv_ref, wm_ref, b_ref, ggo_ref, o_ref):
        for ck in range(tr // CHUNK):
            rows = pl.ds(ck * CHUNK, CHUNK)
            ua, _, _, _, s = _gmlp_chunk(zu_ref[rows, :], zv_ref[rows, :], gv_ref[...], wm_ref, b_ref[...], n_heads)
            yh, _ = _rms_stats(ua * s)
            o_ref[rows, :] = (yh * ggo_ref[...]).astype(BF16)

    full = lambda a: pl.BlockSpec(a.shape, lambda i, nd=a.ndim: (0,) * nd)
    return pl.pallas_call(
        body,
        name="gmlp_fwd",
        grid=(t_dim // tr,),
        in_specs=[pl.BlockSpec((tr, dg), lambda i: (i, 1)), pl.BlockSpec((tr, dg), lambda i: (i, 2)),
                  full(gv), full(wm), full(bias), full(ggo)],
        out_specs=pl.BlockSpec((tr, dg), lambda i: (i, 0)),
        out_shape=jax.ShapeDtypeStruct((t_dim, dg), BF16),
        compiler_params=_params(("parallel",)),
    )(z, z, gv, wm, bias, ggo)


def _gmlp_bwd(z, dycat, gv, wm, bias, ggo):
    t_dim = z.shape[0]
    dg = gv.shape[1]
    n_heads = dg // GMLP_HEAD
    tr = _tile(t_dim, 256, CHUNK)

    def body(zu_ref, zv_ref, dy_ref, gv_ref, wm_ref, b_ref, ggo_ref,
             dz_ref, dggo_ref, dgv_ref, dwm_ref, dsum_ref):
        @pl.when(pl.program_id(0) == 0)
        def _():
            dggo_ref[...] = jnp.zeros_like(dggo_ref)
            dgv_ref[...] = jnp.zeros_like(dgv_ref)
            dwm_ref[...] = jnp.zeros_like(dwm_ref)
            dsum_ref[...] = jnp.zeros_like(dsum_ref)

        for ck in range(tr // CHUNK):
            rows = pl.ds(ck * CHUNK, CHUNK)
            zu = zu_ref[rows, :]
            zv = zv_ref[rows, :]
            gvv = gv_ref[...]
            ua, vh, r, vb, s = _gmlp_chunk(zu, zv, gvv, wm_ref, b_ref[...], n_heads)
            dy, dggo = _rms_backward(ua * s, ggo_ref[...], dy_ref[rows, :].astype(F32))
            dggo_ref[...] += dggo
            ds = dy * ua
            dsum_ref[...] += ds
            dsb = ds.astype(BF16)
            parts = []
            for h in range(n_heads):
                cols = slice(h * GMLP_HEAD, (h + 1) * GMLP_HEAD)
                dwm_ref[h] += _dot(dsb[:, cols], vb[:, cols], "nt")
                parts.append(_dot(wm_ref[h], dsb[:, cols], "tn"))
            dv = jnp.concatenate(parts, axis=1)
            dgv_ref[...] += jnp.sum(dv * vh, axis=0, keepdims=True)
            dvh = dv * gvv
            dvg = r * (dvh - jnp.mean(dvh, axis=-1, keepdims=True) - vh * jnp.mean(dvh * vh, axis=-1, keepdims=True))
            dz_ref[rows, pl.ds(2 * dg, dg)] = (dvg * _gelu_grad(zv)).astype(BF16)
            dz_ref[rows, pl.ds(dg, dg)] = (dy * s * _gelu_grad(zu)).astype(BF16)

    full = lambda a: pl.BlockSpec(a.shape, lambda i, nd=a.ndim: (0,) * nd)
    return pl.pallas_call(
        body,
        name="gmlp_bwd",
        grid=(t_dim // tr,),
        in_specs=[pl.BlockSpec((tr, dg), lambda i: (i, 1)), pl.BlockSpec((tr, dg), lambda i: (i, 2)),
                  pl.BlockSpec((tr, dg), lambda i: (i, 1)), full(gv), full(wm), full(bias), full(ggo)],
        out_specs=[pl.BlockSpec((tr, 3 * dg), lambda i: (i, 0)),
                   pl.BlockSpec((1, dg), lambda i: (0, 0)), pl.BlockSpec((1, dg), lambda i: (0, 0)),
                   pl.BlockSpec(wm.shape, lambda i: (0, 0, 0)), pl.BlockSpec((CHUNK, dg), lambda i: (0, 0))],
        out_shape=[jax.ShapeDtypeStruct((t_dim, 3 * dg), BF16),
                   jax.ShapeDtypeStruct((1, dg), F32), jax.ShapeDtypeStruct((1, dg), F32),
                   jax.ShapeDtypeStruct(wm.shape, F32), jax.ShapeDtypeStruct((CHUNK, dg), F32)],
        compiler_params=_params(("arbitrary",)),
    )(z, z, dycat, gv, wm, bias, ggo)


def _ple_head(npl, w_gate, h3, pp, tgt, g_final):
    t_dim, d = h3.shape
    tr = _tile(t_dim, 256, 16)

    def body(n_ref, w_ref, h_ref, pp_ref, t_ref, g_ref, dgq_ref, dpp_ref, dh_ref, dg_ref, loss_ref):
        @pl.when(pl.program_id(0) == 0)
        def _():
            dg_ref[...] = jnp.zeros_like(dg_ref)
            loss_ref[...] = jnp.zeros_like(loss_ref)

        gate = _sigmoid(_dot(n_ref[...], w_ref[...], "nn"))
        ppv = pp_ref[...]
        h4 = h_ref[...] + gate * ppv
        xh, _ = _rms_stats(h4)
        err = xh * g_ref[...] - t_ref[...]
        dh4, dg = _rms_backward(h4, g_ref[...], err * (1.0 / d))
        dh_ref[...] = dh4
        dgq_ref[...] = (dh4 * ppv * gate * (1.0 - gate)).astype(BF16)
        dpp_ref[...] = (dh4 * gate).astype(BF16)
        dg_ref[...] += dg
        loss_ref[...] += jnp.full((1, LANES), 0.5 * jnp.sum(err * err) * (1.0 / d), F32)

    rows = pl.BlockSpec((tr, d), lambda i: (i, 0))
    whole = lambda a: pl.BlockSpec(a.shape, lambda i: (0, 0))
    return pl.pallas_call(
        body,
        name="ple_head",
        grid=(t_dim // tr,),
        in_specs=[rows, whole(w_gate), rows, rows, rows, whole(g_final)],
        out_specs=[rows, rows, rows, pl.BlockSpec((1, d), lambda i: (0, 0)), pl.BlockSpec((1, LANES), lambda i: (0, 0))],
        out_shape=[jax.ShapeDtypeStruct((t_dim, d), BF16), jax.ShapeDtypeStruct((t_dim, d), BF16),
                   jax.ShapeDtypeStruct((t_dim, d), F32), jax.ShapeDtypeStruct((1, d), F32),
                   jax.ShapeDtypeStruct((1, LANES), F32)],
        compiler_params=_params(("arbitrary",)),
    )(npl, w_gate, h3, pp, tgt, g_final)


def _position():
    x, y, c = lax.axis_index("x"), lax.axis_index("y"), lax.axis_index("c")
    chips = [(1 - x, y), (x, 1 - y), (1 - x, 1 - y)]
    return x, y, c, chips


def _region(ref, kind, shard_shape, q, half, part=None):
    rs, cs = shard_shape
    r0, nr = (0, rs) if half is None else (half * (rs // 2), rs // 2)
    if part is not None:
        r0, nr = r0 + part * (rs // 4), rs // 4
    if kind == "row":
        return ref.at[pl.ds(q * rs + r0, nr), :]
    return ref.at[pl.ds(r0, nr), pl.ds(q * cs, cs)]


def _full_shape(kind, shard_shape):
    rs, cs = shard_shape
    return (N_CHIPS * rs, cs) if kind == "row" else (rs, N_CHIPS * cs)


def _remote(src, dst, send_sems, recv_sems, k, to):
    return pltpu.make_async_remote_copy(src_ref=src, dst_ref=dst, send_sem=send_sems.at[k], recv_sem=recv_sems.at[k],
                                        device_id=to, device_id_type=MESH)


def _same(arrays):
    return [jax.ShapeDtypeStruct(a.shape, a.dtype) for a in arrays]


def _gather_near_carry(gathered, kinds, shapes):
    nw = len(gathered)

    def copies(ops, full, send_sems, recv_sems):
        x, y, c, _ = _position()
        out = []
        for w in range(nw):
            mine = _region(full[w], kinds[w], shapes[w], 2 * x + y, c)
            out.append(_remote(mine, mine, send_sems, recv_sems, 2 * w, (1 - x, y, c)))
            out.append(_remote(mine, mine, send_sems, recv_sems, 2 * w + 1, (x, 1 - y, c)))
        return out

    return _Carry(gathered, _same(gathered), {i: i for i in range(nw)}, 2 * nw, copies)


def _gather_far_carry(gathered, kinds, shapes):
    nw = len(gathered)

    def copies(ops, full, send_sems, recv_sems):
        x, y, c, _ = _position()
        out = []
        for w in range(nw):
            from_x = _region(full[w], kinds[w], shapes[w], 2 * (1 - x) + y, c, part=1)
            from_y = _region(full[w], kinds[w], shapes[w], 2 * x + (1 - y), c, part=0)
            out.append(_remote(from_x, from_x, send_sems, recv_sems, 2 * w, (x, 1 - y, c)))
            out.append(_remote(from_y, from_y, send_sems, recv_sems, 2 * w + 1, (1 - x, y, c)))
        return out

    return _Carry(gathered, _same(gathered), {i: i for i in range(nw)}, 2 * nw, copies)


def _gather_d2d_carry(gathered, kinds, shapes):
    nw = len(gathered)

    def copies(ops, full, send_sems, recv_sems):
        x, y, c, chips = _position()
        out = []
        for w in range(nw):
            for j, (cx, cy) in enumerate(chips):
                landed = _region(full[w], kinds[w], shapes[w], 2 * cx + cy, c)
                out.append(_remote(landed, landed, send_sems, recv_sems, 3 * w + j, (x, y, 1 - c)))
        return out

    return _Carry(gathered, _same(gathered), {i: i for i in range(nw)}, 3 * nw, copies)


def _pairs_carry(grads, kinds, shapes):
    nw = len(grads)

    def copies(g, got, send_sems, recv_sems):
        x, y, c, _ = _position()
        out = []
        for w in range(nw):
            for q in range(N_CHIPS):
                out.append(_remote(_region(g[w], kinds[w], shapes[w], q, 1 - c), got[w].at[q], send_sems, recv_sems,
                                   N_CHIPS * w + q, (x, y, 1 - c)))
        return out

    outs = [jax.ShapeDtypeStruct((N_CHIPS, s[0] // 2, s[1]), BF16) for s in shapes]
    return _Carry(grads, outs, {}, N_CHIPS * nw, copies)


def _pair_sum(name, grad, got, kind, shard_shape, c_arr):
    rs, cs = shard_shape
    hr = rs // 2
    tr = _tile(hr, 512, 16)
    nr = hr // tr

    def body(c_ref, g_ref, s_ref, o_ref):
        o_ref[...] = (g_ref[...].astype(F32) + s_ref[...].astype(F32)).astype(BF16)

    if kind == "row":
        g_spec = pl.BlockSpec((tr, cs), lambda q, i, c_ref: (q * (rs // tr) + c_ref[0] * nr + i, 0))
    else:
        g_spec = pl.BlockSpec((tr, cs), lambda q, i, c_ref: (c_ref[0] * nr + i, q))
    blk = pl.BlockSpec((None, tr, cs), lambda q, i, c_ref: (q, i, 0))
    return pl.pallas_call(
        body,
        name=name,
        grid_spec=pltpu.PrefetchScalarGridSpec(num_scalar_prefetch=1, grid=(N_CHIPS, nr), in_specs=[g_spec, blk],
                                               out_specs=blk),
        out_shape=jax.ShapeDtypeStruct((N_CHIPS, hr, cs), BF16),
        compiler_params=_params(("parallel", "parallel")),
    )(c_arr, grad, got)


def _scatter_carry(sums, shapes):
    nw = len(sums)

    def copies(ps, got, send_sems, recv_sems):
        x, y, c, chips = _position()
        out = []
        for w in range(nw):
            for j, (cx, cy) in enumerate(chips):
                out.append(_remote(ps[w].at[2 * cx + cy], got[w].at[j], send_sems, recv_sems, 3 * w + j, (cx, cy, c)))
        return out

    outs = [jax.ShapeDtypeStruct((3, s[0] // 2, s[1]), BF16) for s in shapes]
    return _Carry(sums, outs, {}, 3 * nw, copies)


def _owner_sum(name, sums, got, shard_shape, qc_arr):
    rs, cs = shard_shape
    hr = rs // 2
    tr = _tile(hr, 512, 16)
    nr = hr // tr

    def body(qc_ref, mine_ref, got_ref, o_ref):
        acc = mine_ref[...].astype(F32)
        for j in range(3):
            acc = acc + got_ref[j].astype(F32)
        o_ref[...] = acc

    return pl.pallas_call(
        body,
        name=name,
        grid_spec=pltpu.PrefetchScalarGridSpec(
            num_scalar_prefetch=1, grid=(nr,),
            in_specs=[pl.BlockSpec((None, tr, cs), lambda i, qc_ref: (qc_ref[0], i, 0)),
                      pl.BlockSpec((3, tr, cs), lambda i, qc_ref: (0, i, 0))],
            out_specs=pl.BlockSpec((tr, cs), lambda i, qc_ref: (qc_ref[1] * nr + i, 0))),
        out_shape=jax.ShapeDtypeStruct((rs, cs), F32),
        compiler_params=_params(("parallel",)),
    )(qc_arr, sums, got)


def _share_carry(grads, shapes):
    nw = len(grads)

    def copies(ops, out, send_sems, recv_sems):
        x, y, c, _ = _position()
        res = []
        for w in range(nw):
            hr = shapes[w][0] // 2
            mine = out[w].at[pl.ds(c * hr, hr), :]
            res.append(_remote(mine, mine, send_sems, recv_sems, w, (x, y, 1 - c)))
        return res

    return _Carry(grads, _same(grads), {i: i for i in range(nw)}, nw, copies)


def _place_block(packed, me):
    return lax.dynamic_update_slice(jnp.zeros((N_DEV,) + packed.shape, F32), packed[None], (me, 0, 0))


def _exchange_carry(blocks):
    def copies(ops, res, send_sems, recv_sems):
        x, y, c, _ = _position()
        mine = res[0].at[4 * x + 2 * y + c]
        out = []
        for k in range(1, N_DEV):
            to = ((1 - x) if k & 4 else x, (1 - y) if k & 2 else y, (1 - c) if k & 1 else c)
            out.append(_remote(mine, mine, send_sems, recv_sems, k - 1, to))
        return out

    return _Carry([blocks], _same([blocks]), {0: 0}, N_DEV - 1, copies)


def _sum_blocks(name, blocks):
    n, rows, lanes = blocks.shape
    tr = _tile(rows, 4096, 8)

    def body(b_ref, o_ref):
        acc = b_ref[0]
        for k in range(1, n):
            acc = acc + b_ref[k]
        o_ref[...] = acc

    return pl.pallas_call(
        body,
        name=name,
        grid=(rows // tr,),
        in_specs=[pl.BlockSpec((n, tr, lanes), lambda i: (0, i, 0))],
        out_specs=pl.BlockSpec((tr, lanes), lambda i: (i, 0)),
        out_shape=jax.ShapeDtypeStruct((rows, lanes), F32),
        compiler_params=_params(("parallel",)),
    )(blocks)


def _adamw(name, w, g, m, v):
    def fn(rows, pars):
        wv, gv, mv, vv = rows
        m_new = ADAM_B1 * mv + (1.0 - ADAM_B1) * gv
        v_new = ADAM_B2 * vv + (1.0 - ADAM_B2) * (gv * gv)
        m_hat = m_new / (1.0 - ADAM_B1 ** ADAM_STEP)
        v_hat = v_new / (1.0 - ADAM_B2 ** ADAM_STEP)
        delta = -ADAM_LR * (m_hat / (jnp.sqrt(v_hat) + ADAM_EPS) + ADAM_WD * wv)
        return [delta, m_new, v_new, gv], []

    c = w.shape[1]
    return _rowwise(name, fn, [w, g, m, v], [], [(c, F32)] * 4, tr=256)


def _pack(arrays):
    rows = []
    for a in arrays:
        flat = a.reshape(-1).astype(F32)
        pad = (-flat.shape[0]) % LANES
        rows.append(jnp.pad(flat, (0, pad)).reshape(-1, LANES))
    stacked = jnp.concatenate(rows, axis=0)
    pad_rows = (-stacked.shape[0]) % 8
    return jnp.pad(stacked, ((0, pad_rows), (0, 0)))


def _unpack(packed, shapes):
    out, r = [], 0
    for s in shapes:
        n = math.prod(s)
        nr = -(-n // LANES)
        out.append(packed[r:r + nr].reshape(-1)[:n].reshape(s))
        r += nr
    return out


BIG = ["w1_gate", "w1_up", "w1_down", "w_in", "ssm_w_glu", "w_out", "w2_gate", "w2_up", "w2_down", "w_ple_gate",
       "w_ple_proj"]
KIND = {"w1_gate": "col", "w1_up": "col", "w1_down": "row", "w_in": "col", "ssm_w_glu": "row", "w_out": "row",
        "w2_gate": "col", "w2_up": "col", "w2_down": "row", "w_ple_gate": "row", "w_ple_proj": "col"}
SMALL = ["norm_ffn1", "norm_mix", "ssm_log_dt", "ssm_a_re", "ssm_a_im", "ssm_b_re", "ssm_b_im", "ssm_c_re", "ssm_c_im",
         "ssm_d", "gmlp_norm_v", "gmlp_w_s", "gmlp_b_s", "norm_ssm_out", "norm_gmlp_out", "norm_ffn2", "norm_ple",
         "norm_final"]
WEIGHTS = ["norm_ffn1", "w1_gate", "w1_up", "w1_down", "norm_mix", "w_in", "ssm_log_dt", "ssm_a_re", "ssm_a_im",
           "ssm_b_re", "ssm_b_im", "ssm_c_re", "ssm_c_im", "ssm_d", "ssm_w_glu", "gmlp_norm_v", "gmlp_w_s", "gmlp_b_s",
           "norm_ssm_out", "norm_gmlp_out", "w_out", "norm_ffn2", "w2_gate", "w2_up", "w2_down", "norm_ple",
           "w_ple_gate", "w_ple_proj", "norm_final"]


class _Trip:
    def __init__(self, names, arrays, carry):
        self.names, self.arrays, self.carry = names, arrays, carry


class _Reducer:
    def __init__(self, shard_shape, c_arr, qc_arr):
        self.shard_shape, self.c_arr, self.qc_arr = shard_shape, c_arr, qc_arr
        self.halves = {}

    def swap(self, names, grads):
        kinds = [KIND[n] for n in names]
        shapes = [self.shard_shape[n] for n in names]
        return _Trip(names, grads, _pairs_carry(grads, kinds, shapes))

    def send(self, trip, swapped):
        shapes = [self.shard_shape[n] for n in trip.names]
        sums = [_pair_sum("pair_sum_" + n, g, s, KIND[n], sh, self.c_arr)
                for n, g, s, sh in zip(trip.names, trip.arrays, swapped, shapes)]
        return _Trip(trip.names, sums, _scatter_carry(sums, shapes))

    def end(self, trip, got):
        for n, ps, g in zip(trip.names, trip.arrays, got):
            self.halves[n] = _owner_sum("owner_sum_" + n, ps, g, self.shard_shape[n], self.qc_arr)


def _ride(*carries):
    present = [c for c in carries if c is not None]
    joined = functools.reduce(_join, present) if present else None

    def split(results):
        out, at = [], 0
        for c in carries:
            n = len(c.out_shapes) if c is not None else 0
            out.append(list(results[at:at + n]))
            at += n
        return out

    return joined, split


def _step(x, p, tgt, w, m, v):
    d_model = x.shape[1]
    d_ssm = w["ssm_d"].shape[1]
    n_groups = d_ssm // SSM_GROUP
    row = lambda a: a.reshape(1, -1)

    xi, yi, ci = lax.axis_index("x"), lax.axis_index("y"), lax.axis_index("c")
    c_arr = jnp.reshape(ci, (1,)).astype(jnp.int32)
    q_arr = jnp.reshape(2 * xi + yi, (1,)).astype(jnp.int32)
    qc_arr = jnp.stack([2 * xi + yi, ci]).astype(jnp.int32)
    shard_shape = {n: w[n].shape for n in BIG}
    full = {n: _cast_into_gathered("cast_" + n, w[n], KIND[n], q_arr) for n in BIG}

    def gather(stage, names):
        return stage([full[n] for n in names], [KIND[n] for n in names], [shard_shape[n] for n in names])

    def gathered(names, arrays):
        full.update(zip(names, arrays))

    groups = [["w1_gate"], ["w1_up"], ["w1_down"], ["w_in"], ["w2_gate"], ["ssm_w_glu", "w_out"], ["w2_up"],
              ["w2_down", "w_ple_gate", "w_ple_proj"]]
    near, far, d2d = _gather_near_carry, _gather_far_carry, _gather_d2d_carry

    def stages(*work):
        carries = [gather(stage, groups[g]) for stage, g in work]
        return functools.reduce(_join, carries), [n for _, g in work for n in groups[g]]

    def alone(name, *work):
        carry, names = stages(*work)
        gathered(names, _comm_call(name, carry))

    alone("gather_a", (near, 0))
    alone("gather_b", (far, 0), (near, 1))
    carry, names = stages((d2d, 0), (far, 1))
    n1, landed = _rms_fwd("ffn1_norm", x, w["norm_ffn1"], carry)
    gathered(names, landed)
    carry, names = stages((d2d, 1), (near, 2), (near, 3))
    gate1, landed = _mm_nn("ffn1_gate", n1, full["w1_gate"], BF16, tm=1024, tn=512, tk=2048, carry=carry)
    gathered(names, landed)
    carry, names = stages((far, 2), (far, 3), (near, 4))
    (a1, da_dgate1, da_dup1), landed = _ffn_up("ffn1", n1, full["w1_up"], gate1, carry)
    gathered(names, landed)
    alone("gather_d", (d2d, 2))
    carry, names = stages((d2d, 3), (far, 4), (near, 5))
    h1, landed = _ffn_down("ffn1", a1, full["w1_down"], x, carry)
    gathered(names, landed)
    ffn1 = (n1, a1, da_dgate1, da_dup1)
    nm = _rms_fwd("mix_norm", h1, w["norm_mix"])
    carry, names = stages((d2d, 4), (far, 5), (near, 6))
    z, landed = _mm_nn("in_proj", nm, full["w_in"], F32, tm=1024, tn=512, tk=2048, carry=carry)
    gathered(names, landed)

    disc, disc_vjp = jax.vjp(_ssm_discretize, w["ssm_log_dt"][0], w["ssm_a_re"], w["ssm_a_im"], w["ssm_b_re"],
                             w["ssm_b_im"])
    abar_r, abar_i, bbar_r, bbar_i = disc
    nb = n_groups // GROUPS_PER_BLOCK
    wb = jnp.concatenate([_blockdiag_in(bbar_r), _blockdiag_in(bbar_i)], axis=-1).astype(BF16)
    wc = jnp.concatenate([_blockdiag_out(w["ssm_c_re"]), -_blockdiag_out(w["ssm_c_im"])], axis=1).astype(BF16)
    abar = jnp.concatenate([abar_r.reshape(nb, -1), abar_i.reshape(nb, -1)], axis=-1)
    abar_conj = jnp.concatenate([abar_r.reshape(nb, -1), -abar_i.reshape(nb, -1)], axis=-1)
    carry, names = stages((d2d, 5), (far, 6), (near, 7))
    (states, y_pre, yg), landed = _ssm_scan_fwd(z, wb, wc, abar, w["ssm_d"], carry)
    gathered(names, landed)
    q = _mm_nn("glu_proj", yg, full["ssm_w_glu"], F32, tm=1024, tn=1024, tk=1024)

    def glu_norm(rows, pars):
        yv = _gelu(rows[0]) * _sigmoid(rows[1])
        yh, _ = _rms_stats(yv)
        return [yh * pars[0]], []

    yn_ssm = _rowwise("ssm_glu_norm", glu_norm, [y_pre, q], [w["norm_ssm_out"]], [(d_ssm, BF16)])[0]

    tril = jnp.tril(jnp.ones((CHUNK, CHUNK), dtype=bool))
    wm = jnp.where(tril[None], w["gmlp_w_s"], 0.0).astype(BF16)
    bias = jnp.repeat(w["gmlp_b_s"].T, GMLP_HEAD, axis=1)
    yn_gmlp = _gmlp_fwd(z, w["gmlp_norm_v"], wm, bias, w["norm_gmlp_out"])
    ycat = jnp.concatenate([yn_ssm, yn_gmlp], axis=1)
    carry, names = stages((d2d, 6), (far, 7))
    h2, landed = _mm_nn("out_proj", ycat, full["w_out"], F32, res=h1, alpha=1.0, tm=512, tn=1024, tk=2048,
                        carry=carry)
    gathered(names, landed)

    n2 = _rms_fwd("ffn2_norm", h2, w["norm_ffn2"])
    carry, names = stages((d2d, 7))
    (a2, da_dgate2, da_dup2), landed = _ffn_gateup("ffn2", n2, full["w2_gate"], full["w2_up"], carry)
    gathered(names, landed)
    h3 = _ffn_down("ffn2", a2, full["w2_down"], h2)
    ffn2 = (n2, a2, da_dgate2, da_dup2)
    npl = _rms_fwd("ple_norm", h3, w["norm_ple"])
    pp = _mm_nn("ple_proj", p, full["w_ple_proj"], F32, tm=1024, tn=1024, tk=2048)
    dgq, dpp, dh4, g_norm_final, loss_part = _ple_head(npl, full["w_ple_gate"], h3, pp, tgt, row(w["norm_final"]))
    reducer = _Reducer(shard_shape, c_arr, qc_arr)
    (g_w_ple_proj,) = _mm_tn("ple_dwproj", p, [dpp], BF16, tm=256, tn=1024, tk=4096)
    (g_w_ple_gate,) = _mm_tn("ple_dwgate", npl, [dgq], BF16, tm=512, tn=1024, tk=4096)
    ple = reducer.swap(["w_ple_gate", "w_ple_proj"], [g_w_ple_gate, g_w_ple_proj])
    dnpl, swapped = _mm_nt_sum("ple_dnorm_in", [dgq], [full["w_ple_gate"]], BF16, tm=512, tn=1024, tk=2048,
                               carry=ple.carry)
    ple = reducer.send(ple, swapped)
    dh3, dh3_b, g_norm_ple = _rms_bwd("ple_dnorm", h3, w["norm_ple"], [dnpl], dh4, 0.5)

    dh2, dh2_b, g_norm_ffn2, got, _, up2_w = _ffn_bwd(
        "ffn2", ["w2_gate", "w2_up", "w2_down"], h2, w["norm_ffn2"], full["w2_gate"], full["w2_up"], full["w2_down"],
        ffn2, dh3, dh3_b, 1.0, reducer, riding=ple.carry, last_hop_later=True)
    reducer.end(ple, got)

    dycat = _mm_nt_sum("out_dproj", [dh2_b], [full["w_out"]], BF16, tm=512, tn=1024, tk=2048)
    (g_w_out,) = _mm_tn("out_dw", ycat, [dh2_b], BF16, tm=512, tn=1024, tk=4096)

    dz, g_norm_gmlp_out, g_gmlp_norm_v, g_wm, g_s = _gmlp_bwd(z, dycat, w["gmlp_norm_v"], wm, bias,
                                                                  w["norm_gmlp_out"])
    g_gmlp_w_s = jnp.where(tril[None], g_wm, 0.0)
    g_gmlp_b_s = g_s.reshape(CHUNK, -1, GMLP_HEAD).sum(axis=-1).T

    def glu_bwd(rows, pars):
        dyn, ypre, qv = rows
        ygv = _gelu(ypre)
        sg = _sigmoid(qv)
        dy, dg = _rms_backward(ygv * sg, pars[0], dyn.astype(F32))
        return [dy * ygv * sg * (1.0 - sg), dy * sg], [dg]

    dq, dyg_part, g_norm_ssm_out = _rowwise("ssm_dglu", glu_bwd, [(dycat, d_ssm, 0), y_pre, q], [w["norm_ssm_out"]],
                                            [(d_ssm, BF16), (d_ssm, F32)], [(1, d_ssm)])
    dyg_proj = _mm_nt_sum("glu_dproj", [dq], [full["ssm_w_glu"]], F32, tm=1024, tn=1024, tk=1024)
    (g_ssm_w_glu,) = _mm_tn("glu_dw", yg, [dq], BF16, tm=512, tn=1024, tk=4096)

    def gelu_bwd(rows, pars):
        return [(rows[0] + rows[1]) * _gelu_grad(rows[2])], []

    dy_pre = _rowwise("ssm_dgelu", gelu_bwd, [dyg_part, dyg_proj, y_pre], [], [(d_ssm, F32)])[0]
    mixers = reducer.swap(["w_out", "ssm_w_glu"], [g_w_out, g_ssm_w_glu])
    me = 4 * xi + 2 * yi + ci
    small = {"gmlp_norm_v": g_gmlp_norm_v, "gmlp_w_s": g_gmlp_w_s, "gmlp_b_s": g_gmlp_b_s,
             "norm_gmlp_out": g_norm_gmlp_out, "norm_ssm_out": g_norm_ssm_out, "norm_ffn2": g_norm_ffn2,
             "norm_ple": g_norm_ple, "norm_final": g_norm_final}
    before_scan = [n for n in SMALL if n in small]
    scan_blocks = _place_block(_pack([small[n] for n in before_scan] + [loss_part[:, :1]]), me)
    carry, split = _ride(up2_w.carry, mixers.carry, _exchange_carry(scan_blocks))
    (dz, g_wb, g_wc, g_abar, g_ssm_d), results = _ssm_scan_bwd(z, dy_pre, states, wb, wc, abar_conj, w["ssm_d"],
                                                              dz, carry)
    got, swapped, (scan_blocks,) = split(results)
    reducer.end(up2_w, got)
    mixers = reducer.send(mixers, swapped)
    g_abar = jnp.transpose(g_abar, (1, 0, 2)).reshape(nb, -1)
    sw = g_abar.shape[-1] // 2
    g_bbar_r = _blockdiag_in_grad(g_wb[..., :sw], SSM_STATE, SSM_GROUP)
    g_bbar_i = _blockdiag_in_grad(g_wb[..., sw:], SSM_STATE, SSM_GROUP)
    g_ssm_c_re = _blockdiag_out_grad(g_wc[:, :sw, :], SSM_GROUP, SSM_STATE)
    g_ssm_c_im = -_blockdiag_out_grad(g_wc[:, sw:, :], SSM_GROUP, SSM_STATE)
    g_abar_r = g_abar[..., :sw].reshape(n_groups, SSM_STATE)
    g_abar_i = g_abar[..., sw:].reshape(n_groups, SSM_STATE)
    g_ssm_log_dt, g_ssm_a_re, g_ssm_a_im, g_ssm_b_re, g_ssm_b_im = disc_vjp((g_abar_r, g_abar_i, g_bbar_r, g_bbar_i))

    (g_w_in,), got = _mm_tn("in_dw", nm, [dz], BF16, tm=512, tn=1536, tk=4096, carry=mixers.carry)
    reducer.end(mixers, got)
    in_w = reducer.swap(["w_in"], [g_w_in])
    dnm, swapped = _mm_nt_sum("in_dproj", [dz], [full["w_in"]], BF16, tm=1024, tn=1024, tk=3072, carry=in_w.carry)
    in_w = reducer.send(in_w, swapped)
    dh1, dh1_b, g_norm_mix = _rms_bwd("mix_dnorm", h1, w["norm_mix"], [dnm], dh2, 0.5)
    small = {"norm_mix": g_norm_mix, "ssm_log_dt": g_ssm_log_dt, "ssm_a_re": g_ssm_a_re,
             "ssm_a_im": g_ssm_a_im, "ssm_b_re": g_ssm_b_re, "ssm_b_im": g_ssm_b_im, "ssm_c_re": g_ssm_c_re,
             "ssm_c_im": g_ssm_c_im, "ssm_d": g_ssm_d}
    after_scan = [n for n in SMALL if n in small]
    ffn_blocks = _place_block(_pack([small[n] for n in after_scan]), me)
    done = [n for n in BIG if n in reducer.halves]
    carry, split = _ride(in_w.carry, _share_carry([reducer.halves[n] for n in done], [shard_shape[n] for n in done]))
    dx, _, g_norm_ffn1, results, (ffn_blocks,), _ = _ffn_bwd(
        "ffn1", ["w1_gate", "w1_up", "w1_down"], x, w["norm_ffn1"], full["w1_gate"], full["w1_up"], full["w1_down"],
        ffn1, dh1, dh1_b, 1.0, reducer, riding=carry, riding_dwd=_exchange_carry(ffn_blocks))
    got, shared = split(results)
    grad = dict(zip(done, shared))
    reducer.end(in_w, got)
    scan_grads = _unpack(_sum_blocks("sum_before_scan", scan_blocks), [w[n].shape for n in before_scan] + [(1,)])
    loss = scan_grads[-1].reshape(())
    early = before_scan + after_scan
    early_grads = scan_grads[:-1] + _unpack(_sum_blocks("sum_after_scan", ffn_blocks), [w[n].shape for n in after_scan])

    rest = [n for n in BIG if n not in done]
    late_blocks = _place_block(_pack([g_norm_ffn1]), me)
    last = _join(_share_carry([reducer.halves[n] for n in rest], [shard_shape[n] for n in rest]),
                 _exchange_carry(late_blocks))
    *shared, late_blocks = _comm_call("share_halves", last)
    grad.update(zip(rest, shared))
    grad.update(zip(early, early_grads))
    grad["norm_ffn1"] = _unpack(_sum_blocks("sum_first_norm", late_blocks), [w["norm_ffn1"].shape])[0]

    small_shapes = [w[n].shape for n in SMALL]
    delta, new_m, new_v = {}, {}, {}
    for n in BIG:
        delta[n], new_m[n], new_v[n], grad[n] = _adamw("adamw_" + n, w[n], grad[n], m[n], v[n])
    d_p, m_p, v_p, _ = _adamw("adamw_small", _pack([w[n] for n in SMALL]), _pack([grad[n] for n in SMALL]),
                              _pack([m[n] for n in SMALL]), _pack([v[n] for n in SMALL]))
    for name_list, packed in ((delta, d_p), (new_m, m_p), (new_v, v_p)):
        for n, a in zip(SMALL, _unpack(packed, small_shapes)):
            name_list[n] = a
    return loss, dx, grad, delta, new_m, new_v


def kernel(x, p, norm_ffn1, w1_gate, w1_up, w1_down, norm_mix, w_in, ssm_log_dt, ssm_a_re, ssm_a_im, ssm_b_re, ssm_b_im, ssm_c_re, ssm_c_im, ssm_d, ssm_w_glu, gmlp_norm_v, gmlp_w_s, gmlp_b_s, norm_ssm_out, norm_gmlp_out, w_out, norm_ffn2, w2_gate, w2_up, w2_down, norm_ple, w_ple_gate, w_ple_proj, norm_final, loss_target, m_norm_ffn1, m_w1_gate, m_w1_up, m_w1_down, m_norm_mix, m_w_in, m_ssm_log_dt, m_ssm_a_re, m_ssm_a_im, m_ssm_b_re, m_ssm_b_im, m_ssm_c_re, m_ssm_c_im, m_ssm_d, m_ssm_w_glu, m_gmlp_norm_v, m_gmlp_w_s, m_gmlp_b_s, m_norm_ssm_out, m_norm_gmlp_out, m_w_out, m_norm_ffn2, m_w2_gate, m_w2_up, m_w2_down, m_norm_ple, m_w_ple_gate, m_w_ple_proj, m_norm_final, v_norm_ffn1, v_w1_gate, v_w1_up, v_w1_down, v_norm_mix, v_w_in, v_ssm_log_dt, v_ssm_a_re, v_ssm_a_im, v_ssm_b_re, v_ssm_b_im, v_ssm_c_re, v_ssm_c_im, v_ssm_d, v_ssm_w_glu, v_gmlp_norm_v, v_gmlp_w_s, v_gmlp_b_s, v_norm_ssm_out, v_norm_gmlp_out, v_w_out, v_norm_ffn2, v_w2_gate, v_w2_up, v_w2_down, v_norm_ple, v_w_ple_gate, v_w_ple_proj, v_norm_final):
    given = dict(locals())
    shapes = {n: given[n].shape for n in WEIGHTS}

    def block(name):
        a = given[name]
        if a.ndim == 1:
            return a.reshape(1, -1)
        return a[0] if a.ndim >= 3 else a

    w = {n: block(n) for n in WEIGHTS}
    m = {n: block("m_" + n) for n in WEIGHTS}
    v = {n: block("v_" + n) for n in WEIGHTS}
    loss, dx, grad, delta, new_m, new_v = _step(x[0], p[0, 0], loss_target[0], w, m, v)
    outs = [loss, dx[None]]
    for tree in (grad, delta, new_m, new_v):
        outs += [tree[n].reshape(shapes[n]) for n in WEIGHTS]
    return tuple(outs)
```

```python
import functools
import math

import jax
import jax.numpy as jnp
from jax import lax
from jax.experimental import pallas as pl
from jax.experimental.pallas import tpu as pltpu

F32 = jnp.float32
BF16 = jnp.bfloat16
EPS = 1e-6
SSM_GROUP = 16
SSM_STATE = 64
GROUPS_PER_BLOCK = 8
GMLP_HEAD = 128
CHUNK = 128
ADAM_LR = 0.001
ADAM_B1 = 0.9
ADAM_B2 = 0.999
ADAM_EPS = 1e-08
ADAM_WD = 0.01
ADAM_STEP = 10
N_CHIPS = 4
N_DEV = 8
LANES = 128
VMEM_LIMIT_BYTES = 56 * 1024 * 1024
MESH = pl.DeviceIdType.MESH
GELU_C = math.sqrt(2.0 / math.pi)
GELU_A = 0.044715

_DOT_DIMS = {
    "nn": (((1,), (0,)), ((), ())),
    "nt": (((1,), (1,)), ((), ())),
    "tn": (((0,), (0,)), ((), ())),
}


def _tile(dim, pref, align):
    if dim <= pref:
        return dim
    t = (pref // align) * align
    while t >= align:
        if dim % t == 0:
            return t
        t -= align
    return dim


def _params(semantics):
    return pltpu.CompilerParams(dimension_semantics=semantics, vmem_limit_bytes=VMEM_LIMIT_BYTES)


def _gelu(x):
    return 0.5 * x * (1.0 + jnp.tanh(GELU_C * (x + GELU_A * x * x * x)))


def _gelu_grad(x):
    t = jnp.tanh(GELU_C * (x + GELU_A * x * x * x))
    return 0.5 * (1.0 + t) + 0.5 * x * (1.0 - t * t) * GELU_C * (1.0 + 3.0 * GELU_A * x * x)


def _sigmoid(x):
    return 1.0 / (1.0 + jnp.exp(-x))


def _dot(a, b, mode):
    return lax.dot_general(a.astype(BF16), b.astype(BF16), _DOT_DIMS[mode], preferred_element_type=F32)


class _Carry:
    def __init__(self, arrays, out_shapes, aliases, n_copies, copies):
        self.arrays = list(arrays)
        self.out_shapes = list(out_shapes)
        self.aliases = dict(aliases)
        self.n_copies = n_copies
        self.copies = copies

    def scratch(self):
        return [pltpu.SemaphoreType.DMA((self.n_copies,)), pltpu.SemaphoreType.DMA((self.n_copies,))]

    def split(self, refs):
        n_in, n_out = len(self.arrays), len(self.out_shapes)
        return refs[:n_in], refs[n_in:n_in + n_out], refs[n_in + n_out], refs[n_in + n_out + 1]

    def start(self, refs):
        for cp in self.copies(*self.split(refs)):
            cp.start()

    def wait(self, refs):
        for cp in self.copies(*self.split(refs)):
            cp.wait()


class _SemRange:
    def __init__(self, sems, offset):
        self.sems, self.offset = sems, offset

    @property
    def at(self):
        return self

    def __getitem__(self, k):
        return self.sems.at[self.offset + k]


def _join(first, second):
    n_in, n_out = len(first.arrays), len(first.out_shapes)
    aliases = dict(first.aliases)
    aliases.update({n_in + i: n_out + o for i, o in second.aliases.items()})

    def copies(ops, res, send_sems, recv_sems):
        return (first.copies(ops[:n_in], res[:n_out], send_sems, recv_sems)
                + second.copies(ops[n_in:], res[n_out:], _SemRange(send_sems, first.n_copies),
                                _SemRange(recv_sems, first.n_copies)))

    return _Carry(first.arrays + second.arrays, first.out_shapes + second.out_shapes, aliases,
                  first.n_copies + second.n_copies, copies)


_ANY = pl.BlockSpec(memory_space=pl.ANY)


def _comm_call(name, carry):
    def body(*refs):
        carry.start(refs)
        carry.wait(refs)

    n_in = len(carry.arrays)
    return pl.pallas_call(
        body,
        name=name,
        in_specs=[_ANY] * n_in,
        out_specs=[_ANY] * len(carry.out_shapes),
        out_shape=carry.out_shapes,
        input_output_aliases=carry.aliases,
        scratch_shapes=carry.scratch(),
    )(*carry.arrays)


def _carried_call(body, carry, *, name, grid, in_specs, out_specs, out_shape, scratch_shapes, semantics, args,
                  aliases=None):
    aliases = dict(aliases or {})
    if carry is None:
        res = pl.pallas_call(body, name=name, grid=grid, in_specs=in_specs, out_specs=out_specs, out_shape=out_shape,
                             scratch_shapes=scratch_shapes, input_output_aliases=aliases,
                             compiler_params=_params(semantics))(*args)
        return res, []
    n_in, n_out, n_scr = len(in_specs), len(out_specs), len(scratch_shapes)
    nci, nco = len(carry.arrays), len(carry.out_shapes)

    def wrapped(*refs):
        ins = refs[:n_in]
        outs = refs[n_in + nci:n_in + nci + n_out]
        scr = refs[n_in + nci + n_out + nco:n_in + nci + n_out + nco + n_scr]
        c_refs = (refs[n_in:n_in + nci] + refs[n_in + nci + n_out:n_in + nci + n_out + nco]
                  + refs[n_in + nci + n_out + nco + n_scr:])
        first = functools.reduce(jnp.logical_and, [pl.program_id(d) == 0 for d in range(len(grid))])
        last = functools.reduce(jnp.logical_and, [pl.program_id(d) == grid[d] - 1 for d in range(len(grid))])

        @pl.when(first)
        def _():
            carry.start(c_refs)

        body(*ins, *outs, *scr)

        @pl.when(last)
        def _():
            carry.wait(c_refs)

    res = pl.pallas_call(
        wrapped,
        name=name,
        grid=grid,
        in_specs=list(in_specs) + [_ANY] * nci,
        out_specs=list(out_specs) + [_ANY] * nco,
        out_shape=list(out_shape) + carry.out_shapes,
        input_output_aliases={**aliases, **{n_in + i: n_out + o for i, o in carry.aliases.items()}},
        scratch_shapes=list(scratch_shapes) + carry.scratch(),
        compiler_params=_params(("arbitrary",) * len(grid)),
    )(*args, *carry.arrays)
    return res[:n_out], res[n_out:]


def _matmul(name, mode, a_list, b_list, products, out_dtypes, epilogue, extras=(), tm=512, tn=512, tk=2048,
            carry=None, n_part=(0, 1)):
    a0, b0 = a_list[0], b_list[0]
    if mode == "tn":
        k_dim, m_dim = a0.shape
    else:
        m_dim, k_dim = a0.shape
    n_dim = (b0.shape[0] if mode == "nt" else b0.shape[1]) // n_part[1]
    tm = _tile(m_dim, tm, LANES)
    tn = _tile(n_dim, tn, LANES)
    tk = _tile(k_dim, tk, LANES)
    nk = k_dim // tk
    j0 = n_part[0] * (n_dim // tn)
    chunk = 2 * LANES if (nk == 1 and epilogue is not _identity and tn % (2 * LANES) == 0) else tn
    n_acc = 1 + max(p[2] for p in products)
    na, nb, ne, no = len(a_list), len(b_list), len(extras), len(out_dtypes)

    if mode == "tn":
        a_spec = pl.BlockSpec((tk, tm), lambda i, j, k: (k, i))
    else:
        a_spec = pl.BlockSpec((tm, tk), lambda i, j, k: (i, k))
    if mode == "nt":
        b_spec = pl.BlockSpec((tn, tk), lambda i, j, k: (j0 + j, k))
    else:
        b_spec = pl.BlockSpec((tk, tn), lambda i, j, k: (k, j0 + j))
    t_spec = pl.BlockSpec((tm, tn), lambda i, j, k: (i, j))

    def body(*refs):
        a_refs = refs[:na]
        b_refs = refs[na:na + nb]
        e_refs = refs[na + nb:na + nb + ne]
        o_refs = refs[na + nb + ne:na + nb + ne + no]
        acc_refs = refs[na + nb + ne + no:]

        def partial_sums(cols):
            sums = [None] * n_acc
            for ai, bi, ci in products:
                b = b_refs[bi][cols, :] if mode == "nt" else b_refs[bi][:, cols]
                d = _dot(a_refs[ai][...], b, mode)
                sums[ci] = d if sums[ci] is None else sums[ci] + d
            return sums

        def finish(accs, cols):
            outs = epilogue(accs, [e[:, cols] for e in e_refs])
            for o_ref, o in zip(o_refs, outs):
                o_ref[:, cols] = o.astype(o_ref.dtype)

        if nk == 1:
            for c0 in range(0, tn, chunk):
                finish(partial_sums(slice(c0, c0 + chunk)), slice(c0, c0 + chunk))
        else:
            sums = partial_sums(slice(None))
            finish = functools.partial(finish, cols=slice(None))
            k = pl.program_id(2)

            @pl.when(k == 0)
            def _():
                for acc, s in zip(acc_refs, sums):
                    acc[...] = s

            @pl.when(k > 0)
            def _():
                for acc, s in zip(acc_refs, sums):
                    acc[...] += s

            @pl.when(k == nk - 1)
            def _():
                finish([acc[...] for acc in acc_refs])

    scratch = [pltpu.VMEM((tm, tn), F32) for _ in range(n_acc)] if nk > 1 else []
    outs, carried = _carried_call(
        body, carry,
        name=name,
        grid=(m_dim // tm, n_dim // tn, nk),
        in_specs=[a_spec] * na + [b_spec] * nb + [t_spec] * ne,
        out_specs=[t_spec] * no,
        out_shape=[jax.ShapeDtypeStruct((m_dim, n_dim), dt) for dt in out_dtypes],
        scratch_shapes=scratch,
        semantics=("parallel", "parallel", "arbitrary"),
        args=[*a_list, *b_list, *extras],
    )
    return (outs, carried) if carry else outs


def _identity(accs, extras):
    return accs


def _single(result, carry):
    return (result[0][0], result[1]) if carry else result[0]


def _mm_nn(name, a, b, out_dtype, res=None, alpha=1.0, carry=None, **tiles):
    if res is None:
        return _single(_matmul(name, "nn", [a], [b], [(0, 0, 0)], [out_dtype], _identity, carry=carry, **tiles), carry)

    def epilogue(accs, extras):
        return [extras[0] + alpha * accs[0]]

    return _single(_matmul(name, "nn", [a], [b], [(0, 0, 0)], [out_dtype], epilogue, extras=(res,), carry=carry,
                           **tiles), carry)


def _mm_nt_sum(name, a_list, b_list, out_dtype, carry=None, n_part=(0, 1), **tiles):
    products = [(i, i, 0) for i in range(len(a_list))]
    return _single(_matmul(name, "nt", a_list, b_list, products, [out_dtype], _identity, carry=carry, n_part=n_part,
                           **tiles), carry)


def _mm_tn(name, a, b_list, out_dtype, carry=None, **tiles):
    products = [(0, i, i) for i in range(len(b_list))]
    return _matmul(name, "tn", [a], b_list, products, [out_dtype] * len(b_list), _identity, carry=carry, **tiles)


def _rowwise(name, fn, row_ins, par_ins, row_outs, acc_outs=(), tr=512, carry=None):
    first = row_ins[0][0] if isinstance(row_ins[0], tuple) else row_ins[0]
    t_dim = first.shape[0]
    tr = _tile(t_dim, tr, 16)
    arrays, specs = [], []
    for r in row_ins:
        if isinstance(r, tuple):
            arr, width, blk = r
            specs.append(pl.BlockSpec((tr, width), lambda i, blk=blk: (i, blk)))
        else:
            arr = r
            specs.append(pl.BlockSpec((tr, arr.shape[1]), lambda i: (i, 0)))
        arrays.append(arr)
    for p in par_ins:
        arrays.append(p)
        specs.append(pl.BlockSpec(p.shape, lambda i, nd=p.ndim: (0,) * nd))
    nr, npar, nro, nacc = len(row_ins), len(par_ins), len(row_outs), len(acc_outs)

    def body(*refs):
        rows = [r[...] for r in refs[:nr]]
        pars = [p[...] for p in refs[nr:nr + npar]]
        o_refs = refs[nr + npar:nr + npar + nro]
        acc_refs = refs[nr + npar + nro:]
        outs, accs = fn(rows, pars)
        for o_ref, o in zip(o_refs, outs):
            o_ref[...] = o.astype(o_ref.dtype)
        if nacc:
            @pl.when(pl.program_id(0) == 0)
            def _():
                for a_ref in acc_refs:
                    a_ref[...] = jnp.zeros_like(a_ref)

            for a_ref, a in zip(acc_refs, accs):
                a_ref[...] += a

    out_shape = [jax.ShapeDtypeStruct((t_dim, c), dt) for c, dt in row_outs]
    out_shape += [jax.ShapeDtypeStruct(s, F32) for s in acc_outs]
    out_specs = [pl.BlockSpec((tr, c), lambda i: (i, 0)) for c, _ in row_outs]
    out_specs += [pl.BlockSpec(s, lambda i: (0, 0)) for s in acc_outs]
    res, carried = _carried_call(body, carry, name=name, grid=(t_dim // tr,), in_specs=specs, out_specs=out_specs,
                                 out_shape=out_shape, scratch_shapes=[], semantics=("arbitrary",), args=arrays)
    return (res, carried) if carry else res


def _rms_stats(x):
    r = lax.rsqrt(jnp.mean(x * x, axis=-1, keepdims=True) + EPS)
    return x * r, r


def _rms_backward(x, g, dy):
    xh, r = _rms_stats(x)
    a = dy * g
    dx = r * (a - xh * jnp.mean(a * xh, axis=-1, keepdims=True))
    return dx, jnp.sum(dy * xh, axis=0, keepdims=True)


def _rms_fwd(name, x, g, carry=None):
    def fn(rows, pars):
        xh, _ = _rms_stats(rows[0])
        return [xh * pars[0]], []

    res = _rowwise(name, fn, [x], [g], [(x.shape[1], BF16)], carry=carry)
    return (res[0][0], res[1]) if carry else res[0]


def _rms_bwd(name, x, g, dy_parts, dres, scale, carry=None):
    def fn(rows, pars):
        dy = (rows[2] if len(rows) == 3 else jnp.concatenate(rows[2:], axis=1)).astype(F32)
        dx, dg = _rms_backward(rows[0], pars[0], dy)
        tot = rows[1] + dx
        return [tot, scale * tot], [dg]

    d = x.shape[1]
    return _rowwise(name, fn, [x, dres, *dy_parts], [g], [(d, F32), (d, BF16)], [(1, d)], tr=256, carry=carry)


def _cast_into_gathered(name, w, kind, q_arr):
    rs, cs = w.shape
    tr = _tile(rs, 256, 16)
    nr = rs // tr

    def body(q_ref, w_ref, o_ref):
        o_ref[...] = w_ref[...].astype(BF16)

    if kind == "row":
        o_spec = pl.BlockSpec((tr, cs), lambda i, q_ref: (q_ref[0] * nr + i, 0))
    else:
        o_spec = pl.BlockSpec((tr, cs), lambda i, q_ref: (i, q_ref[0]))
    return pl.pallas_call(
        body,
        name=name,
        grid_spec=pltpu.PrefetchScalarGridSpec(num_scalar_prefetch=1, grid=(nr,),
                                               in_specs=[pl.BlockSpec((tr, cs), lambda i, q_ref: (i, 0))],
                                               out_specs=o_spec),
        out_shape=jax.ShapeDtypeStruct(_full_shape(kind, (rs, cs)), BF16),
        compiler_params=_params(("parallel",)),
    )(q_arr, w)


def _swiglu_tiles(gate, up):
    s = _sigmoid(gate)
    silu = gate * s
    return [silu * up, up * (s * (1.0 + gate * (1.0 - s))), silu]


def _ffn_gateup(tag, n, wg, wu, carry):
    def act(accs, extras):
        return _swiglu_tiles(accs[0], accs[1])

    return _matmul(tag + "_gateup", "nn", [n], [wg, wu], [(0, 0, 0), (0, 1, 1)], [BF16] * 3, act,
                   tm=1024, tn=512, tk=2048, carry=carry)


def _ffn_up(tag, n, wu, gate, carry):
    def act(accs, extras):
        return _swiglu_tiles(extras[0].astype(F32), accs[0])

    return _matmul(tag + "_up", "nn", [n], [wu], [(0, 0, 0)], [BF16] * 3, act, extras=(gate,),
                   tm=1024, tn=512, tk=2048, carry=carry)


def _ffn_down(tag, a, wd, h, carry=None):
    return _mm_nn(tag + "_down", a, wd, F32, res=h, alpha=0.5, tm=1024, tn=512, tk=5632, carry=carry)


def _ffn_bwd(tag, names, h, g, wg, wu, wd, saved, dh, dfb, next_scale, reducer, riding=None, riding_dwd=None,
             last_hop_later=False, before_norm=None):
    n, a, da_dgate, da_dup = saved

    def act_bwd(accs, extras):
        return [accs[0] * extras[0].astype(F32), accs[0] * extras[1].astype(F32)]

    dact = _matmul(tag + "_dact", "nt", [dfb], [wd], [(0, 0, 0)], [BF16, BF16], act_bwd, extras=(da_dgate, da_dup),
                   tm=1024, tn=512, tk=2048, carry=riding)
    (dgp, du), rode = dact if riding else (dact, [])
    dwd_call = _mm_tn(tag + "_dwd", a, [dfb], BF16, tm=512, tn=2048, tk=4096, carry=riding_dwd)
    (dwd,), rode_dwd = dwd_call if riding_dwd else (dwd_call, [])
    down = reducer.swap(names[2:], [dwd])
    (dwg,), swapped = _mm_tn(tag + "_dwg", n, [dgp], BF16, tm=512, tn=1408, tk=4096, carry=down.carry)
    down = reducer.send(down, swapped)
    gate_w = reducer.swap(names[:1], [dwg])
    carry, split = _ride(down.carry, gate_w.carry)
    (dwu,), results = _mm_tn(tag + "_dwu", n, [du], BF16, tm=512, tn=1408, tk=4096, carry=carry)
    got, swapped = split(results)
    reducer.end(down, got)
    gate_w = reducer.send(gate_w, swapped)
    up_w = reducer.swap(names[1:2], [dwu])
    carry, split = _ride(gate_w.carry, up_w.carry)
    halves = 1 if last_hop_later else 2
    tiles = dict(tm=512, tn=512, tk=5632) if last_hop_later else dict(tm=1024, tn=1024, tk=1408)
    dn_lo, results = _mm_nt_sum(tag + "_dn_lo", [dgp, du], [wg, wu], BF16, carry=carry, n_part=(0, halves), **tiles)
    got, swapped = split(results)
    reducer.end(gate_w, got)
    up_w = reducer.send(up_w, swapped)
    dn = [dn_lo]
    if not last_hop_later:
        dn_hi, got = _mm_nt_sum(tag + "_dn_hi", [dgp, du], [wg, wu], BF16, tm=1024, tn=1024, tk=1408,
                                carry=up_w.carry, n_part=(1, 2))
        reducer.end(up_w, got)
        dn.append(dn_hi)
    riding_norm = before_norm() if before_norm else None
    norm_call = _rms_bwd(tag + "_dnorm", h, g, dn, dh, next_scale, carry=riding_norm)
    (dh_in, dh_in_b, dg), rode_norm = norm_call if riding_norm else (norm_call, [])
    return dh_in, dh_in_b, dg, rode, rode_dwd, (up_w if last_hop_later else None), rode_norm


def _ssm_discretize(log_dt, a_re, a_im, b_re, b_im):
    dt = jnp.exp(log_dt)[:, None]
    lr = jnp.minimum(a_re, -1e-4)
    li = a_im
    mag = jnp.exp(lr * dt)
    ang = li * dt
    abar_r = mag * jnp.cos(ang)
    abar_i = mag * jnp.sin(ang)
    den = lr * lr + li * li
    xr = abar_r - 1.0
    xi = abar_i
    zr = (xr * lr + xi * li) / den
    zi = (xi * lr - xr * li) / den
    bbar_r = zr[..., None] * b_re - zi[..., None] * b_im
    bbar_i = zr[..., None] * b_im + zi[..., None] * b_re
    return abar_r, abar_i, bbar_r, bbar_i


def _blockdiag_in(b):
    g, n, p = b.shape
    nb = g // GROUPS_PER_BLOCK
    eye = jnp.eye(GROUPS_PER_BLOCK, dtype=b.dtype)
    b4 = b.reshape(nb, GROUPS_PER_BLOCK, n, p)
    return jnp.einsum("sgnp,gh->sgphn", b4, eye).reshape(nb, GROUPS_PER_BLOCK * p, GROUPS_PER_BLOCK * n)


def _blockdiag_in_grad(gw, n, p):
    nb = gw.shape[0]
    eye = jnp.eye(GROUPS_PER_BLOCK, dtype=gw.dtype)
    g5 = gw.reshape(nb, GROUPS_PER_BLOCK, p, GROUPS_PER_BLOCK, n)
    return jnp.einsum("sgphn,gh->sgnp", g5, eye).reshape(nb * GROUPS_PER_BLOCK, n, p)


def _blockdiag_out(c):
    g, p, n = c.shape
    nb = g // GROUPS_PER_BLOCK
    eye = jnp.eye(GROUPS_PER_BLOCK, dtype=c.dtype)
    c4 = c.reshape(nb, GROUPS_PER_BLOCK, p, n)
    return jnp.einsum("sgpn,gh->shngp", c4, eye).reshape(nb, GROUPS_PER_BLOCK * n, GROUPS_PER_BLOCK * p)


def _blockdiag_out_grad(gw, p, n):
    nb = gw.shape[0]
    eye = jnp.eye(GROUPS_PER_BLOCK, dtype=gw.dtype)
    g5 = gw.reshape(nb, GROUPS_PER_BLOCK, n, GROUPS_PER_BLOCK, p)
    return jnp.einsum("shngp,gh->sgpn", g5, eye).reshape(nb * GROUPS_PER_BLOCK, p, n)


def _ssm_scan_fwd(z, wb, wc, abar, d, carry=None):
    t_dim = z.shape[0]
    nb, cb, sw2 = wb.shape
    nl = sw2 // LANES
    hl = nl // 2
    tt = _tile(t_dim, 256, 8)
    nt = t_dim // tt

    def body(z_ref, wb_ref, wc_ref, a_ref, d_ref, s_ref, y_ref, yg_ref, drive_ref, st_ref):
        @pl.when(pl.program_id(0) == 0)
        def _():
            st_ref[...] = jnp.zeros_like(st_ref)

        u = z_ref[...]
        ub = u.astype(BF16)
        for b in range(nb):
            drive = _dot(ub[:, b * cb:(b + 1) * cb], wb_ref[b], "nn")
            for l in range(nl):
                drive_ref[l, pl.ds(b, tt, stride=nb), :] = drive[:, l * LANES:(l + 1) * LANES]
        a = a_ref[...]
        chunk = lambda v, l: v[:, l * LANES:(l + 1) * LANES]

        def step(t, state):
            rows = pl.ds(pl.multiple_of(t * nb, nb), nb)
            re, im = [], []
            for l in range(hl):
                ar, ai, sr, si = chunk(a, l), chunk(a, hl + l), state[l], state[hl + l]
                nr = ar * sr - ai * si + drive_ref[l, rows, :]
                ni = ar * si + ai * sr + drive_ref[hl + l, rows, :]
                s_ref[l, rows, :] = nr
                s_ref[hl + l, rows, :] = ni
                re.append(nr)
                im.append(ni)
            return tuple(re + im)

        state = lax.fori_loop(0, tt, step, tuple(st_ref[l] for l in range(nl)), unroll=8)
        for l in range(nl):
            st_ref[l] = state[l]
        parts = []
        for b in range(nb):
            s_b = jnp.concatenate([s_ref[l, pl.ds(b, tt, stride=nb), :] for l in range(nl)], axis=1)
            parts.append(_dot(s_b, wc_ref[b], "nn"))
        y = jnp.concatenate(parts, axis=1) + d_ref[...] * u
        y_ref[...] = y
        yg_ref[...] = _gelu(y).astype(BF16)

    full = lambda a: pl.BlockSpec(a.shape, lambda t, nd=a.ndim: (0,) * nd)
    return _carried_call(
        body, carry,
        name="ssm_scan_fwd",
        grid=(nt,),
        in_specs=[pl.BlockSpec((tt, nb * cb), lambda t: (t, 0)), full(wb), full(wc), full(abar), full(d)],
        out_specs=[
            pl.BlockSpec((nl, tt * nb, LANES), lambda t: (0, t, 0)),
            pl.BlockSpec((tt, nb * cb), lambda t: (t, 0)),
            pl.BlockSpec((tt, nb * cb), lambda t: (t, 0)),
        ],
        out_shape=[
            jax.ShapeDtypeStruct((nl, t_dim * nb, LANES), F32),
            jax.ShapeDtypeStruct((t_dim, nb * cb), F32),
            jax.ShapeDtypeStruct((t_dim, nb * cb), BF16),
        ],
        scratch_shapes=[pltpu.VMEM((nl, tt * nb, LANES), F32), pltpu.VMEM((nl, nb, LANES), F32)],
        semantics=("arbitrary",),
        args=[z, wb, wc, abar, d],
    )


def _ssm_scan_bwd(z, dy, states, wb, wc, abar_conj, d, dz_all, carry=None):
    t_dim = z.shape[0]
    nb, cb, sw2 = wb.shape
    nl = sw2 // LANES
    hl = nl // 2
    tt = _tile(t_dim, 256, 8)
    nt = t_dim // tt
    edges = states.reshape(nl, nt, tt * nb, LANES)[:, :, (tt - 1) * nb:, :]
    before = jnp.concatenate([jnp.zeros((nl, 1, nb, LANES), F32), edges[:, :-1]], axis=1).reshape(nl, nt * nb, LANES)

    def body(z_ref, dy_ref, s_ref, sp_ref, wb_ref, wc_ref, a_ref, d_ref, dz_all_ref,
             dz_ref, gwb_ref, gwc_ref, ga_ref, gd_ref, gin_ref, gs_ref, st_ref):
        @pl.when(pl.program_id(0) == 0)
        def _():
            st_ref[...] = jnp.zeros_like(st_ref)
            gwb_ref[...] = jnp.zeros_like(gwb_ref)
            gwc_ref[...] = jnp.zeros_like(gwc_ref)
            ga_ref[...] = jnp.zeros_like(ga_ref)
            gd_ref[...] = jnp.zeros_like(gd_ref)

        u = z_ref[...]
        dyv = dy_ref[...]
        ub = u.astype(BF16)
        dyb = dyv.astype(BF16)
        for b in range(nb):
            gin = _dot(dyb[:, b * cb:(b + 1) * cb], wc_ref[b], "nt")
            for l in range(nl):
                gin_ref[l, pl.ds(b, tt, stride=nb), :] = gin[:, l * LANES:(l + 1) * LANES]
        a = a_ref[...]
        chunk = lambda v, l: v[:, l * LANES:(l + 1) * LANES]

        def step(k, state):
            rows = pl.ds(pl.multiple_of((tt - 1 - k) * nb, nb), nb)
            re, im = [], []
            for l in range(hl):
                ar, ai, gr, gi = chunk(a, l), chunk(a, hl + l), state[l], state[hl + l]
                nr = ar * gr - ai * gi + gin_ref[l, rows, :]
                ni = ar * gi + ai * gr + gin_ref[hl + l, rows, :]
                gs_ref[l, rows, :] = nr
                gs_ref[hl + l, rows, :] = ni
                re.append(nr)
                im.append(ni)
            return tuple(re + im)

        state = lax.fori_loop(0, tt, step, tuple(st_ref[l] for l in range(nl)), unroll=8)
        for l in range(nl):
            st_ref[l] = state[l]

        parts = []
        for b in range(nb):
            cols = slice(b * cb, (b + 1) * cb)
            gs_b = jnp.concatenate([gs_ref[l, pl.ds(b, tt, stride=nb), :] for l in range(nl)], axis=1)
            s_b = jnp.concatenate([s_ref[l, pl.ds(b, tt, stride=nb), :] for l in range(nl)], axis=1)
            parts.append(_dot(gs_b, wb_ref[b], "nt"))
            gwb_ref[b] += _dot(ub[:, cols], gs_b, "tn")
            gwc_ref[b] += _dot(s_b, dyb[:, cols], "tn")
        dz_ref[...] = (jnp.concatenate(parts, axis=1) + d_ref[...] * dyv).astype(BF16)
        gd_ref[...] += jnp.sum(dyv * u, axis=0, keepdims=True)

        row = lax.broadcasted_iota(jnp.int32, (tt * nb, LANES), 0)
        shifted = lambda v: jnp.where(row < nb, 0.0, pltpu.roll(v, nb, 0))
        over_time = lambda v: jnp.sum(v.reshape(tt, nb, LANES), axis=0)
        for l in range(hl):
            g_r, g_i = gs_ref[l], gs_ref[hl + l]
            p_r, p_i = shifted(s_ref[l]), shifted(s_ref[hl + l])
            f_r, f_i = sp_ref[l], sp_ref[hl + l]
            g0_r, g0_i = gs_ref[l, pl.ds(0, nb), :], gs_ref[hl + l, pl.ds(0, nb), :]
            ga_ref[l] += over_time(g_r * p_r + g_i * p_i) + g0_r * f_r + g0_i * f_i
            ga_ref[hl + l] += over_time(g_i * p_r - g_r * p_i) + g0_i * f_r - g0_r * f_i

    rev = lambda t: (nt - 1 - t, 0)
    rev3 = lambda t: (0, nt - 1 - t, 0)
    full = lambda a: pl.BlockSpec(a.shape, lambda t, nd=a.ndim: (0,) * nd)
    return _carried_call(
        body, carry,
        name="ssm_scan_bwd",
        grid=(nt,),
        in_specs=[
            pl.BlockSpec((tt, nb * cb), rev),
            pl.BlockSpec((tt, nb * cb), rev),
            pl.BlockSpec((nl, tt * nb, LANES), rev3),
            pl.BlockSpec((nl, nb, LANES), rev3),
            full(wb), full(wc), full(abar_conj), full(d), _ANY,
        ],
        out_specs=[
            pl.BlockSpec((tt, nb * cb), rev),
            pl.BlockSpec((nb, cb, sw2), lambda t: (0, 0, 0)),
            pl.BlockSpec((nb, sw2, cb), lambda t: (0, 0, 0)),
            pl.BlockSpec((nl, nb, LANES), lambda t: (0, 0, 0)),
            pl.BlockSpec((1, nb * cb), lambda t: (0, 0)),
        ],
        out_shape=[
            jax.ShapeDtypeStruct(dz_all.shape, BF16),
            jax.ShapeDtypeStruct((nb, cb, sw2), F32),
            jax.ShapeDtypeStruct((nb, sw2, cb), F32),
            jax.ShapeDtypeStruct((nl, nb, LANES), F32),
            jax.ShapeDtypeStruct((1, nb * cb), F32),
        ],
        scratch_shapes=[pltpu.VMEM((nl, tt * nb, LANES), F32), pltpu.VMEM((nl, tt * nb, LANES), F32),
                        pltpu.VMEM((nl, nb, LANES), F32)],
        semantics=("arbitrary",),
        args=[z, dy, states, before, wb, wc, abar_conj, d, dz_all],
        aliases={8: 0},
    )


def _gmlp_chunk(zu, zv, gv, wm_ref, bias, n_heads):
    ua = _gelu(zu)
    vg = _gelu(zv)
    xc = vg - jnp.mean(vg, axis=-1, keepdims=True)
    r = lax.rsqrt(jnp.mean(xc * xc, axis=-1, keepdims=True) + EPS)
    vh = xc * r
    vb = (vh * gv).astype(BF16)
    parts = []
    for h in range(n_heads):
        cols = slice(h * GMLP_HEAD, (h + 1) * GMLP_HEAD)
        parts.append(_dot(wm_ref[h], vb[:, cols], "nn"))
    s = jnp.concatenate(parts, axis=1) + bias
    return ua, vh, r, vb, s


def _gmlp_fwd(z, gv, wm, bias, ggo):
    t_dim = z.shape[0]
    dg = gv.shape[1]
    n_heads = dg // GMLP_HEAD
    tr = _tile(t_dim, 256, CHUNK)

    def body(zu_ref, zv_ref, gv_ref, wm_ref, b_ref, ggo_ref, o_ref):
        for ck in range(tr // CHUNK):
            rows = pl.ds(ck * CHUNK, CHUNK)
            ua, _, _, _, s = _gmlp_chunk(zu_ref[rows, :], zv_ref[rows, :], gv_ref[...], wm_ref, b_ref[...], n_heads)
            yh, _ = _rms_stats(ua * s)
            o_ref[rows, :] = (yh * ggo_ref[...]).astype(BF16)

    full = lambda a: pl.BlockSpec(a.shape, lambda i, nd=a.ndim: (0,) * nd)
    return pl.pallas_call(
        body,
        name="gmlp_fwd",
        grid=(t_dim // tr,),
        in_specs=[pl.BlockSpec((tr, dg), lambda i: (i, 1)), pl.BlockSpec((tr, dg), lambda i: (i, 2)),
                  full(gv), full(wm), full(bias), full(ggo)],
        out_specs=pl.BlockSpec((tr, dg), lambda i: (i, 0)),
        out_shape=jax.ShapeDtypeStruct((t_dim, dg), BF16),
        compiler_params=_params(("parallel",)),
    )(z, z, gv, wm, bias, ggo)


def _gmlp_bwd(z, dycat, gv, wm, bias, ggo):
    t_dim = z.shape[0]
    dg = gv.shape[1]
    n_heads = dg // GMLP_HEAD
    tr = _tile(t_dim, 256, CHUNK)

    def body(zu_ref, zv_ref, dy_ref, gv_ref, wm_ref, b_ref, ggo_ref,
             dz_ref, dggo_ref, dgv_ref, dwm_ref, dsum_ref):
        @pl.when(pl.program_id(0) == 0)
        def _():
            dggo_ref[...] = jnp.zeros_like(dggo_ref)
            dgv_ref[...] = jnp.zeros_like(dgv_ref)
            dwm_ref[...] = jnp.zeros_like(dwm_ref)
            dsum_ref[...] = jnp.zeros_like(dsum_ref)

        for ck in range(tr // CHUNK):
            rows = pl.ds(ck * CHUNK, CHUNK)
            zu = zu_ref[rows, :]
            zv = zv_ref[rows, :]
            gvv = gv_ref[...]
            ua, vh, r, vb, s = _gmlp_chunk(zu, zv, gvv, wm_ref, b_ref[...], n_heads)
            dy, dggo = _rms_backward(ua * s, ggo_ref[...], dy_ref[rows, :].astype(F32))
            dggo_ref[...] += dggo
            ds = dy * ua
            dsum_ref[...] += ds
            dsb = ds.astype(BF16)
            parts = []
            for h in range(n_heads):
                cols = slice(h * GMLP_HEAD, (h + 1) * GMLP_HEAD)
                dwm_ref[h] += _dot(dsb[:, cols], vb[:, cols], "nt")
                parts.append(_dot(wm_ref[h], dsb[:, cols], "tn"))
            dv = jnp.concatenate(parts, axis=1)
            dgv_ref[...] += jnp.sum(dv * vh, axis=0, keepdims=True)
            dvh = dv * gvv
            dvg = r * (dvh - jnp.mean(dvh, axis=-1, keepdims=True) - vh * jnp.mean(dvh * vh, axis=-1, keepdims=True))
            dz_ref[rows, pl.ds(2 * dg, dg)] = (dvg * _gelu_grad(zv)).astype(BF16)
            dz_ref[rows, pl.ds(dg, dg)] = (dy * s * _gelu_grad(zu)).astype(BF16)

    full = lambda a: pl.BlockSpec(a.shape, lambda i, nd=a.ndim: (0,) * nd)
    return pl.pallas_call(
        body,
        name="gmlp_bwd",
        grid=(t_dim // tr,),
        in_specs=[pl.BlockSpec((tr, dg), lambda i: (i, 1)), pl.BlockSpec((tr, dg), lambda i: (i, 2)),
                  pl.BlockSpec((tr, dg), lambda i: (i, 1)), full(gv), full(wm), full(bias), full(ggo)],
        out_specs=[pl.BlockSpec((tr, 3 * dg), lambda i: (i, 0)),
                   pl.BlockSpec((1, dg), lambda i: (0, 0)), pl.BlockSpec((1, dg), lambda i: (0, 0)),
                   pl.BlockSpec(wm.shape, lambda i: (0, 0, 0)), pl.BlockSpec((CHUNK, dg), lambda i: (0, 0))],
        out_shape=[jax.ShapeDtypeStruct((t_dim, 3 * dg), BF16),
                   jax.ShapeDtypeStruct((1, dg), F32), jax.ShapeDtypeStruct((1, dg), F32),
                   jax.ShapeDtypeStruct(wm.shape, F32), jax.ShapeDtypeStruct((CHUNK, dg), F32)],
        compiler_params=_params(("arbitrary",)),
    )(z, z, dycat, gv, wm, bias, ggo)


def _ple_head(npl, w_gate, h3, pp, tgt, g_final):
    t_dim, d = h3.shape
    tr = _tile(t_dim, 256, 16)

    def body(n_ref, w_ref, h_ref, pp_ref, t_ref, g_ref, dgq_ref, dpp_ref, dh_ref, dg_ref, loss_ref):
        @pl.when(pl.program_id(0) == 0)
        def _():
            dg_ref[...] = jnp.zeros_like(dg_ref)
            loss_ref[...] = jnp.zeros_like(loss_ref)

        gate = _sigmoid(_dot(n_ref[...], w_ref[...], "nn"))
        ppv = pp_ref[...]
        h4 = h_ref[...] + gate * ppv
        xh, _ = _rms_stats(h4)
        err = xh * g_ref[...] - t_ref[...]
        dh4, dg = _rms_backward(h4, g_ref[...], err * (1.0 / d))
        dh_ref[...] = dh4
        dgq_ref[...] = (dh4 * ppv * gate * (1.0 - gate)).astype(BF16)
        dpp_ref[...] = (dh4 * gate).astype(BF16)
        dg_ref[...] += dg
        loss_ref[...] += jnp.full((1, LANES), 0.5 * jnp.sum(err * err) * (1.0 / d), F32)

    rows = pl.BlockSpec((tr, d), lambda i: (i, 0))
    whole = lambda a: pl.BlockSpec(a.shape, lambda i: (0, 0))
    return pl.pallas_call(
        body,
        name="ple_head",
        grid=(t_dim // tr,),
        in_specs=[rows, whole(w_gate), rows, rows, rows, whole(g_final)],
        out_specs=[rows, rows, rows, pl.BlockSpec((1, d), lambda i: (0, 0)), pl.BlockSpec((1, LANES), lambda i: (0, 0))],
        out_shape=[jax.ShapeDtypeStruct((t_dim, d), BF16), jax.ShapeDtypeStruct((t_dim, d), BF16),
                   jax.ShapeDtypeStruct((t_dim, d), F32), jax.ShapeDtypeStruct((1, d), F32),
                   jax.ShapeDtypeStruct((1, LANES), F32)],
        compiler_params=_params(("arbitrary",)),
    )(npl, w_gate, h3, pp, tgt, g_final)


def _position():
    x, y, c = lax.axis_index("x"), lax.axis_index("y"), lax.axis_index("c")
    chips = [(1 - x, y), (x, 1 - y), (1 - x, 1 - y)]
    return x, y, c, chips


def _region(ref, kind, shard_shape, q, half, part=None):
    rs, cs = shard_shape
    r0, nr = (0, rs) if half is None else (half * (rs // 2), rs // 2)
    if part is not None:
        r0, nr = r0 + part * (rs // 4), rs // 4
    if kind == "row":
        return ref.at[pl.ds(q * rs + r0, nr), :]
    return ref.at[pl.ds(r0, nr), pl.ds(q * cs, cs)]


def _full_shape(kind, shard_shape):
    rs, cs = shard_shape
    return (N_CHIPS * rs, cs) if kind == "row" else (rs, N_CHIPS * cs)


def _remote(src, dst, send_sems, recv_sems, k, to):
    return pltpu.make_async_remote_copy(src_ref=src, dst_ref=dst, send_sem=send_sems.at[k], recv_sem=recv_sems.at[k],
                                        device_id=to, device_id_type=MESH)


def _same(arrays):
    return [jax.ShapeDtypeStruct(a.shape, a.dtype) for a in arrays]


def _gather_near_carry(gathered, kinds, shapes):
    nw = len(gathered)

    def copies(ops, full, send_sems, recv_sems):
        x, y, c, _ = _position()
        out = []
        for w in range(nw):
            mine = _region(full[w], kinds[w], shapes[w], 2 * x + y, c)
            out.append(_remote(mine, mine, send_sems, recv_sems, 2 * w, (1 - x, y, c)))
            out.append(_remote(mine, mine, send_sems, recv_sems, 2 * w + 1, (x, 1 - y, c)))
        return out

    return _Carry(gathered, _same(gathered), {i: i for i in range(nw)}, 2 * nw, copies)


def _gather_far_carry(gathered, kinds, shapes):
    nw = len(gathered)

    def copies(ops, full, send_sems, recv_sems):
        x, y, c, _ = _position()
        out = []
        for w in range(nw):
            from_x = _region(full[w], kinds[w], shapes[w], 2 * (1 - x) + y, c, part=1)
            from_y = _region(full[w], kinds[w], shapes[w], 2 * x + (1 - y), c, part=0)
            out.append(_remote(from_x, from_x, send_sems, recv_sems, 2 * w, (x, 1 - y, c)))
            out.append(_remote(from_y, from_y, send_sems, recv_sems, 2 * w + 1, (1 - x, y, c)))
        return out

    return _Carry(gathered, _same(gathered), {i: i for i in range(nw)}, 2 * nw, copies)


def _gather_d2d_carry(gathered, kinds, shapes):
    nw = len(gathered)

    def copies(ops, full, send_sems, recv_sems):
        x, y, c, chips = _position()
        out = []
        for w in range(nw):
            for j, (cx, cy) in enumerate(chips):
                landed = _region(full[w], kinds[w], shapes[w], 2 * cx + cy, c)
                out.append(_remote(landed, landed, send_sems, recv_sems, 3 * w + j, (x, y, 1 - c)))
        return out

    return _Carry(gathered, _same(gathered), {i: i for i in range(nw)}, 3 * nw, copies)


def _pairs_carry(grads, kinds, shapes):
    nw = len(grads)

    def copies(g, got, send_sems, recv_sems):
        x, y, c, _ = _position()
        out = []
        for w in range(nw):
            for q in range(N_CHIPS):
                out.append(_remote(_region(g[w], kinds[w], shapes[w], q, 1 - c), got[w].at[q], send_sems, recv_sems,
                                   N_CHIPS * w + q, (x, y, 1 - c)))
        return out

    outs = [jax.ShapeDtypeStruct((N_CHIPS, s[0] // 2, s[1]), BF16) for s in shapes]
    return _Carry(grads, outs, {}, N_CHIPS * nw, copies)


def _pair_sum(name, grad, got, kind, shard_shape, c_arr):
    rs, cs = shard_shape
    hr = rs // 2
    tr = _tile(hr, 512, 16)
    nr = hr // tr

    def body(c_ref, g_ref, s_ref, o_ref):
        o_ref[...] = (g_ref[...].astype(F32) + s_ref[...].astype(F32)).astype(BF16)

    if kind == "row":
        g_spec = pl.BlockSpec((tr, cs), lambda q, i, c_ref: (q * (rs // tr) + c_ref[0] * nr + i, 0))
    else:
        g_spec = pl.BlockSpec((tr, cs), lambda q, i, c_ref: (c_ref[0] * nr + i, q))
    blk = pl.BlockSpec((None, tr, cs), lambda q, i, c_ref: (q, i, 0))
    return pl.pallas_call(
        body,
        name=name,
        grid_spec=pltpu.PrefetchScalarGridSpec(num_scalar_prefetch=1, grid=(N_CHIPS, nr), in_specs=[g_spec, blk],
                                               out_specs=blk),
        out_shape=jax.ShapeDtypeStruct((N_CHIPS, hr, cs), BF16),
        compiler_params=_params(("parallel", "parallel")),
    )(c_arr, grad, got)


def _scatter_carry(sums, shapes):
    nw = len(sums)

    def copies(ps, got, send_sems, recv_sems):
        x, y, c, chips = _position()
        out = []
        for w in range(nw):
            for j, (cx, cy) in enumerate(chips):
                out.append(_remote(ps[w].at[2 * cx + cy], got[w].at[j], send_sems, recv_sems, 3 * w + j, (cx, cy, c)))
        return out

    outs = [jax.ShapeDtypeStruct((3, s[0] // 2, s[1]), BF16) for s in shapes]
    return _Carry(sums, outs, {}, 3 * nw, copies)


def _owner_sum(name, sums, got, shard_shape, qc_arr):
    rs, cs = shard_shape
    hr = rs // 2
    tr = _tile(hr, 512, 16)
    nr = hr // tr

    def body(qc_ref, mine_ref, got_ref, o_ref):
        acc = mine_ref[...].astype(F32)
        for j in range(3):
            acc = acc + got_ref[j].astype(F32)
        o_ref[...] = acc

    return pl.pallas_call(
        body,
        name=name,
        grid_spec=pltpu.PrefetchScalarGridSpec(
            num_scalar_prefetch=1, grid=(nr,),
            in_specs=[pl.BlockSpec((None, tr, cs), lambda i, qc_ref: (qc_ref[0], i, 0)),
                      pl.BlockSpec((3, tr, cs), lambda i, qc_ref: (0, i, 0))],
            out_specs=pl.BlockSpec((tr, cs), lambda i, qc_ref: (qc_ref[1] * nr + i, 0))),
        out_shape=jax.ShapeDtypeStruct((rs, cs), F32),
        compiler_params=_params(("parallel",)),
    )(qc_arr, sums, got)


def _share_carry(grads, shapes):
    nw = len(grads)

    def copies(ops, out, send_sems, recv_sems):
        x, y, c, _ = _position()
        res = []
        for w in range(nw):
            hr = shapes[w][0] // 2
            mine = out[w].at[pl.ds(c * hr, hr), :]
            res.append(_remote(mine, mine, send_sems, recv_sems, w, (x, y, 1 - c)))
        return res

    return _Carry(grads, _same(grads), {i: i for i in range(nw)}, nw, copies)


def _place_block(packed, me):
    return lax.dynamic_update_slice(jnp.zeros((N_DEV,) + packed.shape, F32), packed[None], (me, 0, 0))


def _exchange_carry(blocks):
    def copies(ops, res, send_sems, recv_sems):
        x, y, c, _ = _position()
        mine = res[0].at[4 * x + 2 * y + c]
        out = []
        for k in range(1, N_DEV):
            to = ((1 - x) if k & 4 else x, (1 - y) if k & 2 else y, (1 - c) if k & 1 else c)
            out.append(_remote(mine, mine, send_sems, recv_sems, k - 1, to))
        return out

    return _Carry([blocks], _same([blocks]), {0: 0}, N_DEV - 1, copies)


def _sum_blocks(name, blocks):
    n, rows, lanes = blocks.shape
    tr = _tile(rows, 4096, 8)

    def body(b_ref, o_ref):
        acc = b_ref[0]
        for k in range(1, n):
            acc = acc + b_ref[k]
        o_ref[...] = acc

    return pl.pallas_call(
        body,
        name=name,
        grid=(rows // tr,),
        in_specs=[pl.BlockSpec((n, tr, lanes), lambda i: (0, i, 0))],
        out_specs=pl.BlockSpec((tr, lanes), lambda i: (i, 0)),
        out_shape=jax.ShapeDtypeStruct((rows, lanes), F32),
        compiler_params=_params(("parallel",)),
    )(blocks)


def _adamw(name, w, g, m, v, carry=None):
    def fn(rows, pars):
        wv, gv, mv, vv = rows
        m_new = ADAM_B1 * mv + (1.0 - ADAM_B1) * gv
        v_new = ADAM_B2 * vv + (1.0 - ADAM_B2) * (gv * gv)
        m_hat = m_new / (1.0 - ADAM_B1 ** ADAM_STEP)
        v_hat = v_new / (1.0 - ADAM_B2 ** ADAM_STEP)
        delta = -ADAM_LR * (m_hat / (jnp.sqrt(v_hat) + ADAM_EPS) + ADAM_WD * wv)
        return [delta, m_new, v_new, gv], []

    c = w.shape[1]
    return _rowwise(name, fn, [w, g, m, v], [], [(c, F32)] * 4, tr=256, carry=carry)


def _pack(arrays):
    rows = []
    for a in arrays:
        flat = a.reshape(-1).astype(F32)
        pad = (-flat.shape[0]) % LANES
        rows.append(jnp.pad(flat, (0, pad)).reshape(-1, LANES))
    stacked = jnp.concatenate(rows, axis=0)
    pad_rows = (-stacked.shape[0]) % 8
    return jnp.pad(stacked, ((0, pad_rows), (0, 0)))


def _unpack(packed, shapes):
    out, r = [], 0
    for s in shapes:
        n = math.prod(s)
        nr = -(-n // LANES)
        out.append(packed[r:r + nr].reshape(-1)[:n].reshape(s))
        r += nr
    return out


BIG = ["w1_gate", "w1_up", "w1_down", "w_in", "ssm_w_glu", "w_out", "w2_gate", "w2_up", "w2_down", "w_ple_gate",
       "w_ple_proj"]
KIND = {"w1_gate": "col", "w1_up": "col", "w1_down": "row", "w_in": "col", "ssm_w_glu": "row", "w_out": "row",
        "w2_gate": "col", "w2_up": "col", "w2_down": "row", "w_ple_gate": "row", "w_ple_proj": "col"}
SMALL = ["norm_ffn1", "norm_mix", "ssm_log_dt", "ssm_a_re", "ssm_a_im", "ssm_b_re", "ssm_b_im", "ssm_c_re", "ssm_c_im",
         "ssm_d", "gmlp_norm_v", "gmlp_w_s", "gmlp_b_s", "norm_ssm_out", "norm_gmlp_out", "norm_ffn2", "norm_ple",
         "norm_final"]
WEIGHTS = ["norm_ffn1", "w1_gate", "w1_up", "w1_down", "norm_mix", "w_in", "ssm_log_dt", "ssm_a_re", "ssm_a_im",
           "ssm_b_re", "ssm_b_im", "ssm_c_re", "ssm_c_im", "ssm_d", "ssm_w_glu", "gmlp_norm_v", "gmlp_w_s", "gmlp_b_s",
           "norm_ssm_out", "norm_gmlp_out", "w_out", "norm_ffn2", "w2_gate", "w2_up", "w2_down", "norm_ple",
           "w_ple_gate", "w_ple_proj", "norm_final"]


class _Trip:
    def __init__(self, names, arrays, carry):
        self.names, self.arrays, self.carry = names, arrays, carry


class _Reducer:
    def __init__(self, shard_shape, c_arr, qc_arr):
        self.shard_shape, self.c_arr, self.qc_arr = shard_shape, c_arr, qc_arr
        self.halves = {}

    def swap(self, names, grads):
        kinds = [KIND[n] for n in names]
        shapes = [self.shard_shape[n] for n in names]
        return _Trip(names, grads, _pairs_carry(grads, kinds, shapes))

    def send(self, trip, swapped):
        shapes = [self.shard_shape[n] for n in trip.names]
        sums = [_pair_sum("pair_sum_" + n, g, s, KIND[n], sh, self.c_arr)
                for n, g, s, sh in zip(trip.names, trip.arrays, swapped, shapes)]
        return _Trip(trip.names, sums, _scatter_carry(sums, shapes))

    def end(self, trip, got):
        for n, ps, g in zip(trip.names, trip.arrays, got):
            self.halves[n] = _owner_sum("owner_sum_" + n, ps, g, self.shard_shape[n], self.qc_arr)


def _ride(*carries):
    present = [c for c in carries if c is not None]
    joined = functools.reduce(_join, present) if present else None

    def split(results):
        out, at = [], 0
        for c in carries:
            n = len(c.out_shapes) if c is not None else 0
            out.append(list(results[at:at + n]))
            at += n
        return out

    return joined, split


def _step(x, p, tgt, w, m, v):
    d_model = x.shape[1]
    d_ssm = w["ssm_d"].shape[1]
    n_groups = d_ssm // SSM_GROUP
    row = lambda a: a.reshape(1, -1)

    xi, yi, ci = lax.axis_index("x"), lax.axis_index("y"), lax.axis_index("c")
    c_arr = jnp.reshape(ci, (1,)).astype(jnp.int32)
    q_arr = jnp.reshape(2 * xi + yi, (1,)).astype(jnp.int32)
    qc_arr = jnp.stack([2 * xi + yi, ci]).astype(jnp.int32)
    shard_shape = {n: w[n].shape for n in BIG}
    full = {n: _cast_into_gathered("cast_" + n, w[n], KIND[n], q_arr) for n in BIG}

    def gather(stage, names):
        return stage([full[n] for n in names], [KIND[n] for n in names], [shard_shape[n] for n in names])

    def gathered(names, arrays):
        full.update(zip(names, arrays))

    groups = [["w1_gate"], ["w1_up"], ["w1_down"], ["w_in"], ["w2_gate"], ["ssm_w_glu", "w_out"], ["w2_up"],
              ["w2_down", "w_ple_gate", "w_ple_proj"]]
    near, far, d2d = _gather_near_carry, _gather_far_carry, _gather_d2d_carry

    def stages(*work):
        carries = [gather(stage, groups[g]) for stage, g in work]
        return functools.reduce(_join, carries), [n for _, g in work for n in groups[g]]

    def alone(name, *work):
        carry, names = stages(*work)
        gathered(names, _comm_call(name, carry))

    alone("gather_a", (near, 0))
    alone("gather_b", (far, 0), (near, 1))
    carry, names = stages((d2d, 0), (far, 1))
    n1, landed = _rms_fwd("ffn1_norm", x, w["norm_ffn1"], carry)
    gathered(names, landed)
    carry, names = stages((d2d, 1), (near, 2), (near, 3))
    gate1, landed = _mm_nn("ffn1_gate", n1, full["w1_gate"], BF16, tm=1024, tn=512, tk=2048, carry=carry)
    gathered(names, landed)
    carry, names = stages((far, 2), (far, 3), (near, 4))
    (a1, da_dgate1, da_dup1), landed = _ffn_up("ffn1", n1, full["w1_up"], gate1, carry)
    gathered(names, landed)
    alone("gather_d", (d2d, 2))
    carry, names = stages((d2d, 3), (far, 4), (near, 5))
    h1, landed = _ffn_down("ffn1", a1, full["w1_down"], x, carry)
    gathered(names, landed)
    ffn1 = (n1, a1, da_dgate1, da_dup1)
    nm = _rms_fwd("mix_norm", h1, w["norm_mix"])
    carry, names = stages((d2d, 4), (far, 5), (near, 6))
    z, landed = _mm_nn("in_proj", nm, full["w_in"], F32, tm=1024, tn=512, tk=2048, carry=carry)
    gathered(names, landed)

    disc, disc_vjp = jax.vjp(_ssm_discretize, w["ssm_log_dt"][0], w["ssm_a_re"], w["ssm_a_im"], w["ssm_b_re"],
                             w["ssm_b_im"])
    abar_r, abar_i, bbar_r, bbar_i = disc
    nb = n_groups // GROUPS_PER_BLOCK
    wb = jnp.concatenate([_blockdiag_in(bbar_r), _blockdiag_in(bbar_i)], axis=-1).astype(BF16)
    wc = jnp.concatenate([_blockdiag_out(w["ssm_c_re"]), -_blockdiag_out(w["ssm_c_im"])], axis=1).astype(BF16)
    abar = jnp.concatenate([abar_r.reshape(nb, -1), abar_i.reshape(nb, -1)], axis=-1)
    abar_conj = jnp.concatenate([abar_r.reshape(nb, -1), -abar_i.reshape(nb, -1)], axis=-1)
    carry, names = stages((d2d, 5), (far, 6), (near, 7))
    (states, y_pre, yg), landed = _ssm_scan_fwd(z, wb, wc, abar, w["ssm_d"], carry)
    gathered(names, landed)
    q = _mm_nn("glu_proj", yg, full["ssm_w_glu"], F32, tm=1024, tn=1024, tk=1024)

    def glu_norm(rows, pars):
        yv = _gelu(rows[0]) * _sigmoid(rows[1])
        yh, _ = _rms_stats(yv)
        return [yh * pars[0]], []

    yn_ssm = _rowwise("ssm_glu_norm", glu_norm, [y_pre, q], [w["norm_ssm_out"]], [(d_ssm, BF16)])[0]

    tril = jnp.tril(jnp.ones((CHUNK, CHUNK), dtype=bool))
    wm = jnp.where(tril[None], w["gmlp_w_s"], 0.0).astype(BF16)
    bias = jnp.repeat(w["gmlp_b_s"].T, GMLP_HEAD, axis=1)
    yn_gmlp = _gmlp_fwd(z, w["gmlp_norm_v"], wm, bias, w["norm_gmlp_out"])
    ycat = jnp.concatenate([yn_ssm, yn_gmlp], axis=1)
    carry, names = stages((d2d, 6), (far, 7))
    h2, landed = _mm_nn("out_proj", ycat, full["w_out"], F32, res=h1, alpha=1.0, tm=512, tn=1024, tk=2048,
                        carry=carry)
    gathered(names, landed)

    n2 = _rms_fwd("ffn2_norm", h2, w["norm_ffn2"])
    carry, names = stages((d2d, 7))
    (a2, da_dgate2, da_dup2), landed = _ffn_gateup("ffn2", n2, full["w2_gate"], full["w2_up"], carry)
    gathered(names, landed)
    h3 = _ffn_down("ffn2", a2, full["w2_down"], h2)
    ffn2 = (n2, a2, da_dgate2, da_dup2)
    npl = _rms_fwd("ple_norm", h3, w["norm_ple"])
    pp = _mm_nn("ple_proj", p, full["w_ple_proj"], F32, tm=1024, tn=1024, tk=2048)
    dgq, dpp, dh4, g_norm_final, loss_part = _ple_head(npl, full["w_ple_gate"], h3, pp, tgt, row(w["norm_final"]))
    reducer = _Reducer(shard_shape, c_arr, qc_arr)
    (g_w_ple_proj,) = _mm_tn("ple_dwproj", p, [dpp], BF16, tm=256, tn=1024, tk=4096)
    (g_w_ple_gate,) = _mm_tn("ple_dwgate", npl, [dgq], BF16, tm=512, tn=1024, tk=4096)
    ple = reducer.swap(["w_ple_gate", "w_ple_proj"], [g_w_ple_gate, g_w_ple_proj])
    dnpl, swapped = _mm_nt_sum("ple_dnorm_in", [dgq], [full["w_ple_gate"]], BF16, tm=512, tn=1024, tk=2048,
                               carry=ple.carry)
    ple = reducer.send(ple, swapped)
    dh3, dh3_b, g_norm_ple = _rms_bwd("ple_dnorm", h3, w["norm_ple"], [dnpl], dh4, 0.5)

    dh2, dh2_b, g_norm_ffn2, got, _, up2_w, _ = _ffn_bwd(
        "ffn2", ["w2_gate", "w2_up", "w2_down"], h2, w["norm_ffn2"], full["w2_gate"], full["w2_up"], full["w2_down"],
        ffn2, dh3, dh3_b, 1.0, reducer, riding=ple.carry, last_hop_later=True)
    reducer.end(ple, got)

    dycat = _mm_nt_sum("out_dproj", [dh2_b], [full["w_out"]], BF16, tm=512, tn=1024, tk=2048)
    (g_w_out,) = _mm_tn("out_dw", ycat, [dh2_b], BF16, tm=512, tn=1024, tk=4096)

    dz, g_norm_gmlp_out, g_gmlp_norm_v, g_wm, g_s = _gmlp_bwd(z, dycat, w["gmlp_norm_v"], wm, bias,
                                                                  w["norm_gmlp_out"])
    g_gmlp_w_s = jnp.where(tril[None], g_wm, 0.0)
    g_gmlp_b_s = g_s.reshape(CHUNK, -1, GMLP_HEAD).sum(axis=-1).T

    def glu_bwd(rows, pars):
        dyn, ypre, qv = rows
        ygv = _gelu(ypre)
        sg = _sigmoid(qv)
        dy, dg = _rms_backward(ygv * sg, pars[0], dyn.astype(F32))
        return [dy * ygv * sg * (1.0 - sg), dy * sg], [dg]

    dq, dyg_part, g_norm_ssm_out = _rowwise("ssm_dglu", glu_bwd, [(dycat, d_ssm, 0), y_pre, q], [w["norm_ssm_out"]],
                                            [(d_ssm, BF16), (d_ssm, F32)], [(1, d_ssm)])
    dyg_proj = _mm_nt_sum("glu_dproj", [dq], [full["ssm_w_glu"]], F32, tm=1024, tn=1024, tk=1024)
    (g_ssm_w_glu,) = _mm_tn("glu_dw", yg, [dq], BF16, tm=512, tn=1024, tk=4096)

    def gelu_bwd(rows, pars):
        return [(rows[0] + rows[1]) * _gelu_grad(rows[2])], []

    dy_pre = _rowwise("ssm_dgelu", gelu_bwd, [dyg_part, dyg_proj, y_pre], [], [(d_ssm, F32)])[0]
    mixers = reducer.swap(["w_out", "ssm_w_glu"], [g_w_out, g_ssm_w_glu])
    me = 4 * xi + 2 * yi + ci
    small = {"gmlp_norm_v": g_gmlp_norm_v, "gmlp_w_s": g_gmlp_w_s, "gmlp_b_s": g_gmlp_b_s,
             "norm_gmlp_out": g_norm_gmlp_out, "norm_ssm_out": g_norm_ssm_out, "norm_ffn2": g_norm_ffn2,
             "norm_ple": g_norm_ple, "norm_final": g_norm_final}
    before_scan = [n for n in SMALL if n in small]
    scan_blocks = _place_block(_pack([small[n] for n in before_scan] + [loss_part[:, :1]]), me)
    carry, split = _ride(up2_w.carry, mixers.carry, _exchange_carry(scan_blocks))
    (dz, g_wb, g_wc, g_abar, g_ssm_d), results = _ssm_scan_bwd(z, dy_pre, states, wb, wc, abar_conj, w["ssm_d"],
                                                              dz, carry)
    got, swapped, (scan_blocks,) = split(results)
    reducer.end(up2_w, got)
    mixers = reducer.send(mixers, swapped)
    g_abar = jnp.transpose(g_abar, (1, 0, 2)).reshape(nb, -1)
    sw = g_abar.shape[-1] // 2
    g_bbar_r = _blockdiag_in_grad(g_wb[..., :sw], SSM_STATE, SSM_GROUP)
    g_bbar_i = _blockdiag_in_grad(g_wb[..., sw:], SSM_STATE, SSM_GROUP)
    g_ssm_c_re = _blockdiag_out_grad(g_wc[:, :sw, :], SSM_GROUP, SSM_STATE)
    g_ssm_c_im = -_blockdiag_out_grad(g_wc[:, sw:, :], SSM_GROUP, SSM_STATE)
    g_abar_r = g_abar[..., :sw].reshape(n_groups, SSM_STATE)
    g_abar_i = g_abar[..., sw:].reshape(n_groups, SSM_STATE)
    g_ssm_log_dt, g_ssm_a_re, g_ssm_a_im, g_ssm_b_re, g_ssm_b_im = disc_vjp((g_abar_r, g_abar_i, g_bbar_r, g_bbar_i))

    (g_w_in,), got = _mm_tn("in_dw", nm, [dz], BF16, tm=512, tn=1536, tk=4096, carry=mixers.carry)
    reducer.end(mixers, got)
    in_w = reducer.swap(["w_in"], [g_w_in])
    dnm, swapped = _mm_nt_sum("in_dproj", [dz], [full["w_in"]], BF16, tm=1024, tn=1024, tk=3072, carry=in_w.carry)
    in_w = reducer.send(in_w, swapped)
    dh1, dh1_b, g_norm_mix = _rms_bwd("mix_dnorm", h1, w["norm_mix"], [dnm], dh2, 0.5)
    small = {"norm_mix": g_norm_mix, "ssm_log_dt": g_ssm_log_dt, "ssm_a_re": g_ssm_a_re,
             "ssm_a_im": g_ssm_a_im, "ssm_b_re": g_ssm_b_re, "ssm_b_im": g_ssm_b_im, "ssm_c_re": g_ssm_c_re,
             "ssm_c_im": g_ssm_c_im, "ssm_d": g_ssm_d}
    after_scan = [n for n in SMALL if n in small]
    ffn_blocks = _place_block(_pack([small[n] for n in after_scan]), me)
    done = [n for n in BIG if n in reducer.halves]
    carry, split = _ride(in_w.carry, _share_carry([reducer.halves[n] for n in done], [shard_shape[n] for n in done]))
    rest = [n for n in BIG if n not in done and n != "w_in"]

    def share_rest():
        return _share_carry([reducer.halves[n] for n in rest], [shard_shape[n] for n in rest])

    dx, _, g_norm_ffn1, results, (ffn_blocks,), _, shared_rest = _ffn_bwd(
        "ffn1", ["w1_gate", "w1_up", "w1_down"], x, w["norm_ffn1"], full["w1_gate"], full["w1_up"], full["w1_down"],
        ffn1, dh1, dh1_b, 1.0, reducer, riding=carry, riding_dwd=_exchange_carry(ffn_blocks), before_norm=share_rest)
    got, shared = split(results)
    grad = dict(zip(done, shared))
    reducer.end(in_w, got)
    scan_grads = _unpack(_sum_blocks("sum_before_scan", scan_blocks), [w[n].shape for n in before_scan] + [(1,)])
    loss = scan_grads[-1].reshape(())
    early = before_scan + after_scan
    early_grads = scan_grads[:-1] + _unpack(_sum_blocks("sum_after_scan", ffn_blocks), [w[n].shape for n in after_scan])

    grad.update(zip(rest, shared_rest))
    grad.update(zip(early, early_grads))

    small_shapes = [w[n].shape for n in SMALL]
    delta, new_m, new_v = {}, {}, {}
    late_blocks = _place_block(_pack([g_norm_ffn1]), me)
    last = _join(_share_carry([reducer.halves["w_in"]], [shard_shape["w_in"]]), _exchange_carry(late_blocks))
    for n in BIG:
        if n == BIG[0]:
            (delta[n], new_m[n], new_v[n], grad[n]), (grad["w_in"], late_blocks) = _adamw(
                "adamw_" + n, w[n], grad[n], m[n], v[n], carry=last)
        else:
            delta[n], new_m[n], new_v[n], grad[n] = _adamw("adamw_" + n, w[n], grad[n], m[n], v[n])
    grad["norm_ffn1"] = _unpack(_sum_blocks("sum_first_norm", late_blocks), [w["norm_ffn1"].shape])[0]
    d_p, m_p, v_p, _ = _adamw("adamw_small", _pack([w[n] for n in SMALL]), _pack([grad[n] for n in SMALL]),
                              _pack([m[n] for n in SMALL]), _pack([v[n] for n in SMALL]))
    for name_list, packed in ((delta, d_p), (new_m, m_p), (new_v, v_p)):
        for n, a in zip(SMALL, _unpack(packed, small_shapes)):
            name_list[n] = a
    return loss, dx, grad, delta, new_m, new_v


def kernel(x, p, norm_ffn1, w1_gate, w1_up, w1_down, norm_mix, w_in, ssm_log_dt, ssm_a_re, ssm_a_im, ssm_b_re, ssm_b_im, ssm_c_re, ssm_c_im, ssm_d, ssm_w_glu, gmlp_norm_v, gmlp_w_s, gmlp_b_s, norm_ssm_out, norm_gmlp_out, w_out, norm_ffn2, w2_gate, w2_up, w2_down, norm_ple, w_ple_gate, w_ple_proj, norm_final, loss_target, m_norm_ffn1, m_w1_gate, m_w1_up, m_w1_down, m_norm_mix, m_w_in, m_ssm_log_dt, m_ssm_a_re, m_ssm_a_im, m_ssm_b_re, m_ssm_b_im, m_ssm_c_re, m_ssm_c_im, m_ssm_d, m_ssm_w_glu, m_gmlp_norm_v, m_gmlp_w_s, m_gmlp_b_s, m_norm_ssm_out, m_norm_gmlp_out, m_w_out, m_norm_ffn2, m_w2_gate, m_w2_up, m_w2_down, m_norm_ple, m_w_ple_gate, m_w_ple_proj, m_norm_final, v_norm_ffn1, v_w1_gate, v_w1_up, v_w1_down, v_norm_mix, v_w_in, v_ssm_log_dt, v_ssm_a_re, v_ssm_a_im, v_ssm_b_re, v_ssm_b_im, v_ssm_c_re, v_ssm_c_im, v_ssm_d, v_ssm_w_glu, v_gmlp_norm_v, v_gmlp_w_s, v_gmlp_b_s, v_norm_ssm_out, v_norm_gmlp_out, v_w_out, v_norm_ffn2, v_w2_gate, v_w2_up, v_w2_down, v_norm_ple, v_w_ple_gate, v_w_ple_proj, v_norm_final):
    given = dict(locals())
    shapes = {n: given[n].shape for n in WEIGHTS}

    def block(name):
        a = given[name]
        if a.ndim == 1:
            return a.reshape(1, -1)
        return a[0] if a.ndim >= 3 else a

    w = {n: block(n) for n in WEIGHTS}
    m = {n: block("m_" + n) for n in WEIGHTS}
    v = {n: block("v_" + n) for n in WEIGHTS}
    loss, dx, grad, delta, new_m, new_v = _step(x[0], p[0, 0], loss_target[0], w, m, v)
    outs = [loss, dx[None]]
    for tree in (grad, delta, new_m, new_v):
        outs += [tree[n].reshape(shapes[n]) for n in WEIGHTS]
    return tuple(outs)
```

```python
import functools
import math

import jax
import jax.numpy as jnp
from jax import lax
from jax.experimental import pallas as pl
from jax.experimental.pallas import tpu as pltpu

F32 = jnp.float32
BF16 = jnp.bfloat16
EPS = 1e-6
SSM_GROUP = 16
SSM_STATE = 64
GROUPS_PER_BLOCK = 8
GMLP_HEAD = 128
CHUNK = 128
ADAM_LR = 0.001
ADAM_B1 = 0.9
ADAM_B2 = 0.999
ADAM_EPS = 1e-08
ADAM_WD = 0.01
ADAM_STEP = 10
N_CHIPS = 4
N_DEV = 8
LANES = 128
VMEM_LIMIT_BYTES = 56 * 1024 * 1024
MESH = pl.DeviceIdType.MESH
GELU_C = math.sqrt(2.0 / math.pi)
GELU_A = 0.044715

_DOT_DIMS = {
    "nn": (((1,), (0,)), ((), ())),
    "nt": (((1,), (1,)), ((), ())),
    "tn": (((0,), (0,)), ((), ())),
}


def _tile(dim, pref, align):
    if dim <= pref:
        return dim
    t = (pref // align) * align
    while t >= align:
        if dim % t == 0:
            return t
        t -= align
    return dim


def _params(semantics):
    return pltpu.CompilerParams(dimension_semantics=semantics, vmem_limit_bytes=VMEM_LIMIT_BYTES)


def _gelu(x):
    return 0.5 * x * (1.0 + jnp.tanh(GELU_C * (x + GELU_A * x * x * x)))


def _gelu_grad(x):
    t = jnp.tanh(GELU_C * (x + GELU_A * x * x * x))
    return 0.5 * (1.0 + t) + 0.5 * x * (1.0 - t * t) * GELU_C * (1.0 + 3.0 * GELU_A * x * x)


def _sigmoid(x):
    return 1.0 / (1.0 + jnp.exp(-x))


def _dot(a, b, mode):
    return lax.dot_general(a.astype(BF16), b.astype(BF16), _DOT_DIMS[mode], preferred_element_type=F32)


class _Carry:
    def __init__(self, arrays, out_shapes, aliases, n_copies, copies):
        self.arrays = list(arrays)
        self.out_shapes = list(out_shapes)
        self.aliases = dict(aliases)
        self.n_copies = n_copies
        self.copies = copies

    def scratch(self):
        return [pltpu.SemaphoreType.DMA((self.n_copies,)), pltpu.SemaphoreType.DMA((self.n_copies,))]

    def split(self, refs):
        n_in, n_out = len(self.arrays), len(self.out_shapes)
        return refs[:n_in], refs[n_in:n_in + n_out], refs[n_in + n_out], refs[n_in + n_out + 1]

    def start(self, refs):
        for cp in self.copies(*self.split(refs)):
            cp.start()

    def wait(self, refs):
        for cp in self.copies(*self.split(refs)):
            cp.wait()


class _SemRange:
    def __init__(self, sems, offset):
        self.sems, self.offset = sems, offset

    @property
    def at(self):
        return self

    def __getitem__(self, k):
        return self.sems.at[self.offset + k]


def _join(first, second):
    n_in, n_out = len(first.arrays), len(first.out_shapes)
    aliases = dict(first.aliases)
    aliases.update({n_in + i: n_out + o for i, o in second.aliases.items()})

    def copies(ops, res, send_sems, recv_sems):
        return (first.copies(ops[:n_in], res[:n_out], send_sems, recv_sems)
                + second.copies(ops[n_in:], res[n_out:], _SemRange(send_sems, first.n_copies),
                                _SemRange(recv_sems, first.n_copies)))

    return _Carry(first.arrays + second.arrays, first.out_shapes + second.out_shapes, aliases,
                  first.n_copies + second.n_copies, copies)


_ANY = pl.BlockSpec(memory_space=pl.ANY)


def _comm_call(name, carry):
    def body(*refs):
        carry.start(refs)
        carry.wait(refs)

    n_in = len(carry.arrays)
    return pl.pallas_call(
        body,
        name=name,
        in_specs=[_ANY] * n_in,
        out_specs=[_ANY] * len(carry.out_shapes),
        out_shape=carry.out_shapes,
        input_output_aliases=carry.aliases,
        scratch_shapes=carry.scratch(),
    )(*carry.arrays)


def _carried_call(body, carry, *, name, grid, in_specs, out_specs, out_shape, scratch_shapes, semantics, args,
                  aliases=None):
    aliases = dict(aliases or {})
    if carry is None:
        res = pl.pallas_call(body, name=name, grid=grid, in_specs=in_specs, out_specs=out_specs, out_shape=out_shape,
                             scratch_shapes=scratch_shapes, input_output_aliases=aliases,
                             compiler_params=_params(semantics))(*args)
        return res, []
    n_in, n_out, n_scr = len(in_specs), len(out_specs), len(scratch_shapes)
    nci, nco = len(carry.arrays), len(carry.out_shapes)

    def wrapped(*refs):
        ins = refs[:n_in]
        outs = refs[n_in + nci:n_in + nci + n_out]
        scr = refs[n_in + nci + n_out + nco:n_in + nci + n_out + nco + n_scr]
        c_refs = (refs[n_in:n_in + nci] + refs[n_in + nci + n_out:n_in + nci + n_out + nco]
                  + refs[n_in + nci + n_out + nco + n_scr:])
        first = functools.reduce(jnp.logical_and, [pl.program_id(d) == 0 for d in range(len(grid))])
        last = functools.reduce(jnp.logical_and, [pl.program_id(d) == grid[d] - 1 for d in range(len(grid))])

        @pl.when(first)
        def _():
            carry.start(c_refs)

        body(*ins, *outs, *scr)

        @pl.when(last)
        def _():
            carry.wait(c_refs)

    res = pl.pallas_call(
        wrapped,
        name=name,
        grid=grid,
        in_specs=list(in_specs) + [_ANY] * nci,
        out_specs=list(out_specs) + [_ANY] * nco,
        out_shape=list(out_shape) + carry.out_shapes,
        input_output_aliases={**aliases, **{n_in + i: n_out + o for i, o in carry.aliases.items()}},
        scratch_shapes=list(scratch_shapes) + carry.scratch(),
        compiler_params=_params(("arbitrary",) * len(grid)),
    )(*args, *carry.arrays)
    return res[:n_out], res[n_out:]


def _matmul(name, mode, a_list, b_list, products, out_dtypes, epilogue, extras=(), tm=512, tn=512, tk=2048,
            carry=None, n_part=(0, 1)):
    a0, b0 = a_list[0], b_list[0]
    if mode == "tn":
        k_dim, m_dim = a0.shape
    else:
        m_dim, k_dim = a0.shape
    n_dim = (b0.shape[0] if mode == "nt" else b0.shape[1]) // n_part[1]
    tm = _tile(m_dim, tm, LANES)
    tn = _tile(n_dim, tn, LANES)
    tk = _tile(k_dim, tk, LANES)
    nk = k_dim // tk
    j0 = n_part[0] * (n_dim // tn)
    chunk = 2 * LANES if (nk == 1 and epilogue is not _identity and tn % (2 * LANES) == 0) else tn
    n_acc = 1 + max(p[2] for p in products)
    na, nb, ne, no = len(a_list), len(b_list), len(extras), len(out_dtypes)

    if mode == "tn":
        a_spec = pl.BlockSpec((tk, tm), lambda i, j, k: (k, i))
    else:
        a_spec = pl.BlockSpec((tm, tk), lambda i, j, k: (i, k))
    if mode == "nt":
        b_spec = pl.BlockSpec((tn, tk), lambda i, j, k: (j0 + j, k))
    else:
        b_spec = pl.BlockSpec((tk, tn), lambda i, j, k: (k, j0 + j))
    t_spec = pl.BlockSpec((tm, tn), lambda i, j, k: (i, j))

    def body(*refs):
        a_refs = refs[:na]
        b_refs = refs[na:na + nb]
        e_refs = refs[na + nb:na + nb + ne]
        o_refs = refs[na + nb + ne:na + nb + ne + no]
        acc_refs = refs[na + nb + ne + no:]

        def partial_sums(cols):
            sums = [None] * n_acc
            for ai, bi, ci in products:
                b = b_refs[bi][cols, :] if mode == "nt" else b_refs[bi][:, cols]
                d = _dot(a_refs[ai][...], b, mode)
                sums[ci] = d if sums[ci] is None else sums[ci] + d
            return sums

        def finish(accs, cols):
            outs = epilogue(accs, [e[:, cols] for e in e_refs])
            for o_ref, o in zip(o_refs, outs):
                o_ref[:, cols] = o.astype(o_ref.dtype)

        if nk == 1:
            for c0 in range(0, tn, chunk):
                finish(partial_sums(slice(c0, c0 + chunk)), slice(c0, c0 + chunk))
        else:
            sums = partial_sums(slice(None))
            finish = functools.partial(finish, cols=slice(None))
            k = pl.program_id(2)

            @pl.when(k == 0)
            def _():
                for acc, s in zip(acc_refs, sums):
                    acc[...] = s

            @pl.when(k > 0)
            def _():
                for acc, s in zip(acc_refs, sums):
                    acc[...] += s

            @pl.when(k == nk - 1)
            def _():
                finish([acc[...] for acc in acc_refs])

    scratch = [pltpu.VMEM((tm, tn), F32) for _ in range(n_acc)] if nk > 1 else []
    outs, carried = _carried_call(
        body, carry,
        name=name,
        grid=(m_dim // tm, n_dim // tn, nk),
        in_specs=[a_spec] * na + [b_spec] * nb + [t_spec] * ne,
        out_specs=[t_spec] * no,
        out_shape=[jax.ShapeDtypeStruct((m_dim, n_dim), dt) for dt in out_dtypes],
        scratch_shapes=scratch,
        semantics=("parallel", "parallel", "arbitrary"),
        args=[*a_list, *b_list, *extras],
    )
    return (outs, carried) if carry else outs


def _identity(accs, extras):
    return accs


def _single(result, carry):
    return (result[0][0], result[1]) if carry else result[0]


def _mm_nn(name, a, b, out_dtype, res=None, alpha=1.0, carry=None, **tiles):
    if res is None:
        return _single(_matmul(name, "nn", [a], [b], [(0, 0, 0)], [out_dtype], _identity, carry=carry, **tiles), carry)

    def epilogue(accs, extras):
        return [extras[0] + alpha * accs[0]]

    return _single(_matmul(name, "nn", [a], [b], [(0, 0, 0)], [out_dtype], epilogue, extras=(res,), carry=carry,
                           **tiles), carry)


def _mm_nt_sum(name, a_list, b_list, out_dtype, carry=None, n_part=(0, 1), **tiles):
    products = [(i, i, 0) for i in range(len(a_list))]
    return _single(_matmul(name, "nt", a_list, b_list, products, [out_dtype], _identity, carry=carry, n_part=n_part,
                           **tiles), carry)


def _mm_tn(name, a, b_list, out_dtype, carry=None, **tiles):
    products = [(0, i, i) for i in range(len(b_list))]
    return _matmul(name, "tn", [a], b_list, products, [out_dtype] * len(b_list), _identity, carry=carry, **tiles)


def _rowwise(name, fn, row_ins, par_ins, row_outs, acc_outs=(), tr=512, carry=None):
    first = row_ins[0][0] if isinstance(row_ins[0], tuple) else row_ins[0]
    t_dim = first.shape[0]
    tr = _tile(t_dim, tr, 16)
    arrays, specs = [], []
    for r in row_ins:
        if isinstance(r, tuple):
            arr, width, blk = r
            specs.append(pl.BlockSpec((tr, width), lambda i, blk=blk: (i, blk)))
        else:
            arr = r
            specs.append(pl.BlockSpec((tr, arr.shape[1]), lambda i: (i, 0)))
        arrays.append(arr)
    for p in par_ins:
        arrays.append(p)
        specs.append(pl.BlockSpec(p.shape, lambda i, nd=p.ndim: (0,) * nd))
    nr, npar, nro, nacc = len(row_ins), len(par_ins), len(row_outs), len(acc_outs)

    def body(*refs):
        rows = [r[...] for r in refs[:nr]]
        pars = [p[...] for p in refs[nr:nr + npar]]
        o_refs = refs[nr + npar:nr + npar + nro]
        acc_refs = refs[nr + npar + nro:]
        outs, accs = fn(rows, pars)
        for o_ref, o in zip(o_refs, outs):
            o_ref[...] = o.astype(o_ref.dtype)
        if nacc:
            @pl.when(pl.program_id(0) == 0)
            def _():
                for a_ref in acc_refs:
                    a_ref[...] = jnp.zeros_like(a_ref)

            for a_ref, a in zip(acc_refs, accs):
                a_ref[...] += a

    out_shape = [jax.ShapeDtypeStruct((t_dim, c), dt) for c, dt in row_outs]
    out_shape += [jax.ShapeDtypeStruct(s, F32) for s in acc_outs]
    out_specs = [pl.BlockSpec((tr, c), lambda i: (i, 0)) for c, _ in row_outs]
    out_specs += [pl.BlockSpec(s, lambda i: (0, 0)) for s in acc_outs]
    res, carried = _carried_call(body, carry, name=name, grid=(t_dim // tr,), in_specs=specs, out_specs=out_specs,
                                 out_shape=out_shape, scratch_shapes=[], semantics=("arbitrary",), args=arrays)
    return (res, carried) if carry else res


def _rms_stats(x):
    r = lax.rsqrt(jnp.mean(x * x, axis=-1, keepdims=True) + EPS)
    return x * r, r


def _rms_backward(x, g, dy):
    xh, r = _rms_stats(x)
    a = dy * g
    dx = r * (a - xh * jnp.mean(a * xh, axis=-1, keepdims=True))
    return dx, jnp.sum(dy * xh, axis=0, keepdims=True)


def _rms_fwd(name, x, g, carry=None):
    def fn(rows, pars):
        xh, _ = _rms_stats(rows[0])
        return [xh * pars[0]], []

    res = _rowwise(name, fn, [x], [g], [(x.shape[1], BF16)], carry=carry)
    return (res[0][0], res[1]) if carry else res[0]


def _rms_bwd(name, x, g, dy_parts, dres, scale):
    def fn(rows, pars):
        dy = (rows[2] if len(rows) == 3 else jnp.concatenate(rows[2:], axis=1)).astype(F32)
        dx, dg = _rms_backward(rows[0], pars[0], dy)
        tot = rows[1] + dx
        return [tot, scale * tot], [dg]

    d = x.shape[1]
    return _rowwise(name, fn, [x, dres, *dy_parts], [g], [(d, F32), (d, BF16)], [(1, d)], tr=256)


def _cast_into_gathered(name, w, kind, q_arr):
    rs, cs = w.shape
    tr = _tile(rs, 256, 16)
    nr = rs // tr

    def body(q_ref, w_ref, o_ref):
        o_ref[...] = w_ref[...].astype(BF16)

    if kind == "row":
        o_spec = pl.BlockSpec((tr, cs), lambda i, q_ref: (q_ref[0] * nr + i, 0))
    else:
        o_spec = pl.BlockSpec((tr, cs), lambda i, q_ref: (i, q_ref[0]))
    return pl.pallas_call(
        body,
        name=name,
        grid_spec=pltpu.PrefetchScalarGridSpec(num_scalar_prefetch=1, grid=(nr,),
                                               in_specs=[pl.BlockSpec((tr, cs), lambda i, q_ref: (i, 0))],
                                               out_specs=o_spec),
        out_shape=jax.ShapeDtypeStruct(_full_shape(kind, (rs, cs)), BF16),
        compiler_params=_params(("parallel",)),
    )(q_arr, w)


def _swiglu_tiles(gate, up):
    s = _sigmoid(gate)
    silu = gate * s
    return [silu * up, up * (s * (1.0 + gate * (1.0 - s))), silu]


def _ffn_gateup(tag, n, wg, wu, carry):
    def act(accs, extras):
        return _swiglu_tiles(accs[0], accs[1])

    return _matmul(tag + "_gateup", "nn", [n], [wg, wu], [(0, 0, 0), (0, 1, 1)], [BF16] * 3, act,
                   tm=1024, tn=512, tk=2048, carry=carry)


def _ffn_up(tag, n, wu, gate, carry):
    def act(accs, extras):
        return _swiglu_tiles(extras[0].astype(F32), accs[0])

    return _matmul(tag + "_up", "nn", [n], [wu], [(0, 0, 0)], [BF16] * 3, act, extras=(gate,),
                   tm=1024, tn=512, tk=2048, carry=carry)


def _ffn_down(tag, a, wd, h, carry=None):
    return _mm_nn(tag + "_down", a, wd, F32, res=h, alpha=0.5, tm=1024, tn=512, tk=5632, carry=carry)


def _ffn_bwd(tag, names, h, g, wg, wu, wd, saved, dh, dfb, next_scale, reducer, riding=None, riding_dwd=None,
             last_hop_later=False):
    n, a, da_dgate, da_dup = saved

    def act_bwd(accs, extras):
        return [accs[0] * extras[0].astype(F32), accs[0] * extras[1].astype(F32)]

    dact = _matmul(tag + "_dact", "nt", [dfb], [wd], [(0, 0, 0)], [BF16, BF16], act_bwd, extras=(da_dgate, da_dup),
                   tm=1024, tn=512, tk=2048, carry=riding)
    (dgp, du), rode = dact if riding else (dact, [])
    dwd_call = _mm_tn(tag + "_dwd", a, [dfb], BF16, tm=512, tn=2048, tk=4096, carry=riding_dwd)
    (dwd,), rode_dwd = dwd_call if riding_dwd else (dwd_call, [])
    down = reducer.swap(names[2:], [dwd])
    (dwg,), swapped = _mm_tn(tag + "_dwg", n, [dgp], BF16, tm=512, tn=1408, tk=4096, carry=down.carry)
    down = reducer.send(down, swapped)
    gate_w = reducer.swap(names[:1], [dwg])
    carry, split = _ride(down.carry, gate_w.carry)
    (dwu,), results = _mm_tn(tag + "_dwu", n, [du], BF16, tm=512, tn=1408, tk=4096, carry=carry)
    got, swapped = split(results)
    reducer.end(down, got)
    gate_w = reducer.send(gate_w, swapped)
    up_w = reducer.swap(names[1:2], [dwu])
    carry, split = _ride(gate_w.carry, up_w.carry)
    halves = 1 if last_hop_later else 2
    tiles = dict(tm=512, tn=512, tk=5632) if last_hop_later else dict(tm=1024, tn=1024, tk=1408)
    dn_lo, results = _mm_nt_sum(tag + "_dn_lo", [dgp, du], [wg, wu], BF16, carry=carry, n_part=(0, halves), **tiles)
    got, swapped = split(results)
    reducer.end(gate_w, got)
    up_w = reducer.send(up_w, swapped)
    dn = [dn_lo]
    if not last_hop_later:
        dn_hi, got = _mm_nt_sum(tag + "_dn_hi", [dgp, du], [wg, wu], BF16, tm=1024, tn=1024, tk=1408,
                                carry=up_w.carry, n_part=(1, 2))
        reducer.end(up_w, got)
        dn.append(dn_hi)
    dh_in, dh_in_b, dg = _rms_bwd(tag + "_dnorm", h, g, dn, dh, next_scale)
    return dh_in, dh_in_b, dg, rode, rode_dwd, (up_w if last_hop_later else None)


def _ssm_discretize(log_dt, a_re, a_im, b_re, b_im):
    dt = jnp.exp(log_dt)[:, None]
    lr = jnp.minimum(a_re, -1e-4)
    li = a_im
    mag = jnp.exp(lr * dt)
    ang = li * dt
    abar_r = mag * jnp.cos(ang)
    abar_i = mag * jnp.sin(ang)
    den = lr * lr + li * li
    xr = abar_r - 1.0
    xi = abar_i
    zr = (xr * lr + xi * li) / den
    zi = (xi * lr - xr * li) / den
    bbar_r = zr[..., None] * b_re - zi[..., None] * b_im
    bbar_i = zr[..., None] * b_im + zi[..., None] * b_re
    return abar_r, abar_i, bbar_r, bbar_i


def _blockdiag_in(b):
    g, n, p = b.shape
    nb = g // GROUPS_PER_BLOCK
    eye = jnp.eye(GROUPS_PER_BLOCK, dtype=b.dtype)
    b4 = b.reshape(nb, GROUPS_PER_BLOCK, n, p)
    return jnp.einsum("sgnp,gh->sgphn", b4, eye).reshape(nb, GROUPS_PER_BLOCK * p, GROUPS_PER_BLOCK * n)


def _blockdiag_in_grad(gw, n, p):
    nb = gw.shape[0]
    eye = jnp.eye(GROUPS_PER_BLOCK, dtype=gw.dtype)
    g5 = gw.reshape(nb, GROUPS_PER_BLOCK, p, GROUPS_PER_BLOCK, n)
    return jnp.einsum("sgphn,gh->sgnp", g5, eye).reshape(nb * GROUPS_PER_BLOCK, n, p)


def _blockdiag_out(c):
    g, p, n = c.shape
    nb = g // GROUPS_PER_BLOCK
    eye = jnp.eye(GROUPS_PER_BLOCK, dtype=c.dtype)
    c4 = c.reshape(nb, GROUPS_PER_BLOCK, p, n)
    return jnp.einsum("sgpn,gh->shngp", c4, eye).reshape(nb, GROUPS_PER_BLOCK * n, GROUPS_PER_BLOCK * p)


def _blockdiag_out_grad(gw, p, n):
    nb = gw.shape[0]
    eye = jnp.eye(GROUPS_PER_BLOCK, dtype=gw.dtype)
    g5 = gw.reshape(nb, GROUPS_PER_BLOCK, n, GROUPS_PER_BLOCK, p)
    return jnp.einsum("shngp,gh->sgpn", g5, eye).reshape(nb * GROUPS_PER_BLOCK, p, n)


def _ssm_scan_fwd(z, wb, wc, abar, d, carry=None):
    t_dim = z.shape[0]
    nb, cb, sw2 = wb.shape
    nl = sw2 // LANES
    hl = nl // 2
    tt = _tile(t_dim, 256, 8)
    nt = t_dim // tt

    def body(z_ref, wb_ref, wc_ref, a_ref, d_ref, s_ref, y_ref, yg_ref, drive_ref, st_ref):
        @pl.when(pl.program_id(0) == 0)
        def _():
            st_ref[...] = jnp.zeros_like(st_ref)

        u = z_ref[...]
        ub = u.astype(BF16)
        for b in range(nb):
            drive = _dot(ub[:, b * cb:(b + 1) * cb], wb_ref[b], "nn")
            for l in range(nl):
                drive_ref[l, pl.ds(b, tt, stride=nb), :] = drive[:, l * LANES:(l + 1) * LANES]
        a = a_ref[...]
        chunk = lambda v, l: v[:, l * LANES:(l + 1) * LANES]

        def step(t, state):
            rows = pl.ds(pl.multiple_of(t * nb, nb), nb)
            re, im = [], []
            for l in range(hl):
                ar, ai, sr, si = chunk(a, l), chunk(a, hl + l), state[l], state[hl + l]
                nr = ar * sr - ai * si + drive_ref[l, rows, :]
                ni = ar * si + ai * sr + drive_ref[hl + l, rows, :]
                s_ref[l, rows, :] = nr
                s_ref[hl + l, rows, :] = ni
                re.append(nr)
                im.append(ni)
            return tuple(re + im)

        state = lax.fori_loop(0, tt, step, tuple(st_ref[l] for l in range(nl)), unroll=8)
        for l in range(nl):
            st_ref[l] = state[l]
        parts = []
        for b in range(nb):
            s_b = jnp.concatenate([s_ref[l, pl.ds(b, tt, stride=nb), :] for l in range(nl)], axis=1)
            parts.append(_dot(s_b, wc_ref[b], "nn"))
        y = jnp.concatenate(parts, axis=1) + d_ref[...] * u
        y_ref[...] = y
        yg_ref[...] = _gelu(y).astype(BF16)

    full = lambda a: pl.BlockSpec(a.shape, lambda t, nd=a.ndim: (0,) * nd)
    return _carried_call(
        body, carry,
        name="ssm_scan_fwd",
        grid=(nt,),
        in_specs=[pl.BlockSpec((tt, nb * cb), lambda t: (t, 0)), full(wb), full(wc), full(abar), full(d)],
        out_specs=[
            pl.BlockSpec((nl, tt * nb, LANES), lambda t: (0, t, 0)),
            pl.BlockSpec((tt, nb * cb), lambda t: (t, 0)),
            pl.BlockSpec((tt, nb * cb), lambda t: (t, 0)),
        ],
        out_shape=[
            jax.ShapeDtypeStruct((nl, t_dim * nb, LANES), F32),
            jax.ShapeDtypeStruct((t_dim, nb * cb), F32),
            jax.ShapeDtypeStruct((t_dim, nb * cb), BF16),
        ],
        scratch_shapes=[pltpu.VMEM((nl, tt * nb, LANES), F32), pltpu.VMEM((nl, nb, LANES), F32)],
        semantics=("arbitrary",),
        args=[z, wb, wc, abar, d],
    )


def _ssm_scan_bwd(z, dy, states, wb, wc, abar_conj, d, dz_all, carry=None):
    t_dim = z.shape[0]
    nb, cb, sw2 = wb.shape
    nl = sw2 // LANES
    hl = nl // 2
    tt = _tile(t_dim, 256, 8)
    nt = t_dim // tt
    edges = states.reshape(nl, nt, tt * nb, LANES)[:, :, (tt - 1) * nb:, :]
    before = jnp.concatenate([jnp.zeros((nl, 1, nb, LANES), F32), edges[:, :-1]], axis=1).reshape(nl, nt * nb, LANES)

    def body(z_ref, dy_ref, s_ref, sp_ref, wb_ref, wc_ref, a_ref, d_ref, dz_all_ref,
             dz_ref, gwb_ref, gwc_ref, ga_ref, gd_ref, gin_ref, gs_ref, st_ref):
        @pl.when(pl.program_id(0) == 0)
        def _():
            st_ref[...] = jnp.zeros_like(st_ref)
            gwb_ref[...] = jnp.zeros_like(gwb_ref)
            gwc_ref[...] = jnp.zeros_like(gwc_ref)
            ga_ref[...] = jnp.zeros_like(ga_ref)
            gd_ref[...] = jnp.zeros_like(gd_ref)

        u = z_ref[...]
        dyv = dy_ref[...]
        ub = u.astype(BF16)
        dyb = dyv.astype(BF16)
        for b in range(nb):
            gin = _dot(dyb[:, b * cb:(b + 1) * cb], wc_ref[b], "nt")
            for l in range(nl):
                gin_ref[l, pl.ds(b, tt, stride=nb), :] = gin[:, l * LANES:(l + 1) * LANES]
        a = a_ref[...]
        chunk = lambda v, l: v[:, l * LANES:(l + 1) * LANES]

        def step(k, state):
            rows = pl.ds(pl.multiple_of((tt - 1 - k) * nb, nb), nb)
            re, im = [], []
            for l in range(hl):
                ar, ai, gr, gi = chunk(a, l), chunk(a, hl + l), state[l], state[hl + l]
                nr = ar * gr - ai * gi + gin_ref[l, rows, :]
                ni = ar * gi + ai * gr + gin_ref[hl + l, rows, :]
                gs_ref[l, rows, :] = nr
                gs_ref[hl + l, rows, :] = ni
                re.append(nr)
                im.append(ni)
            return tuple(re + im)

        state = lax.fori_loop(0, tt, step, tuple(st_ref[l] for l in range(nl)), unroll=8)
        for l in range(nl):
            st_ref[l] = state[l]

        parts = []
        for b in range(nb):
            cols = slice(b * cb, (b + 1) * cb)
            gs_b = jnp.concatenate([gs_ref[l, pl.ds(b, tt, stride=nb), :] for l in range(nl)], axis=1)
            s_b = jnp.concatenate([s_ref[l, pl.ds(b, tt, stride=nb), :] for l in range(nl)], axis=1)
            parts.append(_dot(gs_b, wb_ref[b], "nt"))
            gwb_ref[b] += _dot(ub[:, cols], gs_b, "tn")
            gwc_ref[b] += _dot(s_b, dyb[:, cols], "tn")
        dz_ref[...] = (jnp.concatenate(parts, axis=1) + d_ref[...] * dyv).astype(BF16)
        gd_ref[...] += jnp.sum(dyv * u, axis=0, keepdims=True)

        row = lax.broadcasted_iota(jnp.int32, (tt * nb, LANES), 0)
        shifted = lambda v: jnp.where(row < nb, 0.0, pltpu.roll(v, nb, 0))
        over_time = lambda v: jnp.sum(v.reshape(tt, nb, LANES), axis=0)
        for l in range(hl):
            g_r, g_i = gs_ref[l], gs_ref[hl + l]
            p_r, p_i = shifted(s_ref[l]), shifted(s_ref[hl + l])
            f_r, f_i = sp_ref[l], sp_ref[hl + l]
            g0_r, g0_i = gs_ref[l, pl.ds(0, nb), :], gs_ref[hl + l, pl.ds(0, nb), :]
            ga_ref[l] += over_time(g_r * p_r + g_i * p_i) + g0_r * f_r + g0_i * f_i
            ga_ref[hl + l] += over_time(g_i * p_r - g_r * p_i) + g0_i * f_r - g0_r * f_i

    rev = lambda t: (nt - 1 - t, 0)
    rev3 = lambda t: (0, nt - 1 - t, 0)
    full = lambda a: pl.BlockSpec(a.shape, lambda t, nd=a.ndim: (0,) * nd)
    return _carried_call(
        body, carry,
        name="ssm_scan_bwd",
        grid=(nt,),
        in_specs=[
            pl.BlockSpec((tt, nb * cb), rev),
            pl.BlockSpec((tt, nb * cb), rev),
            pl.BlockSpec((nl, tt * nb, LANES), rev3),
            pl.BlockSpec((nl, nb, LANES), rev3),
            full(wb), full(wc), full(abar_conj), full(d), _ANY,
        ],
        out_specs=[
            pl.BlockSpec((tt, nb * cb), rev),
            pl.BlockSpec((nb, cb, sw2), lambda t: (0, 0, 0)),
            pl.BlockSpec((nb, sw2, cb), lambda t: (0, 0, 0)),
            pl.BlockSpec((nl, nb, LANES), lambda t: (0, 0, 0)),
            pl.BlockSpec((1, nb * cb), lambda t: (0, 0)),
        ],
        out_shape=[
            jax.ShapeDtypeStruct(dz_all.shape, BF16),
            jax.ShapeDtypeStruct((nb, cb, sw2), F32),
            jax.ShapeDtypeStruct((nb, sw2, cb), F32),
            jax.ShapeDtypeStruct((nl, nb, LANES), F32),
            jax.ShapeDtypeStruct((1, nb * cb), F32),
        ],
        scratch_shapes=[pltpu.VMEM((nl, tt * nb, LANES), F32), pltpu.VMEM((nl, tt * nb, LANES), F32),
                        pltpu.VMEM((nl, nb, LANES), F32)],
        semantics=("arbitrary",),
        args=[z, dy, states, before, wb, wc, abar_conj, d, dz_all],
        aliases={8: 0},
    )


def _gmlp_chunk(zu, zv, gv, wm_ref, bias, n_heads):
    ua = _gelu(zu)
    vg = _gelu(zv)
    xc = vg - jnp.mean(vg, axis=-1, keepdims=True)
    r = lax.rsqrt(jnp.mean(xc * xc, axis=-1, keepdims=True) + EPS)
    vh = xc * r
    vb = (vh * gv).astype(BF16)
    parts = []
    for h in range(n_heads):
        cols = slice(h * GMLP_HEAD, (h + 1) * GMLP_HEAD)
        parts.append(_dot(wm_ref[h], vb[:, cols], "nn"))
    s = jnp.concatenate(parts, axis=1) + bias
    return ua, vh, r, vb, s


def _gmlp_fwd(z, gv, wm, bias, ggo):
    t_dim = z.shape[0]
    dg = gv.shape[1]
    n_heads = dg // GMLP_HEAD
    tr = _tile(t_dim, 256, CHUNK)

    def body(zu_ref, zv_ref, gv_ref, wm_ref, b_ref, ggo_ref, o_ref):
        for ck in range(tr // CHUNK):
            rows = pl.ds(ck * CHUNK, CHUNK)
            ua, _, _, _, s = _gmlp_chunk(zu_ref[rows, :], zv_ref[rows, :], gv_ref[...], wm_ref, b_ref[...], n_heads)
            yh, _ = _rms_stats(ua * s)
            o_ref[rows, :] = (yh * ggo_ref[...]).astype(BF16)

    full = lambda a: pl.BlockSpec(a.shape, lambda i, nd=a.ndim: (0,) * nd)
    return pl.pallas_call(
        body,
        name="gmlp_fwd",
        grid=(t_dim // tr,),
        in_specs=[pl.BlockSpec((tr, dg), lambda i: (i, 1)), pl.BlockSpec((tr, dg), lambda i: (i, 2)),
                  full(gv), full(wm), full(bias), full(ggo)],
        out_specs=pl.BlockSpec((tr, dg), lambda i: (i, 0)),
        out_shape=jax.ShapeDtypeStruct((t_dim, dg), BF16),
        compiler_params=_params(("parallel",)),
    )(z, z, gv, wm, bias, ggo)


def _gmlp_bwd(z, dycat, gv, wm, bias, ggo):
    t_dim = z.shape[0]
    dg = gv.shape[1]
    n_heads = dg // GMLP_HEAD
    tr = _tile(t_dim, 256, CHUNK)

    def body(zu_ref, zv_ref, dy_ref, gv_ref, wm_ref, b_ref, ggo_ref,
             dz_ref, dggo_ref, dgv_ref, dwm_ref, dsum_ref):
        @pl.when(pl.program_id(0) == 0)
        def _():
            dggo_ref[...] = jnp.zeros_like(dggo_ref)
            dgv_ref[...] = jnp.zeros_like(dgv_ref)
            dwm_ref[...] = jnp.zeros_like(dwm_ref)
            dsum_ref[...] = jnp.zeros_like(dsum_ref)

        for ck in range(tr // CHUNK):
            rows = pl.ds(ck * CHUNK, CHUNK)
            zu = zu_ref[rows, :]
            zv = zv_ref[rows, :]
            gvv = gv_ref[...]
            ua, vh, r, vb, s = _gmlp_chunk(zu, zv, gvv, wm_ref, b_ref[...], n_heads)
            dy, dggo = _rms_backward(ua * s, ggo_ref[...], dy_ref[rows, :].astype(F32))
            dggo_ref[...] += dggo
            ds = dy * ua
            dsum_ref[...] += ds
            dsb = ds.astype(BF16)
            parts = []
            for h in range(n_heads):
                cols = slice(h * GMLP_HEAD, (h + 1) * GMLP_HEAD)
                dwm_ref[h] += _dot(dsb[:, cols], vb[:, cols], "nt")
                parts.append(_dot(wm_ref[h], dsb[:, cols], "tn"))
            dv = jnp.concatenate(parts, axis=1)
            dgv_ref[...] += jnp.sum(dv * vh, axis=0, keepdims=True)
            dvh = dv * gvv
            dvg = r * (dvh - jnp.mean(dvh, axis=-1, keepdims=True) - vh * jnp.mean(dvh * vh, axis=-1, keepdims=True))
            dz_ref[rows, pl.ds(2 * dg, dg)] = (dvg * _gelu_grad(zv)).astype(BF16)
            dz_ref[rows, pl.ds(dg, dg)] = (dy * s * _gelu_grad(zu)).astype(BF16)

    full = lambda a: pl.BlockSpec(a.shape, lambda i, nd=a.ndim: (0,) * nd)
    return pl.pallas_call(
        body,
        name="gmlp_bwd",
        grid=(t_dim // tr,),
        in_specs=[pl.BlockSpec((tr, dg), lambda i: (i, 1)), pl.BlockSpec((tr, dg), lambda i: (i, 2)),
                  pl.BlockSpec((tr, dg), lambda i: (i, 1)), full(gv), full(wm), full(bias), full(ggo)],
        out_specs=[pl.BlockSpec((tr, 3 * dg), lambda i: (i, 0)),
                   pl.BlockSpec((1, dg), lambda i: (0, 0)), pl.BlockSpec((1, dg), lambda i: (0, 0)),
                   pl.BlockSpec(wm.shape, lambda i: (0, 0, 0)), pl.BlockSpec((CHUNK, dg), lambda i: (0, 0))],
        out_shape=[jax.ShapeDtypeStruct((t_dim, 3 * dg), BF16),
                   jax.ShapeDtypeStruct((1, dg), F32), jax.ShapeDtypeStruct((1, dg), F32),
                   jax.ShapeDtypeStruct(wm.shape, F32), jax.ShapeDtypeStruct((CHUNK, dg), F32)],
        compiler_params=_params(("arbitrary",)),
    )(z, z, dycat, gv, wm, bias, ggo)


def _ple_head(npl, w_gate, h3, pp, tgt, g_final):
    t_dim, d = h3.shape
    tr = _tile(t_dim, 256, 16)

    def body(n_ref, w_ref, h_ref, pp_ref, t_ref, g_ref, dgq_ref, dpp_ref, dh_ref, dg_ref, loss_ref):
        @pl.when(pl.program_id(0) == 0)
        def _():
            dg_ref[...] = jnp.zeros_like(dg_ref)
            loss_ref[...] = jnp.zeros_like(loss_ref)

        gate = _sigmoid(_dot(n_ref[...], w_ref[...], "nn"))
        ppv = pp_ref[...]
        h4 = h_ref[...] + gate * ppv
        xh, _ = _rms_stats(h4)
        err = xh * g_ref[...] - t_ref[...]
        dh4, dg = _rms_backward(h4, g_ref[...], err * (1.0 / d))
        dh_ref[...] = dh4
        dgq_ref[...] = (dh4 * ppv * gate * (1.0 - gate)).astype(BF16)
        dpp_ref[...] = (dh4 * gate).astype(BF16)
        dg_ref[...] += dg
        loss_ref[...] += jnp.full((1, LANES), 0.5 * jnp.sum(err * err) * (1.0 / d), F32)

    rows = pl.BlockSpec((tr, d), lambda i: (i, 0))
    whole = lambda a: pl.BlockSpec(a.shape, lambda i: (0, 0))
    return pl.pallas_call(
        body,
        name="ple_head",
        grid=(t_dim // tr,),
        in_specs=[rows, whole(w_gate), rows, rows, rows, whole(g_final)],
        out_specs=[rows, rows, rows, pl.BlockSpec((1, d), lambda i: (0, 0)), pl.BlockSpec((1, LANES), lambda i: (0, 0))],
        out_shape=[jax.ShapeDtypeStruct((t_dim, d), BF16), jax.ShapeDtypeStruct((t_dim, d), BF16),
                   jax.ShapeDtypeStruct((t_dim, d), F32), jax.ShapeDtypeStruct((1, d), F32),
                   jax.ShapeDtypeStruct((1, LANES), F32)],
        compiler_params=_params(("arbitrary",)),
    )(npl, w_gate, h3, pp, tgt, g_final)


def _position():
    x, y, c = lax.axis_index("x"), lax.axis_index("y"), lax.axis_index("c")
    chips = [(1 - x, y), (x, 1 - y), (1 - x, 1 - y)]
    return x, y, c, chips


def _region(ref, kind, shard_shape, q, half, part=None):
    rs, cs = shard_shape
    r0, nr = (0, rs) if half is None else (half * (rs // 2), rs // 2)
    if part is not None:
        r0, nr = r0 + part * (rs // 4), rs // 4
    if kind == "row":
        return ref.at[pl.ds(q * rs + r0, nr), :]
    return ref.at[pl.ds(r0, nr), pl.ds(q * cs, cs)]


def _full_shape(kind, shard_shape):
    rs, cs = shard_shape
    return (N_CHIPS * rs, cs) if kind == "row" else (rs, N_CHIPS * cs)


def _remote(src, dst, send_sems, recv_sems, k, to):
    return pltpu.make_async_remote_copy(src_ref=src, dst_ref=dst, send_sem=send_sems.at[k], recv_sem=recv_sems.at[k],
                                        device_id=to, device_id_type=MESH)


def _same(arrays):
    return [jax.ShapeDtypeStruct(a.shape, a.dtype) for a in arrays]


def _gather_near_carry(gathered, kinds, shapes):
    nw = len(gathered)

    def copies(ops, full, send_sems, recv_sems):
        x, y, c, _ = _position()
        out = []
        for w in range(nw):
            mine = _region(full[w], kinds[w], shapes[w], 2 * x + y, c)
            out.append(_remote(mine, mine, send_sems, recv_sems, 2 * w, (1 - x, y, c)))
            out.append(_remote(mine, mine, send_sems, recv_sems, 2 * w + 1, (x, 1 - y, c)))
        return out

    return _Carry(gathered, _same(gathered), {i: i for i in range(nw)}, 2 * nw, copies)


def _gather_far_carry(gathered, kinds, shapes):
    nw = len(gathered)

    def copies(ops, full, send_sems, recv_sems):
        x, y, c, _ = _position()
        out = []
        for w in range(nw):
            from_x = _region(full[w], kinds[w], shapes[w], 2 * (1 - x) + y, c, part=1)
            from_y = _region(full[w], kinds[w], shapes[w], 2 * x + (1 - y), c, part=0)
            out.append(_remote(from_x, from_x, send_sems, recv_sems, 2 * w, (x, 1 - y, c)))
            out.append(_remote(from_y, from_y, send_sems, recv_sems, 2 * w + 1, (1 - x, y, c)))
        return out

    return _Carry(gathered, _same(gathered), {i: i for i in range(nw)}, 2 * nw, copies)


def _gather_d2d_carry(gathered, kinds, shapes):
    nw = len(gathered)

    def copies(ops, full, send_sems, recv_sems):
        x, y, c, chips = _position()
        out = []
        for w in range(nw):
            for j, (cx, cy) in enumerate(chips):
                landed = _region(full[w], kinds[w], shapes[w], 2 * cx + cy, c)
                out.append(_remote(landed, landed, send_sems, recv_sems, 3 * w + j, (x, y, 1 - c)))
        return out

    return _Carry(gathered, _same(gathered), {i: i for i in range(nw)}, 3 * nw, copies)


def _pairs_carry(grads, kinds, shapes):
    nw = len(grads)

    def copies(g, got, send_sems, recv_sems):
        x, y, c, _ = _position()
        out = []
        for w in range(nw):
            for q in range(N_CHIPS):
                out.append(_remote(_region(g[w], kinds[w], shapes[w], q, 1 - c), got[w].at[q], send_sems, recv_sems,
                                   N_CHIPS * w + q, (x, y, 1 - c)))
        return out

    outs = [jax.ShapeDtypeStruct((N_CHIPS, s[0] // 2, s[1]), BF16) for s in shapes]
    return _Carry(grads, outs, {}, N_CHIPS * nw, copies)


def _pair_sum(name, grad, got, kind, shard_shape, c_arr):
    rs, cs = shard_shape
    hr = rs // 2
    tr = _tile(hr, 512, 16)
    nr = hr // tr

    def body(c_ref, g_ref, s_ref, o_ref):
        o_ref[...] = (g_ref[...].astype(F32) + s_ref[...].astype(F32)).astype(BF16)

    if kind == "row":
        g_spec = pl.BlockSpec((tr, cs), lambda q, i, c_ref: (q * (rs // tr) + c_ref[0] * nr + i, 0))
    else:
        g_spec = pl.BlockSpec((tr, cs), lambda q, i, c_ref: (c_ref[0] * nr + i, q))
    blk = pl.BlockSpec((None, tr, cs), lambda q, i, c_ref: (q, i, 0))
    return pl.pallas_call(
        body,
        name=name,
        grid_spec=pltpu.PrefetchScalarGridSpec(num_scalar_prefetch=1, grid=(N_CHIPS, nr), in_specs=[g_spec, blk],
                                               out_specs=blk),
        out_shape=jax.ShapeDtypeStruct((N_CHIPS, hr, cs), BF16),
        compiler_params=_params(("parallel", "parallel")),
    )(c_arr, grad, got)


def _scatter_carry(sums, shapes):
    nw = len(sums)

    def copies(ps, got, send_sems, recv_sems):
        x, y, c, chips = _position()
        out = []
        for w in range(nw):
            for j, (cx, cy) in enumerate(chips):
                out.append(_remote(ps[w].at[2 * cx + cy], got[w].at[j], send_sems, recv_sems, 3 * w + j, (cx, cy, c)))
        return out

    outs = [jax.ShapeDtypeStruct((3, s[0] // 2, s[1]), BF16) for s in shapes]
    return _Carry(sums, outs, {}, 3 * nw, copies)


def _owner_sum(name, sums, got, shard_shape, qc_arr):
    rs, cs = shard_shape
    hr = rs // 2
    tr = _tile(hr, 512, 16)
    nr = hr // tr

    def body(qc_ref, mine_ref, got_ref, o_ref):
        acc = mine_ref[...].astype(F32)
        for j in range(3):
            acc = acc + got_ref[j].astype(F32)
        o_ref[...] = acc

    return pl.pallas_call(
        body,
        name=name,
        grid_spec=pltpu.PrefetchScalarGridSpec(
            num_scalar_prefetch=1, grid=(nr,),
            in_specs=[pl.BlockSpec((None, tr, cs), lambda i, qc_ref: (qc_ref[0], i, 0)),
                      pl.BlockSpec((3, tr, cs), lambda i, qc_ref: (0, i, 0))],
            out_specs=pl.BlockSpec((tr, cs), lambda i, qc_ref: (qc_ref[1] * nr + i, 0))),
        out_shape=jax.ShapeDtypeStruct((rs, cs), F32),
        compiler_params=_params(("parallel",)),
    )(qc_arr, sums, got)


def _share_carry(grads, shapes):
    nw = len(grads)

    def copies(ops, out, send_sems, recv_sems):
        x, y, c, _ = _position()
        res = []
        for w in range(nw):
            hr = shapes[w][0] // 2
            mine = out[w].at[pl.ds(c * hr, hr), :]
            res.append(_remote(mine, mine, send_sems, recv_sems, w, (x, y, 1 - c)))
        return res

    return _Carry(grads, _same(grads), {i: i for i in range(nw)}, nw, copies)


def _place_block(packed, me):
    return lax.dynamic_update_slice(jnp.zeros((N_DEV,) + packed.shape, F32), packed[None], (me, 0, 0))


def _exchange_carry(blocks):
    def copies(ops, res, send_sems, recv_sems):
        x, y, c, _ = _position()
        mine = res[0].at[4 * x + 2 * y + c]
        out = []
        for k in range(1, N_DEV):
            to = ((1 - x) if k & 4 else x, (1 - y) if k & 2 else y, (1 - c) if k & 1 else c)
            out.append(_remote(mine, mine, send_sems, recv_sems, k - 1, to))
        return out

    return _Carry([blocks], _same([blocks]), {0: 0}, N_DEV - 1, copies)


def _sum_blocks(name, blocks):
    n, rows, lanes = blocks.shape
    tr = _tile(rows, 4096, 8)

    def body(b_ref, o_ref):
        acc = b_ref[0]
        for k in range(1, n):
            acc = acc + b_ref[k]
        o_ref[...] = acc

    return pl.pallas_call(
        body,
        name=name,
        grid=(rows // tr,),
        in_specs=[pl.BlockSpec((n, tr, lanes), lambda i: (0, i, 0))],
        out_specs=pl.BlockSpec((tr, lanes), lambda i: (i, 0)),
        out_shape=jax.ShapeDtypeStruct((rows, lanes), F32),
        compiler_params=_params(("parallel",)),
    )(blocks)


def _adamw(name, w, g, m, v):
    def fn(rows, pars):
        wv, gv, mv, vv = rows
        m_new = ADAM_B1 * mv + (1.0 - ADAM_B1) * gv
        v_new = ADAM_B2 * vv + (1.0 - ADAM_B2) * (gv * gv)
        m_hat = m_new / (1.0 - ADAM_B1 ** ADAM_STEP)
        v_hat = v_new / (1.0 - ADAM_B2 ** ADAM_STEP)
        delta = -ADAM_LR * (m_hat / (jnp.sqrt(v_hat) + ADAM_EPS) + ADAM_WD * wv)
        return [delta, m_new, v_new, gv], []

    c = w.shape[1]
    return _rowwise(name, fn, [w, g, m, v], [], [(c, F32)] * 4, tr=256)


def _pack(arrays):
    rows = []
    for a in arrays:
        flat = a.reshape(-1).astype(F32)
        pad = (-flat.shape[0]) % LANES
        rows.append(jnp.pad(flat, (0, pad)).reshape(-1, LANES))
    stacked = jnp.concatenate(rows, axis=0)
    pad_rows = (-stacked.shape[0]) % 8
    return jnp.pad(stacked, ((0, pad_rows), (0, 0)))


def _unpack(packed, shapes):
    out, r = [], 0
    for s in shapes:
        n = math.prod(s)
        nr = -(-n // LANES)
        out.append(packed[r:r + nr].reshape(-1)[:n].reshape(s))
        r += nr
    return out


BIG = ["w1_gate", "w1_up", "w1_down", "w_in", "ssm_w_glu", "w_out", "w2_gate", "w2_up", "w2_down", "w_ple_gate",
       "w_ple_proj"]
KIND = {"w1_gate": "col", "w1_up": "col", "w1_down": "row", "w_in": "col", "ssm_w_glu": "row", "w_out": "row",
        "w2_gate": "col", "w2_up": "col", "w2_down": "row", "w_ple_gate": "row", "w_ple_proj": "col"}
SMALL = ["norm_ffn1", "norm_mix", "ssm_log_dt", "ssm_a_re", "ssm_a_im", "ssm_b_re", "ssm_b_im", "ssm_c_re", "ssm_c_im",
         "ssm_d", "gmlp_norm_v", "gmlp_w_s", "gmlp_b_s", "norm_ssm_out", "norm_gmlp_out", "norm_ffn2", "norm_ple",
         "norm_final"]
WEIGHTS = ["norm_ffn1", "w1_gate", "w1_up", "w1_down", "norm_mix", "w_in", "ssm_log_dt", "ssm_a_re", "ssm_a_im",
           "ssm_b_re", "ssm_b_im", "ssm_c_re", "ssm_c_im", "ssm_d", "ssm_w_glu", "gmlp_norm_v", "gmlp_w_s", "gmlp_b_s",
           "norm_ssm_out", "norm_gmlp_out", "w_out", "norm_ffn2", "w2_gate", "w2_up", "w2_down", "norm_ple",
           "w_ple_gate", "w_ple_proj", "norm_final"]


class _Trip:
    def __init__(self, names, arrays, carry):
        self.names, self.arrays, self.carry = names, arrays, carry


class _Reducer:
    def __init__(self, shard_shape, c_arr, qc_arr):
        self.shard_shape, self.c_arr, self.qc_arr = shard_shape, c_arr, qc_arr
        self.halves = {}

    def swap(self, names, grads):
        kinds = [KIND[n] for n in names]
        shapes = [self.shard_shape[n] for n in names]
        return _Trip(names, grads, _pairs_carry(grads, kinds, shapes))

    def send(self, trip, swapped):
        shapes = [self.shard_shape[n] for n in trip.names]
        sums = [_pair_sum("pair_sum_" + n, g, s, KIND[n], sh, self.c_arr)
                for n, g, s, sh in zip(trip.names, trip.arrays, swapped, shapes)]
        return _Trip(trip.names, sums, _scatter_carry(sums, shapes))

    def end(self, trip, got):
        for n, ps, g in zip(trip.names, trip.arrays, got):
            self.halves[n] = _owner_sum("owner_sum_" + n, ps, g, self.shard_shape[n], self.qc_arr)


def _ride(*carries):
    present = [c for c in carries if c is not None]
    joined = functools.reduce(_join, present) if present else None

    def split(results):
        out, at = [], 0
        for c in carries:
            n = len(c.out_shapes) if c is not None else 0
            out.append(list(results[at:at + n]))
            at += n
        return out

    return joined, split


def _step(x, p, tgt, w, m, v):
    d_model = x.shape[1]
    d_ssm = w["ssm_d"].shape[1]
    n_groups = d_ssm // SSM_GROUP
    row = lambda a: a.reshape(1, -1)

    xi, yi, ci = lax.axis_index("x"), lax.axis_index("y"), lax.axis_index("c")
    c_arr = jnp.reshape(ci, (1,)).astype(jnp.int32)
    q_arr = jnp.reshape(2 * xi + yi, (1,)).astype(jnp.int32)
    qc_arr = jnp.stack([2 * xi + yi, ci]).astype(jnp.int32)
    shard_shape = {n: w[n].shape for n in BIG}
    full = {n: _cast_into_gathered("cast_" + n, w[n], KIND[n], q_arr) for n in BIG}

    def gather(stage, names):
        return stage([full[n] for n in names], [KIND[n] for n in names], [shard_shape[n] for n in names])

    def gathered(names, arrays):
        full.update(zip(names, arrays))

    groups = [["w1_gate"], ["w1_up"], ["w1_down"], ["w_in"], ["w2_gate"], ["ssm_w_glu", "w_out"], ["w2_up"],
              ["w2_down", "w_ple_gate", "w_ple_proj"]]
    near, far, d2d = _gather_near_carry, _gather_far_carry, _gather_d2d_carry

    def stages(*work):
        carries = [gather(stage, groups[g]) for stage, g in work]
        return functools.reduce(_join, carries), [n for _, g in work for n in groups[g]]

    def alone(name, *work):
        carry, names = stages(*work)
        gathered(names, _comm_call(name, carry))

    alone("gather_a", (near, 0))
    alone("gather_b", (far, 0), (near, 1))
    carry, names = stages((d2d, 0), (far, 1))
    n1, landed = _rms_fwd("ffn1_norm", x, w["norm_ffn1"], carry)
    gathered(names, landed)
    carry, names = stages((d2d, 1), (near, 2), (near, 3))
    gate1, landed = _mm_nn("ffn1_gate", n1, full["w1_gate"], BF16, tm=1024, tn=512, tk=2048, carry=carry)
    gathered(names, landed)
    carry, names = stages((far, 2), (far, 3), (near, 4))
    (a1, da_dgate1, da_dup1), landed = _ffn_up("ffn1", n1, full["w1_up"], gate1, carry)
    gathered(names, landed)
    alone("gather_d", (d2d, 2))
    carry, names = stages((d2d, 3), (far, 4), (near, 5))
    h1, landed = _ffn_down("ffn1", a1, full["w1_down"], x, carry)
    gathered(names, landed)
    ffn1 = (n1, a1, da_dgate1, da_dup1)
    nm = _rms_fwd("mix_norm", h1, w["norm_mix"])
    carry, names = stages((d2d, 4), (far, 5), (near, 6))
    z, landed = _mm_nn("in_proj", nm, full["w_in"], F32, tm=1024, tn=512, tk=2048, carry=carry)
    gathered(names, landed)

    disc, disc_vjp = jax.vjp(_ssm_discretize, w["ssm_log_dt"][0], w["ssm_a_re"], w["ssm_a_im"], w["ssm_b_re"],
                             w["ssm_b_im"])
    abar_r, abar_i, bbar_r, bbar_i = disc
    nb = n_groups // GROUPS_PER_BLOCK
    wb = jnp.concatenate([_blockdiag_in(bbar_r), _blockdiag_in(bbar_i)], axis=-1).astype(BF16)
    wc = jnp.concatenate([_blockdiag_out(w["ssm_c_re"]), -_blockdiag_out(w["ssm_c_im"])], axis=1).astype(BF16)
    abar = jnp.concatenate([abar_r.reshape(nb, -1), abar_i.reshape(nb, -1)], axis=-1)
    abar_conj = jnp.concatenate([abar_r.reshape(nb, -1), -abar_i.reshape(nb, -1)], axis=-1)
    carry, names = stages((d2d, 5), (far, 6), (near, 7))
    (states, y_pre, yg), landed = _ssm_scan_fwd(z, wb, wc, abar, w["ssm_d"], carry)
    gathered(names, landed)
    q = _mm_nn("glu_proj", yg, full["ssm_w_glu"], F32, tm=1024, tn=1024, tk=1024)

    def glu_norm(rows, pars):
        yv = _gelu(rows[0]) * _sigmoid(rows[1])
        yh, _ = _rms_stats(yv)
        return [yh * pars[0]], []

    yn_ssm = _rowwise("ssm_glu_norm", glu_norm, [y_pre, q], [w["norm_ssm_out"]], [(d_ssm, BF16)])[0]

    tril = jnp.tril(jnp.ones((CHUNK, CHUNK), dtype=bool))
    wm = jnp.where(tril[None], w["gmlp_w_s"], 0.0).astype(BF16)
    bias = jnp.repeat(w["gmlp_b_s"].T, GMLP_HEAD, axis=1)
    yn_gmlp = _gmlp_fwd(z, w["gmlp_norm_v"], wm, bias, w["norm_gmlp_out"])
    ycat = jnp.concatenate([yn_ssm, yn_gmlp], axis=1)
    carry, names = stages((d2d, 6), (far, 7))
    h2, landed = _mm_nn("out_proj", ycat, full["w_out"], F32, res=h1, alpha=1.0, tm=512, tn=1024, tk=2048,
                        carry=carry)
    gathered(names, landed)

    n2 = _rms_fwd("ffn2_norm", h2, w["norm_ffn2"])
    carry, names = stages((d2d, 7))
    (a2, da_dgate2, da_dup2), landed = _ffn_gateup("ffn2", n2, full["w2_gate"], full["w2_up"], carry)
    gathered(names, landed)
    h3 = _ffn_down("ffn2", a2, full["w2_down"], h2)
    ffn2 = (n2, a2, da_dgate2, da_dup2)
    npl = _rms_fwd("ple_norm", h3, w["norm_ple"])
    pp = _mm_nn("ple_proj", p, full["w_ple_proj"], F32, tm=1024, tn=1024, tk=2048)
    dgq, dpp, dh4, g_norm_final, loss_part = _ple_head(npl, full["w_ple_gate"], h3, pp, tgt, row(w["norm_final"]))
    reducer = _Reducer(shard_shape, c_arr, qc_arr)
    (g_w_ple_proj,) = _mm_tn("ple_dwproj", p, [dpp], BF16, tm=256, tn=1024, tk=4096)
    (g_w_ple_gate,) = _mm_tn("ple_dwgate", npl, [dgq], BF16, tm=512, tn=1024, tk=4096)
    ple = reducer.swap(["w_ple_gate", "w_ple_proj"], [g_w_ple_gate, g_w_ple_proj])
    dnpl, swapped = _mm_nt_sum("ple_dnorm_in", [dgq], [full["w_ple_gate"]], BF16, tm=512, tn=1024, tk=2048,
                               carry=ple.carry)
    ple = reducer.send(ple, swapped)
    dh3, dh3_b, g_norm_ple = _rms_bwd("ple_dnorm", h3, w["norm_ple"], [dnpl], dh4, 0.5)

    dh2, dh2_b, g_norm_ffn2, got, _, up2_w = _ffn_bwd(
        "ffn2", ["w2_gate", "w2_up", "w2_down"], h2, w["norm_ffn2"], full["w2_gate"], full["w2_up"], full["w2_down"],
        ffn2, dh3, dh3_b, 1.0, reducer, riding=ple.carry, last_hop_later=True)
    reducer.end(ple, got)

    dycat = _mm_nt_sum("out_dproj", [dh2_b], [full["w_out"]], BF16, tm=512, tn=1024, tk=2048)
    (g_w_out,) = _mm_tn("out_dw", ycat, [dh2_b], BF16, tm=512, tn=1024, tk=4096)

    dz, g_norm_gmlp_out, g_gmlp_norm_v, g_wm, g_s = _gmlp_bwd(z, dycat, w["gmlp_norm_v"], wm, bias,
                                                                  w["norm_gmlp_out"])
    g_gmlp_w_s = jnp.where(tril[None], g_wm, 0.0)
    g_gmlp_b_s = g_s.reshape(CHUNK, -1, GMLP_HEAD).sum(axis=-1).T

    def glu_bwd(rows, pars):
        dyn, ypre, qv = rows
        ygv = _gelu(ypre)
        sg = _sigmoid(qv)
        dy, dg = _rms_backward(ygv * sg, pars[0], dyn.astype(F32))
        return [dy * ygv * sg * (1.0 - sg), dy * sg], [dg]

    dq, dyg_part, g_norm_ssm_out = _rowwise("ssm_dglu", glu_bwd, [(dycat, d_ssm, 0), y_pre, q], [w["norm_ssm_out"]],
                                            [(d_ssm, BF16), (d_ssm, F32)], [(1, d_ssm)])
    (g_ssm_w_glu,) = _mm_tn("glu_dw", yg, [dq], BF16, tm=512, tn=1024, tk=4096)

    def gelu_bwd(accs, extras):
        return [(accs[0] + extras[0]) * _gelu_grad(extras[1])]

    (dy_pre,) = _matmul("glu_dproj", "nt", [dq], [full["ssm_w_glu"]], [(0, 0, 0)], [F32], gelu_bwd,
                        extras=(dyg_part, y_pre), tm=1024, tn=1024, tk=1024)
    mixers = reducer.swap(["w_out", "ssm_w_glu"], [g_w_out, g_ssm_w_glu])
    me = 4 * xi + 2 * yi + ci
    small = {"gmlp_norm_v": g_gmlp_norm_v, "gmlp_w_s": g_gmlp_w_s, "gmlp_b_s": g_gmlp_b_s,
             "norm_gmlp_out": g_norm_gmlp_out, "norm_ssm_out": g_norm_ssm_out, "norm_ffn2": g_norm_ffn2,
             "norm_ple": g_norm_ple, "norm_final": g_norm_final}
    before_scan = [n for n in SMALL if n in small]
    scan_blocks = _place_block(_pack([small[n] for n in before_scan] + [loss_part[:, :1]]), me)
    carry, split = _ride(up2_w.carry, mixers.carry, _exchange_carry(scan_blocks))
    (dz, g_wb, g_wc, g_abar, g_ssm_d), results = _ssm_scan_bwd(z, dy_pre, states, wb, wc, abar_conj, w["ssm_d"],
                                                              dz, carry)
    got, swapped, (scan_blocks,) = split(results)
    reducer.end(up2_w, got)
    mixers = reducer.send(mixers, swapped)
    g_abar = jnp.transpose(g_abar, (1, 0, 2)).reshape(nb, -1)
    sw = g_abar.shape[-1] // 2
    g_bbar_r = _blockdiag_in_grad(g_wb[..., :sw], SSM_STATE, SSM_GROUP)
    g_bbar_i = _blockdiag_in_grad(g_wb[..., sw:], SSM_STATE, SSM_GROUP)
    g_ssm_c_re = _blockdiag_out_grad(g_wc[:, :sw, :], SSM_GROUP, SSM_STATE)
    g_ssm_c_im = -_blockdiag_out_grad(g_wc[:, sw:, :], SSM_GROUP, SSM_STATE)
    g_abar_r = g_abar[..., :sw].reshape(n_groups, SSM_STATE)
    g_abar_i = g_abar[..., sw:].reshape(n_groups, SSM_STATE)
    g_ssm_log_dt, g_ssm_a_re, g_ssm_a_im, g_ssm_b_re, g_ssm_b_im = disc_vjp((g_abar_r, g_abar_i, g_bbar_r, g_bbar_i))

    (g_w_in,), got = _mm_tn("in_dw", nm, [dz], BF16, tm=512, tn=1536, tk=4096, carry=mixers.carry)
    reducer.end(mixers, got)
    in_w = reducer.swap(["w_in"], [g_w_in])
    dnm, swapped = _mm_nt_sum("in_dproj", [dz], [full["w_in"]], BF16, tm=1024, tn=1024, tk=3072, carry=in_w.carry)
    in_w = reducer.send(in_w, swapped)
    dh1, dh1_b, g_norm_mix = _rms_bwd("mix_dnorm", h1, w["norm_mix"], [dnm], dh2, 0.5)
    small = {"norm_mix": g_norm_mix, "ssm_log_dt": g_ssm_log_dt, "ssm_a_re": g_ssm_a_re,
             "ssm_a_im": g_ssm_a_im, "ssm_b_re": g_ssm_b_re, "ssm_b_im": g_ssm_b_im, "ssm_c_re": g_ssm_c_re,
             "ssm_c_im": g_ssm_c_im, "ssm_d": g_ssm_d}
    after_scan = [n for n in SMALL if n in small]
    ffn_blocks = _place_block(_pack([small[n] for n in after_scan]), me)
    done = [n for n in BIG if n in reducer.halves]
    carry, split = _ride(in_w.carry, _share_carry([reducer.halves[n] for n in done], [shard_shape[n] for n in done]))
    dx, _, g_norm_ffn1, results, (ffn_blocks,), _ = _ffn_bwd(
        "ffn1", ["w1_gate", "w1_up", "w1_down"], x, w["norm_ffn1"], full["w1_gate"], full["w1_up"], full["w1_down"],
        ffn1, dh1, dh1_b, 1.0, reducer, riding=carry, riding_dwd=_exchange_carry(ffn_blocks))
    got, shared = split(results)
    grad = dict(zip(done, shared))
    reducer.end(in_w, got)
    scan_grads = _unpack(_sum_blocks("sum_before_scan", scan_blocks), [w[n].shape for n in before_scan] + [(1,)])
    loss = scan_grads[-1].reshape(())
    early = before_scan + after_scan
    early_grads = scan_grads[:-1] + _unpack(_sum_blocks("sum_after_scan", ffn_blocks), [w[n].shape for n in after_scan])

    rest = [n for n in BIG if n not in done]
    late_blocks = _place_block(_pack([g_norm_ffn1]), me)
    last = _join(_share_carry([reducer.halves[n] for n in rest], [shard_shape[n] for n in rest]),
                 _exchange_carry(late_blocks))
    *shared, late_blocks = _comm_call("share_halves", last)
    grad.update(zip(rest, shared))
    grad.update(zip(early, early_grads))
    grad["norm_ffn1"] = _unpack(_sum_blocks("sum_first_norm", late_blocks), [w["norm_ffn1"].shape])[0]

    small_shapes = [w[n].shape for n in SMALL]
    delta, new_m, new_v = {}, {}, {}
    for n in BIG:
        delta[n], new_m[n], new_v[n], grad[n] = _adamw("adamw_" + n, w[n], grad[n], m[n], v[n])
    d_p, m_p, v_p, _ = _adamw("adamw_small", _pack([w[n] for n in SMALL]), _pack([grad[n] for n in SMALL]),
                              _pack([m[n] for n in SMALL]), _pack([v[n] for n in SMALL]))
    for name_list, packed in ((delta, d_p), (new_m, m_p), (new_v, v_p)):
        for n, a in zip(SMALL, _unpack(packed, small_shapes)):
            name_list[n] = a
    return loss, dx, grad, delta, new_m, new_v


def kernel(x, p, norm_ffn1, w1_gate, w1_up, w1_down, norm_mix, w_in, ssm_log_dt, ssm_a_re, ssm_a_im, ssm_b_re, ssm_b_im, ssm_c_re, ssm_c_im, ssm_d, ssm_w_glu, gmlp_norm_v, gmlp_w_s, gmlp_b_s, norm_ssm_out, norm_gmlp_out, w_out, norm_ffn2, w2_gate, w2_up, w2_down, norm_ple, w_ple_gate, w_ple_proj, norm_final, loss_target, m_norm_ffn1, m_w1_gate, m_w1_up, m_w1_down, m_norm_mix, m_w_in, m_ssm_log_dt, m_ssm_a_re, m_ssm_a_im, m_ssm_b_re, m_ssm_b_im, m_ssm_c_re, m_ssm_c_im, m_ssm_d, m_ssm_w_glu, m_gmlp_norm_v, m_gmlp_w_s, m_gmlp_b_s, m_norm_ssm_out, m_norm_gmlp_out, m_w_out, m_norm_ffn2, m_w2_gate, m_w2_up, m_w2_down, m_norm_ple, m_w_ple_gate, m_w_ple_proj, m_norm_final, v_norm_ffn1, v_w1_gate, v_w1_up, v_w1_down, v_norm_mix, v_w_in, v_ssm_log_dt, v_ssm_a_re, v_ssm_a_im, v_ssm_b_re, v_ssm_b_im, v_ssm_c_re, v_ssm_c_im, v_ssm_d, v_ssm_w_glu, v_gmlp_norm_v, v_gmlp_w_s, v_gmlp_b_s, v_norm_ssm_out, v_norm_gmlp_out, v_w_out, v_norm_ffn2, v_w2_gate, v_w2_up, v_w2_down, v_norm_ple, v_w_ple_gate, v_w_ple_proj, v_norm_final):
    given = dict(locals())
    shapes = {n: given[n].shape for n in WEIGHTS}

    def block(name):
        a = given[name]
        if a.ndim == 1:
            return a.reshape(1, -1)
        return a[0] if a.ndim >= 3 else a

    w = {n: block(n) for n in WEIGHTS}
    m = {n: block("m_" + n) for n in WEIGHTS}
    v = {n: block("v_" + n) for n in WEIGHTS}
    loss, dx, grad, delta, new_m, new_v = _step(x[0], p[0, 0], loss_target[0], w, m, v)
    outs = [loss, dx[None]]
    for tree in (grad, delta, new_m, new_v):
        outs += [tree[n].reshape(shapes[n]) for n in WEIGHTS]
    return tuple(outs)
```

```python
import functools
import math

import jax
import jax.numpy as jnp
from jax import lax
from jax.experimental import pallas as pl
from jax.experimental.pallas import tpu as pltpu

F32 = jnp.float32
BF16 = jnp.bfloat16
EPS = 1e-6
SSM_GROUP = 16
SSM_STATE = 64
GROUPS_PER_BLOCK = 8
GMLP_HEAD = 128
CHUNK = 128
ADAM_LR = 0.001
ADAM_B1 = 0.9
ADAM_B2 = 0.999
ADAM_EPS = 1e-08
ADAM_WD = 0.01
ADAM_STEP = 10
N_CHIPS = 4
N_DEV = 8
LANES = 128
VMEM_LIMIT_BYTES = 56 * 1024 * 1024
MESH = pl.DeviceIdType.MESH
GELU_C = math.sqrt(2.0 / math.pi)
GELU_A = 0.044715

_DOT_DIMS = {
    "nn": (((1,), (0,)), ((), ())),
    "nt": (((1,), (1,)), ((), ())),
    "tn": (((0,), (0,)), ((), ())),
}


def _tile(dim, pref, align):
    if dim <= pref:
        return dim
    t = (pref // align) * align
    while t >= align:
        if dim % t == 0:
            return t
        t -= align
    return dim


def _params(semantics):
    return pltpu.CompilerParams(dimension_semantics=semantics, vmem_limit_bytes=VMEM_LIMIT_BYTES)


def _gelu(x):
    return 0.5 * x * (1.0 + jnp.tanh(GELU_C * (x + GELU_A * x * x * x)))


def _gelu_grad(x):
    t = jnp.tanh(GELU_C * (x + GELU_A * x * x * x))
    return 0.5 * (1.0 + t) + 0.5 * x * (1.0 - t * t) * GELU_C * (1.0 + 3.0 * GELU_A * x * x)


def _sigmoid(x):
    return 1.0 / (1.0 + jnp.exp(-x))


def _dot(a, b, mode):
    return lax.dot_general(a.astype(BF16), b.astype(BF16), _DOT_DIMS[mode], preferred_element_type=F32)


class _Carry:
    def __init__(self, arrays, out_shapes, aliases, n_copies, copies):
        self.arrays = list(arrays)
        self.out_shapes = list(out_shapes)
        self.aliases = dict(aliases)
        self.n_copies = n_copies
        self.copies = copies

    def scratch(self):
        return [pltpu.SemaphoreType.DMA((self.n_copies,)), pltpu.SemaphoreType.DMA((self.n_copies,))]

    def split(self, refs):
        n_in, n_out = len(self.arrays), len(self.out_shapes)
        return refs[:n_in], refs[n_in:n_in + n_out], refs[n_in + n_out], refs[n_in + n_out + 1]

    def start(self, refs):
        for cp in self.copies(*self.split(refs)):
            cp.start()

    def wait(self, refs):
        for cp in self.copies(*self.split(refs)):
            cp.wait()


class _SemRange:
    def __init__(self, sems, offset):
        self.sems, self.offset = sems, offset

    @property
    def at(self):
        return self

    def __getitem__(self, k):
        return self.sems.at[self.offset + k]


def _join(first, second):
    n_in, n_out = len(first.arrays), len(first.out_shapes)
    aliases = dict(first.aliases)
    aliases.update({n_in + i: n_out + o for i, o in second.aliases.items()})

    def copies(ops, res, send_sems, recv_sems):
        return (first.copies(ops[:n_in], res[:n_out], send_sems, recv_sems)
                + second.copies(ops[n_in:], res[n_out:], _SemRange(send_sems, first.n_copies),
                                _SemRange(recv_sems, first.n_copies)))

    return _Carry(first.arrays + second.arrays, first.out_shapes + second.out_shapes, aliases,
                  first.n_copies + second.n_copies, copies)


_ANY = pl.BlockSpec(memory_space=pl.ANY)


def _comm_call(name, carry):
    def body(*refs):
        carry.start(refs)
        carry.wait(refs)

    n_in = len(carry.arrays)
    return pl.pallas_call(
        body,
        name=name,
        in_specs=[_ANY] * n_in,
        out_specs=[_ANY] * len(carry.out_shapes),
        out_shape=carry.out_shapes,
        input_output_aliases=carry.aliases,
        scratch_shapes=carry.scratch(),
    )(*carry.arrays)


def _carried_call(body, carry, *, name, grid, in_specs, out_specs, out_shape, scratch_shapes, semantics, args,
                  aliases=None):
    aliases = dict(aliases or {})
    if carry is None:
        res = pl.pallas_call(body, name=name, grid=grid, in_specs=in_specs, out_specs=out_specs, out_shape=out_shape,
                             scratch_shapes=scratch_shapes, input_output_aliases=aliases,
                             compiler_params=_params(semantics))(*args)
        return res, []
    n_in, n_out, n_scr = len(in_specs), len(out_specs), len(scratch_shapes)
    nci, nco = len(carry.arrays), len(carry.out_shapes)

    def wrapped(*refs):
        ins = refs[:n_in]
        outs = refs[n_in + nci:n_in + nci + n_out]
        scr = refs[n_in + nci + n_out + nco:n_in + nci + n_out + nco + n_scr]
        c_refs = (refs[n_in:n_in + nci] + refs[n_in + nci + n_out:n_in + nci + n_out + nco]
                  + refs[n_in + nci + n_out + nco + n_scr:])
        first = functools.reduce(jnp.logical_and, [pl.program_id(d) == 0 for d in range(len(grid))])
        last = functools.reduce(jnp.logical_and, [pl.program_id(d) == grid[d] - 1 for d in range(len(grid))])

        @pl.when(first)
        def _():
            carry.start(c_refs)

        body(*ins, *outs, *scr)

        @pl.when(last)
        def _():
            carry.wait(c_refs)

    res = pl.pallas_call(
        wrapped,
        name=name,
        grid=grid,
        in_specs=list(in_specs) + [_ANY] * nci,
        out_specs=list(out_specs) + [_ANY] * nco,
        out_shape=list(out_shape) + carry.out_shapes,
        input_output_aliases={**aliases, **{n_in + i: n_out + o for i, o in carry.aliases.items()}},
        scratch_shapes=list(scratch_shapes) + carry.scratch(),
        compiler_params=_params(("arbitrary",) * len(grid)),
    )(*args, *carry.arrays)
    return res[:n_out], res[n_out:]


def _matmul(name, mode, a_list, b_list, products, out_dtypes, epilogue, extras=(), tm=512, tn=512, tk=2048,
            carry=None, n_part=(0, 1)):
    a0, b0 = a_list[0], b_list[0]
    if mode == "tn":
        k_dim, m_dim = a0.shape
    else:
        m_dim, k_dim = a0.shape
    n_dim = (b0.shape[0] if mode == "nt" else b0.shape[1]) // n_part[1]
    tm = _tile(m_dim, tm, LANES)
    tn = _tile(n_dim, tn, LANES)
    tk = _tile(k_dim, tk, LANES)
    nk = k_dim // tk
    j0 = n_part[0] * (n_dim // tn)
    chunk = 2 * LANES if (nk == 1 and epilogue is not _identity and tn % (2 * LANES) == 0) else tn
    n_acc = 1 + max(p[2] for p in products)
    na, nb, ne, no = len(a_list), len(b_list), len(extras), len(out_dtypes)

    if mode == "tn":
        a_spec = pl.BlockSpec((tk, tm), lambda i, j, k: (k, i))
    else:
        a_spec = pl.BlockSpec((tm, tk), lambda i, j, k: (i, k))
    if mode == "nt":
        b_spec = pl.BlockSpec((tn, tk), lambda i, j, k: (j0 + j, k))
    else:
        b_spec = pl.BlockSpec((tk, tn), lambda i, j, k: (k, j0 + j))
    t_spec = pl.BlockSpec((tm, tn), lambda i, j, k: (i, j))

    def body(*refs):
        a_refs = refs[:na]
        b_refs = refs[na:na + nb]
        e_refs = refs[na + nb:na + nb + ne]
        o_refs = refs[na + nb + ne:na + nb + ne + no]
        acc_refs = refs[na + nb + ne + no:]

        def partial_sums(cols):
            sums = [None] * n_acc
            for ai, bi, ci in products:
                b = b_refs[bi][cols, :] if mode == "nt" else b_refs[bi][:, cols]
                d = _dot(a_refs[ai][...], b, mode)
                sums[ci] = d if sums[ci] is None else sums[ci] + d
            return sums

        def finish(accs, cols):
            outs = epilogue(accs, [e[:, cols] for e in e_refs])
            for o_ref, o in zip(o_refs, outs):
                o_ref[:, cols] = o.astype(o_ref.dtype)

        if nk == 1:
            for c0 in range(0, tn, chunk):
                finish(partial_sums(slice(c0, c0 + chunk)), slice(c0, c0 + chunk))
        else:
            sums = partial_sums(slice(None))
            finish = functools.partial(finish, cols=slice(None))
            k = pl.program_id(2)

            @pl.when(k == 0)
            def _():
                for acc, s in zip(acc_refs, sums):
                    acc[...] = s

            @pl.when(k > 0)
            def _():
                for acc, s in zip(acc_refs, sums):
                    acc[...] += s

            @pl.when(k == nk - 1)
            def _():
                finish([acc[...] for acc in acc_refs])

    scratch = [pltpu.VMEM((tm, tn), F32) for _ in range(n_acc)] if nk > 1 else []
    outs, carried = _carried_call(
        body, carry,
        name=name,
        grid=(m_dim // tm, n_dim // tn, nk),
        in_specs=[a_spec] * na + [b_spec] * nb + [t_spec] * ne,
        out_specs=[t_spec] * no,
        out_shape=[jax.ShapeDtypeStruct((m_dim, n_dim), dt) for dt in out_dtypes],
        scratch_shapes=scratch,
        semantics=("parallel", "parallel", "arbitrary"),
        args=[*a_list, *b_list, *extras],
    )
    return (outs, carried) if carry else outs


def _identity(accs, extras):
    return accs


def _single(result, carry):
    return (result[0][0], result[1]) if carry else result[0]


def _mm_nn(name, a, b, out_dtype, res=None, alpha=1.0, carry=None, **tiles):
    if res is None:
        return _single(_matmul(name, "nn", [a], [b], [(0, 0, 0)], [out_dtype], _identity, carry=carry, **tiles), carry)

    def epilogue(accs, extras):
        return [extras[0] + alpha * accs[0]]

    return _single(_matmul(name, "nn", [a], [b], [(0, 0, 0)], [out_dtype], epilogue, extras=(res,), carry=carry,
                           **tiles), carry)


def _mm_nt_sum(name, a_list, b_list, out_dtype, carry=None, n_part=(0, 1), **tiles):
    products = [(i, i, 0) for i in range(len(a_list))]
    return _single(_matmul(name, "nt", a_list, b_list, products, [out_dtype], _identity, carry=carry, n_part=n_part,
                           **tiles), carry)


def _mm_tn(name, a, b_list, out_dtype, carry=None, **tiles):
    products = [(0, i, i) for i in range(len(b_list))]
    return _matmul(name, "tn", [a], b_list, products, [out_dtype] * len(b_list), _identity, carry=carry, **tiles)


def _rowwise(name, fn, row_ins, par_ins, row_outs, acc_outs=(), tr=512, carry=None):
    first = row_ins[0][0] if isinstance(row_ins[0], tuple) else row_ins[0]
    t_dim = first.shape[0]
    tr = _tile(t_dim, tr, 16)
    arrays, specs = [], []
    for r in row_ins:
        if isinstance(r, tuple):
            arr, width, blk = r
            specs.append(pl.BlockSpec((tr, width), lambda i, blk=blk: (i, blk)))
        else:
            arr = r
            specs.append(pl.BlockSpec((tr, arr.shape[1]), lambda i: (i, 0)))
        arrays.append(arr)
    for p in par_ins:
        arrays.append(p)
        specs.append(pl.BlockSpec(p.shape, lambda i, nd=p.ndim: (0,) * nd))
    nr, npar, nro, nacc = len(row_ins), len(par_ins), len(row_outs), len(acc_outs)

    def body(*refs):
        rows = [r[...] for r in refs[:nr]]
        pars = [p[...] for p in refs[nr:nr + npar]]
        o_refs = refs[nr + npar:nr + npar + nro]
        acc_refs = refs[nr + npar + nro:]
        outs, accs = fn(rows, pars)
        for o_ref, o in zip(o_refs, outs):
            o_ref[...] = o.astype(o_ref.dtype)
        if nacc:
            @pl.when(pl.program_id(0) == 0)
            def _():
                for a_ref in acc_refs:
                    a_ref[...] = jnp.zeros_like(a_ref)

            for a_ref, a in zip(acc_refs, accs):
                a_ref[...] += a

    out_shape = [jax.ShapeDtypeStruct((t_dim, c), dt) for c, dt in row_outs]
    out_shape += [jax.ShapeDtypeStruct(s, F32) for s in acc_outs]
    out_specs = [pl.BlockSpec((tr, c), lambda i: (i, 0)) for c, _ in row_outs]
    out_specs += [pl.BlockSpec(s, lambda i: (0, 0)) for s in acc_outs]
    res, carried = _carried_call(body, carry, name=name, grid=(t_dim // tr,), in_specs=specs, out_specs=out_specs,
                                 out_shape=out_shape, scratch_shapes=[], semantics=("arbitrary",), args=arrays)
    return (res, carried) if carry else res


def _rms_stats(x):
    r = lax.rsqrt(jnp.mean(x * x, axis=-1, keepdims=True) + EPS)
    return x * r, r


def _rms_backward(x, g, dy):
    xh, r = _rms_stats(x)
    a = dy * g
    dx = r * (a - xh * jnp.mean(a * xh, axis=-1, keepdims=True))
    return dx, jnp.sum(dy * xh, axis=0, keepdims=True)


def _rms_fwd(name, x, g, carry=None):
    def fn(rows, pars):
        xh, _ = _rms_stats(rows[0])
        return [xh * pars[0]], []

    res = _rowwise(name, fn, [x], [g], [(x.shape[1], BF16)], carry=carry)
    return (res[0][0], res[1]) if carry else res[0]


def _rms_bwd(name, x, g, dy_parts, dres, scale):
    def fn(rows, pars):
        dy = (rows[2] if len(rows) == 3 else jnp.concatenate(rows[2:], axis=1)).astype(F32)
        dx, dg = _rms_backward(rows[0], pars[0], dy)
        tot = rows[1] + dx
        return [tot, scale * tot], [dg]

    d = x.shape[1]
    return _rowwise(name, fn, [x, dres, *dy_parts], [g], [(d, F32), (d, BF16)], [(1, d)], tr=256)


def _cast_into_gathered(name, w, kind, q_arr):
    rs, cs = w.shape
    tr = _tile(rs, 256, 16)
    nr = rs // tr

    def body(q_ref, w_ref, o_ref):
        o_ref[...] = w_ref[...].astype(BF16)

    if kind == "row":
        o_spec = pl.BlockSpec((tr, cs), lambda i, q_ref: (q_ref[0] * nr + i, 0))
    else:
        o_spec = pl.BlockSpec((tr, cs), lambda i, q_ref: (i, q_ref[0]))
    return pl.pallas_call(
        body,
        name=name,
        grid_spec=pltpu.PrefetchScalarGridSpec(num_scalar_prefetch=1, grid=(nr,),
                                               in_specs=[pl.BlockSpec((tr, cs), lambda i, q_ref: (i, 0))],
                                               out_specs=o_spec),
        out_shape=jax.ShapeDtypeStruct(_full_shape(kind, (rs, cs)), BF16),
        compiler_params=_params(("parallel",)),
    )(q_arr, w)


def _swiglu_tiles(gate, up):
    s = _sigmoid(gate)
    silu = gate * s
    return [silu * up, up * (s * (1.0 + gate * (1.0 - s))), silu]


def _ffn_gateup(tag, n, wg, wu, carry):
    def act(accs, extras):
        return _swiglu_tiles(accs[0], accs[1])

    return _matmul(tag + "_gateup", "nn", [n], [wg, wu], [(0, 0, 0), (0, 1, 1)], [BF16] * 3, act,
                   tm=1024, tn=512, tk=2048, carry=carry)


def _ffn_up(tag, n, wu, gate, carry):
    def act(accs, extras):
        return _swiglu_tiles(extras[0].astype(F32), accs[0])

    return _matmul(tag + "_up", "nn", [n], [wu], [(0, 0, 0)], [BF16] * 3, act, extras=(gate,),
                   tm=1024, tn=512, tk=2048, carry=carry)


def _ffn_down(tag, a, wd, h, carry=None):
    return _mm_nn(tag + "_down", a, wd, F32, res=h, alpha=0.5, tm=1024, tn=512, tk=5632, carry=carry)


def _ffn_bwd(tag, names, h, g, wg, wu, wd, saved, dh, dfb, next_scale, reducer, riding=None, riding_dwd=None,
             last_hop_later=False):
    n, a, da_dgate, da_dup = saved

    def act_bwd(accs, extras):
        return [accs[0] * extras[0].astype(F32), accs[0] * extras[1].astype(F32)]

    dact = _matmul(tag + "_dact", "nt", [dfb], [wd], [(0, 0, 0)], [BF16, BF16], act_bwd, extras=(da_dgate, da_dup),
                   tm=1024, tn=512, tk=2048, carry=riding)
    (dgp, du), rode = dact if riding else (dact, [])
    dwd_call = _mm_tn(tag + "_dwd", a, [dfb], BF16, tm=512, tn=2048, tk=4096, carry=riding_dwd)
    (dwd,), rode_dwd = dwd_call if riding_dwd else (dwd_call, [])
    down = reducer.swap(names[2:], [dwd])
    (dwg,), swapped = _mm_tn(tag + "_dwg", n, [dgp], BF16, tm=512, tn=1408, tk=4096, carry=down.carry)
    down = reducer.send(down, swapped)
    gate_w = reducer.swap(names[:1], [dwg])
    carry, split = _ride(down.carry, gate_w.carry)
    (dwu,), results = _mm_tn(tag + "_dwu", n, [du], BF16, tm=512, tn=1408, tk=4096, carry=carry)
    got, swapped = split(results)
    reducer.end(down, got)
    gate_w = reducer.send(gate_w, swapped)
    up_w = reducer.swap(names[1:2], [dwu])
    carry, split = _ride(gate_w.carry, up_w.carry)
    halves = 1 if last_hop_later else 2
    tiles = dict(tm=512, tn=512, tk=5632) if last_hop_later else dict(tm=1024, tn=1024, tk=1408)
    dn_lo, results = _mm_nt_sum(tag + "_dn_lo", [dgp, du], [wg, wu], BF16, carry=carry, n_part=(0, halves), **tiles)
    got, swapped = split(results)
    reducer.end(gate_w, got)
    up_w = reducer.send(up_w, swapped)
    dn = [dn_lo]
    if not last_hop_later:
        dn_hi, got = _mm_nt_sum(tag + "_dn_hi", [dgp, du], [wg, wu], BF16, tm=1024, tn=1024, tk=1408,
                                carry=up_w.carry, n_part=(1, 2))
        reducer.end(up_w, got)
        dn.append(dn_hi)
    dh_in, dh_in_b, dg = _rms_bwd(tag + "_dnorm", h, g, dn, dh, next_scale)
    return dh_in, dh_in_b, dg, rode, rode_dwd, (up_w if last_hop_later else None)


def _ssm_discretize(log_dt, a_re, a_im, b_re, b_im):
    dt = jnp.exp(log_dt)[:, None]
    lr = jnp.minimum(a_re, -1e-4)
    li = a_im
    mag = jnp.exp(lr * dt)
    ang = li * dt
    abar_r = mag * jnp.cos(ang)
    abar_i = mag * jnp.sin(ang)
    den = lr * lr + li * li
    xr = abar_r - 1.0
    xi = abar_i
    zr = (xr * lr + xi * li) / den
    zi = (xi * lr - xr * li) / den
    bbar_r = zr[..., None] * b_re - zi[..., None] * b_im
    bbar_i = zr[..., None] * b_im + zi[..., None] * b_re
    return abar_r, abar_i, bbar_r, bbar_i


def _blockdiag_in(b):
    g, n, p = b.shape
    nb = g // GROUPS_PER_BLOCK
    eye = jnp.eye(GROUPS_PER_BLOCK, dtype=b.dtype)
    b4 = b.reshape(nb, GROUPS_PER_BLOCK, n, p)
    return jnp.einsum("sgnp,gh->sgphn", b4, eye).reshape(nb, GROUPS_PER_BLOCK * p, GROUPS_PER_BLOCK * n)


def _blockdiag_in_grad(gw, n, p):
    nb = gw.shape[0]
    eye = jnp.eye(GROUPS_PER_BLOCK, dtype=gw.dtype)
    g5 = gw.reshape(nb, GROUPS_PER_BLOCK, p, GROUPS_PER_BLOCK, n)
    return jnp.einsum("sgphn,gh->sgnp", g5, eye).reshape(nb * GROUPS_PER_BLOCK, n, p)


def _blockdiag_out(c):
    g, p, n = c.shape
    nb = g // GROUPS_PER_BLOCK
    eye = jnp.eye(GROUPS_PER_BLOCK, dtype=c.dtype)
    c4 = c.reshape(nb, GROUPS_PER_BLOCK, p, n)
    return jnp.einsum("sgpn,gh->shngp", c4, eye).reshape(nb, GROUPS_PER_BLOCK * n, GROUPS_PER_BLOCK * p)


def _blockdiag_out_grad(gw, p, n):
    nb = gw.shape[0]
    eye = jnp.eye(GROUPS_PER_BLOCK, dtype=gw.dtype)
    g5 = gw.reshape(nb, GROUPS_PER_BLOCK, n, GROUPS_PER_BLOCK, p)
    return jnp.einsum("shngp,gh->sgpn", g5, eye).reshape(nb * GROUPS_PER_BLOCK, p, n)


def _ssm_scan_fwd(z, wb, wc, abar, d, carry=None):
    t_dim = z.shape[0]
    nb, cb, sw2 = wb.shape
    nl = sw2 // LANES
    hl = nl // 2
    tt = _tile(t_dim, 256, 8)
    nt = t_dim // tt

    def body(z_ref, wb_ref, wc_ref, a_ref, d_ref, s_ref, y_ref, yg_ref, drive_ref, st_ref):
        @pl.when(pl.program_id(0) == 0)
        def _():
            st_ref[...] = jnp.zeros_like(st_ref)

        u = z_ref[...]
        ub = u.astype(BF16)
        for b in range(nb):
            drive = _dot(ub[:, b * cb:(b + 1) * cb], wb_ref[b], "nn")
            for l in range(nl):
                drive_ref[l, pl.ds(b, tt, stride=nb), :] = drive[:, l * LANES:(l + 1) * LANES]
        a = a_ref[...]
        chunk = lambda v, l: v[:, l * LANES:(l + 1) * LANES]

        def step(t, state):
            rows = pl.ds(pl.multiple_of(t * nb, nb), nb)
            re, im = [], []
            for l in range(hl):
                ar, ai, sr, si = chunk(a, l), chunk(a, hl + l), state[l], state[hl + l]
                nr = ar * sr - ai * si + drive_ref[l, rows, :]
                ni = ar * si + ai * sr + drive_ref[hl + l, rows, :]
                s_ref[l, rows, :] = nr
                s_ref[hl + l, rows, :] = ni
                re.append(nr)
                im.append(ni)
            return tuple(re + im)

        state = lax.fori_loop(0, tt, step, tuple(st_ref[l] for l in range(nl)), unroll=8)
        for l in range(nl):
            st_ref[l] = state[l]
        parts = []
        for b in range(nb):
            s_b = jnp.concatenate([s_ref[l, pl.ds(b, tt, stride=nb), :] for l in range(nl)], axis=1)
            parts.append(_dot(s_b, wc_ref[b], "nn"))
        y = jnp.concatenate(parts, axis=1) + d_ref[...] * u
        y_ref[...] = y
        yg_ref[...] = _gelu(y).astype(BF16)

    full = lambda a: pl.BlockSpec(a.shape, lambda t, nd=a.ndim: (0,) * nd)
    return _carried_call(
        body, carry,
        name="ssm_scan_fwd",
        grid=(nt,),
        in_specs=[pl.BlockSpec((tt, nb * cb), lambda t: (t, 0)), full(wb), full(wc), full(abar), full(d)],
        out_specs=[
            pl.BlockSpec((nl, tt * nb, LANES), lambda t: (0, t, 0)),
            pl.BlockSpec((tt, nb * cb), lambda t: (t, 0)),
            pl.BlockSpec((tt, nb * cb), lambda t: (t, 0)),
        ],
        out_shape=[
            jax.ShapeDtypeStruct((nl, t_dim * nb, LANES), F32),
            jax.ShapeDtypeStruct((t_dim, nb * cb), F32),
            jax.ShapeDtypeStruct((t_dim, nb * cb), BF16),
        ],
        scratch_shapes=[pltpu.VMEM((nl, tt * nb, LANES), F32), pltpu.VMEM((nl, nb, LANES), F32)],
        semantics=("arbitrary",),
        args=[z, wb, wc, abar, d],
    )


def _ssm_scan_bwd(z, dy, states, wb, wc, abar_conj, d, dz_all, carry=None):
    t_dim = z.shape[0]
    nb, cb, sw2 = wb.shape
    nl = sw2 // LANES
    hl = nl // 2
    tt = _tile(t_dim, 256, 8)
    nt = t_dim // tt
    edges = states.reshape(nl, nt, tt * nb, LANES)[:, :, (tt - 1) * nb:, :]
    before = jnp.concatenate([jnp.zeros((nl, 1, nb, LANES), F32), edges[:, :-1]], axis=1).reshape(nl, nt * nb, LANES)

    def body(z_ref, dy_ref, s_ref, sp_ref, wb_ref, wc_ref, a_ref, d_ref, dz_all_ref,
             dz_ref, gwb_ref, gwc_ref, ga_ref, gd_ref, gin_ref, gs_ref, st_ref):
        @pl.when(pl.program_id(0) == 0)
        def _():
            st_ref[...] = jnp.zeros_like(st_ref)
            gwb_ref[...] = jnp.zeros_like(gwb_ref)
            gwc_ref[...] = jnp.zeros_like(gwc_ref)
            ga_ref[...] = jnp.zeros_like(ga_ref)
            gd_ref[...] = jnp.zeros_like(gd_ref)

        u = z_ref[...]
        dyv = dy_ref[...]
        ub = u.astype(BF16)
        dyb = dyv.astype(BF16)
        for b in range(nb):
            gin = _dot(dyb[:, b * cb:(b + 1) * cb], wc_ref[b], "nt")
            for l in range(nl):
                gin_ref[l, pl.ds(b, tt, stride=nb), :] = gin[:, l * LANES:(l + 1) * LANES]
        a = a_ref[...]
        chunk = lambda v, l: v[:, l * LANES:(l + 1) * LANES]

        def step(k, state):
            rows = pl.ds(pl.multiple_of((tt - 1 - k) * nb, nb), nb)
            re, im = [], []
            for l in range(hl):
                ar, ai, gr, gi = chunk(a, l), chunk(a, hl + l), state[l], state[hl + l]
                nr = ar * gr - ai * gi + gin_ref[l, rows, :]
                ni = ar * gi + ai * gr + gin_ref[hl + l, rows, :]
                gs_ref[l, rows, :] = nr
                gs_ref[hl + l, rows, :] = ni
                re.append(nr)
                im.append(ni)
            return tuple(re + im)

        state = lax.fori_loop(0, tt, step, tuple(st_ref[l] for l in range(nl)), unroll=8)
        for l in range(nl):
            st_ref[l] = state[l]

        parts = []
        for b in range(nb):
            cols = slice(b * cb, (b + 1) * cb)
            gs_b = jnp.concatenate([gs_ref[l, pl.ds(b, tt, stride=nb), :] for l in range(nl)], axis=1)
            s_b = jnp.concatenate([s_ref[l, pl.ds(b, tt, stride=nb), :] for l in range(nl)], axis=1)
            parts.append(_dot(gs_b, wb_ref[b], "nt"))
            gwb_ref[b] += _dot(ub[:, cols], gs_b, "tn")
            gwc_ref[b] += _dot(s_b, dyb[:, cols], "tn")
        dz_ref[...] = (jnp.concatenate(parts, axis=1) + d_ref[...] * dyv).astype(BF16)
        gd_ref[...] += jnp.sum(dyv * u, axis=0, keepdims=True)

        row = lax.broadcasted_iota(jnp.int32, (tt * nb, LANES), 0)
        shifted = lambda v: jnp.where(row < nb, 0.0, pltpu.roll(v, nb, 0))
        over_time = lambda v: jnp.sum(v.reshape(tt, nb, LANES), axis=0)
        for l in range(hl):
            g_r, g_i = gs_ref[l], gs_ref[hl + l]
            p_r, p_i = shifted(s_ref[l]), shifted(s_ref[hl + l])
            f_r, f_i = sp_ref[l], sp_ref[hl + l]
            g0_r, g0_i = gs_ref[l, pl.ds(0, nb), :], gs_ref[hl + l, pl.ds(0, nb), :]
            ga_ref[l] += over_time(g_r * p_r + g_i * p_i) + g0_r * f_r + g0_i * f_i
            ga_ref[hl + l] += over_time(g_i * p_r - g_r * p_i) + g0_i * f_r - g0_r * f_i

    rev = lambda t: (nt - 1 - t, 0)
    rev3 = lambda t: (0, nt - 1 - t, 0)
    full = lambda a: pl.BlockSpec(a.shape, lambda t, nd=a.ndim: (0,) * nd)
    return _carried_call(
        body, carry,
        name="ssm_scan_bwd",
        grid=(nt,),
        in_specs=[
            pl.BlockSpec((tt, nb * cb), rev),
            pl.BlockSpec((tt, nb * cb), rev),
            pl.BlockSpec((nl, tt * nb, LANES), rev3),
            pl.BlockSpec((nl, nb, LANES), rev3),
            full(wb), full(wc), full(abar_conj), full(d), _ANY,
        ],
        out_specs=[
            pl.BlockSpec((tt, nb * cb), rev),
            pl.BlockSpec((nb, cb, sw2), lambda t: (0, 0, 0)),
            pl.BlockSpec((nb, sw2, cb), lambda t: (0, 0, 0)),
            pl.BlockSpec((nl, nb, LANES), lambda t: (0, 0, 0)),
            pl.BlockSpec((1, nb * cb), lambda t: (0, 0)),
        ],
        out_shape=[
            jax.ShapeDtypeStruct(dz_all.shape, BF16),
            jax.ShapeDtypeStruct((nb, cb, sw2), F32),
            jax.ShapeDtypeStruct((nb, sw2, cb), F32),
            jax.ShapeDtypeStruct((nl, nb, LANES), F32),
            jax.ShapeDtypeStruct((1, nb * cb), F32),
        ],
        scratch_shapes=[pltpu.VMEM((nl, tt * nb, LANES), F32), pltpu.VMEM((nl, tt * nb, LANES), F32),
                        pltpu.VMEM((nl, nb, LANES), F32)],
        semantics=("arbitrary",),
        args=[z, dy, states, before, wb, wc, abar_conj, d, dz_all],
        aliases={8: 0},
    )


def _gmlp_chunk(zu, zv, gv, wm_ref, bias, n_heads):
    ua = _gelu(zu)
    vg = _gelu(zv)
    xc = vg - jnp.mean(vg, axis=-1, keepdims=True)
    r = lax.rsqrt(jnp.mean(xc * xc, axis=-1, keepdims=True) + EPS)
    vh = xc * r
    vb = (vh * gv).astype(BF16)
    parts = []
    for h in range(n_heads):
        cols = slice(h * GMLP_HEAD, (h + 1) * GMLP_HEAD)
        parts.append(_dot(wm_ref[h], vb[:, cols], "nn"))
    s = jnp.concatenate(parts, axis=1) + bias
    return ua, vh, r, vb, s


def _gmlp_fwd(z, gv, wm, bias, ggo):
    t_dim = z.shape[0]
    dg = gv.shape[1]
    n_heads = dg // GMLP_HEAD
    tr = _tile(t_dim, 256, CHUNK)

    def body(zu_ref, zv_ref, gv_ref, wm_ref, b_ref, ggo_ref, o_ref):
        for ck in range(tr // CHUNK):
            rows = pl.ds(ck * CHUNK, CHUNK)
            ua, _, _, _, s = _gmlp_chunk(zu_ref[rows, :], zv_ref[rows, :], gv_ref[...], wm_ref, b_ref[...], n_heads)
            yh, _ = _rms_stats(ua * s)
            o_ref[rows, :] = (yh * ggo_ref[...]).astype(BF16)

    full = lambda a: pl.BlockSpec(a.shape, lambda i, nd=a.ndim: (0,) * nd)
    return pl.pallas_call(
        body,
        name="gmlp_fwd",
        grid=(t_dim // tr,),
        in_specs=[pl.BlockSpec((tr, dg), lambda i: (i, 1)), pl.BlockSpec((tr, dg), lambda i: (i, 2)),
                  full(gv), full(wm), full(bias), full(ggo)],
        out_specs=pl.BlockSpec((tr, dg), lambda i: (i, 0)),
        out_shape=jax.ShapeDtypeStruct((t_dim, dg), BF16),
        compiler_params=_params(("parallel",)),
    )(z, z, gv, wm, bias, ggo)


def _gmlp_bwd(z, dycat, gv, wm, bias, ggo):
    t_dim = z.shape[0]
    dg = gv.shape[1]
    n_heads = dg // GMLP_HEAD
    tr = _tile(t_dim, 256, CHUNK)

    def body(zu_ref, zv_ref, dy_ref, gv_ref, wm_ref, b_ref, ggo_ref,
             dz_ref, dggo_ref, dgv_ref, dwm_ref, dsum_ref):
        @pl.when(pl.program_id(0) == 0)
        def _():
            dggo_ref[...] = jnp.zeros_like(dggo_ref)
            dgv_ref[...] = jnp.zeros_like(dgv_ref)
            dwm_ref[...] = jnp.zeros_like(dwm_ref)
            dsum_ref[...] = jnp.zeros_like(dsum_ref)

        for ck in range(tr // CHUNK):
            rows = pl.ds(ck * CHUNK, CHUNK)
            zu = zu_ref[rows, :]
            zv = zv_ref[rows, :]
            gvv = gv_ref[...]
            ua, vh, r, vb, s = _gmlp_chunk(zu, zv, gvv, wm_ref, b_ref[...], n_heads)
            dy, dggo = _rms_backward(ua * s, ggo_ref[...], dy_ref[rows, :].astype(F32))
            dggo_ref[...] += dggo
            ds = dy * ua
            dsum_ref[...] += ds
            dsb = ds.astype(BF16)
            parts = []
            for h in range(n_heads):
                cols = slice(h * GMLP_HEAD, (h + 1) * GMLP_HEAD)
                dwm_ref[h] += _dot(dsb[:, cols], vb[:, cols], "nt")
                parts.append(_dot(wm_ref[h], dsb[:, cols], "tn"))
            dv = jnp.concatenate(parts, axis=1)
            dgv_ref[...] += jnp.sum(dv * vh, axis=0, keepdims=True)
            dvh = dv * gvv
            dvg = r * (dvh - jnp.mean(dvh, axis=-1, keepdims=True) - vh * jnp.mean(dvh * vh, axis=-1, keepdims=True))
            dz_ref[rows, pl.ds(2 * dg, dg)] = (dvg * _gelu_grad(zv)).astype(BF16)
            dz_ref[rows, pl.ds(dg, dg)] = (dy * s * _gelu_grad(zu)).astype(BF16)

    full = lambda a: pl.BlockSpec(a.shape, lambda i, nd=a.ndim: (0,) * nd)
    return pl.pallas_call(
        body,
        name="gmlp_bwd",
        grid=(t_dim // tr,),
        in_specs=[pl.BlockSpec((tr, dg), lambda i: (i, 1)), pl.BlockSpec((tr, dg), lambda i: (i, 2)),
                  pl.BlockSpec((tr, dg), lambda i: (i, 1)), full(gv), full(wm), full(bias), full(ggo)],
        out_specs=[pl.BlockSpec((tr, 3 * dg), lambda i: (i, 0)),
                   pl.BlockSpec((1, dg), lambda i: (0, 0)), pl.BlockSpec((1, dg), lambda i: (0, 0)),
                   pl.BlockSpec(wm.shape, lambda i: (0, 0, 0)), pl.BlockSpec((CHUNK, dg), lambda i: (0, 0))],
        out_shape=[jax.ShapeDtypeStruct((t_dim, 3 * dg), BF16),
                   jax.ShapeDtypeStruct((1, dg), F32), jax.ShapeDtypeStruct((1, dg), F32),
                   jax.ShapeDtypeStruct(wm.shape, F32), jax.ShapeDtypeStruct((CHUNK, dg), F32)],
        compiler_params=_params(("arbitrary",)),
    )(z, z, dycat, gv, wm, bias, ggo)


def _ple_head(npl, w_gate, h3, p, w_proj, tgt, g_final):
    t_dim, d = h3.shape
    tr = _tile(t_dim, 256, 16)

    def body(n_ref, w_ref, h_ref, p_ref, wp_ref, t_ref, g_ref, dgq_ref, dpp_ref, dh_ref, dg_ref, loss_ref):
        @pl.when(pl.program_id(0) == 0)
        def _():
            dg_ref[...] = jnp.zeros_like(dg_ref)
            loss_ref[...] = jnp.zeros_like(loss_ref)

        gate = _sigmoid(_dot(n_ref[...], w_ref[...], "nn"))
        ppv = _dot(p_ref[...], wp_ref[...], "nn")
        h4 = h_ref[...] + gate * ppv
        xh, _ = _rms_stats(h4)
        err = xh * g_ref[...] - t_ref[...]
        dh4, dg = _rms_backward(h4, g_ref[...], err * (1.0 / d))
        dh_ref[...] = dh4
        dgq_ref[...] = (dh4 * ppv * gate * (1.0 - gate)).astype(BF16)
        dpp_ref[...] = (dh4 * gate).astype(BF16)
        dg_ref[...] += dg
        loss_ref[...] += jnp.full((1, LANES), 0.5 * jnp.sum(err * err) * (1.0 / d), F32)

    rows = pl.BlockSpec((tr, d), lambda i: (i, 0))
    whole = lambda a: pl.BlockSpec(a.shape, lambda i: (0, 0))
    return pl.pallas_call(
        body,
        name="ple_head",
        grid=(t_dim // tr,),
        in_specs=[rows, whole(w_gate), rows, pl.BlockSpec((tr, p.shape[1]), lambda i: (i, 0)), whole(w_proj), rows,
                  whole(g_final)],
        out_specs=[rows, rows, rows, pl.BlockSpec((1, d), lambda i: (0, 0)), pl.BlockSpec((1, LANES), lambda i: (0, 0))],
        out_shape=[jax.ShapeDtypeStruct((t_dim, d), BF16), jax.ShapeDtypeStruct((t_dim, d), BF16),
                   jax.ShapeDtypeStruct((t_dim, d), F32), jax.ShapeDtypeStruct((1, d), F32),
                   jax.ShapeDtypeStruct((1, LANES), F32)],
        compiler_params=_params(("arbitrary",)),
    )(npl, w_gate, h3, p, w_proj, tgt, g_final)


def _position():
    x, y, c = lax.axis_index("x"), lax.axis_index("y"), lax.axis_index("c")
    chips = [(1 - x, y), (x, 1 - y), (1 - x, 1 - y)]
    return x, y, c, chips


def _region(ref, kind, shard_shape, q, half, part=None):
    rs, cs = shard_shape
    r0, nr = (0, rs) if half is None else (half * (rs // 2), rs // 2)
    if part is not None:
        r0, nr = r0 + part * (rs // 4), rs // 4
    if kind == "row":
        return ref.at[pl.ds(q * rs + r0, nr), :]
    return ref.at[pl.ds(r0, nr), pl.ds(q * cs, cs)]


def _full_shape(kind, shard_shape):
    rs, cs = shard_shape
    return (N_CHIPS * rs, cs) if kind == "row" else (rs, N_CHIPS * cs)


def _remote(src, dst, send_sems, recv_sems, k, to):
    return pltpu.make_async_remote_copy(src_ref=src, dst_ref=dst, send_sem=send_sems.at[k], recv_sem=recv_sems.at[k],
                                        device_id=to, device_id_type=MESH)


def _same(arrays):
    return [jax.ShapeDtypeStruct(a.shape, a.dtype) for a in arrays]


def _gather_near_carry(gathered, kinds, shapes):
    nw = len(gathered)

    def copies(ops, full, send_sems, recv_sems):
        x, y, c, _ = _position()
        out = []
        for w in range(nw):
            mine = _region(full[w], kinds[w], shapes[w], 2 * x + y, c)
            out.append(_remote(mine, mine, send_sems, recv_sems, 2 * w, (1 - x, y, c)))
            out.append(_remote(mine, mine, send_sems, recv_sems, 2 * w + 1, (x, 1 - y, c)))
        return out

    return _Carry(gathered, _same(gathered), {i: i for i in range(nw)}, 2 * nw, copies)


def _gather_far_carry(gathered, kinds, shapes):
    nw = len(gathered)

    def copies(ops, full, send_sems, recv_sems):
        x, y, c, _ = _position()
        out = []
        for w in range(nw):
            from_x = _region(full[w], kinds[w], shapes[w], 2 * (1 - x) + y, c, part=1)
            from_y = _region(full[w], kinds[w], shapes[w], 2 * x + (1 - y), c, part=0)
            out.append(_remote(from_x, from_x, send_sems, recv_sems, 2 * w, (x, 1 - y, c)))
            out.append(_remote(from_y, from_y, send_sems, recv_sems, 2 * w + 1, (1 - x, y, c)))
        return out

    return _Carry(gathered, _same(gathered), {i: i for i in range(nw)}, 2 * nw, copies)


def _gather_d2d_carry(gathered, kinds, shapes):
    nw = len(gathered)

    def copies(ops, full, send_sems, recv_sems):
        x, y, c, chips = _position()
        out = []
        for w in range(nw):
            for j, (cx, cy) in enumerate(chips):
                landed = _region(full[w], kinds[w], shapes[w], 2 * cx + cy, c)
                out.append(_remote(landed, landed, send_sems, recv_sems, 3 * w + j, (x, y, 1 - c)))
        return out

    return _Carry(gathered, _same(gathered), {i: i for i in range(nw)}, 3 * nw, copies)


def _pairs_carry(grads, kinds, shapes):
    nw = len(grads)

    def copies(g, got, send_sems, recv_sems):
        x, y, c, _ = _position()
        out = []
        for w in range(nw):
            for q in range(N_CHIPS):
                out.append(_remote(_region(g[w], kinds[w], shapes[w], q, 1 - c), got[w].at[q], send_sems, recv_sems,
                                   N_CHIPS * w + q, (x, y, 1 - c)))
        return out

    outs = [jax.ShapeDtypeStruct((N_CHIPS, s[0] // 2, s[1]), BF16) for s in shapes]
    return _Carry(grads, outs, {}, N_CHIPS * nw, copies)


def _pair_sum(name, grad, got, kind, shard_shape, c_arr):
    rs, cs = shard_shape
    hr = rs // 2
    tr = _tile(hr, 512, 16)
    nr = hr // tr

    def body(c_ref, g_ref, s_ref, o_ref):
        o_ref[...] = (g_ref[...].astype(F32) + s_ref[...].astype(F32)).astype(BF16)

    if kind == "row":
        g_spec = pl.BlockSpec((tr, cs), lambda q, i, c_ref: (q * (rs // tr) + c_ref[0] * nr + i, 0))
    else:
        g_spec = pl.BlockSpec((tr, cs), lambda q, i, c_ref: (c_ref[0] * nr + i, q))
    blk = pl.BlockSpec((None, tr, cs), lambda q, i, c_ref: (q, i, 0))
    return pl.pallas_call(
        body,
        name=name,
        grid_spec=pltpu.PrefetchScalarGridSpec(num_scalar_prefetch=1, grid=(N_CHIPS, nr), in_specs=[g_spec, blk],
                                               out_specs=blk),
        out_shape=jax.ShapeDtypeStruct((N_CHIPS, hr, cs), BF16),
        compiler_params=_params(("parallel", "parallel")),
    )(c_arr, grad, got)


def _scatter_carry(sums, shapes):
    nw = len(sums)

    def copies(ps, got, send_sems, recv_sems):
        x, y, c, chips = _position()
        out = []
        for w in range(nw):
            for j, (cx, cy) in enumerate(chips):
                out.append(_remote(ps[w].at[2 * cx + cy], got[w].at[j], send_sems, recv_sems, 3 * w + j, (cx, cy, c)))
        return out

    outs = [jax.ShapeDtypeStruct((3, s[0] // 2, s[1]), BF16) for s in shapes]
    return _Carry(sums, outs, {}, 3 * nw, copies)


def _owner_sum(name, sums, got, shard_shape, qc_arr):
    rs, cs = shard_shape
    hr = rs // 2
    tr = _tile(hr, 512, 16)
    nr = hr // tr

    def body(qc_ref, mine_ref, got_ref, o_ref):
        acc = mine_ref[...].astype(F32)
        for j in range(3):
            acc = acc + got_ref[j].astype(F32)
        o_ref[...] = acc

    return pl.pallas_call(
        body,
        name=name,
        grid_spec=pltpu.PrefetchScalarGridSpec(
            num_scalar_prefetch=1, grid=(nr,),
            in_specs=[pl.BlockSpec((None, tr, cs), lambda i, qc_ref: (qc_ref[0], i, 0)),
                      pl.BlockSpec((3, tr, cs), lambda i, qc_ref: (0, i, 0))],
            out_specs=pl.BlockSpec((tr, cs), lambda i, qc_ref: (qc_ref[1] * nr + i, 0))),
        out_shape=jax.ShapeDtypeStruct((rs, cs), F32),
        compiler_params=_params(("parallel",)),
    )(qc_arr, sums, got)


def _share_carry(grads, shapes):
    nw = len(grads)

    def copies(ops, out, send_sems, recv_sems):
        x, y, c, _ = _position()
        res = []
        for w in range(nw):
            hr = shapes[w][0] // 2
            mine = out[w].at[pl.ds(c * hr, hr), :]
            res.append(_remote(mine, mine, send_sems, recv_sems, w, (x, y, 1 - c)))
        return res

    return _Carry(grads, _same(grads), {i: i for i in range(nw)}, nw, copies)


def _place_block(packed, me):
    return lax.dynamic_update_slice(jnp.zeros((N_DEV,) + packed.shape, F32), packed[None], (me, 0, 0))


def _exchange_carry(blocks):
    def copies(ops, res, send_sems, recv_sems):
        x, y, c, _ = _position()
        mine = res[0].at[4 * x + 2 * y + c]
        out = []
        for k in range(1, N_DEV):
            to = ((1 - x) if k & 4 else x, (1 - y) if k & 2 else y, (1 - c) if k & 1 else c)
            out.append(_remote(mine, mine, send_sems, recv_sems, k - 1, to))
        return out

    return _Carry([blocks], _same([blocks]), {0: 0}, N_DEV - 1, copies)


def _sum_blocks(name, blocks):
    n, rows, lanes = blocks.shape
    tr = _tile(rows, 4096, 8)

    def body(b_ref, o_ref):
        acc = b_ref[0]
        for k in range(1, n):
            acc = acc + b_ref[k]
        o_ref[...] = acc

    return pl.pallas_call(
        body,
        name=name,
        grid=(rows // tr,),
        in_specs=[pl.BlockSpec((n, tr, lanes), lambda i: (0, i, 0))],
        out_specs=pl.BlockSpec((tr, lanes), lambda i: (i, 0)),
        out_shape=jax.ShapeDtypeStruct((rows, lanes), F32),
        compiler_params=_params(("parallel",)),
    )(blocks)


def _adamw(name, w, g, m, v):
    def fn(rows, pars):
        wv, gv, mv, vv = rows
        m_new = ADAM_B1 * mv + (1.0 - ADAM_B1) * gv
        v_new = ADAM_B2 * vv + (1.0 - ADAM_B2) * (gv * gv)
        m_hat = m_new / (1.0 - ADAM_B1 ** ADAM_STEP)
        v_hat = v_new / (1.0 - ADAM_B2 ** ADAM_STEP)
        delta = -ADAM_LR * (m_hat / (jnp.sqrt(v_hat) + ADAM_EPS) + ADAM_WD * wv)
        return [delta, m_new, v_new, gv], []

    c = w.shape[1]
    return _rowwise(name, fn, [w, g, m, v], [], [(c, F32)] * 4, tr=256)


def _pack(arrays):
    rows = []
    for a in arrays:
        flat = a.reshape(-1).astype(F32)
        pad = (-flat.shape[0]) % LANES
        rows.append(jnp.pad(flat, (0, pad)).reshape(-1, LANES))
    stacked = jnp.concatenate(rows, axis=0)
    pad_rows = (-stacked.shape[0]) % 8
    return jnp.pad(stacked, ((0, pad_rows), (0, 0)))


def _unpack(packed, shapes):
    out, r = [], 0
    for s in shapes:
        n = math.prod(s)
        nr = -(-n // LANES)
        out.append(packed[r:r + nr].reshape(-1)[:n].reshape(s))
        r += nr
    return out


BIG = ["w1_gate", "w1_up", "w1_down", "w_in", "ssm_w_glu", "w_out", "w2_gate", "w2_up", "w2_down", "w_ple_gate",
       "w_ple_proj"]
KIND = {"w1_gate": "col", "w1_up": "col", "w1_down": "row", "w_in": "col", "ssm_w_glu": "row", "w_out": "row",
        "w2_gate": "col", "w2_up": "col", "w2_down": "row", "w_ple_gate": "row", "w_ple_proj": "col"}
SMALL = ["norm_ffn1", "norm_mix", "ssm_log_dt", "ssm_a_re", "ssm_a_im", "ssm_b_re", "ssm_b_im", "ssm_c_re", "ssm_c_im",
         "ssm_d", "gmlp_norm_v", "gmlp_w_s", "gmlp_b_s", "norm_ssm_out", "norm_gmlp_out", "norm_ffn2", "norm_ple",
         "norm_final"]
WEIGHTS = ["norm_ffn1", "w1_gate", "w1_up", "w1_down", "norm_mix", "w_in", "ssm_log_dt", "ssm_a_re", "ssm_a_im",
           "ssm_b_re", "ssm_b_im", "ssm_c_re", "ssm_c_im", "ssm_d", "ssm_w_glu", "gmlp_norm_v", "gmlp_w_s", "gmlp_b_s",
           "norm_ssm_out", "norm_gmlp_out", "w_out", "norm_ffn2", "w2_gate", "w2_up", "w2_down", "norm_ple",
           "w_ple_gate", "w_ple_proj", "norm_final"]


class _Trip:
    def __init__(self, names, arrays, carry):
        self.names, self.arrays, self.carry = names, arrays, carry


class _Reducer:
    def __init__(self, shard_shape, c_arr, qc_arr):
        self.shard_shape, self.c_arr, self.qc_arr = shard_shape, c_arr, qc_arr
        self.halves = {}

    def swap(self, names, grads):
        kinds = [KIND[n] for n in names]
        shapes = [self.shard_shape[n] for n in names]
        return _Trip(names, grads, _pairs_carry(grads, kinds, shapes))

    def send(self, trip, swapped):
        shapes = [self.shard_shape[n] for n in trip.names]
        sums = [_pair_sum("pair_sum_" + n, g, s, KIND[n], sh, self.c_arr)
                for n, g, s, sh in zip(trip.names, trip.arrays, swapped, shapes)]
        return _Trip(trip.names, sums, _scatter_carry(sums, shapes))

    def end(self, trip, got):
        for n, ps, g in zip(trip.names, trip.arrays, got):
            self.halves[n] = _owner_sum("owner_sum_" + n, ps, g, self.shard_shape[n], self.qc_arr)


def _ride(*carries):
    present = [c for c in carries if c is not None]
    joined = functools.reduce(_join, present) if present else None

    def split(results):
        out, at = [], 0
        for c in carries:
            n = len(c.out_shapes) if c is not None else 0
            out.append(list(results[at:at + n]))
            at += n
        return out

    return joined, split


def _step(x, p, tgt, w, m, v):
    d_model = x.shape[1]
    d_ssm = w["ssm_d"].shape[1]
    n_groups = d_ssm // SSM_GROUP
    row = lambda a: a.reshape(1, -1)

    xi, yi, ci = lax.axis_index("x"), lax.axis_index("y"), lax.axis_index("c")
    c_arr = jnp.reshape(ci, (1,)).astype(jnp.int32)
    q_arr = jnp.reshape(2 * xi + yi, (1,)).astype(jnp.int32)
    qc_arr = jnp.stack([2 * xi + yi, ci]).astype(jnp.int32)
    shard_shape = {n: w[n].shape for n in BIG}
    full = {n: _cast_into_gathered("cast_" + n, w[n], KIND[n], q_arr) for n in BIG}

    def gather(stage, names):
        return stage([full[n] for n in names], [KIND[n] for n in names], [shard_shape[n] for n in names])

    def gathered(names, arrays):
        full.update(zip(names, arrays))

    groups = [["w1_gate"], ["w1_up"], ["w1_down"], ["w_in"], ["w2_gate"], ["ssm_w_glu", "w_out"], ["w2_up"],
              ["w2_down", "w_ple_gate", "w_ple_proj"]]
    near, far, d2d = _gather_near_carry, _gather_far_carry, _gather_d2d_carry

    def stages(*work):
        carries = [gather(stage, groups[g]) for stage, g in work]
        return functools.reduce(_join, carries), [n for _, g in work for n in groups[g]]

    def alone(name, *work):
        carry, names = stages(*work)
        gathered(names, _comm_call(name, carry))

    alone("gather_a", (near, 0))
    alone("gather_b", (far, 0), (near, 1))
    carry, names = stages((d2d, 0), (far, 1))
    n1, landed = _rms_fwd("ffn1_norm", x, w["norm_ffn1"], carry)
    gathered(names, landed)
    carry, names = stages((d2d, 1), (near, 2), (near, 3))
    gate1, landed = _mm_nn("ffn1_gate", n1, full["w1_gate"], BF16, tm=1024, tn=512, tk=2048, carry=carry)
    gathered(names, landed)
    carry, names = stages((far, 2), (far, 3), (near, 4))
    (a1, da_dgate1, da_dup1), landed = _ffn_up("ffn1", n1, full["w1_up"], gate1, carry)
    gathered(names, landed)
    alone("gather_d", (d2d, 2))
    carry, names = stages((d2d, 3), (far, 4), (near, 5))
    h1, landed = _ffn_down("ffn1", a1, full["w1_down"], x, carry)
    gathered(names, landed)
    ffn1 = (n1, a1, da_dgate1, da_dup1)
    nm = _rms_fwd("mix_norm", h1, w["norm_mix"])
    carry, names = stages((d2d, 4), (far, 5), (near, 6))
    z, landed = _mm_nn("in_proj", nm, full["w_in"], F32, tm=1024, tn=512, tk=2048, carry=carry)
    gathered(names, landed)

    disc, disc_vjp = jax.vjp(_ssm_discretize, w["ssm_log_dt"][0], w["ssm_a_re"], w["ssm_a_im"], w["ssm_b_re"],
                             w["ssm_b_im"])
    abar_r, abar_i, bbar_r, bbar_i = disc
    nb = n_groups // GROUPS_PER_BLOCK
    wb = jnp.concatenate([_blockdiag_in(bbar_r), _blockdiag_in(bbar_i)], axis=-1).astype(BF16)
    wc = jnp.concatenate([_blockdiag_out(w["ssm_c_re"]), -_blockdiag_out(w["ssm_c_im"])], axis=1).astype(BF16)
    abar = jnp.concatenate([abar_r.reshape(nb, -1), abar_i.reshape(nb, -1)], axis=-1)
    abar_conj = jnp.concatenate([abar_r.reshape(nb, -1), -abar_i.reshape(nb, -1)], axis=-1)
    carry, names = stages((d2d, 5), (far, 6), (near, 7))
    (states, y_pre, yg), landed = _ssm_scan_fwd(z, wb, wc, abar, w["ssm_d"], carry)
    gathered(names, landed)
    q = _mm_nn("glu_proj", yg, full["ssm_w_glu"], F32, tm=1024, tn=1024, tk=1024)

    def glu_norm(rows, pars):
        yv = _gelu(rows[0]) * _sigmoid(rows[1])
        yh, _ = _rms_stats(yv)
        return [yh * pars[0]], []

    yn_ssm = _rowwise("ssm_glu_norm", glu_norm, [y_pre, q], [w["norm_ssm_out"]], [(d_ssm, BF16)])[0]

    tril = jnp.tril(jnp.ones((CHUNK, CHUNK), dtype=bool))
    wm = jnp.where(tril[None], w["gmlp_w_s"], 0.0).astype(BF16)
    bias = jnp.repeat(w["gmlp_b_s"].T, GMLP_HEAD, axis=1)
    yn_gmlp = _gmlp_fwd(z, w["gmlp_norm_v"], wm, bias, w["norm_gmlp_out"])
    ycat = jnp.concatenate([yn_ssm, yn_gmlp], axis=1)
    carry, names = stages((d2d, 6), (far, 7))
    h2, landed = _mm_nn("out_proj", ycat, full["w_out"], F32, res=h1, alpha=1.0, tm=512, tn=1024, tk=2048,
                        carry=carry)
    gathered(names, landed)

    n2 = _rms_fwd("ffn2_norm", h2, w["norm_ffn2"])
    carry, names = stages((d2d, 7))
    (a2, da_dgate2, da_dup2), landed = _ffn_gateup("ffn2", n2, full["w2_gate"], full["w2_up"], carry)
    gathered(names, landed)
    h3 = _ffn_down("ffn2", a2, full["w2_down"], h2)
    ffn2 = (n2, a2, da_dgate2, da_dup2)
    npl = _rms_fwd("ple_norm", h3, w["norm_ple"])
    dgq, dpp, dh4, g_norm_final, loss_part = _ple_head(npl, full["w_ple_gate"], h3, p, full["w_ple_proj"], tgt,
                                                       row(w["norm_final"]))
    reducer = _Reducer(shard_shape, c_arr, qc_arr)
    (g_w_ple_proj,) = _mm_tn("ple_dwproj", p, [dpp], BF16, tm=256, tn=1024, tk=4096)
    (g_w_ple_gate,) = _mm_tn("ple_dwgate", npl, [dgq], BF16, tm=512, tn=1024, tk=4096)
    ple = reducer.swap(["w_ple_gate", "w_ple_proj"], [g_w_ple_gate, g_w_ple_proj])
    dnpl, swapped = _mm_nt_sum("ple_dnorm_in", [dgq], [full["w_ple_gate"]], BF16, tm=512, tn=1024, tk=2048,
                               carry=ple.carry)
    ple = reducer.send(ple, swapped)
    dh3, dh3_b, g_norm_ple = _rms_bwd("ple_dnorm", h3, w["norm_ple"], [dnpl], dh4, 0.5)

    dh2, dh2_b, g_norm_ffn2, got, _, up2_w = _ffn_bwd(
        "ffn2", ["w2_gate", "w2_up", "w2_down"], h2, w["norm_ffn2"], full["w2_gate"], full["w2_up"], full["w2_down"],
        ffn2, dh3, dh3_b, 1.0, reducer, riding=ple.carry, last_hop_later=True)
    reducer.end(ple, got)

    dycat = _mm_nt_sum("out_dproj", [dh2_b], [full["w_out"]], BF16, tm=512, tn=1024, tk=2048)
    (g_w_out,) = _mm_tn("out_dw", ycat, [dh2_b], BF16, tm=512, tn=1024, tk=4096)

    dz, g_norm_gmlp_out, g_gmlp_norm_v, g_wm, g_s = _gmlp_bwd(z, dycat, w["gmlp_norm_v"], wm, bias,
                                                                  w["norm_gmlp_out"])
    g_gmlp_w_s = jnp.where(tril[None], g_wm, 0.0)
    g_gmlp_b_s = g_s.reshape(CHUNK, -1, GMLP_HEAD).sum(axis=-1).T

    def glu_bwd(rows, pars):
        dyn, ypre, qv = rows
        ygv = _gelu(ypre)
        sg = _sigmoid(qv)
        dy, dg = _rms_backward(ygv * sg, pars[0], dyn.astype(F32))
        return [dy * ygv * sg * (1.0 - sg), dy * sg], [dg]

    dq, dyg_part, g_norm_ssm_out = _rowwise("ssm_dglu", glu_bwd, [(dycat, d_ssm, 0), y_pre, q], [w["norm_ssm_out"]],
                                            [(d_ssm, BF16), (d_ssm, F32)], [(1, d_ssm)])
    (g_ssm_w_glu,) = _mm_tn("glu_dw", yg, [dq], BF16, tm=512, tn=1024, tk=4096)

    def gelu_bwd(accs, extras):
        return [(accs[0] + extras[0]) * _gelu_grad(extras[1])]

    (dy_pre,) = _matmul("glu_dproj", "nt", [dq], [full["ssm_w_glu"]], [(0, 0, 0)], [F32], gelu_bwd,
                        extras=(dyg_part, y_pre), tm=1024, tn=1024, tk=1024)
    mixers = reducer.swap(["w_out", "ssm_w_glu"], [g_w_out, g_ssm_w_glu])
    me = 4 * xi + 2 * yi + ci
    small = {"gmlp_norm_v": g_gmlp_norm_v, "gmlp_w_s": g_gmlp_w_s, "gmlp_b_s": g_gmlp_b_s,
             "norm_gmlp_out": g_norm_gmlp_out, "norm_ssm_out": g_norm_ssm_out, "norm_ffn2": g_norm_ffn2,
             "norm_ple": g_norm_ple, "norm_final": g_norm_final}
    before_scan = [n for n in SMALL if n in small]
    scan_blocks = _place_block(_pack([small[n] for n in before_scan] + [loss_part[:, :1]]), me)
    carry, split = _ride(up2_w.carry, mixers.carry, _exchange_carry(scan_blocks))
    (dz, g_wb, g_wc, g_abar, g_ssm_d), results = _ssm_scan_bwd(z, dy_pre, states, wb, wc, abar_conj, w["ssm_d"],
                                                              dz, carry)
    got, swapped, (scan_blocks,) = split(results)
    reducer.end(up2_w, got)
    mixers = reducer.send(mixers, swapped)
    g_abar = jnp.transpose(g_abar, (1, 0, 2)).reshape(nb, -1)
    sw = g_abar.shape[-1] // 2
    g_bbar_r = _blockdiag_in_grad(g_wb[..., :sw], SSM_STATE, SSM_GROUP)
    g_bbar_i = _blockdiag_in_grad(g_wb[..., sw:], SSM_STATE, SSM_GROUP)
    g_ssm_c_re = _blockdiag_out_grad(g_wc[:, :sw, :], SSM_GROUP, SSM_STATE)
    g_ssm_c_im = -_blockdiag_out_grad(g_wc[:, sw:, :], SSM_GROUP, SSM_STATE)
    g_abar_r = g_abar[..., :sw].reshape(n_groups, SSM_STATE)
    g_abar_i = g_abar[..., sw:].reshape(n_groups, SSM_STATE)
    g_ssm_log_dt, g_ssm_a_re, g_ssm_a_im, g_ssm_b_re, g_ssm_b_im = disc_vjp((g_abar_r, g_abar_i, g_bbar_r, g_bbar_i))

    (g_w_in,), got = _mm_tn("in_dw", nm, [dz], BF16, tm=512, tn=1536, tk=4096, carry=mixers.carry)
    reducer.end(mixers, got)
    in_w = reducer.swap(["w_in"], [g_w_in])
    dnm, swapped = _mm_nt_sum("in_dproj", [dz], [full["w_in"]], BF16, tm=1024, tn=1024, tk=3072, carry=in_w.carry)
    in_w = reducer.send(in_w, swapped)
    dh1, dh1_b, g_norm_mix = _rms_bwd("mix_dnorm", h1, w["norm_mix"], [dnm], dh2, 0.5)
    small = {"norm_mix": g_norm_mix, "ssm_log_dt": g_ssm_log_dt, "ssm_a_re": g_ssm_a_re,
             "ssm_a_im": g_ssm_a_im, "ssm_b_re": g_ssm_b_re, "ssm_b_im": g_ssm_b_im, "ssm_c_re": g_ssm_c_re,
             "ssm_c_im": g_ssm_c_im, "ssm_d": g_ssm_d}
    after_scan = [n for n in SMALL if n in small]
    ffn_blocks = _place_block(_pack([small[n] for n in after_scan]), me)
    done = [n for n in BIG if n in reducer.halves]
    carry, split = _ride(in_w.carry, _share_carry([reducer.halves[n] for n in done], [shard_shape[n] for n in done]))
    dx, _, g_norm_ffn1, results, (ffn_blocks,), _ = _ffn_bwd(
        "ffn1", ["w1_gate", "w1_up", "w1_down"], x, w["norm_ffn1"], full["w1_gate"], full["w1_up"], full["w1_down"],
        ffn1, dh1, dh1_b, 1.0, reducer, riding=carry, riding_dwd=_exchange_carry(ffn_blocks))
    got, shared = split(results)
    grad = dict(zip(done, shared))
    reducer.end(in_w, got)
    scan_grads = _unpack(_sum_blocks("sum_before_scan", scan_blocks), [w[n].shape for n in before_scan] + [(1,)])
    loss = scan_grads[-1].reshape(())
    early = before_scan + after_scan
    early_grads = scan_grads[:-1] + _unpack(_sum_blocks("sum_after_scan", ffn_blocks), [w[n].shape for n in after_scan])

    rest = [n for n in BIG if n not in done]
    late_blocks = _place_block(_pack([g_norm_ffn1]), me)
    last = _join(_share_carry([reducer.halves[n] for n in rest], [shard_shape[n] for n in rest]),
                 _exchange_carry(late_blocks))
    *shared, late_blocks = _comm_call("share_halves", last)
    grad.update(zip(rest, shared))
    grad.update(zip(early, early_grads))
    grad["norm_ffn1"] = _unpack(_sum_blocks("sum_first_norm", late_blocks), [w["norm_ffn1"].shape])[0]

    small_shapes = [w[n].shape for n in SMALL]
    delta, new_m, new_v = {}, {}, {}
    for n in BIG:
        delta[n], new_m[n], new_v[n], grad[n] = _adamw("adamw_" + n, w[n], grad[n], m[n], v[n])
    d_p, m_p, v_p, _ = _adamw("adamw_small", _pack([w[n] for n in SMALL]), _pack([grad[n] for n in SMALL]),
                              _pack([m[n] for n in SMALL]), _pack([v[n] for n in SMALL]))
    for name_list, packed in ((delta, d_p), (new_m, m_p), (new_v, v_p)):
        for n, a in zip(SMALL, _unpack(packed, small_shapes)):
            name_list[n] = a
    return loss, dx, grad, delta, new_m, new_v


def kernel(x, p, norm_ffn1, w1_gate, w1_up, w1_down, norm_mix, w_in, ssm_log_dt, ssm_a_re, ssm_a_im, ssm_b_re, ssm_b_im, ssm_c_re, ssm_c_im, ssm_d, ssm_w_glu, gmlp_norm_v, gmlp_w_s, gmlp_b_s, norm_ssm_out, norm_gmlp_out, w_out, norm_ffn2, w2_gate, w2_up, w2_down, norm_ple, w_ple_gate, w_ple_proj, norm_final, loss_target, m_norm_ffn1, m_w1_gate, m_w1_up, m_w1_down, m_norm_mix, m_w_in, m_ssm_log_dt, m_ssm_a_re, m_ssm_a_im, m_ssm_b_re, m_ssm_b_im, m_ssm_c_re, m_ssm_c_im, m_ssm_d, m_ssm_w_glu, m_gmlp_norm_v, m_gmlp_w_s, m_gmlp_b_s, m_norm_ssm_out, m_norm_gmlp_out, m_w_out, m_norm_ffn2, m_w2_gate, m_w2_up, m_w2_down, m_norm_ple, m_w_ple_gate, m_w_ple_proj, m_norm_final, v_norm_ffn1, v_w1_gate, v_w1_up, v_w1_down, v_norm_mix, v_w_in, v_ssm_log_dt, v_ssm_a_re, v_ssm_a_im, v_ssm_b_re, v_ssm_b_im, v_ssm_c_re, v_ssm_c_im, v_ssm_d, v_ssm_w_glu, v_gmlp_norm_v, v_gmlp_w_s, v_gmlp_b_s, v_norm_ssm_out, v_norm_gmlp_out, v_w_out, v_norm_ffn2, v_w2_gate, v_w2_up, v_w2_down, v_norm_ple, v_w_ple_gate, v_w_ple_proj, v_norm_final):
    given = dict(locals())
    shapes = {n: given[n].shape for n in WEIGHTS}

    def block(name):
        a = given[name]
        if a.ndim == 1:
            return a.reshape(1, -1)
        return a[0] if a.ndim >= 3 else a

    w = {n: block(n) for n in WEIGHTS}
    m = {n: block("m_" + n) for n in WEIGHTS}
    v = {n: block("v_" + n) for n in WEIGHTS}
    loss, dx, grad, delta, new_m, new_v = _step(x[0], p[0, 0], loss_target[0], w, m, v)
    outs = [loss, dx[None]]
    for tree in (grad, delta, new_m, new_v):
        outs += [tree[n].reshape(shapes[n]) for n in WEIGHTS]
    return tuple(outs)
```

```python
import functools
import math

import jax
import jax.numpy as jnp
from jax import lax
from jax.experimental import pallas as pl
from jax.experimental.pallas import tpu as pltpu

F32 = jnp.float32
BF16 = jnp.bfloat16
EPS = 1e-6
SSM_GROUP = 16
SSM_STATE = 64
GROUPS_PER_BLOCK = 8
GMLP_HEAD = 128
CHUNK = 128
ADAM_LR = 0.001
ADAM_B1 = 0.9
ADAM_B2 = 0.999
ADAM_EPS = 1e-08
ADAM_WD = 0.01
ADAM_STEP = 10
N_CHIPS = 4
N_DEV = 8
LANES = 128
VMEM_LIMIT_BYTES = 56 * 1024 * 1024
MESH = pl.DeviceIdType.MESH
GELU_C = math.sqrt(2.0 / math.pi)
GELU_A = 0.044715

_DOT_DIMS = {
    "nn": (((1,), (0,)), ((), ())),
    "nt": (((1,), (1,)), ((), ())),
    "tn": (((0,), (0,)), ((), ())),
}


def _tile(dim, pref, align):
    if dim <= pref:
        return dim
    t = (pref // align) * align
    while t >= align:
        if dim % t == 0:
            return t
        t -= align
    return dim


def _params(semantics):
    return pltpu.CompilerParams(dimension_semantics=semantics, vmem_limit_bytes=VMEM_LIMIT_BYTES)


def _gelu(x):
    return 0.5 * x * (1.0 + jnp.tanh(GELU_C * (x + GELU_A * x * x * x)))


def _gelu_grad(x):
    t = jnp.tanh(GELU_C * (x + GELU_A * x * x * x))
    return 0.5 * (1.0 + t) + 0.5 * x * (1.0 - t * t) * GELU_C * (1.0 + 3.0 * GELU_A * x * x)


def _sigmoid(x):
    return 1.0 / (1.0 + jnp.exp(-x))


def _dot(a, b, mode):
    return lax.dot_general(a.astype(BF16), b.astype(BF16), _DOT_DIMS[mode], preferred_element_type=F32)


class _Carry:
    def __init__(self, arrays, out_shapes, aliases, n_copies, copies):
        self.arrays = list(arrays)
        self.out_shapes = list(out_shapes)
        self.aliases = dict(aliases)
        self.n_copies = n_copies
        self.copies = copies

    def scratch(self):
        return [pltpu.SemaphoreType.DMA((self.n_copies,)), pltpu.SemaphoreType.DMA((self.n_copies,))]

    def split(self, refs):
        n_in, n_out = len(self.arrays), len(self.out_shapes)
        return refs[:n_in], refs[n_in:n_in + n_out], refs[n_in + n_out], refs[n_in + n_out + 1]

    def start(self, refs):
        for cp in self.copies(*self.split(refs)):
            cp.start()

    def wait(self, refs):
        for cp in self.copies(*self.split(refs)):
            cp.wait()


class _SemRange:
    def __init__(self, sems, offset):
        self.sems, self.offset = sems, offset

    @property
    def at(self):
        return self

    def __getitem__(self, k):
        return self.sems.at[self.offset + k]


def _join(first, second):
    n_in, n_out = len(first.arrays), len(first.out_shapes)
    aliases = dict(first.aliases)
    aliases.update({n_in + i: n_out + o for i, o in second.aliases.items()})

    def copies(ops, res, send_sems, recv_sems):
        return (first.copies(ops[:n_in], res[:n_out], send_sems, recv_sems)
                + second.copies(ops[n_in:], res[n_out:], _SemRange(send_sems, first.n_copies),
                                _SemRange(recv_sems, first.n_copies)))

    return _Carry(first.arrays + second.arrays, first.out_shapes + second.out_shapes, aliases,
                  first.n_copies + second.n_copies, copies)


_ANY = pl.BlockSpec(memory_space=pl.ANY)


def _comm_call(name, carry):
    def body(*refs):
        carry.start(refs)
        carry.wait(refs)

    n_in = len(carry.arrays)
    return pl.pallas_call(
        body,
        name=name,
        in_specs=[_ANY] * n_in,
        out_specs=[_ANY] * len(carry.out_shapes),
        out_shape=carry.out_shapes,
        input_output_aliases=carry.aliases,
        scratch_shapes=carry.scratch(),
    )(*carry.arrays)


def _carried_call(body, carry, *, name, grid, in_specs, out_specs, out_shape, scratch_shapes, semantics, args,
                  aliases=None):
    aliases = dict(aliases or {})
    if carry is None:
        res = pl.pallas_call(body, name=name, grid=grid, in_specs=in_specs, out_specs=out_specs, out_shape=out_shape,
                             scratch_shapes=scratch_shapes, input_output_aliases=aliases,
                             compiler_params=_params(semantics))(*args)
        return res, []
    n_in, n_out, n_scr = len(in_specs), len(out_specs), len(scratch_shapes)
    nci, nco = len(carry.arrays), len(carry.out_shapes)

    def wrapped(*refs):
        ins = refs[:n_in]
        outs = refs[n_in + nci:n_in + nci + n_out]
        scr = refs[n_in + nci + n_out + nco:n_in + nci + n_out + nco + n_scr]
        c_refs = (refs[n_in:n_in + nci] + refs[n_in + nci + n_out:n_in + nci + n_out + nco]
                  + refs[n_in + nci + n_out + nco + n_scr:])
        first = functools.reduce(jnp.logical_and, [pl.program_id(d) == 0 for d in range(len(grid))])
        last = functools.reduce(jnp.logical_and, [pl.program_id(d) == grid[d] - 1 for d in range(len(grid))])

        @pl.when(first)
        def _():
            carry.start(c_refs)

        body(*ins, *outs, *scr)

        @pl.when(last)
        def _():
            carry.wait(c_refs)

    res = pl.pallas_call(
        wrapped,
        name=name,
        grid=grid,
        in_specs=list(in_specs) + [_ANY] * nci,
        out_specs=list(out_specs) + [_ANY] * nco,
        out_shape=list(out_shape) + carry.out_shapes,
        input_output_aliases={**aliases, **{n_in + i: n_out + o for i, o in carry.aliases.items()}},
        scratch_shapes=list(scratch_shapes) + carry.scratch(),
        compiler_params=_params(("arbitrary",) * len(grid)),
    )(*args, *carry.arrays)
    return res[:n_out], res[n_out:]


def _matmul(name, mode, a_list, b_list, products, out_dtypes, epilogue, extras=(), tm=512, tn=512, tk=2048,
            carry=None, n_part=(0, 1)):
    a0, b0 = a_list[0], b_list[0]
    if mode == "tn":
        k_dim, m_dim = a0.shape
    else:
        m_dim, k_dim = a0.shape
    n_dim = (b0.shape[0] if mode == "nt" else b0.shape[1]) // n_part[1]
    tm = _tile(m_dim, tm, LANES)
    tn = _tile(n_dim, tn, LANES)
    tk = _tile(k_dim, tk, LANES)
    nk = k_dim // tk
    j0 = n_part[0] * (n_dim // tn)
    chunk = 2 * LANES if (nk == 1 and epilogue is not _identity and tn % (2 * LANES) == 0) else tn
    n_acc = 1 + max(p[2] for p in products)
    na, nb, ne, no = len(a_list), len(b_list), len(extras), len(out_dtypes)

    if mode == "tn":
        a_spec = pl.BlockSpec((tk, tm), lambda i, j, k: (k, i))
    else:
        a_spec = pl.BlockSpec((tm, tk), lambda i, j, k: (i, k))
    if mode == "nt":
        b_spec = pl.BlockSpec((tn, tk), lambda i, j, k: (j0 + j, k))
    else:
        b_spec = pl.BlockSpec((tk, tn), lambda i, j, k: (k, j0 + j))
    t_spec = pl.BlockSpec((tm, tn), lambda i, j, k: (i, j))

    def body(*refs):
        a_refs = refs[:na]
        b_refs = refs[na:na + nb]
        e_refs = refs[na + nb:na + nb + ne]
        o_refs = refs[na + nb + ne:na + nb + ne + no]
        acc_refs = refs[na + nb + ne + no:]

        def partial_sums(cols):
            sums = [None] * n_acc
            for ai, bi, ci in products:
                b = b_refs[bi][cols, :] if mode == "nt" else b_refs[bi][:, cols]
                d = _dot(a_refs[ai][...], b, mode)
                sums[ci] = d if sums[ci] is None else sums[ci] + d
            return sums

        def finish(accs, cols):
            outs = epilogue(accs, [e[:, cols] for e in e_refs])
            for o_ref, o in zip(o_refs, outs):
                o_ref[:, cols] = o.astype(o_ref.dtype)

        if nk == 1:
            for c0 in range(0, tn, chunk):
                finish(partial_sums(slice(c0, c0 + chunk)), slice(c0, c0 + chunk))
        else:
            sums = partial_sums(slice(None))
            finish = functools.partial(finish, cols=slice(None))
            k = pl.program_id(2)

            @pl.when(k == 0)
            def _():
                for acc, s in zip(acc_refs, sums):
                    acc[...] = s

            @pl.when(k > 0)
            def _():
                for acc, s in zip(acc_refs, sums):
                    acc[...] += s

            @pl.when(k == nk - 1)
            def _():
                finish([acc[...] for acc in acc_refs])

    scratch = [pltpu.VMEM((tm, tn), F32) for _ in range(n_acc)] if nk > 1 else []
    outs, carried = _carried_call(
        body, carry,
        name=name,
        grid=(m_dim // tm, n_dim // tn, nk),
        in_specs=[a_spec] * na + [b_spec] * nb + [t_spec] * ne,
        out_specs=[t_spec] * no,
        out_shape=[jax.ShapeDtypeStruct((m_dim, n_dim), dt) for dt in out_dtypes],
        scratch_shapes=scratch,
        semantics=("parallel", "parallel", "arbitrary"),
        args=[*a_list, *b_list, *extras],
    )
    return (outs, carried) if carry else outs


def _identity(accs, extras):
    return accs


def _single(result, carry):
    return (result[0][0], result[1]) if carry else result[0]


def _mm_nn(name, a, b, out_dtype, res=None, alpha=1.0, carry=None, **tiles):
    if res is None:
        return _single(_matmul(name, "nn", [a], [b], [(0, 0, 0)], [out_dtype], _identity, carry=carry, **tiles), carry)

    def epilogue(accs, extras):
        return [extras[0] + alpha * accs[0]]

    return _single(_matmul(name, "nn", [a], [b], [(0, 0, 0)], [out_dtype], epilogue, extras=(res,), carry=carry,
                           **tiles), carry)


def _mm_nt_sum(name, a_list, b_list, out_dtype, carry=None, n_part=(0, 1), **tiles):
    products = [(i, i, 0) for i in range(len(a_list))]
    return _single(_matmul(name, "nt", a_list, b_list, products, [out_dtype], _identity, carry=carry, n_part=n_part,
                           **tiles), carry)


def _mm_tn(name, a, b_list, out_dtype, carry=None, **tiles):
    products = [(0, i, i) for i in range(len(b_list))]
    return _matmul(name, "tn", [a], b_list, products, [out_dtype] * len(b_list), _identity, carry=carry, **tiles)


def _rowwise(name, fn, row_ins, par_ins, row_outs, acc_outs=(), tr=512, carry=None):
    first = row_ins[0][0] if isinstance(row_ins[0], tuple) else row_ins[0]
    t_dim = first.shape[0]
    tr = _tile(t_dim, tr, 16)
    arrays, specs = [], []
    for r in row_ins:
        if isinstance(r, tuple):
            arr, width, blk = r
            specs.append(pl.BlockSpec((tr, width), lambda i, blk=blk: (i, blk)))
        else:
            arr = r
            specs.append(pl.BlockSpec((tr, arr.shape[1]), lambda i: (i, 0)))
        arrays.append(arr)
    for p in par_ins:
        arrays.append(p)
        specs.append(pl.BlockSpec(p.shape, lambda i, nd=p.ndim: (0,) * nd))
    nr, npar, nro, nacc = len(row_ins), len(par_ins), len(row_outs), len(acc_outs)

    def body(*refs):
        rows = [r[...] for r in refs[:nr]]
        pars = [p[...] for p in refs[nr:nr + npar]]
        o_refs = refs[nr + npar:nr + npar + nro]
        acc_refs = refs[nr + npar + nro:]
        outs, accs = fn(rows, pars)
        for o_ref, o in zip(o_refs, outs):
            o_ref[...] = o.astype(o_ref.dtype)
        if nacc:
            @pl.when(pl.program_id(0) == 0)
            def _():
                for a_ref in acc_refs:
                    a_ref[...] = jnp.zeros_like(a_ref)

            for a_ref, a in zip(acc_refs, accs):
                a_ref[...] += a

    out_shape = [jax.ShapeDtypeStruct((t_dim, c), dt) for c, dt in row_outs]
    out_shape += [jax.ShapeDtypeStruct(s, F32) for s in acc_outs]
    out_specs = [pl.BlockSpec((tr, c), lambda i: (i, 0)) for c, _ in row_outs]
    out_specs += [pl.BlockSpec(s, lambda i: (0, 0)) for s in acc_outs]
    res, carried = _carried_call(body, carry, name=name, grid=(t_dim // tr,), in_specs=specs, out_specs=out_specs,
                                 out_shape=out_shape, scratch_shapes=[], semantics=("arbitrary",), args=arrays)
    return (res, carried) if carry else res


def _rms_stats(x):
    r = lax.rsqrt(jnp.mean(x * x, axis=-1, keepdims=True) + EPS)
    return x * r, r


def _rms_backward(x, g, dy):
    xh, r = _rms_stats(x)
    a = dy * g
    dx = r * (a - xh * jnp.mean(a * xh, axis=-1, keepdims=True))
    return dx, jnp.sum(dy * xh, axis=0, keepdims=True)


def _rms_fwd(name, x, g, carry=None):
    def fn(rows, pars):
        xh, _ = _rms_stats(rows[0])
        return [xh * pars[0]], []

    res = _rowwise(name, fn, [x], [g], [(x.shape[1], BF16)], carry=carry)
    return (res[0][0], res[1]) if carry else res[0]


def _rms_bwd(name, x, g, dy_parts, dres, scale):
    def fn(rows, pars):
        dy = (rows[2] if len(rows) == 3 else jnp.concatenate(rows[2:], axis=1)).astype(F32)
        dx, dg = _rms_backward(rows[0], pars[0], dy)
        tot = rows[1] + dx
        return [tot, scale * tot], [dg]

    d = x.shape[1]
    return _rowwise(name, fn, [x, dres, *dy_parts], [g], [(d, F32), (d, BF16)], [(1, d)], tr=256)


def _cast_into_gathered(name, w, kind, q_arr):
    rs, cs = w.shape
    tr = _tile(rs, 256, 16)
    nr = rs // tr

    def body(q_ref, w_ref, o_ref):
        o_ref[...] = w_ref[...].astype(BF16)

    if kind == "row":
        o_spec = pl.BlockSpec((tr, cs), lambda i, q_ref: (q_ref[0] * nr + i, 0))
    else:
        o_spec = pl.BlockSpec((tr, cs), lambda i, q_ref: (i, q_ref[0]))
    return pl.pallas_call(
        body,
        name=name,
        grid_spec=pltpu.PrefetchScalarGridSpec(num_scalar_prefetch=1, grid=(nr,),
                                               in_specs=[pl.BlockSpec((tr, cs), lambda i, q_ref: (i, 0))],
                                               out_specs=o_spec),
        out_shape=jax.ShapeDtypeStruct(_full_shape(kind, (rs, cs)), BF16),
        compiler_params=_params(("parallel",)),
    )(q_arr, w)


def _swiglu_tiles(gate, up):
    s = _sigmoid(gate)
    silu = gate * s
    return [silu * up, up * (s * (1.0 + gate * (1.0 - s))), silu]


def _ffn_gateup(tag, n, wg, wu, carry):
    def act(accs, extras):
        return _swiglu_tiles(accs[0], accs[1])

    return _matmul(tag + "_gateup", "nn", [n], [wg, wu], [(0, 0, 0), (0, 1, 1)], [BF16] * 3, act,
                   tm=1024, tn=512, tk=2048, carry=carry)


def _ffn_up(tag, n, wu, gate, carry):
    def act(accs, extras):
        return _swiglu_tiles(extras[0].astype(F32), accs[0])

    return _matmul(tag + "_up", "nn", [n], [wu], [(0, 0, 0)], [BF16] * 3, act, extras=(gate,),
                   tm=1024, tn=512, tk=2048, carry=carry)


def _ffn_down(tag, a, wd, h, carry=None):
    return _mm_nn(tag + "_down", a, wd, F32, res=h, alpha=0.5, tm=1024, tn=512, tk=5632, carry=carry)


def _ffn_bwd(tag, names, h, g, wg, wu, wd, saved, dh, dfb, next_scale, reducer, riding=None, riding_dwd=None,
             last_hop_later=False):
    n, a, da_dgate, da_dup = saved

    def act_bwd(accs, extras):
        return [accs[0] * extras[0].astype(F32), accs[0] * extras[1].astype(F32)]

    dact = _matmul(tag + "_dact", "nt", [dfb], [wd], [(0, 0, 0)], [BF16, BF16], act_bwd, extras=(da_dgate, da_dup),
                   tm=1024, tn=512, tk=2048, carry=riding)
    (dgp, du), rode = dact if riding else (dact, [])
    dwd_call = _mm_tn(tag + "_dwd", a, [dfb], BF16, tm=512, tn=2048, tk=4096, carry=riding_dwd)
    (dwd,), rode_dwd = dwd_call if riding_dwd else (dwd_call, [])
    down = reducer.swap(names[2:], [dwd])
    (dwg,), swapped = _mm_tn(tag + "_dwg", n, [dgp], BF16, tm=512, tn=1408, tk=4096, carry=down.carry)
    down = reducer.send(down, swapped)
    gate_w = reducer.swap(names[:1], [dwg])
    carry, split = _ride(down.carry, gate_w.carry)
    (dwu,), results = _mm_tn(tag + "_dwu", n, [du], BF16, tm=512, tn=1408, tk=4096, carry=carry)
    got, swapped = split(results)
    reducer.end(down, got)
    gate_w = reducer.send(gate_w, swapped)
    up_w = reducer.swap(names[1:2], [dwu])
    carry, split = _ride(gate_w.carry, up_w.carry)
    halves = 1 if last_hop_later else 2
    tiles = dict(tm=512, tn=512, tk=5632) if last_hop_later else dict(tm=1024, tn=1024, tk=1408)
    dn_lo, results = _mm_nt_sum(tag + "_dn_lo", [dgp, du], [wg, wu], BF16, carry=carry, n_part=(0, halves), **tiles)
    got, swapped = split(results)
    reducer.end(gate_w, got)
    up_w = reducer.send(up_w, swapped)
    dn = [dn_lo]
    if not last_hop_later:
        dn_hi, got = _mm_nt_sum(tag + "_dn_hi", [dgp, du], [wg, wu], BF16, tm=1024, tn=1024, tk=1408,
                                carry=up_w.carry, n_part=(1, 2))
        reducer.end(up_w, got)
        dn.append(dn_hi)
    dh_in, dh_in_b, dg = _rms_bwd(tag + "_dnorm", h, g, dn, dh, next_scale)
    return dh_in, dh_in_b, dg, rode, rode_dwd, (up_w if last_hop_later else None)


def _ssm_discretize(log_dt, a_re, a_im, b_re, b_im):
    dt = jnp.exp(log_dt)[:, None]
    lr = jnp.minimum(a_re, -1e-4)
    li = a_im
    mag = jnp.exp(lr * dt)
    ang = li * dt
    abar_r = mag * jnp.cos(ang)
    abar_i = mag * jnp.sin(ang)
    den = lr * lr + li * li
    xr = abar_r - 1.0
    xi = abar_i
    zr = (xr * lr + xi * li) / den
    zi = (xi * lr - xr * li) / den
    bbar_r = zr[..., None] * b_re - zi[..., None] * b_im
    bbar_i = zr[..., None] * b_im + zi[..., None] * b_re
    return abar_r, abar_i, bbar_r, bbar_i


def _blockdiag_in(b):
    g, n, p = b.shape
    nb = g // GROUPS_PER_BLOCK
    eye = jnp.eye(GROUPS_PER_BLOCK, dtype=b.dtype)
    b4 = b.reshape(nb, GROUPS_PER_BLOCK, n, p)
    return jnp.einsum("sgnp,gh->sgphn", b4, eye).reshape(nb, GROUPS_PER_BLOCK * p, GROUPS_PER_BLOCK * n)


def _blockdiag_in_grad(gw, n, p):
    nb = gw.shape[0]
    eye = jnp.eye(GROUPS_PER_BLOCK, dtype=gw.dtype)
    g5 = gw.reshape(nb, GROUPS_PER_BLOCK, p, GROUPS_PER_BLOCK, n)
    return jnp.einsum("sgphn,gh->sgnp", g5, eye).reshape(nb * GROUPS_PER_BLOCK, n, p)


def _blockdiag_out(c):
    g, p, n = c.shape
    nb = g // GROUPS_PER_BLOCK
    eye = jnp.eye(GROUPS_PER_BLOCK, dtype=c.dtype)
    c4 = c.reshape(nb, GROUPS_PER_BLOCK, p, n)
    return jnp.einsum("sgpn,gh->shngp", c4, eye).reshape(nb, GROUPS_PER_BLOCK * n, GROUPS_PER_BLOCK * p)


def _blockdiag_out_grad(gw, p, n):
    nb = gw.shape[0]
    eye = jnp.eye(GROUPS_PER_BLOCK, dtype=gw.dtype)
    g5 = gw.reshape(nb, GROUPS_PER_BLOCK, n, GROUPS_PER_BLOCK, p)
    return jnp.einsum("shngp,gh->sgpn", g5, eye).reshape(nb * GROUPS_PER_BLOCK, p, n)


def _ssm_scan_fwd(z, wb, wc, abar, d, carry=None):
    t_dim = z.shape[0]
    nb, cb, sw2 = wb.shape
    nl = sw2 // LANES
    hl = nl // 2
    tt = _tile(t_dim, 256, 8)
    nt = t_dim // tt

    def body(z_ref, wb_ref, wc_ref, a_ref, d_ref, s_ref, y_ref, yg_ref, drive_ref, st_ref):
        @pl.when(pl.program_id(0) == 0)
        def _():
            st_ref[...] = jnp.zeros_like(st_ref)

        u = z_ref[...]
        ub = u.astype(BF16)
        for b in range(nb):
            drive = _dot(ub[:, b * cb:(b + 1) * cb], wb_ref[b], "nn")
            for l in range(nl):
                drive_ref[l, pl.ds(b, tt, stride=nb), :] = drive[:, l * LANES:(l + 1) * LANES]
        a = a_ref[...]
        chunk = lambda v, l: v[:, l * LANES:(l + 1) * LANES]

        def step(t, state):
            rows = pl.ds(pl.multiple_of(t * nb, nb), nb)
            re, im = [], []
            for l in range(hl):
                ar, ai, sr, si = chunk(a, l), chunk(a, hl + l), state[l], state[hl + l]
                nr = ar * sr - ai * si + drive_ref[l, rows, :]
                ni = ar * si + ai * sr + drive_ref[hl + l, rows, :]
                s_ref[l, rows, :] = nr
                s_ref[hl + l, rows, :] = ni
                re.append(nr)
                im.append(ni)
            return tuple(re + im)

        state = lax.fori_loop(0, tt, step, tuple(st_ref[l] for l in range(nl)), unroll=8)
        for l in range(nl):
            st_ref[l] = state[l]
        parts = []
        for b in range(nb):
            s_b = jnp.concatenate([s_ref[l, pl.ds(b, tt, stride=nb), :] for l in range(nl)], axis=1)
            parts.append(_dot(s_b, wc_ref[b], "nn"))
        y = jnp.concatenate(parts, axis=1) + d_ref[...] * u
        y_ref[...] = y
        yg_ref[...] = _gelu(y).astype(BF16)

    full = lambda a: pl.BlockSpec(a.shape, lambda t, nd=a.ndim: (0,) * nd)
    return _carried_call(
        body, carry,
        name="ssm_scan_fwd",
        grid=(nt,),
        in_specs=[pl.BlockSpec((tt, nb * cb), lambda t: (t, 0)), full(wb), full(wc), full(abar), full(d)],
        out_specs=[
            pl.BlockSpec((nl, tt * nb, LANES), lambda t: (0, t, 0)),
            pl.BlockSpec((tt, nb * cb), lambda t: (t, 0)),
            pl.BlockSpec((tt, nb * cb), lambda t: (t, 0)),
        ],
        out_shape=[
            jax.ShapeDtypeStruct((nl, t_dim * nb, LANES), F32),
            jax.ShapeDtypeStruct((t_dim, nb * cb), F32),
            jax.ShapeDtypeStruct((t_dim, nb * cb), BF16),
        ],
        scratch_shapes=[pltpu.VMEM((nl, tt * nb, LANES), F32), pltpu.VMEM((nl, nb, LANES), F32)],
        semantics=("arbitrary",),
        args=[z, wb, wc, abar, d],
    )


def _ssm_scan_bwd(z, dy, states, wb, wc, abar_conj, d, dz_all, carry=None):
    t_dim = z.shape[0]
    nb, cb, sw2 = wb.shape
    nl = sw2 // LANES
    hl = nl // 2
    tt = _tile(t_dim, 256, 8)
    nt = t_dim // tt
    edges = states.reshape(nl, nt, tt * nb, LANES)[:, :, (tt - 1) * nb:, :]
    before = jnp.concatenate([jnp.zeros((nl, 1, nb, LANES), F32), edges[:, :-1]], axis=1).reshape(nl, nt * nb, LANES)

    def body(z_ref, dy_ref, s_ref, sp_ref, wb_ref, wc_ref, a_ref, d_ref, dz_all_ref,
             dz_ref, gwb_ref, gwc_ref, ga_ref, gd_ref, gin_ref, gs_ref, st_ref):
        @pl.when(pl.program_id(0) == 0)
        def _():
            st_ref[...] = jnp.zeros_like(st_ref)
            gwb_ref[...] = jnp.zeros_like(gwb_ref)
            gwc_ref[...] = jnp.zeros_like(gwc_ref)
            ga_ref[...] = jnp.zeros_like(ga_ref)
            gd_ref[...] = jnp.zeros_like(gd_ref)

        u = z_ref[...]
        dyv = dy_ref[...]
        ub = u.astype(BF16)
        dyb = dyv.astype(BF16)
        for b in range(nb):
            gin = _dot(dyb[:, b * cb:(b + 1) * cb], wc_ref[b], "nt")
            for l in range(nl):
                gin_ref[l, pl.ds(b, tt, stride=nb), :] = gin[:, l * LANES:(l + 1) * LANES]
        a = a_ref[...]
        chunk = lambda v, l: v[:, l * LANES:(l + 1) * LANES]

        def step(k, state):
            rows = pl.ds(pl.multiple_of((tt - 1 - k) * nb, nb), nb)
            re, im = [], []
            for l in range(hl):
                ar, ai, gr, gi = chunk(a, l), chunk(a, hl + l), state[l], state[hl + l]
                nr = ar * gr - ai * gi + gin_ref[l, rows, :]
                ni = ar * gi + ai * gr + gin_ref[hl + l, rows, :]
                gs_ref[l, rows, :] = nr
                gs_ref[hl + l, rows, :] = ni
                re.append(nr)
                im.append(ni)
            return tuple(re + im)

        state = lax.fori_loop(0, tt, step, tuple(st_ref[l] for l in range(nl)), unroll=8)
        for l in range(nl):
            st_ref[l] = state[l]

        parts = []
        for b in range(nb):
            cols = slice(b * cb, (b + 1) * cb)
            gs_b = jnp.concatenate([gs_ref[l, pl.ds(b, tt, stride=nb), :] for l in range(nl)], axis=1)
            s_b = jnp.concatenate([s_ref[l, pl.ds(b, tt, stride=nb), :] for l in range(nl)], axis=1)
            parts.append(_dot(gs_b, wb_ref[b], "nt"))
            gwb_ref[b] += _dot(ub[:, cols], gs_b, "tn")
            gwc_ref[b] += _dot(s_b, dyb[:, cols], "tn")
        dz_ref[...] = (jnp.concatenate(parts, axis=1) + d_ref[...] * dyv).astype(BF16)
        gd_ref[...] += jnp.sum(dyv * u, axis=0, keepdims=True)

        row = lax.broadcasted_iota(jnp.int32, (tt * nb, LANES), 0)
        shifted = lambda v: jnp.where(row < nb, 0.0, pltpu.roll(v, nb, 0))
        over_time = lambda v: jnp.sum(v.reshape(tt, nb, LANES), axis=0)
        for l in range(hl):
            g_r, g_i = gs_ref[l], gs_ref[hl + l]
            p_r, p_i = shifted(s_ref[l]), shifted(s_ref[hl + l])
            f_r, f_i = sp_ref[l], sp_ref[hl + l]
            g0_r, g0_i = gs_ref[l, pl.ds(0, nb), :], gs_ref[hl + l, pl.ds(0, nb), :]
            ga_ref[l] += over_time(g_r * p_r + g_i * p_i) + g0_r * f_r + g0_i * f_i
            ga_ref[hl + l] += over_time(g_i * p_r - g_r * p_i) + g0_i * f_r - g0_r * f_i

    rev = lambda t: (nt - 1 - t, 0)
    rev3 = lambda t: (0, nt - 1 - t, 0)
    full = lambda a: pl.BlockSpec(a.shape, lambda t, nd=a.ndim: (0,) * nd)
    return _carried_call(
        body, carry,
        name="ssm_scan_bwd",
        grid=(nt,),
        in_specs=[
            pl.BlockSpec((tt, nb * cb), rev),
            pl.BlockSpec((tt, nb * cb), rev),
            pl.BlockSpec((nl, tt * nb, LANES), rev3),
            pl.BlockSpec((nl, nb, LANES), rev3),
            full(wb), full(wc), full(abar_conj), full(d), _ANY,
        ],
        out_specs=[
            pl.BlockSpec((tt, nb * cb), rev),
            pl.BlockSpec((nb, cb, sw2), lambda t: (0, 0, 0)),
            pl.BlockSpec((nb, sw2, cb), lambda t: (0, 0, 0)),
            pl.BlockSpec((nl, nb, LANES), lambda t: (0, 0, 0)),
            pl.BlockSpec((1, nb * cb), lambda t: (0, 0)),
        ],
        out_shape=[
            jax.ShapeDtypeStruct(dz_all.shape, BF16),
            jax.ShapeDtypeStruct((nb, cb, sw2), F32),
            jax.ShapeDtypeStruct((nb, sw2, cb), F32),
            jax.ShapeDtypeStruct((nl, nb, LANES), F32),
            jax.ShapeDtypeStruct((1, nb * cb), F32),
        ],
        scratch_shapes=[pltpu.VMEM((nl, tt * nb, LANES), F32), pltpu.VMEM((nl, tt * nb, LANES), F32),
                        pltpu.VMEM((nl, nb, LANES), F32)],
        semantics=("arbitrary",),
        args=[z, dy, states, before, wb, wc, abar_conj, d, dz_all],
        aliases={8: 0},
    )


def _gmlp_chunk(zu, zv, gv, wm_ref, bias, n_heads):
    ua = _gelu(zu)
    vg = _gelu(zv)
    xc = vg - jnp.mean(vg, axis=-1, keepdims=True)
    r = lax.rsqrt(jnp.mean(xc * xc, axis=-1, keepdims=True) + EPS)
    vh = xc * r
    vb = (vh * gv).astype(BF16)
    parts = []
    for h in range(n_heads):
        cols = slice(h * GMLP_HEAD, (h + 1) * GMLP_HEAD)
        parts.append(_dot(wm_ref[h], vb[:, cols], "nn"))
    s = jnp.concatenate(parts, axis=1) + bias
    return ua, vh, r, vb, s


def _gmlp_fwd(z, gv, wm, bias, ggo):
    t_dim = z.shape[0]
    dg = gv.shape[1]
    n_heads = dg // GMLP_HEAD
    tr = _tile(t_dim, 256, CHUNK)

    def body(zu_ref, zv_ref, gv_ref, wm_ref, b_ref, ggo_ref, o_ref):
        for ck in range(tr // CHUNK):
            rows = pl.ds(ck * CHUNK, CHUNK)
            ua, _, _, _, s = _gmlp_chunk(zu_ref[rows, :], zv_ref[rows, :], gv_ref[...], wm_ref, b_ref[...], n_heads)
            yh, _ = _rms_stats(ua * s)
            o_ref[rows, :] = (yh * ggo_ref[...]).astype(BF16)

    full = lambda a: pl.BlockSpec(a.shape, lambda i, nd=a.ndim: (0,) * nd)
    return pl.pallas_call(
        body,
        name="gmlp_fwd",
        grid=(t_dim // tr,),
        in_specs=[pl.BlockSpec((tr, dg), lambda i: (i, 1)), pl.BlockSpec((tr, dg), lambda i: (i, 2)),
                  full(gv), full(wm), full(bias), full(ggo)],
        out_specs=pl.BlockSpec((tr, dg), lambda i: (i, 0)),
        out_shape=jax.ShapeDtypeStruct((t_dim, dg), BF16),
        compiler_params=_params(("parallel",)),
    )(z, z, gv, wm, bias, ggo)


def _gmlp_bwd(z, dycat, gv, wm, bias, ggo):
    t_dim = z.shape[0]
    dg = gv.shape[1]
    n_heads = dg // GMLP_HEAD
    tr = _tile(t_dim, 256, CHUNK)

    def body(zu_ref, zv_ref, dy_ref, gv_ref, wm_ref, b_ref, ggo_ref,
             dz_ref, dggo_ref, dgv_ref, dwm_ref, dsum_ref):
        @pl.when(pl.program_id(0) == 0)
        def _():
            dggo_ref[...] = jnp.zeros_like(dggo_ref)
            dgv_ref[...] = jnp.zeros_like(dgv_ref)
            dwm_ref[...] = jnp.zeros_like(dwm_ref)
            dsum_ref[...] = jnp.zeros_like(dsum_ref)

        for ck in range(tr // CHUNK):
            rows = pl.ds(ck * CHUNK, CHUNK)
            zu = zu_ref[rows, :]
            zv = zv_ref[rows, :]
            gvv = gv_ref[...]
            ua, vh, r, vb, s = _gmlp_chunk(zu, zv, gvv, wm_ref, b_ref[...], n_heads)
            dy, dggo = _rms_backward(ua * s, ggo_ref[...], dy_ref[rows, :].astype(F32))
            dggo_ref[...] += dggo
            ds = dy * ua
            dsum_ref[...] += ds
            dsb = ds.astype(BF16)
            parts = []
            for h in range(n_heads):
                cols = slice(h * GMLP_HEAD, (h + 1) * GMLP_HEAD)
                dwm_ref[h] += _dot(dsb[:, cols], vb[:, cols], "nt")
                parts.append(_dot(wm_ref[h], dsb[:, cols], "tn"))
            dv = jnp.concatenate(parts, axis=1)
            dgv_ref[...] += jnp.sum(dv * vh, axis=0, keepdims=True)
            dvh = dv * gvv
            dvg = r * (dvh - jnp.mean(dvh, axis=-1, keepdims=True) - vh * jnp.mean(dvh * vh, axis=-1, keepdims=True))
            dz_ref[rows, pl.ds(2 * dg, dg)] = (dvg * _gelu_grad(zv)).astype(BF16)
            dz_ref[rows, pl.ds(dg, dg)] = (dy * s * _gelu_grad(zu)).astype(BF16)

    full = lambda a: pl.BlockSpec(a.shape, lambda i, nd=a.ndim: (0,) * nd)
    return pl.pallas_call(
        body,
        name="gmlp_bwd",
        grid=(t_dim // tr,),
        in_specs=[pl.BlockSpec((tr, dg), lambda i: (i, 1)), pl.BlockSpec((tr, dg), lambda i: (i, 2)),
                  pl.BlockSpec((tr, dg), lambda i: (i, 1)), full(gv), full(wm), full(bias), full(ggo)],
        out_specs=[pl.BlockSpec((tr, 3 * dg), lambda i: (i, 0)),
                   pl.BlockSpec((1, dg), lambda i: (0, 0)), pl.BlockSpec((1, dg), lambda i: (0, 0)),
                   pl.BlockSpec(wm.shape, lambda i: (0, 0, 0)), pl.BlockSpec((CHUNK, dg), lambda i: (0, 0))],
        out_shape=[jax.ShapeDtypeStruct((t_dim, 3 * dg), BF16),
                   jax.ShapeDtypeStruct((1, dg), F32), jax.ShapeDtypeStruct((1, dg), F32),
                   jax.ShapeDtypeStruct(wm.shape, F32), jax.ShapeDtypeStruct((CHUNK, dg), F32)],
        compiler_params=_params(("arbitrary",)),
    )(z, z, dycat, gv, wm, bias, ggo)


def _ple_head(g_ple, w_gate, h3, p, w_proj, tgt, g_final):
    t_dim, d = h3.shape
    tr = _tile(t_dim, 256, 16)

    def body(gp_ref, w_ref, h_ref, p_ref, wp_ref, t_ref, g_ref, n_ref, dgq_ref, dpp_ref, dh_ref, dg_ref, loss_ref):
        @pl.when(pl.program_id(0) == 0)
        def _():
            dg_ref[...] = jnp.zeros_like(dg_ref)
            loss_ref[...] = jnp.zeros_like(loss_ref)

        h3v = h_ref[...]
        npl = (_rms_stats(h3v)[0] * gp_ref[...]).astype(BF16)
        n_ref[...] = npl
        gate = _sigmoid(_dot(npl, w_ref[...], "nn"))
        ppv = _dot(p_ref[...], wp_ref[...], "nn")
        h4 = h3v + gate * ppv
        xh, _ = _rms_stats(h4)
        err = xh * g_ref[...] - t_ref[...]
        dh4, dg = _rms_backward(h4, g_ref[...], err * (1.0 / d))
        dh_ref[...] = dh4
        dgq_ref[...] = (dh4 * ppv * gate * (1.0 - gate)).astype(BF16)
        dpp_ref[...] = (dh4 * gate).astype(BF16)
        dg_ref[...] += dg
        loss_ref[...] += jnp.full((1, LANES), 0.5 * jnp.sum(err * err) * (1.0 / d), F32)

    rows = pl.BlockSpec((tr, d), lambda i: (i, 0))
    whole = lambda a: pl.BlockSpec(a.shape, lambda i: (0, 0))
    return pl.pallas_call(
        body,
        name="ple_head",
        grid=(t_dim // tr,),
        in_specs=[whole(g_ple), whole(w_gate), rows, pl.BlockSpec((tr, p.shape[1]), lambda i: (i, 0)), whole(w_proj),
                  rows, whole(g_final)],
        out_specs=[rows, rows, rows, rows, pl.BlockSpec((1, d), lambda i: (0, 0)),
                   pl.BlockSpec((1, LANES), lambda i: (0, 0))],
        out_shape=[jax.ShapeDtypeStruct((t_dim, d), BF16), jax.ShapeDtypeStruct((t_dim, d), BF16),
                   jax.ShapeDtypeStruct((t_dim, d), BF16),
                   jax.ShapeDtypeStruct((t_dim, d), F32), jax.ShapeDtypeStruct((1, d), F32),
                   jax.ShapeDtypeStruct((1, LANES), F32)],
        compiler_params=_params(("arbitrary",)),
    )(g_ple, w_gate, h3, p, w_proj, tgt, g_final)


def _position():
    x, y, c = lax.axis_index("x"), lax.axis_index("y"), lax.axis_index("c")
    chips = [(1 - x, y), (x, 1 - y), (1 - x, 1 - y)]
    return x, y, c, chips


def _region(ref, kind, shard_shape, q, half, part=None):
    rs, cs = shard_shape
    r0, nr = (0, rs) if half is None else (half * (rs // 2), rs // 2)
    if part is not None:
        r0, nr = r0 + part * (rs // 4), rs // 4
    if kind == "row":
        return ref.at[pl.ds(q * rs + r0, nr), :]
    return ref.at[pl.ds(r0, nr), pl.ds(q * cs, cs)]


def _full_shape(kind, shard_shape):
    rs, cs = shard_shape
    return (N_CHIPS * rs, cs) if kind == "row" else (rs, N_CHIPS * cs)


def _remote(src, dst, send_sems, recv_sems, k, to):
    return pltpu.make_async_remote_copy(src_ref=src, dst_ref=dst, send_sem=send_sems.at[k], recv_sem=recv_sems.at[k],
                                        device_id=to, device_id_type=MESH)


def _same(arrays):
    return [jax.ShapeDtypeStruct(a.shape, a.dtype) for a in arrays]


def _gather_near_carry(gathered, kinds, shapes):
    nw = len(gathered)

    def copies(ops, full, send_sems, recv_sems):
        x, y, c, _ = _position()
        out = []
        for w in range(nw):
            mine = _region(full[w], kinds[w], shapes[w], 2 * x + y, c)
            out.append(_remote(mine, mine, send_sems, recv_sems, 2 * w, (1 - x, y, c)))
            out.append(_remote(mine, mine, send_sems, recv_sems, 2 * w + 1, (x, 1 - y, c)))
        return out

    return _Carry(gathered, _same(gathered), {i: i for i in range(nw)}, 2 * nw, copies)


def _gather_far_carry(gathered, kinds, shapes):
    nw = len(gathered)

    def copies(ops, full, send_sems, recv_sems):
        x, y, c, _ = _position()
        out = []
        for w in range(nw):
            from_x = _region(full[w], kinds[w], shapes[w], 2 * (1 - x) + y, c, part=1)
            from_y = _region(full[w], kinds[w], shapes[w], 2 * x + (1 - y), c, part=0)
            out.append(_remote(from_x, from_x, send_sems, recv_sems, 2 * w, (x, 1 - y, c)))
            out.append(_remote(from_y, from_y, send_sems, recv_sems, 2 * w + 1, (1 - x, y, c)))
        return out

    return _Carry(gathered, _same(gathered), {i: i for i in range(nw)}, 2 * nw, copies)


def _gather_d2d_carry(gathered, kinds, shapes):
    nw = len(gathered)

    def copies(ops, full, send_sems, recv_sems):
        x, y, c, chips = _position()
        out = []
        for w in range(nw):
            for j, (cx, cy) in enumerate(chips):
                landed = _region(full[w], kinds[w], shapes[w], 2 * cx + cy, c)
                out.append(_remote(landed, landed, send_sems, recv_sems, 3 * w + j, (x, y, 1 - c)))
        return out

    return _Carry(gathered, _same(gathered), {i: i for i in range(nw)}, 3 * nw, copies)


def _pairs_carry(grads, kinds, shapes):
    nw = len(grads)

    def copies(g, got, send_sems, recv_sems):
        x, y, c, _ = _position()
        out = []
        for w in range(nw):
            for q in range(N_CHIPS):
                out.append(_remote(_region(g[w], kinds[w], shapes[w], q, 1 - c), got[w].at[q], send_sems, recv_sems,
                                   N_CHIPS * w + q, (x, y, 1 - c)))
        return out

    outs = [jax.ShapeDtypeStruct((N_CHIPS, s[0] // 2, s[1]), BF16) for s in shapes]
    return _Carry(grads, outs, {}, N_CHIPS * nw, copies)


def _pair_sum(name, grad, got, kind, shard_shape, c_arr):
    rs, cs = shard_shape
    hr = rs // 2
    tr = _tile(hr, 512, 16)
    nr = hr // tr

    def body(c_ref, g_ref, s_ref, o_ref):
        o_ref[...] = (g_ref[...].astype(F32) + s_ref[...].astype(F32)).astype(BF16)

    if kind == "row":
        g_spec = pl.BlockSpec((tr, cs), lambda q, i, c_ref: (q * (rs // tr) + c_ref[0] * nr + i, 0))
    else:
        g_spec = pl.BlockSpec((tr, cs), lambda q, i, c_ref: (c_ref[0] * nr + i, q))
    blk = pl.BlockSpec((None, tr, cs), lambda q, i, c_ref: (q, i, 0))
    return pl.pallas_call(
        body,
        name=name,
        grid_spec=pltpu.PrefetchScalarGridSpec(num_scalar_prefetch=1, grid=(N_CHIPS, nr), in_specs=[g_spec, blk],
                                               out_specs=blk),
        out_shape=jax.ShapeDtypeStruct((N_CHIPS, hr, cs), BF16),
        compiler_params=_params(("parallel", "parallel")),
    )(c_arr, grad, got)


def _scatter_carry(sums, shapes):
    nw = len(sums)

    def copies(ps, got, send_sems, recv_sems):
        x, y, c, chips = _position()
        out = []
        for w in range(nw):
            for j, (cx, cy) in enumerate(chips):
                out.append(_remote(ps[w].at[2 * cx + cy], got[w].at[j], send_sems, recv_sems, 3 * w + j, (cx, cy, c)))
        return out

    outs = [jax.ShapeDtypeStruct((3, s[0] // 2, s[1]), BF16) for s in shapes]
    return _Carry(sums, outs, {}, 3 * nw, copies)


def _owner_sum(name, sums, got, shard_shape, qc_arr):
    rs, cs = shard_shape
    hr = rs // 2
    tr = _tile(hr, 512, 16)
    nr = hr // tr

    def body(qc_ref, mine_ref, got_ref, o_ref):
        acc = mine_ref[...].astype(F32)
        for j in range(3):
            acc = acc + got_ref[j].astype(F32)
        o_ref[...] = acc

    return pl.pallas_call(
        body,
        name=name,
        grid_spec=pltpu.PrefetchScalarGridSpec(
            num_scalar_prefetch=1, grid=(nr,),
            in_specs=[pl.BlockSpec((None, tr, cs), lambda i, qc_ref: (qc_ref[0], i, 0)),
                      pl.BlockSpec((3, tr, cs), lambda i, qc_ref: (0, i, 0))],
            out_specs=pl.BlockSpec((tr, cs), lambda i, qc_ref: (qc_ref[1] * nr + i, 0))),
        out_shape=jax.ShapeDtypeStruct((rs, cs), F32),
        compiler_params=_params(("parallel",)),
    )(qc_arr, sums, got)


def _share_carry(grads, shapes):
    nw = len(grads)

    def copies(ops, out, send_sems, recv_sems):
        x, y, c, _ = _position()
        res = []
        for w in range(nw):
            hr = shapes[w][0] // 2
            mine = out[w].at[pl.ds(c * hr, hr), :]
            res.append(_remote(mine, mine, send_sems, recv_sems, w, (x, y, 1 - c)))
        return res

    return _Carry(grads, _same(grads), {i: i for i in range(nw)}, nw, copies)


def _place_block(packed, me):
    return lax.dynamic_update_slice(jnp.zeros((N_DEV,) + packed.shape, F32), packed[None], (me, 0, 0))


def _exchange_carry(blocks):
    def copies(ops, res, send_sems, recv_sems):
        x, y, c, _ = _position()
        mine = res[0].at[4 * x + 2 * y + c]
        out = []
        for k in range(1, N_DEV):
            to = ((1 - x) if k & 4 else x, (1 - y) if k & 2 else y, (1 - c) if k & 1 else c)
            out.append(_remote(mine, mine, send_sems, recv_sems, k - 1, to))
        return out

    return _Carry([blocks], _same([blocks]), {0: 0}, N_DEV - 1, copies)


def _sum_blocks(name, blocks):
    n, rows, lanes = blocks.shape
    tr = _tile(rows, 4096, 8)

    def body(b_ref, o_ref):
        acc = b_ref[0]
        for k in range(1, n):
            acc = acc + b_ref[k]
        o_ref[...] = acc

    return pl.pallas_call(
        body,
        name=name,
        grid=(rows // tr,),
        in_specs=[pl.BlockSpec((n, tr, lanes), lambda i: (0, i, 0))],
        out_specs=pl.BlockSpec((tr, lanes), lambda i: (i, 0)),
        out_shape=jax.ShapeDtypeStruct((rows, lanes), F32),
        compiler_params=_params(("parallel",)),
    )(blocks)


def _adamw(name, w, g, m, v):
    def fn(rows, pars):
        wv, gv, mv, vv = rows
        m_new = ADAM_B1 * mv + (1.0 - ADAM_B1) * gv
        v_new = ADAM_B2 * vv + (1.0 - ADAM_B2) * (gv * gv)
        m_hat = m_new / (1.0 - ADAM_B1 ** ADAM_STEP)
        v_hat = v_new / (1.0 - ADAM_B2 ** ADAM_STEP)
        delta = -ADAM_LR * (m_hat / (jnp.sqrt(v_hat) + ADAM_EPS) + ADAM_WD * wv)
        return [delta, m_new, v_new, gv], []

    c = w.shape[1]
    return _rowwise(name, fn, [w, g, m, v], [], [(c, F32)] * 4, tr=256)


def _pack(arrays):
    rows = []
    for a in arrays:
        flat = a.reshape(-1).astype(F32)
        pad = (-flat.shape[0]) % LANES
        rows.append(jnp.pad(flat, (0, pad)).reshape(-1, LANES))
    stacked = jnp.concatenate(rows, axis=0)
    pad_rows = (-stacked.shape[0]) % 8
    return jnp.pad(stacked, ((0, pad_rows), (0, 0)))


def _unpack(packed, shapes):
    out, r = [], 0
    for s in shapes:
        n = math.prod(s)
        nr = -(-n // LANES)
        out.append(packed[r:r + nr].reshape(-1)[:n].reshape(s))
        r += nr
    return out


BIG = ["w1_gate", "w1_up", "w1_down", "w_in", "ssm_w_glu", "w_out", "w2_gate", "w2_up", "w2_down", "w_ple_gate",
       "w_ple_proj"]
KIND = {"w1_gate": "col", "w1_up": "col", "w1_down": "row", "w_in": "col", "ssm_w_glu": "row", "w_out": "row",
        "w2_gate": "col", "w2_up": "col", "w2_down": "row", "w_ple_gate": "row", "w_ple_proj": "col"}
SMALL = ["norm_ffn1", "norm_mix", "ssm_log_dt", "ssm_a_re", "ssm_a_im", "ssm_b_re", "ssm_b_im", "ssm_c_re", "ssm_c_im",
         "ssm_d", "gmlp_norm_v", "gmlp_w_s", "gmlp_b_s", "norm_ssm_out", "norm_gmlp_out", "norm_ffn2", "norm_ple",
         "norm_final"]
WEIGHTS = ["norm_ffn1", "w1_gate", "w1_up", "w1_down", "norm_mix", "w_in", "ssm_log_dt", "ssm_a_re", "ssm_a_im",
           "ssm_b_re", "ssm_b_im", "ssm_c_re", "ssm_c_im", "ssm_d", "ssm_w_glu", "gmlp_norm_v", "gmlp_w_s", "gmlp_b_s",
           "norm_ssm_out", "norm_gmlp_out", "w_out", "norm_ffn2", "w2_gate", "w2_up", "w2_down", "norm_ple",
           "w_ple_gate", "w_ple_proj", "norm_final"]


class _Trip:
    def __init__(self, names, arrays, carry):
        self.names, self.arrays, self.carry = names, arrays, carry


class _Reducer:
    def __init__(self, shard_shape, c_arr, qc_arr):
        self.shard_shape, self.c_arr, self.qc_arr = shard_shape, c_arr, qc_arr
        self.halves = {}

    def swap(self, names, grads):
        kinds = [KIND[n] for n in names]
        shapes = [self.shard_shape[n] for n in names]
        return _Trip(names, grads, _pairs_carry(grads, kinds, shapes))

    def send(self, trip, swapped):
        shapes = [self.shard_shape[n] for n in trip.names]
        sums = [_pair_sum("pair_sum_" + n, g, s, KIND[n], sh, self.c_arr)
                for n, g, s, sh in zip(trip.names, trip.arrays, swapped, shapes)]
        return _Trip(trip.names, sums, _scatter_carry(sums, shapes))

    def end(self, trip, got):
        for n, ps, g in zip(trip.names, trip.arrays, got):
            self.halves[n] = _owner_sum("owner_sum_" + n, ps, g, self.shard_shape[n], self.qc_arr)


def _ride(*carries):
    present = [c for c in carries if c is not None]
    joined = functools.reduce(_join, present) if present else None

    def split(results):
        out, at = [], 0
        for c in carries:
            n = len(c.out_shapes) if c is not None else 0
            out.append(list(results[at:at + n]))
            at += n
        return out

    return joined, split


def _step(x, p, tgt, w, m, v):
    d_model = x.shape[1]
    d_ssm = w["ssm_d"].shape[1]
    n_groups = d_ssm // SSM_GROUP
    row = lambda a: a.reshape(1, -1)

    xi, yi, ci = lax.axis_index("x"), lax.axis_index("y"), lax.axis_index("c")
    c_arr = jnp.reshape(ci, (1,)).astype(jnp.int32)
    q_arr = jnp.reshape(2 * xi + yi, (1,)).astype(jnp.int32)
    qc_arr = jnp.stack([2 * xi + yi, ci]).astype(jnp.int32)
    shard_shape = {n: w[n].shape for n in BIG}
    full = {n: _cast_into_gathered("cast_" + n, w[n], KIND[n], q_arr) for n in BIG}

    def gather(stage, names):
        return stage([full[n] for n in names], [KIND[n] for n in names], [shard_shape[n] for n in names])

    def gathered(names, arrays):
        full.update(zip(names, arrays))

    groups = [["w1_gate"], ["w1_up"], ["w1_down"], ["w_in"], ["w2_gate"], ["ssm_w_glu", "w_out"], ["w2_up"],
              ["w2_down", "w_ple_gate", "w_ple_proj"]]
    near, far, d2d = _gather_near_carry, _gather_far_carry, _gather_d2d_carry

    def stages(*work):
        carries = [gather(stage, groups[g]) for stage, g in work]
        return functools.reduce(_join, carries), [n for _, g in work for n in groups[g]]

    def alone(name, *work):
        carry, names = stages(*work)
        gathered(names, _comm_call(name, carry))

    alone("gather_a", (near, 0))
    alone("gather_b", (far, 0), (near, 1))
    carry, names = stages((d2d, 0), (far, 1))
    n1, landed = _rms_fwd("ffn1_norm", x, w["norm_ffn1"], carry)
    gathered(names, landed)
    carry, names = stages((d2d, 1), (near, 2), (near, 3))
    gate1, landed = _mm_nn("ffn1_gate", n1, full["w1_gate"], BF16, tm=1024, tn=512, tk=2048, carry=carry)
    gathered(names, landed)
    carry, names = stages((far, 2), (far, 3), (near, 4))
    (a1, da_dgate1, da_dup1), landed = _ffn_up("ffn1", n1, full["w1_up"], gate1, carry)
    gathered(names, landed)
    alone("gather_d", (d2d, 2))
    carry, names = stages((d2d, 3), (far, 4), (near, 5))
    h1, landed = _ffn_down("ffn1", a1, full["w1_down"], x, carry)
    gathered(names, landed)
    ffn1 = (n1, a1, da_dgate1, da_dup1)
    nm = _rms_fwd("mix_norm", h1, w["norm_mix"])
    carry, names = stages((d2d, 4), (far, 5), (near, 6))
    z, landed = _mm_nn("in_proj", nm, full["w_in"], F32, tm=1024, tn=512, tk=2048, carry=carry)
    gathered(names, landed)

    disc, disc_vjp = jax.vjp(_ssm_discretize, w["ssm_log_dt"][0], w["ssm_a_re"], w["ssm_a_im"], w["ssm_b_re"],
                             w["ssm_b_im"])
    abar_r, abar_i, bbar_r, bbar_i = disc
    nb = n_groups // GROUPS_PER_BLOCK
    wb = jnp.concatenate([_blockdiag_in(bbar_r), _blockdiag_in(bbar_i)], axis=-1).astype(BF16)
    wc = jnp.concatenate([_blockdiag_out(w["ssm_c_re"]), -_blockdiag_out(w["ssm_c_im"])], axis=1).astype(BF16)
    abar = jnp.concatenate([abar_r.reshape(nb, -1), abar_i.reshape(nb, -1)], axis=-1)
    abar_conj = jnp.concatenate([abar_r.reshape(nb, -1), -abar_i.reshape(nb, -1)], axis=-1)
    carry, names = stages((d2d, 5), (far, 6), (near, 7))
    (states, y_pre, yg), landed = _ssm_scan_fwd(z, wb, wc, abar, w["ssm_d"], carry)
    gathered(names, landed)
    q = _mm_nn("glu_proj", yg, full["ssm_w_glu"], F32, tm=1024, tn=1024, tk=1024)

    def glu_norm(rows, pars):
        yv = _gelu(rows[0]) * _sigmoid(rows[1])
        yh, _ = _rms_stats(yv)
        return [yh * pars[0]], []

    yn_ssm = _rowwise("ssm_glu_norm", glu_norm, [y_pre, q], [w["norm_ssm_out"]], [(d_ssm, BF16)])[0]

    tril = jnp.tril(jnp.ones((CHUNK, CHUNK), dtype=bool))
    wm = jnp.where(tril[None], w["gmlp_w_s"], 0.0).astype(BF16)
    bias = jnp.repeat(w["gmlp_b_s"].T, GMLP_HEAD, axis=1)
    yn_gmlp = _gmlp_fwd(z, w["gmlp_norm_v"], wm, bias, w["norm_gmlp_out"])
    ycat = jnp.concatenate([yn_ssm, yn_gmlp], axis=1)
    carry, names = stages((d2d, 6), (far, 7))
    h2, landed = _mm_nn("out_proj", ycat, full["w_out"], F32, res=h1, alpha=1.0, tm=512, tn=1024, tk=2048,
                        carry=carry)
    gathered(names, landed)

    n2 = _rms_fwd("ffn2_norm", h2, w["norm_ffn2"])
    carry, names = stages((d2d, 7))
    (a2, da_dgate2, da_dup2), landed = _ffn_gateup("ffn2", n2, full["w2_gate"], full["w2_up"], carry)
    gathered(names, landed)
    h3 = _ffn_down("ffn2", a2, full["w2_down"], h2)
    ffn2 = (n2, a2, da_dgate2, da_dup2)
    npl, dgq, dpp, dh4, g_norm_final, loss_part = _ple_head(w["norm_ple"], full["w_ple_gate"], h3, p,
                                                            full["w_ple_proj"], tgt, row(w["norm_final"]))
    reducer = _Reducer(shard_shape, c_arr, qc_arr)
    (g_w_ple_proj,) = _mm_tn("ple_dwproj", p, [dpp], BF16, tm=256, tn=1024, tk=4096)
    (g_w_ple_gate,) = _mm_tn("ple_dwgate", npl, [dgq], BF16, tm=512, tn=1024, tk=4096)
    ple = reducer.swap(["w_ple_gate", "w_ple_proj"], [g_w_ple_gate, g_w_ple_proj])
    dnpl, swapped = _mm_nt_sum("ple_dnorm_in", [dgq], [full["w_ple_gate"]], BF16, tm=512, tn=1024, tk=2048,
                               carry=ple.carry)
    ple = reducer.send(ple, swapped)
    dh3, dh3_b, g_norm_ple = _rms_bwd("ple_dnorm", h3, w["norm_ple"], [dnpl], dh4, 0.5)

    dh2, dh2_b, g_norm_ffn2, got, _, up2_w = _ffn_bwd(
        "ffn2", ["w2_gate", "w2_up", "w2_down"], h2, w["norm_ffn2"], full["w2_gate"], full["w2_up"], full["w2_down"],
        ffn2, dh3, dh3_b, 1.0, reducer, riding=ple.carry, last_hop_later=True)
    reducer.end(ple, got)

    dycat = _mm_nt_sum("out_dproj", [dh2_b], [full["w_out"]], BF16, tm=512, tn=1024, tk=2048)
    (g_w_out,) = _mm_tn("out_dw", ycat, [dh2_b], BF16, tm=512, tn=1024, tk=4096)

    dz, g_norm_gmlp_out, g_gmlp_norm_v, g_wm, g_s = _gmlp_bwd(z, dycat, w["gmlp_norm_v"], wm, bias,
                                                                  w["norm_gmlp_out"])
    g_gmlp_w_s = jnp.where(tril[None], g_wm, 0.0)
    g_gmlp_b_s = g_s.reshape(CHUNK, -1, GMLP_HEAD).sum(axis=-1).T

    def glu_bwd(rows, pars):
        dyn, ypre, qv = rows
        ygv = _gelu(ypre)
        sg = _sigmoid(qv)
        dy, dg = _rms_backward(ygv * sg, pars[0], dyn.astype(F32))
        return [dy * ygv * sg * (1.0 - sg), dy * sg], [dg]

    dq, dyg_part, g_norm_ssm_out = _rowwise("ssm_dglu", glu_bwd, [(dycat, d_ssm, 0), y_pre, q], [w["norm_ssm_out"]],
                                            [(d_ssm, BF16), (d_ssm, F32)], [(1, d_ssm)])
    (g_ssm_w_glu,) = _mm_tn("glu_dw", yg, [dq], BF16, tm=512, tn=1024, tk=4096)

    def gelu_bwd(accs, extras):
        return [(accs[0] + extras[0]) * _gelu_grad(extras[1])]

    (dy_pre,) = _matmul("glu_dproj", "nt", [dq], [full["ssm_w_glu"]], [(0, 0, 0)], [F32], gelu_bwd,
                        extras=(dyg_part, y_pre), tm=1024, tn=1024, tk=1024)
    mixers = reducer.swap(["w_out", "ssm_w_glu"], [g_w_out, g_ssm_w_glu])
    me = 4 * xi + 2 * yi + ci
    small = {"gmlp_norm_v": g_gmlp_norm_v, "gmlp_w_s": g_gmlp_w_s, "gmlp_b_s": g_gmlp_b_s,
             "norm_gmlp_out": g_norm_gmlp_out, "norm_ssm_out": g_norm_ssm_out, "norm_ffn2": g_norm_ffn2,
             "norm_ple": g_norm_ple, "norm_final": g_norm_final}
    before_scan = [n for n in SMALL if n in small]
    scan_blocks = _place_block(_pack([small[n] for n in before_scan] + [loss_part[:, :1]]), me)
    carry, split = _ride(up2_w.carry, mixers.carry, _exchange_carry(scan_blocks))
    (dz, g_wb, g_wc, g_abar, g_ssm_d), results = _ssm_scan_bwd(z, dy_pre, states, wb, wc, abar_conj, w["ssm_d"],
                                                              dz, carry)
    got, swapped, (scan_blocks,) = split(results)
    reducer.end(up2_w, got)
    mixers = reducer.send(mixers, swapped)
    g_abar = jnp.transpose(g_abar, (1, 0, 2)).reshape(nb, -1)
    sw = g_abar.shape[-1] // 2
    g_bbar_r = _blockdiag_in_grad(g_wb[..., :sw], SSM_STATE, SSM_GROUP)
    g_bbar_i = _blockdiag_in_grad(g_wb[..., sw:], SSM_STATE, SSM_GROUP)
    g_ssm_c_re = _blockdiag_out_grad(g_wc[:, :sw, :], SSM_GROUP, SSM_STATE)
    g_ssm_c_im = -_blockdiag_out_grad(g_wc[:, sw:, :], SSM_GROUP, SSM_STATE)
    g_abar_r = g_abar[..., :sw].reshape(n_groups, SSM_STATE)
    g_abar_i = g_abar[..., sw:].reshape(n_groups, SSM_STATE)
    g_ssm_log_dt, g_ssm_a_re, g_ssm_a_im, g_ssm_b_re, g_ssm_b_im = disc_vjp((g_abar_r, g_abar_i, g_bbar_r, g_bbar_i))

    (g_w_in,), got = _mm_tn("in_dw", nm, [dz], BF16, tm=512, tn=1536, tk=4096, carry=mixers.carry)
    reducer.end(mixers, got)
    in_w = reducer.swap(["w_in"], [g_w_in])
    dnm, swapped = _mm_nt_sum("in_dproj", [dz], [full["w_in"]], BF16, tm=1024, tn=1024, tk=3072, carry=in_w.carry)
    in_w = reducer.send(in_w, swapped)
    dh1, dh1_b, g_norm_mix = _rms_bwd("mix_dnorm", h1, w["norm_mix"], [dnm], dh2, 0.5)
    small = {"norm_mix": g_norm_mix, "ssm_log_dt": g_ssm_log_dt, "ssm_a_re": g_ssm_a_re,
             "ssm_a_im": g_ssm_a_im, "ssm_b_re": g_ssm_b_re, "ssm_b_im": g_ssm_b_im, "ssm_c_re": g_ssm_c_re,
             "ssm_c_im": g_ssm_c_im, "ssm_d": g_ssm_d}
    after_scan = [n for n in SMALL if n in small]
    ffn_blocks = _place_block(_pack([small[n] for n in after_scan]), me)
    done = [n for n in BIG if n in reducer.halves]
    carry, split = _ride(in_w.carry, _share_carry([reducer.halves[n] for n in done], [shard_shape[n] for n in done]))
    dx, _, g_norm_ffn1, results, (ffn_blocks,), _ = _ffn_bwd(
        "ffn1", ["w1_gate", "w1_up", "w1_down"], x, w["norm_ffn1"], full["w1_gate"], full["w1_up"], full["w1_down"],
        ffn1, dh1, dh1_b, 1.0, reducer, riding=carry, riding_dwd=_exchange_carry(ffn_blocks))
    got, shared = split(results)
    grad = dict(zip(done, shared))
    reducer.end(in_w, got)
    scan_grads = _unpack(_sum_blocks("sum_before_scan", scan_blocks), [w[n].shape for n in before_scan] + [(1,)])
    loss = scan_grads[-1].reshape(())
    early = before_scan + after_scan
    early_grads = scan_grads[:-1] + _unpack(_sum_blocks("sum_after_scan", ffn_blocks), [w[n].shape for n in after_scan])

    rest = [n for n in BIG if n not in done]
    late_blocks = _place_block(_pack([g_norm_ffn1]), me)
    last = _join(_share_carry([reducer.halves[n] for n in rest], [shard_shape[n] for n in rest]),
                 _exchange_carry(late_blocks))
    *shared, late_blocks = _comm_call("share_halves", last)
    grad.update(zip(rest, shared))
    grad.update(zip(early, early_grads))
    grad["norm_ffn1"] = _unpack(_sum_blocks("sum_first_norm", late_blocks), [w["norm_ffn1"].shape])[0]

    small_shapes = [w[n].shape for n in SMALL]
    delta, new_m, new_v = {}, {}, {}
    for n in BIG:
        delta[n], new_m[n], new_v[n], grad[n] = _adamw("adamw_" + n, w[n], grad[n], m[n], v[n])
    d_p, m_p, v_p, _ = _adamw("adamw_small", _pack([w[n] for n in SMALL]), _pack([grad[n] for n in SMALL]),
                              _pack([m[n] for n in SMALL]), _pack([v[n] for n in SMALL]))
    for name_list, packed in ((delta, d_p), (new_m, m_p), (new_v, v_p)):
        for n, a in zip(SMALL, _unpack(packed, small_shapes)):
            name_list[n] = a
    return loss, dx, grad, delta, new_m, new_v


def kernel(x, p, norm_ffn1, w1_gate, w1_up, w1_down, norm_mix, w_in, ssm_log_dt, ssm_a_re, ssm_a_im, ssm_b_re, ssm_b_im, ssm_c_re, ssm_c_im, ssm_d, ssm_w_glu, gmlp_norm_v, gmlp_w_s, gmlp_b_s, norm_ssm_out, norm_gmlp_out, w_out, norm_ffn2, w2_gate, w2_up, w2_down, norm_ple, w_ple_gate, w_ple_proj, norm_final, loss_target, m_norm_ffn1, m_w1_gate, m_w1_up, m_w1_down, m_norm_mix, m_w_in, m_ssm_log_dt, m_ssm_a_re, m_ssm_a_im, m_ssm_b_re, m_ssm_b_im, m_ssm_c_re, m_ssm_c_im, m_ssm_d, m_ssm_w_glu, m_gmlp_norm_v, m_gmlp_w_s, m_gmlp_b_s, m_norm_ssm_out, m_norm_gmlp_out, m_w_out, m_norm_ffn2, m_w2_gate, m_w2_up, m_w2_down, m_norm_ple, m_w_ple_gate, m_w_ple_proj, m_norm_final, v_norm_ffn1, v_w1_gate, v_w1_up, v_w1_down, v_norm_mix, v_w_in, v_ssm_log_dt, v_ssm_a_re, v_ssm_a_im, v_ssm_b_re, v_ssm_b_im, v_ssm_c_re, v_ssm_c_im, v_ssm_d, v_ssm_w_glu, v_gmlp_norm_v, v_gmlp_w_s, v_gmlp_b_s, v_norm_ssm_out, v_norm_gmlp_out, v_w_out, v_norm_ffn2, v_w2_gate, v_w2_up, v_w2_down, v_norm_ple, v_w_ple_gate, v_w_ple_proj, v_norm_final):
    given = dict(locals())
    shapes = {n: given[n].shape for n in WEIGHTS}

    def block(name):
        a = given[name]
        if a.ndim == 1:
            return a.reshape(1, -1)
        return a[0] if a.ndim >= 3 else a

    w = {n: block(n) for n in WEIGHTS}
    m = {n: block("m_" + n) for n in WEIGHTS}
    v = {n: block("v_" + n) for n in WEIGHTS}
    loss, dx, grad, delta, new_m, new_v = _step(x[0], p[0, 0], loss_target[0], w, m, v)
    outs = [loss, dx[None]]
    for tree in (grad, delta, new_m, new_v):
        outs += [tree[n].reshape(shapes[n]) for n in WEIGHTS]
    return tuple(outs)
```
